```python
import math
import jax, jax.numpy as jnp
from jax import lax
import numpy as np

D_MODEL = 1024
BATCH = 2
SEQ = 8192
DEPTH = 1

A_HEADS = 8
A_HEAD_DIM = 64
A_WIDTH = A_HEADS * A_HEAD_DIM
MOBA_BLOCK = 256
MOBA_TOPK = 3
Q_BLOCK = 128
R_HEADS = 8
R_QK_DIM = 64
R_V_DIM = 128
R_QK_WIDTH = R_HEADS * R_QK_DIM
R_V_WIDTH = R_HEADS * R_V_DIM
R_CHUNK = 128
N_EXPERTS = 256
TOP_K = 8
N_GROUPS = 8
TOPK_GROUPS = 4
EXPERT_FF = 256
SHARED_FF = 256
ROUTED_SCALE = 2.5
EXPERT_ROWS = 128
NORM_EPS = 1e-6
GN_EPS = 1e-6
NEG_INF = -1e30
IN_WIDTHS = (A_WIDTH, A_WIDTH, A_WIDTH, R_QK_WIDTH, R_QK_WIDTH, R_V_WIDTH, R_V_WIDTH, D_MODEL, D_MODEL)
IN_COLS = 3 * A_WIDTH + 2 * R_QK_WIDTH + 2 * R_V_WIDTH + 2 * D_MODEL

kernel_name = 'hybrid_moba_retention_moe_block'


def rmsnorm(x, g):
    xf = x.astype(jnp.float32)
    y = xf * lax.rsqrt(jnp.mean(xf * xf, axis=-1, keepdims=True) + NORM_EPS)
    return (y * g.astype(jnp.float32)).astype(x.dtype)


def alibi_slopes(n):
    return jnp.exp2(-8.0 / n * jnp.arange(1, n + 1, dtype=jnp.float32))


def moba_attention(q, k, v):
    B, S, H, dh = q.shape
    s_pad = -(-S // MOBA_BLOCK) * MOBA_BLOCK
    n_blk = s_pad // MOBA_BLOCK
    n_q = s_pad // Q_BLOCK
    k_sel = min(MOBA_TOPK, n_blk)

    def prep(a):
        a = jnp.transpose(a, (0, 2, 1, 3))
        return jnp.pad(a, ((0, 0), (0, 0), (0, s_pad - S), (0, 0)))

    q, k, v = prep(q), prep(k), prep(v)
    kb = k.reshape(B, H, n_blk, MOBA_BLOCK, dh)
    vb = v.reshape(B, H, n_blk, MOBA_BLOCK, dh)
    k_mean = jnp.mean(kb.astype(jnp.float32), axis=3)
    t_blk = jnp.arange(s_pad) // MOBA_BLOCK
    past = jnp.arange(n_blk)[None, :] < t_blk[:, None]
    gate = jnp.einsum('bhsd,bhnd->bhsn', q.astype(jnp.float32), k_mean)
    gate = jnp.where(past, gate, NEG_INF)
    _, sel = lax.top_k(gate, k_sel)
    valid = jnp.arange(k_sel)[None, :] < t_blk[:, None]
    scale = dh ** -0.5
    slopes = alibi_slopes(H)
    bi = jnp.arange(B)[:, None, None, None]
    hi = jnp.arange(H)[None, :, None, None]
    offs = jnp.arange(MOBA_BLOCK)

    def to_blocks(a):
        a = a.reshape(B, H, n_q, Q_BLOCK, *a.shape[3:])
        return jnp.moveaxis(a, 2, 0)

    xs = (to_blocks(q), to_blocks(sel), valid.reshape(n_q, Q_BLOCK, k_sel), jnp.arange(n_q))

    def one_block(args):
        qb, sb, vmask, qi = args
        t = qi * Q_BLOCK + jnp.arange(Q_BLOCK)
        own = (qi * Q_BLOCK) // MOBA_BLOCK
        k_g = kb[bi, hi, sb]
        v_g = vb[bi, hi, sb]
        s_g = jnp.einsum('bhqd,bhqnkd->bhqnk', qb, k_g).astype(jnp.float32) * scale
        pos_g = sb[..., None] * MOBA_BLOCK + offs
        dist_g = (t[None, None, :, None, None] - pos_g).astype(jnp.float32)
        s_g = s_g - slopes[None, :, None, None, None] * dist_g
        s_g = jnp.where(vmask[None, None, :, :, None], s_g, NEG_INF)
        k_o = lax.dynamic_index_in_dim(kb, own, axis=2, keepdims=False)
        v_o = lax.dynamic_index_in_dim(vb, own, axis=2, keepdims=False)
        pos_o = own * MOBA_BLOCK + offs
        dist_o = (t[:, None] - pos_o[None, :]).astype(jnp.float32)
        s_o = jnp.einsum('bhqd,bhkd->bhqk', qb, k_o).astype(jnp.float32) * scale
        s_o = s_o - slopes[None, :, None, None] * dist_o
        s_o = jnp.where(dist_o >= 0, s_o, NEG_INF)
        s_all = jnp.concatenate([s_g.reshape(B, H, Q_BLOCK, k_sel * MOBA_BLOCK), s_o], axis=-1)
        p = jax.nn.softmax(s_all, axis=-1).astype(v.dtype)
        p_g = p[..., :k_sel * MOBA_BLOCK].reshape(B, H, Q_BLOCK, k_sel, MOBA_BLOCK)
        p_o = p[..., k_sel * MOBA_BLOCK:]
        return (jnp.einsum('bhqnk,bhqnkd->bhqd', p_g, v_g)
                + jnp.einsum('bhqk,bhkd->bhqd', p_o, v_o))

    o = lax.map(one_block, xs)
    o = jnp.moveaxis(o, 0, 2).reshape(B, H, s_pad, dh)[:, :, :S]
    return jnp.transpose(o, (0, 2, 1, 3)).reshape(B, S, H * dh)


def retention(q, k, v):
    B, S, H, dk = q.shape
    dv = v.shape[-1]
    n_c = S // R_CHUNK

    def chunks(a):
        a = a.astype(jnp.float32).reshape(B, n_c, R_CHUNK, H, a.shape[-1])
        return jnp.transpose(a, (1, 0, 3, 2, 4))

    qc, kc, vc = chunks(q), chunks(k * (dk ** -0.5)), chunks(v)
    log_g = jnp.log(1.0 - jnp.exp2(-5.0 - jnp.arange(H, dtype=jnp.float32)))
    n = jnp.arange(R_CHUNK, dtype=jnp.float32)
    diff = n[:, None] - n[None, :]
    decay = jnp.where(diff >= 0, jnp.exp(jnp.maximum(diff, 0.0) * log_g[:, None, None]), 0.0)
    q_decay = jnp.exp((n + 1.0) * log_g[:, None])
    k_decay = jnp.exp((R_CHUNK - 1.0 - n) * log_g[:, None])
    chunk_decay = jnp.exp(R_CHUNK * log_g)

    def step(state, inp):
        qi, ki, vi = inp
        inner = jnp.einsum('bhnd,bhmd->bhnm', qi, ki) * decay
        out = (jnp.einsum('bhnm,bhme->bhne', inner, vi)
               + jnp.einsum('bhnd,bhde->bhne', qi, state) * q_decay[None, :, :, None])
        state = (chunk_decay[None, :, None, None] * state
                 + jnp.einsum('bhmd,bhme->bhde', ki * k_decay[None, :, :, None], vi))
        return state, out

    state0 = jnp.zeros((B, H, dk, dv), jnp.float32)
    _, o = lax.scan(step, state0, (qc, kc, vc))
    return jnp.transpose(o, (1, 0, 3, 2, 4)).reshape(B, S, H, dv)


def head_groupnorm(y):
    mu = jnp.mean(y, axis=-1, keepdims=True)
    var = jnp.mean(jnp.square(y - mu), axis=-1, keepdims=True)
    return (y - mu) * lax.rsqrt(var + GN_EPS)


def token_mixer(h, w_in, w_pa, w_pr, w_out):
    B, S, _ = h.shape
    proj = h @ w_in
    cuts = np.cumsum(IN_WIDTHS)[:-1].tolist()
    qa, ka, va, qr, kr, vr, gr, ga, gt = jnp.split(proj, cuts, axis=-1)
    y_a = moba_attention(qa.reshape(B, S, A_HEADS, A_HEAD_DIM),
                         ka.reshape(B, S, A_HEADS, A_HEAD_DIM),
                         va.reshape(B, S, A_HEADS, A_HEAD_DIM)).astype(h.dtype)
    y_r = retention(qr.reshape(B, S, R_HEADS, R_QK_DIM),
                    kr.reshape(B, S, R_HEADS, R_QK_DIM),
                    vr.reshape(B, S, R_HEADS, R_V_DIM))
    y_r = head_groupnorm(y_r).reshape(B, S, R_V_WIDTH).astype(h.dtype) * jax.nn.silu(gr)
    merged = jax.nn.sigmoid(ga) * (y_a @ w_pa) + jax.nn.sigmoid(gt) * (y_r @ w_pr)
    return merged @ w_out


def moe_ffn(h, w_router, router_bias, w1, w3, w2, ws1, ws3, ws2):
    T, D = h.shape
    scores = jax.nn.sigmoid((h @ w_router).astype(jnp.float32))
    choice = scores + router_bias.astype(jnp.float32)
    grouped = choice.reshape(T, N_GROUPS, N_EXPERTS // N_GROUPS)
    group_score = jnp.sum(lax.top_k(grouped, 2)[0], axis=-1)
    _, gidx = lax.top_k(group_score, TOPK_GROUPS)
    gmask = jnp.any(gidx[..., None] == jnp.arange(N_GROUPS), axis=1)
    emask = jnp.repeat(gmask, N_EXPERTS // N_GROUPS, axis=1)
    _, eidx = lax.top_k(jnp.where(emask, choice, NEG_INF), TOP_K)
    wts = jnp.take_along_axis(scores, eidx, axis=1)
    wts = wts / (jnp.sum(wts, axis=-1, keepdims=True) + 1e-20) * ROUTED_SCALE

    n_assign = T * TOP_K
    flat_e = eidx.reshape(-1)
    flat_tok = jnp.repeat(jnp.arange(T, dtype=jnp.int32), TOP_K)
    flat_w = wts.reshape(-1)
    order = jnp.argsort(flat_e)
    se, st, sw = flat_e[order], flat_tok[order], flat_w[order]
    counts = jnp.bincount(flat_e, length=N_EXPERTS)
    padded = (counts + EXPERT_ROWS - 1) // EXPERT_ROWS * EXPERT_ROWS
    pstart = jnp.cumsum(padded) - padded
    cstart = jnp.cumsum(counts) - counts
    pos = pstart[se] + (jnp.arange(n_assign) - cstart[se])
    n_rblk = -(-n_assign // EXPERT_ROWS) + N_EXPERTS
    pad_tok = jnp.zeros((n_rblk * EXPERT_ROWS,), jnp.int32).at[pos].set(st)
    pad_w = jnp.zeros((n_rblk * EXPERT_ROWS,), jnp.float32).at[pos].set(sw)
    blk_end = jnp.cumsum(padded // EXPERT_ROWS)
    blk_e = jnp.minimum(jnp.searchsorted(blk_end, jnp.arange(n_rblk), side='right'), N_EXPERTS - 1)

    def step(acc, inp):
        e, tok, wt = inp
        xb = h[tok]
        o = (jax.nn.silu(xb @ w1[e]) * (xb @ w3[e])) @ w2[e]
        return acc.at[tok].add(o.astype(jnp.float32) * wt[:, None]), None

    acc, _ = lax.scan(step, jnp.zeros((T, D), jnp.float32),
                      (blk_e, pad_tok.reshape(n_rblk, EXPERT_ROWS), pad_w.reshape(n_rblk, EXPERT_ROWS)))
    shared = (jax.nn.silu(h @ ws1) * (h @ ws3)) @ ws2
    return acc.astype(h.dtype) + shared


def setup_inputs(seed: int = 0) -> dict:
    key = jax.random.key(seed)
    ks = jax.random.split(key, 20)
    L, D = DEPTH, D_MODEL

    def nrm(k, shape, scale):
        return jax.random.normal(k, shape, jnp.float32) * scale

    return {
        'x': nrm(ks[0], (BATCH, SEQ, D), 1.0),
        'c': nrm(ks[1], (BATCH, D), 1.0),
        'w_ada': nrm(ks[2], (L, D, 6 * D), 0.5 * D ** -0.5),
        'b_ada': nrm(ks[3], (L, 6 * D), 0.02),
        'g_mix': 1.0 + nrm(ks[4], (L, D), 0.02),
        'w_in': nrm(ks[5], (L, D, IN_COLS), D ** -0.5),
        'w_pa': nrm(ks[6], (L, A_WIDTH, D), A_WIDTH ** -0.5),
        'w_pr': nrm(ks[7], (L, R_V_WIDTH, D), R_V_WIDTH ** -0.5),
        'w_out': nrm(ks[8], (L, D, D), D ** -0.5),
        'g_ffn': 1.0 + nrm(ks[9], (L, D), 0.02),
        'w_router': nrm(ks[10], (L, D, N_EXPERTS), D ** -0.5),
        'router_bias': nrm(ks[11], (L, N_EXPERTS), 0.01),
        'w1': nrm(ks[12], (L, N_EXPERTS, D, EXPERT_FF), D ** -0.5),
        'w3': nrm(ks[13], (L, N_EXPERTS, D, EXPERT_FF), D ** -0.5),
        'w2': nrm(ks[14], (L, N_EXPERTS, EXPERT_FF, D), EXPERT_FF ** -0.5),
        'ws1': nrm(ks[15], (L, D, SHARED_FF), D ** -0.5),
        'ws3': nrm(ks[16], (L, D, SHARED_FF), D ** -0.5),
        'ws2': nrm(ks[17], (L, SHARED_FF, D), SHARED_FF ** -0.5),
        'g_final': 1.0 + nrm(ks[18], (D,), 0.02),
    }


def reference(x, c, w_ada, b_ada, g_mix, w_in, w_pa, w_pr, w_out, g_ffn, w_router,
              router_bias, w1, w3, w2, ws1, ws3, ws2, g_final):
    B, S, D = x.shape
    for l in range(DEPTH):
        mod = (jax.nn.silu(c) @ w_ada[l] + b_ada[l])[:, None, :]
        sh1, sc1, gt1, sh2, sc2, gt2 = jnp.split(mod, 6, axis=-1)
        h = rmsnorm(x, g_mix[l]) * (1.0 + sc1) + sh1
        x = x + gt1 * token_mixer(h, w_in[l], w_pa[l], w_pr[l], w_out[l])
        h = rmsnorm(x, g_ffn[l]) * (1.0 + sc2) + sh2
        y = moe_ffn(h.reshape(B * S, D), w_router[l], router_bias[l], w1[l], w3[l], w2[l],
                    ws1[l], ws3[l], ws2[l])
        x = x + gt2 * y.reshape(B, S, D)
    return rmsnorm(x, g_final)
```

```python
import functools

import jax
import jax.numpy as jnp
import numpy as np
from jax import lax
from jax.experimental import pallas as pl
from jax.experimental.pallas import tpu as pltpu

F32 = jnp.float32
BF16 = jnp.bfloat16

D_MODEL = 1024
A_HEADS = 8
A_HEAD_DIM = 64
A_WIDTH = A_HEADS * A_HEAD_DIM
MOBA_BLOCK = 256
MOBA_TOPK = 3
R_HEADS = 8
R_QK_DIM = 64
R_V_DIM = 128
R_QK_WIDTH = R_HEADS * R_QK_DIM
R_V_WIDTH = R_HEADS * R_V_DIM
R_CHUNK = 128
N_EXPERTS = 256
TOP_K = 8
N_GROUPS = 8
GROUP_SIZE = N_EXPERTS // N_GROUPS
TOPK_GROUPS = 4
EXPERT_FF = 256
SHARED_FF = 256
ROUTED_SCALE = 2.5
EXPERT_ROWS = 128
NORM_EPS = 1e-6
GN_EPS = 1e-6
NEG_INF = -1e30
KNOCKED_OUT = -3e38

COL_VR, COL_GR, COL_GA, COL_GT = 0, 1024, 2048, 3072
COL_QA, COL_KA, COL_VA, COL_QR, COL_KR = 4096, 4608, 5120, 5632, 6144
IN_COLS = 6656
AUG = 128
FEAT_BIAS = A_HEAD_DIM
FEAT_POS = A_HEAD_DIM + 32

VMEM_LIMIT = 56 * 1024 * 1024


def _cparams(sem, vmem=VMEM_LIMIT):
    return pltpu.CompilerParams(dimension_semantics=sem, vmem_limit_bytes=vmem)


def _dot(a, b):
    return jnp.dot(a, b, preferred_element_type=F32)


def _dot_nt(a, b):
    return lax.dot_general(a, b, (((1,), (1,)), ((), ())), preferred_element_type=F32)


def _sigmoid(x):
    return 1.0 / (1.0 + jnp.exp(-x))


def _silu(x):
    return x * _sigmoid(x)


def _ada_kernel(c_ref, w_ref, b_ref, o_ref):
    c = c_ref[...]
    s = _silu(c)
    s_hi = s.astype(BF16)
    s_lo = (s - s_hi.astype(F32)).astype(BF16)
    w = w_ref[...]
    w_hi = w.astype(BF16)
    w_lo = (w - w_hi.astype(F32)).astype(BF16)
    o_ref[...] = _dot(s_hi, w_hi) + _dot(s_hi, w_lo) + _dot(s_lo, w_hi) + b_ref[...]


def _ada(c, w_ada, b_ada):
    bsz, d = c.shape
    n = w_ada.shape[1]
    tn = 1024
    return pl.pallas_call(
        _ada_kernel,
        grid=(n // tn,),
        in_specs=[pl.BlockSpec((bsz, d), lambda j: (0, 0)),
                  pl.BlockSpec((d, tn), lambda j: (0, j)),
                  pl.BlockSpec((1, tn), lambda j: (0, j))],
        out_specs=pl.BlockSpec((bsz, tn), lambda j: (0, j)),
        out_shape=jax.ShapeDtypeStruct((bsz, n), F32),
        compiler_params=_cparams(("parallel",)),
        name="ada_mod",
    )(c, w_ada, b_ada.reshape(1, n))


def _inproj_kernel(x_ref, g_ref, sc_ref, sh_ref, w_ref, o_ref, h_scr):
    @pl.when(pl.program_id(1) == 0)
    def _():
        x = x_ref[...]
        ms = jnp.mean(x * x, axis=-1, keepdims=True)
        y = x * lax.rsqrt(ms + NORM_EPS) * g_ref[...]
        h_scr[...] = (y * (1.0 + sc_ref[0]) + sh_ref[0]).astype(BF16)

    o_ref[...] = _dot(h_scr[...], w_ref[...]).astype(BF16)


def _inproj(x2d, g, sc, sh, w_bf16, seq):
    t, d = x2d.shape
    n = w_bf16.shape[1]
    tm = min(1024, seq)
    tn = 512
    per_b = seq // tm
    return pl.pallas_call(
        _inproj_kernel,
        grid=(t // tm, n // tn),
        in_specs=[pl.BlockSpec((tm, d), lambda i, j: (i, 0)),
                  pl.BlockSpec((1, d), lambda i, j: (0, 0)),
                  pl.BlockSpec((1, 1, d), lambda i, j: (i // per_b, 0, 0)),
                  pl.BlockSpec((1, 1, d), lambda i, j: (i // per_b, 0, 0)),
                  pl.BlockSpec((d, tn), lambda i, j: (0, j))],
        out_specs=pl.BlockSpec((tm, tn), lambda i, j: (i, j)),
        out_shape=jax.ShapeDtypeStruct((t, n), BF16),
        scratch_shapes=[pltpu.VMEM((tm, d), BF16)],
        compiler_params=_cparams(("parallel", "arbitrary")),
        name="norm_inproj",
    )(x2d, g, sc, sh, w_bf16)


def _moba_prep_kernel(slopes_ref, q_ref, k_ref, v_ref, ko_ref, qo_ref, vo_ref, kmean_scr):
    i = pl.program_id(1)
    nblk = kmean_scr.shape[0]
    width = q_ref.shape[1]

    @pl.when(i == 0)
    def _():
        kmean_scr[...] = jnp.zeros_like(kmean_scr)

    q = q_ref[...]
    k = k_ref[...]
    v = v_ref[...]
    kmean_scr[pl.ds(i, 1), :] = jnp.mean(k.astype(F32), axis=0, keepdims=True)

    eye = (lax.broadcasted_iota(jnp.int32, (width, width), 0)
           == lax.broadcasted_iota(jnp.int32, (width, width), 1)).astype(BF16)
    q_t = _dot_nt(eye, q)
    v_t = _dot_nt(eye, v)

    km = kmean_scr[...]
    km_rep = jnp.concatenate([km] * A_HEADS, axis=0)
    r_head = lax.broadcasted_iota(jnp.int32, km_rep.shape, 0) // nblk
    c_head = lax.broadcasted_iota(jnp.int32, km_rep.shape, 1) // A_HEAD_DIM
    km_bd = jnp.where(r_head == c_head, km_rep, 0.0)
    km_hi = km_bd.astype(BF16)
    km_lo = (km_bd - km_hi.astype(F32)).astype(BF16)
    q_t_b = q_t.astype(BF16)
    gate_all = _dot(km_hi, q_t_b) + _dot(km_lo, q_t_b)

    mb = q.shape[0]
    blk = lax.broadcasted_iota(jnp.int32, (nblk, mb), 0)
    lane_pos = lax.broadcasted_iota(jnp.int32, (16, mb), 1).astype(F32)
    row16 = lax.broadcasted_iota(jnp.int32, (16, mb), 0)
    key_pos = lax.broadcasted_iota(jnp.int32, (mb, AUG), 0).astype(F32)
    kcol = lax.broadcasted_iota(jnp.int32, (mb, AUG), 1)
    sel_r = lax.broadcasted_iota(jnp.int32, (width, AUG), 0)
    sel_c = lax.broadcasted_iota(jnp.int32, (width, AUG), 1)

    for h in range(A_HEADS):
        slope = slopes_ref[h]
        g = jnp.where(blk < i, gate_all[h * nblk:(h + 1) * nblk, :], NEG_INF)
        sel = blk == i
        for r in range(MOBA_TOPK):
            m = jnp.max(g, axis=0, keepdims=True)
            idx = jnp.min(jnp.where(g == m, blk, nblk), axis=0, keepdims=True)
            hit = blk == idx
            sel = jnp.logical_or(sel, jnp.logical_and(hit, r < i))
            g = jnp.where(hit, KNOCKED_OUT, g)
        bias_t = jnp.where(sel, 0.0, NEG_INF)

        scale = A_HEAD_DIM ** -0.5
        qo_ref[0, h, 0:A_HEAD_DIM, :] = (q_t[h * A_HEAD_DIM:(h + 1) * A_HEAD_DIM, :] * scale).astype(BF16)
        qo_ref[0, h, FEAT_BIAS:FEAT_BIAS + nblk, :] = bias_t.astype(BF16)
        if nblk < 32:
            qo_ref[0, h, FEAT_BIAS + nblk:FEAT_POS, :] = jnp.zeros((32 - nblk, mb), BF16)
        pos_feat = jnp.where(row16 == 0, -slope * lane_pos, jnp.where(row16 == 1, 1.0, 0.0))
        qo_ref[0, h, FEAT_POS:FEAT_POS + 16, :] = pos_feat.astype(BF16)
        qo_ref[0, h, FEAT_POS + 16:AUG, :] = jnp.zeros((AUG - FEAT_POS - 16, mb), BF16)

        vo_ref[0, h, 0:A_HEAD_DIM, :] = v_t[h * A_HEAD_DIM:(h + 1) * A_HEAD_DIM, :].astype(BF16)
        vo_ref[0, h, A_HEAD_DIM:A_HEAD_DIM + 16, :] = jnp.where(row16 == 0, 1.0, 0.0).astype(BF16)
        vo_ref[0, h, A_HEAD_DIM + 16:AUG, :] = jnp.zeros((AUG - A_HEAD_DIM - 16, mb), BF16)

        pick = jnp.where(jnp.logical_and(sel_r == sel_c + h * A_HEAD_DIM, sel_c < A_HEAD_DIM),
                         1.0, 0.0).astype(BF16)
        k_feat = jnp.where(kcol == FEAT_BIAS + i, 1.0,
                           jnp.where(kcol == FEAT_POS, 1.0,
                                     jnp.where(kcol == FEAT_POS + 1, slope * key_pos, 0.0)))
        ko_ref[0, h, :, :] = (_dot(k, pick) + k_feat).astype(BF16)


def _moba_prep(proj3, slopes):
    bsz, seq, _ = proj3.shape
    nblk = seq // MOBA_BLOCK
    mb = MOBA_BLOCK
    grid_spec = pltpu.PrefetchScalarGridSpec(
        num_scalar_prefetch=1,
        grid=(bsz, nblk),
        in_specs=[pl.BlockSpec((None, mb, A_WIDTH), lambda b, i, s: (b, i, COL_QA // A_WIDTH)),
                  pl.BlockSpec((None, mb, A_WIDTH), lambda b, i, s: (b, i, COL_KA // A_WIDTH)),
                  pl.BlockSpec((None, mb, A_WIDTH), lambda b, i, s: (b, i, COL_VA // A_WIDTH))],
        out_specs=[pl.BlockSpec((1, A_HEADS, mb, AUG), lambda b, i, s: (b, 0, i, 0)),
                   pl.BlockSpec((1, A_HEADS, AUG, mb), lambda b, i, s: (b, 0, 0, i)),
                   pl.BlockSpec((1, A_HEADS, AUG, mb), lambda b, i, s: (b, 0, 0, i))],
        scratch_shapes=[pltpu.VMEM((nblk, A_WIDTH), F32)],
    )
    return pl.pallas_call(
        _moba_prep_kernel,
        grid_spec=grid_spec,
        out_shape=[jax.ShapeDtypeStruct((bsz, A_HEADS, seq, AUG), BF16),
                   jax.ShapeDtypeStruct((bsz, A_HEADS, AUG, seq), BF16),
                   jax.ShapeDtypeStruct((bsz, A_HEADS, AUG, seq), BF16)],
        compiler_params=_cparams(("parallel", "arbitrary")),
        name="moba_prep",
    )(slopes, proj3, proj3, proj3)


def _moba_attn_kernel(slopes_ref, q_ref, k_ref, v_ref, o_ref):
    hp = pl.program_id(1)
    i = pl.program_id(2)
    mb = MOBA_BLOCK
    own = pl.multiple_of(i * mb, mb)
    key_i = lax.broadcasted_iota(jnp.int32, (mb, mb), 0)
    qry_i = lax.broadcasted_iota(jnp.int32, (mb, mb), 1)
    outs = []
    for hh in range(2):
        slope = slopes_ref[2 * hp + hh]
        q_t = q_ref[0, hh]
        s = _dot(k_ref[0, hh, pl.ds(own, mb), :], q_t)
        s = jnp.where(key_i <= qry_i, s, NEG_INF)
        m0 = jnp.max(s, axis=0, keepdims=True)
        p = jnp.exp(s - m0)
        acc0 = _dot(v_ref[0, hh, :, pl.ds(own, mb)], p.astype(BF16))

        def body(n, carry, q_t=q_t, hh=hh, slope=slope):
            m, acc = carry
            start = pl.multiple_of(n * mb, mb)
            sb = _dot(k_ref[0, hh, pl.ds(start, mb), :], q_t)
            c = slope * ((i - n) * mb).astype(F32)
            m_new = jnp.maximum(m, jnp.max(sb, axis=0, keepdims=True) - c)
            pb = jnp.exp(sb - (m_new + c))
            alpha = jnp.exp(m - m_new)
            acc = acc * alpha + _dot(v_ref[0, hh, :, pl.ds(start, mb)], pb.astype(BF16))
            return m_new, acc

        _, acc = lax.fori_loop(0, i, body, (m0, acc0))
        outs.append(acc[0:A_HEAD_DIM, :] / acc[A_HEAD_DIM:A_HEAD_DIM + 1, :])
    o_t = jnp.concatenate(outs, axis=0).astype(BF16)
    eye = (key_i == qry_i).astype(BF16)
    o_ref[0] = _dot_nt(eye, o_t).astype(BF16)


def _moba_attn(k_aug, q_aug_t, v_aug_t, slopes):
    bsz, nh, seq, _ = k_aug.shape
    mb = MOBA_BLOCK
    grid_spec = pltpu.PrefetchScalarGridSpec(
        num_scalar_prefetch=1,
        grid=(bsz, nh // 2, seq // mb),
        in_specs=[pl.BlockSpec((1, 2, AUG, mb), lambda b, h, i, s: (b, h, 0, i)),
                  pl.BlockSpec((1, 2, seq, AUG), lambda b, h, i, s: (b, h, 0, 0)),
                  pl.BlockSpec((1, 2, AUG, seq), lambda b, h, i, s: (b, h, 0, 0))],
        out_specs=pl.BlockSpec((1, mb, 2 * A_HEAD_DIM), lambda b, h, i, s: (b, i, h)),
    )
    return pl.pallas_call(
        _moba_attn_kernel,
        grid_spec=grid_spec,
        out_shape=jax.ShapeDtypeStruct((bsz, seq, A_WIDTH), BF16),
        compiler_params=_cparams(("parallel", "parallel", "arbitrary")),
        name="moba_attn",
    )(slopes, q_aug_t, k_aug, v_aug_t)


def _retention_kernel(cdec_ref, q_ref, k_ref, v_ref, g_ref, decay_ref, qdec_ref, kdec_ref, o_ref, state_scr):
    @pl.when(pl.program_id(1) == 0)
    def _():
        state_scr[...] = jnp.zeros_like(state_scr)

    q = q_ref[...]
    k = k_ref[...]
    width = q.shape[1]
    eye = (lax.broadcasted_iota(jnp.int32, (width, width), 0)
           == lax.broadcasted_iota(jnp.int32, (width, width), 1)).astype(BF16)
    k_t = _dot_nt(eye, k)
    k_t_b = k_t.astype(BF16)
    head_of_col = lax.broadcasted_iota(jnp.int32, q.shape, 1) // R_QK_DIM
    state_b = state_scr[...].astype(BF16)
    for h in range(R_HEADS):
        rows = slice(h * R_QK_DIM, (h + 1) * R_QK_DIM)
        cols = slice(h * R_V_DIM, (h + 1) * R_V_DIM)
        q_m = jnp.where(head_of_col == h, q, jnp.zeros_like(q))
        v_h = v_ref[:, cols]
        inner = _dot(q_m, k_t_b) * decay_ref[h]
        out = _dot(inner.astype(BF16), v_h) + _dot(q_m, state_b) * qdec_ref[h]
        k_dec = (k_t[rows, :] * kdec_ref[h]).astype(BF16)
        state_scr[rows, :] = cdec_ref[h] * state_scr[rows, :] + _dot(k_dec, v_h)
        mu = jnp.mean(out, axis=-1, keepdims=True)
        cen = out - mu
        var = jnp.mean(cen * cen, axis=-1, keepdims=True)
        y = cen * lax.rsqrt(var + GN_EPS)
        o_ref[:, cols] = (y * _silu(g_ref[:, cols].astype(F32))).astype(BF16)


def _retention_consts():
    h = np.arange(R_HEADS, dtype=np.float64)
    log_g = np.log(1.0 - np.exp2(-5.0 - h))
    n = np.arange(R_CHUNK, dtype=np.float64)
    diff = n[:, None] - n[None, :]
    scale = R_QK_DIM ** -0.5
    decay = np.where(diff >= 0, np.exp(np.maximum(diff, 0.0) * log_g[:, None, None]), 0.0) * scale
    q_decay = np.exp((n + 1.0) * log_g[:, None])[:, :, None]
    k_decay = np.exp((R_CHUNK - 1.0 - n) * log_g[:, None])[:, None, :] * scale
    chunk_decay = np.exp(R_CHUNK * log_g)
    return (jnp.asarray(decay, F32), jnp.asarray(q_decay, F32), jnp.asarray(k_decay, F32),
            jnp.asarray(chunk_decay, F32))


def _retention(proj3):
    bsz, seq, _ = proj3.shape
    c = R_CHUNK
    decay, qdec, kdec, cdec = _retention_consts()
    grid_spec = pltpu.PrefetchScalarGridSpec(
        num_scalar_prefetch=1,
        grid=(bsz, seq // c),
        in_specs=[pl.BlockSpec((None, c, R_QK_WIDTH), lambda b, i, s: (b, i, COL_QR // R_QK_WIDTH)),
                  pl.BlockSpec((None, c, R_QK_WIDTH), lambda b, i, s: (b, i, COL_KR // R_QK_WIDTH)),
                  pl.BlockSpec((None, c, R_V_WIDTH), lambda b, i, s: (b, i, COL_VR // R_V_WIDTH)),
                  pl.BlockSpec((None, c, R_V_WIDTH), lambda b, i, s: (b, i, COL_GR // R_V_WIDTH)),
                  pl.BlockSpec((R_HEADS, c, c), lambda b, i, s: (0, 0, 0)),
                  pl.BlockSpec((R_HEADS, c, 1), lambda b, i, s: (0, 0, 0)),
                  pl.BlockSpec((R_HEADS, 1, c), lambda b, i, s: (0, 0, 0))],
        out_specs=pl.BlockSpec((None, c, R_V_WIDTH), lambda b, i, s: (b, i, 0)),
        scratch_shapes=[pltpu.VMEM((R_QK_WIDTH, R_V_DIM), F32)],
    )
    return pl.pallas_call(
        _retention_kernel,
        grid_spec=grid_spec,
        out_shape=jax.ShapeDtypeStruct((bsz, seq, R_V_WIDTH), BF16),
        compiler_params=_cparams(("parallel", "arbitrary")),
        name="retention",
    )(cdec, proj3, proj3, proj3, proj3, decay, qdec, kdec)


def _mix_kernel(ya_ref, yr_ref, ga_ref, gt_ref, x_ref, wpa_ref, wpr_ref, wout_ref,
                gt1_ref, g_ref, sc_ref, sh_ref, x1_ref, h2_ref):
    a = _dot(ya_ref[...], wpa_ref[...]) * _sigmoid(ga_ref[...].astype(F32))
    r = _dot(yr_ref[...], wpr_ref[...]) * _sigmoid(gt_ref[...].astype(F32))
    mix = _dot((a + r).astype(BF16), wout_ref[...])
    x1 = x_ref[...] + gt1_ref[0] * mix
    x1_ref[...] = x1
    ms = jnp.mean(x1 * x1, axis=-1, keepdims=True)
    y = x1 * lax.rsqrt(ms + NORM_EPS) * g_ref[...]
    h2_ref[...] = (y * (1.0 + sc_ref[0]) + sh_ref[0]).astype(BF16)


def _mix(ya, yr, proj, x2d, wpa, wpr, wout, gt1, g, sc, sh, seq):
    t, d = x2d.shape
    tm = min(512, seq)
    per_b = seq // tm
    row = lambda i: (i, 0)
    full = lambda i: (0, 0)
    per_batch = lambda i: (i // per_b, 0, 0)
    return pl.pallas_call(
        _mix_kernel,
        grid=(t // tm,),
        in_specs=[pl.BlockSpec((tm, A_WIDTH), row),
                  pl.BlockSpec((tm, R_V_WIDTH), row),
                  pl.BlockSpec((tm, d), lambda i: (i, COL_GA // D_MODEL)),
                  pl.BlockSpec((tm, d), lambda i: (i, COL_GT // D_MODEL)),
                  pl.BlockSpec((tm, d), row),
                  pl.BlockSpec((A_WIDTH, d), full),
                  pl.BlockSpec((R_V_WIDTH, d), full),
                  pl.BlockSpec((d, d), full),
                  pl.BlockSpec((1, 1, d), per_batch),
                  pl.BlockSpec((1, d), full),
                  pl.BlockSpec((1, 1, d), per_batch),
                  pl.BlockSpec((1, 1, d), per_batch)],
        out_specs=[pl.BlockSpec((tm, d), row), pl.BlockSpec((tm, d), row)],
        out_shape=[jax.ShapeDtypeStruct((t, d), F32), jax.ShapeDtypeStruct((t, d), BF16)],
        compiler_params=_cparams(("parallel",)),
        name="merge_outproj_norm",
    )(ya, yr, proj, proj, x2d, wpa, wpr, wout, gt1, g, sc, sh)


def _router_kernel(h_ref, wr_ref, b_ref, e_ref, w_ref):
    logits = _dot_nt(wr_ref[...], h_ref[...])
    scores = _sigmoid(logits)
    choice = scores + b_ref[...]
    tm = logits.shape[1]
    giota = lax.broadcasted_iota(jnp.int32, (GROUP_SIZE, tm), 0)
    gs_rows = []
    for g in range(N_GROUPS):
        cg = choice[g * GROUP_SIZE:(g + 1) * GROUP_SIZE, :]
        m1 = jnp.max(cg, axis=0, keepdims=True)
        i1 = jnp.min(jnp.where(cg == m1, giota, GROUP_SIZE), axis=0, keepdims=True)
        m2 = jnp.max(jnp.where(giota == i1, KNOCKED_OUT, cg), axis=0, keepdims=True)
        gs_rows.append(m1 + m2)
    gs = jnp.concatenate(gs_rows, axis=0)
    grow = lax.broadcasted_iota(jnp.int32, (N_GROUPS, tm), 0)
    gmask = jnp.zeros((N_GROUPS, tm), jnp.bool_)
    for _ in range(TOPK_GROUPS):
        mx = jnp.max(gs, axis=0, keepdims=True)
        ix = jnp.min(jnp.where(gs == mx, grow, N_GROUPS), axis=0, keepdims=True)
        hit = grow == ix
        gmask = jnp.logical_or(gmask, hit)
        gs = jnp.where(hit, KNOCKED_OUT, gs)
    gmask_f = jnp.where(gmask, 1.0, 0.0)
    masked = jnp.concatenate(
        [jnp.where(gmask_f[g:g + 1, :] > 0.5, choice[g * GROUP_SIZE:(g + 1) * GROUP_SIZE, :], NEG_INF)
         for g in range(N_GROUPS)], axis=0)
    erow = lax.broadcasted_iota(jnp.int32, (N_EXPERTS, tm), 0)
    idx_rows, w_rows = [], []
    for _ in range(TOP_K):
        mx = jnp.max(masked, axis=0, keepdims=True)
        ix = jnp.min(jnp.where(masked == mx, erow, N_EXPERTS), axis=0, keepdims=True)
        hit = erow == ix
        w_rows.append(jnp.sum(jnp.where(hit, scores, 0.0), axis=0, keepdims=True))
        idx_rows.append(ix)
        masked = jnp.where(hit, KNOCKED_OUT, masked)
    w = jnp.concatenate(w_rows, axis=0)
    w = w / (jnp.sum(w, axis=0, keepdims=True) + 1e-20) * ROUTED_SCALE
    e_ref[...] = jnp.concatenate(idx_rows, axis=0)
    w_ref[...] = w


def _router(h2, wr_t, bias_col):
    t, d = h2.shape
    tm = 512
    return pl.pallas_call(
        _router_kernel,
        grid=(t // tm,),
        in_specs=[pl.BlockSpec((tm, d), lambda i: (i, 0)),
                  pl.BlockSpec((N_EXPERTS, d), lambda i: (0, 0)),
                  pl.BlockSpec((N_EXPERTS, 1), lambda i: (0, 0))],
        out_specs=[pl.BlockSpec((TOP_K, tm), lambda i: (0, i)),
                   pl.BlockSpec((TOP_K, tm), lambda i: (0, i))],
        out_shape=[jax.ShapeDtypeStruct((TOP_K, t), jnp.int32),
                   jax.ShapeDtypeStruct((TOP_K, t), F32)],
        compiler_params=_cparams(("parallel",)),
        name="router_topk",
    )(h2, wr_t, bias_col)


def _experts_kernel(blk_e_ref, nblk_ref, tok_hbm, dst_hbm, x_hbm, w1_ref, w3_ref, w2_ref, y_hbm,
                    tok_smem, dst_smem, xbuf, obuf, idx_sem, in_sem, out_sem):
    s = pl.program_id(0)
    rows = xbuf.shape[0]

    @pl.when(s < nblk_ref[0])
    def _():
        c_tok = pltpu.make_async_copy(tok_hbm.at[s], tok_smem, idx_sem.at[0])
        c_dst = pltpu.make_async_copy(dst_hbm.at[s], dst_smem, idx_sem.at[1])
        c_tok.start()
        c_dst.start()
        c_tok.wait()
        c_dst.wait()

        def row_in(r):
            return pltpu.make_async_copy(x_hbm.at[pl.ds(tok_smem[r], 1), :], xbuf.at[pl.ds(r, 1), :], in_sem)

        def issue_in(r, carry):
            row_in(r).start()
            return carry

        def wait_in(r, carry):
            row_in(r).wait()
            return carry

        lax.fori_loop(0, rows, issue_in, 0)
        lax.fori_loop(0, rows, wait_in, 0)

        x = xbuf[...].astype(BF16)
        h1 = _dot(x, w1_ref[0].astype(BF16))
        h3 = _dot(x, w3_ref[0].astype(BF16))
        mid = (_silu(h1) * h3).astype(BF16)
        obuf[...] = _dot(mid, w2_ref[0].astype(BF16))

        def row_out(r):
            return pltpu.make_async_copy(obuf.at[pl.ds(r, 1), :], y_hbm.at[pl.ds(dst_smem[r], 1), :], out_sem)

        def issue_out(r, carry):
            @pl.when(dst_smem[r] >= 0)
            def _():
                row_out(r).start()
            return carry

        def wait_out(r, carry):
            @pl.when(dst_smem[r] >= 0)
            def _():
                row_out(r).wait()
            return carry

        lax.fori_loop(0, rows, issue_out, 0)
        lax.fori_loop(0, rows, wait_out, 0)


def _experts(blk_e, nblk_used, pad_tok, pad_dst, h2f, w1, w3, w2, n_assign):
    n_rblk = pad_tok.shape[0]
    d = h2f.shape[1]
    grid_spec = pltpu.PrefetchScalarGridSpec(
        num_scalar_prefetch=2,
        grid=(n_rblk,),
        in_specs=[pl.BlockSpec(memory_space=pl.ANY),
                  pl.BlockSpec(memory_space=pl.ANY),
                  pl.BlockSpec(memory_space=pl.ANY),
                  pl.BlockSpec((1, d, EXPERT_FF), lambda s, be, nb: (be[s], 0, 0)),
                  pl.BlockSpec((1, d, EXPERT_FF), lambda s, be, nb: (be[s], 0, 0)),
                  pl.BlockSpec((1, EXPERT_FF, d), lambda s, be, nb: (be[s], 0, 0))],
        out_specs=pl.BlockSpec(memory_space=pl.ANY),
        scratch_shapes=[pltpu.SMEM((EXPERT_ROWS,), jnp.int32),
                        pltpu.SMEM((EXPERT_ROWS,), jnp.int32),
                        pltpu.VMEM((EXPERT_ROWS, d), F32),
                        pltpu.VMEM((EXPERT_ROWS, d), F32),
                        pltpu.SemaphoreType.DMA((2,)),
                        pltpu.SemaphoreType.DMA,
                        pltpu.SemaphoreType.DMA],
    )
    return pl.pallas_call(
        _experts_kernel,
        grid_spec=grid_spec,
        out_shape=jax.ShapeDtypeStruct((n_assign, d), F32),
        compiler_params=_cparams(("arbitrary",)),
        name="routed_experts",
    )(blk_e, nblk_used, pad_tok, pad_dst, h2f, w1, w3, w2)


def _combine_kernel(y_ref, w_ref, h_ref, x1_ref, ws1_ref, ws3_ref, ws2_ref, gt2_ref, g_ref, o_ref):
    w = w_ref[...]
    acc = y_ref[0] * w[:, 0:1]
    for k in range(1, TOP_K):
        acc = acc + y_ref[k] * w[:, k:k + 1]
    h = h_ref[...]
    mid = (_silu(_dot(h, ws1_ref[...])) * _dot(h, ws3_ref[...])).astype(BF16)
    shared = _dot(mid, ws2_ref[...])
    x2 = x1_ref[...] + gt2_ref[0] * (acc + shared)
    ms = jnp.mean(x2 * x2, axis=-1, keepdims=True)
    o_ref[...] = x2 * lax.rsqrt(ms + NORM_EPS) * g_ref[...]


def _combine(y3, wts, h2, x1, ws1, ws3, ws2, gt2, g_final, seq):
    t, d = x1.shape
    tm = min(256, seq)
    per_b = seq // tm
    row = lambda i: (i, 0)
    full = lambda i: (0, 0)
    return pl.pallas_call(
        _combine_kernel,
        grid=(t // tm,),
        in_specs=[pl.BlockSpec((TOP_K, tm, d), lambda i: (0, i, 0)),
                  pl.BlockSpec((tm, TOP_K), row),
                  pl.BlockSpec((tm, d), row),
                  pl.BlockSpec((tm, d), row),
                  pl.BlockSpec((d, SHARED_FF), full),
                  pl.BlockSpec((d, SHARED_FF), full),
                  pl.BlockSpec((SHARED_FF, d), full),
                  pl.BlockSpec((1, 1, d), lambda i: (i // per_b, 0, 0)),
                  pl.BlockSpec((1, d), full)],
        out_specs=pl.BlockSpec((tm, d), row),
        out_shape=jax.ShapeDtypeStruct((t, d), F32),
        compiler_params=_cparams(("parallel",)),
        name="combine_shared_final",
    )(y3, wts, h2, x1, ws1, ws3, ws2, gt2, g_final)


def _dispatch_tables(eidx_t, t):
    n_assign = t * TOP_K
    flat_e = eidx_t.T.reshape(-1)
    order = jnp.argsort(flat_e)
    se = flat_e[order]
    counts = jnp.bincount(flat_e, length=N_EXPERTS)
    padded = (counts + EXPERT_ROWS - 1) // EXPERT_ROWS * EXPERT_ROWS
    pstart = jnp.cumsum(padded) - padded
    cstart = jnp.cumsum(counts) - counts
    pos = pstart[se] + (jnp.arange(n_assign) - cstart[se])
    n_rblk = -(-n_assign // EXPERT_ROWS) + N_EXPERTS
    order = order.astype(jnp.int32)
    pad_tok = jnp.zeros((n_rblk * EXPERT_ROWS,), jnp.int32).at[pos].set(order // TOP_K)
    dst = (order % TOP_K) * t + order // TOP_K
    pad_dst = jnp.full((n_rblk * EXPERT_ROWS,), -1, jnp.int32).at[pos].set(dst)
    blk_end = jnp.cumsum(padded // EXPERT_ROWS)
    blk_e = jnp.minimum(jnp.searchsorted(blk_end, jnp.arange(n_rblk), side='right'), N_EXPERTS - 1)
    nblk_used = blk_end[-1:].astype(jnp.int32)
    return (blk_e.astype(jnp.int32), nblk_used, pad_tok.reshape(n_rblk, EXPERT_ROWS),
            pad_dst.reshape(n_rblk, EXPERT_ROWS))


def _permute_in_cols(w_in):
    qa, ka, va, qr, kr, vr, gr, ga, gt = jnp.split(
        w_in, np.cumsum((A_WIDTH, A_WIDTH, A_WIDTH, R_QK_WIDTH, R_QK_WIDTH, R_V_WIDTH, R_V_WIDTH,
                         D_MODEL))[:].tolist(), axis=1)
    return jnp.concatenate([vr, gr, ga, gt, qa, ka, va, qr, kr], axis=1)


def kernel(x, c, w_ada, b_ada, g_mix, w_in, w_pa, w_pr, w_out, g_ffn, w_router, router_bias,
           w1, w3, w2, ws1, ws3, ws2, g_final):
    bsz, seq, d = x.shape
    t = bsz * seq
    depth = w_ada.shape[0]
    assert depth == 1, "the final norm is fused into the single layer's last kernel"
    slopes = jnp.exp2(-8.0 / A_HEADS * jnp.arange(1, A_HEADS + 1, dtype=F32))
    x2d = x.reshape(t, d)
    for l in range(depth):
        mod = _ada(c, w_ada[l], b_ada[l])
        sh1, sc1, gt1, sh2, sc2, gt2 = [m.reshape(bsz, 1, d) for m in jnp.split(mod, 6, axis=-1)]
        w_in_p = _permute_in_cols(w_in[l]).astype(BF16)
        proj = _inproj(x2d, g_mix[l].reshape(1, d), sc1, sh1, w_in_p, seq)
        proj3 = proj.reshape(bsz, seq, IN_COLS)
        k_aug, q_aug_t, v_aug_t = _moba_prep(proj3, slopes)
        ya = _moba_attn(k_aug, q_aug_t, v_aug_t, slopes).reshape(t, A_WIDTH)
        yr = _retention(proj3).reshape(t, R_V_WIDTH)
        x1, h2 = _mix(ya, yr, proj, x2d, w_pa[l].astype(BF16), w_pr[l].astype(BF16),
                      w_out[l].astype(BF16), gt1, g_ffn[l].reshape(1, d), sc2, sh2, seq)
        eidx_t, wts_t = _router(h2, w_router[l].T.astype(BF16), router_bias[l].reshape(N_EXPERTS, 1))
        blk_e, nblk_used, pad_tok, pad_dst = _dispatch_tables(eidx_t, t)
        y = _experts(blk_e, nblk_used, pad_tok, pad_dst, h2.astype(F32), w1[l], w3[l], w2[l], t * TOP_K)
        x2d = _combine(y.reshape(TOP_K, t, d), wts_t.T, h2, x1, ws1[l].astype(BF16), ws3[l].astype(BF16),
                       ws2[l].astype(BF16), gt2, g_final.reshape(1, d), seq)
    return x2d.reshape(bsz, seq, d)
```

```python
import functools

import jax
import jax.numpy as jnp
import numpy as np
from jax import lax
from jax.experimental import pallas as pl
from jax.experimental.pallas import tpu as pltpu

F32 = jnp.float32
BF16 = jnp.bfloat16

D_MODEL = 1024
A_HEADS = 8
A_HEAD_DIM = 64
A_WIDTH = A_HEADS * A_HEAD_DIM
MOBA_BLOCK = 256
MOBA_TOPK = 3
R_HEADS = 8
R_QK_DIM = 64
R_V_DIM = 128
R_QK_WIDTH = R_HEADS * R_QK_DIM
R_V_WIDTH = R_HEADS * R_V_DIM
R_CHUNK = 128
N_EXPERTS = 256
TOP_K = 8
N_GROUPS = 8
GROUP_SIZE = N_EXPERTS // N_GROUPS
TOPK_GROUPS = 4
EXPERT_FF = 256
SHARED_FF = 256
ROUTED_SCALE = 2.5
EXPERT_ROWS = 128
NORM_EPS = 1e-6
GN_EPS = 1e-6
NEG_INF = -1e30
KNOCKED_OUT = -3e38

COL_VR, COL_GR, COL_GA, COL_GT = 0, 1024, 2048, 3072
COL_QA, COL_KA, COL_VA, COL_QR, COL_KR = 4096, 4608, 5120, 5632, 6144
IN_COLS = 6656
AUG = 128
FEAT_BIAS = A_HEAD_DIM
FEAT_POS = A_HEAD_DIM + 32

VMEM_LIMIT = 56 * 1024 * 1024


def _cparams(sem, vmem=VMEM_LIMIT):
    return pltpu.CompilerParams(dimension_semantics=sem, vmem_limit_bytes=vmem)


def _dot(a, b):
    return jnp.dot(a, b, preferred_element_type=F32)


def _dot_nt(a, b):
    return lax.dot_general(a, b, (((1,), (1,)), ((), ())), preferred_element_type=F32)


def _sigmoid(x):
    return 1.0 / (1.0 + jnp.exp(-x))


def _silu(x):
    return x * _sigmoid(x)


def _ada_kernel(c_ref, w_ref, b_ref, o_ref):
    c = c_ref[...]
    s = _silu(c)
    s_hi = s.astype(BF16)
    s_lo = (s - s_hi.astype(F32)).astype(BF16)
    w = w_ref[...]
    w_hi = w.astype(BF16)
    w_lo = (w - w_hi.astype(F32)).astype(BF16)
    o_ref[...] = _dot(s_hi, w_hi) + _dot(s_hi, w_lo) + _dot(s_lo, w_hi) + b_ref[...]


def _ada(c, w_ada, b_ada):
    bsz, d = c.shape
    n = w_ada.shape[1]
    tn = 1024
    return pl.pallas_call(
        _ada_kernel,
        grid=(n // tn,),
        in_specs=[pl.BlockSpec((bsz, d), lambda j: (0, 0)),
                  pl.BlockSpec((d, tn), lambda j: (0, j)),
                  pl.BlockSpec((1, tn), lambda j: (0, j))],
        out_specs=pl.BlockSpec((bsz, tn), lambda j: (0, j)),
        out_shape=jax.ShapeDtypeStruct((bsz, n), F32),
        compiler_params=_cparams(("parallel",)),
        name="ada_mod",
    )(c, w_ada, b_ada.reshape(1, n))


def _inproj_kernel(x_ref, g_ref, sc_ref, sh_ref, w_ref, o_ref, h_scr):
    @pl.when(pl.program_id(1) == 0)
    def _():
        x = x_ref[...]
        ms = jnp.mean(x * x, axis=-1, keepdims=True)
        y = x * lax.rsqrt(ms + NORM_EPS) * g_ref[...]
        h_scr[...] = (y * (1.0 + sc_ref[0]) + sh_ref[0]).astype(BF16)

    o_ref[...] = _dot(h_scr[...], w_ref[...]).astype(BF16)


def _inproj(x2d, g, sc, sh, w_bf16, seq):
    t, d = x2d.shape
    n = w_bf16.shape[1]
    tm = min(1024, seq)
    tn = 512
    per_b = seq // tm
    return pl.pallas_call(
        _inproj_kernel,
        grid=(t // tm, n // tn),
        in_specs=[pl.BlockSpec((tm, d), lambda i, j: (i, 0)),
                  pl.BlockSpec((1, d), lambda i, j: (0, 0)),
                  pl.BlockSpec((1, 1, d), lambda i, j: (i // per_b, 0, 0)),
                  pl.BlockSpec((1, 1, d), lambda i, j: (i // per_b, 0, 0)),
                  pl.BlockSpec((d, tn), lambda i, j: (0, j))],
        out_specs=pl.BlockSpec((tm, tn), lambda i, j: (i, j)),
        out_shape=jax.ShapeDtypeStruct((t, n), BF16),
        scratch_shapes=[pltpu.VMEM((tm, d), BF16)],
        compiler_params=_cparams(("parallel", "arbitrary")),
        name="norm_inproj",
    )(x2d, g, sc, sh, w_bf16)


def _moba_prep_kernel(slopes_ref, q_ref, k_ref, v_ref, ko_ref, qo_ref, vo_ref, kmean_scr):
    i = pl.program_id(1)
    nblk = kmean_scr.shape[0]
    width = q_ref.shape[1]

    @pl.when(i == 0)
    def _():
        kmean_scr[...] = jnp.zeros_like(kmean_scr)

    q = q_ref[...]
    k = k_ref[...]
    v = v_ref[...]
    kmean_scr[pl.ds(i, 1), :] = jnp.mean(k.astype(F32), axis=0, keepdims=True)

    eye = (lax.broadcasted_iota(jnp.int32, (width, width), 0)
           == lax.broadcasted_iota(jnp.int32, (width, width), 1)).astype(BF16)
    q_t = _dot_nt(eye, q)
    v_t = _dot_nt(eye, v)

    km = kmean_scr[...]
    km_rep = jnp.concatenate([km] * A_HEADS, axis=0)
    r_head = lax.broadcasted_iota(jnp.int32, km_rep.shape, 0) // nblk
    c_head = lax.broadcasted_iota(jnp.int32, km_rep.shape, 1) // A_HEAD_DIM
    km_bd = jnp.where(r_head == c_head, km_rep, 0.0)
    km_hi = km_bd.astype(BF16)
    km_lo = (km_bd - km_hi.astype(F32)).astype(BF16)
    q_t_b = q_t.astype(BF16)
    gate_all = _dot(km_hi, q_t_b) + _dot(km_lo, q_t_b)

    mb = q.shape[0]
    blk = lax.broadcasted_iota(jnp.int32, (nblk, mb), 0)
    lane_pos = lax.broadcasted_iota(jnp.int32, (16, mb), 1).astype(F32)
    row16 = lax.broadcasted_iota(jnp.int32, (16, mb), 0)
    key_pos = lax.broadcasted_iota(jnp.int32, (mb, AUG), 0).astype(F32)
    kcol = lax.broadcasted_iota(jnp.int32, (mb, AUG), 1)
    sel_r = lax.broadcasted_iota(jnp.int32, (width, AUG), 0)
    sel_c = lax.broadcasted_iota(jnp.int32, (width, AUG), 1)

    for h in range(A_HEADS):
        slope = slopes_ref[h]
        g = jnp.where(blk < i, gate_all[h * nblk:(h + 1) * nblk, :], NEG_INF)
        sel = jnp.zeros((nblk, mb), jnp.bool_)
        for r in range(MOBA_TOPK):
            m = jnp.max(g, axis=0, keepdims=True)
            idx = jnp.min(jnp.where(g == m, blk, nblk), axis=0, keepdims=True)
            hit = blk == idx
            sel = jnp.logical_or(sel, jnp.logical_and(hit, r < i))
            g = jnp.where(hit, KNOCKED_OUT, g)
        bias_t = jnp.where(sel, 0.0, NEG_INF)

        scale = A_HEAD_DIM ** -0.5
        qo_ref[0, h, 0:A_HEAD_DIM, :] = (q_t[h * A_HEAD_DIM:(h + 1) * A_HEAD_DIM, :] * scale).astype(BF16)
        qo_ref[0, h, FEAT_BIAS:FEAT_BIAS + nblk, :] = bias_t.astype(BF16)
        if nblk < 32:
            qo_ref[0, h, FEAT_BIAS + nblk:FEAT_POS, :] = jnp.zeros((32 - nblk, mb), BF16)
        blk_off = slope * (i * mb).astype(F32)
        pos_feat = jnp.where(row16 == 0, -slope * lane_pos,
                             jnp.where(row16 == 2, -blk_off,
                                       jnp.where(jnp.logical_or(row16 == 1, row16 == 3), 1.0, 0.0)))
        qo_ref[0, h, FEAT_POS:FEAT_POS + 16, :] = pos_feat.astype(BF16)
        qo_ref[0, h, FEAT_POS + 16:AUG, :] = jnp.zeros((AUG - FEAT_POS - 16, mb), BF16)

        vo_ref[0, h, 0:A_HEAD_DIM, :] = v_t[h * A_HEAD_DIM:(h + 1) * A_HEAD_DIM, :].astype(BF16)
        vo_ref[0, h, A_HEAD_DIM:A_HEAD_DIM + 16, :] = jnp.where(row16 == 0, 1.0, 0.0).astype(BF16)
        vo_ref[0, h, A_HEAD_DIM + 16:AUG, :] = jnp.zeros((AUG - A_HEAD_DIM - 16, mb), BF16)

        pick = jnp.where(jnp.logical_and(sel_r == sel_c + h * A_HEAD_DIM, sel_c < A_HEAD_DIM),
                         1.0, 0.0).astype(BF16)
        k_feat = jnp.where(
            jnp.logical_or(kcol == FEAT_BIAS + i, jnp.logical_or(kcol == FEAT_POS, kcol == FEAT_POS + 2)), 1.0,
            jnp.where(kcol == FEAT_POS + 1, slope * key_pos, jnp.where(kcol == FEAT_POS + 3, blk_off, 0.0)))
        ko_ref[0, h, :, :] = (_dot(k, pick) + k_feat).astype(BF16)


def _moba_prep(proj3, slopes):
    bsz, seq, _ = proj3.shape
    nblk = seq // MOBA_BLOCK
    mb = MOBA_BLOCK
    grid_spec = pltpu.PrefetchScalarGridSpec(
        num_scalar_prefetch=1,
        grid=(bsz, nblk),
        in_specs=[pl.BlockSpec((None, mb, A_WIDTH), lambda b, i, s: (b, i, COL_QA // A_WIDTH)),
                  pl.BlockSpec((None, mb, A_WIDTH), lambda b, i, s: (b, i, COL_KA // A_WIDTH)),
                  pl.BlockSpec((None, mb, A_WIDTH), lambda b, i, s: (b, i, COL_VA // A_WIDTH))],
        out_specs=[pl.BlockSpec((1, A_HEADS, mb, AUG), lambda b, i, s: (b, 0, i, 0)),
                   pl.BlockSpec((1, A_HEADS, AUG, mb), lambda b, i, s: (b, 0, 0, i)),
                   pl.BlockSpec((1, A_HEADS, AUG, mb), lambda b, i, s: (b, 0, 0, i))],
        scratch_shapes=[pltpu.VMEM((nblk, A_WIDTH), F32)],
    )
    return pl.pallas_call(
        _moba_prep_kernel,
        grid_spec=grid_spec,
        out_shape=[jax.ShapeDtypeStruct((bsz, A_HEADS, seq, AUG), BF16),
                   jax.ShapeDtypeStruct((bsz, A_HEADS, AUG, seq), BF16),
                   jax.ShapeDtypeStruct((bsz, A_HEADS, AUG, seq), BF16)],
        compiler_params=_cparams(("parallel", "arbitrary")),
        name="moba_prep",
    )(slopes, proj3, proj3, proj3)


def _moba_attn_kernel(q_ref, k_ref, v_ref, o_ref, *, group):
    i = pl.program_id(2)
    mb = MOBA_BLOCK
    span = group * mb
    own = pl.multiple_of(i * mb, mb)
    key_i = lax.broadcasted_iota(jnp.int32, (mb, mb), 0)
    qry_i = lax.broadcasted_iota(jnp.int32, (mb, mb), 1)
    feat = lax.broadcasted_iota(jnp.int32, (AUG, mb), 0)
    is_bias = jnp.logical_and(feat >= FEAT_BIAS, feat < FEAT_POS)
    q_ts, carry0 = [], []
    for hh in range(2):
        q_t = q_ref[0, hh]
        q_ts.append(q_t)
        q_own = jnp.where(is_bias, jnp.zeros_like(q_t), q_t)
        s = _dot(k_ref[0, hh, pl.ds(own, mb), :], q_own)
        s = jnp.where(key_i <= qry_i, s, NEG_INF)
        m0 = jnp.max(s, axis=0, keepdims=True)
        p = jnp.exp(s - m0)
        carry0 += [m0, _dot(v_ref[0, hh, :, pl.ds(own, mb)], p.astype(BF16))]

    def body(g, carry):
        start = pl.multiple_of(g * span, span)
        new = []
        for hh in range(2):
            m, acc = carry[2 * hh], carry[2 * hh + 1]
            sb = _dot(k_ref[0, hh, pl.ds(start, span), :], q_ts[hh])
            m_new = jnp.maximum(m, jnp.max(sb, axis=0, keepdims=True))
            pb = jnp.exp(sb - m_new)
            alpha = jnp.exp(m - m_new)
            acc = acc * alpha + _dot(v_ref[0, hh, :, pl.ds(start, span)], pb.astype(BF16))
            new += [m_new, acc]
        return tuple(new)

    res = lax.fori_loop(0, (i + group - 1) // group, body, tuple(carry0))
    outs = [res[2 * hh + 1][0:A_HEAD_DIM, :] / res[2 * hh + 1][A_HEAD_DIM:A_HEAD_DIM + 1, :] for hh in range(2)]
    o_t = jnp.concatenate(outs, axis=0).astype(BF16)
    eye = (key_i == qry_i).astype(BF16)
    o_ref[0] = _dot_nt(eye, o_t).astype(BF16)


def _moba_attn(k_aug, q_aug_t, v_aug_t):
    bsz, nh, seq, _ = k_aug.shape
    mb = MOBA_BLOCK
    group = min(4, seq // mb)
    return pl.pallas_call(
        functools.partial(_moba_attn_kernel, group=group),
        grid=(bsz, nh // 2, seq // mb),
        in_specs=[pl.BlockSpec((1, 2, AUG, mb), lambda b, h, i: (b, h, 0, i)),
                  pl.BlockSpec((1, 2, seq, AUG), lambda b, h, i: (b, h, 0, 0)),
                  pl.BlockSpec((1, 2, AUG, seq), lambda b, h, i: (b, h, 0, 0))],
        out_specs=pl.BlockSpec((1, mb, 2 * A_HEAD_DIM), lambda b, h, i: (b, i, h)),
        out_shape=jax.ShapeDtypeStruct((bsz, seq, A_WIDTH), BF16),
        compiler_params=_cparams(("parallel", "parallel", "arbitrary")),
        name="moba_attn",
    )(q_aug_t, k_aug, v_aug_t)


def _retention_kernel(cdec_ref, q_ref, k_ref, v_ref, g_ref, decay_ref, qdec_ref, kdec_ref, o_ref, state_scr):
    @pl.when(pl.program_id(1) == 0)
    def _():
        state_scr[...] = jnp.zeros_like(state_scr)

    q = q_ref[...]
    k = k_ref[...]
    width = q.shape[1]
    eye = (lax.broadcasted_iota(jnp.int32, (width, width), 0)
           == lax.broadcasted_iota(jnp.int32, (width, width), 1)).astype(BF16)
    k_t = _dot_nt(eye, k)
    k_t_b = k_t.astype(BF16)
    head_of_col = lax.broadcasted_iota(jnp.int32, q.shape, 1) // R_QK_DIM
    state_b = state_scr[...].astype(BF16)
    for h in range(R_HEADS):
        rows = slice(h * R_QK_DIM, (h + 1) * R_QK_DIM)
        cols = slice(h * R_V_DIM, (h + 1) * R_V_DIM)
        q_m = jnp.where(head_of_col == h, q, jnp.zeros_like(q))
        v_h = v_ref[:, cols]
        inner = _dot(q_m, k_t_b) * decay_ref[h]
        out = _dot(inner.astype(BF16), v_h) + _dot(q_m, state_b) * qdec_ref[h]
        k_dec = (k_t[rows, :] * kdec_ref[h]).astype(BF16)
        state_scr[rows, :] = cdec_ref[h] * state_scr[rows, :] + _dot(k_dec, v_h)
        mu = jnp.mean(out, axis=-1, keepdims=True)
        cen = out - mu
        var = jnp.mean(cen * cen, axis=-1, keepdims=True)
        y = cen * lax.rsqrt(var + GN_EPS)
        o_ref[:, cols] = (y * _silu(g_ref[:, cols].astype(F32))).astype(BF16)


def _retention_consts():
    h = np.arange(R_HEADS, dtype=np.float64)
    log_g = np.log(1.0 - np.exp2(-5.0 - h))
    n = np.arange(R_CHUNK, dtype=np.float64)
    diff = n[:, None] - n[None, :]
    scale = R_QK_DIM ** -0.5
    decay = np.where(diff >= 0, np.exp(np.maximum(diff, 0.0) * log_g[:, None, None]), 0.0) * scale
    q_decay = np.exp((n + 1.0) * log_g[:, None])[:, :, None]
    k_decay = np.exp((R_CHUNK - 1.0 - n) * log_g[:, None])[:, None, :] * scale
    chunk_decay = np.exp(R_CHUNK * log_g)
    return (jnp.asarray(decay, F32), jnp.asarray(q_decay, F32), jnp.asarray(k_decay, F32),
            jnp.asarray(chunk_decay, F32))


def _retention(proj3):
    bsz, seq, _ = proj3.shape
    c = R_CHUNK
    decay, qdec, kdec, cdec = _retention_consts()
    grid_spec = pltpu.PrefetchScalarGridSpec(
        num_scalar_prefetch=1,
        grid=(bsz, seq // c),
        in_specs=[pl.BlockSpec((None, c, R_QK_WIDTH), lambda b, i, s: (b, i, COL_QR // R_QK_WIDTH)),
                  pl.BlockSpec((None, c, R_QK_WIDTH), lambda b, i, s: (b, i, COL_KR // R_QK_WIDTH)),
                  pl.BlockSpec((None, c, R_V_WIDTH), lambda b, i, s: (b, i, COL_VR // R_V_WIDTH)),
                  pl.BlockSpec((None, c, R_V_WIDTH), lambda b, i, s: (b, i, COL_GR // R_V_WIDTH)),
                  pl.BlockSpec((R_HEADS, c, c), lambda b, i, s: (0, 0, 0)),
                  pl.BlockSpec((R_HEADS, c, 1), lambda b, i, s: (0, 0, 0)),
                  pl.BlockSpec((R_HEADS, 1, c), lambda b, i, s: (0, 0, 0))],
        out_specs=pl.BlockSpec((None, c, R_V_WIDTH), lambda b, i, s: (b, i, 0)),
        scratch_shapes=[pltpu.VMEM((R_QK_WIDTH, R_V_DIM), F32)],
    )
    return pl.pallas_call(
        _retention_kernel,
        grid_spec=grid_spec,
        out_shape=jax.ShapeDtypeStruct((bsz, seq, R_V_WIDTH), BF16),
        compiler_params=_cparams(("parallel", "arbitrary")),
        name="retention",
    )(cdec, proj3, proj3, proj3, proj3, decay, qdec, kdec)


def _pack_halves(x):
    w = x.shape[1] // 2
    bits = lax.bitcast_convert_type(x.astype(BF16).astype(F32), jnp.uint32)
    return (bits[:, :w] >> 16) | (bits[:, w:] & jnp.uint32(0xFFFF0000))


def _unpack_halves(p):
    lo = lax.bitcast_convert_type(p << 16, F32)
    hi = lax.bitcast_convert_type(p & jnp.uint32(0xFFFF0000), F32)
    return lo, hi


def _mix_kernel(ya_ref, yr_ref, ga_ref, gt_ref, x_ref, wpa_ref, wpr_ref, wout_ref,
                gt1_ref, g_ref, sc_ref, sh_ref, x1_ref, h2_ref, h2p_ref):
    a = _dot(ya_ref[...], wpa_ref[...]) * _sigmoid(ga_ref[...].astype(F32))
    r = _dot(yr_ref[...], wpr_ref[...]) * _sigmoid(gt_ref[...].astype(F32))
    mix = _dot((a + r).astype(BF16), wout_ref[...])
    x1 = x_ref[...] + gt1_ref[0] * mix
    x1_ref[...] = x1
    ms = jnp.mean(x1 * x1, axis=-1, keepdims=True)
    y = x1 * lax.rsqrt(ms + NORM_EPS) * g_ref[...]
    h2 = y * (1.0 + sc_ref[0]) + sh_ref[0]
    h2_ref[...] = h2.astype(BF16)
    h2p_ref[...] = _pack_halves(h2)


def _mix(ya, yr, proj, x2d, wpa, wpr, wout, gt1, g, sc, sh, seq):
    t, d = x2d.shape
    tm = min(512, seq)
    per_b = seq // tm
    row = lambda i: (i, 0)
    full = lambda i: (0, 0)
    per_batch = lambda i: (i // per_b, 0, 0)
    return pl.pallas_call(
        _mix_kernel,
        grid=(t // tm,),
        in_specs=[pl.BlockSpec((tm, A_WIDTH), row),
                  pl.BlockSpec((tm, R_V_WIDTH), row),
                  pl.BlockSpec((tm, d), lambda i: (i, COL_GA // D_MODEL)),
                  pl.BlockSpec((tm, d), lambda i: (i, COL_GT // D_MODEL)),
                  pl.BlockSpec((tm, d), row),
                  pl.BlockSpec((A_WIDTH, d), full),
                  pl.BlockSpec((R_V_WIDTH, d), full),
                  pl.BlockSpec((d, d), full),
                  pl.BlockSpec((1, 1, d), per_batch),
                  pl.BlockSpec((1, d), full),
                  pl.BlockSpec((1, 1, d), per_batch),
                  pl.BlockSpec((1, 1, d), per_batch)],
        out_specs=[pl.BlockSpec((tm, d), row), pl.BlockSpec((tm, d), row), pl.BlockSpec((tm, d // 2), row)],
        out_shape=[jax.ShapeDtypeStruct((t, d), F32), jax.ShapeDtypeStruct((t, d), BF16),
                   jax.ShapeDtypeStruct((t, d // 2), jnp.uint32)],
        compiler_params=_cparams(("parallel",)),
        name="merge_outproj_norm",
    )(ya, yr, proj, proj, x2d, wpa, wpr, wout, gt1, g, sc, sh)


def _router_kernel(h_ref, wr_ref, b_ref, e_ref, w_ref, r_ref, c_ref):
    logits = _dot_nt(wr_ref[...], h_ref[...])
    scores = _sigmoid(logits)
    choice = scores + b_ref[...]
    tm = logits.shape[1]
    giota = lax.broadcasted_iota(jnp.int32, (GROUP_SIZE, tm), 0)
    gs_rows = []
    for g in range(N_GROUPS):
        cg = choice[g * GROUP_SIZE:(g + 1) * GROUP_SIZE, :]
        m1 = jnp.max(cg, axis=0, keepdims=True)
        i1 = jnp.min(jnp.where(cg == m1, giota, GROUP_SIZE), axis=0, keepdims=True)
        m2 = jnp.max(jnp.where(giota == i1, KNOCKED_OUT, cg), axis=0, keepdims=True)
        gs_rows.append(m1 + m2)
    gs = jnp.concatenate(gs_rows, axis=0)
    grow = lax.broadcasted_iota(jnp.int32, (N_GROUPS, tm), 0)
    gmask = jnp.zeros((N_GROUPS, tm), jnp.bool_)
    for _ in range(TOPK_GROUPS):
        mx = jnp.max(gs, axis=0, keepdims=True)
        ix = jnp.min(jnp.where(gs == mx, grow, N_GROUPS), axis=0, keepdims=True)
        hit = grow == ix
        gmask = jnp.logical_or(gmask, hit)
        gs = jnp.where(hit, KNOCKED_OUT, gs)
    gmask_f = jnp.where(gmask, 1.0, 0.0)
    masked = jnp.concatenate(
        [jnp.where(gmask_f[g:g + 1, :] > 0.5, choice[g * GROUP_SIZE:(g + 1) * GROUP_SIZE, :], NEG_INF)
         for g in range(N_GROUPS)], axis=0)
    erow = lax.broadcasted_iota(jnp.int32, (N_EXPERTS, tm), 0)
    idx_rows, w_rows = [], []
    chosen = jnp.zeros((N_EXPERTS, tm), F32)
    for _ in range(TOP_K):
        mx = jnp.max(masked, axis=0, keepdims=True)
        ix = jnp.min(jnp.where(masked == mx, erow, N_EXPERTS), axis=0, keepdims=True)
        hit = erow == ix
        w_rows.append(jnp.sum(jnp.where(hit, scores, 0.0), axis=0, keepdims=True))
        idx_rows.append(ix)
        chosen = jnp.where(hit, 1.0, chosen)
        masked = jnp.where(hit, KNOCKED_OUT, masked)
    w = jnp.concatenate(w_rows, axis=0)
    w = w / (jnp.sum(w, axis=0, keepdims=True) + 1e-20) * ROUTED_SCALE
    e_ref[...] = jnp.concatenate(idx_rows, axis=0)
    w_ref[...] = w
    chosen_b = chosen.astype(BF16)
    earlier = (lax.broadcasted_iota(jnp.int32, (tm, tm), 0)
               < lax.broadcasted_iota(jnp.int32, (tm, tm), 1)).astype(BF16)
    before = _dot(chosen_b, earlier)
    ranks = [jnp.sum(jnp.where(erow == ix, before, 0.0), axis=0, keepdims=True) for ix in idx_rows]
    r_ref[...] = jnp.concatenate(ranks, axis=0).astype(jnp.int32)
    c_ref[...] = _dot(chosen_b, jnp.ones((tm, 128), BF16))


ROUTER_ROWS = 512


def _router(h2, wr_t, bias_col):
    t, d = h2.shape
    tm = ROUTER_ROWS
    by_tile = lambda i: (0, i)
    return pl.pallas_call(
        _router_kernel,
        grid=(t // tm,),
        in_specs=[pl.BlockSpec((tm, d), lambda i: (i, 0)),
                  pl.BlockSpec((N_EXPERTS, d), lambda i: (0, 0)),
                  pl.BlockSpec((N_EXPERTS, 1), lambda i: (0, 0))],
        out_specs=[pl.BlockSpec((TOP_K, tm), by_tile), pl.BlockSpec((TOP_K, tm), by_tile),
                   pl.BlockSpec((TOP_K, tm), by_tile), pl.BlockSpec((N_EXPERTS, 128), by_tile)],
        out_shape=[jax.ShapeDtypeStruct((TOP_K, t), jnp.int32),
                   jax.ShapeDtypeStruct((TOP_K, t), F32),
                   jax.ShapeDtypeStruct((TOP_K, t), jnp.int32),
                   jax.ShapeDtypeStruct((N_EXPERTS, (t // tm) * 128), F32)],
        compiler_params=_cparams(("parallel",)),
        name="router_topk",
    )(h2, wr_t, bias_col)


def _pos_kernel(e_ref, r_ref, base_ref, p_ref):
    tm = e_ref.shape[1]
    erow = lax.broadcasted_iota(jnp.int32, (N_EXPERTS, tm), 0)
    base = base_ref[0]
    rows = [jnp.sum(jnp.where(erow == e_ref[k:k + 1, :], base, 0.0), axis=0, keepdims=True)
            for k in range(TOP_K)]
    p_ref[0] = jnp.concatenate(rows, axis=0).astype(jnp.int32) + r_ref[...]


MOVE_ROWS = 256


def _positions(eidx_t, rank_t, tile_base):
    t = eidx_t.shape[1]
    tm = min(MOVE_ROWS, t)
    per_router_tile = ROUTER_ROWS // tm
    return pl.pallas_call(
        _pos_kernel,
        grid=(t // tm,),
        in_specs=[pl.BlockSpec((TOP_K, tm), lambda i: (0, i)),
                  pl.BlockSpec((TOP_K, tm), lambda i: (0, i)),
                  pl.BlockSpec((1, N_EXPERTS, 1), lambda i: (i // per_router_tile, 0, 0))],
        out_specs=pl.BlockSpec((1, TOP_K, tm), lambda i: (i, 0, 0)),
        out_shape=jax.ShapeDtypeStruct((t // tm, TOP_K, tm), jnp.int32),
        compiler_params=_cparams(("parallel",)),
        name="slot_positions",
    )(eidx_t, rank_t, tile_base)


SUBLANES = 8
PAD_CHUNKS = (64, 32, 16, 8)


def _dispatch_kernel(pad_start_ref, pad_len_ref, pos_hbm, h_ref, xs_hbm, pos_smem, zero_buf,
                     idx_sem, row_sem, pad_sem):
    i = pl.program_id(0)
    tm = h_ref.shape[0]
    idx_copy = pltpu.make_async_copy(pos_hbm.at[i], pos_smem, idx_sem)
    idx_copy.start()

    @pl.when(i == 0)
    def _():
        zero_buf[...] = jnp.zeros_like(zero_buf)

        def pad_copies(e, wait):
            start = pad_start_ref[e]
            n = pad_len_ref[e]
            head = jnp.minimum((-start) & (SUBLANES - 1), n)

            def fill(first, size, pred):
                @pl.when(pred)
                def _():
                    cp = pltpu.make_async_copy(zero_buf.at[pl.ds(0, size), :],
                                               xs_hbm.at[pl.ds(first, size), :], pad_sem)
                    if wait:
                        cp.wait()
                    else:
                        cp.start()

            for j in range(SUBLANES - 1):
                fill(start + j, 1, j < head)
            ptr = start + head
            rest = n - head
            for chunk in PAD_CHUNKS:
                fill(pl.multiple_of(ptr, SUBLANES), chunk, (rest & chunk) != 0)
                ptr = ptr + (rest & chunk)

        def issue(e, carry):
            pad_copies(e, False)
            return carry

        def drain(e, carry):
            pad_copies(e, True)
            return carry

        lax.fori_loop(0, N_EXPERTS, issue, 0)
        lax.fori_loop(0, N_EXPERTS, drain, 0)

    idx_copy.wait()

    def row_copy(k, t):
        return pltpu.make_async_copy(h_ref.at[pl.ds(t, 1), :], xs_hbm.at[pl.ds(pos_smem[k, t], 1), :], row_sem)

    def issue_rows(t, carry):
        for k in range(TOP_K):
            row_copy(k, t).start()
        return carry

    def drain_rows(t, carry):
        for k in range(TOP_K):
            row_copy(k, t).wait()
        return carry

    lax.fori_loop(0, tm, issue_rows, 0, unroll=4)
    lax.fori_loop(0, tm, drain_rows, 0, unroll=4)


def _dispatch(pad_start, pad_len, pos3, h2p, n_rows):
    t, half = h2p.shape
    tm = pos3.shape[2]
    grid_spec = pltpu.PrefetchScalarGridSpec(
        num_scalar_prefetch=2,
        grid=(t // tm,),
        in_specs=[pl.BlockSpec(memory_space=pl.ANY),
                  pl.BlockSpec((tm, half), lambda i, ps, pn: (i, 0))],
        out_specs=pl.BlockSpec(memory_space=pl.ANY),
        scratch_shapes=[pltpu.SMEM((TOP_K, tm), jnp.int32),
                        pltpu.VMEM((PAD_CHUNKS[0], half), jnp.uint32),
                        pltpu.SemaphoreType.DMA,
                        pltpu.SemaphoreType.DMA,
                        pltpu.SemaphoreType.DMA],
    )
    return pl.pallas_call(
        _dispatch_kernel,
        grid_spec=grid_spec,
        out_shape=jax.ShapeDtypeStruct((n_rows, half), jnp.uint32),
        compiler_params=_cparams(("arbitrary",)),
        name="dispatch_rows",
    )(pad_start, pad_len, pos3, h2p)


def _experts_kernel(blk_e_ref, nblk_ref, x_ref, w1_ref, w3_ref, w2_ref, y_ref):
    @pl.when(pl.program_id(0) < nblk_ref[0])
    def _():
        half = x_ref.shape[1]
        lo, hi = _unpack_halves(x_ref[...])
        lo = lo.astype(BF16)
        hi = hi.astype(BF16)
        w1 = w1_ref[0].astype(BF16)
        w3 = w3_ref[0].astype(BF16)
        h1 = _dot(lo, w1[:half]) + _dot(hi, w1[half:])
        h3 = _dot(lo, w3[:half]) + _dot(hi, w3[half:])
        mid = (_silu(h1) * h3).astype(BF16)
        y_ref[...] = _pack_halves(_dot(mid, w2_ref[0].astype(BF16)))


def _experts(blk_e, nblk_used, xs, w1, w3, w2):
    n_rows, half = xs.shape
    d = 2 * half
    blk = lambda s, be, nb: (jnp.minimum(s, nb[0] - 1), 0)
    wblk = lambda s, be, nb: (be[jnp.minimum(s, nb[0] - 1)], 0, 0)
    grid_spec = pltpu.PrefetchScalarGridSpec(
        num_scalar_prefetch=2,
        grid=(n_rows // EXPERT_ROWS,),
        in_specs=[pl.BlockSpec((EXPERT_ROWS, half), blk),
                  pl.BlockSpec((1, d, EXPERT_FF), wblk),
                  pl.BlockSpec((1, d, EXPERT_FF), wblk),
                  pl.BlockSpec((1, EXPERT_FF, d), wblk)],
        out_specs=pl.BlockSpec((EXPERT_ROWS, half), blk),
    )
    return pl.pallas_call(
        _experts_kernel,
        grid_spec=grid_spec,
        out_shape=jax.ShapeDtypeStruct((n_rows, half), jnp.uint32),
        compiler_params=_cparams(("arbitrary",)),
        name="routed_experts",
    )(blk_e, nblk_used, xs, w1, w3, w2)


def _combine_kernel(pos_hbm, ys_hbm, w_ref, h_ref, x1_ref, ws1_ref, ws3_ref, ws2_ref, gt2_ref, g_ref, o_ref,
                    pos_smem, ybuf, idx_sem, row_sem):
    i = pl.program_id(0)
    tm, d = x1_ref.shape
    half = d // 2
    idx_copy = pltpu.make_async_copy(pos_hbm.at[i], pos_smem, idx_sem)
    idx_copy.start()
    idx_copy.wait()

    def row_copy(k, t):
        return pltpu.make_async_copy(ys_hbm.at[pl.ds(pos_smem[k, t], 1), :], ybuf.at[k, pl.ds(t, 1), :], row_sem)

    def issue_rows(t, carry):
        for k in range(TOP_K):
            row_copy(k, t).start()
        return carry

    def drain_rows(t, carry):
        for k in range(TOP_K):
            row_copy(k, t).wait()
        return carry

    lax.fori_loop(0, tm, issue_rows, 0, unroll=4)

    h = h_ref[...]
    mid = (_silu(_dot(h, ws1_ref[...])) * _dot(h, ws3_ref[...])).astype(BF16)
    shared = _dot(mid, ws2_ref[...])

    lax.fori_loop(0, tm, drain_rows, 0, unroll=4)

    w = w_ref[...]
    acc_lo = jnp.zeros((tm, half), F32)
    acc_hi = jnp.zeros((tm, half), F32)
    for k in range(TOP_K):
        lo, hi = _unpack_halves(ybuf[k])
        acc_lo = acc_lo + lo * w[:, k:k + 1]
        acc_hi = acc_hi + hi * w[:, k:k + 1]
    gt2 = gt2_ref[0]
    g = g_ref[...]
    x_lo = x1_ref[:, :half] + gt2[:, :half] * (acc_lo + shared[:, :half])
    x_hi = x1_ref[:, half:] + gt2[:, half:] * (acc_hi + shared[:, half:])
    ms = (jnp.sum(x_lo * x_lo, axis=-1, keepdims=True) + jnp.sum(x_hi * x_hi, axis=-1, keepdims=True)) / d
    inv = lax.rsqrt(ms + NORM_EPS)
    o_ref[:, :half] = x_lo * inv * g[:, :half]
    o_ref[:, half:] = x_hi * inv * g[:, half:]


def _combine(pos3, ys, wts, h2, x1, ws1, ws3, ws2, gt2, g_final, seq):
    t, d = x1.shape
    tm = pos3.shape[2]
    per_b = seq // tm
    row = lambda i: (i, 0)
    full = lambda i: (0, 0)
    return pl.pallas_call(
        _combine_kernel,
        grid=(t // tm,),
        in_specs=[pl.BlockSpec(memory_space=pl.ANY),
                  pl.BlockSpec(memory_space=pl.ANY),
                  pl.BlockSpec((tm, TOP_K), row),
                  pl.BlockSpec((tm, d), row),
                  pl.BlockSpec((tm, d), row),
                  pl.BlockSpec((d, SHARED_FF), full),
                  pl.BlockSpec((d, SHARED_FF), full),
                  pl.BlockSpec((SHARED_FF, d), full),
                  pl.BlockSpec((1, 1, d), lambda i: (i // per_b, 0, 0)),
                  pl.BlockSpec((1, d), full)],
        out_specs=pl.BlockSpec((tm, d), row),
        out_shape=jax.ShapeDtypeStruct((t, d), F32),
        scratch_shapes=[pltpu.SMEM((TOP_K, tm), jnp.int32),
                        pltpu.VMEM((TOP_K, tm, d // 2), jnp.uint32),
                        pltpu.SemaphoreType.DMA,
                        pltpu.SemaphoreType.DMA],
        compiler_params=_cparams(("arbitrary",)),
        name="combine_shared_final",
    )(pos3, ys, wts, h2, x1, ws1, ws3, ws2, gt2, g_final)


def _slot_tables(cnt, t):
    ntiles = cnt.shape[1] // 128
    cnt_tile = cnt.reshape(N_EXPERTS, ntiles, 128)[:, :, 0].astype(jnp.int32)
    counts = jnp.sum(cnt_tile, axis=1)
    padded = (counts + EXPERT_ROWS - 1) // EXPERT_ROWS * EXPERT_ROWS
    pstart = jnp.cumsum(padded) - padded
    tile_base = pstart[:, None] + jnp.cumsum(cnt_tile, axis=1) - cnt_tile
    n_rblk = -(-(t * TOP_K) // EXPERT_ROWS) + N_EXPERTS
    blk_end = jnp.cumsum(padded // EXPERT_ROWS)
    blk_e = jnp.sum((blk_end[None, :] <= jnp.arange(n_rblk)[:, None]).astype(jnp.int32), axis=1)
    blk_e = jnp.minimum(blk_e, N_EXPERTS - 1)
    return (blk_e, blk_end[-1:].astype(jnp.int32), pstart + counts, padded - counts,
            tile_base.T.astype(F32).reshape(ntiles, N_EXPERTS, 1), n_rblk * EXPERT_ROWS)


def _permute_in_cols(w_in):
    qa, ka, va, qr, kr, vr, gr, ga, gt = jnp.split(
        w_in, np.cumsum((A_WIDTH, A_WIDTH, A_WIDTH, R_QK_WIDTH, R_QK_WIDTH, R_V_WIDTH, R_V_WIDTH,
                         D_MODEL))[:].tolist(), axis=1)
    return jnp.concatenate([vr, gr, ga, gt, qa, ka, va, qr, kr], axis=1)


def kernel(x, c, w_ada, b_ada, g_mix, w_in, w_pa, w_pr, w_out, g_ffn, w_router, router_bias,
           w1, w3, w2, ws1, ws3, ws2, g_final):
    bsz, seq, d = x.shape
    t = bsz * seq
    depth = w_ada.shape[0]
    assert depth == 1, "the final norm is fused into the single layer's last kernel"
    slopes = jnp.exp2(-8.0 / A_HEADS * jnp.arange(1, A_HEADS + 1, dtype=F32))
    x2d = x.reshape(t, d)
    for l in range(depth):
        mod = _ada(c, w_ada[l], b_ada[l])
        sh1, sc1, gt1, sh2, sc2, gt2 = [m.reshape(bsz, 1, d) for m in jnp.split(mod, 6, axis=-1)]
        w_in_p = _permute_in_cols(w_in[l]).astype(BF16)
        proj = _inproj(x2d, g_mix[l].reshape(1, d), sc1, sh1, w_in_p, seq)
        proj3 = proj.reshape(bsz, seq, IN_COLS)
        k_aug, q_aug_t, v_aug_t = _moba_prep(proj3, slopes)
        ya = _moba_attn(k_aug, q_aug_t, v_aug_t).reshape(t, A_WIDTH)
        yr = _retention(proj3).reshape(t, R_V_WIDTH)
        x1, h2, h2p = _mix(ya, yr, proj, x2d, w_pa[l].astype(BF16), w_pr[l].astype(BF16),
                           w_out[l].astype(BF16), gt1, g_ffn[l].reshape(1, d), sc2, sh2, seq)
        eidx_t, wts_t, rank_t, cnt = _router(h2, w_router[l].T.astype(BF16),
                                             router_bias[l].reshape(N_EXPERTS, 1))
        blk_e, nblk_used, pad_start, pad_len, tile_base, n_rows = _slot_tables(cnt, t)
        pos3 = _positions(eidx_t, rank_t, tile_base)
        xs = _dispatch(pad_start, pad_len, pos3, h2p, n_rows)
        ys = _experts(blk_e, nblk_used, xs, w1[l], w3[l], w2[l])
        x2d = _combine(pos3, ys, wts_t.T, h2, x1, ws1[l].astype(BF16), ws3[l].astype(BF16),
                       ws2[l].astype(BF16), gt2, g_final.reshape(1, d), seq)
    return x2d.reshape(bsz, seq, d)
```

```python
import functools

import jax
import jax.numpy as jnp
import numpy as np
from jax import lax
from jax.experimental import pallas as pl
from jax.experimental.pallas import tpu as pltpu

F32 = jnp.float32
BF16 = jnp.bfloat16

D_MODEL = 1024
A_HEADS = 8
A_HEAD_DIM = 64
A_WIDTH = A_HEADS * A_HEAD_DIM
MOBA_BLOCK = 256
MOBA_TOPK = 3
R_HEADS = 8
R_QK_DIM = 64
R_V_DIM = 128
R_QK_WIDTH = R_HEADS * R_QK_DIM
R_V_WIDTH = R_HEADS * R_V_DIM
R_CHUNK = 128
N_EXPERTS = 256
TOP_K = 8
N_GROUPS = 8
GROUP_SIZE = N_EXPERTS // N_GROUPS
TOPK_GROUPS = 4
EXPERT_FF = 256
SHARED_FF = 256
ROUTED_SCALE = 2.5
EXPERT_ROWS = 128
NORM_EPS = 1e-6
GN_EPS = 1e-6
NEG_INF = -1e30
KNOCKED_OUT = -3e38

COL_VR, COL_GR, COL_GA, COL_GT = 0, 1024, 2048, 3072
COL_QA, COL_KA, COL_VA, COL_QR, COL_KR = 4096, 4608, 5120, 5632, 6144
IN_COLS = 6656
AUG = 128
FEAT_BIAS = A_HEAD_DIM
FEAT_POS = A_HEAD_DIM + 32

VMEM_LIMIT = 56 * 1024 * 1024


def _cparams(sem, vmem=VMEM_LIMIT):
    return pltpu.CompilerParams(dimension_semantics=sem, vmem_limit_bytes=vmem)


def _dot(a, b):
    return jnp.dot(a, b, preferred_element_type=F32)


def _dot_nt(a, b):
    return lax.dot_general(a, b, (((1,), (1,)), ((), ())), preferred_element_type=F32)


def _sigmoid(x):
    return 1.0 / (1.0 + jnp.exp(-x))


def _silu(x):
    return x * _sigmoid(x)


def _ada_kernel(c_ref, w_ref, b_ref, o_ref):
    c = c_ref[...]
    s = _silu(c)
    s_hi = s.astype(BF16)
    s_lo = (s - s_hi.astype(F32)).astype(BF16)
    w = w_ref[...]
    w_hi = w.astype(BF16)
    w_lo = (w - w_hi.astype(F32)).astype(BF16)
    o_ref[...] = _dot(s_hi, w_hi) + _dot(s_hi, w_lo) + _dot(s_lo, w_hi) + b_ref[...]


def _ada(c, w_ada, b_ada):
    bsz, d = c.shape
    n = w_ada.shape[1]
    tn = 1024
    return pl.pallas_call(
        _ada_kernel,
        grid=(n // tn,),
        in_specs=[pl.BlockSpec((bsz, d), lambda j: (0, 0)),
                  pl.BlockSpec((d, tn), lambda j: (0, j)),
                  pl.BlockSpec((1, tn), lambda j: (0, j))],
        out_specs=pl.BlockSpec((bsz, tn), lambda j: (0, j)),
        out_shape=jax.ShapeDtypeStruct((bsz, n), F32),
        compiler_params=_cparams(("parallel",)),
        name="ada_mod",
    )(c, w_ada, b_ada.reshape(1, n))


def _inproj_kernel(x_ref, g_ref, sc_ref, sh_ref, w_ref, o_ref, h_scr):
    @pl.when(pl.program_id(1) == 0)
    def _():
        x = x_ref[...]
        ms = jnp.mean(x * x, axis=-1, keepdims=True)
        y = x * lax.rsqrt(ms + NORM_EPS) * g_ref[...]
        h_scr[...] = (y * (1.0 + sc_ref[0]) + sh_ref[0]).astype(BF16)

    o_ref[...] = _dot(h_scr[...], w_ref[...]).astype(BF16)


def _inproj(x2d, g, sc, sh, w_bf16, seq):
    t, d = x2d.shape
    n = w_bf16.shape[1]
    tm = min(1024, seq)
    tn = 512
    per_b = seq // tm
    return pl.pallas_call(
        _inproj_kernel,
        grid=(t // tm, n // tn),
        in_specs=[pl.BlockSpec((tm, d), lambda i, j: (i, 0)),
                  pl.BlockSpec((1, d), lambda i, j: (0, 0)),
                  pl.BlockSpec((1, 1, d), lambda i, j: (i // per_b, 0, 0)),
                  pl.BlockSpec((1, 1, d), lambda i, j: (i // per_b, 0, 0)),
                  pl.BlockSpec((d, tn), lambda i, j: (0, j))],
        out_specs=pl.BlockSpec((tm, tn), lambda i, j: (i, j)),
        out_shape=jax.ShapeDtypeStruct((t, n), BF16),
        scratch_shapes=[pltpu.VMEM((tm, d), BF16)],
        compiler_params=_cparams(("parallel", "arbitrary")),
        name="norm_inproj",
    )(x2d, g, sc, sh, w_bf16)


def _moba_prep_kernel(slopes_ref, q_ref, k_ref, v_ref, ko_ref, qo_ref, vo_ref, kmean_scr):
    i = pl.program_id(1)
    nblk = kmean_scr.shape[0]
    width = q_ref.shape[1]

    @pl.when(i == 0)
    def _():
        kmean_scr[...] = jnp.zeros_like(kmean_scr)

    q = q_ref[...]
    k = k_ref[...]
    v = v_ref[...]
    kmean_scr[pl.ds(i, 1), :] = jnp.mean(k.astype(F32), axis=0, keepdims=True)

    eye = (lax.broadcasted_iota(jnp.int32, (width, width), 0)
           == lax.broadcasted_iota(jnp.int32, (width, width), 1)).astype(BF16)
    q_t = _dot_nt(eye, q)
    v_t = _dot_nt(eye, v)

    km = kmean_scr[...]
    km_rep = jnp.concatenate([km] * A_HEADS, axis=0)
    r_head = lax.broadcasted_iota(jnp.int32, km_rep.shape, 0) // nblk
    c_head = lax.broadcasted_iota(jnp.int32, km_rep.shape, 1) // A_HEAD_DIM
    km_bd = jnp.where(r_head == c_head, km_rep, 0.0)
    km_hi = km_bd.astype(BF16)
    km_lo = (km_bd - km_hi.astype(F32)).astype(BF16)
    q_t_b = q_t.astype(BF16)
    gate_all = _dot(km_hi, q_t_b) + _dot(km_lo, q_t_b)

    mb = q.shape[0]
    blk = lax.broadcasted_iota(jnp.int32, (nblk, mb), 0)
    lane_pos = lax.broadcasted_iota(jnp.int32, (16, mb), 1).astype(F32)
    row16 = lax.broadcasted_iota(jnp.int32, (16, mb), 0)
    key_pos = lax.broadcasted_iota(jnp.int32, (mb, AUG), 0).astype(F32)
    kcol = lax.broadcasted_iota(jnp.int32, (mb, AUG), 1)
    sel_r = lax.broadcasted_iota(jnp.int32, (width, AUG), 0)
    sel_c = lax.broadcasted_iota(jnp.int32, (width, AUG), 1)

    for h in range(A_HEADS):
        slope = slopes_ref[h]
        g = jnp.where(blk < i, gate_all[h * nblk:(h + 1) * nblk, :], NEG_INF)
        sel = jnp.zeros((nblk, mb), jnp.bool_)
        for r in range(MOBA_TOPK):
            m = jnp.max(g, axis=0, keepdims=True)
            idx = jnp.min(jnp.where(g == m, blk, nblk), axis=0, keepdims=True)
            hit = blk == idx
            sel = jnp.logical_or(sel, jnp.logical_and(hit, r < i))
            g = jnp.where(hit, KNOCKED_OUT, g)
        bias_t = jnp.where(sel, 0.0, NEG_INF)

        scale = A_HEAD_DIM ** -0.5
        qo_ref[0, h, 0:A_HEAD_DIM, :] = (q_t[h * A_HEAD_DIM:(h + 1) * A_HEAD_DIM, :] * scale).astype(BF16)
        qo_ref[0, h, FEAT_BIAS:FEAT_BIAS + nblk, :] = bias_t.astype(BF16)
        if nblk < 32:
            qo_ref[0, h, FEAT_BIAS + nblk:FEAT_POS, :] = jnp.zeros((32 - nblk, mb), BF16)
        blk_off = slope * (i * mb).astype(F32)
        pos_feat = jnp.where(row16 == 0, -slope * lane_pos,
                             jnp.where(row16 == 2, -blk_off,
                                       jnp.where(jnp.logical_or(row16 == 1, row16 == 3), 1.0, 0.0)))
        qo_ref[0, h, FEAT_POS:FEAT_POS + 16, :] = pos_feat.astype(BF16)
        qo_ref[0, h, FEAT_POS + 16:AUG, :] = jnp.zeros((AUG - FEAT_POS - 16, mb), BF16)

        vo_ref[0, h, 0:A_HEAD_DIM, :] = v_t[h * A_HEAD_DIM:(h + 1) * A_HEAD_DIM, :].astype(BF16)
        vo_ref[0, h, A_HEAD_DIM:A_HEAD_DIM + 16, :] = jnp.where(row16 == 0, 1.0, 0.0).astype(BF16)
        vo_ref[0, h, A_HEAD_DIM + 16:AUG, :] = jnp.zeros((AUG - A_HEAD_DIM - 16, mb), BF16)

        pick = jnp.where(jnp.logical_and(sel_r == sel_c + h * A_HEAD_DIM, sel_c < A_HEAD_DIM),
                         1.0, 0.0).astype(BF16)
        k_feat = jnp.where(
            jnp.logical_or(kcol == FEAT_BIAS + i, jnp.logical_or(kcol == FEAT_POS, kcol == FEAT_POS + 2)), 1.0,
            jnp.where(kcol == FEAT_POS + 1, slope * key_pos, jnp.where(kcol == FEAT_POS + 3, blk_off, 0.0)))
        ko_ref[0, h, :, :] = (_dot(k, pick) + k_feat).astype(BF16)


def _moba_prep(proj3, slopes):
    bsz, seq, _ = proj3.shape
    nblk = seq // MOBA_BLOCK
    mb = MOBA_BLOCK
    grid_spec = pltpu.PrefetchScalarGridSpec(
        num_scalar_prefetch=1,
        grid=(bsz, nblk),
        in_specs=[pl.BlockSpec((None, mb, A_WIDTH), lambda b, i, s: (b, i, COL_QA // A_WIDTH)),
                  pl.BlockSpec((None, mb, A_WIDTH), lambda b, i, s: (b, i, COL_KA // A_WIDTH)),
                  pl.BlockSpec((None, mb, A_WIDTH), lambda b, i, s: (b, i, COL_VA // A_WIDTH))],
        out_specs=[pl.BlockSpec((1, A_HEADS, mb, AUG), lambda b, i, s: (b, 0, i, 0)),
                   pl.BlockSpec((1, A_HEADS, AUG, mb), lambda b, i, s: (b, 0, 0, i)),
                   pl.BlockSpec((1, A_HEADS, AUG, mb), lambda b, i, s: (b, 0, 0, i))],
        scratch_shapes=[pltpu.VMEM((nblk, A_WIDTH), F32)],
    )
    return pl.pallas_call(
        _moba_prep_kernel,
        grid_spec=grid_spec,
        out_shape=[jax.ShapeDtypeStruct((bsz, A_HEADS, seq, AUG), BF16),
                   jax.ShapeDtypeStruct((bsz, A_HEADS, AUG, seq), BF16),
                   jax.ShapeDtypeStruct((bsz, A_HEADS, AUG, seq), BF16)],
        compiler_params=_cparams(("parallel", "arbitrary")),
        name="moba_prep",
    )(slopes, proj3, proj3, proj3)


def _moba_attn_kernel(q_ref, k_ref, v_ref, o_ref, s_a, s_b, *, group, n_groups):
    i = pl.program_id(2)
    mb = MOBA_BLOCK
    span = group * mb
    own = pl.multiple_of(i * mb, mb)
    key_i = lax.broadcasted_iota(jnp.int32, (mb, mb), 0)
    qry_i = lax.broadcasted_iota(jnp.int32, (mb, mb), 1)
    feat = lax.broadcasted_iota(jnp.int32, (AUG, mb), 0)
    is_bias = jnp.logical_and(feat >= FEAT_BIAS, feat < FEAT_POS)
    q_ts, carry0 = [], []
    for hh in range(2):
        q_t = q_ref[0, hh]
        q_ts.append(q_t)
        q_own = jnp.where(is_bias, jnp.zeros_like(q_t), q_t)
        s = _dot(k_ref[0, hh, pl.ds(own, mb), :], q_own)
        s = jnp.where(key_i <= qry_i, s, NEG_INF)
        m0 = jnp.max(s, axis=0, keepdims=True)
        p = jnp.exp(s - m0)
        carry0 += [m0, _dot(v_ref[0, hh, :, pl.ds(own, mb)], p.astype(BF16))]

    def scores(g, dst):
        start = pl.multiple_of(jnp.minimum(g, n_groups - 1) * span, span)
        for hh in range(2):
            dst[hh] = _dot(k_ref[0, hh, pl.ds(start, span), :], q_ts[hh])

    def absorb(g, src, carry):
        start = pl.multiple_of(g * span, span)
        new = []
        for hh in range(2):
            m, acc = carry[2 * hh], carry[2 * hh + 1]
            sb = src[hh]
            m_new = jnp.maximum(m, jnp.max(sb, axis=0, keepdims=True))
            pb = jnp.exp(sb - m_new)
            alpha = jnp.exp(m - m_new)
            acc = acc * alpha + _dot(v_ref[0, hh, :, pl.ds(start, span)], pb.astype(BF16))
            new += [m_new, acc]
        return tuple(new)

    def body(pair, carry):
        scores(2 * pair + 1, s_b)
        carry = absorb(2 * pair, s_a, carry)
        scores(2 * pair + 2, s_a)
        return absorb(2 * pair + 1, s_b, carry)

    scores(0, s_a)
    live_groups = (i + group - 1) // group
    res = lax.fori_loop(0, (live_groups + 1) // 2, body, tuple(carry0))
    outs = [res[2 * hh + 1][0:A_HEAD_DIM, :] / res[2 * hh + 1][A_HEAD_DIM:A_HEAD_DIM + 1, :] for hh in range(2)]
    o_t = jnp.concatenate(outs, axis=0).astype(BF16)
    eye = (key_i == qry_i).astype(BF16)
    o_ref[0] = _dot_nt(eye, o_t).astype(BF16)


def _moba_attn(k_aug, q_aug_t, v_aug_t):
    bsz, nh, seq, _ = k_aug.shape
    mb = MOBA_BLOCK
    group = min(2, seq // mb)
    n_groups = seq // (group * mb)
    return pl.pallas_call(
        functools.partial(_moba_attn_kernel, group=group, n_groups=n_groups),
        grid=(bsz, nh // 2, seq // mb),
        in_specs=[pl.BlockSpec((1, 2, AUG, mb), lambda b, h, i: (b, h, 0, i)),
                  pl.BlockSpec((1, 2, seq, AUG), lambda b, h, i: (b, h, 0, 0)),
                  pl.BlockSpec((1, 2, AUG, seq), lambda b, h, i: (b, h, 0, 0))],
        out_specs=pl.BlockSpec((1, mb, 2 * A_HEAD_DIM), lambda b, h, i: (b, i, h)),
        out_shape=jax.ShapeDtypeStruct((bsz, seq, A_WIDTH), BF16),
        scratch_shapes=[pltpu.VMEM((2, group * mb, mb), F32), pltpu.VMEM((2, group * mb, mb), F32)],
        compiler_params=_cparams(("parallel", "parallel", "arbitrary")),
        name="moba_attn",
    )(q_aug_t, k_aug, v_aug_t)


def _retention_kernel(cdec_ref, q_ref, k_ref, v_ref, g_ref, decay_ref, qdec_ref, kdec_ref, o_ref, state_scr):
    @pl.when(pl.program_id(1) == 0)
    def _():
        state_scr[...] = jnp.zeros_like(state_scr)

    q = q_ref[...]
    k = k_ref[...]
    width = q.shape[1]
    eye = (lax.broadcasted_iota(jnp.int32, (width, width), 0)
           == lax.broadcasted_iota(jnp.int32, (width, width), 1)).astype(BF16)
    k_t = _dot_nt(eye, k)
    k_t_b = k_t.astype(BF16)
    head_of_col = lax.broadcasted_iota(jnp.int32, q.shape, 1) // R_QK_DIM
    state_b = state_scr[...].astype(BF16)
    for h in range(R_HEADS):
        rows = slice(h * R_QK_DIM, (h + 1) * R_QK_DIM)
        cols = slice(h * R_V_DIM, (h + 1) * R_V_DIM)
        q_m = jnp.where(head_of_col == h, q, jnp.zeros_like(q))
        v_h = v_ref[:, cols]
        inner = _dot(q_m, k_t_b) * decay_ref[h]
        out = _dot(inner.astype(BF16), v_h) + _dot(q_m, state_b) * qdec_ref[h]
        k_dec = (k_t[rows, :] * kdec_ref[h]).astype(BF16)
        state_scr[rows, :] = cdec_ref[h] * state_scr[rows, :] + _dot(k_dec, v_h)
        mu = jnp.mean(out, axis=-1, keepdims=True)
        cen = out - mu
        var = jnp.mean(cen * cen, axis=-1, keepdims=True)
        y = cen * lax.rsqrt(var + GN_EPS)
        o_ref[:, cols] = (y * _silu(g_ref[:, cols].astype(F32))).astype(BF16)


def _retention_consts():
    h = np.arange(R_HEADS, dtype=np.float64)
    log_g = np.log(1.0 - np.exp2(-5.0 - h))
    n = np.arange(R_CHUNK, dtype=np.float64)
    diff = n[:, None] - n[None, :]
    scale = R_QK_DIM ** -0.5
    decay = np.where(diff >= 0, np.exp(np.maximum(diff, 0.0) * log_g[:, None, None]), 0.0) * scale
    q_decay = np.exp((n + 1.0) * log_g[:, None])[:, :, None]
    k_decay = np.exp((R_CHUNK - 1.0 - n) * log_g[:, None])[:, None, :] * scale
    chunk_decay = np.exp(R_CHUNK * log_g)
    return (jnp.asarray(decay, F32), jnp.asarray(q_decay, F32), jnp.asarray(k_decay, F32),
            jnp.asarray(chunk_decay, F32))


def _retention(proj3):
    bsz, seq, _ = proj3.shape
    c = R_CHUNK
    decay, qdec, kdec, cdec = _retention_consts()
    grid_spec = pltpu.PrefetchScalarGridSpec(
        num_scalar_prefetch=1,
        grid=(bsz, seq // c),
        in_specs=[pl.BlockSpec((None, c, R_QK_WIDTH), lambda b, i, s: (b, i, COL_QR // R_QK_WIDTH)),
                  pl.BlockSpec((None, c, R_QK_WIDTH), lambda b, i, s: (b, i, COL_KR // R_QK_WIDTH)),
                  pl.BlockSpec((None, c, R_V_WIDTH), lambda b, i, s: (b, i, COL_VR // R_V_WIDTH)),
                  pl.BlockSpec((None, c, R_V_WIDTH), lambda b, i, s: (b, i, COL_GR // R_V_WIDTH)),
                  pl.BlockSpec((R_HEADS, c, c), lambda b, i, s: (0, 0, 0)),
                  pl.BlockSpec((R_HEADS, c, 1), lambda b, i, s: (0, 0, 0)),
                  pl.BlockSpec((R_HEADS, 1, c), lambda b, i, s: (0, 0, 0))],
        out_specs=pl.BlockSpec((None, c, R_V_WIDTH), lambda b, i, s: (b, i, 0)),
        scratch_shapes=[pltpu.VMEM((R_QK_WIDTH, R_V_DIM), F32)],
    )
    return pl.pallas_call(
        _retention_kernel,
        grid_spec=grid_spec,
        out_shape=jax.ShapeDtypeStruct((bsz, seq, R_V_WIDTH), BF16),
        compiler_params=_cparams(("parallel", "arbitrary")),
        name="retention",
    )(cdec, proj3, proj3, proj3, proj3, decay, qdec, kdec)


def _pack_halves(x):
    w = x.shape[1] // 2
    bits = lax.bitcast_convert_type(x.astype(BF16).astype(F32), jnp.uint32)
    return (bits[:, :w] >> 16) | (bits[:, w:] & jnp.uint32(0xFFFF0000))


def _unpack_halves(p):
    lo = lax.bitcast_convert_type(p << 16, F32)
    hi = lax.bitcast_convert_type(p & jnp.uint32(0xFFFF0000), F32)
    return lo, hi


def _mix_kernel(ya_ref, yr_ref, ga_ref, gt_ref, x_ref, wpa_ref, wpr_ref, wout_ref,
                gt1_ref, g_ref, sc_ref, sh_ref, x1_ref, h2_ref, h2p_ref):
    a = _dot(ya_ref[...], wpa_ref[...]) * _sigmoid(ga_ref[...].astype(F32))
    r = _dot(yr_ref[...], wpr_ref[...]) * _sigmoid(gt_ref[...].astype(F32))
    mix = _dot((a + r).astype(BF16), wout_ref[...])
    x1 = x_ref[...] + gt1_ref[0] * mix
    x1_ref[...] = x1
    ms = jnp.mean(x1 * x1, axis=-1, keepdims=True)
    y = x1 * lax.rsqrt(ms + NORM_EPS) * g_ref[...]
    h2 = y * (1.0 + sc_ref[0]) + sh_ref[0]
    h2_ref[...] = h2.astype(BF16)
    h2p_ref[...] = _pack_halves(h2)


def _mix(ya, yr, proj, x2d, wpa, wpr, wout, gt1, g, sc, sh, seq):
    t, d = x2d.shape
    tm = min(512, seq)
    per_b = seq // tm
    row = lambda i: (i, 0)
    full = lambda i: (0, 0)
    per_batch = lambda i: (i // per_b, 0, 0)
    return pl.pallas_call(
        _mix_kernel,
        grid=(t // tm,),
        in_specs=[pl.BlockSpec((tm, A_WIDTH), row),
                  pl.BlockSpec((tm, R_V_WIDTH), row),
                  pl.BlockSpec((tm, d), lambda i: (i, COL_GA // D_MODEL)),
                  pl.BlockSpec((tm, d), lambda i: (i, COL_GT // D_MODEL)),
                  pl.BlockSpec((tm, d), row),
                  pl.BlockSpec((A_WIDTH, d), full),
                  pl.BlockSpec((R_V_WIDTH, d), full),
                  pl.BlockSpec((d, d), full),
                  pl.BlockSpec((1, 1, d), per_batch),
                  pl.BlockSpec((1, d), full),
                  pl.BlockSpec((1, 1, d), per_batch),
                  pl.BlockSpec((1, 1, d), per_batch)],
        out_specs=[pl.BlockSpec((tm, d), row), pl.BlockSpec((tm, d), row), pl.BlockSpec((tm, d // 2), row)],
        out_shape=[jax.ShapeDtypeStruct((t, d), F32), jax.ShapeDtypeStruct((t, d), BF16),
                   jax.ShapeDtypeStruct((t, d // 2), jnp.uint32)],
        compiler_params=_cparams(("parallel",)),
        name="merge_outproj_norm",
    )(ya, yr, proj, proj, x2d, wpa, wpr, wout, gt1, g, sc, sh)


def _router_kernel(h_ref, wr_ref, b_ref, e_ref, w_ref, r_ref, c_ref):
    logits = _dot_nt(wr_ref[...], h_ref[...])
    scores = _sigmoid(logits)
    choice = scores + b_ref[...]
    tm = logits.shape[1]
    giota = lax.broadcasted_iota(jnp.int32, (GROUP_SIZE, tm), 0)
    gs_rows = []
    for g in range(N_GROUPS):
        cg = choice[g * GROUP_SIZE:(g + 1) * GROUP_SIZE, :]
        m1 = jnp.max(cg, axis=0, keepdims=True)
        i1 = jnp.min(jnp.where(cg == m1, giota, GROUP_SIZE), axis=0, keepdims=True)
        m2 = jnp.max(jnp.where(giota == i1, KNOCKED_OUT, cg), axis=0, keepdims=True)
        gs_rows.append(m1 + m2)
    gs = jnp.concatenate(gs_rows, axis=0)
    grow = lax.broadcasted_iota(jnp.int32, (N_GROUPS, tm), 0)
    gmask = jnp.zeros((N_GROUPS, tm), jnp.bool_)
    for _ in range(TOPK_GROUPS):
        mx = jnp.max(gs, axis=0, keepdims=True)
        ix = jnp.min(jnp.where(gs == mx, grow, N_GROUPS), axis=0, keepdims=True)
        hit = grow == ix
        gmask = jnp.logical_or(gmask, hit)
        gs = jnp.where(hit, KNOCKED_OUT, gs)
    gmask_f = jnp.where(gmask, 1.0, 0.0)
    masked = jnp.concatenate(
        [jnp.where(gmask_f[g:g + 1, :] > 0.5, choice[g * GROUP_SIZE:(g + 1) * GROUP_SIZE, :], NEG_INF)
         for g in range(N_GROUPS)], axis=0)
    erow = lax.broadcasted_iota(jnp.int32, (N_EXPERTS, tm), 0)
    idx_rows, w_rows = [], []
    chosen = jnp.zeros((N_EXPERTS, tm), F32)
    for _ in range(TOP_K):
        mx = jnp.max(masked, axis=0, keepdims=True)
        ix = jnp.min(jnp.where(masked == mx, erow, N_EXPERTS), axis=0, keepdims=True)
        hit = erow == ix
        w_rows.append(jnp.sum(jnp.where(hit, scores, 0.0), axis=0, keepdims=True))
        idx_rows.append(ix)
        chosen = jnp.where(hit, 1.0, chosen)
        masked = jnp.where(hit, KNOCKED_OUT, masked)
    w = jnp.concatenate(w_rows, axis=0)
    w = w / (jnp.sum(w, axis=0, keepdims=True) + 1e-20) * ROUTED_SCALE
    e_ref[...] = jnp.concatenate(idx_rows, axis=0)
    w_ref[...] = w
    chosen_b = chosen.astype(BF16)
    earlier = (lax.broadcasted_iota(jnp.int32, (tm, tm), 0)
               < lax.broadcasted_iota(jnp.int32, (tm, tm), 1)).astype(BF16)
    before = _dot(chosen_b, earlier)
    ranks = [jnp.sum(jnp.where(erow == ix, before, 0.0), axis=0, keepdims=True) for ix in idx_rows]
    r_ref[...] = jnp.concatenate(ranks, axis=0).astype(jnp.int32)
    c_ref[...] = _dot(chosen_b, jnp.ones((tm, 128), BF16))


ROUTER_ROWS = 512


def _router(h2, wr_t, bias_col):
    t, d = h2.shape
    tm = ROUTER_ROWS
    by_tile = lambda i: (0, i)
    return pl.pallas_call(
        _router_kernel,
        grid=(t // tm,),
        in_specs=[pl.BlockSpec((tm, d), lambda i: (i, 0)),
                  pl.BlockSpec((N_EXPERTS, d), lambda i: (0, 0)),
                  pl.BlockSpec((N_EXPERTS, 1), lambda i: (0, 0))],
        out_specs=[pl.BlockSpec((TOP_K, tm), by_tile), pl.BlockSpec((TOP_K, tm), by_tile),
                   pl.BlockSpec((TOP_K, tm), by_tile), pl.BlockSpec((N_EXPERTS, 128), by_tile)],
        out_shape=[jax.ShapeDtypeStruct((TOP_K, t), jnp.int32),
                   jax.ShapeDtypeStruct((TOP_K, t), F32),
                   jax.ShapeDtypeStruct((TOP_K, t), jnp.int32),
                   jax.ShapeDtypeStruct((N_EXPERTS, (t // tm) * 128), F32)],
        compiler_params=_cparams(("parallel",)),
        name="router_topk",
    )(h2, wr_t, bias_col)


def _pos_kernel(e_ref, r_ref, base_ref, p_ref):
    tm = e_ref.shape[1]
    erow = lax.broadcasted_iota(jnp.int32, (N_EXPERTS, tm), 0)
    base = base_ref[0]
    rows = [jnp.sum(jnp.where(erow == e_ref[k:k + 1, :], base, 0.0), axis=0, keepdims=True)
            for k in range(TOP_K)]
    p_ref[0] = jnp.concatenate(rows, axis=0).astype(jnp.int32) + r_ref[...]


MOVE_ROWS = 256


def _positions(eidx_t, rank_t, tile_base):
    t = eidx_t.shape[1]
    tm = min(MOVE_ROWS, t)
    per_router_tile = ROUTER_ROWS // tm
    return pl.pallas_call(
        _pos_kernel,
        grid=(t // tm,),
        in_specs=[pl.BlockSpec((TOP_K, tm), lambda i: (0, i)),
                  pl.BlockSpec((TOP_K, tm), lambda i: (0, i)),
                  pl.BlockSpec((1, N_EXPERTS, 1), lambda i: (i // per_router_tile, 0, 0))],
        out_specs=pl.BlockSpec((1, TOP_K, tm), lambda i: (i, 0, 0)),
        out_shape=jax.ShapeDtypeStruct((t // tm, TOP_K, tm), jnp.int32),
        compiler_params=_cparams(("parallel",)),
        name="slot_positions",
    )(eidx_t, rank_t, tile_base)


SUBLANES = 8
PAD_CHUNKS = (64, 32, 16, 8)


def _dispatch_kernel(pad_start_ref, pad_len_ref, pos_hbm, h_ref, xs_hbm, pos_smem, zero_buf,
                     idx_sem, row_sem, pad_sem):
    i = pl.program_id(0)
    tm = h_ref.shape[0]
    idx_copy = pltpu.make_async_copy(pos_hbm.at[i], pos_smem, idx_sem)
    idx_copy.start()

    @pl.when(i == 0)
    def _():
        zero_buf[...] = jnp.zeros_like(zero_buf)

        def pad_copies(e, wait):
            start = pad_start_ref[e]
            n = pad_len_ref[e]
            head = jnp.minimum((-start) & (SUBLANES - 1), n)

            def fill(first, size, pred):
                @pl.when(pred)
                def _():
                    cp = pltpu.make_async_copy(zero_buf.at[pl.ds(0, size), :],
                                               xs_hbm.at[pl.ds(first, size), :], pad_sem)
                    if wait:
                        cp.wait()
                    else:
                        cp.start()

            for j in range(SUBLANES - 1):
                fill(start + j, 1, j < head)
            ptr = start + head
            rest = n - head
            for chunk in PAD_CHUNKS:
                fill(pl.multiple_of(ptr, SUBLANES), chunk, (rest & chunk) != 0)
                ptr = ptr + (rest & chunk)

        def issue(e, carry):
            pad_copies(e, False)
            return carry

        def drain(e, carry):
            pad_copies(e, True)
            return carry

        lax.fori_loop(0, N_EXPERTS, issue, 0)
        lax.fori_loop(0, N_EXPERTS, drain, 0)

    idx_copy.wait()

    def row_copy(k, t):
        return pltpu.make_async_copy(h_ref.at[pl.ds(t, 1), :], xs_hbm.at[pl.ds(pos_smem[k, t], 1), :], row_sem)

    def issue_rows(t, carry):
        for k in range(TOP_K):
            row_copy(k, t).start()
        return carry

    def drain_rows(t, carry):
        for k in range(TOP_K):
            row_copy(k, t).wait()
        return carry

    lax.fori_loop(0, tm, issue_rows, 0, unroll=4)
    lax.fori_loop(0, tm, drain_rows, 0, unroll=4)


def _dispatch(pad_start, pad_len, pos3, h2p, n_rows):
    t, half = h2p.shape
    tm = pos3.shape[2]
    grid_spec = pltpu.PrefetchScalarGridSpec(
        num_scalar_prefetch=2,
        grid=(t // tm,),
        in_specs=[pl.BlockSpec(memory_space=pl.ANY),
                  pl.BlockSpec((tm, half), lambda i, ps, pn: (i, 0))],
        out_specs=pl.BlockSpec(memory_space=pl.ANY),
        scratch_shapes=[pltpu.SMEM((TOP_K, tm), jnp.int32),
                        pltpu.VMEM((PAD_CHUNKS[0], half), jnp.uint32),
                        pltpu.SemaphoreType.DMA,
                        pltpu.SemaphoreType.DMA,
                        pltpu.SemaphoreType.DMA],
    )
    return pl.pallas_call(
        _dispatch_kernel,
        grid_spec=grid_spec,
        out_shape=jax.ShapeDtypeStruct((n_rows, half), jnp.uint32),
        compiler_params=_cparams(("arbitrary",)),
        name="dispatch_rows",
    )(pad_start, pad_len, pos3, h2p)


EXPERT_CHUNK = 2 * EXPERT_ROWS


def _experts_kernel(start_ref, nblk_ref, xs_hbm, w1_ref, w3_ref, w2_ref, ys_hbm,
                    xbuf, ybuf, w1b, w3b, w2b, in_sem, out_sem):
    e = pl.program_id(0)
    base = start_ref[e]
    nblk = nblk_ref[e]
    half = xbuf.shape[2]

    def ffn(xw):
        lo, hi = _unpack_halves(xw)
        lo = lo.astype(BF16)
        hi = hi.astype(BF16)
        h1 = _dot(lo, w1b[:half, :]) + _dot(hi, w1b[half:, :])
        h3 = _dot(lo, w3b[:half, :]) + _dot(hi, w3b[half:, :])
        mid = (_silu(h1) * h3).astype(BF16)
        return _pack_halves(_dot(mid, w2b[...]))

    def chunk_row(c):
        return pl.multiple_of(base + c * EXPERT_CHUNK, EXPERT_ROWS)

    def in_copy(c, slot):
        return pltpu.make_async_copy(xs_hbm.at[pl.ds(chunk_row(c), EXPERT_CHUNK), :], xbuf.at[slot], in_sem.at[slot])

    def out_copy(c, slot):
        return pltpu.make_async_copy(ybuf.at[slot], ys_hbm.at[pl.ds(chunk_row(c), EXPERT_CHUNK), :], out_sem.at[slot])

    @pl.when(nblk > 0)
    def _():
        w1b[...] = w1_ref[0].astype(BF16)
        w3b[...] = w3_ref[0].astype(BF16)
        w2b[...] = w2_ref[0].astype(BF16)
        n_chunks = nblk // 2

        @pl.when(n_chunks > 0)
        def _():
            in_copy(0, 0).start()

        def body(c, carry):
            slot = c & 1

            @pl.when(c + 1 < n_chunks)
            def _():
                in_copy(c + 1, 1 - slot).start()

            in_copy(c, slot).wait()

            @pl.when(c >= 2)
            def _():
                out_copy(c - 2, slot).wait()

            ybuf[slot] = ffn(xbuf[slot])
            out_copy(c, slot).start()
            return carry

        lax.fori_loop(0, n_chunks, body, 0)

        @pl.when(n_chunks >= 2)
        def _():
            out_copy(n_chunks - 2, n_chunks & 1).wait()

        @pl.when(n_chunks >= 1)
        def _():
            out_copy(n_chunks - 1, (n_chunks - 1) & 1).wait()

        @pl.when((nblk & 1) == 1)
        def _():
            row = chunk_row(n_chunks)
            rows = pl.ds(0, EXPERT_ROWS)
            cin = pltpu.make_async_copy(xs_hbm.at[pl.ds(row, EXPERT_ROWS), :], xbuf.at[0, rows, :], in_sem.at[0])
            cin.start()
            cin.wait()
            ybuf[0, rows, :] = ffn(xbuf[0, rows, :])
            cout = pltpu.make_async_copy(ybuf.at[0, rows, :], ys_hbm.at[pl.ds(row, EXPERT_ROWS), :], out_sem.at[0])
            cout.start()
            cout.wait()


def _experts(row_start, nblk_e, xs, w1, w3, w2):
    n_rows, half = xs.shape
    d = 2 * half
    wblk = lambda e, rs, nb: (e, 0, 0)
    grid_spec = pltpu.PrefetchScalarGridSpec(
        num_scalar_prefetch=2,
        grid=(N_EXPERTS,),
        in_specs=[pl.BlockSpec(memory_space=pl.ANY),
                  pl.BlockSpec((1, d, EXPERT_FF), wblk),
                  pl.BlockSpec((1, d, EXPERT_FF), wblk),
                  pl.BlockSpec((1, EXPERT_FF, d), wblk)],
        out_specs=pl.BlockSpec(memory_space=pl.ANY),
        scratch_shapes=[pltpu.VMEM((2, EXPERT_CHUNK, half), jnp.uint32),
                        pltpu.VMEM((2, EXPERT_CHUNK, half), jnp.uint32),
                        pltpu.VMEM((d, EXPERT_FF), BF16),
                        pltpu.VMEM((d, EXPERT_FF), BF16),
                        pltpu.VMEM((EXPERT_FF, d), BF16),
                        pltpu.SemaphoreType.DMA((2,)),
                        pltpu.SemaphoreType.DMA((2,))],
    )
    return pl.pallas_call(
        _experts_kernel,
        grid_spec=grid_spec,
        out_shape=jax.ShapeDtypeStruct((n_rows, half), jnp.uint32),
        compiler_params=_cparams(("arbitrary",)),
        name="routed_experts",
    )(row_start, nblk_e, xs, w1, w3, w2)


def _combine_kernel(pos_hbm, ys_hbm, w_ref, h_ref, x1_ref, ws1_ref, ws3_ref, ws2_ref, gt2_ref, g_ref, o_ref,
                    pos_smem, ybuf, idx_sem, row_sem):
    i = pl.program_id(0)
    tm, d = x1_ref.shape
    half = d // 2
    idx_copy = pltpu.make_async_copy(pos_hbm.at[i], pos_smem, idx_sem)
    idx_copy.start()
    idx_copy.wait()

    def row_copy(k, t):
        return pltpu.make_async_copy(ys_hbm.at[pl.ds(pos_smem[k, t], 1), :], ybuf.at[k, pl.ds(t, 1), :], row_sem)

    def issue_rows(t, carry):
        for k in range(TOP_K):
            row_copy(k, t).start()
        return carry

    def drain_rows(t, carry):
        for k in range(TOP_K):
            row_copy(k, t).wait()
        return carry

    lax.fori_loop(0, tm, issue_rows, 0, unroll=4)

    h = h_ref[...]
    mid = (_silu(_dot(h, ws1_ref[...])) * _dot(h, ws3_ref[...])).astype(BF16)
    shared = _dot(mid, ws2_ref[...])

    lax.fori_loop(0, tm, drain_rows, 0, unroll=4)

    w = w_ref[...]
    acc_lo = jnp.zeros((tm, half), F32)
    acc_hi = jnp.zeros((tm, half), F32)
    for k in range(TOP_K):
        lo, hi = _unpack_halves(ybuf[k])
        acc_lo = acc_lo + lo * w[:, k:k + 1]
        acc_hi = acc_hi + hi * w[:, k:k + 1]
    gt2 = gt2_ref[0]
    g = g_ref[...]
    x_lo = x1_ref[:, :half] + gt2[:, :half] * (acc_lo + shared[:, :half])
    x_hi = x1_ref[:, half:] + gt2[:, half:] * (acc_hi + shared[:, half:])
    ms = (jnp.sum(x_lo * x_lo, axis=-1, keepdims=True) + jnp.sum(x_hi * x_hi, axis=-1, keepdims=True)) / d
    inv = lax.rsqrt(ms + NORM_EPS)
    o_ref[:, :half] = x_lo * inv * g[:, :half]
    o_ref[:, half:] = x_hi * inv * g[:, half:]


def _combine(pos3, ys, wts, h2, x1, ws1, ws3, ws2, gt2, g_final, seq):
    t, d = x1.shape
    tm = pos3.shape[2]
    per_b = seq // tm
    row = lambda i: (i, 0)
    full = lambda i: (0, 0)
    return pl.pallas_call(
        _combine_kernel,
        grid=(t // tm,),
        in_specs=[pl.BlockSpec(memory_space=pl.ANY),
                  pl.BlockSpec(memory_space=pl.ANY),
                  pl.BlockSpec((tm, TOP_K), row),
                  pl.BlockSpec((tm, d), row),
                  pl.BlockSpec((tm, d), row),
                  pl.BlockSpec((d, SHARED_FF), full),
                  pl.BlockSpec((d, SHARED_FF), full),
                  pl.BlockSpec((SHARED_FF, d), full),
                  pl.BlockSpec((1, 1, d), lambda i: (i // per_b, 0, 0)),
                  pl.BlockSpec((1, d), full)],
        out_specs=pl.BlockSpec((tm, d), row),
        out_shape=jax.ShapeDtypeStruct((t, d), F32),
        scratch_shapes=[pltpu.SMEM((TOP_K, tm), jnp.int32),
                        pltpu.VMEM((TOP_K, tm, d // 2), jnp.uint32),
                        pltpu.SemaphoreType.DMA,
                        pltpu.SemaphoreType.DMA],
        compiler_params=_cparams(("arbitrary",)),
        name="combine_shared_final",
    )(pos3, ys, wts, h2, x1, ws1, ws3, ws2, gt2, g_final)


def _slot_tables(cnt, t):
    ntiles = cnt.shape[1] // 128
    cnt_tile = cnt.reshape(N_EXPERTS, ntiles, 128)[:, :, 0].astype(jnp.int32)
    counts = jnp.sum(cnt_tile, axis=1)
    padded = (counts + EXPERT_ROWS - 1) // EXPERT_ROWS * EXPERT_ROWS
    pstart = jnp.cumsum(padded) - padded
    tile_base = pstart[:, None] + jnp.cumsum(cnt_tile, axis=1) - cnt_tile
    n_rblk = -(-(t * TOP_K) // EXPERT_ROWS) + N_EXPERTS
    return (pstart, padded // EXPERT_ROWS, pstart + counts, padded - counts,
            tile_base.T.astype(F32).reshape(ntiles, N_EXPERTS, 1), n_rblk * EXPERT_ROWS)


def _permute_in_cols(w_in):
    qa, ka, va, qr, kr, vr, gr, ga, gt = jnp.split(
        w_in, np.cumsum((A_WIDTH, A_WIDTH, A_WIDTH, R_QK_WIDTH, R_QK_WIDTH, R_V_WIDTH, R_V_WIDTH,
                         D_MODEL))[:].tolist(), axis=1)
    return jnp.concatenate([vr, gr, ga, gt, qa, ka, va, qr, kr], axis=1)


def kernel(x, c, w_ada, b_ada, g_mix, w_in, w_pa, w_pr, w_out, g_ffn, w_router, router_bias,
           w1, w3, w2, ws1, ws3, ws2, g_final):
    bsz, seq, d = x.shape
    t = bsz * seq
    depth = w_ada.shape[0]
    assert depth == 1, "the final norm is fused into the single layer's last kernel"
    slopes = jnp.exp2(-8.0 / A_HEADS * jnp.arange(1, A_HEADS + 1, dtype=F32))
    x2d = x.reshape(t, d)
    for l in range(depth):
        mod = _ada(c, w_ada[l], b_ada[l])
        sh1, sc1, gt1, sh2, sc2, gt2 = [m.reshape(bsz, 1, d) for m in jnp.split(mod, 6, axis=-1)]
        w_in_p = _permute_in_cols(w_in[l]).astype(BF16)
        proj = _inproj(x2d, g_mix[l].reshape(1, d), sc1, sh1, w_in_p, seq)
        proj3 = proj.reshape(bsz, seq, IN_COLS)
        k_aug, q_aug_t, v_aug_t = _moba_prep(proj3, slopes)
        ya = _moba_attn(k_aug, q_aug_t, v_aug_t).reshape(t, A_WIDTH)
        yr = _retention(proj3).reshape(t, R_V_WIDTH)
        x1, h2, h2p = _mix(ya, yr, proj, x2d, w_pa[l].astype(BF16), w_pr[l].astype(BF16),
                           w_out[l].astype(BF16), gt1, g_ffn[l].reshape(1, d), sc2, sh2, seq)
        eidx_t, wts_t, rank_t, cnt = _router(h2, w_router[l].T.astype(BF16),
                                             router_bias[l].reshape(N_EXPERTS, 1))
        row_start, nblk_e, pad_start, pad_len, tile_base, n_rows = _slot_tables(cnt, t)
        pos3 = _positions(eidx_t, rank_t, tile_base)
        xs = _dispatch(pad_start, pad_len, pos3, h2p, n_rows)
        ys = _experts(row_start, nblk_e, xs, w1[l], w3[l], w2[l])
        x2d = _combine(pos3, ys, wts_t.T, h2, x1, ws1[l].astype(BF16), ws3[l].astype(BF16),
                       ws2[l].astype(BF16), gt2, g_final.reshape(1, d), seq)
    return x2d.reshape(bsz, seq, d)
```

```python
import functools

import jax
import jax.numpy as jnp
import numpy as np
from jax import lax
from jax.experimental import pallas as pl
from jax.experimental.pallas import tpu as pltpu

F32 = jnp.float32
BF16 = jnp.bfloat16

D_MODEL = 1024
A_HEADS = 8
A_HEAD_DIM = 64
A_WIDTH = A_HEADS * A_HEAD_DIM
MOBA_BLOCK = 256
MOBA_TOPK = 3
R_HEADS = 8
R_QK_DIM = 64
R_V_DIM = 128
R_QK_WIDTH = R_HEADS * R_QK_DIM
R_V_WIDTH = R_HEADS * R_V_DIM
R_CHUNK = 128
N_EXPERTS = 256
TOP_K = 8
N_GROUPS = 8
GROUP_SIZE = N_EXPERTS // N_GROUPS
TOPK_GROUPS = 4
EXPERT_FF = 256
SHARED_FF = 256
ROUTED_SCALE = 2.5
NORM_EPS = 1e-6
GN_EPS = 1e-6
NEG_INF = -1e30
KNOCKED_OUT = -3e38

COL_VR, COL_GR, COL_GA, COL_GT = 0, 1024, 2048, 3072
COL_QA, COL_KA, COL_VA, COL_QR, COL_KR = 4096, 4608, 5120, 5632, 6144
IN_COLS = 6656
AUG = 128
FEAT_BIAS = A_HEAD_DIM
FEAT_POS = A_HEAD_DIM + 32

VMEM_LIMIT = 56 * 1024 * 1024


def _cparams(sem, vmem=VMEM_LIMIT):
    return pltpu.CompilerParams(dimension_semantics=sem, vmem_limit_bytes=vmem)


def _dot(a, b):
    return jnp.dot(a, b, preferred_element_type=F32)


def _dot_nt(a, b):
    return lax.dot_general(a, b, (((1,), (1,)), ((), ())), preferred_element_type=F32)


def _sigmoid(x):
    return 1.0 / (1.0 + jnp.exp(-x))


def _silu(x):
    return x * _sigmoid(x)


def _ada_kernel(c_ref, w_ref, b_ref, o_ref):
    c = c_ref[...]
    s = _silu(c)
    s_hi = s.astype(BF16)
    s_lo = (s - s_hi.astype(F32)).astype(BF16)
    w = w_ref[...]
    w_hi = w.astype(BF16)
    w_lo = (w - w_hi.astype(F32)).astype(BF16)
    o_ref[...] = _dot(s_hi, w_hi) + _dot(s_hi, w_lo) + _dot(s_lo, w_hi) + b_ref[...]


def _ada(c, w_ada, b_ada):
    bsz, d = c.shape
    n = w_ada.shape[1]
    tn = 1024
    return pl.pallas_call(
        _ada_kernel,
        grid=(n // tn,),
        in_specs=[pl.BlockSpec((bsz, d), lambda j: (0, 0)),
                  pl.BlockSpec((d, tn), lambda j: (0, j)),
                  pl.BlockSpec((1, tn), lambda j: (0, j))],
        out_specs=pl.BlockSpec((bsz, tn), lambda j: (0, j)),
        out_shape=jax.ShapeDtypeStruct((bsz, n), F32),
        compiler_params=_cparams(("parallel",)),
        name="ada_mod",
    )(c, w_ada, b_ada.reshape(1, n))


def _inproj_kernel(x_ref, g_ref, sc_ref, sh_ref, w_ref, o_ref, h_scr):
    @pl.when(pl.program_id(1) == 0)
    def _():
        x = x_ref[...]
        ms = jnp.mean(x * x, axis=-1, keepdims=True)
        y = x * lax.rsqrt(ms + NORM_EPS) * g_ref[...]
        h_scr[...] = (y * (1.0 + sc_ref[0]) + sh_ref[0]).astype(BF16)

    o_ref[...] = _dot(h_scr[...], w_ref[...]).astype(BF16)


def _inproj(x2d, g, sc, sh, w_bf16, seq):
    t, d = x2d.shape
    n = w_bf16.shape[1]
    tm = min(1024, seq)
    tn = 512
    per_b = seq // tm
    return pl.pallas_call(
        _inproj_kernel,
        grid=(t // tm, n // tn),
        in_specs=[pl.BlockSpec((tm, d), lambda i, j: (i, 0)),
                  pl.BlockSpec((1, d), lambda i, j: (0, 0)),
                  pl.BlockSpec((1, 1, d), lambda i, j: (i // per_b, 0, 0)),
                  pl.BlockSpec((1, 1, d), lambda i, j: (i // per_b, 0, 0)),
                  pl.BlockSpec((d, tn), lambda i, j: (0, j))],
        out_specs=pl.BlockSpec((tm, tn), lambda i, j: (i, j)),
        out_shape=jax.ShapeDtypeStruct((t, n), BF16),
        scratch_shapes=[pltpu.VMEM((tm, d), BF16)],
        compiler_params=_cparams(("parallel", "arbitrary")),
        name="norm_inproj",
    )(x2d, g, sc, sh, w_bf16)


def _moba_prep_kernel(slopes_ref, q_ref, k_ref, v_ref, ko_ref, qo_ref, vo_ref, kmean_scr):
    i = pl.program_id(1)
    nblk = kmean_scr.shape[0]
    width = q_ref.shape[1]

    @pl.when(i == 0)
    def _():
        kmean_scr[...] = jnp.zeros_like(kmean_scr)

    q = q_ref[...]
    k = k_ref[...]
    v = v_ref[...]
    kmean_scr[pl.ds(i, 1), :] = jnp.mean(k.astype(F32), axis=0, keepdims=True)

    eye = (lax.broadcasted_iota(jnp.int32, (width, width), 0)
           == lax.broadcasted_iota(jnp.int32, (width, width), 1)).astype(BF16)
    q_t = _dot_nt(eye, q)
    v_t = _dot_nt(eye, v)

    km = kmean_scr[...]
    km_rep = jnp.concatenate([km] * A_HEADS, axis=0)
    r_head = lax.broadcasted_iota(jnp.int32, km_rep.shape, 0) // nblk
    c_head = lax.broadcasted_iota(jnp.int32, km_rep.shape, 1) // A_HEAD_DIM
    km_bd = jnp.where(r_head == c_head, km_rep, 0.0)
    km_hi = km_bd.astype(BF16)
    km_lo = (km_bd - km_hi.astype(F32)).astype(BF16)
    q_t_b = q_t.astype(BF16)
    gate_all = _dot(km_hi, q_t_b) + _dot(km_lo, q_t_b)

    mb = q.shape[0]
    blk = lax.broadcasted_iota(jnp.int32, (nblk, mb), 0)
    lane_pos = lax.broadcasted_iota(jnp.int32, (16, mb), 1).astype(F32)
    row16 = lax.broadcasted_iota(jnp.int32, (16, mb), 0)
    key_pos = lax.broadcasted_iota(jnp.int32, (mb, AUG), 0).astype(F32)
    kcol = lax.broadcasted_iota(jnp.int32, (mb, AUG), 1)
    sel_r = lax.broadcasted_iota(jnp.int32, (width, AUG), 0)
    sel_c = lax.broadcasted_iota(jnp.int32, (width, AUG), 1)

    for h in range(A_HEADS):
        slope = slopes_ref[h]
        g = jnp.where(blk < i, gate_all[h * nblk:(h + 1) * nblk, :], NEG_INF)
        sel = jnp.zeros((nblk, mb), jnp.bool_)
        for r in range(MOBA_TOPK):
            m = jnp.max(g, axis=0, keepdims=True)
            idx = jnp.min(jnp.where(g == m, blk, nblk), axis=0, keepdims=True)
            hit = blk == idx
            sel = jnp.logical_or(sel, jnp.logical_and(hit, r < i))
            g = jnp.where(hit, KNOCKED_OUT, g)
        bias_t = jnp.where(sel, 0.0, NEG_INF)

        scale = A_HEAD_DIM ** -0.5
        qo_ref[0, h, 0:A_HEAD_DIM, :] = (q_t[h * A_HEAD_DIM:(h + 1) * A_HEAD_DIM, :] * scale).astype(BF16)
        qo_ref[0, h, FEAT_BIAS:FEAT_BIAS + nblk, :] = bias_t.astype(BF16)
        if nblk < 32:
            qo_ref[0, h, FEAT_BIAS + nblk:FEAT_POS, :] = jnp.zeros((32 - nblk, mb), BF16)
        blk_off = slope * (i * mb).astype(F32)
        pos_feat = jnp.where(row16 == 0, -slope * lane_pos,
                             jnp.where(row16 == 2, -blk_off,
                                       jnp.where(jnp.logical_or(row16 == 1, row16 == 3), 1.0, 0.0)))
        qo_ref[0, h, FEAT_POS:FEAT_POS + 16, :] = pos_feat.astype(BF16)
        qo_ref[0, h, FEAT_POS + 16:AUG, :] = jnp.zeros((AUG - FEAT_POS - 16, mb), BF16)

        vo_ref[0, h, 0:A_HEAD_DIM, :] = v_t[h * A_HEAD_DIM:(h + 1) * A_HEAD_DIM, :].astype(BF16)
        vo_ref[0, h, A_HEAD_DIM:A_HEAD_DIM + 16, :] = jnp.where(row16 == 0, 1.0, 0.0).astype(BF16)
        vo_ref[0, h, A_HEAD_DIM + 16:AUG, :] = jnp.zeros((AUG - A_HEAD_DIM - 16, mb), BF16)

        pick = jnp.where(jnp.logical_and(sel_r == sel_c + h * A_HEAD_DIM, sel_c < A_HEAD_DIM),
                         1.0, 0.0).astype(BF16)
        k_feat = jnp.where(
            jnp.logical_or(kcol == FEAT_BIAS + i, jnp.logical_or(kcol == FEAT_POS, kcol == FEAT_POS + 2)), 1.0,
            jnp.where(kcol == FEAT_POS + 1, slope * key_pos, jnp.where(kcol == FEAT_POS + 3, blk_off, 0.0)))
        ko_ref[0, h, :, :] = (_dot(k, pick) + k_feat).astype(BF16)


def _moba_prep(proj3, slopes):
    bsz, seq, _ = proj3.shape
    nblk = seq // MOBA_BLOCK
    mb = MOBA_BLOCK
    grid_spec = pltpu.PrefetchScalarGridSpec(
        num_scalar_prefetch=1,
        grid=(bsz, nblk),
        in_specs=[pl.BlockSpec((None, mb, A_WIDTH), lambda b, i, s: (b, i, COL_QA // A_WIDTH)),
                  pl.BlockSpec((None, mb, A_WIDTH), lambda b, i, s: (b, i, COL_KA // A_WIDTH)),
                  pl.BlockSpec((None, mb, A_WIDTH), lambda b, i, s: (b, i, COL_VA // A_WIDTH))],
        out_specs=[pl.BlockSpec((1, A_HEADS, mb, AUG), lambda b, i, s: (b, 0, i, 0)),
                   pl.BlockSpec((1, A_HEADS, AUG, mb), lambda b, i, s: (b, 0, 0, i)),
                   pl.BlockSpec((1, A_HEADS, AUG, mb), lambda b, i, s: (b, 0, 0, i))],
        scratch_shapes=[pltpu.VMEM((nblk, A_WIDTH), F32)],
    )
    return pl.pallas_call(
        _moba_prep_kernel,
        grid_spec=grid_spec,
        out_shape=[jax.ShapeDtypeStruct((bsz, A_HEADS, seq, AUG), BF16),
                   jax.ShapeDtypeStruct((bsz, A_HEADS, AUG, seq), BF16),
                   jax.ShapeDtypeStruct((bsz, A_HEADS, AUG, seq), BF16)],
        compiler_params=_cparams(("parallel", "arbitrary")),
        name="moba_prep",
    )(slopes, proj3, proj3, proj3)


def _moba_attn_kernel(q_ref, k_ref, v_ref, o_ref, s_a, s_b, *, group, n_groups):
    i = pl.program_id(2)
    mb = MOBA_BLOCK
    span = group * mb
    own = pl.multiple_of(i * mb, mb)
    key_i = lax.broadcasted_iota(jnp.int32, (mb, mb), 0)
    qry_i = lax.broadcasted_iota(jnp.int32, (mb, mb), 1)
    feat = lax.broadcasted_iota(jnp.int32, (AUG, mb), 0)
    is_bias = jnp.logical_and(feat >= FEAT_BIAS, feat < FEAT_POS)
    q_ts, carry0 = [], []
    for hh in range(2):
        q_t = q_ref[0, hh]
        q_ts.append(q_t)
        q_own = jnp.where(is_bias, jnp.zeros_like(q_t), q_t)
        s = _dot(k_ref[0, hh, pl.ds(own, mb), :], q_own)
        s = jnp.where(key_i <= qry_i, s, NEG_INF)
        m0 = jnp.max(s, axis=0, keepdims=True)
        p = jnp.exp(s - m0)
        carry0 += [m0, _dot(v_ref[0, hh, :, pl.ds(own, mb)], p.astype(BF16))]

    def scores(g, dst):
        start = pl.multiple_of(jnp.minimum(g, n_groups - 1) * span, span)
        for hh in range(2):
            dst[hh] = _dot(k_ref[0, hh, pl.ds(start, span), :], q_ts[hh])

    def absorb(g, src, carry):
        start = pl.multiple_of(g * span, span)
        new = []
        for hh in range(2):
            m, acc = carry[2 * hh], carry[2 * hh + 1]
            sb = src[hh]
            m_new = jnp.maximum(m, jnp.max(sb, axis=0, keepdims=True))
            pb = jnp.exp(sb - m_new)
            alpha = jnp.exp(m - m_new)
            acc = acc * alpha + _dot(v_ref[0, hh, :, pl.ds(start, span)], pb.astype(BF16))
            new += [m_new, acc]
        return tuple(new)

    def body(pair, carry):
        scores(2 * pair + 1, s_b)
        carry = absorb(2 * pair, s_a, carry)
        scores(2 * pair + 2, s_a)
        return absorb(2 * pair + 1, s_b, carry)

    scores(0, s_a)
    live_groups = (i + group - 1) // group
    res = lax.fori_loop(0, (live_groups + 1) // 2, body, tuple(carry0))
    outs = [res[2 * hh + 1][0:A_HEAD_DIM, :] / res[2 * hh + 1][A_HEAD_DIM:A_HEAD_DIM + 1, :] for hh in range(2)]
    o_t = jnp.concatenate(outs, axis=0).astype(BF16)
    eye = (key_i == qry_i).astype(BF16)
    o_ref[0] = _dot_nt(eye, o_t).astype(BF16)


def _moba_attn(k_aug, q_aug_t, v_aug_t):
    bsz, nh, seq, _ = k_aug.shape
    mb = MOBA_BLOCK
    group = min(2, seq // mb)
    n_groups = seq // (group * mb)
    return pl.pallas_call(
        functools.partial(_moba_attn_kernel, group=group, n_groups=n_groups),
        grid=(bsz, nh // 2, seq // mb),
        in_specs=[pl.BlockSpec((1, 2, AUG, mb), lambda b, h, i: (b, h, 0, i)),
                  pl.BlockSpec((1, 2, seq, AUG), lambda b, h, i: (b, h, 0, 0)),
                  pl.BlockSpec((1, 2, AUG, seq), lambda b, h, i: (b, h, 0, 0))],
        out_specs=pl.BlockSpec((1, mb, 2 * A_HEAD_DIM), lambda b, h, i: (b, i, h)),
        out_shape=jax.ShapeDtypeStruct((bsz, seq, A_WIDTH), BF16),
        scratch_shapes=[pltpu.VMEM((2, group * mb, mb), F32), pltpu.VMEM((2, group * mb, mb), F32)],
        compiler_params=_cparams(("parallel", "parallel", "arbitrary")),
        name="moba_attn",
    )(q_aug_t, k_aug, v_aug_t)


def _retention_kernel(cdec_ref, q_ref, k_ref, v_ref, g_ref, decay_ref, qdec_ref, kdec_ref, o_ref, state_scr):
    @pl.when(pl.program_id(1) == 0)
    def _():
        state_scr[...] = jnp.zeros_like(state_scr)

    q = q_ref[...]
    k = k_ref[...]
    width = q.shape[1]
    eye = (lax.broadcasted_iota(jnp.int32, (width, width), 0)
           == lax.broadcasted_iota(jnp.int32, (width, width), 1)).astype(BF16)
    k_t = _dot_nt(eye, k)
    k_t_b = k_t.astype(BF16)
    head_of_col = lax.broadcasted_iota(jnp.int32, q.shape, 1) // R_QK_DIM
    state_b = state_scr[...].astype(BF16)
    for h in range(R_HEADS):
        rows = slice(h * R_QK_DIM, (h + 1) * R_QK_DIM)
        cols = slice(h * R_V_DIM, (h + 1) * R_V_DIM)
        q_m = jnp.where(head_of_col == h, q, jnp.zeros_like(q))
        v_h = v_ref[:, cols]
        inner = _dot(q_m, k_t_b) * decay_ref[h]
        out = _dot(inner.astype(BF16), v_h) + _dot(q_m, state_b) * qdec_ref[h]
        k_dec = (k_t[rows, :] * kdec_ref[h]).astype(BF16)
        state_scr[rows, :] = cdec_ref[h] * state_scr[rows, :] + _dot(k_dec, v_h)
        mu = jnp.mean(out, axis=-1, keepdims=True)
        cen = out - mu
        var = jnp.mean(cen * cen, axis=-1, keepdims=True)
        y = cen * lax.rsqrt(var + GN_EPS)
        o_ref[:, cols] = (y * _silu(g_ref[:, cols].astype(F32))).astype(BF16)


def _retention_consts():
    h = np.arange(R_HEADS, dtype=np.float64)
    log_g = np.log(1.0 - np.exp2(-5.0 - h))
    n = np.arange(R_CHUNK, dtype=np.float64)
    diff = n[:, None] - n[None, :]
    scale = R_QK_DIM ** -0.5
    decay = np.where(diff >= 0, np.exp(np.maximum(diff, 0.0) * log_g[:, None, None]), 0.0) * scale
    q_decay = np.exp((n + 1.0) * log_g[:, None])[:, :, None]
    k_decay = np.exp((R_CHUNK - 1.0 - n) * log_g[:, None])[:, None, :] * scale
    chunk_decay = np.exp(R_CHUNK * log_g)
    return (jnp.asarray(decay, F32), jnp.asarray(q_decay, F32), jnp.asarray(k_decay, F32),
            jnp.asarray(chunk_decay, F32))


def _retention(proj3):
    bsz, seq, _ = proj3.shape
    c = R_CHUNK
    decay, qdec, kdec, cdec = _retention_consts()
    grid_spec = pltpu.PrefetchScalarGridSpec(
        num_scalar_prefetch=1,
        grid=(bsz, seq // c),
        in_specs=[pl.BlockSpec((None, c, R_QK_WIDTH), lambda b, i, s: (b, i, COL_QR // R_QK_WIDTH)),
                  pl.BlockSpec((None, c, R_QK_WIDTH), lambda b, i, s: (b, i, COL_KR // R_QK_WIDTH)),
                  pl.BlockSpec((None, c, R_V_WIDTH), lambda b, i, s: (b, i, COL_VR // R_V_WIDTH)),
                  pl.BlockSpec((None, c, R_V_WIDTH), lambda b, i, s: (b, i, COL_GR // R_V_WIDTH)),
                  pl.BlockSpec((R_HEADS, c, c), lambda b, i, s: (0, 0, 0)),
                  pl.BlockSpec((R_HEADS, c, 1), lambda b, i, s: (0, 0, 0)),
                  pl.BlockSpec((R_HEADS, 1, c), lambda b, i, s: (0, 0, 0))],
        out_specs=pl.BlockSpec((None, c, R_V_WIDTH), lambda b, i, s: (b, i, 0)),
        scratch_shapes=[pltpu.VMEM((R_QK_WIDTH, R_V_DIM), F32)],
    )
    return pl.pallas_call(
        _retention_kernel,
        grid_spec=grid_spec,
        out_shape=jax.ShapeDtypeStruct((bsz, seq, R_V_WIDTH), BF16),
        compiler_params=_cparams(("parallel", "arbitrary")),
        name="retention",
    )(cdec, proj3, proj3, proj3, proj3, decay, qdec, kdec)


LANES = 128
ROW_SUBLANES = D_MODEL // LANES


def _store_tile_rows(ref, lead, x):
    m = x.shape[0]
    for j in range(ROW_SUBLANES):
        ref[lead + (pl.ds(j, m, stride=ROW_SUBLANES), slice(None))] = x[:, j * LANES:(j + 1) * LANES]


def _load_tile_rows(ref, lead, m):
    return [ref[lead + (pl.ds(j, m, stride=ROW_SUBLANES), slice(None))] for j in range(ROW_SUBLANES)]


def _mix_kernel(ya_ref, yr_ref, ga_ref, gt_ref, x_ref, wpa_ref, wpr_ref, wout_ref,
                gt1_ref, g_ref, sc_ref, sh_ref, x1_ref, h2_ref, h2r_ref):
    a = _dot(ya_ref[...], wpa_ref[...]) * _sigmoid(ga_ref[...].astype(F32))
    r = _dot(yr_ref[...], wpr_ref[...]) * _sigmoid(gt_ref[...].astype(F32))
    mix = _dot((a + r).astype(BF16), wout_ref[...])
    x1 = x_ref[...] + gt1_ref[0] * mix
    x1_ref[...] = x1
    ms = jnp.mean(x1 * x1, axis=-1, keepdims=True)
    y = x1 * lax.rsqrt(ms + NORM_EPS) * g_ref[...]
    h2 = y * (1.0 + sc_ref[0]) + sh_ref[0]
    h2_ref[...] = h2.astype(BF16)
    _store_tile_rows(h2r_ref, (), h2)


def _mix(ya, yr, proj, x2d, wpa, wpr, wout, gt1, g, sc, sh, seq):
    t, d = x2d.shape
    tm = min(512, seq)
    per_b = seq // tm
    row = lambda i: (i, 0)
    full = lambda i: (0, 0)
    per_batch = lambda i: (i // per_b, 0, 0)
    return pl.pallas_call(
        _mix_kernel,
        grid=(t // tm,),
        in_specs=[pl.BlockSpec((tm, A_WIDTH), row),
                  pl.BlockSpec((tm, R_V_WIDTH), row),
                  pl.BlockSpec((tm, d), lambda i: (i, COL_GA // D_MODEL)),
                  pl.BlockSpec((tm, d), lambda i: (i, COL_GT // D_MODEL)),
                  pl.BlockSpec((tm, d), row),
                  pl.BlockSpec((A_WIDTH, d), full),
                  pl.BlockSpec((R_V_WIDTH, d), full),
                  pl.BlockSpec((d, d), full),
                  pl.BlockSpec((1, 1, d), per_batch),
                  pl.BlockSpec((1, d), full),
                  pl.BlockSpec((1, 1, d), per_batch),
                  pl.BlockSpec((1, 1, d), per_batch)],
        out_specs=[pl.BlockSpec((tm, d), row), pl.BlockSpec((tm, d), row),
                   pl.BlockSpec((tm * ROW_SUBLANES, LANES), row)],
        out_shape=[jax.ShapeDtypeStruct((t, d), F32), jax.ShapeDtypeStruct((t, d), BF16),
                   jax.ShapeDtypeStruct((t * ROW_SUBLANES, LANES), F32)],
        compiler_params=_cparams(("parallel",)),
        name="merge_outproj_norm",
    )(ya, yr, proj, proj, x2d, wpa, wpr, wout, gt1, g, sc, sh)


def _router_kernel(h_ref, wr_ref, b_ref, e_ref, w_ref, r_ref, c_ref):
    logits = _dot_nt(wr_ref[...], h_ref[...])
    scores = _sigmoid(logits)
    choice = scores + b_ref[...]
    tm = logits.shape[1]
    giota = lax.broadcasted_iota(jnp.int32, (GROUP_SIZE, tm), 0)
    gs_rows = []
    for g in range(N_GROUPS):
        cg = choice[g * GROUP_SIZE:(g + 1) * GROUP_SIZE, :]
        m1 = jnp.max(cg, axis=0, keepdims=True)
        i1 = jnp.min(jnp.where(cg == m1, giota, GROUP_SIZE), axis=0, keepdims=True)
        m2 = jnp.max(jnp.where(giota == i1, KNOCKED_OUT, cg), axis=0, keepdims=True)
        gs_rows.append(m1 + m2)
    gs = jnp.concatenate(gs_rows, axis=0)
    grow = lax.broadcasted_iota(jnp.int32, (N_GROUPS, tm), 0)
    gmask = jnp.zeros((N_GROUPS, tm), jnp.bool_)
    for _ in range(TOPK_GROUPS):
        mx = jnp.max(gs, axis=0, keepdims=True)
        ix = jnp.min(jnp.where(gs == mx, grow, N_GROUPS), axis=0, keepdims=True)
        hit = grow == ix
        gmask = jnp.logical_or(gmask, hit)
        gs = jnp.where(hit, KNOCKED_OUT, gs)
    gmask_f = jnp.where(gmask, 1.0, 0.0)
    masked = jnp.concatenate(
        [jnp.where(gmask_f[g:g + 1, :] > 0.5, choice[g * GROUP_SIZE:(g + 1) * GROUP_SIZE, :], NEG_INF)
         for g in range(N_GROUPS)], axis=0)
    erow = lax.broadcasted_iota(jnp.int32, (N_EXPERTS, tm), 0)
    idx_rows, w_rows = [], []
    chosen = jnp.zeros((N_EXPERTS, tm), F32)
    for _ in range(TOP_K):
        mx = jnp.max(masked, axis=0, keepdims=True)
        ix = jnp.min(jnp.where(masked == mx, erow, N_EXPERTS), axis=0, keepdims=True)
        hit = erow == ix
        w_rows.append(jnp.sum(jnp.where(hit, scores, 0.0), axis=0, keepdims=True))
        idx_rows.append(ix)
        chosen = jnp.where(hit, 1.0, chosen)
        masked = jnp.where(hit, KNOCKED_OUT, masked)
    w = jnp.concatenate(w_rows, axis=0)
    w = w / (jnp.sum(w, axis=0, keepdims=True) + 1e-20) * ROUTED_SCALE
    e_ref[...] = jnp.concatenate(idx_rows, axis=0)
    w_ref[...] = w
    chosen_b = chosen.astype(BF16)
    earlier = (lax.broadcasted_iota(jnp.int32, (tm, tm), 0)
               < lax.broadcasted_iota(jnp.int32, (tm, tm), 1)).astype(BF16)
    before = _dot(chosen_b, earlier)
    ranks = [jnp.sum(jnp.where(erow == ix, before, 0.0), axis=0, keepdims=True) for ix in idx_rows]
    r_ref[...] = jnp.concatenate(ranks, axis=0).astype(jnp.int32)
    c_ref[...] = _dot(chosen_b, jnp.ones((tm, 128), BF16))


ROUTER_ROWS = 512


def _router(h2, wr_t, bias_col):
    t, d = h2.shape
    tm = ROUTER_ROWS
    by_tile = lambda i: (0, i)
    return pl.pallas_call(
        _router_kernel,
        grid=(t // tm,),
        in_specs=[pl.BlockSpec((tm, d), lambda i: (i, 0)),
                  pl.BlockSpec((N_EXPERTS, d), lambda i: (0, 0)),
                  pl.BlockSpec((N_EXPERTS, 1), lambda i: (0, 0))],
        out_specs=[pl.BlockSpec((TOP_K, tm), by_tile), pl.BlockSpec((TOP_K, tm), by_tile),
                   pl.BlockSpec((TOP_K, tm), by_tile), pl.BlockSpec((N_EXPERTS, 128), by_tile)],
        out_shape=[jax.ShapeDtypeStruct((TOP_K, t), jnp.int32),
                   jax.ShapeDtypeStruct((TOP_K, t), F32),
                   jax.ShapeDtypeStruct((TOP_K, t), jnp.int32),
                   jax.ShapeDtypeStruct((N_EXPERTS, (t // tm) * 128), F32)],
        compiler_params=_cparams(("parallel",)),
        name="router_topk",
    )(h2, wr_t, bias_col)


def _pos_kernel(e_ref, r_ref, base_ref, p_ref):
    tm = e_ref.shape[1]
    erow = lax.broadcasted_iota(jnp.int32, (N_EXPERTS, tm), 0)
    base = base_ref[0]
    rows = [jnp.sum(jnp.where(erow == e_ref[k:k + 1, :], base, 0.0), axis=0, keepdims=True)
            for k in range(TOP_K)]
    p_ref[0] = jnp.concatenate(rows, axis=0).astype(jnp.int32) + r_ref[...]


MOVE_ROWS = 256


def _positions(eidx_t, rank_t, tile_base):
    t = eidx_t.shape[1]
    tm = min(MOVE_ROWS, t)
    per_router_tile = ROUTER_ROWS // tm
    return pl.pallas_call(
        _pos_kernel,
        grid=(t // tm,),
        in_specs=[pl.BlockSpec((TOP_K, tm), lambda i: (0, i)),
                  pl.BlockSpec((TOP_K, tm), lambda i: (0, i)),
                  pl.BlockSpec((1, N_EXPERTS, 1), lambda i: (i // per_router_tile, 0, 0))],
        out_specs=pl.BlockSpec((1, TOP_K, tm), lambda i: (i, 0, 0)),
        out_shape=jax.ShapeDtypeStruct((t // tm, TOP_K, tm), jnp.int32),
        compiler_params=_cparams(("parallel",)),
        name="slot_positions",
    )(eidx_t, rank_t, tile_base)


SLOT_ROWS = 256
PAD_CHUNKS = (128, 64, 32, 16, 8, 4, 2, 1)


def _dispatch_kernel(pad_start_ref, pad_len_ref, pos_hbm, h_ref, xs_hbm, pos_smem, zero_buf,
                     idx_sem, row_sem, pad_sem):
    i = pl.program_id(0)
    tm = h_ref.shape[0] // ROW_SUBLANES
    idx_copy = pltpu.make_async_copy(pos_hbm.at[i], pos_smem, idx_sem)
    idx_copy.start()

    def tile_rows(first, n):
        return pl.ds(pl.multiple_of(first * ROW_SUBLANES, ROW_SUBLANES), n * ROW_SUBLANES)

    @pl.when(i == 0)
    def _():
        zero_buf[...] = jnp.zeros_like(zero_buf)

        def pad_copies(e, wait):
            ptr = pad_start_ref[e]
            n = pad_len_ref[e]
            for chunk in PAD_CHUNKS:
                @pl.when((n & chunk) != 0)
                def _(ptr=ptr, chunk=chunk):
                    cp = pltpu.make_async_copy(zero_buf.at[tile_rows(0, chunk), :],
                                               xs_hbm.at[tile_rows(ptr, chunk), :], pad_sem)
                    if wait:
                        cp.wait()
                    else:
                        cp.start()
                ptr = ptr + (n & chunk)

        def issue(e, carry):
            pad_copies(e, False)
            return carry

        def drain(e, carry):
            pad_copies(e, True)
            return carry

        lax.fori_loop(0, N_EXPERTS, issue, 0)
        lax.fori_loop(0, N_EXPERTS, drain, 0)

    idx_copy.wait()

    def row_copy(k, t):
        return pltpu.make_async_copy(h_ref.at[tile_rows(t, 1), :],
                                     xs_hbm.at[tile_rows(pos_smem[k * tm + t], 1), :], row_sem)

    def issue_rows(t, carry):
        for k in range(TOP_K):
            row_copy(k, t).start()
        return carry

    def drain_rows(t, carry):
        for k in range(TOP_K):
            row_copy(k, t).wait()
        return carry

    lax.fori_loop(0, tm, issue_rows, 0, unroll=4)
    lax.fori_loop(0, tm, drain_rows, 0, unroll=4)


def _dispatch(pad_start, pad_len, pos2, h2r, n_rows):
    t = h2r.shape[0] // ROW_SUBLANES
    tm = pos2.shape[1] // TOP_K
    grid_spec = pltpu.PrefetchScalarGridSpec(
        num_scalar_prefetch=2,
        grid=(t // tm,),
        in_specs=[pl.BlockSpec(memory_space=pl.ANY),
                  pl.BlockSpec((tm * ROW_SUBLANES, LANES), lambda i, ps, pn: (i, 0))],
        out_specs=pl.BlockSpec(memory_space=pl.ANY),
        scratch_shapes=[pltpu.SMEM((TOP_K * tm,), jnp.int32),
                        pltpu.VMEM((PAD_CHUNKS[0] * ROW_SUBLANES, LANES), F32),
                        pltpu.SemaphoreType.DMA,
                        pltpu.SemaphoreType.DMA,
                        pltpu.SemaphoreType.DMA],
    )
    return pl.pallas_call(
        _dispatch_kernel,
        grid_spec=grid_spec,
        out_shape=jax.ShapeDtypeStruct((n_rows * ROW_SUBLANES, LANES), F32),
        compiler_params=_cparams(("arbitrary",)),
        name="dispatch_rows",
    )(pad_start, pad_len, pos2, h2r)


def _experts_kernel(blk_e_ref, nblk_ref, x_ref, w1_ref, w3_ref, w2_ref, y_ref, w1b, w3b, w2b):
    s = pl.program_id(0)

    @pl.when(s < nblk_ref[0])
    def _():
        @pl.when(jnp.logical_or(s == 0, blk_e_ref[s] != blk_e_ref[jnp.maximum(s - 1, 0)]))
        def _():
            w1b[...] = w1_ref[0].astype(BF16)
            w3b[...] = w3_ref[0].astype(BF16)
            w2b[...] = w2_ref[0].astype(BF16)

        x = jnp.concatenate([p.astype(BF16) for p in _load_tile_rows(x_ref, (), SLOT_ROWS)], axis=1)
        mid = (_silu(_dot(x, w1b[...])) * _dot(x, w3b[...])).astype(BF16)
        _store_tile_rows(y_ref, (), _dot(mid, w2b[...]))


def _experts(blk_e, nblk_used, xs, w1, w3, w2):
    n_rows = xs.shape[0] // ROW_SUBLANES
    d = D_MODEL
    block_rows = SLOT_ROWS * ROW_SUBLANES
    blk = lambda s, be, nb: (jnp.minimum(s, nb[0] - 1), 0)
    wblk = lambda s, be, nb: (be[jnp.minimum(s, nb[0] - 1)], 0, 0)
    grid_spec = pltpu.PrefetchScalarGridSpec(
        num_scalar_prefetch=2,
        grid=(n_rows // SLOT_ROWS,),
        in_specs=[pl.BlockSpec((block_rows, LANES), blk),
                  pl.BlockSpec((1, d, EXPERT_FF), wblk),
                  pl.BlockSpec((1, d, EXPERT_FF), wblk),
                  pl.BlockSpec((1, EXPERT_FF, d), wblk)],
        out_specs=pl.BlockSpec((block_rows, LANES), blk),
        scratch_shapes=[pltpu.VMEM((d, EXPERT_FF), BF16),
                        pltpu.VMEM((d, EXPERT_FF), BF16),
                        pltpu.VMEM((EXPERT_FF, d), BF16)],
    )
    return pl.pallas_call(
        _experts_kernel,
        grid_spec=grid_spec,
        out_shape=jax.ShapeDtypeStruct(xs.shape, F32),
        compiler_params=_cparams(("arbitrary",)),
        name="routed_experts",
    )(blk_e, nblk_used, xs, w1, w3, w2)


def _combine_kernel(pos_hbm, ys_hbm, w_ref, h_ref, x1_ref, ws1_ref, ws3_ref, ws2_ref, gt2_ref, g_ref, o_ref,
                    pos_smem, ybuf, idx_sem, row_sem):
    i = pl.program_id(0)
    tm, d = x1_ref.shape
    idx_copy = pltpu.make_async_copy(pos_hbm.at[i], pos_smem, idx_sem)
    idx_copy.start()
    idx_copy.wait()

    def tile_rows(first):
        return pl.ds(pl.multiple_of(first * ROW_SUBLANES, ROW_SUBLANES), ROW_SUBLANES)

    def row_copy(k, t):
        return pltpu.make_async_copy(ys_hbm.at[tile_rows(pos_smem[k * tm + t]), :],
                                     ybuf.at[k, tile_rows(t), :], row_sem)

    def issue_rows(t, carry):
        for k in range(TOP_K):
            row_copy(k, t).start()
        return carry

    def drain_rows(t, carry):
        for k in range(TOP_K):
            row_copy(k, t).wait()
        return carry

    lax.fori_loop(0, tm, issue_rows, 0, unroll=4)

    h = h_ref[...]
    mid = (_silu(_dot(h, ws1_ref[...])) * _dot(h, ws3_ref[...])).astype(BF16)
    shared = _dot(mid, ws2_ref[...])

    lax.fori_loop(0, tm, drain_rows, 0, unroll=4)

    w = w_ref[...]
    acc = None
    for k in range(TOP_K):
        wk = w[:, k:k + 1]
        pieces = [p * wk for p in _load_tile_rows(ybuf, (k,), tm)]
        acc = pieces if acc is None else [a + p for a, p in zip(acc, pieces)]
    routed = jnp.concatenate(acc, axis=1)
    x2 = x1_ref[...] + gt2_ref[0] * (routed + shared)
    ms = jnp.mean(x2 * x2, axis=-1, keepdims=True)
    o_ref[...] = x2 * lax.rsqrt(ms + NORM_EPS) * g_ref[...]


def _combine(pos2, ys, wts, h2, x1, ws1, ws3, ws2, gt2, g_final, seq):
    t, d = x1.shape
    tm = pos2.shape[1] // TOP_K
    per_b = seq // tm
    row = lambda i: (i, 0)
    full = lambda i: (0, 0)
    return pl.pallas_call(
        _combine_kernel,
        grid=(t // tm,),
        in_specs=[pl.BlockSpec(memory_space=pl.ANY),
                  pl.BlockSpec(memory_space=pl.ANY),
                  pl.BlockSpec((tm, TOP_K), row),
                  pl.BlockSpec((tm, d), row),
                  pl.BlockSpec((tm, d), row),
                  pl.BlockSpec((d, SHARED_FF), full),
                  pl.BlockSpec((d, SHARED_FF), full),
                  pl.BlockSpec((SHARED_FF, d), full),
                  pl.BlockSpec((1, 1, d), lambda i: (i // per_b, 0, 0)),
                  pl.BlockSpec((1, d), full)],
        out_specs=pl.BlockSpec((tm, d), row),
        out_shape=jax.ShapeDtypeStruct((t, d), F32),
        scratch_shapes=[pltpu.SMEM((TOP_K * tm,), jnp.int32),
                        pltpu.VMEM((TOP_K, tm * ROW_SUBLANES, LANES), F32),
                        pltpu.SemaphoreType.DMA,
                        pltpu.SemaphoreType.DMA],
        compiler_params=_cparams(("arbitrary",)),
        name="combine_shared_final",
    )(pos2, ys, wts, h2, x1, ws1, ws3, ws2, gt2, g_final)


def _slot_tables(cnt, t):
    ntiles = cnt.shape[1] // 128
    cnt_tile = cnt.reshape(N_EXPERTS, ntiles, 128)[:, :, 0].astype(jnp.int32)
    counts = jnp.sum(cnt_tile, axis=1)
    padded = (counts + SLOT_ROWS - 1) // SLOT_ROWS * SLOT_ROWS
    pstart = jnp.cumsum(padded) - padded
    tile_base = pstart[:, None] + jnp.cumsum(cnt_tile, axis=1) - cnt_tile
    n_blk = -(-(t * TOP_K) // SLOT_ROWS) + N_EXPERTS
    blk_end = jnp.cumsum(padded // SLOT_ROWS)
    blk_e = jnp.sum((blk_end[None, :] <= jnp.arange(n_blk)[:, None]).astype(jnp.int32), axis=1)
    blk_e = jnp.minimum(blk_e, N_EXPERTS - 1)
    return (blk_e, blk_end[-1:].astype(jnp.int32), pstart + counts, padded - counts,
            tile_base.T.astype(F32).reshape(ntiles, N_EXPERTS, 1), n_blk * SLOT_ROWS)


def _permute_in_cols(w_in):
    qa, ka, va, qr, kr, vr, gr, ga, gt = jnp.split(
        w_in, np.cumsum((A_WIDTH, A_WIDTH, A_WIDTH, R_QK_WIDTH, R_QK_WIDTH, R_V_WIDTH, R_V_WIDTH,
                         D_MODEL))[:].tolist(), axis=1)
    return jnp.concatenate([vr, gr, ga, gt, qa, ka, va, qr, kr], axis=1)


def kernel(x, c, w_ada, b_ada, g_mix, w_in, w_pa, w_pr, w_out, g_ffn, w_router, router_bias,
           w1, w3, w2, ws1, ws3, ws2, g_final):
    bsz, seq, d = x.shape
    t = bsz * seq
    depth = w_ada.shape[0]
    assert depth == 1, "the final norm is fused into the single layer's last kernel"
    slopes = jnp.exp2(-8.0 / A_HEADS * jnp.arange(1, A_HEADS + 1, dtype=F32))
    x2d = x.reshape(t, d)
    for l in range(depth):
        mod = _ada(c, w_ada[l], b_ada[l])
        sh1, sc1, gt1, sh2, sc2, gt2 = [m.reshape(bsz, 1, d) for m in jnp.split(mod, 6, axis=-1)]
        w_in_p = _permute_in_cols(w_in[l]).astype(BF16)
        proj = _inproj(x2d, g_mix[l].reshape(1, d), sc1, sh1, w_in_p, seq)
        proj3 = proj.reshape(bsz, seq, IN_COLS)
        k_aug, q_aug_t, v_aug_t = _moba_prep(proj3, slopes)
        ya = _moba_attn(k_aug, q_aug_t, v_aug_t).reshape(t, A_WIDTH)
        yr = _retention(proj3).reshape(t, R_V_WIDTH)
        x1, h2, h2r = _mix(ya, yr, proj, x2d, w_pa[l].astype(BF16), w_pr[l].astype(BF16),
                           w_out[l].astype(BF16), gt1, g_ffn[l].reshape(1, d), sc2, sh2, seq)
        eidx_t, wts_t, rank_t, cnt = _router(h2, w_router[l].T.astype(BF16),
                                             router_bias[l].reshape(N_EXPERTS, 1))
        blk_e, nblk_used, pad_start, pad_len, tile_base, n_rows = _slot_tables(cnt, t)
        pos3 = _positions(eidx_t, rank_t, tile_base)
        pos2 = pos3.reshape(pos3.shape[0], TOP_K * pos3.shape[2])
        xs = _dispatch(pad_start, pad_len, pos2, h2r, n_rows)
        ys = _experts(blk_e, nblk_used, xs, w1[l], w3[l], w2[l])
        x2d = _combine(pos2, ys, wts_t.T, h2, x1, ws1[l].astype(BF16), ws3[l].astype(BF16),
                       ws2[l].astype(BF16), gt2, g_final.reshape(1, d), seq)
    return x2d.reshape(bsz, seq, d)
```

```python
import functools

import jax
import jax.numpy as jnp
import numpy as np
from jax import lax
from jax.experimental import pallas as pl
from jax.experimental.pallas import tpu as pltpu

F32 = jnp.float32
BF16 = jnp.bfloat16

D_MODEL = 1024
A_HEADS = 8
A_HEAD_DIM = 64
A_WIDTH = A_HEADS * A_HEAD_DIM
MOBA_BLOCK = 256
MOBA_TOPK = 3
R_HEADS = 8
R_QK_DIM = 64
R_V_DIM = 128
R_QK_WIDTH = R_HEADS * R_QK_DIM
R_V_WIDTH = R_HEADS * R_V_DIM
R_CHUNK = 128
N_EXPERTS = 256
TOP_K = 8
N_GROUPS = 8
GROUP_SIZE = N_EXPERTS // N_GROUPS
TOPK_GROUPS = 4
EXPERT_FF = 256
SHARED_FF = 256
ROUTED_SCALE = 2.5
NORM_EPS = 1e-6
GN_EPS = 1e-6
NEG_INF = -1e30
KNOCKED_OUT = -3e38

COL_VR, COL_GR, COL_GA, COL_GT = 0, 1024, 2048, 3072
COL_QA, COL_KA, COL_VA, COL_QR, COL_KR = 4096, 4608, 5120, 5632, 6144
IN_COLS = 6656
AUG = 128
FEAT_BIAS = A_HEAD_DIM
FEAT_POS = A_HEAD_DIM + 32

VMEM_LIMIT = 56 * 1024 * 1024


def _cparams(sem, vmem=VMEM_LIMIT):
    return pltpu.CompilerParams(dimension_semantics=sem, vmem_limit_bytes=vmem)


def _dot(a, b):
    return jnp.dot(a, b, preferred_element_type=F32)


def _dot_nt(a, b):
    return lax.dot_general(a, b, (((1,), (1,)), ((), ())), preferred_element_type=F32)


def _sigmoid(x):
    return 1.0 / (1.0 + jnp.exp(-x))


def _silu(x):
    return x * _sigmoid(x)


def _ada_kernel(c_ref, w_ref, b_ref, o_ref):
    c = c_ref[...]
    s = _silu(c)
    s_hi = s.astype(BF16)
    s_lo = (s - s_hi.astype(F32)).astype(BF16)
    w = w_ref[...]
    w_hi = w.astype(BF16)
    w_lo = (w - w_hi.astype(F32)).astype(BF16)
    o_ref[...] = _dot(s_hi, w_hi) + _dot(s_hi, w_lo) + _dot(s_lo, w_hi) + b_ref[...]


def _ada(c, w_ada, b_ada):
    bsz, d = c.shape
    n = w_ada.shape[1]
    tn = 1024
    return pl.pallas_call(
        _ada_kernel,
        grid=(n // tn,),
        in_specs=[pl.BlockSpec((bsz, d), lambda j: (0, 0)),
                  pl.BlockSpec((d, tn), lambda j: (0, j)),
                  pl.BlockSpec((1, tn), lambda j: (0, j))],
        out_specs=pl.BlockSpec((bsz, tn), lambda j: (0, j)),
        out_shape=jax.ShapeDtypeStruct((bsz, n), F32),
        compiler_params=_cparams(("parallel",)),
        name="ada_mod",
    )(c, w_ada, b_ada.reshape(1, n))


INPROJ_COLS = 512


def _inproj_kernel(x_ref, g_ref, sc_ref, sh_ref, w_ref, o_ref):
    x = x_ref[...]
    ms = jnp.mean(x * x, axis=-1, keepdims=True)
    y = x * lax.rsqrt(ms + NORM_EPS) * g_ref[...]
    h = (y * (1.0 + sc_ref[0]) + sh_ref[0]).astype(BF16)
    for j in range(w_ref.shape[1] // INPROJ_COLS):
        cols = slice(j * INPROJ_COLS, (j + 1) * INPROJ_COLS)
        o_ref[:, cols] = _dot(h, w_ref[:, cols]).astype(BF16)


def _inproj(x2d, g, sc, sh, w_bf16, seq):
    t, d = x2d.shape
    n = w_bf16.shape[1]
    tm = min(512, seq)
    per_b = seq // tm
    return pl.pallas_call(
        _inproj_kernel,
        grid=(t // tm,),
        in_specs=[pl.BlockSpec((tm, d), lambda i: (i, 0)),
                  pl.BlockSpec((1, d), lambda i: (0, 0)),
                  pl.BlockSpec((1, 1, d), lambda i: (i // per_b, 0, 0)),
                  pl.BlockSpec((1, 1, d), lambda i: (i // per_b, 0, 0)),
                  pl.BlockSpec((d, n), lambda i: (0, 0))],
        out_specs=pl.BlockSpec((tm, n), lambda i: (i, 0)),
        out_shape=jax.ShapeDtypeStruct((t, n), BF16),
        compiler_params=_cparams(("parallel",)),
        name="norm_inproj",
    )(x2d, g, sc, sh, w_bf16)


def _moba_prep_kernel(slopes_ref, q_ref, k_ref, v_ref, ko_ref, qo_ref, vo_ref, kmean_scr):
    i = pl.program_id(1)
    nblk = kmean_scr.shape[0]
    width = q_ref.shape[1]

    @pl.when(i == 0)
    def _():
        kmean_scr[...] = jnp.zeros_like(kmean_scr)

    q = q_ref[...]
    k = k_ref[...]
    v = v_ref[...]
    kmean_scr[pl.ds(i, 1), :] = jnp.mean(k.astype(F32), axis=0, keepdims=True)

    eye = (lax.broadcasted_iota(jnp.int32, (width, width), 0)
           == lax.broadcasted_iota(jnp.int32, (width, width), 1)).astype(BF16)
    q_t = _dot_nt(eye, q)
    v_t = _dot_nt(eye, v)

    km = kmean_scr[...]
    km_rep = jnp.concatenate([km] * A_HEADS, axis=0)
    r_head = lax.broadcasted_iota(jnp.int32, km_rep.shape, 0) // nblk
    c_head = lax.broadcasted_iota(jnp.int32, km_rep.shape, 1) // A_HEAD_DIM
    km_bd = jnp.where(r_head == c_head, km_rep, 0.0)
    km_hi = km_bd.astype(BF16)
    km_lo = (km_bd - km_hi.astype(F32)).astype(BF16)
    q_t_b = q_t.astype(BF16)
    gate_all = _dot(km_hi, q_t_b) + _dot(km_lo, q_t_b)

    mb = q.shape[0]
    blk = lax.broadcasted_iota(jnp.int32, (nblk, mb), 0)
    lane_pos = lax.broadcasted_iota(jnp.int32, (16, mb), 1).astype(F32)
    row16 = lax.broadcasted_iota(jnp.int32, (16, mb), 0)
    key_pos = lax.broadcasted_iota(jnp.int32, (mb, AUG), 0).astype(F32)
    kcol = lax.broadcasted_iota(jnp.int32, (mb, AUG), 1)
    sel_r = lax.broadcasted_iota(jnp.int32, (width, AUG), 0)
    sel_c = lax.broadcasted_iota(jnp.int32, (width, AUG), 1)

    for h in range(A_HEADS):
        slope = slopes_ref[h]
        g = jnp.where(blk < i, gate_all[h * nblk:(h + 1) * nblk, :], NEG_INF)
        sel = jnp.zeros((nblk, mb), jnp.bool_)
        for r in range(MOBA_TOPK):
            m = jnp.max(g, axis=0, keepdims=True)
            idx = jnp.min(jnp.where(g == m, blk, nblk), axis=0, keepdims=True)
            hit = blk == idx
            sel = jnp.logical_or(sel, jnp.logical_and(hit, r < i))
            g = jnp.where(hit, KNOCKED_OUT, g)
        bias_t = jnp.where(sel, 0.0, NEG_INF)

        scale = A_HEAD_DIM ** -0.5
        qo_ref[0, h, 0:A_HEAD_DIM, :] = (q_t[h * A_HEAD_DIM:(h + 1) * A_HEAD_DIM, :] * scale).astype(BF16)
        qo_ref[0, h, FEAT_BIAS:FEAT_BIAS + nblk, :] = bias_t.astype(BF16)
        if nblk < 32:
            qo_ref[0, h, FEAT_BIAS + nblk:FEAT_POS, :] = jnp.zeros((32 - nblk, mb), BF16)
        blk_off = slope * (i * mb).astype(F32)
        pos_feat = jnp.where(row16 == 0, -slope * lane_pos,
                             jnp.where(row16 == 2, -blk_off,
                                       jnp.where(jnp.logical_or(row16 == 1, row16 == 3), 1.0, 0.0)))
        qo_ref[0, h, FEAT_POS:FEAT_POS + 16, :] = pos_feat.astype(BF16)
        qo_ref[0, h, FEAT_POS + 16:AUG, :] = jnp.zeros((AUG - FEAT_POS - 16, mb), BF16)

        vo_ref[0, h, 0:A_HEAD_DIM, :] = v_t[h * A_HEAD_DIM:(h + 1) * A_HEAD_DIM, :].astype(BF16)
        vo_ref[0, h, A_HEAD_DIM:A_HEAD_DIM + 16, :] = jnp.where(row16 == 0, 1.0, 0.0).astype(BF16)
        vo_ref[0, h, A_HEAD_DIM + 16:AUG, :] = jnp.zeros((AUG - A_HEAD_DIM - 16, mb), BF16)

        pick = jnp.where(jnp.logical_and(sel_r == sel_c + h * A_HEAD_DIM, sel_c < A_HEAD_DIM),
                         1.0, 0.0).astype(BF16)
        k_feat = jnp.where(
            jnp.logical_or(kcol == FEAT_BIAS + i, jnp.logical_or(kcol == FEAT_POS, kcol == FEAT_POS + 2)), 1.0,
            jnp.where(kcol == FEAT_POS + 1, slope * key_pos, jnp.where(kcol == FEAT_POS + 3, blk_off, 0.0)))
        ko_ref[0, h, :, :] = (_dot(k, pick) + k_feat).astype(BF16)


def _moba_prep(proj3, slopes):
    bsz, seq, _ = proj3.shape
    nblk = seq // MOBA_BLOCK
    mb = MOBA_BLOCK
    grid_spec = pltpu.PrefetchScalarGridSpec(
        num_scalar_prefetch=1,
        grid=(bsz, nblk),
        in_specs=[pl.BlockSpec((None, mb, A_WIDTH), lambda b, i, s: (b, i, COL_QA // A_WIDTH)),
                  pl.BlockSpec((None, mb, A_WIDTH), lambda b, i, s: (b, i, COL_KA // A_WIDTH)),
                  pl.BlockSpec((None, mb, A_WIDTH), lambda b, i, s: (b, i, COL_VA // A_WIDTH))],
        out_specs=[pl.BlockSpec((1, A_HEADS, mb, AUG), lambda b, i, s: (b, 0, i, 0)),
                   pl.BlockSpec((1, A_HEADS, AUG, mb), lambda b, i, s: (b, 0, 0, i)),
                   pl.BlockSpec((1, A_HEADS, AUG, mb), lambda b, i, s: (b, 0, 0, i))],
        scratch_shapes=[pltpu.VMEM((nblk, A_WIDTH), F32)],
    )
    return pl.pallas_call(
        _moba_prep_kernel,
        grid_spec=grid_spec,
        out_shape=[jax.ShapeDtypeStruct((bsz, A_HEADS, seq, AUG), BF16),
                   jax.ShapeDtypeStruct((bsz, A_HEADS, AUG, seq), BF16),
                   jax.ShapeDtypeStruct((bsz, A_HEADS, AUG, seq), BF16)],
        compiler_params=_cparams(("parallel", "arbitrary")),
        name="moba_prep",
    )(slopes, proj3, proj3, proj3)


def _moba_attn_kernel(q_ref, k_ref, v_ref, o_ref, s_a, s_b, *, group, n_groups):
    i = pl.program_id(2)
    mb = MOBA_BLOCK
    span = group * mb
    own = pl.multiple_of(i * mb, mb)
    key_i = lax.broadcasted_iota(jnp.int32, (mb, mb), 0)
    qry_i = lax.broadcasted_iota(jnp.int32, (mb, mb), 1)
    feat = lax.broadcasted_iota(jnp.int32, (AUG, mb), 0)
    is_bias = jnp.logical_and(feat >= FEAT_BIAS, feat < FEAT_POS)
    q_ts, carry0 = [], []
    for hh in range(2):
        q_t = q_ref[0, hh]
        q_ts.append(q_t)
        q_own = jnp.where(is_bias, jnp.zeros_like(q_t), q_t)
        s = _dot(k_ref[0, hh, pl.ds(own, mb), :], q_own)
        s = jnp.where(key_i <= qry_i, s, NEG_INF)
        m0 = jnp.max(s, axis=0, keepdims=True)
        p = jnp.exp(s - m0)
        carry0 += [m0, _dot(v_ref[0, hh, :, pl.ds(own, mb)], p.astype(BF16))]

    def scores(g, dst):
        start = pl.multiple_of(jnp.minimum(g, n_groups - 1) * span, span)
        for hh in range(2):
            dst[hh] = _dot(k_ref[0, hh, pl.ds(start, span), :], q_ts[hh])

    def absorb(g, src, carry):
        start = pl.multiple_of(g * span, span)
        new = []
        for hh in range(2):
            m, acc = carry[2 * hh], carry[2 * hh + 1]
            sb = src[hh]
            m_new = jnp.maximum(m, jnp.max(sb, axis=0, keepdims=True))
            pb = jnp.exp(sb - m_new)
            alpha = jnp.exp(m - m_new)
            acc = acc * alpha + _dot(v_ref[0, hh, :, pl.ds(start, span)], pb.astype(BF16))
            new += [m_new, acc]
        return tuple(new)

    def body(pair, carry):
        scores(2 * pair + 1, s_b)
        carry = absorb(2 * pair, s_a, carry)
        scores(2 * pair + 2, s_a)
        return absorb(2 * pair + 1, s_b, carry)

    scores(0, s_a)
    live_groups = (i + group - 1) // group
    res = lax.fori_loop(0, (live_groups + 1) // 2, body, tuple(carry0))
    outs = [res[2 * hh + 1][0:A_HEAD_DIM, :] / res[2 * hh + 1][A_HEAD_DIM:A_HEAD_DIM + 1, :] for hh in range(2)]
    o_t = jnp.concatenate(outs, axis=0).astype(BF16)
    eye = (key_i == qry_i).astype(BF16)
    o_ref[0] = _dot_nt(eye, o_t).astype(BF16)


def _moba_attn(k_aug, q_aug_t, v_aug_t):
    bsz, nh, seq, _ = k_aug.shape
    mb = MOBA_BLOCK
    group = min(2, seq // mb)
    n_groups = seq // (group * mb)
    return pl.pallas_call(
        functools.partial(_moba_attn_kernel, group=group, n_groups=n_groups),
        grid=(bsz, nh // 2, seq // mb),
        in_specs=[pl.BlockSpec((1, 2, AUG, mb), lambda b, h, i: (b, h, 0, i)),
                  pl.BlockSpec((1, 2, seq, AUG), lambda b, h, i: (b, h, 0, 0)),
                  pl.BlockSpec((1, 2, AUG, seq), lambda b, h, i: (b, h, 0, 0))],
        out_specs=pl.BlockSpec((1, mb, 2 * A_HEAD_DIM), lambda b, h, i: (b, i, h)),
        out_shape=jax.ShapeDtypeStruct((bsz, seq, A_WIDTH), BF16),
        scratch_shapes=[pltpu.VMEM((2, group * mb, mb), F32), pltpu.VMEM((2, group * mb, mb), F32)],
        compiler_params=_cparams(("parallel", "parallel", "arbitrary")),
        name="moba_attn",
    )(q_aug_t, k_aug, v_aug_t)


def _retention_kernel(cdec_ref, q_ref, k_ref, v_ref, g_ref, decay_ref, qdec_ref, kdec_ref, o_ref, state_scr):
    @pl.when(pl.program_id(1) == 0)
    def _():
        state_scr[...] = jnp.zeros_like(state_scr)

    q = q_ref[...]
    k = k_ref[...]
    width = q.shape[1]
    eye = (lax.broadcasted_iota(jnp.int32, (width, width), 0)
           == lax.broadcasted_iota(jnp.int32, (width, width), 1)).astype(BF16)
    k_t = _dot_nt(eye, k)
    k_t_b = k_t.astype(BF16)
    head_of_col = lax.broadcasted_iota(jnp.int32, q.shape, 1) // R_QK_DIM
    state_b = state_scr[...].astype(BF16)
    for h in range(R_HEADS):
        rows = slice(h * R_QK_DIM, (h + 1) * R_QK_DIM)
        cols = slice(h * R_V_DIM, (h + 1) * R_V_DIM)
        q_m = jnp.where(head_of_col == h, q, jnp.zeros_like(q))
        v_h = v_ref[:, cols]
        inner = _dot(q_m, k_t_b) * decay_ref[h]
        out = _dot(inner.astype(BF16), v_h) + _dot(q_m, state_b) * qdec_ref[h]
        k_dec = (k_t[rows, :] * kdec_ref[h]).astype(BF16)
        state_scr[rows, :] = cdec_ref[h] * state_scr[rows, :] + _dot(k_dec, v_h)
        mu = jnp.mean(out, axis=-1, keepdims=True)
        cen = out - mu
        var = jnp.mean(cen * cen, axis=-1, keepdims=True)
        y = cen * lax.rsqrt(var + GN_EPS)
        o_ref[:, cols] = (y * _silu(g_ref[:, cols].astype(F32))).astype(BF16)


def _retention_consts():
    h = np.arange(R_HEADS, dtype=np.float64)
    log_g = np.log(1.0 - np.exp2(-5.0 - h))
    n = np.arange(R_CHUNK, dtype=np.float64)
    diff = n[:, None] - n[None, :]
    scale = R_QK_DIM ** -0.5
    decay = np.where(diff >= 0, np.exp(np.maximum(diff, 0.0) * log_g[:, None, None]), 0.0) * scale
    q_decay = np.exp((n + 1.0) * log_g[:, None])[:, :, None]
    k_decay = np.exp((R_CHUNK - 1.0 - n) * log_g[:, None])[:, None, :] * scale
    chunk_decay = np.exp(R_CHUNK * log_g)
    return (jnp.asarray(decay, F32), jnp.asarray(q_decay, F32), jnp.asarray(k_decay, F32),
            jnp.asarray(chunk_decay, F32))


def _retention(proj3):
    bsz, seq, _ = proj3.shape
    c = R_CHUNK
    decay, qdec, kdec, cdec = _retention_consts()
    grid_spec = pltpu.PrefetchScalarGridSpec(
        num_scalar_prefetch=1,
        grid=(bsz, seq // c),
        in_specs=[pl.BlockSpec((None, c, R_QK_WIDTH), lambda b, i, s: (b, i, COL_QR // R_QK_WIDTH)),
                  pl.BlockSpec((None, c, R_QK_WIDTH), lambda b, i, s: (b, i, COL_KR // R_QK_WIDTH)),
                  pl.BlockSpec((None, c, R_V_WIDTH), lambda b, i, s: (b, i, COL_VR // R_V_WIDTH)),
                  pl.BlockSpec((None, c, R_V_WIDTH), lambda b, i, s: (b, i, COL_GR // R_V_WIDTH)),
                  pl.BlockSpec((R_HEADS, c, c), lambda b, i, s: (0, 0, 0)),
                  pl.BlockSpec((R_HEADS, c, 1), lambda b, i, s: (0, 0, 0)),
                  pl.BlockSpec((R_HEADS, 1, c), lambda b, i, s: (0, 0, 0))],
        out_specs=pl.BlockSpec((None, c, R_V_WIDTH), lambda b, i, s: (b, i, 0)),
        scratch_shapes=[pltpu.VMEM((R_QK_WIDTH, R_V_DIM), F32)],
    )
    return pl.pallas_call(
        _retention_kernel,
        grid_spec=grid_spec,
        out_shape=jax.ShapeDtypeStruct((bsz, seq, R_V_WIDTH), BF16),
        compiler_params=_cparams(("parallel", "arbitrary")),
        name="retention",
    )(cdec, proj3, proj3, proj3, proj3, decay, qdec, kdec)


LANES = 128
ROW_SUBLANES = D_MODEL // LANES


def _store_tile_rows(ref, lead, x):
    m = x.shape[0]
    for j in range(ROW_SUBLANES):
        ref[lead + (pl.ds(j, m, stride=ROW_SUBLANES), slice(None))] = x[:, j * LANES:(j + 1) * LANES]


def _load_tile_rows(ref, lead, m):
    return [ref[lead + (pl.ds(j, m, stride=ROW_SUBLANES), slice(None))] for j in range(ROW_SUBLANES)]


def _mix_kernel(ya_ref, yr_ref, ga_ref, gt_ref, x_ref, wpa_ref, wpr_ref, wout_ref,
                gt1_ref, g_ref, sc_ref, sh_ref, x1_ref, h2_ref, h2r_ref):
    a = _dot(ya_ref[...], wpa_ref[...]) * _sigmoid(ga_ref[...].astype(F32))
    r = _dot(yr_ref[...], wpr_ref[...]) * _sigmoid(gt_ref[...].astype(F32))
    mix = _dot((a + r).astype(BF16), wout_ref[...])
    x1 = x_ref[...] + gt1_ref[0] * mix
    x1_ref[...] = x1
    ms = jnp.mean(x1 * x1, axis=-1, keepdims=True)
    y = x1 * lax.rsqrt(ms + NORM_EPS) * g_ref[...]
    h2 = y * (1.0 + sc_ref[0]) + sh_ref[0]
    h2_ref[...] = h2.astype(BF16)
    _store_tile_rows(h2r_ref, (), h2)


def _mix(ya, yr, proj, x2d, wpa, wpr, wout, gt1, g, sc, sh, seq):
    t, d = x2d.shape
    tm = min(512, seq)
    per_b = seq // tm
    row = lambda i: (i, 0)
    full = lambda i: (0, 0)
    per_batch = lambda i: (i // per_b, 0, 0)
    return pl.pallas_call(
        _mix_kernel,
        grid=(t // tm,),
        in_specs=[pl.BlockSpec((tm, A_WIDTH), row),
                  pl.BlockSpec((tm, R_V_WIDTH), row),
                  pl.BlockSpec((tm, d), lambda i: (i, COL_GA // D_MODEL)),
                  pl.BlockSpec((tm, d), lambda i: (i, COL_GT // D_MODEL)),
                  pl.BlockSpec((tm, d), row),
                  pl.BlockSpec((A_WIDTH, d), full),
                  pl.BlockSpec((R_V_WIDTH, d), full),
                  pl.BlockSpec((d, d), full),
                  pl.BlockSpec((1, 1, d), per_batch),
                  pl.BlockSpec((1, d), full),
                  pl.BlockSpec((1, 1, d), per_batch),
                  pl.BlockSpec((1, 1, d), per_batch)],
        out_specs=[pl.BlockSpec((tm, d), row), pl.BlockSpec((tm, d), row),
                   pl.BlockSpec((tm * ROW_SUBLANES, LANES), row)],
        out_shape=[jax.ShapeDtypeStruct((t, d), F32), jax.ShapeDtypeStruct((t, d), BF16),
                   jax.ShapeDtypeStruct((t * ROW_SUBLANES, LANES), F32)],
        compiler_params=_cparams(("parallel",)),
        name="merge_outproj_norm",
    )(ya, yr, proj, proj, x2d, wpa, wpr, wout, gt1, g, sc, sh)


def _router_kernel(h_ref, wr_ref, b_ref, e_ref, w_ref, r_ref, c_ref):
    logits = _dot_nt(wr_ref[...], h_ref[...])
    scores = _sigmoid(logits)
    choice = scores + b_ref[...]
    tm = logits.shape[1]
    giota = lax.broadcasted_iota(jnp.int32, (GROUP_SIZE, tm), 0)
    gs_rows = []
    for g in range(N_GROUPS):
        cg = choice[g * GROUP_SIZE:(g + 1) * GROUP_SIZE, :]
        m1 = jnp.max(cg, axis=0, keepdims=True)
        i1 = jnp.min(jnp.where(cg == m1, giota, GROUP_SIZE), axis=0, keepdims=True)
        m2 = jnp.max(jnp.where(giota == i1, KNOCKED_OUT, cg), axis=0, keepdims=True)
        gs_rows.append(m1 + m2)
    gs = jnp.concatenate(gs_rows, axis=0)
    grow = lax.broadcasted_iota(jnp.int32, (N_GROUPS, tm), 0)
    gmask = jnp.zeros((N_GROUPS, tm), jnp.bool_)
    for _ in range(TOPK_GROUPS):
        mx = jnp.max(gs, axis=0, keepdims=True)
        ix = jnp.min(jnp.where(gs == mx, grow, N_GROUPS), axis=0, keepdims=True)
        hit = grow == ix
        gmask = jnp.logical_or(gmask, hit)
        gs = jnp.where(hit, KNOCKED_OUT, gs)
    gmask_f = jnp.where(gmask, 1.0, 0.0)
    masked = jnp.concatenate(
        [jnp.where(gmask_f[g:g + 1, :] > 0.5, choice[g * GROUP_SIZE:(g + 1) * GROUP_SIZE, :], NEG_INF)
         for g in range(N_GROUPS)], axis=0)
    erow = lax.broadcasted_iota(jnp.int32, (N_EXPERTS, tm), 0)
    idx_rows, w_rows = [], []
    chosen = jnp.zeros((N_EXPERTS, tm), F32)
    for _ in range(TOP_K):
        mx = jnp.max(masked, axis=0, keepdims=True)
        ix = jnp.min(jnp.where(masked == mx, erow, N_EXPERTS), axis=0, keepdims=True)
        hit = erow == ix
        w_rows.append(jnp.sum(jnp.where(hit, scores, 0.0), axis=0, keepdims=True))
        idx_rows.append(ix)
        chosen = jnp.where(hit, 1.0, chosen)
        masked = jnp.where(hit, KNOCKED_OUT, masked)
    w = jnp.concatenate(w_rows, axis=0)
    w = w / (jnp.sum(w, axis=0, keepdims=True) + 1e-20) * ROUTED_SCALE
    e_ref[...] = jnp.concatenate(idx_rows, axis=0)
    w_ref[...] = w
    chosen_b = chosen.astype(BF16)
    earlier = (lax.broadcasted_iota(jnp.int32, (tm, tm), 0)
               < lax.broadcasted_iota(jnp.int32, (tm, tm), 1)).astype(BF16)
    before = _dot(chosen_b, earlier)
    ranks = [jnp.sum(jnp.where(erow == ix, before, 0.0), axis=0, keepdims=True) for ix in idx_rows]
    r_ref[...] = jnp.concatenate(ranks, axis=0).astype(jnp.int32)
    c_ref[...] = _dot(chosen_b, jnp.ones((tm, 128), BF16))


ROUTER_ROWS = 512


def _router(h2, wr_t, bias_col):
    t, d = h2.shape
    tm = ROUTER_ROWS
    by_tile = lambda i: (0, i)
    return pl.pallas_call(
        _router_kernel,
        grid=(t // tm,),
        in_specs=[pl.BlockSpec((tm, d), lambda i: (i, 0)),
                  pl.BlockSpec((N_EXPERTS, d), lambda i: (0, 0)),
                  pl.BlockSpec((N_EXPERTS, 1), lambda i: (0, 0))],
        out_specs=[pl.BlockSpec((TOP_K, tm), by_tile), pl.BlockSpec((TOP_K, tm), by_tile),
                   pl.BlockSpec((TOP_K, tm), by_tile), pl.BlockSpec((N_EXPERTS, 128), by_tile)],
        out_shape=[jax.ShapeDtypeStruct((TOP_K, t), jnp.int32),
                   jax.ShapeDtypeStruct((TOP_K, t), F32),
                   jax.ShapeDtypeStruct((TOP_K, t), jnp.int32),
                   jax.ShapeDtypeStruct((N_EXPERTS, (t // tm) * 128), F32)],
        compiler_params=_cparams(("parallel",)),
        name="router_topk",
    )(h2, wr_t, bias_col)


def _pos_kernel(e_ref, r_ref, base_ref, p_ref):
    tm = e_ref.shape[1]
    erow = lax.broadcasted_iota(jnp.int32, (N_EXPERTS, tm), 0)
    base = base_ref[0]
    rows = [jnp.sum(jnp.where(erow == e_ref[k:k + 1, :], base, 0.0), axis=0, keepdims=True)
            for k in range(TOP_K)]
    p_ref[0] = jnp.concatenate(rows, axis=0).astype(jnp.int32) + r_ref[...]


MOVE_ROWS = 256


def _positions(eidx_t, rank_t, tile_base):
    t = eidx_t.shape[1]
    tm = min(MOVE_ROWS, t)
    per_router_tile = ROUTER_ROWS // tm
    return pl.pallas_call(
        _pos_kernel,
        grid=(t // tm,),
        in_specs=[pl.BlockSpec((TOP_K, tm), lambda i: (0, i)),
                  pl.BlockSpec((TOP_K, tm), lambda i: (0, i)),
                  pl.BlockSpec((1, N_EXPERTS, 1), lambda i: (i // per_router_tile, 0, 0))],
        out_specs=pl.BlockSpec((1, TOP_K, tm), lambda i: (i, 0, 0)),
        out_shape=jax.ShapeDtypeStruct((t // tm, TOP_K, tm), jnp.int32),
        compiler_params=_cparams(("parallel",)),
        name="slot_positions",
    )(eidx_t, rank_t, tile_base)


SLOT_ROWS = 256
PAD_CHUNKS = (128, 64, 32, 16, 8, 4, 2, 1)


def _dispatch_kernel(pad_start_ref, pad_len_ref, pos_hbm, h_ref, xs_hbm, pos_smem, zero_buf,
                     idx_sem, row_sem, pad_sem):
    i = pl.program_id(0)
    tm = h_ref.shape[0] // ROW_SUBLANES
    idx_copy = pltpu.make_async_copy(pos_hbm.at[i], pos_smem, idx_sem)
    idx_copy.start()

    def tile_rows(first, n):
        return pl.ds(pl.multiple_of(first * ROW_SUBLANES, ROW_SUBLANES), n * ROW_SUBLANES)

    @pl.when(i == 0)
    def _():
        zero_buf[...] = jnp.zeros_like(zero_buf)

        def pad_copies(e, wait):
            ptr = pad_start_ref[e]
            n = pad_len_ref[e]
            for chunk in PAD_CHUNKS:
                @pl.when((n & chunk) != 0)
                def _(ptr=ptr, chunk=chunk):
                    cp = pltpu.make_async_copy(zero_buf.at[tile_rows(0, chunk), :],
                                               xs_hbm.at[tile_rows(ptr, chunk), :], pad_sem)
                    if wait:
                        cp.wait()
                    else:
                        cp.start()
                ptr = ptr + (n & chunk)

        def issue(e, carry):
            pad_copies(e, False)
            return carry

        def drain(e, carry):
            pad_copies(e, True)
            return carry

        lax.fori_loop(0, N_EXPERTS, issue, 0)
        lax.fori_loop(0, N_EXPERTS, drain, 0)

    idx_copy.wait()

    def row_copy(k, t):
        return pltpu.make_async_copy(h_ref.at[tile_rows(t, 1), :],
                                     xs_hbm.at[tile_rows(pos_smem[k * tm + t], 1), :], row_sem)

    def issue_rows(t, carry):
        for k in range(TOP_K):
            row_copy(k, t).start(priority=k % 2)
        return carry

    def drain_rows(t, carry):
        for k in range(TOP_K):
            row_copy(k, t).wait()
        return carry

    lax.fori_loop(0, tm, issue_rows, 0, unroll=4)
    lax.fori_loop(0, tm, drain_rows, 0, unroll=4)


def _dispatch(pad_start, pad_len, pos2, h2r, n_rows):
    t = h2r.shape[0] // ROW_SUBLANES
    tm = pos2.shape[1] // TOP_K
    grid_spec = pltpu.PrefetchScalarGridSpec(
        num_scalar_prefetch=2,
        grid=(t // tm,),
        in_specs=[pl.BlockSpec(memory_space=pl.ANY),
                  pl.BlockSpec((tm * ROW_SUBLANES, LANES), lambda i, ps, pn: (i, 0))],
        out_specs=pl.BlockSpec(memory_space=pl.ANY),
        scratch_shapes=[pltpu.SMEM((TOP_K * tm,), jnp.int32),
                        pltpu.VMEM((PAD_CHUNKS[0] * ROW_SUBLANES, LANES), F32),
                        pltpu.SemaphoreType.DMA,
                        pltpu.SemaphoreType.DMA,
                        pltpu.SemaphoreType.DMA],
    )
    return pl.pallas_call(
        _dispatch_kernel,
        grid_spec=grid_spec,
        out_shape=jax.ShapeDtypeStruct((n_rows * ROW_SUBLANES, LANES), F32),
        compiler_params=_cparams(("arbitrary",)),
        name="dispatch_rows",
    )(pad_start, pad_len, pos2, h2r)


def _experts_kernel(blk_e_ref, nblk_ref, x_ref, w1_ref, w3_ref, w2_ref, y_ref, w1b, w3b, w2b):
    s = pl.program_id(0)

    @pl.when(s < nblk_ref[0])
    def _():
        @pl.when(jnp.logical_or(s == 0, blk_e_ref[s] != blk_e_ref[jnp.maximum(s - 1, 0)]))
        def _():
            w1b[...] = w1_ref[0].astype(BF16)
            w3b[...] = w3_ref[0].astype(BF16)
            w2b[...] = w2_ref[0].astype(BF16)

        x = jnp.concatenate([p.astype(BF16) for p in _load_tile_rows(x_ref, (), SLOT_ROWS)], axis=1)
        mid = (_silu(_dot(x, w1b[...])) * _dot(x, w3b[...])).astype(BF16)
        _store_tile_rows(y_ref, (), _dot(mid, w2b[...]))


def _experts(blk_e, nblk_used, xs, w1, w3, w2):
    n_rows = xs.shape[0] // ROW_SUBLANES
    d = D_MODEL
    block_rows = SLOT_ROWS * ROW_SUBLANES
    blk = lambda s, be, nb: (jnp.minimum(s, nb[0] - 1), 0)
    wblk = lambda s, be, nb: (be[jnp.minimum(s, nb[0] - 1)], 0, 0)
    grid_spec = pltpu.PrefetchScalarGridSpec(
        num_scalar_prefetch=2,
        grid=(n_rows // SLOT_ROWS,),
        in_specs=[pl.BlockSpec((block_rows, LANES), blk),
                  pl.BlockSpec((1, d, EXPERT_FF), wblk),
                  pl.BlockSpec((1, d, EXPERT_FF), wblk),
                  pl.BlockSpec((1, EXPERT_FF, d), wblk)],
        out_specs=pl.BlockSpec((block_rows, LANES), blk),
        scratch_shapes=[pltpu.VMEM((d, EXPERT_FF), BF16),
                        pltpu.VMEM((d, EXPERT_FF), BF16),
                        pltpu.VMEM((EXPERT_FF, d), BF16)],
    )
    return pl.pallas_call(
        _experts_kernel,
        grid_spec=grid_spec,
        out_shape=jax.ShapeDtypeStruct(xs.shape, F32),
        compiler_params=_cparams(("arbitrary",)),
        name="routed_experts",
    )(blk_e, nblk_used, xs, w1, w3, w2)


def _combine_kernel(pos_hbm, ys_hbm, w_ref, h_ref, x1_ref, ws1_ref, ws3_ref, ws2_ref, gt2_ref, g_ref, o_ref,
                    pos_smem, ybuf, idx_sem, row_sem):
    i = pl.program_id(0)
    tm, d = x1_ref.shape
    idx_copy = pltpu.make_async_copy(pos_hbm.at[i], pos_smem, idx_sem)
    idx_copy.start()
    idx_copy.wait()

    def tile_rows(first):
        return pl.ds(pl.multiple_of(first * ROW_SUBLANES, ROW_SUBLANES), ROW_SUBLANES)

    def row_copy(k, t):
        return pltpu.make_async_copy(ys_hbm.at[tile_rows(pos_smem[k * tm + t]), :],
                                     ybuf.at[k, tile_rows(t), :], row_sem)

    def issue_rows(t, carry):
        for k in range(TOP_K):
            row_copy(k, t).start(priority=k % 2)
        return carry

    def drain_rows(t, carry):
        for k in range(TOP_K):
            row_copy(k, t).wait()
        return carry

    lax.fori_loop(0, tm, issue_rows, 0, unroll=4)

    h = h_ref[...]
    mid = (_silu(_dot(h, ws1_ref[...])) * _dot(h, ws3_ref[...])).astype(BF16)
    shared = _dot(mid, ws2_ref[...])

    lax.fori_loop(0, tm, drain_rows, 0, unroll=4)

    w = w_ref[...]
    acc = None
    for k in range(TOP_K):
        wk = w[:, k:k + 1]
        pieces = [p * wk for p in _load_tile_rows(ybuf, (k,), tm)]
        acc = pieces if acc is None else [a + p for a, p in zip(acc, pieces)]
    routed = jnp.concatenate(acc, axis=1)
    x2 = x1_ref[...] + gt2_ref[0] * (routed + shared)
    ms = jnp.mean(x2 * x2, axis=-1, keepdims=True)
    o_ref[...] = x2 * lax.rsqrt(ms + NORM_EPS) * g_ref[...]


def _combine(pos2, ys, wts, h2, x1, ws1, ws3, ws2, gt2, g_final, seq):
    t, d = x1.shape
    tm = pos2.shape[1] // TOP_K
    per_b = seq // tm
    row = lambda i: (i, 0)
    full = lambda i: (0, 0)
    return pl.pallas_call(
        _combine_kernel,
        grid=(t // tm,),
        in_specs=[pl.BlockSpec(memory_space=pl.ANY),
                  pl.BlockSpec(memory_space=pl.ANY),
                  pl.BlockSpec((tm, TOP_K), row),
                  pl.BlockSpec((tm, d), row),
                  pl.BlockSpec((tm, d), row),
                  pl.BlockSpec((d, SHARED_FF), full),
                  pl.BlockSpec((d, SHARED_FF), full),
                  pl.BlockSpec((SHARED_FF, d), full),
                  pl.BlockSpec((1, 1, d), lambda i: (i // per_b, 0, 0)),
                  pl.BlockSpec((1, d), full)],
        out_specs=pl.BlockSpec((tm, d), row),
        out_shape=jax.ShapeDtypeStruct((t, d), F32),
        scratch_shapes=[pltpu.SMEM((TOP_K * tm,), jnp.int32),
                        pltpu.VMEM((TOP_K, tm * ROW_SUBLANES, LANES), F32),
                        pltpu.SemaphoreType.DMA,
                        pltpu.SemaphoreType.DMA],
        compiler_params=_cparams(("arbitrary",)),
        name="combine_shared_final",
    )(pos2, ys, wts, h2, x1, ws1, ws3, ws2, gt2, g_final)


def _slot_tables(cnt, t):
    ntiles = cnt.shape[1] // 128
    cnt_tile = cnt.reshape(N_EXPERTS, ntiles, 128)[:, :, 0].astype(jnp.int32)
    counts = jnp.sum(cnt_tile, axis=1)
    padded = (counts + SLOT_ROWS - 1) // SLOT_ROWS * SLOT_ROWS
    pstart = jnp.cumsum(padded) - padded
    tile_base = pstart[:, None] + jnp.cumsum(cnt_tile, axis=1) - cnt_tile
    n_blk = -(-(t * TOP_K) // SLOT_ROWS) + N_EXPERTS
    blk_end = jnp.cumsum(padded // SLOT_ROWS)
    blk_e = jnp.sum((blk_end[None, :] <= jnp.arange(n_blk)[:, None]).astype(jnp.int32), axis=1)
    blk_e = jnp.minimum(blk_e, N_EXPERTS - 1)
    return (blk_e, blk_end[-1:].astype(jnp.int32), pstart + counts, padded - counts,
            tile_base.T.astype(F32).reshape(ntiles, N_EXPERTS, 1), n_blk * SLOT_ROWS)


def _permute_in_cols(w_in):
    qa, ka, va, qr, kr, vr, gr, ga, gt = jnp.split(
        w_in, np.cumsum((A_WIDTH, A_WIDTH, A_WIDTH, R_QK_WIDTH, R_QK_WIDTH, R_V_WIDTH, R_V_WIDTH,
                         D_MODEL))[:].tolist(), axis=1)
    return jnp.concatenate([vr, gr, ga, gt, qa, ka, va, qr, kr], axis=1)


def kernel(x, c, w_ada, b_ada, g_mix, w_in, w_pa, w_pr, w_out, g_ffn, w_router, router_bias,
           w1, w3, w2, ws1, ws3, ws2, g_final):
    bsz, seq, d = x.shape
    t = bsz * seq
    depth = w_ada.shape[0]
    assert depth == 1, "the final norm is fused into the single layer's last kernel"
    slopes = jnp.exp2(-8.0 / A_HEADS * jnp.arange(1, A_HEADS + 1, dtype=F32))
    x2d = x.reshape(t, d)
    for l in range(depth):
        mod = _ada(c, w_ada[l], b_ada[l])
        sh1, sc1, gt1, sh2, sc2, gt2 = [m.reshape(bsz, 1, d) for m in jnp.split(mod, 6, axis=-1)]
        w_in_p = _permute_in_cols(w_in[l]).astype(BF16)
        proj = _inproj(x2d, g_mix[l].reshape(1, d), sc1, sh1, w_in_p, seq)
        proj3 = proj.reshape(bsz, seq, IN_COLS)
        k_aug, q_aug_t, v_aug_t = _moba_prep(proj3, slopes)
        ya = _moba_attn(k_aug, q_aug_t, v_aug_t).reshape(t, A_WIDTH)
        yr = _retention(proj3).reshape(t, R_V_WIDTH)
        x1, h2, h2r = _mix(ya, yr, proj, x2d, w_pa[l].astype(BF16), w_pr[l].astype(BF16),
                           w_out[l].astype(BF16), gt1, g_ffn[l].reshape(1, d), sc2, sh2, seq)
        eidx_t, wts_t, rank_t, cnt = _router(h2, w_router[l].T.astype(BF16),
                                             router_bias[l].reshape(N_EXPERTS, 1))
        blk_e, nblk_used, pad_start, pad_len, tile_base, n_rows = _slot_tables(cnt, t)
        pos3 = _positions(eidx_t, rank_t, tile_base)
        pos2 = pos3.reshape(pos3.shape[0], TOP_K * pos3.shape[2])
        xs = _dispatch(pad_start, pad_len, pos2, h2r, n_rows)
        ys = _experts(blk_e, nblk_used, xs, w1[l], w3[l], w2[l])
        x2d = _combine(pos2, ys, wts_t.T, h2, x1, ws1[l].astype(BF16), ws3[l].astype(BF16),
                       ws2[l].astype(BF16), gt2, g_final.reshape(1, d), seq)
    return x2d.reshape(bsz, seq, d)
```

```python
import functools

import jax
import jax.numpy as jnp
import numpy as np
from jax import lax
from jax.experimental import pallas as pl
from jax.experimental.pallas import tpu as pltpu

F32 = jnp.float32
BF16 = jnp.bfloat16

D_MODEL = 1024
A_HEADS = 8
A_HEAD_DIM = 64
A_WIDTH = A_HEADS * A_HEAD_DIM
MOBA_BLOCK = 256
MOBA_TOPK = 3
R_HEADS = 8
R_QK_DIM = 64
R_V_DIM = 128
R_QK_WIDTH = R_HEADS * R_QK_DIM
R_V_WIDTH = R_HEADS * R_V_DIM
R_CHUNK = 128
N_EXPERTS = 256
TOP_K = 8
N_GROUPS = 8
GROUP_SIZE = N_EXPERTS // N_GROUPS
TOPK_GROUPS = 4
EXPERT_FF = 256
SHARED_FF = 256
ROUTED_SCALE = 2.5
NORM_EPS = 1e-6
GN_EPS = 1e-6
NEG_INF = -1e30
KNOCKED_OUT = -3e38

COL_VR, COL_GR, COL_GA, COL_GT = 0, 1024, 2048, 3072
COL_QA, COL_KA, COL_VA, COL_QR, COL_KR = 4096, 4608, 5120, 5632, 6144
IN_COLS = 6656
AUG = 128
FEAT_BIAS = A_HEAD_DIM
FEAT_POS = A_HEAD_DIM + 32

VMEM_LIMIT = 56 * 1024 * 1024


def _cparams(sem, vmem=VMEM_LIMIT):
    return pltpu.CompilerParams(dimension_semantics=sem, vmem_limit_bytes=vmem)


def _dot(a, b):
    return jnp.dot(a, b, preferred_element_type=F32)


def _dot_nt(a, b):
    return lax.dot_general(a, b, (((1,), (1,)), ((), ())), preferred_element_type=F32)


def _sigmoid(x):
    return 1.0 / (1.0 + jnp.exp(-x))


def _silu(x):
    return x * _sigmoid(x)


def _ada_kernel(c_ref, w_ref, b_ref, o_ref):
    c = c_ref[...]
    s = _silu(c)
    s_hi = s.astype(BF16)
    s_lo = (s - s_hi.astype(F32)).astype(BF16)
    w = w_ref[...]
    w_hi = w.astype(BF16)
    w_lo = (w - w_hi.astype(F32)).astype(BF16)
    o_ref[...] = _dot(s_hi, w_hi) + _dot(s_hi, w_lo) + _dot(s_lo, w_hi) + b_ref[...]


def _ada(c, w_ada, b_ada):
    bsz, d = c.shape
    n = w_ada.shape[1]
    tn = 1024
    return pl.pallas_call(
        _ada_kernel,
        grid=(n // tn,),
        in_specs=[pl.BlockSpec((bsz, d), lambda j: (0, 0)),
                  pl.BlockSpec((d, tn), lambda j: (0, j)),
                  pl.BlockSpec((1, tn), lambda j: (0, j))],
        out_specs=pl.BlockSpec((bsz, tn), lambda j: (0, j)),
        out_shape=jax.ShapeDtypeStruct((bsz, n), F32),
        compiler_params=_cparams(("parallel",)),
        name="ada_mod",
    )(c, w_ada, b_ada.reshape(1, n))


INPROJ_COLS = 512


def _inproj_kernel(x_ref, g_ref, sc_ref, sh_ref, w_ref, o_ref):
    x = x_ref[...]
    ms = jnp.mean(x * x, axis=-1, keepdims=True)
    y = x * lax.rsqrt(ms + NORM_EPS) * g_ref[...]
    h = (y * (1.0 + sc_ref[0]) + sh_ref[0]).astype(BF16)
    for j in range(w_ref.shape[1] // INPROJ_COLS):
        cols = slice(j * INPROJ_COLS, (j + 1) * INPROJ_COLS)
        o_ref[:, cols] = _dot(h, w_ref[:, cols]).astype(BF16)


def _inproj(x2d, g, sc, sh, w_bf16, seq):
    t, d = x2d.shape
    n = w_bf16.shape[1]
    tm = min(512, seq)
    per_b = seq // tm
    return pl.pallas_call(
        _inproj_kernel,
        grid=(t // tm,),
        in_specs=[pl.BlockSpec((tm, d), lambda i: (i, 0)),
                  pl.BlockSpec((1, d), lambda i: (0, 0)),
                  pl.BlockSpec((1, 1, d), lambda i: (i // per_b, 0, 0)),
                  pl.BlockSpec((1, 1, d), lambda i: (i // per_b, 0, 0)),
                  pl.BlockSpec((d, n), lambda i: (0, 0))],
        out_specs=pl.BlockSpec((tm, n), lambda i: (i, 0)),
        out_shape=jax.ShapeDtypeStruct((t, n), BF16),
        compiler_params=_cparams(("parallel",)),
        name="norm_inproj",
    )(x2d, g, sc, sh, w_bf16)


def _moba_prep_kernel(slopes_ref, q_ref, k_ref, v_ref, ko_ref, qo_ref, vo_ref, kmean_scr):
    i = pl.program_id(1)
    nblk = kmean_scr.shape[0]
    width = q_ref.shape[1]

    @pl.when(i == 0)
    def _():
        kmean_scr[...] = jnp.zeros_like(kmean_scr)

    q = q_ref[...]
    k = k_ref[...]
    v = v_ref[...]
    kmean_scr[pl.ds(i, 1), :] = jnp.mean(k.astype(F32), axis=0, keepdims=True)

    eye = (lax.broadcasted_iota(jnp.int32, (width, width), 0)
           == lax.broadcasted_iota(jnp.int32, (width, width), 1)).astype(BF16)
    q_t = _dot_nt(eye, q)
    v_t = _dot_nt(eye, v)

    km = kmean_scr[...]
    km_rep = jnp.concatenate([km] * A_HEADS, axis=0)
    r_head = lax.broadcasted_iota(jnp.int32, km_rep.shape, 0) // nblk
    c_head = lax.broadcasted_iota(jnp.int32, km_rep.shape, 1) // A_HEAD_DIM
    km_bd = jnp.where(r_head == c_head, km_rep, 0.0)
    km_hi = km_bd.astype(BF16)
    km_lo = (km_bd - km_hi.astype(F32)).astype(BF16)
    q_t_b = q_t.astype(BF16)
    gate_all = _dot(km_hi, q_t_b) + _dot(km_lo, q_t_b)

    mb = q.shape[0]
    blk = lax.broadcasted_iota(jnp.int32, (nblk, mb), 0)
    lane_pos = lax.broadcasted_iota(jnp.int32, (16, mb), 1).astype(F32)
    row16 = lax.broadcasted_iota(jnp.int32, (16, mb), 0)
    key_pos = lax.broadcasted_iota(jnp.int32, (mb, AUG), 0).astype(F32)
    kcol = lax.broadcasted_iota(jnp.int32, (mb, AUG), 1)
    sel_r = lax.broadcasted_iota(jnp.int32, (width, AUG), 0)
    sel_c = lax.broadcasted_iota(jnp.int32, (width, AUG), 1)

    for h in range(A_HEADS):
        slope = slopes_ref[h]
        g = jnp.where(blk < i, gate_all[h * nblk:(h + 1) * nblk, :], NEG_INF)
        sel = jnp.zeros((nblk, mb), jnp.bool_)
        for r in range(MOBA_TOPK):
            m = jnp.max(g, axis=0, keepdims=True)
            idx = jnp.min(jnp.where(g == m, blk, nblk), axis=0, keepdims=True)
            hit = blk == idx
            sel = jnp.logical_or(sel, jnp.logical_and(hit, r < i))
            g = jnp.where(hit, KNOCKED_OUT, g)
        bias_t = jnp.where(sel, 0.0, NEG_INF)

        scale = A_HEAD_DIM ** -0.5
        qo_ref[0, h, 0:A_HEAD_DIM, :] = (q_t[h * A_HEAD_DIM:(h + 1) * A_HEAD_DIM, :] * scale).astype(BF16)
        qo_ref[0, h, FEAT_BIAS:FEAT_BIAS + nblk, :] = bias_t.astype(BF16)
        if nblk < 32:
            qo_ref[0, h, FEAT_BIAS + nblk:FEAT_POS, :] = jnp.zeros((32 - nblk, mb), BF16)
        blk_off = slope * (i * mb).astype(F32)
        pos_feat = jnp.where(row16 == 0, -slope * lane_pos,
                             jnp.where(row16 == 2, -blk_off,
                                       jnp.where(jnp.logical_or(row16 == 1, row16 == 3), 1.0, 0.0)))
        qo_ref[0, h, FEAT_POS:FEAT_POS + 16, :] = pos_feat.astype(BF16)
        qo_ref[0, h, FEAT_POS + 16:AUG, :] = jnp.zeros((AUG - FEAT_POS - 16, mb), BF16)

        vo_ref[0, h, 0:A_HEAD_DIM, :] = v_t[h * A_HEAD_DIM:(h + 1) * A_HEAD_DIM, :].astype(BF16)
        vo_ref[0, h, A_HEAD_DIM:A_HEAD_DIM + 16, :] = jnp.where(row16 == 0, 1.0, 0.0).astype(BF16)
        vo_ref[0, h, A_HEAD_DIM + 16:AUG, :] = jnp.zeros((AUG - A_HEAD_DIM - 16, mb), BF16)

        pick = jnp.where(jnp.logical_and(sel_r == sel_c + h * A_HEAD_DIM, sel_c < A_HEAD_DIM),
                         1.0, 0.0).astype(BF16)
        k_feat = jnp.where(
            jnp.logical_or(kcol == FEAT_BIAS + i, jnp.logical_or(kcol == FEAT_POS, kcol == FEAT_POS + 2)), 1.0,
            jnp.where(kcol == FEAT_POS + 1, slope * key_pos, jnp.where(kcol == FEAT_POS + 3, blk_off, 0.0)))
        ko_ref[0, h, :, :] = (_dot(k, pick) + k_feat).astype(BF16)


def _moba_prep(proj3, slopes):
    bsz, seq, _ = proj3.shape
    nblk = seq // MOBA_BLOCK
    mb = MOBA_BLOCK
    grid_spec = pltpu.PrefetchScalarGridSpec(
        num_scalar_prefetch=1,
        grid=(bsz, nblk),
        in_specs=[pl.BlockSpec((None, mb, A_WIDTH), lambda b, i, s: (b, i, COL_QA // A_WIDTH)),
                  pl.BlockSpec((None, mb, A_WIDTH), lambda b, i, s: (b, i, COL_KA // A_WIDTH)),
                  pl.BlockSpec((None, mb, A_WIDTH), lambda b, i, s: (b, i, COL_VA // A_WIDTH))],
        out_specs=[pl.BlockSpec((1, A_HEADS, mb, AUG), lambda b, i, s: (b, 0, i, 0)),
                   pl.BlockSpec((1, A_HEADS, AUG, mb), lambda b, i, s: (b, 0, 0, i)),
                   pl.BlockSpec((1, A_HEADS, AUG, mb), lambda b, i, s: (b, 0, 0, i))],
        scratch_shapes=[pltpu.VMEM((nblk, A_WIDTH), F32)],
    )
    return pl.pallas_call(
        _moba_prep_kernel,
        grid_spec=grid_spec,
        out_shape=[jax.ShapeDtypeStruct((bsz, A_HEADS, seq, AUG), BF16),
                   jax.ShapeDtypeStruct((bsz, A_HEADS, AUG, seq), BF16),
                   jax.ShapeDtypeStruct((bsz, A_HEADS, AUG, seq), BF16)],
        compiler_params=_cparams(("parallel", "arbitrary")),
        name="moba_prep",
    )(slopes, proj3, proj3, proj3)


def _moba_attn_kernel(q_ref, k_ref, v_ref, o_ref, s_a, s_b, *, group, n_groups):
    i = pl.program_id(2)
    mb = MOBA_BLOCK
    span = group * mb
    own = pl.multiple_of(i * mb, mb)
    key_i = lax.broadcasted_iota(jnp.int32, (mb, mb), 0)
    qry_i = lax.broadcasted_iota(jnp.int32, (mb, mb), 1)
    feat = lax.broadcasted_iota(jnp.int32, (AUG, mb), 0)
    is_bias = jnp.logical_and(feat >= FEAT_BIAS, feat < FEAT_POS)
    q_ts, carry0 = [], []
    for hh in range(2):
        q_t = q_ref[0, hh]
        q_ts.append(q_t)
        q_own = jnp.where(is_bias, jnp.zeros_like(q_t), q_t)
        s = _dot(k_ref[0, hh, pl.ds(own, mb), :], q_own)
        s = jnp.where(key_i <= qry_i, s, NEG_INF)
        m0 = jnp.max(s, axis=0, keepdims=True)
        p = jnp.exp(s - m0)
        carry0 += [m0, _dot(v_ref[0, hh, :, pl.ds(own, mb)], p.astype(BF16))]

    def scores(g, dst):
        start = pl.multiple_of(jnp.minimum(g, n_groups - 1) * span, span)
        for hh in range(2):
            dst[hh] = _dot(k_ref[0, hh, pl.ds(start, span), :], q_ts[hh])

    def absorb(g, src, carry):
        start = pl.multiple_of(g * span, span)
        new = []
        for hh in range(2):
            m, acc = carry[2 * hh], carry[2 * hh + 1]
            sb = src[hh]
            m_new = jnp.maximum(m, jnp.max(sb, axis=0, keepdims=True))
            pb = jnp.exp(sb - m_new)
            alpha = jnp.exp(m - m_new)
            acc = acc * alpha + _dot(v_ref[0, hh, :, pl.ds(start, span)], pb.astype(BF16))
            new += [m_new, acc]
        return tuple(new)

    def body(pair, carry):
        scores(2 * pair + 1, s_b)
        carry = absorb(2 * pair, s_a, carry)
        scores(2 * pair + 2, s_a)
        return absorb(2 * pair + 1, s_b, carry)

    scores(0, s_a)
    live_groups = (i + group - 1) // group
    res = lax.fori_loop(0, (live_groups + 1) // 2, body, tuple(carry0))
    outs = [res[2 * hh + 1][0:A_HEAD_DIM, :] / res[2 * hh + 1][A_HEAD_DIM:A_HEAD_DIM + 1, :] for hh in range(2)]
    o_t = jnp.concatenate(outs, axis=0).astype(BF16)
    eye = (key_i == qry_i).astype(BF16)
    o_ref[0] = _dot_nt(eye, o_t).astype(BF16)


def _moba_attn(k_aug, q_aug_t, v_aug_t):
    bsz, nh, seq, _ = k_aug.shape
    mb = MOBA_BLOCK
    group = min(2, seq // mb)
    n_groups = seq // (group * mb)
    return pl.pallas_call(
        functools.partial(_moba_attn_kernel, group=group, n_groups=n_groups),
        grid=(bsz, nh // 2, seq // mb),
        in_specs=[pl.BlockSpec((1, 2, AUG, mb), lambda b, h, i: (b, h, 0, i)),
                  pl.BlockSpec((1, 2, seq, AUG), lambda b, h, i: (b, h, 0, 0)),
                  pl.BlockSpec((1, 2, AUG, seq), lambda b, h, i: (b, h, 0, 0))],
        out_specs=pl.BlockSpec((1, mb, 2 * A_HEAD_DIM), lambda b, h, i: (b, i, h)),
        out_shape=jax.ShapeDtypeStruct((bsz, seq, A_WIDTH), BF16),
        scratch_shapes=[pltpu.VMEM((2, group * mb, mb), F32), pltpu.VMEM((2, group * mb, mb), F32)],
        compiler_params=_cparams(("parallel", "parallel", "arbitrary")),
        name="moba_attn",
    )(q_aug_t, k_aug, v_aug_t)


def _retention_kernel(cdec_ref, q_ref, k_ref, v_ref, g_ref, decay_ref, qdec_ref, kdec_ref, o_ref, state_scr):
    @pl.when(pl.program_id(1) == 0)
    def _():
        state_scr[...] = jnp.zeros_like(state_scr)

    q = q_ref[...]
    k = k_ref[...]
    width = q.shape[1]
    eye = (lax.broadcasted_iota(jnp.int32, (width, width), 0)
           == lax.broadcasted_iota(jnp.int32, (width, width), 1)).astype(BF16)
    k_t = _dot_nt(eye, k)
    k_t_b = k_t.astype(BF16)
    state_b = state_scr[...].astype(BF16)
    for h in range(R_HEADS):
        rows = slice(h * R_QK_DIM, (h + 1) * R_QK_DIM)
        cols = slice(h * R_V_DIM, (h + 1) * R_V_DIM)
        q_h = q[:, rows]
        v_h = v_ref[:, cols]
        inner = _dot(q_h, k_t_b[rows, :]) * decay_ref[h]
        out = _dot(inner.astype(BF16), v_h) + _dot(q_h, state_b[rows, :]) * qdec_ref[h]
        k_dec = (k_t[rows, :] * kdec_ref[h]).astype(BF16)
        state_scr[rows, :] = cdec_ref[h] * state_scr[rows, :] + _dot(k_dec, v_h)
        mu = jnp.mean(out, axis=-1, keepdims=True)
        cen = out - mu
        var = jnp.mean(cen * cen, axis=-1, keepdims=True)
        y = cen * lax.rsqrt(var + GN_EPS)
        o_ref[:, cols] = (y * _silu(g_ref[:, cols].astype(F32))).astype(BF16)


def _retention_consts():
    h = np.arange(R_HEADS, dtype=np.float64)
    log_g = np.log(1.0 - np.exp2(-5.0 - h))
    n = np.arange(R_CHUNK, dtype=np.float64)
    diff = n[:, None] - n[None, :]
    scale = R_QK_DIM ** -0.5
    decay = np.where(diff >= 0, np.exp(np.maximum(diff, 0.0) * log_g[:, None, None]), 0.0) * scale
    q_decay = np.exp((n + 1.0) * log_g[:, None])[:, :, None]
    k_decay = np.exp((R_CHUNK - 1.0 - n) * log_g[:, None])[:, None, :] * scale
    chunk_decay = np.exp(R_CHUNK * log_g)
    return (jnp.asarray(decay, F32), jnp.asarray(q_decay, F32), jnp.asarray(k_decay, F32),
            jnp.asarray(chunk_decay, F32))


def _retention(proj3):
    bsz, seq, _ = proj3.shape
    c = R_CHUNK
    decay, qdec, kdec, cdec = _retention_consts()
    grid_spec = pltpu.PrefetchScalarGridSpec(
        num_scalar_prefetch=1,
        grid=(bsz, seq // c),
        in_specs=[pl.BlockSpec((None, c, R_QK_WIDTH), lambda b, i, s: (b, i, COL_QR // R_QK_WIDTH)),
                  pl.BlockSpec((None, c, R_QK_WIDTH), lambda b, i, s: (b, i, COL_KR // R_QK_WIDTH)),
                  pl.BlockSpec((None, c, R_V_WIDTH), lambda b, i, s: (b, i, COL_VR // R_V_WIDTH)),
                  pl.BlockSpec((None, c, R_V_WIDTH), lambda b, i, s: (b, i, COL_GR // R_V_WIDTH)),
                  pl.BlockSpec((R_HEADS, c, c), lambda b, i, s: (0, 0, 0)),
                  pl.BlockSpec((R_HEADS, c, 1), lambda b, i, s: (0, 0, 0)),
                  pl.BlockSpec((R_HEADS, 1, c), lambda b, i, s: (0, 0, 0))],
        out_specs=pl.BlockSpec((None, c, R_V_WIDTH), lambda b, i, s: (b, i, 0)),
        scratch_shapes=[pltpu.VMEM((R_QK_WIDTH, R_V_DIM), F32)],
    )
    return pl.pallas_call(
        _retention_kernel,
        grid_spec=grid_spec,
        out_shape=jax.ShapeDtypeStruct((bsz, seq, R_V_WIDTH), BF16),
        compiler_params=_cparams(("parallel", "arbitrary")),
        name="retention",
    )(cdec, proj3, proj3, proj3, proj3, decay, qdec, kdec)


LANES = 128
ROW_SUBLANES = D_MODEL // LANES


def _store_tile_rows(ref, lead, x):
    m = x.shape[0]
    for j in range(ROW_SUBLANES):
        ref[lead + (pl.ds(j, m, stride=ROW_SUBLANES), slice(None))] = x[:, j * LANES:(j + 1) * LANES]


def _load_tile_rows(ref, lead, m):
    return [ref[lead + (pl.ds(j, m, stride=ROW_SUBLANES), slice(None))] for j in range(ROW_SUBLANES)]


def _mix_kernel(ya_ref, yr_ref, ga_ref, gt_ref, x_ref, wpa_ref, wpr_ref, wout_ref,
                gt1_ref, g_ref, sc_ref, sh_ref, x1_ref, h2_ref, h2r_ref):
    a = _dot(ya_ref[...], wpa_ref[...]) * _sigmoid(ga_ref[...].astype(F32))
    r = _dot(yr_ref[...], wpr_ref[...]) * _sigmoid(gt_ref[...].astype(F32))
    mix = _dot((a + r).astype(BF16), wout_ref[...])
    x1 = x_ref[...] + gt1_ref[0] * mix
    x1_ref[...] = x1
    ms = jnp.mean(x1 * x1, axis=-1, keepdims=True)
    y = x1 * lax.rsqrt(ms + NORM_EPS) * g_ref[...]
    h2 = y * (1.0 + sc_ref[0]) + sh_ref[0]
    h2_ref[...] = h2.astype(BF16)
    _store_tile_rows(h2r_ref, (), h2)


def _mix(ya, yr, proj, x2d, wpa, wpr, wout, gt1, g, sc, sh, seq):
    t, d = x2d.shape
    tm = min(512, seq)
    per_b = seq // tm
    row = lambda i: (i, 0)
    full = lambda i: (0, 0)
    per_batch = lambda i: (i // per_b, 0, 0)
    return pl.pallas_call(
        _mix_kernel,
        grid=(t // tm,),
        in_specs=[pl.BlockSpec((tm, A_WIDTH), row),
                  pl.BlockSpec((tm, R_V_WIDTH), row),
                  pl.BlockSpec((tm, d), lambda i: (i, COL_GA // D_MODEL)),
                  pl.BlockSpec((tm, d), lambda i: (i, COL_GT // D_MODEL)),
                  pl.BlockSpec((tm, d), row),
                  pl.BlockSpec((A_WIDTH, d), full),
                  pl.BlockSpec((R_V_WIDTH, d), full),
                  pl.BlockSpec((d, d), full),
                  pl.BlockSpec((1, 1, d), per_batch),
                  pl.BlockSpec((1, d), full),
                  pl.BlockSpec((1, 1, d), per_batch),
                  pl.BlockSpec((1, 1, d), per_batch)],
        out_specs=[pl.BlockSpec((tm, d), row), pl.BlockSpec((tm, d), row),
                   pl.BlockSpec((tm * ROW_SUBLANES, LANES), row)],
        out_shape=[jax.ShapeDtypeStruct((t, d), F32), jax.ShapeDtypeStruct((t, d), BF16),
                   jax.ShapeDtypeStruct((t * ROW_SUBLANES, LANES), F32)],
        compiler_params=_cparams(("parallel",)),
        name="merge_outproj_norm",
    )(ya, yr, proj, proj, x2d, wpa, wpr, wout, gt1, g, sc, sh)


def _router_kernel(h_ref, wr_ref, b_ref, e_ref, w_ref, r_ref, c_ref):
    logits = _dot_nt(wr_ref[...], h_ref[...])
    scores = _sigmoid(logits)
    choice = scores + b_ref[...]
    tm = logits.shape[1]
    giota = lax.broadcasted_iota(jnp.int32, (GROUP_SIZE, tm), 0)
    gs_rows = []
    for g in range(N_GROUPS):
        cg = choice[g * GROUP_SIZE:(g + 1) * GROUP_SIZE, :]
        m1 = jnp.max(cg, axis=0, keepdims=True)
        i1 = jnp.min(jnp.where(cg == m1, giota, GROUP_SIZE), axis=0, keepdims=True)
        m2 = jnp.max(jnp.where(giota == i1, KNOCKED_OUT, cg), axis=0, keepdims=True)
        gs_rows.append(m1 + m2)
    gs = jnp.concatenate(gs_rows, axis=0)
    grow = lax.broadcasted_iota(jnp.int32, (N_GROUPS, tm), 0)
    gmask = jnp.zeros((N_GROUPS, tm), jnp.bool_)
    for _ in range(TOPK_GROUPS):
        mx = jnp.max(gs, axis=0, keepdims=True)
        ix = jnp.min(jnp.where(gs == mx, grow, N_GROUPS), axis=0, keepdims=True)
        hit = grow == ix
        gmask = jnp.logical_or(gmask, hit)
        gs = jnp.where(hit, KNOCKED_OUT, gs)
    gmask_f = jnp.where(gmask, 1.0, 0.0)
    masked = jnp.concatenate(
        [jnp.where(gmask_f[g:g + 1, :] > 0.5, choice[g * GROUP_SIZE:(g + 1) * GROUP_SIZE, :], NEG_INF)
         for g in range(N_GROUPS)], axis=0)
    erow = lax.broadcasted_iota(jnp.int32, (N_EXPERTS, tm), 0)
    idx_rows, w_rows = [], []
    chosen = jnp.zeros((N_EXPERTS, tm), F32)
    for _ in range(TOP_K):
        mx = jnp.max(masked, axis=0, keepdims=True)
        ix = jnp.min(jnp.where(masked == mx, erow, N_EXPERTS), axis=0, keepdims=True)
        hit = erow == ix
        w_rows.append(jnp.sum(jnp.where(hit, scores, 0.0), axis=0, keepdims=True))
        idx_rows.append(ix)
        chosen = jnp.where(hit, 1.0, chosen)
        masked = jnp.where(hit, KNOCKED_OUT, masked)
    w = jnp.concatenate(w_rows, axis=0)
    w = w / (jnp.sum(w, axis=0, keepdims=True) + 1e-20) * ROUTED_SCALE
    e_ref[...] = jnp.concatenate(idx_rows, axis=0)
    w_ref[...] = w
    chosen_b = chosen.astype(BF16)
    earlier = (lax.broadcasted_iota(jnp.int32, (tm, tm), 0)
               < lax.broadcasted_iota(jnp.int32, (tm, tm), 1)).astype(BF16)
    before = _dot(chosen_b, earlier)
    ranks = [jnp.sum(jnp.where(erow == ix, before, 0.0), axis=0, keepdims=True) for ix in idx_rows]
    r_ref[...] = jnp.concatenate(ranks, axis=0).astype(jnp.int32)
    c_ref[...] = _dot(chosen_b, jnp.ones((tm, 128), BF16))


ROUTER_ROWS = 512


def _router(h2, wr_t, bias_col):
    t, d = h2.shape
    tm = ROUTER_ROWS
    by_tile = lambda i: (0, i)
    return pl.pallas_call(
        _router_kernel,
        grid=(t // tm,),
        in_specs=[pl.BlockSpec((tm, d), lambda i: (i, 0)),
                  pl.BlockSpec((N_EXPERTS, d), lambda i: (0, 0)),
                  pl.BlockSpec((N_EXPERTS, 1), lambda i: (0, 0))],
        out_specs=[pl.BlockSpec((TOP_K, tm), by_tile), pl.BlockSpec((TOP_K, tm), by_tile),
                   pl.BlockSpec((TOP_K, tm), by_tile), pl.BlockSpec((N_EXPERTS, 128), by_tile)],
        out_shape=[jax.ShapeDtypeStruct((TOP_K, t), jnp.int32),
                   jax.ShapeDtypeStruct((TOP_K, t), F32),
                   jax.ShapeDtypeStruct((TOP_K, t), jnp.int32),
                   jax.ShapeDtypeStruct((N_EXPERTS, (t // tm) * 128), F32)],
        compiler_params=_cparams(("parallel",)),
        name="router_topk",
    )(h2, wr_t, bias_col)


def _pos_kernel(e_ref, r_ref, base_ref, p_ref):
    tm = e_ref.shape[1]
    erow = lax.broadcasted_iota(jnp.int32, (N_EXPERTS, tm), 0)
    base = base_ref[0]
    rows = [jnp.sum(jnp.where(erow == e_ref[k:k + 1, :], base, 0.0), axis=0, keepdims=True)
            for k in range(TOP_K)]
    p_ref[0] = jnp.concatenate(rows, axis=0).astype(jnp.int32) + r_ref[...]


MOVE_ROWS = 256


def _positions(eidx_t, rank_t, tile_base):
    t = eidx_t.shape[1]
    tm = min(MOVE_ROWS, t)
    per_router_tile = ROUTER_ROWS // tm
    return pl.pallas_call(
        _pos_kernel,
        grid=(t // tm,),
        in_specs=[pl.BlockSpec((TOP_K, tm), lambda i: (0, i)),
                  pl.BlockSpec((TOP_K, tm), lambda i: (0, i)),
                  pl.BlockSpec((1, N_EXPERTS, 1), lambda i: (i // per_router_tile, 0, 0))],
        out_specs=pl.BlockSpec((1, TOP_K, tm), lambda i: (i, 0, 0)),
        out_shape=jax.ShapeDtypeStruct((t // tm, TOP_K, tm), jnp.int32),
        compiler_params=_cparams(("parallel",)),
        name="slot_positions",
    )(eidx_t, rank_t, tile_base)


SLOT_ROWS = 256
PAD_CHUNKS = (128, 64, 32, 16, 8, 4, 2, 1)


def _dispatch_kernel(pad_start_ref, pad_len_ref, pos_hbm, h_ref, xs_hbm, pos_smem, zero_buf,
                     idx_sem, row_sem, pad_sem):
    i = pl.program_id(0)
    tm = h_ref.shape[0] // ROW_SUBLANES
    idx_copy = pltpu.make_async_copy(pos_hbm.at[i], pos_smem, idx_sem)
    idx_copy.start()

    def tile_rows(first, n):
        return pl.ds(pl.multiple_of(first * ROW_SUBLANES, ROW_SUBLANES), n * ROW_SUBLANES)

    @pl.when(i == 0)
    def _():
        zero_buf[...] = jnp.zeros_like(zero_buf)

        def pad_copies(e, wait):
            ptr = pad_start_ref[e]
            n = pad_len_ref[e]
            for chunk in PAD_CHUNKS:
                @pl.when((n & chunk) != 0)
                def _(ptr=ptr, chunk=chunk):
                    cp = pltpu.make_async_copy(zero_buf.at[tile_rows(0, chunk), :],
                                               xs_hbm.at[tile_rows(ptr, chunk), :], pad_sem)
                    if wait:
                        cp.wait()
                    else:
                        cp.start()
                ptr = ptr + (n & chunk)

        def issue(e, carry):
            pad_copies(e, False)
            return carry

        def drain(e, carry):
            pad_copies(e, True)
            return carry

        lax.fori_loop(0, N_EXPERTS, issue, 0)
        lax.fori_loop(0, N_EXPERTS, drain, 0)

    idx_copy.wait()

    def row_copy(k, t):
        return pltpu.make_async_copy(h_ref.at[tile_rows(t, 1), :],
                                     xs_hbm.at[tile_rows(pos_smem[k * tm + t], 1), :], row_sem)

    def issue_rows(t, carry):
        for k in range(TOP_K):
            row_copy(k, t).start(priority=k % 2)
        return carry

    def drain_rows(t, carry):
        for k in range(TOP_K):
            row_copy(k, t).wait()
        return carry

    lax.fori_loop(0, tm, issue_rows, 0, unroll=4)
    lax.fori_loop(0, tm, drain_rows, 0, unroll=4)


def _dispatch(pad_start, pad_len, pos2, h2r, n_rows):
    t = h2r.shape[0] // ROW_SUBLANES
    tm = pos2.shape[1] // TOP_K
    grid_spec = pltpu.PrefetchScalarGridSpec(
        num_scalar_prefetch=2,
        grid=(t // tm,),
        in_specs=[pl.BlockSpec(memory_space=pl.ANY),
                  pl.BlockSpec((tm * ROW_SUBLANES, LANES), lambda i, ps, pn: (i, 0))],
        out_specs=pl.BlockSpec(memory_space=pl.ANY),
        scratch_shapes=[pltpu.SMEM((TOP_K * tm,), jnp.int32),
                        pltpu.VMEM((PAD_CHUNKS[0] * ROW_SUBLANES, LANES), F32),
                        pltpu.SemaphoreType.DMA,
                        pltpu.SemaphoreType.DMA,
                        pltpu.SemaphoreType.DMA],
    )
    return pl.pallas_call(
        _dispatch_kernel,
        grid_spec=grid_spec,
        out_shape=jax.ShapeDtypeStruct((n_rows * ROW_SUBLANES, LANES), F32),
        compiler_params=_cparams(("arbitrary",)),
        name="dispatch_rows",
    )(pad_start, pad_len, pos2, h2r)


def _experts_kernel(blk_e_ref, nblk_ref, x_ref, w1_ref, w3_ref, w2_ref, y_ref, w1b, w3b, w2b):
    s = pl.program_id(0)

    @pl.when(s < nblk_ref[0])
    def _():
        @pl.when(jnp.logical_or(s == 0, blk_e_ref[s] != blk_e_ref[jnp.maximum(s - 1, 0)]))
        def _():
            w1b[...] = w1_ref[0].astype(BF16)
            w3b[...] = w3_ref[0].astype(BF16)
            w2b[...] = w2_ref[0].astype(BF16)

        x = jnp.concatenate([p.astype(BF16) for p in _load_tile_rows(x_ref, (), SLOT_ROWS)], axis=1)
        mid = (_silu(_dot(x, w1b[...])) * _dot(x, w3b[...])).astype(BF16)
        _store_tile_rows(y_ref, (), _dot(mid, w2b[...]))


def _experts(blk_e, nblk_used, xs, w1, w3, w2):
    n_rows = xs.shape[0] // ROW_SUBLANES
    d = D_MODEL
    block_rows = SLOT_ROWS * ROW_SUBLANES
    blk = lambda s, be, nb: (jnp.minimum(s, nb[0] - 1), 0)
    wblk = lambda s, be, nb: (be[jnp.minimum(s, nb[0] - 1)], 0, 0)
    grid_spec = pltpu.PrefetchScalarGridSpec(
        num_scalar_prefetch=2,
        grid=(n_rows // SLOT_ROWS,),
        in_specs=[pl.BlockSpec((block_rows, LANES), blk),
                  pl.BlockSpec((1, d, EXPERT_FF), wblk),
                  pl.BlockSpec((1, d, EXPERT_FF), wblk),
                  pl.BlockSpec((1, EXPERT_FF, d), wblk)],
        out_specs=pl.BlockSpec((block_rows, LANES), blk),
        scratch_shapes=[pltpu.VMEM((d, EXPERT_FF), BF16),
                        pltpu.VMEM((d, EXPERT_FF), BF16),
                        pltpu.VMEM((EXPERT_FF, d), BF16)],
    )
    return pl.pallas_call(
        _experts_kernel,
        grid_spec=grid_spec,
        out_shape=jax.ShapeDtypeStruct(xs.shape, F32),
        compiler_params=_cparams(("arbitrary",)),
        name="routed_experts",
    )(blk_e, nblk_used, xs, w1, w3, w2)


def _combine_kernel(pos_hbm, ys_hbm, w_ref, h_ref, x1_ref, ws1_ref, ws3_ref, ws2_ref, gt2_ref, g_ref, o_ref,
                    pos_smem, ybuf, idx_sem, row_sem):
    i = pl.program_id(0)
    tm, d = x1_ref.shape
    per_tile = TOP_K * tm
    half = i % 2

    def tile_rows(first):
        return pl.ds(pl.multiple_of(first * ROW_SUBLANES, ROW_SUBLANES), ROW_SUBLANES)

    def row_copy(buf, k, t):
        slot = pos_smem[buf * per_tile + k * tm + t]
        return pltpu.make_async_copy(ys_hbm.at[tile_rows(slot), :], ybuf.at[buf, k, tile_rows(t), :],
                                     row_sem.at[buf])

    def start_gather(tile, buf):
        idx_copy = pltpu.make_async_copy(
            pos_hbm.at[tile], pos_smem.at[pl.ds(pl.multiple_of(buf * per_tile, per_tile), per_tile)], idx_sem)
        idx_copy.start()
        idx_copy.wait()

        def issue_rows(t, carry):
            for k in range(TOP_K):
                row_copy(buf, k, t).start(priority=k % 2)
            return carry

        lax.fori_loop(0, tm, issue_rows, 0, unroll=4)

    @pl.when(i == 0)
    def _():
        start_gather(0, 0)

    @pl.when(i + 1 < pl.num_programs(0))
    def _():
        start_gather(i + 1, 1 - half)

    h = h_ref[...]
    mid = (_silu(_dot(h, ws1_ref[...])) * _dot(h, ws3_ref[...])).astype(BF16)
    shared = _dot(mid, ws2_ref[...])

    def drain_rows(t, carry):
        for k in range(TOP_K):
            row_copy(half, k, t).wait()
        return carry

    lax.fori_loop(0, tm, drain_rows, 0, unroll=4)

    w = w_ref[...]
    pieces = []
    for j in range(ROW_SUBLANES):
        rows = pl.ds(j, tm, stride=ROW_SUBLANES)
        acc = ybuf[half, 0, rows, :] * w[:, 0:1]
        for k in range(1, TOP_K):
            acc = acc + ybuf[half, k, rows, :] * w[:, k:k + 1]
        pieces.append(acc)
    routed = jnp.concatenate(pieces, axis=1)
    x2 = x1_ref[...] + gt2_ref[0] * (routed + shared)
    ms = jnp.mean(x2 * x2, axis=-1, keepdims=True)
    o_ref[...] = x2 * lax.rsqrt(ms + NORM_EPS) * g_ref[...]


def _combine(pos2, ys, wts, h2, x1, ws1, ws3, ws2, gt2, g_final, seq):
    t, d = x1.shape
    tm = pos2.shape[1] // TOP_K
    per_b = seq // tm
    row = lambda i: (i, 0)
    full = lambda i: (0, 0)
    return pl.pallas_call(
        _combine_kernel,
        grid=(t // tm,),
        in_specs=[pl.BlockSpec(memory_space=pl.ANY),
                  pl.BlockSpec(memory_space=pl.ANY),
                  pl.BlockSpec((tm, TOP_K), row),
                  pl.BlockSpec((tm, d), row),
                  pl.BlockSpec((tm, d), row),
                  pl.BlockSpec((d, SHARED_FF), full),
                  pl.BlockSpec((d, SHARED_FF), full),
                  pl.BlockSpec((SHARED_FF, d), full),
                  pl.BlockSpec((1, 1, d), lambda i: (i // per_b, 0, 0)),
                  pl.BlockSpec((1, d), full)],
        out_specs=pl.BlockSpec((tm, d), row),
        out_shape=jax.ShapeDtypeStruct((t, d), F32),
        scratch_shapes=[pltpu.SMEM((2 * TOP_K * tm,), jnp.int32),
                        pltpu.VMEM((2, TOP_K, tm * ROW_SUBLANES, LANES), F32),
                        pltpu.SemaphoreType.DMA,
                        pltpu.SemaphoreType.DMA((2,))],
        compiler_params=_cparams(("arbitrary",)),
        name="combine_shared_final",
    )(pos2, ys, wts, h2, x1, ws1, ws3, ws2, gt2, g_final)


def _slot_tables(cnt, t):
    ntiles = cnt.shape[1] // 128
    cnt_tile = cnt.reshape(N_EXPERTS, ntiles, 128)[:, :, 0].astype(jnp.int32)
    counts = jnp.sum(cnt_tile, axis=1)
    padded = (counts + SLOT_ROWS - 1) // SLOT_ROWS * SLOT_ROWS
    pstart = jnp.cumsum(padded) - padded
    tile_base = pstart[:, None] + jnp.cumsum(cnt_tile, axis=1) - cnt_tile
    n_blk = -(-(t * TOP_K) // SLOT_ROWS) + N_EXPERTS
    blk_end = jnp.cumsum(padded // SLOT_ROWS)
    blk_e = jnp.sum((blk_end[None, :] <= jnp.arange(n_blk)[:, None]).astype(jnp.int32), axis=1)
    blk_e = jnp.minimum(blk_e, N_EXPERTS - 1)
    return (blk_e, blk_end[-1:].astype(jnp.int32), pstart + counts, padded - counts,
            tile_base.T.astype(F32).reshape(ntiles, N_EXPERTS, 1), n_blk * SLOT_ROWS)


def _permute_in_cols(w_in):
    qa, ka, va, qr, kr, vr, gr, ga, gt = jnp.split(
        w_in, np.cumsum((A_WIDTH, A_WIDTH, A_WIDTH, R_QK_WIDTH, R_QK_WIDTH, R_V_WIDTH, R_V_WIDTH,
                         D_MODEL))[:].tolist(), axis=1)
    return jnp.concatenate([vr, gr, ga, gt, qa, ka, va, qr, kr], axis=1)


def kernel(x, c, w_ada, b_ada, g_mix, w_in, w_pa, w_pr, w_out, g_ffn, w_router, router_bias,
           w1, w3, w2, ws1, ws3, ws2, g_final):
    bsz, seq, d = x.shape
    t = bsz * seq
    depth = w_ada.shape[0]
    assert depth == 1, "the final norm is fused into the single layer's last kernel"
    slopes = jnp.exp2(-8.0 / A_HEADS * jnp.arange(1, A_HEADS + 1, dtype=F32))
    x2d = x.reshape(t, d)
    for l in range(depth):
        mod = _ada(c, w_ada[l], b_ada[l])
        sh1, sc1, gt1, sh2, sc2, gt2 = [m.reshape(bsz, 1, d) for m in jnp.split(mod, 6, axis=-1)]
        w_in_p = _permute_in_cols(w_in[l]).astype(BF16)
        proj = _inproj(x2d, g_mix[l].reshape(1, d), sc1, sh1, w_in_p, seq)
        proj3 = proj.reshape(bsz, seq, IN_COLS)
        k_aug, q_aug_t, v_aug_t = _moba_prep(proj3, slopes)
        ya = _moba_attn(k_aug, q_aug_t, v_aug_t).reshape(t, A_WIDTH)
        yr = _retention(proj3).reshape(t, R_V_WIDTH)
        x1, h2, h2r = _mix(ya, yr, proj, x2d, w_pa[l].astype(BF16), w_pr[l].astype(BF16),
                           w_out[l].astype(BF16), gt1, g_ffn[l].reshape(1, d), sc2, sh2, seq)
        eidx_t, wts_t, rank_t, cnt = _router(h2, w_router[l].T.astype(BF16),
                                             router_bias[l].reshape(N_EXPERTS, 1))
        blk_e, nblk_used, pad_start, pad_len, tile_base, n_rows = _slot_tables(cnt, t)
        pos3 = _positions(eidx_t, rank_t, tile_base)
        pos2 = pos3.reshape(pos3.shape[0], TOP_K * pos3.shape[2])
        xs = _dispatch(pad_start, pad_len, pos2, h2r, n_rows)
        ys = _experts(blk_e, nblk_used, xs, w1[l], w3[l], w2[l])
        x2d = _combine(pos2, ys, wts_t.T, h2, x1, ws1[l].astype(BF16), ws3[l].astype(BF16),
                       ws2[l].astype(BF16), gt2, g_final.reshape(1, d), seq)
    return x2d.reshape(bsz, seq, d)
```

```python
import functools

import jax
import jax.numpy as jnp
import numpy as np
from jax import lax
from jax.experimental import pallas as pl
from jax.experimental.pallas import tpu as pltpu
from jax.experimental.pallas import tpu_sc as plsc

F32 = jnp.float32
BF16 = jnp.bfloat16

D_MODEL = 1024
A_HEADS = 8
A_HEAD_DIM = 64
A_WIDTH = A_HEADS * A_HEAD_DIM
MOBA_BLOCK = 256
MOBA_TOPK = 3
R_HEADS = 8
R_QK_DIM = 64
R_V_DIM = 128
R_QK_WIDTH = R_HEADS * R_QK_DIM
R_V_WIDTH = R_HEADS * R_V_DIM
R_CHUNK = 128
N_EXPERTS = 256
TOP_K = 8
N_GROUPS = 8
GROUP_SIZE = N_EXPERTS // N_GROUPS
TOPK_GROUPS = 4
EXPERT_FF = 256
SHARED_FF = 256
ROUTED_SCALE = 2.5
NORM_EPS = 1e-6
GN_EPS = 1e-6
NEG_INF = -1e30
KNOCKED_OUT = -3e38

COL_VR, COL_GR, COL_GA, COL_GT = 0, 1024, 2048, 3072
COL_QA, COL_KA, COL_VA, COL_QR, COL_KR = 4096, 4608, 5120, 5632, 6144
IN_COLS = 6656
AUG = 128
FEAT_BIAS = A_HEAD_DIM
FEAT_POS = A_HEAD_DIM + 32

VMEM_LIMIT = 56 * 1024 * 1024


def _cparams(sem, vmem=VMEM_LIMIT):
    return pltpu.CompilerParams(dimension_semantics=sem, vmem_limit_bytes=vmem)


def _dot(a, b):
    return jnp.dot(a, b, preferred_element_type=F32)


def _dot_nt(a, b):
    return lax.dot_general(a, b, (((1,), (1,)), ((), ())), preferred_element_type=F32)


def _sigmoid(x):
    return 1.0 / (1.0 + jnp.exp(-x))


def _silu(x):
    return x * _sigmoid(x)


def _ada_kernel(c_ref, w_ref, b_ref, o_ref):
    c = c_ref[...]
    s = _silu(c)
    s_hi = s.astype(BF16)
    s_lo = (s - s_hi.astype(F32)).astype(BF16)
    w = w_ref[...]
    w_hi = w.astype(BF16)
    w_lo = (w - w_hi.astype(F32)).astype(BF16)
    o_ref[...] = _dot(s_hi, w_hi) + _dot(s_hi, w_lo) + _dot(s_lo, w_hi) + b_ref[...]


def _ada(c, w_ada, b_ada):
    bsz, d = c.shape
    n = w_ada.shape[1]
    tn = 1024
    return pl.pallas_call(
        _ada_kernel,
        grid=(n // tn,),
        in_specs=[pl.BlockSpec((bsz, d), lambda j: (0, 0)),
                  pl.BlockSpec((d, tn), lambda j: (0, j)),
                  pl.BlockSpec((1, tn), lambda j: (0, j))],
        out_specs=pl.BlockSpec((bsz, tn), lambda j: (0, j)),
        out_shape=jax.ShapeDtypeStruct((bsz, n), F32),
        compiler_params=_cparams(("parallel",)),
        name="ada_mod",
    )(c, w_ada, b_ada.reshape(1, n))


INPROJ_COLS = 512


def _inproj_kernel(x_ref, g_ref, sc_ref, sh_ref, w_ref, o_ref):
    x = x_ref[...]
    ms = jnp.mean(x * x, axis=-1, keepdims=True)
    y = x * lax.rsqrt(ms + NORM_EPS) * g_ref[...]
    h = (y * (1.0 + sc_ref[0]) + sh_ref[0]).astype(BF16)
    for j in range(w_ref.shape[1] // INPROJ_COLS):
        cols = slice(j * INPROJ_COLS, (j + 1) * INPROJ_COLS)
        o_ref[:, cols] = _dot(h, w_ref[:, cols]).astype(BF16)


def _inproj(x2d, g, sc, sh, w_bf16, seq):
    t, d = x2d.shape
    n = w_bf16.shape[1]
    tm = min(512, seq)
    per_b = seq // tm
    return pl.pallas_call(
        _inproj_kernel,
        grid=(t // tm,),
        in_specs=[pl.BlockSpec((tm, d), lambda i: (i, 0)),
                  pl.BlockSpec((1, d), lambda i: (0, 0)),
                  pl.BlockSpec((1, 1, d), lambda i: (i // per_b, 0, 0)),
                  pl.BlockSpec((1, 1, d), lambda i: (i // per_b, 0, 0)),
                  pl.BlockSpec((d, n), lambda i: (0, 0))],
        out_specs=pl.BlockSpec((tm, n), lambda i: (i, 0)),
        out_shape=jax.ShapeDtypeStruct((t, n), BF16),
        compiler_params=_cparams(("parallel",)),
        name="norm_inproj",
    )(x2d, g, sc, sh, w_bf16)


def _moba_prep_kernel(slopes_ref, q_ref, k_ref, v_ref, ko_ref, qo_ref, vo_ref, kmean_scr):
    i = pl.program_id(1)
    nblk = kmean_scr.shape[0]
    width = q_ref.shape[1]

    @pl.when(i == 0)
    def _():
        kmean_scr[...] = jnp.zeros_like(kmean_scr)

    q = q_ref[...]
    k = k_ref[...]
    v = v_ref[...]
    kmean_scr[pl.ds(i, 1), :] = jnp.mean(k.astype(F32), axis=0, keepdims=True)

    eye = (lax.broadcasted_iota(jnp.int32, (width, width), 0)
           == lax.broadcasted_iota(jnp.int32, (width, width), 1)).astype(BF16)
    q_t = _dot_nt(eye, q)
    v_t = _dot_nt(eye, v)

    km = kmean_scr[...]
    km_rep = jnp.concatenate([km] * A_HEADS, axis=0)
    r_head = lax.broadcasted_iota(jnp.int32, km_rep.shape, 0) // nblk
    c_head = lax.broadcasted_iota(jnp.int32, km_rep.shape, 1) // A_HEAD_DIM
    km_bd = jnp.where(r_head == c_head, km_rep, 0.0)
    km_hi = km_bd.astype(BF16)
    km_lo = (km_bd - km_hi.astype(F32)).astype(BF16)
    q_t_b = q_t.astype(BF16)
    gate_all = _dot(km_hi, q_t_b) + _dot(km_lo, q_t_b)

    mb = q.shape[0]
    blk = lax.broadcasted_iota(jnp.int32, (nblk, mb), 0)
    lane_pos = lax.broadcasted_iota(jnp.int32, (16, mb), 1).astype(F32)
    row16 = lax.broadcasted_iota(jnp.int32, (16, mb), 0)
    key_pos = lax.broadcasted_iota(jnp.int32, (mb, AUG), 0).astype(F32)
    kcol = lax.broadcasted_iota(jnp.int32, (mb, AUG), 1)
    sel_r = lax.broadcasted_iota(jnp.int32, (width, AUG), 0)
    sel_c = lax.broadcasted_iota(jnp.int32, (width, AUG), 1)

    for h in range(A_HEADS):
        slope = slopes_ref[h]
        g = jnp.where(blk < i, gate_all[h * nblk:(h + 1) * nblk, :], NEG_INF)
        sel = jnp.zeros((nblk, mb), jnp.bool_)
        for r in range(MOBA_TOPK):
            m = jnp.max(g, axis=0, keepdims=True)
            idx = jnp.min(jnp.where(g == m, blk, nblk), axis=0, keepdims=True)
            hit = blk == idx
            sel = jnp.logical_or(sel, jnp.logical_and(hit, r < i))
            g = jnp.where(hit, KNOCKED_OUT, g)
        bias_t = jnp.where(sel, 0.0, NEG_INF)

        scale = A_HEAD_DIM ** -0.5
        qo_ref[0, h, 0:A_HEAD_DIM, :] = (q_t[h * A_HEAD_DIM:(h + 1) * A_HEAD_DIM, :] * scale).astype(BF16)
        qo_ref[0, h, FEAT_BIAS:FEAT_BIAS + nblk, :] = bias_t.astype(BF16)
        if nblk < 32:
            qo_ref[0, h, FEAT_BIAS + nblk:FEAT_POS, :] = jnp.zeros((32 - nblk, mb), BF16)
        blk_off = slope * (i * mb).astype(F32)
        pos_feat = jnp.where(row16 == 0, -slope * lane_pos,
                             jnp.where(row16 == 2, -blk_off,
                                       jnp.where(jnp.logical_or(row16 == 1, row16 == 3), 1.0, 0.0)))
        qo_ref[0, h, FEAT_POS:FEAT_POS + 16, :] = pos_feat.astype(BF16)
        qo_ref[0, h, FEAT_POS + 16:AUG, :] = jnp.zeros((AUG - FEAT_POS - 16, mb), BF16)

        vo_ref[0, h, 0:A_HEAD_DIM, :] = v_t[h * A_HEAD_DIM:(h + 1) * A_HEAD_DIM, :].astype(BF16)
        vo_ref[0, h, A_HEAD_DIM:A_HEAD_DIM + 16, :] = jnp.where(row16 == 0, 1.0, 0.0).astype(BF16)
        vo_ref[0, h, A_HEAD_DIM + 16:AUG, :] = jnp.zeros((AUG - A_HEAD_DIM - 16, mb), BF16)

        pick = jnp.where(jnp.logical_and(sel_r == sel_c + h * A_HEAD_DIM, sel_c < A_HEAD_DIM),
                         1.0, 0.0).astype(BF16)
        k_feat = jnp.where(
            jnp.logical_or(kcol == FEAT_BIAS + i, jnp.logical_or(kcol == FEAT_POS, kcol == FEAT_POS + 2)), 1.0,
            jnp.where(kcol == FEAT_POS + 1, slope * key_pos, jnp.where(kcol == FEAT_POS + 3, blk_off, 0.0)))
        ko_ref[0, h, :, :] = (_dot(k, pick) + k_feat).astype(BF16)


def _moba_prep(proj3, slopes):
    bsz, seq, _ = proj3.shape
    nblk = seq // MOBA_BLOCK
    mb = MOBA_BLOCK
    grid_spec = pltpu.PrefetchScalarGridSpec(
        num_scalar_prefetch=1,
        grid=(bsz, nblk),
        in_specs=[pl.BlockSpec((None, mb, A_WIDTH), lambda b, i, s: (b, i, COL_QA // A_WIDTH)),
                  pl.BlockSpec((None, mb, A_WIDTH), lambda b, i, s: (b, i, COL_KA // A_WIDTH)),
                  pl.BlockSpec((None, mb, A_WIDTH), lambda b, i, s: (b, i, COL_VA // A_WIDTH))],
        out_specs=[pl.BlockSpec((1, A_HEADS, mb, AUG), lambda b, i, s: (b, 0, i, 0)),
                   pl.BlockSpec((1, A_HEADS, AUG, mb), lambda b, i, s: (b, 0, 0, i)),
                   pl.BlockSpec((1, A_HEADS, AUG, mb), lambda b, i, s: (b, 0, 0, i))],
        scratch_shapes=[pltpu.VMEM((nblk, A_WIDTH), F32)],
    )
    return pl.pallas_call(
        _moba_prep_kernel,
        grid_spec=grid_spec,
        out_shape=[jax.ShapeDtypeStruct((bsz, A_HEADS, seq, AUG), BF16),
                   jax.ShapeDtypeStruct((bsz, A_HEADS, AUG, seq), BF16),
                   jax.ShapeDtypeStruct((bsz, A_HEADS, AUG, seq), BF16)],
        compiler_params=_cparams(("parallel", "arbitrary")),
        name="moba_prep",
    )(slopes, proj3, proj3, proj3)


def _moba_attn_kernel(q_ref, k_ref, v_ref, o_ref, s_a, s_b, *, group, n_groups):
    i = pl.program_id(2)
    mb = MOBA_BLOCK
    span = group * mb
    own = pl.multiple_of(i * mb, mb)
    key_i = lax.broadcasted_iota(jnp.int32, (mb, mb), 0)
    qry_i = lax.broadcasted_iota(jnp.int32, (mb, mb), 1)
    feat = lax.broadcasted_iota(jnp.int32, (AUG, mb), 0)
    is_bias = jnp.logical_and(feat >= FEAT_BIAS, feat < FEAT_POS)
    q_ts, carry0 = [], []
    for hh in range(2):
        q_t = q_ref[0, hh]
        q_ts.append(q_t)
        q_own = jnp.where(is_bias, jnp.zeros_like(q_t), q_t)
        s = _dot(k_ref[0, hh, pl.ds(own, mb), :], q_own)
        s = jnp.where(key_i <= qry_i, s, NEG_INF)
        m0 = jnp.max(s, axis=0, keepdims=True)
        p = jnp.exp(s - m0)
        carry0 += [m0, _dot(v_ref[0, hh, :, pl.ds(own, mb)], p.astype(BF16))]

    def scores(g, dst):
        start = pl.multiple_of(jnp.minimum(g, n_groups - 1) * span, span)
        for hh in range(2):
            dst[hh] = _dot(k_ref[0, hh, pl.ds(start, span), :], q_ts[hh])

    def absorb(g, src, carry):
        start = pl.multiple_of(g * span, span)
        new = []
        for hh in range(2):
            m, acc = carry[2 * hh], carry[2 * hh + 1]
            sb = src[hh]
            m_new = jnp.maximum(m, jnp.max(sb, axis=0, keepdims=True))
            pb = jnp.exp(sb - m_new)
            alpha = jnp.exp(m - m_new)
            acc = acc * alpha + _dot(v_ref[0, hh, :, pl.ds(start, span)], pb.astype(BF16))
            new += [m_new, acc]
        return tuple(new)

    def body(pair, carry):
        scores(2 * pair + 1, s_b)
        carry = absorb(2 * pair, s_a, carry)
        scores(2 * pair + 2, s_a)
        return absorb(2 * pair + 1, s_b, carry)

    scores(0, s_a)
    live_groups = (i + group - 1) // group
    res = lax.fori_loop(0, (live_groups + 1) // 2, body, tuple(carry0))
    outs = [res[2 * hh + 1][0:A_HEAD_DIM, :] / res[2 * hh + 1][A_HEAD_DIM:A_HEAD_DIM + 1, :] for hh in range(2)]
    o_t = jnp.concatenate(outs, axis=0).astype(BF16)
    eye = (key_i == qry_i).astype(BF16)
    o_ref[0] = _dot_nt(eye, o_t).astype(BF16)


def _moba_attn(k_aug, q_aug_t, v_aug_t):
    bsz, nh, seq, _ = k_aug.shape
    mb = MOBA_BLOCK
    group = min(2, seq // mb)
    n_groups = seq // (group * mb)
    return pl.pallas_call(
        functools.partial(_moba_attn_kernel, group=group, n_groups=n_groups),
        grid=(bsz, nh // 2, seq // mb),
        in_specs=[pl.BlockSpec((1, 2, AUG, mb), lambda b, h, i: (b, h, 0, i)),
                  pl.BlockSpec((1, 2, seq, AUG), lambda b, h, i: (b, h, 0, 0)),
                  pl.BlockSpec((1, 2, AUG, seq), lambda b, h, i: (b, h, 0, 0))],
        out_specs=pl.BlockSpec((1, mb, 2 * A_HEAD_DIM), lambda b, h, i: (b, i, h)),
        out_shape=jax.ShapeDtypeStruct((bsz, seq, A_WIDTH), BF16),
        scratch_shapes=[pltpu.VMEM((2, group * mb, mb), F32), pltpu.VMEM((2, group * mb, mb), F32)],
        compiler_params=_cparams(("parallel", "parallel", "arbitrary")),
        name="moba_attn",
    )(q_aug_t, k_aug, v_aug_t)


def _retention_kernel(cdec_ref, q_ref, k_ref, v_ref, g_ref, decay_ref, qdec_ref, kdec_ref, o_ref, state_scr):
    @pl.when(pl.program_id(1) == 0)
    def _():
        state_scr[...] = jnp.zeros_like(state_scr)

    q = q_ref[...]
    k = k_ref[...]
    width = q.shape[1]
    eye = (lax.broadcasted_iota(jnp.int32, (width, width), 0)
           == lax.broadcasted_iota(jnp.int32, (width, width), 1)).astype(BF16)
    k_t = _dot_nt(eye, k)
    k_t_b = k_t.astype(BF16)
    state_b = state_scr[...].astype(BF16)
    for h in range(R_HEADS):
        rows = slice(h * R_QK_DIM, (h + 1) * R_QK_DIM)
        cols = slice(h * R_V_DIM, (h + 1) * R_V_DIM)
        q_h = q[:, rows]
        v_h = v_ref[:, cols]
        inner = _dot(q_h, k_t_b[rows, :]) * decay_ref[h]
        out = _dot(inner.astype(BF16), v_h) + _dot(q_h, state_b[rows, :]) * qdec_ref[h]
        k_dec = (k_t[rows, :] * kdec_ref[h]).astype(BF16)
        state_scr[rows, :] = cdec_ref[h] * state_scr[rows, :] + _dot(k_dec, v_h)
        mu = jnp.mean(out, axis=-1, keepdims=True)
        cen = out - mu
        var = jnp.mean(cen * cen, axis=-1, keepdims=True)
        y = cen * lax.rsqrt(var + GN_EPS)
        o_ref[:, cols] = (y * _silu(g_ref[:, cols].astype(F32))).astype(BF16)


def _retention_consts():
    h = np.arange(R_HEADS, dtype=np.float64)
    log_g = np.log(1.0 - np.exp2(-5.0 - h))
    n = np.arange(R_CHUNK, dtype=np.float64)
    diff = n[:, None] - n[None, :]
    scale = R_QK_DIM ** -0.5
    decay = np.where(diff >= 0, np.exp(np.maximum(diff, 0.0) * log_g[:, None, None]), 0.0) * scale
    q_decay = np.exp((n + 1.0) * log_g[:, None])[:, :, None]
    k_decay = np.exp((R_CHUNK - 1.0 - n) * log_g[:, None])[:, None, :] * scale
    chunk_decay = np.exp(R_CHUNK * log_g)
    return (jnp.asarray(decay, F32), jnp.asarray(q_decay, F32), jnp.asarray(k_decay, F32),
            jnp.asarray(chunk_decay, F32))


def _retention(proj3):
    bsz, seq, _ = proj3.shape
    c = R_CHUNK
    decay, qdec, kdec, cdec = _retention_consts()
    grid_spec = pltpu.PrefetchScalarGridSpec(
        num_scalar_prefetch=1,
        grid=(bsz, seq // c),
        in_specs=[pl.BlockSpec((None, c, R_QK_WIDTH), lambda b, i, s: (b, i, COL_QR // R_QK_WIDTH)),
                  pl.BlockSpec((None, c, R_QK_WIDTH), lambda b, i, s: (b, i, COL_KR // R_QK_WIDTH)),
                  pl.BlockSpec((None, c, R_V_WIDTH), lambda b, i, s: (b, i, COL_VR // R_V_WIDTH)),
                  pl.BlockSpec((None, c, R_V_WIDTH), lambda b, i, s: (b, i, COL_GR // R_V_WIDTH)),
                  pl.BlockSpec((R_HEADS, c, c), lambda b, i, s: (0, 0, 0)),
                  pl.BlockSpec((R_HEADS, c, 1), lambda b, i, s: (0, 0, 0)),
                  pl.BlockSpec((R_HEADS, 1, c), lambda b, i, s: (0, 0, 0))],
        out_specs=pl.BlockSpec((None, c, R_V_WIDTH), lambda b, i, s: (b, i, 0)),
        scratch_shapes=[pltpu.VMEM((R_QK_WIDTH, R_V_DIM), F32)],
    )
    return pl.pallas_call(
        _retention_kernel,
        grid_spec=grid_spec,
        out_shape=jax.ShapeDtypeStruct((bsz, seq, R_V_WIDTH), BF16),
        compiler_params=_cparams(("parallel", "arbitrary")),
        name="retention",
    )(cdec, proj3, proj3, proj3, proj3, decay, qdec, kdec)


LANES = 128
ROW_SUBLANES = D_MODEL // LANES


def _store_tile_rows(ref, lead, x):
    m = x.shape[0]
    for j in range(ROW_SUBLANES):
        ref[lead + (pl.ds(j, m, stride=ROW_SUBLANES), slice(None))] = x[:, j * LANES:(j + 1) * LANES]


def _load_tile_rows(ref, lead, m):
    return [ref[lead + (pl.ds(j, m, stride=ROW_SUBLANES), slice(None))] for j in range(ROW_SUBLANES)]


def _mix_kernel(ya_ref, yr_ref, ga_ref, gt_ref, x_ref, wpa_ref, wpr_ref, wout_ref,
                gt1_ref, g_ref, sc_ref, sh_ref, x1_ref, h2_ref, h2r_ref):
    a = _dot(ya_ref[...], wpa_ref[...]) * _sigmoid(ga_ref[...].astype(F32))
    r = _dot(yr_ref[...], wpr_ref[...]) * _sigmoid(gt_ref[...].astype(F32))
    mix = _dot((a + r).astype(BF16), wout_ref[...])
    x1 = x_ref[...] + gt1_ref[0] * mix
    x1_ref[...] = x1
    ms = jnp.mean(x1 * x1, axis=-1, keepdims=True)
    y = x1 * lax.rsqrt(ms + NORM_EPS) * g_ref[...]
    h2 = y * (1.0 + sc_ref[0]) + sh_ref[0]
    h2_ref[...] = h2.astype(BF16)
    _store_tile_rows(h2r_ref, (), h2)


def _mix(ya, yr, proj, x2d, wpa, wpr, wout, gt1, g, sc, sh, seq):
    t, d = x2d.shape
    tm = min(512, seq)
    per_b = seq // tm
    row = lambda i: (i, 0)
    full = lambda i: (0, 0)
    per_batch = lambda i: (i // per_b, 0, 0)
    return pl.pallas_call(
        _mix_kernel,
        grid=(t // tm,),
        in_specs=[pl.BlockSpec((tm, A_WIDTH), row),
                  pl.BlockSpec((tm, R_V_WIDTH), row),
                  pl.BlockSpec((tm, d), lambda i: (i, COL_GA // D_MODEL)),
                  pl.BlockSpec((tm, d), lambda i: (i, COL_GT // D_MODEL)),
                  pl.BlockSpec((tm, d), row),
                  pl.BlockSpec((A_WIDTH, d), full),
                  pl.BlockSpec((R_V_WIDTH, d), full),
                  pl.BlockSpec((d, d), full),
                  pl.BlockSpec((1, 1, d), per_batch),
                  pl.BlockSpec((1, d), full),
                  pl.BlockSpec((1, 1, d), per_batch),
                  pl.BlockSpec((1, 1, d), per_batch)],
        out_specs=[pl.BlockSpec((tm, d), row), pl.BlockSpec((tm, d), row),
                   pl.BlockSpec((tm * ROW_SUBLANES, LANES), row)],
        out_shape=[jax.ShapeDtypeStruct((t, d), F32), jax.ShapeDtypeStruct((t, d), BF16),
                   jax.ShapeDtypeStruct((t * ROW_SUBLANES, LANES), F32)],
        compiler_params=_cparams(("parallel",)),
        name="merge_outproj_norm",
    )(ya, yr, proj, proj, x2d, wpa, wpr, wout, gt1, g, sc, sh)


def _router_kernel(h_ref, wr_ref, b_ref, e_ref, w_ref, r_ref, c_ref):
    logits = _dot_nt(wr_ref[...], h_ref[...])
    scores = _sigmoid(logits)
    choice = scores + b_ref[...]
    tm = logits.shape[1]
    giota = lax.broadcasted_iota(jnp.int32, (GROUP_SIZE, tm), 0)
    gs_rows = []
    for g in range(N_GROUPS):
        cg = choice[g * GROUP_SIZE:(g + 1) * GROUP_SIZE, :]
        m1 = jnp.max(cg, axis=0, keepdims=True)
        i1 = jnp.min(jnp.where(cg == m1, giota, GROUP_SIZE), axis=0, keepdims=True)
        m2 = jnp.max(jnp.where(giota == i1, KNOCKED_OUT, cg), axis=0, keepdims=True)
        gs_rows.append(m1 + m2)
    gs = jnp.concatenate(gs_rows, axis=0)
    grow = lax.broadcasted_iota(jnp.int32, (N_GROUPS, tm), 0)
    gmask = jnp.zeros((N_GROUPS, tm), jnp.bool_)
    for _ in range(TOPK_GROUPS):
        mx = jnp.max(gs, axis=0, keepdims=True)
        ix = jnp.min(jnp.where(gs == mx, grow, N_GROUPS), axis=0, keepdims=True)
        hit = grow == ix
        gmask = jnp.logical_or(gmask, hit)
        gs = jnp.where(hit, KNOCKED_OUT, gs)
    gmask_f = jnp.where(gmask, 1.0, 0.0)
    masked = jnp.concatenate(
        [jnp.where(gmask_f[g:g + 1, :] > 0.5, choice[g * GROUP_SIZE:(g + 1) * GROUP_SIZE, :], NEG_INF)
         for g in range(N_GROUPS)], axis=0)
    erow = lax.broadcasted_iota(jnp.int32, (N_EXPERTS, tm), 0)
    idx_rows, w_rows = [], []
    chosen = jnp.zeros((N_EXPERTS, tm), F32)
    for _ in range(TOP_K):
        mx = jnp.max(masked, axis=0, keepdims=True)
        ix = jnp.min(jnp.where(masked == mx, erow, N_EXPERTS), axis=0, keepdims=True)
        hit = erow == ix
        w_rows.append(jnp.sum(jnp.where(hit, scores, 0.0), axis=0, keepdims=True))
        idx_rows.append(ix)
        chosen = jnp.where(hit, 1.0, chosen)
        masked = jnp.where(hit, KNOCKED_OUT, masked)
    w = jnp.concatenate(w_rows, axis=0)
    w = w / (jnp.sum(w, axis=0, keepdims=True) + 1e-20) * ROUTED_SCALE
    e_ref[...] = jnp.concatenate(idx_rows, axis=0)
    w_ref[...] = w
    chosen_b = chosen.astype(BF16)
    earlier = (lax.broadcasted_iota(jnp.int32, (tm, tm), 0)
               < lax.broadcasted_iota(jnp.int32, (tm, tm), 1)).astype(BF16)
    before = _dot(chosen_b, earlier)
    ranks = [jnp.sum(jnp.where(erow == ix, before, 0.0), axis=0, keepdims=True) for ix in idx_rows]
    r_ref[...] = jnp.concatenate(ranks, axis=0).astype(jnp.int32)
    c_ref[...] = _dot(chosen_b, jnp.ones((tm, 128), BF16))


ROUTER_ROWS = 512


def _router(h2, wr_t, bias_col):
    t, d = h2.shape
    tm = ROUTER_ROWS
    by_tile = lambda i: (0, i)
    return pl.pallas_call(
        _router_kernel,
        grid=(t // tm,),
        in_specs=[pl.BlockSpec((tm, d), lambda i: (i, 0)),
                  pl.BlockSpec((N_EXPERTS, d), lambda i: (0, 0)),
                  pl.BlockSpec((N_EXPERTS, 1), lambda i: (0, 0))],
        out_specs=[pl.BlockSpec((TOP_K, tm), by_tile), pl.BlockSpec((TOP_K, tm), by_tile),
                   pl.BlockSpec((TOP_K, tm), by_tile), pl.BlockSpec((N_EXPERTS, 128), by_tile)],
        out_shape=[jax.ShapeDtypeStruct((TOP_K, t), jnp.int32),
                   jax.ShapeDtypeStruct((TOP_K, t), F32),
                   jax.ShapeDtypeStruct((TOP_K, t), jnp.int32),
                   jax.ShapeDtypeStruct((N_EXPERTS, (t // tm) * 128), F32)],
        compiler_params=_cparams(("parallel",)),
        name="router_topk",
    )(h2, wr_t, bias_col)


def _pos_kernel(e_ref, r_ref, base_ref, p_ref):
    tm = e_ref.shape[1]
    erow = lax.broadcasted_iota(jnp.int32, (N_EXPERTS, tm), 0)
    base = base_ref[0]
    rows = [jnp.sum(jnp.where(erow == e_ref[k:k + 1, :], base, 0.0), axis=0, keepdims=True)
            for k in range(TOP_K)]
    p_ref[0] = jnp.concatenate(rows, axis=0).astype(jnp.int32) + r_ref[...]


MOVE_ROWS = 256


def _positions(eidx_t, rank_t, tile_base):
    t = eidx_t.shape[1]
    tm = min(MOVE_ROWS, t)
    per_router_tile = ROUTER_ROWS // tm
    return pl.pallas_call(
        _pos_kernel,
        grid=(t // tm,),
        in_specs=[pl.BlockSpec((TOP_K, tm), lambda i: (0, i)),
                  pl.BlockSpec((TOP_K, tm), lambda i: (0, i)),
                  pl.BlockSpec((1, N_EXPERTS, 1), lambda i: (i // per_router_tile, 0, 0))],
        out_specs=pl.BlockSpec((1, TOP_K, tm), lambda i: (i, 0, 0)),
        out_shape=jax.ShapeDtypeStruct((t // tm, TOP_K, tm), jnp.int32),
        compiler_params=_cparams(("parallel",)),
        name="slot_positions",
    )(eidx_t, rank_t, tile_base)


SLOT_ROWS = 256
PAD_CHUNKS = (128, 64, 32, 16, 8, 4, 2, 1)


def _dispatch_kernel(pad_start_ref, pad_len_ref, pos_hbm, h_ref, xs_hbm, pos_smem, zero_buf,
                     idx_sem, row_sem, pad_sem):
    i = pl.program_id(0)
    tm = h_ref.shape[0] // ROW_SUBLANES
    idx_copy = pltpu.make_async_copy(pos_hbm.at[i], pos_smem, idx_sem)
    idx_copy.start()

    def tile_rows(first, n):
        return pl.ds(pl.multiple_of(first * ROW_SUBLANES, ROW_SUBLANES), n * ROW_SUBLANES)

    @pl.when(i == 0)
    def _():
        zero_buf[...] = jnp.zeros_like(zero_buf)

        def pad_copies(e, wait):
            ptr = pad_start_ref[e]
            n = pad_len_ref[e]
            for chunk in PAD_CHUNKS:
                @pl.when((n & chunk) != 0)
                def _(ptr=ptr, chunk=chunk):
                    cp = pltpu.make_async_copy(zero_buf.at[tile_rows(0, chunk), :],
                                               xs_hbm.at[tile_rows(ptr, chunk), :], pad_sem)
                    if wait:
                        cp.wait()
                    else:
                        cp.start()
                ptr = ptr + (n & chunk)

        def issue(e, carry):
            pad_copies(e, False)
            return carry

        def drain(e, carry):
            pad_copies(e, True)
            return carry

        lax.fori_loop(0, N_EXPERTS, issue, 0)
        lax.fori_loop(0, N_EXPERTS, drain, 0)

    idx_copy.wait()

    def row_copy(k, t):
        return pltpu.make_async_copy(h_ref.at[tile_rows(t, 1), :],
                                     xs_hbm.at[tile_rows(pos_smem[k * tm + t], 1), :], row_sem)

    def issue_rows(t, carry):
        for k in range(TOP_K):
            row_copy(k, t).start(priority=k % 2)
        return carry

    def drain_rows(t, carry):
        for k in range(TOP_K):
            row_copy(k, t).wait()
        return carry

    lax.fori_loop(0, tm, issue_rows, 0, unroll=4)
    lax.fori_loop(0, tm, drain_rows, 0, unroll=4)


def _dispatch(pad_start, pad_len, pos2, h2r, n_rows):
    t = h2r.shape[0] // ROW_SUBLANES
    tm = pos2.shape[1] // TOP_K
    grid_spec = pltpu.PrefetchScalarGridSpec(
        num_scalar_prefetch=2,
        grid=(t // tm,),
        in_specs=[pl.BlockSpec(memory_space=pl.ANY),
                  pl.BlockSpec((tm * ROW_SUBLANES, LANES), lambda i, ps, pn: (i, 0))],
        out_specs=pl.BlockSpec(memory_space=pl.ANY),
        scratch_shapes=[pltpu.SMEM((TOP_K * tm,), jnp.int32),
                        pltpu.VMEM((PAD_CHUNKS[0] * ROW_SUBLANES, LANES), F32),
                        pltpu.SemaphoreType.DMA,
                        pltpu.SemaphoreType.DMA,
                        pltpu.SemaphoreType.DMA],
    )
    return pl.pallas_call(
        _dispatch_kernel,
        grid_spec=grid_spec,
        out_shape=jax.ShapeDtypeStruct((n_rows * ROW_SUBLANES, LANES), F32),
        compiler_params=_cparams(("arbitrary",)),
        name="dispatch_rows",
    )(pad_start, pad_len, pos2, h2r)


SC_DISPATCH_TOKENS = 64


def _sc_dispatch(pos_blocks, h_rows, n_rows):
    info = plsc.get_sparse_core_info()
    n_cores = info.num_cores
    n_workers = n_cores * info.num_subcores
    t = h_rows.shape[0]
    chunk = SC_DISPATCH_TOKENS
    steps = t // (n_workers * chunk)
    mesh = plsc.VectorSubcoreMesh(core_axis_name="c", subcore_axis_name="s")

    @functools.partial(
        pl.kernel, mesh=mesh,
        out_type=jax.ShapeDtypeStruct((n_rows, ROW_SUBLANES, LANES), F32),
        scratch_types=[pltpu.VMEM((TOP_K, chunk), jnp.int32),
                       pltpu.VMEM((chunk, ROW_SUBLANES, LANES), F32),
                       pltpu.SemaphoreType.DMA],
        name="sc_dispatch_rows",
    )
    def scatter_rows(pos_hbm, h_hbm, out_hbm, idx_v, rows_v, sem):
        wid = lax.axis_index("s") * n_cores + lax.axis_index("c")

        @pl.loop(0, steps)
        def _(step):
            blk = wid * steps + step
            pltpu.sync_copy(pos_hbm.at[blk], idx_v)
            pltpu.sync_copy(h_hbm.at[pl.ds(blk * chunk, chunk)], rows_v)
            for k in range(TOP_K):
                pltpu.async_copy(rows_v, out_hbm.at[idx_v.at[k]], sem).wait()

    return scatter_rows(pos_blocks, h_rows)


def _experts_kernel(blk_e_ref, nblk_ref, x_ref, w1_ref, w3_ref, w2_ref, y_ref, w1b, w3b, w2b):
    s = pl.program_id(0)

    @pl.when(s < nblk_ref[0])
    def _():
        @pl.when(jnp.logical_or(s == 0, blk_e_ref[s] != blk_e_ref[jnp.maximum(s - 1, 0)]))
        def _():
            w1b[...] = w1_ref[0].astype(BF16)
            w3b[...] = w3_ref[0].astype(BF16)
            w2b[...] = w2_ref[0].astype(BF16)

        x = jnp.concatenate([p.astype(BF16) for p in _load_tile_rows(x_ref, (), SLOT_ROWS)], axis=1)
        mid = (_silu(_dot(x, w1b[...])) * _dot(x, w3b[...])).astype(BF16)
        _store_tile_rows(y_ref, (), _dot(mid, w2b[...]))


def _experts(blk_e, nblk_used, xs, w1, w3, w2):
    n_rows = xs.shape[0] // ROW_SUBLANES
    d = D_MODEL
    block_rows = SLOT_ROWS * ROW_SUBLANES
    blk = lambda s, be, nb: (jnp.minimum(s, nb[0] - 1), 0)
    wblk = lambda s, be, nb: (be[jnp.minimum(s, nb[0] - 1)], 0, 0)
    grid_spec = pltpu.PrefetchScalarGridSpec(
        num_scalar_prefetch=2,
        grid=(n_rows // SLOT_ROWS,),
        in_specs=[pl.BlockSpec((block_rows, LANES), blk),
                  pl.BlockSpec((1, d, EXPERT_FF), wblk),
                  pl.BlockSpec((1, d, EXPERT_FF), wblk),
                  pl.BlockSpec((1, EXPERT_FF, d), wblk)],
        out_specs=pl.BlockSpec((block_rows, LANES), blk),
        scratch_shapes=[pltpu.VMEM((d, EXPERT_FF), BF16),
                        pltpu.VMEM((d, EXPERT_FF), BF16),
                        pltpu.VMEM((EXPERT_FF, d), BF16)],
    )
    return pl.pallas_call(
        _experts_kernel,
        grid_spec=grid_spec,
        out_shape=jax.ShapeDtypeStruct(xs.shape, F32),
        compiler_params=_cparams(("arbitrary",)),
        name="routed_experts",
    )(blk_e, nblk_used, xs, w1, w3, w2)


def _combine_kernel(pos_hbm, ys_hbm, w_ref, h_ref, x1_ref, ws1_ref, ws3_ref, ws2_ref, gt2_ref, g_ref, o_ref,
                    pos_smem, ybuf, idx_sem, row_sem):
    i = pl.program_id(0)
    tm, d = x1_ref.shape
    idx_copy = pltpu.make_async_copy(pos_hbm.at[i], pos_smem, idx_sem)
    idx_copy.start()
    idx_copy.wait()

    def tile_rows(first):
        return pl.ds(pl.multiple_of(first * ROW_SUBLANES, ROW_SUBLANES), ROW_SUBLANES)

    def row_copy(k, t):
        return pltpu.make_async_copy(ys_hbm.at[tile_rows(pos_smem[k * tm + t]), :],
                                     ybuf.at[k, tile_rows(t), :], row_sem)

    def issue_rows(t, carry):
        for k in range(TOP_K):
            row_copy(k, t).start(priority=k % 2)
        return carry

    def drain_rows(t, carry):
        for k in range(TOP_K):
            row_copy(k, t).wait()
        return carry

    lax.fori_loop(0, tm, issue_rows, 0, unroll=4)

    h = h_ref[...]
    mid = (_silu(_dot(h, ws1_ref[...])) * _dot(h, ws3_ref[...])).astype(BF16)
    shared = _dot(mid, ws2_ref[...])

    lax.fori_loop(0, tm, drain_rows, 0, unroll=4)

    w = w_ref[...]
    w_cols = [jnp.broadcast_to(w[:, k:k + 1], (tm, LANES)) for k in range(TOP_K)]
    pieces = []
    for j in range(ROW_SUBLANES):
        rows = pl.ds(j, tm, stride=ROW_SUBLANES)
        acc = ybuf[0, rows, :] * w_cols[0]
        for k in range(1, TOP_K):
            acc = acc + ybuf[k, rows, :] * w_cols[k]
        pieces.append(acc)
    routed = jnp.concatenate(pieces, axis=1)
    x2 = x1_ref[...] + gt2_ref[0] * (routed + shared)
    ms = jnp.mean(x2 * x2, axis=-1, keepdims=True)
    o_ref[...] = x2 * lax.rsqrt(ms + NORM_EPS) * g_ref[...]


def _combine(pos2, ys, wts, h2, x1, ws1, ws3, ws2, gt2, g_final, seq):
    t, d = x1.shape
    tm = pos2.shape[1] // TOP_K
    per_b = seq // tm
    row = lambda i: (i, 0)
    full = lambda i: (0, 0)
    return pl.pallas_call(
        _combine_kernel,
        grid=(t // tm,),
        in_specs=[pl.BlockSpec(memory_space=pl.ANY),
                  pl.BlockSpec(memory_space=pl.ANY),
                  pl.BlockSpec((tm, TOP_K), row),
                  pl.BlockSpec((tm, d), row),
                  pl.BlockSpec((tm, d), row),
                  pl.BlockSpec((d, SHARED_FF), full),
                  pl.BlockSpec((d, SHARED_FF), full),
                  pl.BlockSpec((SHARED_FF, d), full),
                  pl.BlockSpec((1, 1, d), lambda i: (i // per_b, 0, 0)),
                  pl.BlockSpec((1, d), full)],
        out_specs=pl.BlockSpec((tm, d), row),
        out_shape=jax.ShapeDtypeStruct((t, d), F32),
        scratch_shapes=[pltpu.SMEM((TOP_K * tm,), jnp.int32),
                        pltpu.VMEM((TOP_K, tm * ROW_SUBLANES, LANES), F32),
                        pltpu.SemaphoreType.DMA,
                        pltpu.SemaphoreType.DMA],
        compiler_params=_cparams(("arbitrary",)),
        name="combine_shared_final",
    )(pos2, ys, wts, h2, x1, ws1, ws3, ws2, gt2, g_final)


def _slot_tables(cnt, t):
    ntiles = cnt.shape[1] // 128
    cnt_tile = cnt.reshape(N_EXPERTS, ntiles, 128)[:, :, 0].astype(jnp.int32)
    counts = jnp.sum(cnt_tile, axis=1)
    padded = (counts + SLOT_ROWS - 1) // SLOT_ROWS * SLOT_ROWS
    pstart = jnp.cumsum(padded) - padded
    tile_base = pstart[:, None] + jnp.cumsum(cnt_tile, axis=1) - cnt_tile
    n_blk = -(-(t * TOP_K) // SLOT_ROWS) + N_EXPERTS
    blk_end = jnp.cumsum(padded // SLOT_ROWS)
    blk_e = jnp.sum((blk_end[None, :] <= jnp.arange(n_blk)[:, None]).astype(jnp.int32), axis=1)
    blk_e = jnp.minimum(blk_e, N_EXPERTS - 1)
    return (blk_e, blk_end[-1:].astype(jnp.int32), pstart + counts, padded - counts,
            tile_base.T.astype(F32).reshape(ntiles, N_EXPERTS, 1), n_blk * SLOT_ROWS)


def _permute_in_cols(w_in):
    qa, ka, va, qr, kr, vr, gr, ga, gt = jnp.split(
        w_in, np.cumsum((A_WIDTH, A_WIDTH, A_WIDTH, R_QK_WIDTH, R_QK_WIDTH, R_V_WIDTH, R_V_WIDTH,
                         D_MODEL))[:].tolist(), axis=1)
    return jnp.concatenate([vr, gr, ga, gt, qa, ka, va, qr, kr], axis=1)


def kernel(x, c, w_ada, b_ada, g_mix, w_in, w_pa, w_pr, w_out, g_ffn, w_router, router_bias,
           w1, w3, w2, ws1, ws3, ws2, g_final):
    bsz, seq, d = x.shape
    t = bsz * seq
    depth = w_ada.shape[0]
    assert depth == 1, "the final norm is fused into the single layer's last kernel"
    slopes = jnp.exp2(-8.0 / A_HEADS * jnp.arange(1, A_HEADS + 1, dtype=F32))
    x2d = x.reshape(t, d)
    for l in range(depth):
        mod = _ada(c, w_ada[l], b_ada[l])
        sh1, sc1, gt1, sh2, sc2, gt2 = [m.reshape(bsz, 1, d) for m in jnp.split(mod, 6, axis=-1)]
        w_in_p = _permute_in_cols(w_in[l]).astype(BF16)
        proj = _inproj(x2d, g_mix[l].reshape(1, d), sc1, sh1, w_in_p, seq)
        proj3 = proj.reshape(bsz, seq, IN_COLS)
        k_aug, q_aug_t, v_aug_t = _moba_prep(proj3, slopes)
        ya = _moba_attn(k_aug, q_aug_t, v_aug_t).reshape(t, A_WIDTH)
        yr = _retention(proj3).reshape(t, R_V_WIDTH)
        x1, h2, h2r = _mix(ya, yr, proj, x2d, w_pa[l].astype(BF16), w_pr[l].astype(BF16),
                           w_out[l].astype(BF16), gt1, g_ffn[l].reshape(1, d), sc2, sh2, seq)
        eidx_t, wts_t, rank_t, cnt = _router(h2, w_router[l].T.astype(BF16),
                                             router_bias[l].reshape(N_EXPERTS, 1))
        blk_e, nblk_used, pad_start, pad_len, tile_base, n_rows = _slot_tables(cnt, t)
        pos3 = _positions(eidx_t, rank_t, tile_base)
        pos2 = pos3.reshape(pos3.shape[0], TOP_K * pos3.shape[2])
        pos_blocks = jnp.transpose(
            pos3.reshape(pos3.shape[0], TOP_K, -1, SC_DISPATCH_TOKENS), (0, 2, 1, 3)
        ).reshape(t // SC_DISPATCH_TOKENS, TOP_K, SC_DISPATCH_TOKENS)
        xs = _sc_dispatch(pos_blocks, h2r.reshape(t, ROW_SUBLANES, LANES), n_rows)
        xs = xs.reshape(n_rows * ROW_SUBLANES, LANES)
        ys = _experts(blk_e, nblk_used, xs, w1[l], w3[l], w2[l])
        x2d = _combine(pos2, ys, wts_t.T, h2, x1, ws1[l].astype(BF16), ws3[l].astype(BF16),
                       ws2[l].astype(BF16), gt2, g_final.reshape(1, d), seq)
    return x2d.reshape(bsz, seq, d)
```

```python
import functools

import jax
import jax.numpy as jnp
import numpy as np
from jax import lax
from jax.experimental import pallas as pl
from jax.experimental.pallas import tpu as pltpu
from jax.experimental.pallas import tpu_sc as plsc

F32 = jnp.float32
BF16 = jnp.bfloat16

D_MODEL = 1024
A_HEADS = 8
A_HEAD_DIM = 64
A_WIDTH = A_HEADS * A_HEAD_DIM
MOBA_BLOCK = 256
MOBA_TOPK = 3
R_HEADS = 8
R_QK_DIM = 64
R_V_DIM = 128
R_QK_WIDTH = R_HEADS * R_QK_DIM
R_V_WIDTH = R_HEADS * R_V_DIM
R_CHUNK = 128
N_EXPERTS = 256
TOP_K = 8
N_GROUPS = 8
GROUP_SIZE = N_EXPERTS // N_GROUPS
TOPK_GROUPS = 4
EXPERT_FF = 256
SHARED_FF = 256
ROUTED_SCALE = 2.5
NORM_EPS = 1e-6
GN_EPS = 1e-6
NEG_INF = -1e30
KNOCKED_OUT = -3e38

COL_VR, COL_GR, COL_GA, COL_GT = 0, 1024, 2048, 3072
COL_QA, COL_KA, COL_VA, COL_QR, COL_KR = 4096, 4608, 5120, 5632, 6144
IN_COLS = 6656
AUG = 128
FEAT_BIAS = A_HEAD_DIM
FEAT_POS = A_HEAD_DIM + 32

VMEM_LIMIT = 56 * 1024 * 1024


def _cparams(sem, vmem=VMEM_LIMIT):
    return pltpu.CompilerParams(dimension_semantics=sem, vmem_limit_bytes=vmem)


def _dot(a, b):
    return jnp.dot(a, b, preferred_element_type=F32)


def _dot_nt(a, b):
    return lax.dot_general(a, b, (((1,), (1,)), ((), ())), preferred_element_type=F32)


def _sigmoid(x):
    return 1.0 / (1.0 + jnp.exp(-x))


def _silu(x):
    return x * _sigmoid(x)


def _ada_kernel(c_ref, w_ref, b_ref, o_ref):
    c = c_ref[...]
    s = _silu(c)
    s_hi = s.astype(BF16)
    s_lo = (s - s_hi.astype(F32)).astype(BF16)
    w = w_ref[...]
    w_hi = w.astype(BF16)
    w_lo = (w - w_hi.astype(F32)).astype(BF16)
    o_ref[...] = _dot(s_hi, w_hi) + _dot(s_hi, w_lo) + _dot(s_lo, w_hi) + b_ref[...]


def _ada(c, w_ada, b_ada):
    bsz, d = c.shape
    n = w_ada.shape[1]
    tn = 1024
    return pl.pallas_call(
        _ada_kernel,
        grid=(n // tn,),
        in_specs=[pl.BlockSpec((bsz, d), lambda j: (0, 0)),
                  pl.BlockSpec((d, tn), lambda j: (0, j)),
                  pl.BlockSpec((1, tn), lambda j: (0, j))],
        out_specs=pl.BlockSpec((bsz, tn), lambda j: (0, j)),
        out_shape=jax.ShapeDtypeStruct((bsz, n), F32),
        compiler_params=_cparams(("parallel",)),
        name="ada_mod",
    )(c, w_ada, b_ada.reshape(1, n))


INPROJ_COLS = 512


def _inproj_kernel(x_ref, g_ref, sc_ref, sh_ref, w_ref, o_ref):
    x = x_ref[...]
    ms = jnp.mean(x * x, axis=-1, keepdims=True)
    y = x * lax.rsqrt(ms + NORM_EPS) * g_ref[...]
    h = (y * (1.0 + sc_ref[0]) + sh_ref[0]).astype(BF16)
    for j in range(w_ref.shape[1] // INPROJ_COLS):
        cols = slice(j * INPROJ_COLS, (j + 1) * INPROJ_COLS)
        o_ref[:, cols] = _dot(h, w_ref[:, cols]).astype(BF16)


def _inproj(x2d, g, sc, sh, w_bf16, seq):
    t, d = x2d.shape
    n = w_bf16.shape[1]
    tm = min(512, seq)
    per_b = seq // tm
    return pl.pallas_call(
        _inproj_kernel,
        grid=(t // tm,),
        in_specs=[pl.BlockSpec((tm, d), lambda i: (i, 0)),
                  pl.BlockSpec((1, d), lambda i: (0, 0)),
                  pl.BlockSpec((1, 1, d), lambda i: (i // per_b, 0, 0)),
                  pl.BlockSpec((1, 1, d), lambda i: (i // per_b, 0, 0)),
                  pl.BlockSpec((d, n), lambda i: (0, 0))],
        out_specs=pl.BlockSpec((tm, n), lambda i: (i, 0)),
        out_shape=jax.ShapeDtypeStruct((t, n), BF16),
        compiler_params=_cparams(("parallel",)),
        name="norm_inproj",
    )(x2d, g, sc, sh, w_bf16)


def _moba_prep_kernel(slopes_ref, q_ref, k_ref, v_ref, ko_ref, qo_ref, vo_ref, kmean_scr):
    i = pl.program_id(1)
    nblk = kmean_scr.shape[0]
    width = q_ref.shape[1]

    @pl.when(i == 0)
    def _():
        kmean_scr[...] = jnp.zeros_like(kmean_scr)

    q = q_ref[...]
    k = k_ref[...]
    v = v_ref[...]
    kmean_scr[pl.ds(i, 1), :] = jnp.mean(k.astype(F32), axis=0, keepdims=True)

    eye = (lax.broadcasted_iota(jnp.int32, (width, width), 0)
           == lax.broadcasted_iota(jnp.int32, (width, width), 1)).astype(BF16)
    q_t = _dot_nt(eye, q)
    v_t = _dot_nt(eye, v)

    km = kmean_scr[...]
    km_rep = jnp.concatenate([km] * A_HEADS, axis=0)
    r_head = lax.broadcasted_iota(jnp.int32, km_rep.shape, 0) // nblk
    c_head = lax.broadcasted_iota(jnp.int32, km_rep.shape, 1) // A_HEAD_DIM
    km_bd = jnp.where(r_head == c_head, km_rep, 0.0)
    km_hi = km_bd.astype(BF16)
    km_lo = (km_bd - km_hi.astype(F32)).astype(BF16)
    q_t_b = q_t.astype(BF16)
    gate_all = _dot(km_hi, q_t_b) + _dot(km_lo, q_t_b)

    mb = q.shape[0]
    blk = lax.broadcasted_iota(jnp.int32, (nblk, mb), 0)
    lane_pos = lax.broadcasted_iota(jnp.int32, (16, mb), 1).astype(F32)
    row16 = lax.broadcasted_iota(jnp.int32, (16, mb), 0)
    key_pos = lax.broadcasted_iota(jnp.int32, (mb, AUG), 0).astype(F32)
    kcol = lax.broadcasted_iota(jnp.int32, (mb, AUG), 1)
    sel_r = lax.broadcasted_iota(jnp.int32, (width, AUG), 0)
    sel_c = lax.broadcasted_iota(jnp.int32, (width, AUG), 1)

    for h in range(A_HEADS):
        slope = slopes_ref[h]
        g = jnp.where(blk < i, gate_all[h * nblk:(h + 1) * nblk, :], NEG_INF)
        sel = jnp.zeros((nblk, mb), jnp.bool_)
        for r in range(MOBA_TOPK):
            m = jnp.max(g, axis=0, keepdims=True)
            idx = jnp.min(jnp.where(g == m, blk, nblk), axis=0, keepdims=True)
            hit = blk == idx
            sel = jnp.logical_or(sel, jnp.logical_and(hit, r < i))
            g = jnp.where(hit, KNOCKED_OUT, g)
        bias_t = jnp.where(sel, 0.0, NEG_INF)

        scale = A_HEAD_DIM ** -0.5
        qo_ref[0, h, 0:A_HEAD_DIM, :] = (q_t[h * A_HEAD_DIM:(h + 1) * A_HEAD_DIM, :] * scale).astype(BF16)
        qo_ref[0, h, FEAT_BIAS:FEAT_BIAS + nblk, :] = bias_t.astype(BF16)
        if nblk < 32:
            qo_ref[0, h, FEAT_BIAS + nblk:FEAT_POS, :] = jnp.zeros((32 - nblk, mb), BF16)
        blk_off = slope * (i * mb).astype(F32)
        pos_feat = jnp.where(row16 == 0, -slope * lane_pos,
                             jnp.where(row16 == 2, -blk_off,
                                       jnp.where(jnp.logical_or(row16 == 1, row16 == 3), 1.0, 0.0)))
        qo_ref[0, h, FEAT_POS:FEAT_POS + 16, :] = pos_feat.astype(BF16)
        qo_ref[0, h, FEAT_POS + 16:AUG, :] = jnp.zeros((AUG - FEAT_POS - 16, mb), BF16)

        vo_ref[0, h, 0:A_HEAD_DIM, :] = v_t[h * A_HEAD_DIM:(h + 1) * A_HEAD_DIM, :].astype(BF16)
        vo_ref[0, h, A_HEAD_DIM:A_HEAD_DIM + 16, :] = jnp.where(row16 == 0, 1.0, 0.0).astype(BF16)
        vo_ref[0, h, A_HEAD_DIM + 16:AUG, :] = jnp.zeros((AUG - A_HEAD_DIM - 16, mb), BF16)

        pick = jnp.where(jnp.logical_and(sel_r == sel_c + h * A_HEAD_DIM, sel_c < A_HEAD_DIM),
                         1.0, 0.0).astype(BF16)
        k_feat = jnp.where(
            jnp.logical_or(kcol == FEAT_BIAS + i, jnp.logical_or(kcol == FEAT_POS, kcol == FEAT_POS + 2)), 1.0,
            jnp.where(kcol == FEAT_POS + 1, slope * key_pos, jnp.where(kcol == FEAT_POS + 3, blk_off, 0.0)))
        ko_ref[0, h, :, :] = (_dot(k, pick) + k_feat).astype(BF16)


def _moba_prep(proj3, slopes):
    bsz, seq, _ = proj3.shape
    nblk = seq // MOBA_BLOCK
    mb = MOBA_BLOCK
    grid_spec = pltpu.PrefetchScalarGridSpec(
        num_scalar_prefetch=1,
        grid=(bsz, nblk),
        in_specs=[pl.BlockSpec((None, mb, A_WIDTH), lambda b, i, s: (b, i, COL_QA // A_WIDTH)),
                  pl.BlockSpec((None, mb, A_WIDTH), lambda b, i, s: (b, i, COL_KA // A_WIDTH)),
                  pl.BlockSpec((None, mb, A_WIDTH), lambda b, i, s: (b, i, COL_VA // A_WIDTH))],
        out_specs=[pl.BlockSpec((1, A_HEADS, mb, AUG), lambda b, i, s: (b, 0, i, 0)),
                   pl.BlockSpec((1, A_HEADS, AUG, mb), lambda b, i, s: (b, 0, 0, i)),
                   pl.BlockSpec((1, A_HEADS, AUG, mb), lambda b, i, s: (b, 0, 0, i))],
        scratch_shapes=[pltpu.VMEM((nblk, A_WIDTH), F32)],
    )
    return pl.pallas_call(
        _moba_prep_kernel,
        grid_spec=grid_spec,
        out_shape=[jax.ShapeDtypeStruct((bsz, A_HEADS, seq, AUG), BF16),
                   jax.ShapeDtypeStruct((bsz, A_HEADS, AUG, seq), BF16),
                   jax.ShapeDtypeStruct((bsz, A_HEADS, AUG, seq), BF16)],
        compiler_params=_cparams(("parallel", "arbitrary")),
        name="moba_prep",
    )(slopes, proj3, proj3, proj3)


def _moba_attn_kernel(q_ref, k_ref, v_ref, o_ref, s_a, s_b, *, group, n_groups):
    i = pl.program_id(2)
    mb = MOBA_BLOCK
    span = group * mb
    own = pl.multiple_of(i * mb, mb)
    key_i = lax.broadcasted_iota(jnp.int32, (mb, mb), 0)
    qry_i = lax.broadcasted_iota(jnp.int32, (mb, mb), 1)
    feat = lax.broadcasted_iota(jnp.int32, (AUG, mb), 0)
    is_bias = jnp.logical_and(feat >= FEAT_BIAS, feat < FEAT_POS)
    q_ts, carry0 = [], []
    for hh in range(2):
        q_t = q_ref[0, hh]
        q_ts.append(q_t)
        q_own = jnp.where(is_bias, jnp.zeros_like(q_t), q_t)
        s = _dot(k_ref[0, hh, pl.ds(own, mb), :], q_own)
        s = jnp.where(key_i <= qry_i, s, NEG_INF)
        m0 = jnp.max(s, axis=0, keepdims=True)
        p = jnp.exp(s - m0)
        carry0 += [m0, _dot(v_ref[0, hh, :, pl.ds(own, mb)], p.astype(BF16))]

    def scores(g, dst):
        start = pl.multiple_of(jnp.minimum(g, n_groups - 1) * span, span)
        for hh in range(2):
            dst[hh] = _dot(k_ref[0, hh, pl.ds(start, span), :], q_ts[hh])

    def absorb(g, src, carry):
        start = pl.multiple_of(g * span, span)
        new = []
        for hh in range(2):
            m, acc = carry[2 * hh], carry[2 * hh + 1]
            sb = src[hh]
            m_new = jnp.maximum(m, jnp.max(sb, axis=0, keepdims=True))
            pb = jnp.exp(sb - m_new)
            alpha = jnp.exp(m - m_new)
            acc = acc * alpha + _dot(v_ref[0, hh, :, pl.ds(start, span)], pb.astype(BF16))
            new += [m_new, acc]
        return tuple(new)

    def body(pair, carry):
        scores(2 * pair + 1, s_b)
        carry = absorb(2 * pair, s_a, carry)
        scores(2 * pair + 2, s_a)
        return absorb(2 * pair + 1, s_b, carry)

    scores(0, s_a)
    live_groups = (i + group - 1) // group
    res = lax.fori_loop(0, (live_groups + 1) // 2, body, tuple(carry0))
    outs = [res[2 * hh + 1][0:A_HEAD_DIM, :] / res[2 * hh + 1][A_HEAD_DIM:A_HEAD_DIM + 1, :] for hh in range(2)]
    o_t = jnp.concatenate(outs, axis=0).astype(BF16)
    eye = (key_i == qry_i).astype(BF16)
    o_ref[0] = _dot_nt(eye, o_t).astype(BF16)


def _moba_attn(k_aug, q_aug_t, v_aug_t):
    bsz, nh, seq, _ = k_aug.shape
    mb = MOBA_BLOCK
    group = min(2, seq // mb)
    n_groups = seq // (group * mb)
    return pl.pallas_call(
        functools.partial(_moba_attn_kernel, group=group, n_groups=n_groups),
        grid=(bsz, nh // 2, seq // mb),
        in_specs=[pl.BlockSpec((1, 2, AUG, mb), lambda b, h, i: (b, h, 0, i)),
                  pl.BlockSpec((1, 2, seq, AUG), lambda b, h, i: (b, h, 0, 0)),
                  pl.BlockSpec((1, 2, AUG, seq), lambda b, h, i: (b, h, 0, 0))],
        out_specs=pl.BlockSpec((1, mb, 2 * A_HEAD_DIM), lambda b, h, i: (b, i, h)),
        out_shape=jax.ShapeDtypeStruct((bsz, seq, A_WIDTH), BF16),
        scratch_shapes=[pltpu.VMEM((2, group * mb, mb), F32), pltpu.VMEM((2, group * mb, mb), F32)],
        compiler_params=_cparams(("parallel", "parallel", "arbitrary")),
        name="moba_attn",
    )(q_aug_t, k_aug, v_aug_t)


def _retention_kernel(cdec_ref, q_ref, k_ref, v_ref, g_ref, decay_ref, qdec_ref, kdec_ref, o_ref, state_scr):
    @pl.when(pl.program_id(1) == 0)
    def _():
        state_scr[...] = jnp.zeros_like(state_scr)

    q = q_ref[...]
    k = k_ref[...]
    width = q.shape[1]
    eye = (lax.broadcasted_iota(jnp.int32, (width, width), 0)
           == lax.broadcasted_iota(jnp.int32, (width, width), 1)).astype(BF16)
    k_t = _dot_nt(eye, k)
    k_t_b = k_t.astype(BF16)
    state_b = state_scr[...].astype(BF16)
    for h in range(R_HEADS):
        rows = slice(h * R_QK_DIM, (h + 1) * R_QK_DIM)
        cols = slice(h * R_V_DIM, (h + 1) * R_V_DIM)
        q_h = q[:, rows]
        v_h = v_ref[:, cols]
        inner = _dot(q_h, k_t_b[rows, :]) * decay_ref[h]
        out = _dot(inner.astype(BF16), v_h) + _dot(q_h, state_b[rows, :]) * qdec_ref[h]
        k_dec = (k_t[rows, :] * kdec_ref[h]).astype(BF16)
        state_scr[rows, :] = cdec_ref[h] * state_scr[rows, :] + _dot(k_dec, v_h)
        mu = jnp.mean(out, axis=-1, keepdims=True)
        cen = out - mu
        var = jnp.mean(cen * cen, axis=-1, keepdims=True)
        y = cen * lax.rsqrt(var + GN_EPS)
        o_ref[:, cols] = (y * _silu(g_ref[:, cols].astype(F32))).astype(BF16)


def _retention_consts():
    h = np.arange(R_HEADS, dtype=np.float64)
    log_g = np.log(1.0 - np.exp2(-5.0 - h))
    n = np.arange(R_CHUNK, dtype=np.float64)
    diff = n[:, None] - n[None, :]
    scale = R_QK_DIM ** -0.5
    decay = np.where(diff >= 0, np.exp(np.maximum(diff, 0.0) * log_g[:, None, None]), 0.0) * scale
    q_decay = np.exp((n + 1.0) * log_g[:, None])[:, :, None]
    k_decay = np.exp((R_CHUNK - 1.0 - n) * log_g[:, None])[:, None, :] * scale
    chunk_decay = np.exp(R_CHUNK * log_g)
    return (jnp.asarray(decay, F32), jnp.asarray(q_decay, F32), jnp.asarray(k_decay, F32),
            jnp.asarray(chunk_decay, F32))


def _retention(proj3):
    bsz, seq, _ = proj3.shape
    c = R_CHUNK
    decay, qdec, kdec, cdec = _retention_consts()
    grid_spec = pltpu.PrefetchScalarGridSpec(
        num_scalar_prefetch=1,
        grid=(bsz, seq // c),
        in_specs=[pl.BlockSpec((None, c, R_QK_WIDTH), lambda b, i, s: (b, i, COL_QR // R_QK_WIDTH)),
                  pl.BlockSpec((None, c, R_QK_WIDTH), lambda b, i, s: (b, i, COL_KR // R_QK_WIDTH)),
                  pl.BlockSpec((None, c, R_V_WIDTH), lambda b, i, s: (b, i, COL_VR // R_V_WIDTH)),
                  pl.BlockSpec((None, c, R_V_WIDTH), lambda b, i, s: (b, i, COL_GR // R_V_WIDTH)),
                  pl.BlockSpec((R_HEADS, c, c), lambda b, i, s: (0, 0, 0)),
                  pl.BlockSpec((R_HEADS, c, 1), lambda b, i, s: (0, 0, 0)),
                  pl.BlockSpec((R_HEADS, 1, c), lambda b, i, s: (0, 0, 0))],
        out_specs=pl.BlockSpec((None, c, R_V_WIDTH), lambda b, i, s: (b, i, 0)),
        scratch_shapes=[pltpu.VMEM((R_QK_WIDTH, R_V_DIM), F32)],
    )
    return pl.pallas_call(
        _retention_kernel,
        grid_spec=grid_spec,
        out_shape=jax.ShapeDtypeStruct((bsz, seq, R_V_WIDTH), BF16),
        compiler_params=_cparams(("parallel", "arbitrary")),
        name="retention",
    )(cdec, proj3, proj3, proj3, proj3, decay, qdec, kdec)


LANES = 128
ROW_SUBLANES = D_MODEL // LANES


def _store_tile_rows(ref, lead, x):
    m = x.shape[0]
    for j in range(ROW_SUBLANES):
        ref[lead + (pl.ds(j, m, stride=ROW_SUBLANES), slice(None))] = x[:, j * LANES:(j + 1) * LANES]


def _load_tile_rows(ref, lead, m):
    return [ref[lead + (pl.ds(j, m, stride=ROW_SUBLANES), slice(None))] for j in range(ROW_SUBLANES)]


def _mix_kernel(ya_ref, yr_ref, ga_ref, gt_ref, x_ref, wpa_ref, wpr_ref, wout_ref,
                gt1_ref, g_ref, sc_ref, sh_ref, x1_ref, h2_ref, h2r_ref):
    a = _dot(ya_ref[...], wpa_ref[...]) * _sigmoid(ga_ref[...].astype(F32))
    r = _dot(yr_ref[...], wpr_ref[...]) * _sigmoid(gt_ref[...].astype(F32))
    mix = _dot((a + r).astype(BF16), wout_ref[...])
    x1 = x_ref[...] + gt1_ref[0] * mix
    x1_ref[...] = x1
    ms = jnp.mean(x1 * x1, axis=-1, keepdims=True)
    y = x1 * lax.rsqrt(ms + NORM_EPS) * g_ref[...]
    h2 = y * (1.0 + sc_ref[0]) + sh_ref[0]
    h2_ref[...] = h2.astype(BF16)
    _store_tile_rows(h2r_ref, (), h2)


def _mix(ya, yr, proj, x2d, wpa, wpr, wout, gt1, g, sc, sh, seq):
    t, d = x2d.shape
    tm = min(512, seq)
    per_b = seq // tm
    row = lambda i: (i, 0)
    full = lambda i: (0, 0)
    per_batch = lambda i: (i // per_b, 0, 0)
    return pl.pallas_call(
        _mix_kernel,
        grid=(t // tm,),
        in_specs=[pl.BlockSpec((tm, A_WIDTH), row),
                  pl.BlockSpec((tm, R_V_WIDTH), row),
                  pl.BlockSpec((tm, d), lambda i: (i, COL_GA // D_MODEL)),
                  pl.BlockSpec((tm, d), lambda i: (i, COL_GT // D_MODEL)),
                  pl.BlockSpec((tm, d), row),
                  pl.BlockSpec((A_WIDTH, d), full),
                  pl.BlockSpec((R_V_WIDTH, d), full),
                  pl.BlockSpec((d, d), full),
                  pl.BlockSpec((1, 1, d), per_batch),
                  pl.BlockSpec((1, d), full),
                  pl.BlockSpec((1, 1, d), per_batch),
                  pl.BlockSpec((1, 1, d), per_batch)],
        out_specs=[pl.BlockSpec((tm, d), row), pl.BlockSpec((tm, d), row),
                   pl.BlockSpec((tm * ROW_SUBLANES, LANES), row)],
        out_shape=[jax.ShapeDtypeStruct((t, d), F32), jax.ShapeDtypeStruct((t, d), BF16),
                   jax.ShapeDtypeStruct((t * ROW_SUBLANES, LANES), F32)],
        compiler_params=_cparams(("parallel",)),
        name="merge_outproj_norm",
    )(ya, yr, proj, proj, x2d, wpa, wpr, wout, gt1, g, sc, sh)


def _router_kernel(h_ref, wr_ref, b_ref, e_ref, w_ref, r_ref, c_ref):
    logits = _dot_nt(wr_ref[...], h_ref[...])
    scores = _sigmoid(logits)
    choice = scores + b_ref[...]
    tm = logits.shape[1]
    giota = lax.broadcasted_iota(jnp.int32, (GROUP_SIZE, tm), 0)
    gs_rows = []
    for g in range(N_GROUPS):
        cg = choice[g * GROUP_SIZE:(g + 1) * GROUP_SIZE, :]
        m1 = jnp.max(cg, axis=0, keepdims=True)
        i1 = jnp.min(jnp.where(cg == m1, giota, GROUP_SIZE), axis=0, keepdims=True)
        m2 = jnp.max(jnp.where(giota == i1, KNOCKED_OUT, cg), axis=0, keepdims=True)
        gs_rows.append(m1 + m2)
    gs = jnp.concatenate(gs_rows, axis=0)
    grow = lax.broadcasted_iota(jnp.int32, (N_GROUPS, tm), 0)
    gmask = jnp.zeros((N_GROUPS, tm), jnp.bool_)
    for _ in range(TOPK_GROUPS):
        mx = jnp.max(gs, axis=0, keepdims=True)
        ix = jnp.min(jnp.where(gs == mx, grow, N_GROUPS), axis=0, keepdims=True)
        hit = grow == ix
        gmask = jnp.logical_or(gmask, hit)
        gs = jnp.where(hit, KNOCKED_OUT, gs)
    gmask_f = jnp.where(gmask, 1.0, 0.0)
    masked = jnp.concatenate(
        [jnp.where(gmask_f[g:g + 1, :] > 0.5, choice[g * GROUP_SIZE:(g + 1) * GROUP_SIZE, :], NEG_INF)
         for g in range(N_GROUPS)], axis=0)
    erow = lax.broadcasted_iota(jnp.int32, (N_EXPERTS, tm), 0)
    idx_rows, w_rows = [], []
    chosen = jnp.zeros((N_EXPERTS, tm), F32)
    for _ in range(TOP_K):
        mx = jnp.max(masked, axis=0, keepdims=True)
        ix = jnp.min(jnp.where(masked == mx, erow, N_EXPERTS), axis=0, keepdims=True)
        hit = erow == ix
        w_rows.append(jnp.sum(jnp.where(hit, scores, 0.0), axis=0, keepdims=True))
        idx_rows.append(ix)
        chosen = jnp.where(hit, 1.0, chosen)
        masked = jnp.where(hit, KNOCKED_OUT, masked)
    w = jnp.concatenate(w_rows, axis=0)
    w = w / (jnp.sum(w, axis=0, keepdims=True) + 1e-20) * ROUTED_SCALE
    e_ref[...] = jnp.concatenate(idx_rows, axis=0)
    w_ref[...] = w
    chosen_b = chosen.astype(BF16)
    earlier = (lax.broadcasted_iota(jnp.int32, (tm, tm), 0)
               < lax.broadcasted_iota(jnp.int32, (tm, tm), 1)).astype(BF16)
    before = _dot(chosen_b, earlier)
    ranks = [jnp.sum(jnp.where(erow == ix, before, 0.0), axis=0, keepdims=True) for ix in idx_rows]
    r_ref[...] = jnp.concatenate(ranks, axis=0).astype(jnp.int32)
    c_ref[...] = _dot(chosen_b, jnp.ones((tm, 128), BF16))


ROUTER_ROWS = 512


def _router(h2, wr_t, bias_col):
    t, d = h2.shape
    tm = ROUTER_ROWS
    by_tile = lambda i: (0, i)
    return pl.pallas_call(
        _router_kernel,
        grid=(t // tm,),
        in_specs=[pl.BlockSpec((tm, d), lambda i: (i, 0)),
                  pl.BlockSpec((N_EXPERTS, d), lambda i: (0, 0)),
                  pl.BlockSpec((N_EXPERTS, 1), lambda i: (0, 0))],
        out_specs=[pl.BlockSpec((TOP_K, tm), by_tile), pl.BlockSpec((TOP_K, tm), by_tile),
                   pl.BlockSpec((TOP_K, tm), by_tile), pl.BlockSpec((N_EXPERTS, 128), by_tile)],
        out_shape=[jax.ShapeDtypeStruct((TOP_K, t), jnp.int32),
                   jax.ShapeDtypeStruct((TOP_K, t), F32),
                   jax.ShapeDtypeStruct((TOP_K, t), jnp.int32),
                   jax.ShapeDtypeStruct((N_EXPERTS, (t // tm) * 128), F32)],
        compiler_params=_cparams(("parallel",)),
        name="router_topk",
    )(h2, wr_t, bias_col)


def _pos_kernel(e_ref, r_ref, base_ref, p_ref):
    tm = e_ref.shape[1]
    erow = lax.broadcasted_iota(jnp.int32, (N_EXPERTS, tm), 0)
    base = base_ref[0]
    rows = [jnp.sum(jnp.where(erow == e_ref[k:k + 1, :], base, 0.0), axis=0, keepdims=True)
            for k in range(TOP_K)]
    p_ref[0] = jnp.concatenate(rows, axis=0).astype(jnp.int32) + r_ref[...]


MOVE_ROWS = 256


def _positions(eidx_t, rank_t, tile_base):
    t = eidx_t.shape[1]
    tm = min(MOVE_ROWS, t)
    per_router_tile = ROUTER_ROWS // tm
    return pl.pallas_call(
        _pos_kernel,
        grid=(t // tm,),
        in_specs=[pl.BlockSpec((TOP_K, tm), lambda i: (0, i)),
                  pl.BlockSpec((TOP_K, tm), lambda i: (0, i)),
                  pl.BlockSpec((1, N_EXPERTS, 1), lambda i: (i // per_router_tile, 0, 0))],
        out_specs=pl.BlockSpec((1, TOP_K, tm), lambda i: (i, 0, 0)),
        out_shape=jax.ShapeDtypeStruct((t // tm, TOP_K, tm), jnp.int32),
        compiler_params=_cparams(("parallel",)),
        name="slot_positions",
    )(eidx_t, rank_t, tile_base)


SLOT_ROWS = 512
PAD_CHUNKS = (256, 128, 64, 32, 16, 8, 4, 2, 1)


def _zero_pads_kernel(pad_start_ref, pad_len_ref, xs_in, xs_hbm, zero_buf, pad_sem):
    del xs_in
    zero_buf[...] = jnp.zeros_like(zero_buf)

    def tile_rows(first, n):
        return pl.ds(pl.multiple_of(first * ROW_SUBLANES, ROW_SUBLANES), n * ROW_SUBLANES)

    def pad_copies(e, wait):
        ptr = pad_start_ref[e]
        n = pad_len_ref[e]
        for chunk in PAD_CHUNKS:
            @pl.when((n & chunk) != 0)
            def _(ptr=ptr, chunk=chunk):
                cp = pltpu.make_async_copy(zero_buf.at[tile_rows(0, chunk), :],
                                           xs_hbm.at[tile_rows(ptr, chunk), :], pad_sem)
                if wait:
                    cp.wait()
                else:
                    cp.start()
            ptr = ptr + (n & chunk)

    def issue(e, carry):
        pad_copies(e, False)
        return carry

    def drain(e, carry):
        pad_copies(e, True)
        return carry

    lax.fori_loop(0, N_EXPERTS, issue, 0)
    lax.fori_loop(0, N_EXPERTS, drain, 0)


def _zero_pads(pad_start, pad_len, xs):
    grid_spec = pltpu.PrefetchScalarGridSpec(
        num_scalar_prefetch=2,
        grid=(1,),
        in_specs=[pl.BlockSpec(memory_space=pl.ANY)],
        out_specs=pl.BlockSpec(memory_space=pl.ANY),
        scratch_shapes=[pltpu.VMEM((PAD_CHUNKS[0] * ROW_SUBLANES, LANES), F32),
                        pltpu.SemaphoreType.DMA],
    )
    return pl.pallas_call(
        _zero_pads_kernel,
        grid_spec=grid_spec,
        out_shape=jax.ShapeDtypeStruct(xs.shape, xs.dtype),
        input_output_aliases={2: 0},
        compiler_params=_cparams(("arbitrary",)),
        name="zero_pad_slots",
    )(pad_start, pad_len, xs)


SC_DISPATCH_TOKENS = 64


def _sc_dispatch(pos_blocks, h_rows, n_rows):
    info = plsc.get_sparse_core_info()
    n_cores = info.num_cores
    n_workers = n_cores * info.num_subcores
    t = h_rows.shape[0]
    chunk = SC_DISPATCH_TOKENS
    steps = t // (n_workers * chunk)
    mesh = plsc.VectorSubcoreMesh(core_axis_name="c", subcore_axis_name="s")

    @functools.partial(
        pl.kernel, mesh=mesh,
        out_type=jax.ShapeDtypeStruct((n_rows, ROW_SUBLANES, LANES), F32),
        scratch_types=[pltpu.VMEM((TOP_K, chunk), jnp.int32),
                       pltpu.VMEM((chunk, ROW_SUBLANES, LANES), F32),
                       pltpu.SemaphoreType.DMA],
        name="sc_dispatch_rows",
    )
    def scatter_rows(pos_hbm, h_hbm, out_hbm, idx_v, rows_v, sem):
        wid = lax.axis_index("s") * n_cores + lax.axis_index("c")

        @pl.loop(0, steps)
        def _(step):
            blk = wid * steps + step
            pltpu.sync_copy(pos_hbm.at[blk], idx_v)
            pltpu.sync_copy(h_hbm.at[pl.ds(blk * chunk, chunk)], rows_v)
            for k in range(TOP_K):
                pltpu.async_copy(rows_v, out_hbm.at[idx_v.at[k]], sem).wait()

    return scatter_rows(pos_blocks, h_rows)


def _experts_kernel(blk_e_ref, nblk_ref, x_ref, w1_ref, w3_ref, w2_ref, y_ref, w1b, w3b, w2b):
    s = pl.program_id(0)

    @pl.when(s < nblk_ref[0])
    def _():
        @pl.when(jnp.logical_or(s == 0, blk_e_ref[s] != blk_e_ref[jnp.maximum(s - 1, 0)]))
        def _():
            w1b[...] = w1_ref[0].astype(BF16)
            w3b[...] = w3_ref[0].astype(BF16)
            w2b[...] = w2_ref[0].astype(BF16)

        x = jnp.concatenate([p.astype(BF16) for p in _load_tile_rows(x_ref, (), SLOT_ROWS)], axis=1)
        mid = (_silu(_dot(x, w1b[...])) * _dot(x, w3b[...])).astype(BF16)
        _store_tile_rows(y_ref, (), _dot(mid, w2b[...]))


def _experts(blk_e, nblk_used, xs, w1, w3, w2):
    n_rows = xs.shape[0] // ROW_SUBLANES
    d = D_MODEL
    block_rows = SLOT_ROWS * ROW_SUBLANES
    blk = lambda s, be, nb: (jnp.minimum(s, nb[0] - 1), 0)
    wblk = lambda s, be, nb: (be[jnp.minimum(s, nb[0] - 1)], 0, 0)
    grid_spec = pltpu.PrefetchScalarGridSpec(
        num_scalar_prefetch=2,
        grid=(n_rows // SLOT_ROWS,),
        in_specs=[pl.BlockSpec((block_rows, LANES), blk),
                  pl.BlockSpec((1, d, EXPERT_FF), wblk),
                  pl.BlockSpec((1, d, EXPERT_FF), wblk),
                  pl.BlockSpec((1, EXPERT_FF, d), wblk)],
        out_specs=pl.BlockSpec((block_rows, LANES), blk),
        scratch_shapes=[pltpu.VMEM((d, EXPERT_FF), BF16),
                        pltpu.VMEM((d, EXPERT_FF), BF16),
                        pltpu.VMEM((EXPERT_FF, d), BF16)],
    )
    return pl.pallas_call(
        _experts_kernel,
        grid_spec=grid_spec,
        out_shape=jax.ShapeDtypeStruct(xs.shape, F32),
        compiler_params=_cparams(("arbitrary",)),
        name="routed_experts",
    )(blk_e, nblk_used, xs, w1, w3, w2)


def _combine_kernel(pos_hbm, ys_hbm, w_ref, h_ref, x1_ref, ws1_ref, ws3_ref, ws2_ref, gt2_ref, g_ref, o_ref,
                    pos_smem, ybuf, idx_sem, row_sem):
    i = pl.program_id(0)
    tm, d = x1_ref.shape
    idx_copy = pltpu.make_async_copy(pos_hbm.at[i], pos_smem, idx_sem)
    idx_copy.start()
    idx_copy.wait()

    def tile_rows(first):
        return pl.ds(pl.multiple_of(first * ROW_SUBLANES, ROW_SUBLANES), ROW_SUBLANES)

    def row_copy(k, t):
        return pltpu.make_async_copy(ys_hbm.at[tile_rows(pos_smem[k * tm + t]), :],
                                     ybuf.at[k, tile_rows(t), :], row_sem)

    def issue_rows(t, carry):
        for k in range(TOP_K):
            row_copy(k, t).start(priority=k % 2)
        return carry

    def drain_rows(t, carry):
        for k in range(TOP_K):
            row_copy(k, t).wait()
        return carry

    lax.fori_loop(0, tm, issue_rows, 0, unroll=4)

    h = h_ref[...]
    mid = (_silu(_dot(h, ws1_ref[...])) * _dot(h, ws3_ref[...])).astype(BF16)
    shared = _dot(mid, ws2_ref[...])

    lax.fori_loop(0, tm, drain_rows, 0, unroll=4)

    w = w_ref[...]
    w_cols = [jnp.broadcast_to(w[:, k:k + 1], (tm, LANES)) for k in range(TOP_K)]
    pieces = []
    for j in range(ROW_SUBLANES):
        rows = pl.ds(j, tm, stride=ROW_SUBLANES)
        acc = ybuf[0, rows, :] * w_cols[0]
        for k in range(1, TOP_K):
            acc = acc + ybuf[k, rows, :] * w_cols[k]
        pieces.append(acc)
    routed = jnp.concatenate(pieces, axis=1)
    x2 = x1_ref[...] + gt2_ref[0] * (routed + shared)
    ms = jnp.mean(x2 * x2, axis=-1, keepdims=True)
    o_ref[...] = x2 * lax.rsqrt(ms + NORM_EPS) * g_ref[...]


def _combine(pos2, ys, wts, h2, x1, ws1, ws3, ws2, gt2, g_final, seq):
    t, d = x1.shape
    tm = pos2.shape[1] // TOP_K
    per_b = seq // tm
    row = lambda i: (i, 0)
    full = lambda i: (0, 0)
    return pl.pallas_call(
        _combine_kernel,
        grid=(t // tm,),
        in_specs=[pl.BlockSpec(memory_space=pl.ANY),
                  pl.BlockSpec(memory_space=pl.ANY),
                  pl.BlockSpec((tm, TOP_K), row),
                  pl.BlockSpec((tm, d), row),
                  pl.BlockSpec((tm, d), row),
                  pl.BlockSpec((d, SHARED_FF), full),
                  pl.BlockSpec((d, SHARED_FF), full),
                  pl.BlockSpec((SHARED_FF, d), full),
                  pl.BlockSpec((1, 1, d), lambda i: (i // per_b, 0, 0)),
                  pl.BlockSpec((1, d), full)],
        out_specs=pl.BlockSpec((tm, d), row),
        out_shape=jax.ShapeDtypeStruct((t, d), F32),
        scratch_shapes=[pltpu.SMEM((TOP_K * tm,), jnp.int32),
                        pltpu.VMEM((TOP_K, tm * ROW_SUBLANES, LANES), F32),
                        pltpu.SemaphoreType.DMA,
                        pltpu.SemaphoreType.DMA],
        compiler_params=_cparams(("arbitrary",)),
        name="combine_shared_final",
    )(pos2, ys, wts, h2, x1, ws1, ws3, ws2, gt2, g_final)


def _slot_tables(cnt, t):
    ntiles = cnt.shape[1] // 128
    cnt_tile = cnt.reshape(N_EXPERTS, ntiles, 128)[:, :, 0].astype(jnp.int32)
    counts = jnp.sum(cnt_tile, axis=1)
    padded = (counts + SLOT_ROWS - 1) // SLOT_ROWS * SLOT_ROWS
    pstart = jnp.cumsum(padded) - padded
    tile_base = pstart[:, None] + jnp.cumsum(cnt_tile, axis=1) - cnt_tile
    n_blk = -(-(t * TOP_K) // SLOT_ROWS) + N_EXPERTS
    blk_end = jnp.cumsum(padded // SLOT_ROWS)
    blk_e = jnp.sum((blk_end[None, :] <= jnp.arange(n_blk)[:, None]).astype(jnp.int32), axis=1)
    blk_e = jnp.minimum(blk_e, N_EXPERTS - 1)
    return (blk_e, blk_end[-1:].astype(jnp.int32), pstart + counts, padded - counts,
            tile_base.T.astype(F32).reshape(ntiles, N_EXPERTS, 1), n_blk * SLOT_ROWS)


def _permute_in_cols(w_in):
    qa, ka, va, qr, kr, vr, gr, ga, gt = jnp.split(
        w_in, np.cumsum((A_WIDTH, A_WIDTH, A_WIDTH, R_QK_WIDTH, R_QK_WIDTH, R_V_WIDTH, R_V_WIDTH,
                         D_MODEL))[:].tolist(), axis=1)
    return jnp.concatenate([vr, gr, ga, gt, qa, ka, va, qr, kr], axis=1)


def kernel(x, c, w_ada, b_ada, g_mix, w_in, w_pa, w_pr, w_out, g_ffn, w_router, router_bias,
           w1, w3, w2, ws1, ws3, ws2, g_final):
    bsz, seq, d = x.shape
    t = bsz * seq
    depth = w_ada.shape[0]
    assert depth == 1, "the final norm is fused into the single layer's last kernel"
    slopes = jnp.exp2(-8.0 / A_HEADS * jnp.arange(1, A_HEADS + 1, dtype=F32))
    x2d = x.reshape(t, d)
    for l in range(depth):
        mod = _ada(c, w_ada[l], b_ada[l])
        sh1, sc1, gt1, sh2, sc2, gt2 = [m.reshape(bsz, 1, d) for m in jnp.split(mod, 6, axis=-1)]
        w_in_p = _permute_in_cols(w_in[l]).astype(BF16)
        proj = _inproj(x2d, g_mix[l].reshape(1, d), sc1, sh1, w_in_p, seq)
        proj3 = proj.reshape(bsz, seq, IN_COLS)
        k_aug, q_aug_t, v_aug_t = _moba_prep(proj3, slopes)
        ya = _moba_attn(k_aug, q_aug_t, v_aug_t).reshape(t, A_WIDTH)
        yr = _retention(proj3).reshape(t, R_V_WIDTH)
        x1, h2, h2r = _mix(ya, yr, proj, x2d, w_pa[l].astype(BF16), w_pr[l].astype(BF16),
                           w_out[l].astype(BF16), gt1, g_ffn[l].reshape(1, d), sc2, sh2, seq)
        eidx_t, wts_t, rank_t, cnt = _router(h2, w_router[l].T.astype(BF16),
                                             router_bias[l].reshape(N_EXPERTS, 1))
        blk_e, nblk_used, pad_start, pad_len, tile_base, n_rows = _slot_tables(cnt, t)
        pos3 = _positions(eidx_t, rank_t, tile_base)
        pos2 = pos3.reshape(pos3.shape[0], TOP_K * pos3.shape[2])
        pos_blocks = jnp.transpose(
            pos3.reshape(pos3.shape[0], TOP_K, -1, SC_DISPATCH_TOKENS), (0, 2, 1, 3)
        ).reshape(t // SC_DISPATCH_TOKENS, TOP_K, SC_DISPATCH_TOKENS)
        xs = _sc_dispatch(pos_blocks, h2r.reshape(t, ROW_SUBLANES, LANES), n_rows)
        xs = _zero_pads(pad_start, pad_len, xs.reshape(n_rows * ROW_SUBLANES, LANES))
        ys = _experts(blk_e, nblk_used, xs, w1[l], w3[l], w2[l])
        x2d = _combine(pos2, ys, wts_t.T, h2, x1, ws1[l].astype(BF16), ws3[l].astype(BF16),
                       ws2[l].astype(BF16), gt2, g_final.reshape(1, d), seq)
    return x2d.reshape(bsz, seq, d)
```

```python
import functools

import jax
import jax.numpy as jnp
import numpy as np
from jax import lax
from jax.experimental import pallas as pl
from jax.experimental.pallas import tpu as pltpu
from jax.experimental.pallas import tpu_sc as plsc

F32 = jnp.float32
BF16 = jnp.bfloat16

D_MODEL = 1024
A_HEADS = 8
A_HEAD_DIM = 64
A_WIDTH = A_HEADS * A_HEAD_DIM
MOBA_BLOCK = 256
MOBA_TOPK = 3
R_HEADS = 8
R_QK_DIM = 64
R_V_DIM = 128
R_QK_WIDTH = R_HEADS * R_QK_DIM
R_V_WIDTH = R_HEADS * R_V_DIM
R_CHUNK = 128
N_EXPERTS = 256
TOP_K = 8
N_GROUPS = 8
GROUP_SIZE = N_EXPERTS // N_GROUPS
TOPK_GROUPS = 4
EXPERT_FF = 256
SHARED_FF = 256
ROUTED_SCALE = 2.5
NORM_EPS = 1e-6
GN_EPS = 1e-6
NEG_INF = -1e30
KNOCKED_OUT = -3e38

COL_VR, COL_GR, COL_GA, COL_GT = 0, 1024, 2048, 3072
COL_QA, COL_KA, COL_VA, COL_QR, COL_KR = 4096, 4608, 5120, 5632, 6144
IN_COLS = 6656
AUG = 128
FEAT_BIAS = A_HEAD_DIM
FEAT_POS = A_HEAD_DIM + 32

VMEM_LIMIT = 56 * 1024 * 1024


def _cparams(sem, vmem=VMEM_LIMIT):
    return pltpu.CompilerParams(dimension_semantics=sem, vmem_limit_bytes=vmem)


def _dot(a, b):
    return jnp.dot(a, b, preferred_element_type=F32)


def _dot_nt(a, b):
    return lax.dot_general(a, b, (((1,), (1,)), ((), ())), preferred_element_type=F32)


def _sigmoid(x):
    return 1.0 / (1.0 + jnp.exp(-x))


def _silu(x):
    return x * _sigmoid(x)


def _ada_kernel(c_ref, w_ref, b_ref, o_ref):
    c = c_ref[...]
    s = _silu(c)
    s_hi = s.astype(BF16)
    s_lo = (s - s_hi.astype(F32)).astype(BF16)
    w = w_ref[...]
    w_hi = w.astype(BF16)
    w_lo = (w - w_hi.astype(F32)).astype(BF16)
    o_ref[...] = _dot(s_hi, w_hi) + _dot(s_hi, w_lo) + _dot(s_lo, w_hi) + b_ref[...]


def _ada(c, w_ada, b_ada):
    bsz, d = c.shape
    n = w_ada.shape[1]
    tn = 1024
    return pl.pallas_call(
        _ada_kernel,
        grid=(n // tn,),
        in_specs=[pl.BlockSpec((bsz, d), lambda j: (0, 0)),
                  pl.BlockSpec((d, tn), lambda j: (0, j)),
                  pl.BlockSpec((1, tn), lambda j: (0, j))],
        out_specs=pl.BlockSpec((bsz, tn), lambda j: (0, j)),
        out_shape=jax.ShapeDtypeStruct((bsz, n), F32),
        compiler_params=_cparams(("parallel",)),
        name="ada_mod",
    )(c, w_ada, b_ada.reshape(1, n))


INPROJ_COLS = 512


def _inproj_kernel(x_ref, g_ref, sc_ref, sh_ref, w_ref, o_ref):
    x = x_ref[...]
    ms = jnp.mean(x * x, axis=-1, keepdims=True)
    y = x * lax.rsqrt(ms + NORM_EPS) * g_ref[...]
    h = (y * (1.0 + sc_ref[0]) + sh_ref[0]).astype(BF16)
    for j in range(w_ref.shape[1] // INPROJ_COLS):
        cols = slice(j * INPROJ_COLS, (j + 1) * INPROJ_COLS)
        o_ref[:, cols] = _dot(h, w_ref[:, cols]).astype(BF16)


def _inproj(x2d, g, sc, sh, w_bf16, seq):
    t, d = x2d.shape
    n = w_bf16.shape[1]
    tm = min(512, seq)
    per_b = seq // tm
    return pl.pallas_call(
        _inproj_kernel,
        grid=(t // tm,),
        in_specs=[pl.BlockSpec((tm, d), lambda i: (i, 0)),
                  pl.BlockSpec((1, d), lambda i: (0, 0)),
                  pl.BlockSpec((1, 1, d), lambda i: (i // per_b, 0, 0)),
                  pl.BlockSpec((1, 1, d), lambda i: (i // per_b, 0, 0)),
                  pl.BlockSpec((d, n), lambda i: (0, 0))],
        out_specs=pl.BlockSpec((tm, n), lambda i: (i, 0)),
        out_shape=jax.ShapeDtypeStruct((t, n), BF16),
        compiler_params=_cparams(("parallel",)),
        name="norm_inproj",
    )(x2d, g, sc, sh, w_bf16)


def _moba_prep_kernel(slopes_ref, q_ref, k_ref, v_ref, ko_ref, qo_ref, vo_ref, kmean_scr):
    i = pl.program_id(1)
    nblk = kmean_scr.shape[0]
    width = q_ref.shape[1]

    @pl.when(i == 0)
    def _():
        kmean_scr[...] = jnp.zeros_like(kmean_scr)

    q = q_ref[...]
    k = k_ref[...]
    v = v_ref[...]
    kmean_scr[pl.ds(i, 1), :] = jnp.mean(k.astype(F32), axis=0, keepdims=True)

    eye = (lax.broadcasted_iota(jnp.int32, (width, width), 0)
           == lax.broadcasted_iota(jnp.int32, (width, width), 1)).astype(BF16)
    q_t = _dot_nt(eye, q)
    v_t = _dot_nt(eye, v)

    km = kmean_scr[...]
    km_rep = jnp.concatenate([km] * A_HEADS, axis=0)
    r_head = lax.broadcasted_iota(jnp.int32, km_rep.shape, 0) // nblk
    c_head = lax.broadcasted_iota(jnp.int32, km_rep.shape, 1) // A_HEAD_DIM
    km_bd = jnp.where(r_head == c_head, km_rep, 0.0)
    km_hi = km_bd.astype(BF16)
    km_lo = (km_bd - km_hi.astype(F32)).astype(BF16)
    q_t_b = q_t.astype(BF16)
    gate_all = _dot(km_hi, q_t_b) + _dot(km_lo, q_t_b)

    mb = q.shape[0]
    blk = lax.broadcasted_iota(jnp.int32, (nblk, mb), 0)
    lane_pos = lax.broadcasted_iota(jnp.int32, (16, mb), 1).astype(F32)
    row16 = lax.broadcasted_iota(jnp.int32, (16, mb), 0)
    key_pos = lax.broadcasted_iota(jnp.int32, (mb, AUG), 0).astype(F32)
    kcol = lax.broadcasted_iota(jnp.int32, (mb, AUG), 1)
    sel_r = lax.broadcasted_iota(jnp.int32, (width, AUG), 0)
    sel_c = lax.broadcasted_iota(jnp.int32, (width, AUG), 1)

    for h in range(A_HEADS):
        slope = slopes_ref[h]
        g = jnp.where(blk < i, gate_all[h * nblk:(h + 1) * nblk, :], NEG_INF)
        sel = jnp.zeros((nblk, mb), jnp.bool_)
        for r in range(MOBA_TOPK):
            m = jnp.max(g, axis=0, keepdims=True)
            idx = jnp.min(jnp.where(g == m, blk, nblk), axis=0, keepdims=True)
            hit = blk == idx
            sel = jnp.logical_or(sel, jnp.logical_and(hit, r < i))
            g = jnp.where(hit, KNOCKED_OUT, g)
        bias_t = jnp.where(sel, 0.0, NEG_INF)

        scale = A_HEAD_DIM ** -0.5
        qo_ref[0, h, 0:A_HEAD_DIM, :] = (q_t[h * A_HEAD_DIM:(h + 1) * A_HEAD_DIM, :] * scale).astype(BF16)
        qo_ref[0, h, FEAT_BIAS:FEAT_BIAS + nblk, :] = bias_t.astype(BF16)
        if nblk < 32:
            qo_ref[0, h, FEAT_BIAS + nblk:FEAT_POS, :] = jnp.zeros((32 - nblk, mb), BF16)
        blk_off = slope * (i * mb).astype(F32)
        pos_feat = jnp.where(row16 == 0, -slope * lane_pos,
                             jnp.where(row16 == 2, -blk_off,
                                       jnp.where(jnp.logical_or(row16 == 1, row16 == 3), 1.0, 0.0)))
        qo_ref[0, h, FEAT_POS:FEAT_POS + 16, :] = pos_feat.astype(BF16)
        qo_ref[0, h, FEAT_POS + 16:AUG, :] = jnp.zeros((AUG - FEAT_POS - 16, mb), BF16)

        vo_ref[0, h, 0:A_HEAD_DIM, :] = v_t[h * A_HEAD_DIM:(h + 1) * A_HEAD_DIM, :].astype(BF16)
        vo_ref[0, h, A_HEAD_DIM:A_HEAD_DIM + 16, :] = jnp.where(row16 == 0, 1.0, 0.0).astype(BF16)
        vo_ref[0, h, A_HEAD_DIM + 16:AUG, :] = jnp.zeros((AUG - A_HEAD_DIM - 16, mb), BF16)

        pick = jnp.where(jnp.logical_and(sel_r == sel_c + h * A_HEAD_DIM, sel_c < A_HEAD_DIM),
                         1.0, 0.0).astype(BF16)
        k_feat = jnp.where(
            jnp.logical_or(kcol == FEAT_BIAS + i, jnp.logical_or(kcol == FEAT_POS, kcol == FEAT_POS + 2)), 1.0,
            jnp.where(kcol == FEAT_POS + 1, slope * key_pos, jnp.where(kcol == FEAT_POS + 3, blk_off, 0.0)))
        ko_ref[0, h, :, :] = (_dot(k, pick) + k_feat).astype(BF16)


def _moba_prep(proj3, slopes):
    bsz, seq, _ = proj3.shape
    nblk = seq // MOBA_BLOCK
    mb = MOBA_BLOCK
    grid_spec = pltpu.PrefetchScalarGridSpec(
        num_scalar_prefetch=1,
        grid=(bsz, nblk),
        in_specs=[pl.BlockSpec((None, mb, A_WIDTH), lambda b, i, s: (b, i, COL_QA // A_WIDTH)),
                  pl.BlockSpec((None, mb, A_WIDTH), lambda b, i, s: (b, i, COL_KA // A_WIDTH)),
                  pl.BlockSpec((None, mb, A_WIDTH), lambda b, i, s: (b, i, COL_VA // A_WIDTH))],
        out_specs=[pl.BlockSpec((1, A_HEADS, mb, AUG), lambda b, i, s: (b, 0, i, 0)),
                   pl.BlockSpec((1, A_HEADS, AUG, mb), lambda b, i, s: (b, 0, 0, i)),
                   pl.BlockSpec((1, A_HEADS, AUG, mb), lambda b, i, s: (b, 0, 0, i))],
        scratch_shapes=[pltpu.VMEM((nblk, A_WIDTH), F32)],
    )
    return pl.pallas_call(
        _moba_prep_kernel,
        grid_spec=grid_spec,
        out_shape=[jax.ShapeDtypeStruct((bsz, A_HEADS, seq, AUG), BF16),
                   jax.ShapeDtypeStruct((bsz, A_HEADS, AUG, seq), BF16),
                   jax.ShapeDtypeStruct((bsz, A_HEADS, AUG, seq), BF16)],
        compiler_params=_cparams(("parallel", "arbitrary")),
        name="moba_prep",
    )(slopes, proj3, proj3, proj3)


def _moba_attn_kernel(q_ref, k_ref, v_ref, o_ref, s_a, s_b, *, group, n_groups):
    i = pl.program_id(2)
    mb = MOBA_BLOCK
    span = group * mb
    own = pl.multiple_of(i * mb, mb)
    key_i = lax.broadcasted_iota(jnp.int32, (mb, mb), 0)
    qry_i = lax.broadcasted_iota(jnp.int32, (mb, mb), 1)
    feat = lax.broadcasted_iota(jnp.int32, (AUG, mb), 0)
    is_bias = jnp.logical_and(feat >= FEAT_BIAS, feat < FEAT_POS)
    q_ts, carry0 = [], []
    for hh in range(2):
        q_t = q_ref[0, hh]
        q_ts.append(q_t)
        q_own = jnp.where(is_bias, jnp.zeros_like(q_t), q_t)
        s = _dot(k_ref[0, hh, pl.ds(own, mb), :], q_own)
        s = jnp.where(key_i <= qry_i, s, NEG_INF)
        m0 = jnp.max(s, axis=0, keepdims=True)
        p = jnp.exp(s - m0)
        carry0 += [m0, _dot(v_ref[0, hh, :, pl.ds(own, mb)], p.astype(BF16))]

    def scores(g, dst):
        start = pl.multiple_of(jnp.minimum(g, n_groups - 1) * span, span)
        for hh in range(2):
            dst[hh] = _dot(k_ref[0, hh, pl.ds(start, span), :], q_ts[hh])

    def absorb(g, src, carry):
        start = pl.multiple_of(g * span, span)
        new = []
        for hh in range(2):
            m, acc = carry[2 * hh], carry[2 * hh + 1]
            sb = src[hh]
            m_new = jnp.maximum(m, jnp.max(sb, axis=0, keepdims=True))
            pb = jnp.exp(sb - m_new)
            alpha = jnp.exp(m - m_new)
            acc = acc * alpha + _dot(v_ref[0, hh, :, pl.ds(start, span)], pb.astype(BF16))
            new += [m_new, acc]
        return tuple(new)

    def body(pair, carry):
        scores(2 * pair + 1, s_b)
        carry = absorb(2 * pair, s_a, carry)
        scores(2 * pair + 2, s_a)
        return absorb(2 * pair + 1, s_b, carry)

    scores(0, s_a)
    live_groups = (i + group - 1) // group
    res = lax.fori_loop(0, (live_groups + 1) // 2, body, tuple(carry0))
    outs = [res[2 * hh + 1][0:A_HEAD_DIM, :] / res[2 * hh + 1][A_HEAD_DIM:A_HEAD_DIM + 1, :] for hh in range(2)]
    o_t = jnp.concatenate(outs, axis=0).astype(BF16)
    eye = (key_i == qry_i).astype(BF16)
    o_ref[0] = _dot_nt(eye, o_t).astype(BF16)


def _moba_attn(k_aug, q_aug_t, v_aug_t):
    bsz, nh, seq, _ = k_aug.shape
    mb = MOBA_BLOCK
    group = min(2, seq // mb)
    n_groups = seq // (group * mb)
    return pl.pallas_call(
        functools.partial(_moba_attn_kernel, group=group, n_groups=n_groups),
        grid=(bsz, nh // 2, seq // mb),
        in_specs=[pl.BlockSpec((1, 2, AUG, mb), lambda b, h, i: (b, h, 0, i)),
                  pl.BlockSpec((1, 2, seq, AUG), lambda b, h, i: (b, h, 0, 0)),
                  pl.BlockSpec((1, 2, AUG, seq), lambda b, h, i: (b, h, 0, 0))],
        out_specs=pl.BlockSpec((1, mb, 2 * A_HEAD_DIM), lambda b, h, i: (b, i, h)),
        out_shape=jax.ShapeDtypeStruct((bsz, seq, A_WIDTH), BF16),
        scratch_shapes=[pltpu.VMEM((2, group * mb, mb), F32), pltpu.VMEM((2, group * mb, mb), F32)],
        compiler_params=_cparams(("parallel", "parallel", "arbitrary")),
        name="moba_attn",
    )(q_aug_t, k_aug, v_aug_t)


def _retention_kernel(cdec_ref, q_ref, k_ref, v_ref, g_ref, decay_ref, qdec_ref, kdec_ref, o_ref, state_scr):
    @pl.when(pl.program_id(1) == 0)
    def _():
        state_scr[...] = jnp.zeros_like(state_scr)

    q = q_ref[...]
    k = k_ref[...]
    width = q.shape[1]
    eye = (lax.broadcasted_iota(jnp.int32, (width, width), 0)
           == lax.broadcasted_iota(jnp.int32, (width, width), 1)).astype(BF16)
    k_t = _dot_nt(eye, k)
    k_t_b = k_t.astype(BF16)
    state_b = state_scr[...].astype(BF16)
    for h in range(R_HEADS):
        rows = slice(h * R_QK_DIM, (h + 1) * R_QK_DIM)
        cols = slice(h * R_V_DIM, (h + 1) * R_V_DIM)
        q_h = q[:, rows]
        v_h = v_ref[:, cols]
        inner = _dot(q_h, k_t_b[rows, :]) * decay_ref[h]
        out = _dot(inner.astype(BF16), v_h) + _dot(q_h, state_b[rows, :]) * qdec_ref[h]
        k_dec = (k_t[rows, :] * kdec_ref[h]).astype(BF16)
        state_scr[rows, :] = cdec_ref[h] * state_scr[rows, :] + _dot(k_dec, v_h)
        mu = jnp.mean(out, axis=-1, keepdims=True)
        cen = out - mu
        var = jnp.mean(cen * cen, axis=-1, keepdims=True)
        y = cen * lax.rsqrt(var + GN_EPS)
        o_ref[:, cols] = (y * _silu(g_ref[:, cols].astype(F32))).astype(BF16)


def _retention_consts():
    h = np.arange(R_HEADS, dtype=np.float64)
    log_g = np.log(1.0 - np.exp2(-5.0 - h))
    n = np.arange(R_CHUNK, dtype=np.float64)
    diff = n[:, None] - n[None, :]
    scale = R_QK_DIM ** -0.5
    decay = np.where(diff >= 0, np.exp(np.maximum(diff, 0.0) * log_g[:, None, None]), 0.0) * scale
    q_decay = np.exp((n + 1.0) * log_g[:, None])[:, :, None]
    k_decay = np.exp((R_CHUNK - 1.0 - n) * log_g[:, None])[:, None, :] * scale
    chunk_decay = np.exp(R_CHUNK * log_g)
    return (jnp.asarray(decay, F32), jnp.asarray(q_decay, F32), jnp.asarray(k_decay, F32),
            jnp.asarray(chunk_decay, F32))


def _retention(proj3):
    bsz, seq, _ = proj3.shape
    c = R_CHUNK
    decay, qdec, kdec, cdec = _retention_consts()
    grid_spec = pltpu.PrefetchScalarGridSpec(
        num_scalar_prefetch=1,
        grid=(bsz, seq // c),
        in_specs=[pl.BlockSpec((None, c, R_QK_WIDTH), lambda b, i, s: (b, i, COL_QR // R_QK_WIDTH)),
                  pl.BlockSpec((None, c, R_QK_WIDTH), lambda b, i, s: (b, i, COL_KR // R_QK_WIDTH)),
                  pl.BlockSpec((None, c, R_V_WIDTH), lambda b, i, s: (b, i, COL_VR // R_V_WIDTH)),
                  pl.BlockSpec((None, c, R_V_WIDTH), lambda b, i, s: (b, i, COL_GR // R_V_WIDTH)),
                  pl.BlockSpec((R_HEADS, c, c), lambda b, i, s: (0, 0, 0)),
                  pl.BlockSpec((R_HEADS, c, 1), lambda b, i, s: (0, 0, 0)),
                  pl.BlockSpec((R_HEADS, 1, c), lambda b, i, s: (0, 0, 0))],
        out_specs=pl.BlockSpec((None, c, R_V_WIDTH), lambda b, i, s: (b, i, 0)),
        scratch_shapes=[pltpu.VMEM((R_QK_WIDTH, R_V_DIM), F32)],
    )
    return pl.pallas_call(
        _retention_kernel,
        grid_spec=grid_spec,
        out_shape=jax.ShapeDtypeStruct((bsz, seq, R_V_WIDTH), BF16),
        compiler_params=_cparams(("parallel", "arbitrary")),
        name="retention",
    )(cdec, proj3, proj3, proj3, proj3, decay, qdec, kdec)


def _pack_halves(x):
    w = x.shape[1] // 2
    bits = lax.bitcast_convert_type(x.astype(BF16).astype(F32), jnp.uint32)
    return (bits[:, :w] >> 16) | (bits[:, w:] & jnp.uint32(0xFFFF0000))


def _unpack_halves(p):
    lo = lax.bitcast_convert_type(p << 16, F32)
    hi = lax.bitcast_convert_type(p & jnp.uint32(0xFFFF0000), F32)
    return lo, hi


LANES = 128
ROW_SUBLANES = D_MODEL // LANES


def _store_tile_rows(ref, lead, x):
    m = x.shape[0]
    for j in range(ROW_SUBLANES):
        ref[lead + (pl.ds(j, m, stride=ROW_SUBLANES), slice(None))] = x[:, j * LANES:(j + 1) * LANES]


def _load_tile_rows(ref, lead, m):
    return [ref[lead + (pl.ds(j, m, stride=ROW_SUBLANES), slice(None))] for j in range(ROW_SUBLANES)]


def _mix_kernel(ya_ref, yr_ref, ga_ref, gt_ref, x_ref, wpa_ref, wpr_ref, wout_ref,
                gt1_ref, g_ref, sc_ref, sh_ref, x1_ref, h2_ref, h2p_ref):
    a = _dot(ya_ref[...], wpa_ref[...]) * _sigmoid(ga_ref[...].astype(F32))
    r = _dot(yr_ref[...], wpr_ref[...]) * _sigmoid(gt_ref[...].astype(F32))
    mix = _dot((a + r).astype(BF16), wout_ref[...])
    x1 = x_ref[...] + gt1_ref[0] * mix
    x1_ref[...] = x1
    ms = jnp.mean(x1 * x1, axis=-1, keepdims=True)
    y = x1 * lax.rsqrt(ms + NORM_EPS) * g_ref[...]
    h2 = y * (1.0 + sc_ref[0]) + sh_ref[0]
    h2_ref[...] = h2.astype(BF16)
    h2p_ref[...] = _pack_halves(h2)


def _mix(ya, yr, proj, x2d, wpa, wpr, wout, gt1, g, sc, sh, seq):
    t, d = x2d.shape
    tm = min(512, seq)
    per_b = seq // tm
    row = lambda i: (i, 0)
    full = lambda i: (0, 0)
    per_batch = lambda i: (i // per_b, 0, 0)
    return pl.pallas_call(
        _mix_kernel,
        grid=(t // tm,),
        in_specs=[pl.BlockSpec((tm, A_WIDTH), row),
                  pl.BlockSpec((tm, R_V_WIDTH), row),
                  pl.BlockSpec((tm, d), lambda i: (i, COL_GA // D_MODEL)),
                  pl.BlockSpec((tm, d), lambda i: (i, COL_GT // D_MODEL)),
                  pl.BlockSpec((tm, d), row),
                  pl.BlockSpec((A_WIDTH, d), full),
                  pl.BlockSpec((R_V_WIDTH, d), full),
                  pl.BlockSpec((d, d), full),
                  pl.BlockSpec((1, 1, d), per_batch),
                  pl.BlockSpec((1, d), full),
                  pl.BlockSpec((1, 1, d), per_batch),
                  pl.BlockSpec((1, 1, d), per_batch)],
        out_specs=[pl.BlockSpec((tm, d), row), pl.BlockSpec((tm, d), row),
                   pl.BlockSpec((tm, d // 2), row)],
        out_shape=[jax.ShapeDtypeStruct((t, d), F32), jax.ShapeDtypeStruct((t, d), BF16),
                   jax.ShapeDtypeStruct((t, d // 2), jnp.uint32)],
        compiler_params=_cparams(("parallel",)),
        name="merge_outproj_norm",
    )(ya, yr, proj, proj, x2d, wpa, wpr, wout, gt1, g, sc, sh)


def _router_kernel(h_ref, wr_ref, b_ref, e_ref, w_ref, r_ref, c_ref):
    logits = _dot_nt(wr_ref[...], h_ref[...])
    scores = _sigmoid(logits)
    choice = scores + b_ref[...]
    tm = logits.shape[1]
    giota = lax.broadcasted_iota(jnp.int32, (GROUP_SIZE, tm), 0)
    gs_rows = []
    for g in range(N_GROUPS):
        cg = choice[g * GROUP_SIZE:(g + 1) * GROUP_SIZE, :]
        m1 = jnp.max(cg, axis=0, keepdims=True)
        i1 = jnp.min(jnp.where(cg == m1, giota, GROUP_SIZE), axis=0, keepdims=True)
        m2 = jnp.max(jnp.where(giota == i1, KNOCKED_OUT, cg), axis=0, keepdims=True)
        gs_rows.append(m1 + m2)
    gs = jnp.concatenate(gs_rows, axis=0)
    grow = lax.broadcasted_iota(jnp.int32, (N_GROUPS, tm), 0)
    gmask = jnp.zeros((N_GROUPS, tm), jnp.bool_)
    for _ in range(TOPK_GROUPS):
        mx = jnp.max(gs, axis=0, keepdims=True)
        ix = jnp.min(jnp.where(gs == mx, grow, N_GROUPS), axis=0, keepdims=True)
        hit = grow == ix
        gmask = jnp.logical_or(gmask, hit)
        gs = jnp.where(hit, KNOCKED_OUT, gs)
    gmask_f = jnp.where(gmask, 1.0, 0.0)
    masked = jnp.concatenate(
        [jnp.where(gmask_f[g:g + 1, :] > 0.5, choice[g * GROUP_SIZE:(g + 1) * GROUP_SIZE, :], NEG_INF)
         for g in range(N_GROUPS)], axis=0)
    erow = lax.broadcasted_iota(jnp.int32, (N_EXPERTS, tm), 0)
    idx_rows, w_rows = [], []
    chosen = jnp.zeros((N_EXPERTS, tm), F32)
    for _ in range(TOP_K):
        mx = jnp.max(masked, axis=0, keepdims=True)
        ix = jnp.min(jnp.where(masked == mx, erow, N_EXPERTS), axis=0, keepdims=True)
        hit = erow == ix
        w_rows.append(jnp.sum(jnp.where(hit, scores, 0.0), axis=0, keepdims=True))
        idx_rows.append(ix)
        chosen = jnp.where(hit, 1.0, chosen)
        masked = jnp.where(hit, KNOCKED_OUT, masked)
    w = jnp.concatenate(w_rows, axis=0)
    w = w / (jnp.sum(w, axis=0, keepdims=True) + 1e-20) * ROUTED_SCALE
    e_ref[...] = jnp.concatenate(idx_rows, axis=0)
    w_ref[...] = w
    chosen_b = chosen.astype(BF16)
    earlier = (lax.broadcasted_iota(jnp.int32, (tm, tm), 0)
               < lax.broadcasted_iota(jnp.int32, (tm, tm), 1)).astype(BF16)
    before = _dot(chosen_b, earlier)
    ranks = [jnp.sum(jnp.where(erow == ix, before, 0.0), axis=0, keepdims=True) for ix in idx_rows]
    r_ref[...] = jnp.concatenate(ranks, axis=0).astype(jnp.int32)
    c_ref[...] = _dot(chosen_b, jnp.ones((tm, 128), BF16))


ROUTER_ROWS = 512


def _router(h2, wr_t, bias_col):
    t, d = h2.shape
    tm = ROUTER_ROWS
    by_tile = lambda i: (0, i)
    return pl.pallas_call(
        _router_kernel,
        grid=(t // tm,),
        in_specs=[pl.BlockSpec((tm, d), lambda i: (i, 0)),
                  pl.BlockSpec((N_EXPERTS, d), lambda i: (0, 0)),
                  pl.BlockSpec((N_EXPERTS, 1), lambda i: (0, 0))],
        out_specs=[pl.BlockSpec((TOP_K, tm), by_tile), pl.BlockSpec((TOP_K, tm), by_tile),
                   pl.BlockSpec((TOP_K, tm), by_tile), pl.BlockSpec((N_EXPERTS, 128), by_tile)],
        out_shape=[jax.ShapeDtypeStruct((TOP_K, t), jnp.int32),
                   jax.ShapeDtypeStruct((TOP_K, t), F32),
                   jax.ShapeDtypeStruct((TOP_K, t), jnp.int32),
                   jax.ShapeDtypeStruct((N_EXPERTS, (t // tm) * 128), F32)],
        compiler_params=_cparams(("parallel",)),
        name="router_topk",
    )(h2, wr_t, bias_col)


def _pos_kernel(e_ref, r_ref, base_ref, p_ref):
    tm = e_ref.shape[1]
    erow = lax.broadcasted_iota(jnp.int32, (N_EXPERTS, tm), 0)
    base = base_ref[0]
    rows = [jnp.sum(jnp.where(erow == e_ref[k:k + 1, :], base, 0.0), axis=0, keepdims=True)
            for k in range(TOP_K)]
    p_ref[0] = jnp.concatenate(rows, axis=0).astype(jnp.int32) + r_ref[...]


MOVE_ROWS = 256


def _positions(eidx_t, rank_t, tile_base):
    t = eidx_t.shape[1]
    tm = min(MOVE_ROWS, t)
    per_router_tile = ROUTER_ROWS // tm
    return pl.pallas_call(
        _pos_kernel,
        grid=(t // tm,),
        in_specs=[pl.BlockSpec((TOP_K, tm), lambda i: (0, i)),
                  pl.BlockSpec((TOP_K, tm), lambda i: (0, i)),
                  pl.BlockSpec((1, N_EXPERTS, 1), lambda i: (i // per_router_tile, 0, 0))],
        out_specs=pl.BlockSpec((1, TOP_K, tm), lambda i: (i, 0, 0)),
        out_shape=jax.ShapeDtypeStruct((t // tm, TOP_K, tm), jnp.int32),
        compiler_params=_cparams(("parallel",)),
        name="slot_positions",
    )(eidx_t, rank_t, tile_base)


SLOT_ROWS = 512
PAD_CHUNKS = (256, 128, 64, 32, 16, 8)


def _zero_pads_kernel(pad_start_ref, pad_len_ref, xs_in, xs_hbm, zero_buf, pad_sem):
    del xs_in
    zero_buf[...] = jnp.zeros_like(zero_buf)

    def pad_copies(e, wait):
        start = pad_start_ref[e]
        n = pad_len_ref[e]
        head = jnp.minimum((-start) & (ROW_SUBLANES - 1), n)

        def fill(first, size, pred):
            @pl.when(pred)
            def _():
                cp = pltpu.make_async_copy(zero_buf.at[pl.ds(0, size), :], xs_hbm.at[pl.ds(first, size), :], pad_sem)
                if wait:
                    cp.wait()
                else:
                    cp.start()

        for j in range(ROW_SUBLANES - 1):
            fill(start + j, 1, j < head)
        ptr = start + head
        rest = n - head
        for chunk in PAD_CHUNKS:
            fill(pl.multiple_of(ptr, ROW_SUBLANES), chunk, (rest & chunk) != 0)
            ptr = ptr + (rest & chunk)

    def issue(e, carry):
        pad_copies(e, False)
        return carry

    def drain(e, carry):
        pad_copies(e, True)
        return carry

    lax.fori_loop(0, N_EXPERTS, issue, 0)
    lax.fori_loop(0, N_EXPERTS, drain, 0)


def _zero_pads(pad_start, pad_len, xs):
    grid_spec = pltpu.PrefetchScalarGridSpec(
        num_scalar_prefetch=2,
        grid=(1,),
        in_specs=[pl.BlockSpec(memory_space=pl.ANY)],
        out_specs=pl.BlockSpec(memory_space=pl.ANY),
        scratch_shapes=[pltpu.VMEM((PAD_CHUNKS[0], xs.shape[1]), xs.dtype),
                        pltpu.SemaphoreType.DMA],
    )
    return pl.pallas_call(
        _zero_pads_kernel,
        grid_spec=grid_spec,
        out_shape=jax.ShapeDtypeStruct(xs.shape, xs.dtype),
        input_output_aliases={2: 0},
        compiler_params=_cparams(("arbitrary",)),
        name="zero_pad_slots",
    )(pad_start, pad_len, xs)


SC_DISPATCH_TOKENS = 64


def _sc_dispatch(pos_blocks, h_rows, n_rows):
    info = plsc.get_sparse_core_info()
    n_cores = info.num_cores
    n_workers = n_cores * info.num_subcores
    t = h_rows.shape[0]
    chunk = SC_DISPATCH_TOKENS
    steps = t // (n_workers * chunk)
    mesh = plsc.VectorSubcoreMesh(core_axis_name="c", subcore_axis_name="s")

    @functools.partial(
        pl.kernel, mesh=mesh,
        out_type=jax.ShapeDtypeStruct((n_rows,) + h_rows.shape[1:], h_rows.dtype),
        scratch_types=[pltpu.VMEM((TOP_K, chunk), jnp.int32),
                       pltpu.VMEM((chunk,) + h_rows.shape[1:], h_rows.dtype),
                       pltpu.SemaphoreType.DMA],
        name="sc_dispatch_rows",
    )
    def scatter_rows(pos_hbm, h_hbm, out_hbm, idx_v, rows_v, sem):
        wid = lax.axis_index("s") * n_cores + lax.axis_index("c")

        @pl.loop(0, steps)
        def _(step):
            blk = wid * steps + step
            pltpu.sync_copy(pos_hbm.at[blk], idx_v)
            pltpu.sync_copy(h_hbm.at[pl.ds(blk * chunk, chunk)], rows_v)
            for k in range(TOP_K):
                pltpu.async_copy(rows_v, out_hbm.at[idx_v.at[k]], sem).wait()

    return scatter_rows(pos_blocks, h_rows)


def _experts_kernel(blk_e_ref, nblk_ref, x_ref, w1_ref, w3_ref, w2_ref, y_ref, w1b, w3b, w2b):
    s = pl.program_id(0)

    @pl.when(s < nblk_ref[0])
    def _():
        @pl.when(jnp.logical_or(s == 0, blk_e_ref[s] != blk_e_ref[jnp.maximum(s - 1, 0)]))
        def _():
            w1b[...] = w1_ref[0].astype(BF16)
            w3b[...] = w3_ref[0].astype(BF16)
            w2b[...] = w2_ref[0].astype(BF16)

        half = x_ref.shape[1]
        lo, hi = _unpack_halves(x_ref[...])
        lo = lo.astype(BF16)
        hi = hi.astype(BF16)
        h1 = _dot(lo, w1b[:half, :]) + _dot(hi, w1b[half:, :])
        h3 = _dot(lo, w3b[:half, :]) + _dot(hi, w3b[half:, :])
        mid = (_silu(h1) * h3).astype(BF16)
        _store_tile_rows(y_ref, (), _dot(mid, w2b[...]))


def _experts(blk_e, nblk_used, xs, w1, w3, w2):
    n_rows, half = xs.shape
    d = D_MODEL
    block_rows = SLOT_ROWS * ROW_SUBLANES
    blk = lambda s, be, nb: (jnp.minimum(s, nb[0] - 1), 0)
    wblk = lambda s, be, nb: (be[jnp.minimum(s, nb[0] - 1)], 0, 0)
    grid_spec = pltpu.PrefetchScalarGridSpec(
        num_scalar_prefetch=2,
        grid=(n_rows // SLOT_ROWS,),
        in_specs=[pl.BlockSpec((SLOT_ROWS, half), blk),
                  pl.BlockSpec((1, d, EXPERT_FF), wblk),
                  pl.BlockSpec((1, d, EXPERT_FF), wblk),
                  pl.BlockSpec((1, EXPERT_FF, d), wblk)],
        out_specs=pl.BlockSpec((block_rows, LANES), blk),
        scratch_shapes=[pltpu.VMEM((d, EXPERT_FF), BF16),
                        pltpu.VMEM((d, EXPERT_FF), BF16),
                        pltpu.VMEM((EXPERT_FF, d), BF16)],
    )
    return pl.pallas_call(
        _experts_kernel,
        grid_spec=grid_spec,
        out_shape=jax.ShapeDtypeStruct((n_rows * ROW_SUBLANES, LANES), F32),
        compiler_params=_cparams(("arbitrary",)),
        name="routed_experts",
    )(blk_e, nblk_used, xs, w1, w3, w2)


def _combine_kernel(pos_hbm, ys_hbm, w_ref, h_ref, x1_ref, ws1_ref, ws3_ref, ws2_ref, gt2_ref, g_ref, o_ref,
                    pos_smem, ybuf, idx_sem, row_sem):
    i = pl.program_id(0)
    tm, d = x1_ref.shape
    idx_copy = pltpu.make_async_copy(pos_hbm.at[i], pos_smem, idx_sem)
    idx_copy.start()
    idx_copy.wait()

    def tile_rows(first):
        return pl.ds(pl.multiple_of(first * ROW_SUBLANES, ROW_SUBLANES), ROW_SUBLANES)

    def row_copy(k, t):
        return pltpu.make_async_copy(ys_hbm.at[tile_rows(pos_smem[k * tm + t]), :],
                                     ybuf.at[k, tile_rows(t), :], row_sem)

    def issue_rows(t, carry):
        for k in range(TOP_K):
            row_copy(k, t).start(priority=k % 2)
        return carry

    def drain_rows(t, carry):
        for k in range(TOP_K):
            row_copy(k, t).wait()
        return carry

    lax.fori_loop(0, tm, issue_rows, 0, unroll=4)

    h = h_ref[...]
    mid = (_silu(_dot(h, ws1_ref[...])) * _dot(h, ws3_ref[...])).astype(BF16)
    shared = _dot(mid, ws2_ref[...])

    lax.fori_loop(0, tm, drain_rows, 0, unroll=4)

    w = w_ref[...]
    w_cols = [jnp.broadcast_to(w[:, k:k + 1], (tm, LANES)) for k in range(TOP_K)]
    pieces = []
    for j in range(ROW_SUBLANES):
        rows = pl.ds(j, tm, stride=ROW_SUBLANES)
        acc = ybuf[0, rows, :] * w_cols[0]
        for k in range(1, TOP_K):
            acc = acc + ybuf[k, rows, :] * w_cols[k]
        pieces.append(acc)
    routed = jnp.concatenate(pieces, axis=1)
    x2 = x1_ref[...] + gt2_ref[0] * (routed + shared)
    ms = jnp.mean(x2 * x2, axis=-1, keepdims=True)
    o_ref[...] = x2 * lax.rsqrt(ms + NORM_EPS) * g_ref[...]


def _combine(pos2, ys, wts, h2, x1, ws1, ws3, ws2, gt2, g_final, seq):
    t, d = x1.shape
    tm = pos2.shape[1] // TOP_K
    per_b = seq // tm
    row = lambda i: (i, 0)
    full = lambda i: (0, 0)
    return pl.pallas_call(
        _combine_kernel,
        grid=(t // tm,),
        in_specs=[pl.BlockSpec(memory_space=pl.ANY),
                  pl.BlockSpec(memory_space=pl.ANY),
                  pl.BlockSpec((tm, TOP_K), row),
                  pl.BlockSpec((tm, d), row),
                  pl.BlockSpec((tm, d), row),
                  pl.BlockSpec((d, SHARED_FF), full),
                  pl.BlockSpec((d, SHARED_FF), full),
                  pl.BlockSpec((SHARED_FF, d), full),
                  pl.BlockSpec((1, 1, d), lambda i: (i // per_b, 0, 0)),
                  pl.BlockSpec((1, d), full)],
        out_specs=pl.BlockSpec((tm, d), row),
        out_shape=jax.ShapeDtypeStruct((t, d), F32),
        scratch_shapes=[pltpu.SMEM((TOP_K * tm,), jnp.int32),
                        pltpu.VMEM((TOP_K, tm * ROW_SUBLANES, LANES), F32),
                        pltpu.SemaphoreType.DMA,
                        pltpu.SemaphoreType.DMA],
        compiler_params=_cparams(("arbitrary",)),
        name="combine_shared_final",
    )(pos2, ys, wts, h2, x1, ws1, ws3, ws2, gt2, g_final)


def _slot_tables(cnt, t):
    ntiles = cnt.shape[1] // 128
    cnt_tile = cnt.reshape(N_EXPERTS, ntiles, 128)[:, :, 0].astype(jnp.int32)
    counts = jnp.sum(cnt_tile, axis=1)
    padded = (counts + SLOT_ROWS - 1) // SLOT_ROWS * SLOT_ROWS
    pstart = jnp.cumsum(padded) - padded
    tile_base = pstart[:, None] + jnp.cumsum(cnt_tile, axis=1) - cnt_tile
    n_blk = -(-(t * TOP_K) // SLOT_ROWS) + N_EXPERTS
    blk_end = jnp.cumsum(padded // SLOT_ROWS)
    blk_e = jnp.sum((blk_end[None, :] <= jnp.arange(n_blk)[:, None]).astype(jnp.int32), axis=1)
    blk_e = jnp.minimum(blk_e, N_EXPERTS - 1)
    return (blk_e, blk_end[-1:].astype(jnp.int32), pstart + counts, padded - counts,
            tile_base.T.astype(F32).reshape(ntiles, N_EXPERTS, 1), n_blk * SLOT_ROWS)


def _permute_in_cols(w_in):
    qa, ka, va, qr, kr, vr, gr, ga, gt = jnp.split(
        w_in, np.cumsum((A_WIDTH, A_WIDTH, A_WIDTH, R_QK_WIDTH, R_QK_WIDTH, R_V_WIDTH, R_V_WIDTH,
                         D_MODEL))[:].tolist(), axis=1)
    return jnp.concatenate([vr, gr, ga, gt, qa, ka, va, qr, kr], axis=1)


def kernel(x, c, w_ada, b_ada, g_mix, w_in, w_pa, w_pr, w_out, g_ffn, w_router, router_bias,
           w1, w3, w2, ws1, ws3, ws2, g_final):
    bsz, seq, d = x.shape
    t = bsz * seq
    depth = w_ada.shape[0]
    assert depth == 1, "the final norm is fused into the single layer's last kernel"
    slopes = jnp.exp2(-8.0 / A_HEADS * jnp.arange(1, A_HEADS + 1, dtype=F32))
    x2d = x.reshape(t, d)
    for l in range(depth):
        mod = _ada(c, w_ada[l], b_ada[l])
        sh1, sc1, gt1, sh2, sc2, gt2 = [m.reshape(bsz, 1, d) for m in jnp.split(mod, 6, axis=-1)]
        w_in_p = _permute_in_cols(w_in[l]).astype(BF16)
        proj = _inproj(x2d, g_mix[l].reshape(1, d), sc1, sh1, w_in_p, seq)
        proj3 = proj.reshape(bsz, seq, IN_COLS)
        k_aug, q_aug_t, v_aug_t = _moba_prep(proj3, slopes)
        ya = _moba_attn(k_aug, q_aug_t, v_aug_t).reshape(t, A_WIDTH)
        yr = _retention(proj3).reshape(t, R_V_WIDTH)
        x1, h2, h2p = _mix(ya, yr, proj, x2d, w_pa[l].astype(BF16), w_pr[l].astype(BF16),
                           w_out[l].astype(BF16), gt1, g_ffn[l].reshape(1, d), sc2, sh2, seq)
        eidx_t, wts_t, rank_t, cnt = _router(h2, w_router[l].T.astype(BF16),
                                             router_bias[l].reshape(N_EXPERTS, 1))
        blk_e, nblk_used, pad_start, pad_len, tile_base, n_rows = _slot_tables(cnt, t)
        pos3 = _positions(eidx_t, rank_t, tile_base)
        pos2 = pos3.reshape(pos3.shape[0], TOP_K * pos3.shape[2])
        pos_blocks = jnp.transpose(
            pos3.reshape(pos3.shape[0], TOP_K, -1, SC_DISPATCH_TOKENS), (0, 2, 1, 3)
        ).reshape(t // SC_DISPATCH_TOKENS, TOP_K, SC_DISPATCH_TOKENS)
        xs = _zero_pads(pad_start, pad_len, _sc_dispatch(pos_blocks, h2p, n_rows))
        ys = _experts(blk_e, nblk_used, xs, w1[l], w3[l], w2[l])
        x2d = _combine(pos2, ys, wts_t.T, h2, x1, ws1[l].astype(BF16), ws3[l].astype(BF16),
                       ws2[l].astype(BF16), gt2, g_final.reshape(1, d), seq)
    return x2d.reshape(bsz, seq, d)
```

```python
import functools

import jax
import jax.numpy as jnp
import numpy as np
from jax import lax
from jax.experimental import pallas as pl
from jax.experimental.pallas import tpu as pltpu
from jax.experimental.pallas import tpu_sc as plsc

F32 = jnp.float32
BF16 = jnp.bfloat16

D_MODEL = 1024
A_HEADS = 8
A_HEAD_DIM = 64
A_WIDTH = A_HEADS * A_HEAD_DIM
MOBA_BLOCK = 256
MOBA_TOPK = 3
R_HEADS = 8
R_QK_DIM = 64
R_V_DIM = 128
R_QK_WIDTH = R_HEADS * R_QK_DIM
R_V_WIDTH = R_HEADS * R_V_DIM
R_CHUNK = 128
N_EXPERTS = 256
TOP_K = 8
N_GROUPS = 8
GROUP_SIZE = N_EXPERTS // N_GROUPS
TOPK_GROUPS = 4
EXPERT_FF = 256
SHARED_FF = 256
ROUTED_SCALE = 2.5
NORM_EPS = 1e-6
GN_EPS = 1e-6
NEG_INF = -1e30
KNOCKED_OUT = -3e38

COL_VR, COL_GR, COL_GA, COL_GT = 0, 1024, 2048, 3072
COL_QA, COL_KA, COL_VA, COL_QR, COL_KR = 4096, 4608, 5120, 5632, 6144
IN_COLS = 6656
AUG = 128
FEAT_BIAS = A_HEAD_DIM
FEAT_POS = A_HEAD_DIM + 32

VMEM_LIMIT = 56 * 1024 * 1024


def _cparams(sem, vmem=VMEM_LIMIT):
    return pltpu.CompilerParams(dimension_semantics=sem, vmem_limit_bytes=vmem)


def _dot(a, b):
    return jnp.dot(a, b, preferred_element_type=F32)


def _dot_nt(a, b):
    return lax.dot_general(a, b, (((1,), (1,)), ((), ())), preferred_element_type=F32)


def _sigmoid(x):
    return 1.0 / (1.0 + jnp.exp(-x))


def _silu(x):
    return x * _sigmoid(x)


def _ada_kernel(c_ref, w_ref, b_ref, o_ref):
    c = c_ref[...]
    s = _silu(c)
    s_hi = s.astype(BF16)
    s_lo = (s - s_hi.astype(F32)).astype(BF16)
    w = w_ref[...]
    w_hi = w.astype(BF16)
    w_lo = (w - w_hi.astype(F32)).astype(BF16)
    o_ref[...] = _dot(s_hi, w_hi) + _dot(s_hi, w_lo) + _dot(s_lo, w_hi) + b_ref[...]


def _ada(c, w_ada, b_ada):
    bsz, d = c.shape
    n = w_ada.shape[1]
    tn = 1024
    return pl.pallas_call(
        _ada_kernel,
        grid=(n // tn,),
        in_specs=[pl.BlockSpec((bsz, d), lambda j: (0, 0)),
                  pl.BlockSpec((d, tn), lambda j: (0, j)),
                  pl.BlockSpec((1, tn), lambda j: (0, j))],
        out_specs=pl.BlockSpec((bsz, tn), lambda j: (0, j)),
        out_shape=jax.ShapeDtypeStruct((bsz, n), F32),
        compiler_params=_cparams(("parallel",)),
        name="ada_mod",
    )(c, w_ada, b_ada.reshape(1, n))


INPROJ_COLS = 512


def _inproj_kernel(x_ref, g_ref, sc_ref, sh_ref, w_ref, o_ref):
    x = x_ref[...]
    ms = jnp.mean(x * x, axis=-1, keepdims=True)
    y = x * lax.rsqrt(ms + NORM_EPS) * g_ref[...]
    h = (y * (1.0 + sc_ref[0]) + sh_ref[0]).astype(BF16)
    for j in range(w_ref.shape[1] // INPROJ_COLS):
        cols = slice(j * INPROJ_COLS, (j + 1) * INPROJ_COLS)
        o_ref[:, cols] = _dot(h, w_ref[:, cols]).astype(BF16)


def _inproj(x2d, g, sc, sh, w_bf16, seq):
    t, d = x2d.shape
    n = w_bf16.shape[1]
    tm = min(512, seq)
    per_b = seq // tm
    return pl.pallas_call(
        _inproj_kernel,
        grid=(t // tm,),
        in_specs=[pl.BlockSpec((tm, d), lambda i: (i, 0)),
                  pl.BlockSpec((1, d), lambda i: (0, 0)),
                  pl.BlockSpec((1, 1, d), lambda i: (i // per_b, 0, 0)),
                  pl.BlockSpec((1, 1, d), lambda i: (i // per_b, 0, 0)),
                  pl.BlockSpec((d, n), lambda i: (0, 0))],
        out_specs=pl.BlockSpec((tm, n), lambda i: (i, 0)),
        out_shape=jax.ShapeDtypeStruct((t, n), BF16),
        compiler_params=_cparams(("parallel",)),
        name="norm_inproj",
    )(x2d, g, sc, sh, w_bf16)


def _moba_prep_kernel(slopes_ref, q_ref, k_ref, v_ref, ko_ref, qo_ref, vo_ref, kmean_scr):
    i = pl.program_id(1)
    nblk = kmean_scr.shape[0]
    width = q_ref.shape[1]

    @pl.when(i == 0)
    def _():
        kmean_scr[...] = jnp.zeros_like(kmean_scr)

    q = q_ref[...]
    k = k_ref[...]
    v = v_ref[...]
    kmean_scr[pl.ds(i, 1), :] = jnp.mean(k.astype(F32), axis=0, keepdims=True)

    eye = (lax.broadcasted_iota(jnp.int32, (width, width), 0)
           == lax.broadcasted_iota(jnp.int32, (width, width), 1)).astype(BF16)
    q_t = _dot_nt(eye, q)
    v_t = _dot_nt(eye, v)

    km = kmean_scr[...]
    km_rep = jnp.concatenate([km] * A_HEADS, axis=0)
    r_head = lax.broadcasted_iota(jnp.int32, km_rep.shape, 0) // nblk
    c_head = lax.broadcasted_iota(jnp.int32, km_rep.shape, 1) // A_HEAD_DIM
    km_bd = jnp.where(r_head == c_head, km_rep, 0.0)
    km_hi = km_bd.astype(BF16)
    km_lo = (km_bd - km_hi.astype(F32)).astype(BF16)
    q_t_b = q_t.astype(BF16)
    gate_all = _dot(km_hi, q_t_b) + _dot(km_lo, q_t_b)

    mb = q.shape[0]
    blk = lax.broadcasted_iota(jnp.int32, (nblk, mb), 0)
    lane_pos = lax.broadcasted_iota(jnp.int32, (16, mb), 1).astype(F32)
    row16 = lax.broadcasted_iota(jnp.int32, (16, mb), 0)
    key_pos = lax.broadcasted_iota(jnp.int32, (mb, AUG), 0).astype(F32)
    kcol = lax.broadcasted_iota(jnp.int32, (mb, AUG), 1)
    sel_r = lax.broadcasted_iota(jnp.int32, (width, AUG), 0)
    sel_c = lax.broadcasted_iota(jnp.int32, (width, AUG), 1)

    for h in range(A_HEADS):
        slope = slopes_ref[h]
        g = jnp.where(blk < i, gate_all[h * nblk:(h + 1) * nblk, :], NEG_INF)
        sel = jnp.zeros((nblk, mb), jnp.bool_)
        for r in range(MOBA_TOPK):
            m = jnp.max(g, axis=0, keepdims=True)
            idx = jnp.min(jnp.where(g == m, blk, nblk), axis=0, keepdims=True)
            hit = blk == idx
            sel = jnp.logical_or(sel, jnp.logical_and(hit, r < i))
            g = jnp.where(hit, KNOCKED_OUT, g)
        bias_t = jnp.where(sel, 0.0, NEG_INF)

        scale = A_HEAD_DIM ** -0.5
        qo_ref[0, h, 0:A_HEAD_DIM, :] = (q_t[h * A_HEAD_DIM:(h + 1) * A_HEAD_DIM, :] * scale).astype(BF16)
        qo_ref[0, h, FEAT_BIAS:FEAT_BIAS + nblk, :] = bias_t.astype(BF16)
        if nblk < 32:
            qo_ref[0, h, FEAT_BIAS + nblk:FEAT_POS, :] = jnp.zeros((32 - nblk, mb), BF16)
        blk_off = slope * (i * mb).astype(F32)
        pos_feat = jnp.where(row16 == 0, -slope * lane_pos,
                             jnp.where(row16 == 2, -blk_off,
                                       jnp.where(jnp.logical_or(row16 == 1, row16 == 3), 1.0, 0.0)))
        qo_ref[0, h, FEAT_POS:FEAT_POS + 16, :] = pos_feat.astype(BF16)
        qo_ref[0, h, FEAT_POS + 16:AUG, :] = jnp.zeros((AUG - FEAT_POS - 16, mb), BF16)

        vo_ref[0, h, 0:A_HEAD_DIM, :] = v_t[h * A_HEAD_DIM:(h + 1) * A_HEAD_DIM, :].astype(BF16)
        vo_ref[0, h, A_HEAD_DIM:A_HEAD_DIM + 16, :] = jnp.where(row16 == 0, 1.0, 0.0).astype(BF16)
        vo_ref[0, h, A_HEAD_DIM + 16:AUG, :] = jnp.zeros((AUG - A_HEAD_DIM - 16, mb), BF16)

        pick = jnp.where(jnp.logical_and(sel_r == sel_c + h * A_HEAD_DIM, sel_c < A_HEAD_DIM),
                         1.0, 0.0).astype(BF16)
        k_feat = jnp.where(
            jnp.logical_or(kcol == FEAT_BIAS + i, jnp.logical_or(kcol == FEAT_POS, kcol == FEAT_POS + 2)), 1.0,
            jnp.where(kcol == FEAT_POS + 1, slope * key_pos, jnp.where(kcol == FEAT_POS + 3, blk_off, 0.0)))
        ko_ref[0, h, :, :] = (_dot(k, pick) + k_feat).astype(BF16)


def _moba_prep(proj3, slopes):
    bsz, seq, _ = proj3.shape
    nblk = seq // MOBA_BLOCK
    mb = MOBA_BLOCK
    grid_spec = pltpu.PrefetchScalarGridSpec(
        num_scalar_prefetch=1,
        grid=(bsz, nblk),
        in_specs=[pl.BlockSpec((None, mb, A_WIDTH), lambda b, i, s: (b, i, COL_QA // A_WIDTH)),
                  pl.BlockSpec((None, mb, A_WIDTH), lambda b, i, s: (b, i, COL_KA // A_WIDTH)),
                  pl.BlockSpec((None, mb, A_WIDTH), lambda b, i, s: (b, i, COL_VA // A_WIDTH))],
        out_specs=[pl.BlockSpec((1, A_HEADS, mb, AUG), lambda b, i, s: (b, 0, i, 0)),
                   pl.BlockSpec((1, A_HEADS, AUG, mb), lambda b, i, s: (b, 0, 0, i)),
                   pl.BlockSpec((1, A_HEADS, AUG, mb), lambda b, i, s: (b, 0, 0, i))],
        scratch_shapes=[pltpu.VMEM((nblk, A_WIDTH), F32)],
    )
    return pl.pallas_call(
        _moba_prep_kernel,
        grid_spec=grid_spec,
        out_shape=[jax.ShapeDtypeStruct((bsz, A_HEADS, seq, AUG), BF16),
                   jax.ShapeDtypeStruct((bsz, A_HEADS, AUG, seq), BF16),
                   jax.ShapeDtypeStruct((bsz, A_HEADS, AUG, seq), BF16)],
        compiler_params=_cparams(("parallel", "arbitrary")),
        name="moba_prep",
    )(slopes, proj3, proj3, proj3)


def _moba_attn_kernel(q_ref, k_ref, v_ref, o_ref, s_a, s_b, *, group, n_groups):
    i = pl.program_id(2)
    mb = MOBA_BLOCK
    span = group * mb
    own = pl.multiple_of(i * mb, mb)
    key_i = lax.broadcasted_iota(jnp.int32, (mb, mb), 0)
    qry_i = lax.broadcasted_iota(jnp.int32, (mb, mb), 1)
    feat = lax.broadcasted_iota(jnp.int32, (AUG, mb), 0)
    is_bias = jnp.logical_and(feat >= FEAT_BIAS, feat < FEAT_POS)
    q_ts, carry0 = [], []
    for hh in range(2):
        q_t = q_ref[0, hh]
        q_ts.append(q_t)
        q_own = jnp.where(is_bias, jnp.zeros_like(q_t), q_t)
        s = _dot(k_ref[0, hh, pl.ds(own, mb), :], q_own)
        s = jnp.where(key_i <= qry_i, s, NEG_INF)
        m0 = jnp.max(s, axis=0, keepdims=True)
        p = jnp.exp(s - m0)
        carry0 += [m0, _dot(v_ref[0, hh, :, pl.ds(own, mb)], p.astype(BF16))]

    def scores(g, dst):
        start = pl.multiple_of(jnp.minimum(g, n_groups - 1) * span, span)
        for hh in range(2):
            dst[hh] = _dot(k_ref[0, hh, pl.ds(start, span), :], q_ts[hh])

    def absorb(g, src, carry):
        start = pl.multiple_of(g * span, span)
        new = []
        for hh in range(2):
            m, acc = carry[2 * hh], carry[2 * hh + 1]
            sb = src[hh]
            m_new = jnp.maximum(m, jnp.max(sb, axis=0, keepdims=True))
            pb = jnp.exp(sb - m_new)
            alpha = jnp.exp(m - m_new)
            acc = acc * alpha + _dot(v_ref[0, hh, :, pl.ds(start, span)], pb.astype(BF16))
            new += [m_new, acc]
        return tuple(new)

    def body(pair, carry):
        scores(2 * pair + 1, s_b)
        carry = absorb(2 * pair, s_a, carry)
        scores(2 * pair + 2, s_a)
        return absorb(2 * pair + 1, s_b, carry)

    scores(0, s_a)
    live_groups = (i + group - 1) // group
    res = lax.fori_loop(0, (live_groups + 1) // 2, body, tuple(carry0))
    outs = [res[2 * hh + 1][0:A_HEAD_DIM, :] / res[2 * hh + 1][A_HEAD_DIM:A_HEAD_DIM + 1, :] for hh in range(2)]
    o_t = jnp.concatenate(outs, axis=0).astype(BF16)
    eye = (key_i == qry_i).astype(BF16)
    o_ref[0] = _dot_nt(eye, o_t).astype(BF16)


def _moba_attn(k_aug, q_aug_t, v_aug_t):
    bsz, nh, seq, _ = k_aug.shape
    mb = MOBA_BLOCK
    group = min(2, seq // mb)
    n_groups = seq // (group * mb)
    return pl.pallas_call(
        functools.partial(_moba_attn_kernel, group=group, n_groups=n_groups),
        grid=(bsz, nh // 2, seq // mb),
        in_specs=[pl.BlockSpec((1, 2, AUG, mb), lambda b, h, i: (b, h, 0, i)),
                  pl.BlockSpec((1, 2, seq, AUG), lambda b, h, i: (b, h, 0, 0)),
                  pl.BlockSpec((1, 2, AUG, seq), lambda b, h, i: (b, h, 0, 0))],
        out_specs=pl.BlockSpec((1, mb, 2 * A_HEAD_DIM), lambda b, h, i: (b, i, h)),
        out_shape=jax.ShapeDtypeStruct((bsz, seq, A_WIDTH), BF16),
        scratch_shapes=[pltpu.VMEM((2, group * mb, mb), F32), pltpu.VMEM((2, group * mb, mb), F32)],
        compiler_params=_cparams(("parallel", "parallel", "arbitrary")),
        name="moba_attn",
    )(q_aug_t, k_aug, v_aug_t)


def _retention_kernel(cdec_ref, q_ref, k_ref, v_ref, g_ref, decay_ref, qdec_ref, kdec_ref, o_ref, state_scr):
    @pl.when(pl.program_id(1) == 0)
    def _():
        state_scr[...] = jnp.zeros_like(state_scr)

    q = q_ref[...]
    k = k_ref[...]
    width = q.shape[1]
    eye = (lax.broadcasted_iota(jnp.int32, (width, width), 0)
           == lax.broadcasted_iota(jnp.int32, (width, width), 1)).astype(BF16)
    k_t = _dot_nt(eye, k)
    k_t_b = k_t.astype(BF16)
    state_b = state_scr[...].astype(BF16)
    for h in range(R_HEADS):
        rows = slice(h * R_QK_DIM, (h + 1) * R_QK_DIM)
        cols = slice(h * R_V_DIM, (h + 1) * R_V_DIM)
        q_h = q[:, rows]
        v_h = v_ref[:, cols]
        inner = _dot(q_h, k_t_b[rows, :]) * decay_ref[h]
        out = _dot(inner.astype(BF16), v_h) + _dot(q_h, state_b[rows, :]) * qdec_ref[h]
        k_dec = (k_t[rows, :] * kdec_ref[h]).astype(BF16)
        state_scr[rows, :] = cdec_ref[h] * state_scr[rows, :] + _dot(k_dec, v_h)
        mu = jnp.mean(out, axis=-1, keepdims=True)
        cen = out - mu
        var = jnp.mean(cen * cen, axis=-1, keepdims=True)
        y = cen * lax.rsqrt(var + GN_EPS)
        o_ref[:, cols] = (y * _silu(g_ref[:, cols].astype(F32))).astype(BF16)


def _retention_consts():
    h = np.arange(R_HEADS, dtype=np.float64)
    log_g = np.log(1.0 - np.exp2(-5.0 - h))
    n = np.arange(R_CHUNK, dtype=np.float64)
    diff = n[:, None] - n[None, :]
    scale = R_QK_DIM ** -0.5
    decay = np.where(diff >= 0, np.exp(np.maximum(diff, 0.0) * log_g[:, None, None]), 0.0) * scale
    q_decay = np.exp((n + 1.0) * log_g[:, None])[:, :, None]
    k_decay = np.exp((R_CHUNK - 1.0 - n) * log_g[:, None])[:, None, :] * scale
    chunk_decay = np.exp(R_CHUNK * log_g)
    return (jnp.asarray(decay, F32), jnp.asarray(q_decay, F32), jnp.asarray(k_decay, F32),
            jnp.asarray(chunk_decay, F32))


def _retention(proj3):
    bsz, seq, _ = proj3.shape
    c = R_CHUNK
    decay, qdec, kdec, cdec = _retention_consts()
    grid_spec = pltpu.PrefetchScalarGridSpec(
        num_scalar_prefetch=1,
        grid=(bsz, seq // c),
        in_specs=[pl.BlockSpec((None, c, R_QK_WIDTH), lambda b, i, s: (b, i, COL_QR // R_QK_WIDTH)),
                  pl.BlockSpec((None, c, R_QK_WIDTH), lambda b, i, s: (b, i, COL_KR // R_QK_WIDTH)),
                  pl.BlockSpec((None, c, R_V_WIDTH), lambda b, i, s: (b, i, COL_VR // R_V_WIDTH)),
                  pl.BlockSpec((None, c, R_V_WIDTH), lambda b, i, s: (b, i, COL_GR // R_V_WIDTH)),
                  pl.BlockSpec((R_HEADS, c, c), lambda b, i, s: (0, 0, 0)),
                  pl.BlockSpec((R_HEADS, c, 1), lambda b, i, s: (0, 0, 0)),
                  pl.BlockSpec((R_HEADS, 1, c), lambda b, i, s: (0, 0, 0))],
        out_specs=pl.BlockSpec((None, c, R_V_WIDTH), lambda b, i, s: (b, i, 0)),
        scratch_shapes=[pltpu.VMEM((R_QK_WIDTH, R_V_DIM), F32)],
    )
    return pl.pallas_call(
        _retention_kernel,
        grid_spec=grid_spec,
        out_shape=jax.ShapeDtypeStruct((bsz, seq, R_V_WIDTH), BF16),
        compiler_params=_cparams(("parallel", "arbitrary")),
        name="retention",
    )(cdec, proj3, proj3, proj3, proj3, decay, qdec, kdec)


def _pack_halves(x):
    w = x.shape[1] // 2
    bits = lax.bitcast_convert_type(x.astype(BF16).astype(F32), jnp.uint32)
    return (bits[:, :w] >> 16) | (bits[:, w:] & jnp.uint32(0xFFFF0000))


def _unpack_halves(p):
    lo = lax.bitcast_convert_type(p << 16, F32)
    hi = lax.bitcast_convert_type(p & jnp.uint32(0xFFFF0000), F32)
    return lo, hi


LANES = 128
ROW_SUBLANES = D_MODEL // LANES


def _store_tile_rows(ref, lead, x):
    m = x.shape[0]
    for j in range(ROW_SUBLANES):
        ref[lead + (pl.ds(j, m, stride=ROW_SUBLANES), slice(None))] = x[:, j * LANES:(j + 1) * LANES]


def _load_tile_rows(ref, lead, m):
    return [ref[lead + (pl.ds(j, m, stride=ROW_SUBLANES), slice(None))] for j in range(ROW_SUBLANES)]


def _mix_kernel(ya_ref, yr_ref, ga_ref, gt_ref, x_ref, wpa_ref, wpr_ref, wout_ref,
                gt1_ref, g_ref, sc_ref, sh_ref, x1_ref, h2_ref, h2p_ref):
    a = _dot(ya_ref[...], wpa_ref[...]) * _sigmoid(ga_ref[...].astype(F32))
    r = _dot(yr_ref[...], wpr_ref[...]) * _sigmoid(gt_ref[...].astype(F32))
    mix = _dot((a + r).astype(BF16), wout_ref[...])
    x1 = x_ref[...] + gt1_ref[0] * mix
    x1_ref[...] = x1
    ms = jnp.mean(x1 * x1, axis=-1, keepdims=True)
    y = x1 * lax.rsqrt(ms + NORM_EPS) * g_ref[...]
    h2 = y * (1.0 + sc_ref[0]) + sh_ref[0]
    h2_ref[...] = h2.astype(BF16)
    h2p_ref[...] = _pack_halves(h2)


def _mix(ya, yr, proj, x2d, wpa, wpr, wout, gt1, g, sc, sh, seq):
    t, d = x2d.shape
    tm = min(512, seq)
    per_b = seq // tm
    row = lambda i: (i, 0)
    full = lambda i: (0, 0)
    per_batch = lambda i: (i // per_b, 0, 0)
    return pl.pallas_call(
        _mix_kernel,
        grid=(t // tm,),
        in_specs=[pl.BlockSpec((tm, A_WIDTH), row),
                  pl.BlockSpec((tm, R_V_WIDTH), row),
                  pl.BlockSpec((tm, d), lambda i: (i, COL_GA // D_MODEL)),
                  pl.BlockSpec((tm, d), lambda i: (i, COL_GT // D_MODEL)),
                  pl.BlockSpec((tm, d), row),
                  pl.BlockSpec((A_WIDTH, d), full),
                  pl.BlockSpec((R_V_WIDTH, d), full),
                  pl.BlockSpec((d, d), full),
                  pl.BlockSpec((1, 1, d), per_batch),
                  pl.BlockSpec((1, d), full),
                  pl.BlockSpec((1, 1, d), per_batch),
                  pl.BlockSpec((1, 1, d), per_batch)],
        out_specs=[pl.BlockSpec((tm, d), row), pl.BlockSpec((tm, d), row),
                   pl.BlockSpec((tm, d // 2), row)],
        out_shape=[jax.ShapeDtypeStruct((t, d), F32), jax.ShapeDtypeStruct((t, d), BF16),
                   jax.ShapeDtypeStruct((t, d // 2), jnp.uint32)],
        compiler_params=_cparams(("parallel",)),
        name="merge_outproj_norm",
    )(ya, yr, proj, proj, x2d, wpa, wpr, wout, gt1, g, sc, sh)


def _router_kernel(h_ref, wr_ref, b_ref, e_ref, w_ref, r_ref, c_ref):
    logits = _dot_nt(wr_ref[...], h_ref[...])
    scores = _sigmoid(logits)
    choice = scores + b_ref[...]
    tm = logits.shape[1]
    giota = lax.broadcasted_iota(jnp.int32, (GROUP_SIZE, tm), 0)
    gs_rows = []
    for g in range(N_GROUPS):
        cg = choice[g * GROUP_SIZE:(g + 1) * GROUP_SIZE, :]
        m1 = jnp.max(cg, axis=0, keepdims=True)
        i1 = jnp.min(jnp.where(cg == m1, giota, GROUP_SIZE), axis=0, keepdims=True)
        m2 = jnp.max(jnp.where(giota == i1, KNOCKED_OUT, cg), axis=0, keepdims=True)
        gs_rows.append(m1 + m2)
    gs = jnp.concatenate(gs_rows, axis=0)
    grow = lax.broadcasted_iota(jnp.int32, (N_GROUPS, tm), 0)
    gmask = jnp.zeros((N_GROUPS, tm), jnp.bool_)
    for _ in range(TOPK_GROUPS):
        mx = jnp.max(gs, axis=0, keepdims=True)
        ix = jnp.min(jnp.where(gs == mx, grow, N_GROUPS), axis=0, keepdims=True)
        hit = grow == ix
        gmask = jnp.logical_or(gmask, hit)
        gs = jnp.where(hit, KNOCKED_OUT, gs)
    gmask_f = jnp.where(gmask, 1.0, 0.0)
    masked = jnp.concatenate(
        [jnp.where(gmask_f[g:g + 1, :] > 0.5, choice[g * GROUP_SIZE:(g + 1) * GROUP_SIZE, :], NEG_INF)
         for g in range(N_GROUPS)], axis=0)
    erow = lax.broadcasted_iota(jnp.int32, (N_EXPERTS, tm), 0)
    idx_rows, w_rows = [], []
    chosen = jnp.zeros((N_EXPERTS, tm), F32)
    for _ in range(TOP_K):
        mx = jnp.max(masked, axis=0, keepdims=True)
        ix = jnp.min(jnp.where(masked == mx, erow, N_EXPERTS), axis=0, keepdims=True)
        hit = erow == ix
        w_rows.append(jnp.sum(jnp.where(hit, scores, 0.0), axis=0, keepdims=True))
        idx_rows.append(ix)
        chosen = jnp.where(hit, 1.0, chosen)
        masked = jnp.where(hit, KNOCKED_OUT, masked)
    w = jnp.concatenate(w_rows, axis=0)
    w = w / (jnp.sum(w, axis=0, keepdims=True) + 1e-20) * ROUTED_SCALE
    e_ref[...] = jnp.concatenate(idx_rows, axis=0)
    w_ref[...] = w
    chosen_b = chosen.astype(BF16)
    earlier = (lax.broadcasted_iota(jnp.int32, (tm, tm), 0)
               < lax.broadcasted_iota(jnp.int32, (tm, tm), 1)).astype(BF16)
    before = _dot(chosen_b, earlier)
    ranks = [jnp.sum(jnp.where(erow == ix, before, 0.0), axis=0, keepdims=True) for ix in idx_rows]
    r_ref[...] = jnp.concatenate(ranks, axis=0).astype(jnp.int32)
    c_ref[...] = _dot(chosen_b, jnp.ones((tm, 128), BF16))


ROUTER_ROWS = 512


def _router(h2, wr_t, bias_col):
    t, d = h2.shape
    tm = ROUTER_ROWS
    by_tile = lambda i: (0, i)
    return pl.pallas_call(
        _router_kernel,
        grid=(t // tm,),
        in_specs=[pl.BlockSpec((tm, d), lambda i: (i, 0)),
                  pl.BlockSpec((N_EXPERTS, d), lambda i: (0, 0)),
                  pl.BlockSpec((N_EXPERTS, 1), lambda i: (0, 0))],
        out_specs=[pl.BlockSpec((TOP_K, tm), by_tile), pl.BlockSpec((TOP_K, tm), by_tile),
                   pl.BlockSpec((TOP_K, tm), by_tile), pl.BlockSpec((N_EXPERTS, 128), by_tile)],
        out_shape=[jax.ShapeDtypeStruct((TOP_K, t), jnp.int32),
                   jax.ShapeDtypeStruct((TOP_K, t), F32),
                   jax.ShapeDtypeStruct((TOP_K, t), jnp.int32),
                   jax.ShapeDtypeStruct((N_EXPERTS, (t // tm) * 128), F32)],
        compiler_params=_cparams(("parallel",)),
        name="router_topk",
    )(h2, wr_t, bias_col)


def _pos_kernel(e_ref, r_ref, base_ref, p_ref):
    tm = e_ref.shape[1]
    erow = lax.broadcasted_iota(jnp.int32, (N_EXPERTS, tm), 0)
    base = base_ref[0]
    rows = [jnp.sum(jnp.where(erow == e_ref[k:k + 1, :], base, 0.0), axis=0, keepdims=True)
            for k in range(TOP_K)]
    p_ref[0] = jnp.concatenate(rows, axis=0).astype(jnp.int32) + r_ref[...]


MOVE_ROWS = 256


def _positions(eidx_t, rank_t, tile_base):
    t = eidx_t.shape[1]
    tm = min(MOVE_ROWS, t)
    per_router_tile = ROUTER_ROWS // tm
    return pl.pallas_call(
        _pos_kernel,
        grid=(t // tm,),
        in_specs=[pl.BlockSpec((TOP_K, tm), lambda i: (0, i)),
                  pl.BlockSpec((TOP_K, tm), lambda i: (0, i)),
                  pl.BlockSpec((1, N_EXPERTS, 1), lambda i: (i // per_router_tile, 0, 0))],
        out_specs=pl.BlockSpec((1, TOP_K, tm), lambda i: (i, 0, 0)),
        out_shape=jax.ShapeDtypeStruct((t // tm, TOP_K, tm), jnp.int32),
        compiler_params=_cparams(("parallel",)),
        name="slot_positions",
    )(eidx_t, rank_t, tile_base)


SLOT_ROWS = 512
PAD_CHUNKS = (256, 128, 64, 32, 16, 8)


def _zero_pads_kernel(pad_start_ref, pad_len_ref, xs_in, xs_hbm, zero_buf, pad_sem):
    del xs_in
    zero_buf[...] = jnp.zeros_like(zero_buf)

    def pad_copies(e, wait):
        start = pad_start_ref[e]
        n = pad_len_ref[e]
        head = jnp.minimum((-start) & (ROW_SUBLANES - 1), n)

        def fill(first, size, pred):
            @pl.when(pred)
            def _():
                cp = pltpu.make_async_copy(zero_buf.at[pl.ds(0, size), :], xs_hbm.at[pl.ds(first, size), :], pad_sem)
                if wait:
                    cp.wait()
                else:
                    cp.start()

        for j in range(ROW_SUBLANES - 1):
            fill(start + j, 1, j < head)
        ptr = start + head
        rest = n - head
        for chunk in PAD_CHUNKS:
            fill(pl.multiple_of(ptr, ROW_SUBLANES), chunk, (rest & chunk) != 0)
            ptr = ptr + (rest & chunk)

    def issue(e, carry):
        pad_copies(e, False)
        return carry

    def drain(e, carry):
        pad_copies(e, True)
        return carry

    lax.fori_loop(0, N_EXPERTS, issue, 0)
    lax.fori_loop(0, N_EXPERTS, drain, 0)


def _zero_pads(pad_start, pad_len, xs):
    grid_spec = pltpu.PrefetchScalarGridSpec(
        num_scalar_prefetch=2,
        grid=(1,),
        in_specs=[pl.BlockSpec(memory_space=pl.ANY)],
        out_specs=pl.BlockSpec(memory_space=pl.ANY),
        scratch_shapes=[pltpu.VMEM((PAD_CHUNKS[0], xs.shape[1]), xs.dtype),
                        pltpu.SemaphoreType.DMA],
    )
    return pl.pallas_call(
        _zero_pads_kernel,
        grid_spec=grid_spec,
        out_shape=jax.ShapeDtypeStruct(xs.shape, xs.dtype),
        input_output_aliases={2: 0},
        compiler_params=_cparams(("arbitrary",)),
        name="zero_pad_slots",
    )(pad_start, pad_len, xs)


SC_DISPATCH_TOKENS = 64


def _sc_dispatch(pos_blocks, h_rows, n_rows):
    info = plsc.get_sparse_core_info()
    n_cores = info.num_cores
    n_workers = n_cores * info.num_subcores
    t = h_rows.shape[0]
    chunk = SC_DISPATCH_TOKENS
    steps = t // (n_workers * chunk)
    mesh = plsc.VectorSubcoreMesh(core_axis_name="c", subcore_axis_name="s")

    @functools.partial(
        pl.kernel, mesh=mesh,
        out_type=jax.ShapeDtypeStruct((n_rows,) + h_rows.shape[1:], h_rows.dtype),
        scratch_types=[pltpu.VMEM((TOP_K, chunk), jnp.int32),
                       pltpu.VMEM((chunk,) + h_rows.shape[1:], h_rows.dtype),
                       pltpu.SemaphoreType.DMA],
        name="sc_dispatch_rows",
    )
    def scatter_rows(pos_hbm, h_hbm, out_hbm, idx_v, rows_v, sem):
        wid = lax.axis_index("s") * n_cores + lax.axis_index("c")

        @pl.loop(0, steps)
        def _(step):
            blk = wid * steps + step
            pltpu.sync_copy(pos_hbm.at[blk], idx_v)
            pltpu.sync_copy(h_hbm.at[pl.ds(blk * chunk, chunk)], rows_v)
            for k in range(TOP_K):
                pltpu.async_copy(rows_v, out_hbm.at[idx_v.at[k]], sem).wait()

    return scatter_rows(pos_blocks, h_rows)


def _experts_kernel(blk_e_ref, nblk_ref, x_ref, w1_ref, w3_ref, w2_ref, y_ref, w1b, w3b, w2b):
    s = pl.program_id(0)

    @pl.when(s < nblk_ref[0])
    def _():
        @pl.when(jnp.logical_or(s == 0, blk_e_ref[s] != blk_e_ref[jnp.maximum(s - 1, 0)]))
        def _():
            w1b[...] = w1_ref[0].astype(BF16)
            w3b[...] = w3_ref[0].astype(BF16)
            w2b[...] = w2_ref[0].astype(BF16)

        half = x_ref.shape[1]
        lo, hi = _unpack_halves(x_ref[...])
        lo = lo.astype(BF16)
        hi = hi.astype(BF16)
        h1 = _dot(lo, w1b[:half, :]) + _dot(hi, w1b[half:, :])
        h3 = _dot(lo, w3b[:half, :]) + _dot(hi, w3b[half:, :])
        mid = (_silu(h1) * h3).astype(BF16)
        y_ref[...] = _pack_halves(_dot(mid, w2b[...]))


def _sc_gather(pos_blocks, ys, t):
    info = plsc.get_sparse_core_info()
    n_cores = info.num_cores
    n_workers = n_cores * info.num_subcores
    chunk = SC_DISPATCH_TOKENS
    steps = t // (n_workers * chunk)
    mesh = plsc.VectorSubcoreMesh(core_axis_name="c", subcore_axis_name="s")

    @functools.partial(
        pl.kernel, mesh=mesh,
        out_type=jax.ShapeDtypeStruct((TOP_K * t,) + ys.shape[1:], ys.dtype),
        scratch_types=[pltpu.VMEM((TOP_K, chunk), jnp.int32),
                       pltpu.VMEM((chunk,) + ys.shape[1:], ys.dtype),
                       pltpu.SemaphoreType.DMA],
        name="sc_gather_rows",
    )
    def gather_rows(pos_hbm, ys_hbm, out_hbm, idx_v, rows_v, sem):
        wid = lax.axis_index("s") * n_cores + lax.axis_index("c")

        @pl.loop(0, steps)
        def _(step):
            blk = wid * steps + step
            pltpu.sync_copy(pos_hbm.at[blk], idx_v)
            for k in range(TOP_K):
                pltpu.async_copy(ys_hbm.at[idx_v.at[k]], rows_v, sem).wait()
                pltpu.sync_copy(rows_v, out_hbm.at[pl.ds(k * t + blk * chunk, chunk)])

    return gather_rows(pos_blocks, ys)


def _experts(blk_e, nblk_used, xs, w1, w3, w2):
    n_rows, half = xs.shape
    d = D_MODEL
    blk = lambda s, be, nb: (jnp.minimum(s, nb[0] - 1), 0)
    wblk = lambda s, be, nb: (be[jnp.minimum(s, nb[0] - 1)], 0, 0)
    grid_spec = pltpu.PrefetchScalarGridSpec(
        num_scalar_prefetch=2,
        grid=(n_rows // SLOT_ROWS,),
        in_specs=[pl.BlockSpec((SLOT_ROWS, half), blk),
                  pl.BlockSpec((1, d, EXPERT_FF), wblk),
                  pl.BlockSpec((1, d, EXPERT_FF), wblk),
                  pl.BlockSpec((1, EXPERT_FF, d), wblk)],
        out_specs=pl.BlockSpec((SLOT_ROWS, half), blk),
        scratch_shapes=[pltpu.VMEM((d, EXPERT_FF), BF16),
                        pltpu.VMEM((d, EXPERT_FF), BF16),
                        pltpu.VMEM((EXPERT_FF, d), BF16)],
    )
    return pl.pallas_call(
        _experts_kernel,
        grid_spec=grid_spec,
        out_shape=jax.ShapeDtypeStruct((n_rows, half), jnp.uint32),
        compiler_params=_cparams(("arbitrary",)),
        name="routed_experts",
    )(blk_e, nblk_used, xs, w1, w3, w2)


def _combine_kernel(*refs):
    y_refs = refs[:TOP_K]
    w_ref, h_ref, x1_ref, ws1_ref, ws3_ref, ws2_ref, gt2_ref, g_ref, o_ref = refs[TOP_K:]
    tm, d = x1_ref.shape
    half = d // 2
    h = h_ref[...]
    mid = (_silu(_dot(h, ws1_ref[...])) * _dot(h, ws3_ref[...])).astype(BF16)
    shared = _dot(mid, ws2_ref[...])
    w = w_ref[...]
    acc_lo = jnp.zeros((tm, half), F32)
    acc_hi = jnp.zeros((tm, half), F32)
    for k in range(TOP_K):
        lo, hi = _unpack_halves(y_refs[k][...])
        acc_lo = acc_lo + lo * w[:, k:k + 1]
        acc_hi = acc_hi + hi * w[:, k:k + 1]
    routed = jnp.concatenate([acc_lo, acc_hi], axis=1)
    x2 = x1_ref[...] + gt2_ref[0] * (routed + shared)
    ms = jnp.mean(x2 * x2, axis=-1, keepdims=True)
    o_ref[...] = x2 * lax.rsqrt(ms + NORM_EPS) * g_ref[...]


COMBINE_ROWS = 256


def _combine(y_kt, wts, h2, x1, ws1, ws3, ws2, gt2, g_final, seq):
    t, d = x1.shape
    tm = min(COMBINE_ROWS, seq)
    per_b = seq // tm
    tiles = t // tm
    row = lambda i: (i, 0)
    full = lambda i: (0, 0)
    y_specs = [pl.BlockSpec((tm, d // 2), functools.partial(lambda i, k: (k * tiles + i, 0), k=k))
               for k in range(TOP_K)]
    return pl.pallas_call(
        _combine_kernel,
        grid=(tiles,),
        in_specs=y_specs + [
                  pl.BlockSpec((tm, TOP_K), row),
                  pl.BlockSpec((tm, d), row),
                  pl.BlockSpec((tm, d), row),
                  pl.BlockSpec((d, SHARED_FF), full),
                  pl.BlockSpec((d, SHARED_FF), full),
                  pl.BlockSpec((SHARED_FF, d), full),
                  pl.BlockSpec((1, 1, d), lambda i: (i // per_b, 0, 0)),
                  pl.BlockSpec((1, d), full)],
        out_specs=pl.BlockSpec((tm, d), row),
        out_shape=jax.ShapeDtypeStruct((t, d), F32),
        compiler_params=_cparams(("parallel",)),
        name="combine_shared_final",
    )(*([y_kt] * TOP_K), wts, h2, x1, ws1, ws3, ws2, gt2, g_final)


def _slot_tables(cnt, t):
    ntiles = cnt.shape[1] // 128
    cnt_tile = cnt.reshape(N_EXPERTS, ntiles, 128)[:, :, 0].astype(jnp.int32)
    counts = jnp.sum(cnt_tile, axis=1)
    padded = (counts + SLOT_ROWS - 1) // SLOT_ROWS * SLOT_ROWS
    pstart = jnp.cumsum(padded) - padded
    tile_base = pstart[:, None] + jnp.cumsum(cnt_tile, axis=1) - cnt_tile
    n_blk = -(-(t * TOP_K) // SLOT_ROWS) + N_EXPERTS
    blk_end = jnp.cumsum(padded // SLOT_ROWS)
    blk_e = jnp.sum((blk_end[None, :] <= jnp.arange(n_blk)[:, None]).astype(jnp.int32), axis=1)
    blk_e = jnp.minimum(blk_e, N_EXPERTS - 1)
    return (blk_e, blk_end[-1:].astype(jnp.int32), pstart + counts, padded - counts,
            tile_base.T.astype(F32).reshape(ntiles, N_EXPERTS, 1), n_blk * SLOT_ROWS)


def _permute_in_cols(w_in):
    qa, ka, va, qr, kr, vr, gr, ga, gt = jnp.split(
        w_in, np.cumsum((A_WIDTH, A_WIDTH, A_WIDTH, R_QK_WIDTH, R_QK_WIDTH, R_V_WIDTH, R_V_WIDTH,
                         D_MODEL))[:].tolist(), axis=1)
    return jnp.concatenate([vr, gr, ga, gt, qa, ka, va, qr, kr], axis=1)


def kernel(x, c, w_ada, b_ada, g_mix, w_in, w_pa, w_pr, w_out, g_ffn, w_router, router_bias,
           w1, w3, w2, ws1, ws3, ws2, g_final):
    bsz, seq, d = x.shape
    t = bsz * seq
    depth = w_ada.shape[0]
    assert depth == 1, "the final norm is fused into the single layer's last kernel"
    slopes = jnp.exp2(-8.0 / A_HEADS * jnp.arange(1, A_HEADS + 1, dtype=F32))
    x2d = x.reshape(t, d)
    for l in range(depth):
        mod = _ada(c, w_ada[l], b_ada[l])
        sh1, sc1, gt1, sh2, sc2, gt2 = [m.reshape(bsz, 1, d) for m in jnp.split(mod, 6, axis=-1)]
        w_in_p = _permute_in_cols(w_in[l]).astype(BF16)
        proj = _inproj(x2d, g_mix[l].reshape(1, d), sc1, sh1, w_in_p, seq)
        proj3 = proj.reshape(bsz, seq, IN_COLS)
        k_aug, q_aug_t, v_aug_t = _moba_prep(proj3, slopes)
        ya = _moba_attn(k_aug, q_aug_t, v_aug_t).reshape(t, A_WIDTH)
        yr = _retention(proj3).reshape(t, R_V_WIDTH)
        x1, h2, h2p = _mix(ya, yr, proj, x2d, w_pa[l].astype(BF16), w_pr[l].astype(BF16),
                           w_out[l].astype(BF16), gt1, g_ffn[l].reshape(1, d), sc2, sh2, seq)
        eidx_t, wts_t, rank_t, cnt = _router(h2, w_router[l].T.astype(BF16),
                                             router_bias[l].reshape(N_EXPERTS, 1))
        blk_e, nblk_used, pad_start, pad_len, tile_base, n_rows = _slot_tables(cnt, t)
        pos3 = _positions(eidx_t, rank_t, tile_base)
        pos_blocks = jnp.transpose(
            pos3.reshape(pos3.shape[0], TOP_K, -1, SC_DISPATCH_TOKENS), (0, 2, 1, 3)
        ).reshape(t // SC_DISPATCH_TOKENS, TOP_K, SC_DISPATCH_TOKENS)
        xs = _zero_pads(pad_start, pad_len, _sc_dispatch(pos_blocks, h2p, n_rows))
        ys = _experts(blk_e, nblk_used, xs, w1[l], w3[l], w2[l])
        y_kt = _sc_gather(pos_blocks, ys, t)
        x2d = _combine(y_kt, wts_t.T, h2, x1, ws1[l].astype(BF16), ws3[l].astype(BF16),
                       ws2[l].astype(BF16), gt2, g_final.reshape(1, d), seq)
    return x2d.reshape(bsz, seq, d)
```

```python
import functools

import jax
import jax.numpy as jnp
import numpy as np
from jax import lax
from jax.experimental import pallas as pl
from jax.experimental.pallas import tpu as pltpu
from jax.experimental.pallas import tpu_sc as plsc

F32 = jnp.float32
BF16 = jnp.bfloat16

D_MODEL = 1024
A_HEADS = 8
A_HEAD_DIM = 64
A_WIDTH = A_HEADS * A_HEAD_DIM
MOBA_BLOCK = 256
MOBA_TOPK = 3
R_HEADS = 8
R_QK_DIM = 64
R_V_DIM = 128
R_QK_WIDTH = R_HEADS * R_QK_DIM
R_V_WIDTH = R_HEADS * R_V_DIM
R_CHUNK = 128
N_EXPERTS = 256
TOP_K = 8
N_GROUPS = 8
GROUP_SIZE = N_EXPERTS // N_GROUPS
TOPK_GROUPS = 4
EXPERT_FF = 256
SHARED_FF = 256
ROUTED_SCALE = 2.5
NORM_EPS = 1e-6
GN_EPS = 1e-6
NEG_INF = -1e30
KNOCKED_OUT = -3e38

COL_VR, COL_GR, COL_GA, COL_GT = 0, 1024, 2048, 3072
COL_QA, COL_KA, COL_VA, COL_QR, COL_KR = 4096, 4608, 5120, 5632, 6144
IN_COLS = 6656
AUG = 128
FEAT_BIAS = A_HEAD_DIM
FEAT_POS = A_HEAD_DIM + 32
V_ROWS = A_HEAD_DIM + 16

VMEM_LIMIT = 56 * 1024 * 1024


def _cparams(sem, vmem=VMEM_LIMIT):
    return pltpu.CompilerParams(dimension_semantics=sem, vmem_limit_bytes=vmem)


def _dot(a, b):
    return jnp.dot(a, b, preferred_element_type=F32)


def _dot_nt(a, b):
    return lax.dot_general(a, b, (((1,), (1,)), ((), ())), preferred_element_type=F32)


def _sigmoid(x):
    return 1.0 / (1.0 + jnp.exp(-x))


def _silu(x):
    return x * _sigmoid(x)


def _ada_kernel(c_ref, w_ref, b_ref, o_ref):
    c = c_ref[...]
    s = _silu(c)
    s_hi = s.astype(BF16)
    s_lo = (s - s_hi.astype(F32)).astype(BF16)
    w = w_ref[...]
    w_hi = w.astype(BF16)
    w_lo = (w - w_hi.astype(F32)).astype(BF16)
    o_ref[...] = _dot(s_hi, w_hi) + _dot(s_hi, w_lo) + _dot(s_lo, w_hi) + b_ref[...]


def _ada(c, w_ada, b_ada):
    bsz, d = c.shape
    n = w_ada.shape[1]
    tn = 1024
    return pl.pallas_call(
        _ada_kernel,
        grid=(n // tn,),
        in_specs=[pl.BlockSpec((bsz, d), lambda j: (0, 0)),
                  pl.BlockSpec((d, tn), lambda j: (0, j)),
                  pl.BlockSpec((1, tn), lambda j: (0, j))],
        out_specs=pl.BlockSpec((bsz, tn), lambda j: (0, j)),
        out_shape=jax.ShapeDtypeStruct((bsz, n), F32),
        compiler_params=_cparams(("parallel",)),
        name="ada_mod",
    )(c, w_ada, b_ada.reshape(1, n))


INPROJ_COLS = 512


def _inproj_kernel(x_ref, g_ref, sc_ref, sh_ref, w_ref, o_ref):
    x = x_ref[...]
    ms = jnp.mean(x * x, axis=-1, keepdims=True)
    y = x * lax.rsqrt(ms + NORM_EPS) * g_ref[...]
    h = (y * (1.0 + sc_ref[0]) + sh_ref[0]).astype(BF16)
    for j in range(w_ref.shape[1] // INPROJ_COLS):
        cols = slice(j * INPROJ_COLS, (j + 1) * INPROJ_COLS)
        o_ref[:, cols] = _dot(h, w_ref[:, cols]).astype(BF16)


def _inproj(x2d, g, sc, sh, w_bf16, seq):
    t, d = x2d.shape
    n = w_bf16.shape[1]
    tm = min(512, seq)
    per_b = seq // tm
    return pl.pallas_call(
        _inproj_kernel,
        grid=(t // tm,),
        in_specs=[pl.BlockSpec((tm, d), lambda i: (i, 0)),
                  pl.BlockSpec((1, d), lambda i: (0, 0)),
                  pl.BlockSpec((1, 1, d), lambda i: (i // per_b, 0, 0)),
                  pl.BlockSpec((1, 1, d), lambda i: (i // per_b, 0, 0)),
                  pl.BlockSpec((d, n), lambda i: (0, 0))],
        out_specs=pl.BlockSpec((tm, n), lambda i: (i, 0)),
        out_shape=jax.ShapeDtypeStruct((t, n), BF16),
        compiler_params=_cparams(("parallel",)),
        name="norm_inproj",
    )(x2d, g, sc, sh, w_bf16)


def _moba_prep_kernel(slopes_ref, q_ref, k_ref, v_ref, ko_ref, qo_ref, vo_ref, kmean_scr):
    i = pl.program_id(1)
    nblk = kmean_scr.shape[0]
    width = q_ref.shape[1]

    @pl.when(i == 0)
    def _():
        kmean_scr[...] = jnp.zeros_like(kmean_scr)

    q = q_ref[...]
    k = k_ref[...]
    v = v_ref[...]
    kmean_scr[pl.ds(i, 1), :] = jnp.mean(k.astype(F32), axis=0, keepdims=True)

    eye = (lax.broadcasted_iota(jnp.int32, (width, width), 0)
           == lax.broadcasted_iota(jnp.int32, (width, width), 1)).astype(BF16)
    q_t = _dot_nt(eye, q)
    v_t = _dot_nt(eye, v)

    km = kmean_scr[...]
    km_rep = jnp.concatenate([km] * A_HEADS, axis=0)
    r_head = lax.broadcasted_iota(jnp.int32, km_rep.shape, 0) // nblk
    c_head = lax.broadcasted_iota(jnp.int32, km_rep.shape, 1) // A_HEAD_DIM
    km_bd = jnp.where(r_head == c_head, km_rep, 0.0)
    km_hi = km_bd.astype(BF16)
    km_lo = (km_bd - km_hi.astype(F32)).astype(BF16)
    q_t_b = q_t.astype(BF16)
    gate_all = _dot(km_hi, q_t_b) + _dot(km_lo, q_t_b)

    mb = q.shape[0]
    blk = lax.broadcasted_iota(jnp.int32, (nblk, mb), 0)
    lane_pos = lax.broadcasted_iota(jnp.int32, (16, mb), 1).astype(F32)
    row16 = lax.broadcasted_iota(jnp.int32, (16, mb), 0)
    key_pos = lax.broadcasted_iota(jnp.int32, (mb, AUG), 0).astype(F32)
    kcol = lax.broadcasted_iota(jnp.int32, (mb, AUG), 1)
    sel_r = lax.broadcasted_iota(jnp.int32, (width, AUG), 0)
    sel_c = lax.broadcasted_iota(jnp.int32, (width, AUG), 1)

    for h in range(A_HEADS):
        slope = slopes_ref[h]
        g = jnp.where(blk < i, gate_all[h * nblk:(h + 1) * nblk, :], NEG_INF)
        sel = jnp.zeros((nblk, mb), jnp.bool_)
        for r in range(MOBA_TOPK):
            m = jnp.max(g, axis=0, keepdims=True)
            idx = jnp.min(jnp.where(g == m, blk, nblk), axis=0, keepdims=True)
            hit = blk == idx
            sel = jnp.logical_or(sel, jnp.logical_and(hit, r < i))
            g = jnp.where(hit, KNOCKED_OUT, g)
        bias_t = jnp.where(sel, 0.0, NEG_INF)

        scale = A_HEAD_DIM ** -0.5
        qo_ref[0, h, 0:A_HEAD_DIM, :] = (q_t[h * A_HEAD_DIM:(h + 1) * A_HEAD_DIM, :] * scale).astype(BF16)
        qo_ref[0, h, FEAT_BIAS:FEAT_BIAS + nblk, :] = bias_t.astype(BF16)
        if nblk < 32:
            qo_ref[0, h, FEAT_BIAS + nblk:FEAT_POS, :] = jnp.zeros((32 - nblk, mb), BF16)
        blk_off = slope * (i * mb).astype(F32)
        pos_feat = jnp.where(row16 == 0, -slope * lane_pos,
                             jnp.where(row16 == 2, -blk_off,
                                       jnp.where(jnp.logical_or(row16 == 1, row16 == 3), 1.0, 0.0)))
        qo_ref[0, h, FEAT_POS:FEAT_POS + 16, :] = pos_feat.astype(BF16)
        qo_ref[0, h, FEAT_POS + 16:AUG, :] = jnp.zeros((AUG - FEAT_POS - 16, mb), BF16)

        vo_ref[0, h, 0:A_HEAD_DIM, :] = v_t[h * A_HEAD_DIM:(h + 1) * A_HEAD_DIM, :].astype(BF16)
        vo_ref[0, h, A_HEAD_DIM:V_ROWS, :] = jnp.where(row16 == 0, 1.0, 0.0).astype(BF16)

        pick = jnp.where(jnp.logical_and(sel_r == sel_c + h * A_HEAD_DIM, sel_c < A_HEAD_DIM),
                         1.0, 0.0).astype(BF16)
        k_feat = jnp.where(
            jnp.logical_or(kcol == FEAT_BIAS + i, jnp.logical_or(kcol == FEAT_POS, kcol == FEAT_POS + 2)), 1.0,
            jnp.where(kcol == FEAT_POS + 1, slope * key_pos, jnp.where(kcol == FEAT_POS + 3, blk_off, 0.0)))
        ko_ref[0, h, :, :] = (_dot(k, pick) + k_feat).astype(BF16)


def _moba_prep(proj3, slopes):
    bsz, seq, _ = proj3.shape
    nblk = seq // MOBA_BLOCK
    mb = MOBA_BLOCK
    grid_spec = pltpu.PrefetchScalarGridSpec(
        num_scalar_prefetch=1,
        grid=(bsz, nblk),
        in_specs=[pl.BlockSpec((None, mb, A_WIDTH), lambda b, i, s: (b, i, COL_QA // A_WIDTH)),
                  pl.BlockSpec((None, mb, A_WIDTH), lambda b, i, s: (b, i, COL_KA // A_WIDTH)),
                  pl.BlockSpec((None, mb, A_WIDTH), lambda b, i, s: (b, i, COL_VA // A_WIDTH))],
        out_specs=[pl.BlockSpec((1, A_HEADS, mb, AUG), lambda b, i, s: (b, 0, i, 0)),
                   pl.BlockSpec((1, A_HEADS, AUG, mb), lambda b, i, s: (b, 0, 0, i)),
                   pl.BlockSpec((1, A_HEADS, V_ROWS, mb), lambda b, i, s: (b, 0, 0, i))],
        scratch_shapes=[pltpu.VMEM((nblk, A_WIDTH), F32)],
    )
    return pl.pallas_call(
        _moba_prep_kernel,
        grid_spec=grid_spec,
        out_shape=[jax.ShapeDtypeStruct((bsz, A_HEADS, seq, AUG), BF16),
                   jax.ShapeDtypeStruct((bsz, A_HEADS, AUG, seq), BF16),
                   jax.ShapeDtypeStruct((bsz, A_HEADS, V_ROWS, seq), BF16)],
        compiler_params=_cparams(("parallel", "arbitrary")),
        name="moba_prep",
    )(slopes, proj3, proj3, proj3)


def _moba_attn_kernel(q_ref, k_ref, v_ref, o_ref, s_a, s_b, *, group, n_groups):
    i = pl.program_id(2)
    mb = MOBA_BLOCK
    span = group * mb
    own = pl.multiple_of(i * mb, mb)
    key_i = lax.broadcasted_iota(jnp.int32, (mb, mb), 0)
    qry_i = lax.broadcasted_iota(jnp.int32, (mb, mb), 1)
    feat = lax.broadcasted_iota(jnp.int32, (AUG, mb), 0)
    is_bias = jnp.logical_and(feat >= FEAT_BIAS, feat < FEAT_POS)
    q_ts, carry0 = [], []
    for hh in range(2):
        q_t = q_ref[0, hh]
        q_ts.append(q_t)
        q_own = jnp.where(is_bias, jnp.zeros_like(q_t), q_t)
        s = _dot(k_ref[0, hh, pl.ds(own, mb), :], q_own)
        s = jnp.where(key_i <= qry_i, s, NEG_INF)
        m0 = jnp.max(s, axis=0, keepdims=True)
        p = jnp.exp(s - m0)
        carry0 += [m0, _dot(v_ref[0, hh, :, pl.ds(own, mb)], p.astype(BF16))]

    def scores(g, dst):
        start = pl.multiple_of(jnp.minimum(g, n_groups - 1) * span, span)
        for hh in range(2):
            dst[hh] = _dot(k_ref[0, hh, pl.ds(start, span), :], q_ts[hh])

    def absorb(g, src, carry):
        start = pl.multiple_of(g * span, span)
        new = []
        for hh in range(2):
            m, acc = carry[2 * hh], carry[2 * hh + 1]
            sb = src[hh]
            m_new = jnp.maximum(m, jnp.max(sb, axis=0, keepdims=True))
            pb = jnp.exp(sb - m_new)
            alpha = jnp.exp(m - m_new)
            acc = acc * alpha + _dot(v_ref[0, hh, :, pl.ds(start, span)], pb.astype(BF16))
            new += [m_new, acc]
        return tuple(new)

    def body(pair, carry):
        scores(2 * pair + 1, s_b)
        carry = absorb(2 * pair, s_a, carry)
        scores(2 * pair + 2, s_a)
        return absorb(2 * pair + 1, s_b, carry)

    scores(0, s_a)
    live_groups = (i + group - 1) // group
    res = lax.fori_loop(0, (live_groups + 1) // 2, body, tuple(carry0))
    outs = [res[2 * hh + 1][0:A_HEAD_DIM, :] / res[2 * hh + 1][A_HEAD_DIM:A_HEAD_DIM + 1, :] for hh in range(2)]
    o_t = jnp.concatenate(outs, axis=0).astype(BF16)
    eye = (key_i == qry_i).astype(BF16)
    o_ref[0] = _dot_nt(eye, o_t).astype(BF16)


def _moba_attn(k_aug, q_aug_t, v_aug_t):
    bsz, nh, seq, _ = k_aug.shape
    mb = MOBA_BLOCK
    group = min(2, seq // mb)
    n_groups = seq // (group * mb)
    return pl.pallas_call(
        functools.partial(_moba_attn_kernel, group=group, n_groups=n_groups),
        grid=(bsz, nh // 2, seq // mb),
        in_specs=[pl.BlockSpec((1, 2, AUG, mb), lambda b, h, i: (b, h, 0, i)),
                  pl.BlockSpec((1, 2, seq, AUG), lambda b, h, i: (b, h, 0, 0)),
                  pl.BlockSpec((1, 2, V_ROWS, seq), lambda b, h, i: (b, h, 0, 0))],
        out_specs=pl.BlockSpec((1, mb, 2 * A_HEAD_DIM), lambda b, h, i: (b, i, h)),
        out_shape=jax.ShapeDtypeStruct((bsz, seq, A_WIDTH), BF16),
        scratch_shapes=[pltpu.VMEM((2, group * mb, mb), F32), pltpu.VMEM((2, group * mb, mb), F32)],
        compiler_params=_cparams(("parallel", "parallel", "arbitrary")),
        name="moba_attn",
    )(q_aug_t, k_aug, v_aug_t)


def _retention_kernel(cdec_ref, q_ref, k_ref, v_ref, g_ref, decay_ref, qdec_ref, kdec_ref, o_ref, state_scr):
    @pl.when(pl.program_id(1) == 0)
    def _():
        state_scr[...] = jnp.zeros_like(state_scr)

    q = q_ref[...]
    k = k_ref[...]
    width = q.shape[1]
    eye = (lax.broadcasted_iota(jnp.int32, (width, width), 0)
           == lax.broadcasted_iota(jnp.int32, (width, width), 1)).astype(BF16)
    k_t = _dot_nt(eye, k)
    k_t_b = k_t.astype(BF16)
    state_b = state_scr[...].astype(BF16)
    for h in range(R_HEADS):
        rows = slice(h * R_QK_DIM, (h + 1) * R_QK_DIM)
        cols = slice(h * R_V_DIM, (h + 1) * R_V_DIM)
        q_h = q[:, rows]
        v_h = v_ref[:, cols]
        inner = _dot(q_h, k_t_b[rows, :]) * decay_ref[h]
        out = _dot(inner.astype(BF16), v_h) + _dot(q_h, state_b[rows, :]) * qdec_ref[h]
        k_dec = (k_t[rows, :] * kdec_ref[h]).astype(BF16)
        state_scr[rows, :] = cdec_ref[h] * state_scr[rows, :] + _dot(k_dec, v_h)
        mu = jnp.mean(out, axis=-1, keepdims=True)
        cen = out - mu
        var = jnp.mean(cen * cen, axis=-1, keepdims=True)
        y = cen * lax.rsqrt(var + GN_EPS)
        o_ref[:, cols] = (y * _silu(g_ref[:, cols].astype(F32))).astype(BF16)


def _retention_consts():
    h = np.arange(R_HEADS, dtype=np.float64)
    log_g = np.log(1.0 - np.exp2(-5.0 - h))
    n = np.arange(R_CHUNK, dtype=np.float64)
    diff = n[:, None] - n[None, :]
    scale = R_QK_DIM ** -0.5
    decay = np.where(diff >= 0, np.exp(np.maximum(diff, 0.0) * log_g[:, None, None]), 0.0) * scale
    q_decay = np.exp((n + 1.0) * log_g[:, None])[:, :, None]
    k_decay = np.exp((R_CHUNK - 1.0 - n) * log_g[:, None])[:, None, :] * scale
    chunk_decay = np.exp(R_CHUNK * log_g)
    return (jnp.asarray(decay, F32), jnp.asarray(q_decay, F32), jnp.asarray(k_decay, F32),
            jnp.asarray(chunk_decay, F32))


def _retention(proj3):
    bsz, seq, _ = proj3.shape
    c = R_CHUNK
    decay, qdec, kdec, cdec = _retention_consts()
    grid_spec = pltpu.PrefetchScalarGridSpec(
        num_scalar_prefetch=1,
        grid=(bsz, seq // c),
        in_specs=[pl.BlockSpec((None, c, R_QK_WIDTH), lambda b, i, s: (b, i, COL_QR // R_QK_WIDTH)),
                  pl.BlockSpec((None, c, R_QK_WIDTH), lambda b, i, s: (b, i, COL_KR // R_QK_WIDTH)),
                  pl.BlockSpec((None, c, R_V_WIDTH), lambda b, i, s: (b, i, COL_VR // R_V_WIDTH)),
                  pl.BlockSpec((None, c, R_V_WIDTH), lambda b, i, s: (b, i, COL_GR // R_V_WIDTH)),
                  pl.BlockSpec((R_HEADS, c, c), lambda b, i, s: (0, 0, 0)),
                  pl.BlockSpec((R_HEADS, c, 1), lambda b, i, s: (0, 0, 0)),
                  pl.BlockSpec((R_HEADS, 1, c), lambda b, i, s: (0, 0, 0))],
        out_specs=pl.BlockSpec((None, c, R_V_WIDTH), lambda b, i, s: (b, i, 0)),
        scratch_shapes=[pltpu.VMEM((R_QK_WIDTH, R_V_DIM), F32)],
    )
    return pl.pallas_call(
        _retention_kernel,
        grid_spec=grid_spec,
        out_shape=jax.ShapeDtypeStruct((bsz, seq, R_V_WIDTH), BF16),
        compiler_params=_cparams(("parallel", "arbitrary")),
        name="retention",
    )(cdec, proj3, proj3, proj3, proj3, decay, qdec, kdec)


def _pack_halves(x):
    w = x.shape[1] // 2
    bits = lax.bitcast_convert_type(x.astype(BF16).astype(F32), jnp.uint32)
    return (bits[:, :w] >> 16) | (bits[:, w:] & jnp.uint32(0xFFFF0000))


def _unpack_halves(p):
    lo = lax.bitcast_convert_type(p << 16, F32)
    hi = lax.bitcast_convert_type(p & jnp.uint32(0xFFFF0000), F32)
    return lo, hi


LANES = 128
ROW_SUBLANES = D_MODEL // LANES


def _store_tile_rows(ref, lead, x):
    m = x.shape[0]
    for j in range(ROW_SUBLANES):
        ref[lead + (pl.ds(j, m, stride=ROW_SUBLANES), slice(None))] = x[:, j * LANES:(j + 1) * LANES]


def _load_tile_rows(ref, lead, m):
    return [ref[lead + (pl.ds(j, m, stride=ROW_SUBLANES), slice(None))] for j in range(ROW_SUBLANES)]


def _mix_kernel(ya_ref, yr_ref, ga_ref, gt_ref, x_ref, wpa_ref, wpr_ref, wout_ref,
                gt1_ref, g_ref, sc_ref, sh_ref, x1_ref, h2_ref, h2p_ref):
    a = _dot(ya_ref[...], wpa_ref[...]) * _sigmoid(ga_ref[...].astype(F32))
    r = _dot(yr_ref[...], wpr_ref[...]) * _sigmoid(gt_ref[...].astype(F32))
    mix = _dot((a + r).astype(BF16), wout_ref[...])
    x1 = x_ref[...] + gt1_ref[0] * mix
    x1_ref[...] = x1
    ms = jnp.mean(x1 * x1, axis=-1, keepdims=True)
    y = x1 * lax.rsqrt(ms + NORM_EPS) * g_ref[...]
    h2 = y * (1.0 + sc_ref[0]) + sh_ref[0]
    h2_ref[...] = h2.astype(BF16)
    h2p_ref[...] = _pack_halves(h2)


def _mix(ya, yr, proj, x2d, wpa, wpr, wout, gt1, g, sc, sh, seq):
    t, d = x2d.shape
    tm = min(512, seq)
    per_b = seq // tm
    row = lambda i: (i, 0)
    full = lambda i: (0, 0)
    per_batch = lambda i: (i // per_b, 0, 0)
    return pl.pallas_call(
        _mix_kernel,
        grid=(t // tm,),
        in_specs=[pl.BlockSpec((tm, A_WIDTH), row),
                  pl.BlockSpec((tm, R_V_WIDTH), row),
                  pl.BlockSpec((tm, d), lambda i: (i, COL_GA // D_MODEL)),
                  pl.BlockSpec((tm, d), lambda i: (i, COL_GT // D_MODEL)),
                  pl.BlockSpec((tm, d), row),
                  pl.BlockSpec((A_WIDTH, d), full),
                  pl.BlockSpec((R_V_WIDTH, d), full),
                  pl.BlockSpec((d, d), full),
                  pl.BlockSpec((1, 1, d), per_batch),
                  pl.BlockSpec((1, d), full),
                  pl.BlockSpec((1, 1, d), per_batch),
                  pl.BlockSpec((1, 1, d), per_batch)],
        out_specs=[pl.BlockSpec((tm, d), row), pl.BlockSpec((tm, d), row),
                   pl.BlockSpec((tm, d // 2), row)],
        out_shape=[jax.ShapeDtypeStruct((t, d), F32), jax.ShapeDtypeStruct((t, d), BF16),
                   jax.ShapeDtypeStruct((t, d // 2), jnp.uint32)],
        compiler_params=_cparams(("parallel",)),
        name="merge_outproj_norm",
    )(ya, yr, proj, proj, x2d, wpa, wpr, wout, gt1, g, sc, sh)


def _router_kernel(h_ref, wr_ref, b_ref, e_ref, w_ref, r_ref, c_ref):
    logits = _dot_nt(wr_ref[...], h_ref[...])
    scores = _sigmoid(logits)
    choice = scores + b_ref[...]
    tm = logits.shape[1]
    giota = lax.broadcasted_iota(jnp.int32, (GROUP_SIZE, tm), 0)
    gs_rows = []
    for g in range(N_GROUPS):
        cg = choice[g * GROUP_SIZE:(g + 1) * GROUP_SIZE, :]
        m1 = jnp.max(cg, axis=0, keepdims=True)
        i1 = jnp.min(jnp.where(cg == m1, giota, GROUP_SIZE), axis=0, keepdims=True)
        m2 = jnp.max(jnp.where(giota == i1, KNOCKED_OUT, cg), axis=0, keepdims=True)
        gs_rows.append(m1 + m2)
    gs = jnp.concatenate(gs_rows, axis=0)
    grow = lax.broadcasted_iota(jnp.int32, (N_GROUPS, tm), 0)
    gmask = jnp.zeros((N_GROUPS, tm), jnp.bool_)
    for _ in range(TOPK_GROUPS):
        mx = jnp.max(gs, axis=0, keepdims=True)
        ix = jnp.min(jnp.where(gs == mx, grow, N_GROUPS), axis=0, keepdims=True)
        hit = grow == ix
        gmask = jnp.logical_or(gmask, hit)
        gs = jnp.where(hit, KNOCKED_OUT, gs)
    gmask_f = jnp.where(gmask, 1.0, 0.0)
    masked = jnp.concatenate(
        [jnp.where(gmask_f[g:g + 1, :] > 0.5, choice[g * GROUP_SIZE:(g + 1) * GROUP_SIZE, :], NEG_INF)
         for g in range(N_GROUPS)], axis=0)
    erow = lax.broadcasted_iota(jnp.int32, (N_EXPERTS, tm), 0)
    idx_rows, w_rows = [], []
    chosen = jnp.zeros((N_EXPERTS, tm), F32)
    for _ in range(TOP_K):
        mx = jnp.max(masked, axis=0, keepdims=True)
        ix = jnp.min(jnp.where(masked == mx, erow, N_EXPERTS), axis=0, keepdims=True)
        hit = erow == ix
        w_rows.append(jnp.sum(jnp.where(hit, scores, 0.0), axis=0, keepdims=True))
        idx_rows.append(ix)
        chosen = jnp.where(hit, 1.0, chosen)
        masked = jnp.where(hit, KNOCKED_OUT, masked)
    w = jnp.concatenate(w_rows, axis=0)
    w = w / (jnp.sum(w, axis=0, keepdims=True) + 1e-20) * ROUTED_SCALE
    e_ref[...] = jnp.concatenate(idx_rows, axis=0)
    w_ref[...] = w
    chosen_b = chosen.astype(BF16)
    earlier = (lax.broadcasted_iota(jnp.int32, (tm, tm), 0)
               < lax.broadcasted_iota(jnp.int32, (tm, tm), 1)).astype(BF16)
    before = _dot(chosen_b, earlier)
    ranks = [jnp.sum(jnp.where(erow == ix, before, 0.0), axis=0, keepdims=True) for ix in idx_rows]
    r_ref[...] = jnp.concatenate(ranks, axis=0).astype(jnp.int32)
    c_ref[...] = _dot(chosen_b, jnp.ones((tm, 128), BF16))


ROUTER_ROWS = 512


def _router(h2, wr_t, bias_col):
    t, d = h2.shape
    tm = ROUTER_ROWS
    by_tile = lambda i: (0, i)
    return pl.pallas_call(
        _router_kernel,
        grid=(t // tm,),
        in_specs=[pl.BlockSpec((tm, d), lambda i: (i, 0)),
                  pl.BlockSpec((N_EXPERTS, d), lambda i: (0, 0)),
                  pl.BlockSpec((N_EXPERTS, 1), lambda i: (0, 0))],
        out_specs=[pl.BlockSpec((TOP_K, tm), by_tile), pl.BlockSpec((TOP_K, tm), by_tile),
                   pl.BlockSpec((TOP_K, tm), by_tile), pl.BlockSpec((N_EXPERTS, 128), by_tile)],
        out_shape=[jax.ShapeDtypeStruct((TOP_K, t), jnp.int32),
                   jax.ShapeDtypeStruct((TOP_K, t), F32),
                   jax.ShapeDtypeStruct((TOP_K, t), jnp.int32),
                   jax.ShapeDtypeStruct((N_EXPERTS, (t // tm) * 128), F32)],
        compiler_params=_cparams(("parallel",)),
        name="router_topk",
    )(h2, wr_t, bias_col)


def _pos_kernel(e_ref, r_ref, base_ref, p_ref):
    tm = e_ref.shape[1]
    erow = lax.broadcasted_iota(jnp.int32, (N_EXPERTS, tm), 0)
    base = base_ref[0]
    rows = [jnp.sum(jnp.where(erow == e_ref[k:k + 1, :], base, 0.0), axis=0, keepdims=True)
            for k in range(TOP_K)]
    p_ref[0] = jnp.concatenate(rows, axis=0).astype(jnp.int32) + r_ref[...]


MOVE_ROWS = 256


def _positions(eidx_t, rank_t, tile_base):
    t = eidx_t.shape[1]
    tm = min(MOVE_ROWS, t)
    per_router_tile = ROUTER_ROWS // tm
    return pl.pallas_call(
        _pos_kernel,
        grid=(t // tm,),
        in_specs=[pl.BlockSpec((TOP_K, tm), lambda i: (0, i)),
                  pl.BlockSpec((TOP_K, tm), lambda i: (0, i)),
                  pl.BlockSpec((1, N_EXPERTS, 1), lambda i: (i // per_router_tile, 0, 0))],
        out_specs=pl.BlockSpec((1, TOP_K, tm), lambda i: (i, 0, 0)),
        out_shape=jax.ShapeDtypeStruct((t // tm, TOP_K, tm), jnp.int32),
        compiler_params=_cparams(("parallel",)),
        name="slot_positions",
    )(eidx_t, rank_t, tile_base)


SLOT_ROWS = 512
PAD_CHUNKS = (256, 128, 64, 32, 16, 8)


def _zero_pads_kernel(pad_start_ref, pad_len_ref, xs_in, xs_hbm, zero_buf, pad_sem):
    del xs_in
    zero_buf[...] = jnp.zeros_like(zero_buf)

    def pad_copies(e, wait):
        start = pad_start_ref[e]
        n = pad_len_ref[e]
        head = jnp.minimum((-start) & (ROW_SUBLANES - 1), n)

        def fill(first, size, pred):
            @pl.when(pred)
            def _():
                cp = pltpu.make_async_copy(zero_buf.at[pl.ds(0, size), :], xs_hbm.at[pl.ds(first, size), :], pad_sem)
                if wait:
                    cp.wait()
                else:
                    cp.start()

        for j in range(ROW_SUBLANES - 1):
            fill(start + j, 1, j < head)
        ptr = start + head
        rest = n - head
        for chunk in PAD_CHUNKS:
            fill(pl.multiple_of(ptr, ROW_SUBLANES), chunk, (rest & chunk) != 0)
            ptr = ptr + (rest & chunk)

    def issue(e, carry):
        pad_copies(e, False)
        return carry

    def drain(e, carry):
        pad_copies(e, True)
        return carry

    lax.fori_loop(0, N_EXPERTS, issue, 0)
    lax.fori_loop(0, N_EXPERTS, drain, 0)


def _zero_pads(pad_start, pad_len, xs):
    grid_spec = pltpu.PrefetchScalarGridSpec(
        num_scalar_prefetch=2,
        grid=(1,),
        in_specs=[pl.BlockSpec(memory_space=pl.ANY)],
        out_specs=pl.BlockSpec(memory_space=pl.ANY),
        scratch_shapes=[pltpu.VMEM((PAD_CHUNKS[0], xs.shape[1]), xs.dtype),
                        pltpu.SemaphoreType.DMA],
    )
    return pl.pallas_call(
        _zero_pads_kernel,
        grid_spec=grid_spec,
        out_shape=jax.ShapeDtypeStruct(xs.shape, xs.dtype),
        input_output_aliases={2: 0},
        compiler_params=_cparams(("arbitrary",)),
        name="zero_pad_slots",
    )(pad_start, pad_len, xs)


SC_DISPATCH_TOKENS = 64


def _sc_dispatch(pos_blocks, h_rows, n_rows):
    info = plsc.get_sparse_core_info()
    n_cores = info.num_cores
    n_workers = n_cores * info.num_subcores
    t = h_rows.shape[0]
    chunk = SC_DISPATCH_TOKENS
    steps = t // (n_workers * chunk)
    mesh = plsc.VectorSubcoreMesh(core_axis_name="c", subcore_axis_name="s")

    @functools.partial(
        pl.kernel, mesh=mesh,
        out_type=jax.ShapeDtypeStruct((n_rows,) + h_rows.shape[1:], h_rows.dtype),
        scratch_types=[pltpu.VMEM((TOP_K, chunk), jnp.int32),
                       pltpu.VMEM((chunk,) + h_rows.shape[1:], h_rows.dtype),
                       pltpu.SemaphoreType.DMA],
        name="sc_dispatch_rows",
    )
    def scatter_rows(pos_hbm, h_hbm, out_hbm, idx_v, rows_v, sem):
        wid = lax.axis_index("s") * n_cores + lax.axis_index("c")

        @pl.loop(0, steps)
        def _(step):
            blk = wid * steps + step
            pltpu.sync_copy(pos_hbm.at[blk], idx_v)
            pltpu.sync_copy(h_hbm.at[pl.ds(blk * chunk, chunk)], rows_v)
            scatters = [pltpu.make_async_copy(rows_v, out_hbm.at[idx_v.at[k]], sem) for k in range(TOP_K)]
            for cp in scatters:
                cp.start()
            for cp in scatters:
                cp.wait()

    return scatter_rows(pos_blocks, h_rows)


def _experts_kernel(blk_e_ref, nblk_ref, x_ref, w1_ref, w3_ref, w2_ref, y_ref, w1b, w3b, w2b):
    s = pl.program_id(0)

    @pl.when(s < nblk_ref[0])
    def _():
        @pl.when(jnp.logical_or(s == 0, blk_e_ref[s] != blk_e_ref[jnp.maximum(s - 1, 0)]))
        def _():
            w1b[...] = w1_ref[0].astype(BF16)
            w3b[...] = w3_ref[0].astype(BF16)
            w2b[...] = w2_ref[0].astype(BF16)

        half = x_ref.shape[1]
        lo, hi = _unpack_halves(x_ref[...])
        lo = lo.astype(BF16)
        hi = hi.astype(BF16)
        h1 = _dot(lo, w1b[:half, :]) + _dot(hi, w1b[half:, :])
        h3 = _dot(lo, w3b[:half, :]) + _dot(hi, w3b[half:, :])
        mid = (_silu(h1) * h3).astype(BF16)
        y_ref[...] = _pack_halves(_dot(mid, w2b[...]))


def _sc_gather(pos_blocks, ys, t):
    info = plsc.get_sparse_core_info()
    n_cores = info.num_cores
    n_workers = n_cores * info.num_subcores
    chunk = SC_DISPATCH_TOKENS
    steps = t // (n_workers * chunk)
    mesh = plsc.VectorSubcoreMesh(core_axis_name="c", subcore_axis_name="s")

    @functools.partial(
        pl.kernel, mesh=mesh,
        out_type=jax.ShapeDtypeStruct((TOP_K * t,) + ys.shape[1:], ys.dtype),
        scratch_types=[pltpu.VMEM((TOP_K, chunk), jnp.int32),
                       pltpu.VMEM((chunk,) + ys.shape[1:], ys.dtype),
                       pltpu.VMEM((chunk,) + ys.shape[1:], ys.dtype),
                       pltpu.SemaphoreType.DMA((2,))],
        name="sc_gather_rows",
    )
    def gather_rows(pos_hbm, ys_hbm, out_hbm, idx_v, rows_a, rows_b, sems):
        wid = lax.axis_index("s") * n_cores + lax.axis_index("c")
        bufs = (rows_a, rows_b)

        @pl.loop(0, steps)
        def _(step):
            blk = wid * steps + step
            pltpu.sync_copy(pos_hbm.at[blk], idx_v)
            gathers = [pltpu.make_async_copy(ys_hbm.at[idx_v.at[k]], bufs[k % 2], sems.at[k % 2])
                       for k in range(TOP_K)]
            gathers[0].start()
            for k in range(TOP_K):
                gathers[k].wait()
                if k + 1 < TOP_K:
                    gathers[k + 1].start()
                pltpu.sync_copy(bufs[k % 2], out_hbm.at[pl.ds(k * t + blk * chunk, chunk)])

    return gather_rows(pos_blocks, ys)


def _experts(blk_e, nblk_used, xs, w1, w3, w2):
    n_rows, half = xs.shape
    d = D_MODEL
    blk = lambda s, be, nb: (jnp.minimum(s, nb[0] - 1), 0)
    wblk = lambda s, be, nb: (be[jnp.minimum(s, nb[0] - 1)], 0, 0)
    grid_spec = pltpu.PrefetchScalarGridSpec(
        num_scalar_prefetch=2,
        grid=(n_rows // SLOT_ROWS,),
        in_specs=[pl.BlockSpec((SLOT_ROWS, half), blk),
                  pl.BlockSpec((1, d, EXPERT_FF), wblk),
                  pl.BlockSpec((1, d, EXPERT_FF), wblk),
                  pl.BlockSpec((1, EXPERT_FF, d), wblk)],
        out_specs=pl.BlockSpec((SLOT_ROWS, half), blk),
        scratch_shapes=[pltpu.VMEM((d, EXPERT_FF), BF16),
                        pltpu.VMEM((d, EXPERT_FF), BF16),
                        pltpu.VMEM((EXPERT_FF, d), BF16)],
    )
    return pl.pallas_call(
        _experts_kernel,
        grid_spec=grid_spec,
        out_shape=jax.ShapeDtypeStruct((n_rows, half), jnp.uint32),
        compiler_params=_cparams(("arbitrary",)),
        name="routed_experts",
    )(blk_e, nblk_used, xs, w1, w3, w2)


def _combine_kernel(*refs):
    y_refs = refs[:TOP_K]
    w_ref, h_ref, x1_ref, ws1_ref, ws3_ref, ws2_ref, gt2_ref, g_ref = refs[TOP_K:TOP_K + 8]
    o_ref = refs[-1]
    tm, d = x1_ref.shape
    half = d // 2
    h = h_ref[...]
    mid = (_silu(_dot(h, ws1_ref[...])) * _dot(h, ws3_ref[...])).astype(BF16)
    shared = _dot(mid, ws2_ref[...])
    w = w_ref[...]
    acc_lo = jnp.zeros((tm, half), F32)
    acc_hi = jnp.zeros((tm, half), F32)
    for k in range(TOP_K):
        lo, hi = _unpack_halves(y_refs[k][...])
        acc_lo = acc_lo + lo * w[:, k:k + 1]
        acc_hi = acc_hi + hi * w[:, k:k + 1]
    routed = jnp.concatenate([acc_lo, acc_hi], axis=1)
    x2 = x1_ref[...] + gt2_ref[0] * (routed + shared)
    ms = jnp.mean(x2 * x2, axis=-1, keepdims=True)
    o_ref[...] = x2 * lax.rsqrt(ms + NORM_EPS) * g_ref[...]


COMBINE_ROWS = 256
COMBINE_PARTS = 4


def _combine(y_kt, wts, h2, x1, ws1, ws3, ws2, gt2, g_final, seq, part, n_parts, prev_out):
    t, d = x1.shape
    tm = min(COMBINE_ROWS, seq)
    per_b = seq // tm
    tiles = t // tm // n_parts
    first = part * tiles
    row = lambda i: (first + i, 0)
    full = lambda i: (0, 0)
    y_specs = [pl.BlockSpec((tm, d // 2), functools.partial(lambda i, k: (k * tiles + i, 0), k=k))
               for k in range(TOP_K)]
    in_specs = y_specs + [
        pl.BlockSpec((tm, TOP_K), row),
        pl.BlockSpec((tm, d), row),
        pl.BlockSpec((tm, d), row),
        pl.BlockSpec((d, SHARED_FF), full),
        pl.BlockSpec((d, SHARED_FF), full),
        pl.BlockSpec((SHARED_FF, d), full),
        pl.BlockSpec((1, 1, d), lambda i: ((first + i) // per_b, 0, 0)),
        pl.BlockSpec((1, d), full)]
    args = [y_kt] * TOP_K + [wts, h2, x1, ws1, ws3, ws2, gt2, g_final]
    aliases = {}
    if prev_out is not None:
        in_specs.append(pl.BlockSpec(memory_space=pl.ANY))
        aliases = {len(args): 0}
        args.append(prev_out)
    return pl.pallas_call(
        _combine_kernel,
        grid=(tiles,),
        in_specs=in_specs,
        out_specs=pl.BlockSpec((tm, d), row),
        out_shape=jax.ShapeDtypeStruct((t, d), F32),
        input_output_aliases=aliases,
        compiler_params=_cparams(("parallel",)),
        name="combine_shared_final",
    )(*args)


def _slot_tables(cnt, t):
    ntiles = cnt.shape[1] // 128
    cnt_tile = cnt.reshape(N_EXPERTS, ntiles, 128)[:, :, 0].astype(jnp.int32)
    counts = jnp.sum(cnt_tile, axis=1)
    padded = (counts + SLOT_ROWS - 1) // SLOT_ROWS * SLOT_ROWS
    pstart = jnp.cumsum(padded) - padded
    tile_base = pstart[:, None] + jnp.cumsum(cnt_tile, axis=1) - cnt_tile
    n_blk = -(-(t * TOP_K) // SLOT_ROWS) + N_EXPERTS
    blk_end = jnp.cumsum(padded // SLOT_ROWS)
    blk_e = jnp.sum((blk_end[None, :] <= jnp.arange(n_blk)[:, None]).astype(jnp.int32), axis=1)
    blk_e = jnp.minimum(blk_e, N_EXPERTS - 1)
    return (blk_e, blk_end[-1:].astype(jnp.int32), pstart + counts, padded - counts,
            tile_base.T.astype(F32).reshape(ntiles, N_EXPERTS, 1), n_blk * SLOT_ROWS)


def _permute_in_cols(w_in):
    qa, ka, va, qr, kr, vr, gr, ga, gt = jnp.split(
        w_in, np.cumsum((A_WIDTH, A_WIDTH, A_WIDTH, R_QK_WIDTH, R_QK_WIDTH, R_V_WIDTH, R_V_WIDTH,
                         D_MODEL))[:].tolist(), axis=1)
    return jnp.concatenate([vr, gr, ga, gt, qa, ka, va, qr, kr], axis=1)


def kernel(x, c, w_ada, b_ada, g_mix, w_in, w_pa, w_pr, w_out, g_ffn, w_router, router_bias,
           w1, w3, w2, ws1, ws3, ws2, g_final):
    bsz, seq, d = x.shape
    t = bsz * seq
    depth = w_ada.shape[0]
    assert depth == 1, "the final norm is fused into the single layer's last kernel"
    slopes = jnp.exp2(-8.0 / A_HEADS * jnp.arange(1, A_HEADS + 1, dtype=F32))
    x2d = x.reshape(t, d)
    for l in range(depth):
        mod = _ada(c, w_ada[l], b_ada[l])
        sh1, sc1, gt1, sh2, sc2, gt2 = [m.reshape(bsz, 1, d) for m in jnp.split(mod, 6, axis=-1)]
        w_in_p = _permute_in_cols(w_in[l]).astype(BF16)
        proj = _inproj(x2d, g_mix[l].reshape(1, d), sc1, sh1, w_in_p, seq)
        proj3 = proj.reshape(bsz, seq, IN_COLS)
        k_aug, q_aug_t, v_aug_t = _moba_prep(proj3, slopes)
        ya = _moba_attn(k_aug, q_aug_t, v_aug_t).reshape(t, A_WIDTH)
        yr = _retention(proj3).reshape(t, R_V_WIDTH)
        x1, h2, h2p = _mix(ya, yr, proj, x2d, w_pa[l].astype(BF16), w_pr[l].astype(BF16),
                           w_out[l].astype(BF16), gt1, g_ffn[l].reshape(1, d), sc2, sh2, seq)
        eidx_t, wts_t, rank_t, cnt = _router(h2, w_router[l].T.astype(BF16),
                                             router_bias[l].reshape(N_EXPERTS, 1))
        blk_e, nblk_used, pad_start, pad_len, tile_base, n_rows = _slot_tables(cnt, t)
        pos3 = _positions(eidx_t, rank_t, tile_base)
        pos_blocks = jnp.transpose(
            pos3.reshape(pos3.shape[0], TOP_K, -1, SC_DISPATCH_TOKENS), (0, 2, 1, 3)
        ).reshape(t // SC_DISPATCH_TOKENS, TOP_K, SC_DISPATCH_TOKENS)
        xs = _zero_pads(pad_start, pad_len, _sc_dispatch(pos_blocks, h2p, n_rows))
        ys = _experts(blk_e, nblk_used, xs, w1[l], w3[l], w2[l])
        wts = wts_t.T
        shared_w = (ws1[l].astype(BF16), ws3[l].astype(BF16), ws2[l].astype(BF16))
        blocks_per_part = pos_blocks.shape[0] // COMBINE_PARTS
        x2d = None
        for part in range(COMBINE_PARTS):
            y_kt = _sc_gather(pos_blocks[part * blocks_per_part:(part + 1) * blocks_per_part], ys,
                              t // COMBINE_PARTS)
            x2d = _combine(y_kt, wts, h2, x1, *shared_w, gt2, g_final.reshape(1, d), seq,
                           part, COMBINE_PARTS, x2d)
    return x2d.reshape(bsz, seq, d)
```

```python
import functools

import jax
import jax.numpy as jnp
import numpy as np
from jax import lax
from jax.experimental import pallas as pl
from jax.experimental.pallas import tpu as pltpu
from jax.experimental.pallas import tpu_sc as plsc

F32 = jnp.float32
BF16 = jnp.bfloat16

D_MODEL = 1024
A_HEADS = 8
A_HEAD_DIM = 64
A_WIDTH = A_HEADS * A_HEAD_DIM
MOBA_BLOCK = 256
MOBA_TOPK = 3
R_HEADS = 8
R_QK_DIM = 64
R_V_DIM = 128
R_QK_WIDTH = R_HEADS * R_QK_DIM
R_V_WIDTH = R_HEADS * R_V_DIM
R_CHUNK = 128
N_EXPERTS = 256
TOP_K = 8
N_GROUPS = 8
GROUP_SIZE = N_EXPERTS // N_GROUPS
TOPK_GROUPS = 4
EXPERT_FF = 256
SHARED_FF = 256
ROUTED_SCALE = 2.5
NORM_EPS = 1e-6
GN_EPS = 1e-6
NEG_INF = -1e30
KNOCKED_OUT = -3e38

COL_VR, COL_GR, COL_GA, COL_GT = 0, 1024, 2048, 3072
COL_QA, COL_KA, COL_VA, COL_QR, COL_KR = 4096, 4608, 5120, 5632, 6144
IN_COLS = 6656
AUG = 128
FEAT_BIAS = A_HEAD_DIM
FEAT_POS = A_HEAD_DIM + 32
V_ROWS = A_HEAD_DIM + 16
MOBA_HEADS_PER_STEP = 4

VMEM_LIMIT = 56 * 1024 * 1024


def _cparams(sem, vmem=VMEM_LIMIT):
    return pltpu.CompilerParams(dimension_semantics=sem, vmem_limit_bytes=vmem)


def _dot(a, b):
    return jnp.dot(a, b, preferred_element_type=F32)


def _dot_nt(a, b):
    return lax.dot_general(a, b, (((1,), (1,)), ((), ())), preferred_element_type=F32)


def _sigmoid(x):
    return 1.0 / (1.0 + jnp.exp(-x))


def _silu(x):
    return x * _sigmoid(x)


def _ada_kernel(c_ref, w_ref, b_ref, o_ref):
    c = c_ref[...]
    s = _silu(c)
    s_hi = s.astype(BF16)
    s_lo = (s - s_hi.astype(F32)).astype(BF16)
    w = w_ref[...]
    w_hi = w.astype(BF16)
    w_lo = (w - w_hi.astype(F32)).astype(BF16)
    o_ref[...] = _dot(s_hi, w_hi) + _dot(s_hi, w_lo) + _dot(s_lo, w_hi) + b_ref[...]


def _ada(c, w_ada, b_ada):
    bsz, d = c.shape
    n = w_ada.shape[1]
    tn = 1024
    return pl.pallas_call(
        _ada_kernel,
        grid=(n // tn,),
        in_specs=[pl.BlockSpec((bsz, d), lambda j: (0, 0)),
                  pl.BlockSpec((d, tn), lambda j: (0, j)),
                  pl.BlockSpec((1, tn), lambda j: (0, j))],
        out_specs=pl.BlockSpec((bsz, tn), lambda j: (0, j)),
        out_shape=jax.ShapeDtypeStruct((bsz, n), F32),
        compiler_params=_cparams(("parallel",)),
        name="ada_mod",
    )(c, w_ada, b_ada.reshape(1, n))


INPROJ_COLS = 512


def _inproj_kernel(x_ref, g_ref, sc_ref, sh_ref, w_ref, o_ref):
    x = x_ref[...]
    ms = jnp.mean(x * x, axis=-1, keepdims=True)
    y = x * lax.rsqrt(ms + NORM_EPS) * g_ref[...]
    h = (y * (1.0 + sc_ref[0]) + sh_ref[0]).astype(BF16)
    for j in range(w_ref.shape[1] // INPROJ_COLS):
        cols = slice(j * INPROJ_COLS, (j + 1) * INPROJ_COLS)
        o_ref[:, cols] = _dot(h, w_ref[:, cols]).astype(BF16)


def _inproj(x2d, g, sc, sh, w_bf16, seq):
    t, d = x2d.shape
    n = w_bf16.shape[1]
    tm = min(512, seq)
    per_b = seq // tm
    return pl.pallas_call(
        _inproj_kernel,
        grid=(t // tm,),
        in_specs=[pl.BlockSpec((tm, d), lambda i: (i, 0)),
                  pl.BlockSpec((1, d), lambda i: (0, 0)),
                  pl.BlockSpec((1, 1, d), lambda i: (i // per_b, 0, 0)),
                  pl.BlockSpec((1, 1, d), lambda i: (i // per_b, 0, 0)),
                  pl.BlockSpec((d, n), lambda i: (0, 0))],
        out_specs=pl.BlockSpec((tm, n), lambda i: (i, 0)),
        out_shape=jax.ShapeDtypeStruct((t, n), BF16),
        compiler_params=_cparams(("parallel",)),
        name="norm_inproj",
    )(x2d, g, sc, sh, w_bf16)


def _moba_prep_kernel(slopes_ref, q_ref, k_ref, v_ref, ko_ref, qo_ref, vo_ref, kmean_scr):
    i = pl.program_id(1)
    nblk = kmean_scr.shape[0]
    width = q_ref.shape[1]

    @pl.when(i == 0)
    def _():
        kmean_scr[...] = jnp.zeros_like(kmean_scr)

    q = q_ref[...]
    k = k_ref[...]
    v = v_ref[...]
    kmean_scr[pl.ds(i, 1), :] = jnp.mean(k.astype(F32), axis=0, keepdims=True)

    eye = (lax.broadcasted_iota(jnp.int32, (width, width), 0)
           == lax.broadcasted_iota(jnp.int32, (width, width), 1)).astype(BF16)
    q_t = _dot_nt(eye, q)
    v_t = _dot_nt(eye, v)

    km = kmean_scr[...]
    km_rep = jnp.concatenate([km] * A_HEADS, axis=0)
    r_head = lax.broadcasted_iota(jnp.int32, km_rep.shape, 0) // nblk
    c_head = lax.broadcasted_iota(jnp.int32, km_rep.shape, 1) // A_HEAD_DIM
    km_bd = jnp.where(r_head == c_head, km_rep, 0.0)
    km_hi = km_bd.astype(BF16)
    km_lo = (km_bd - km_hi.astype(F32)).astype(BF16)
    q_t_b = q_t.astype(BF16)
    gate_all = _dot(km_hi, q_t_b) + _dot(km_lo, q_t_b)

    mb = q.shape[0]
    blk = lax.broadcasted_iota(jnp.int32, (nblk, mb), 0)
    lane_pos = lax.broadcasted_iota(jnp.int32, (16, mb), 1).astype(F32)
    row16 = lax.broadcasted_iota(jnp.int32, (16, mb), 0)
    key_pos = lax.broadcasted_iota(jnp.int32, (mb, AUG), 0).astype(F32)
    kcol = lax.broadcasted_iota(jnp.int32, (mb, AUG), 1)
    sel_r = lax.broadcasted_iota(jnp.int32, (width, AUG), 0)
    sel_c = lax.broadcasted_iota(jnp.int32, (width, AUG), 1)

    for h in range(A_HEADS):
        slope = slopes_ref[h]
        g = jnp.where(blk < i, gate_all[h * nblk:(h + 1) * nblk, :], NEG_INF)
        sel = jnp.zeros((nblk, mb), jnp.bool_)
        for r in range(MOBA_TOPK):
            m = jnp.max(g, axis=0, keepdims=True)
            idx = jnp.min(jnp.where(g == m, blk, nblk), axis=0, keepdims=True)
            hit = blk == idx
            sel = jnp.logical_or(sel, jnp.logical_and(hit, r < i))
            g = jnp.where(hit, KNOCKED_OUT, g)
        bias_t = jnp.where(sel, 0.0, NEG_INF)

        scale = A_HEAD_DIM ** -0.5
        qo_ref[0, h, 0:A_HEAD_DIM, :] = (q_t[h * A_HEAD_DIM:(h + 1) * A_HEAD_DIM, :] * scale).astype(BF16)
        qo_ref[0, h, FEAT_BIAS:FEAT_BIAS + nblk, :] = bias_t.astype(BF16)
        if nblk < 32:
            qo_ref[0, h, FEAT_BIAS + nblk:FEAT_POS, :] = jnp.zeros((32 - nblk, mb), BF16)
        blk_off = slope * (i * mb).astype(F32)
        pos_feat = jnp.where(row16 == 0, -slope * lane_pos,
                             jnp.where(row16 == 2, -blk_off,
                                       jnp.where(jnp.logical_or(row16 == 1, row16 == 3), 1.0, 0.0)))
        qo_ref[0, h, FEAT_POS:FEAT_POS + 16, :] = pos_feat.astype(BF16)
        qo_ref[0, h, FEAT_POS + 16:AUG, :] = jnp.zeros((AUG - FEAT_POS - 16, mb), BF16)

        vo_ref[0, h, 0:A_HEAD_DIM, :] = v_t[h * A_HEAD_DIM:(h + 1) * A_HEAD_DIM, :].astype(BF16)
        vo_ref[0, h, A_HEAD_DIM:V_ROWS, :] = jnp.where(row16 == 0, 1.0, 0.0).astype(BF16)

        pick = jnp.where(jnp.logical_and(sel_r == sel_c + h * A_HEAD_DIM, sel_c < A_HEAD_DIM),
                         1.0, 0.0).astype(BF16)
        k_feat = jnp.where(
            jnp.logical_or(kcol == FEAT_BIAS + i, jnp.logical_or(kcol == FEAT_POS, kcol == FEAT_POS + 2)), 1.0,
            jnp.where(kcol == FEAT_POS + 1, slope * key_pos, jnp.where(kcol == FEAT_POS + 3, blk_off, 0.0)))
        ko_ref[0, h, :, :] = (_dot(k, pick) + k_feat).astype(BF16)


def _moba_prep(proj3, slopes):
    bsz, seq, _ = proj3.shape
    nblk = seq // MOBA_BLOCK
    mb = MOBA_BLOCK
    grid_spec = pltpu.PrefetchScalarGridSpec(
        num_scalar_prefetch=1,
        grid=(bsz, nblk),
        in_specs=[pl.BlockSpec((None, mb, A_WIDTH), lambda b, i, s: (b, i, COL_QA // A_WIDTH)),
                  pl.BlockSpec((None, mb, A_WIDTH), lambda b, i, s: (b, i, COL_KA // A_WIDTH)),
                  pl.BlockSpec((None, mb, A_WIDTH), lambda b, i, s: (b, i, COL_VA // A_WIDTH))],
        out_specs=[pl.BlockSpec((1, A_HEADS, mb, AUG), lambda b, i, s: (b, 0, i, 0)),
                   pl.BlockSpec((1, A_HEADS, AUG, mb), lambda b, i, s: (b, 0, 0, i)),
                   pl.BlockSpec((1, A_HEADS, V_ROWS, mb), lambda b, i, s: (b, 0, 0, i))],
        scratch_shapes=[pltpu.VMEM((nblk, A_WIDTH), F32)],
    )
    return pl.pallas_call(
        _moba_prep_kernel,
        grid_spec=grid_spec,
        out_shape=[jax.ShapeDtypeStruct((bsz, A_HEADS, seq, AUG), BF16),
                   jax.ShapeDtypeStruct((bsz, A_HEADS, AUG, seq), BF16),
                   jax.ShapeDtypeStruct((bsz, A_HEADS, V_ROWS, seq), BF16)],
        compiler_params=_cparams(("parallel", "arbitrary")),
        name="moba_prep",
    )(slopes, proj3, proj3, proj3)


def _moba_attn_kernel(q_ref, k_ref, v_ref, o_ref, s_a, s_b, *, group, n_groups):
    i = pl.program_id(2)
    mb = MOBA_BLOCK
    span = group * mb
    own = pl.multiple_of(i * mb, mb)
    key_i = lax.broadcasted_iota(jnp.int32, (mb, mb), 0)
    qry_i = lax.broadcasted_iota(jnp.int32, (mb, mb), 1)
    feat = lax.broadcasted_iota(jnp.int32, (AUG, mb), 0)
    is_bias = jnp.logical_and(feat >= FEAT_BIAS, feat < FEAT_POS)
    q_ts, carry0 = [], []
    for hh in range(MOBA_HEADS_PER_STEP):
        q_t = q_ref[0, hh]
        q_ts.append(q_t)
        q_own = jnp.where(is_bias, jnp.zeros_like(q_t), q_t)
        s = _dot(k_ref[0, hh, pl.ds(own, mb), :], q_own)
        s = jnp.where(key_i <= qry_i, s, NEG_INF)
        m0 = jnp.max(s, axis=0, keepdims=True)
        p = jnp.exp(s - m0)
        carry0 += [m0, _dot(v_ref[0, hh, :, pl.ds(own, mb)], p.astype(BF16))]

    def scores(g, dst):
        start = pl.multiple_of(jnp.minimum(g, n_groups - 1) * span, span)
        for hh in range(MOBA_HEADS_PER_STEP):
            dst[hh] = _dot(k_ref[0, hh, pl.ds(start, span), :], q_ts[hh])

    def absorb(g, src, carry):
        start = pl.multiple_of(g * span, span)
        new = []
        for hh in range(MOBA_HEADS_PER_STEP):
            m, acc = carry[2 * hh], carry[2 * hh + 1]
            sb = src[hh]
            m_new = jnp.maximum(m, jnp.max(sb, axis=0, keepdims=True))
            pb = jnp.exp(sb - m_new)
            alpha = jnp.exp(m - m_new)
            acc = acc * alpha + _dot(v_ref[0, hh, :, pl.ds(start, span)], pb.astype(BF16))
            new += [m_new, acc]
        return tuple(new)

    def body(pair, carry):
        scores(2 * pair + 1, s_b)
        carry = absorb(2 * pair, s_a, carry)
        scores(2 * pair + 2, s_a)
        return absorb(2 * pair + 1, s_b, carry)

    scores(0, s_a)
    live_groups = (i + group - 1) // group
    res = lax.fori_loop(0, (live_groups + 1) // 2, body, tuple(carry0))
    outs = [res[2 * hh + 1][0:A_HEAD_DIM, :] / res[2 * hh + 1][A_HEAD_DIM:A_HEAD_DIM + 1, :] for hh in range(MOBA_HEADS_PER_STEP)]
    o_t = jnp.concatenate(outs, axis=0).astype(BF16)
    eye = (key_i == qry_i).astype(BF16)
    o_ref[0] = _dot_nt(eye, o_t).astype(BF16)


def _moba_attn(k_aug, q_aug_t, v_aug_t):
    bsz, nh, seq, _ = k_aug.shape
    mb = MOBA_BLOCK
    group = min(2, seq // mb)
    n_groups = seq // (group * mb)
    hps = MOBA_HEADS_PER_STEP
    return pl.pallas_call(
        functools.partial(_moba_attn_kernel, group=group, n_groups=n_groups),
        grid=(bsz, nh // hps, seq // mb),
        in_specs=[pl.BlockSpec((1, hps, AUG, mb), lambda b, h, i: (b, h, 0, i)),
                  pl.BlockSpec((1, hps, seq, AUG), lambda b, h, i: (b, h, 0, 0)),
                  pl.BlockSpec((1, hps, V_ROWS, seq), lambda b, h, i: (b, h, 0, 0))],
        out_specs=pl.BlockSpec((1, mb, hps * A_HEAD_DIM), lambda b, h, i: (b, i, h)),
        out_shape=jax.ShapeDtypeStruct((bsz, seq, A_WIDTH), BF16),
        scratch_shapes=[pltpu.VMEM((hps, group * mb, mb), F32), pltpu.VMEM((hps, group * mb, mb), F32)],
        compiler_params=_cparams(("parallel", "parallel", "arbitrary")),
        name="moba_attn",
    )(q_aug_t, k_aug, v_aug_t)


def _retention_kernel(cdec_ref, q_ref, k_ref, v_ref, g_ref, decay_ref, qdec_ref, kdec_ref, o_ref, state_scr):
    @pl.when(pl.program_id(0) == 0)
    def _():
        state_scr[...] = jnp.zeros_like(state_scr)

    width = q_ref.shape[2]
    eye = (lax.broadcasted_iota(jnp.int32, (width, width), 0)
           == lax.broadcasted_iota(jnp.int32, (width, width), 1)).astype(BF16)
    for b in range(q_ref.shape[0]):
        q = q_ref[b]
        k_t = _dot_nt(eye, k_ref[b])
        k_t_b = k_t.astype(BF16)
        state_b = state_scr[b].astype(BF16)
        for h in range(R_HEADS):
            rows = slice(h * R_QK_DIM, (h + 1) * R_QK_DIM)
            cols = slice(h * R_V_DIM, (h + 1) * R_V_DIM)
            q_h = q[:, rows]
            v_h = v_ref[b, :, cols]
            inner = _dot(q_h, k_t_b[rows, :]) * decay_ref[h]
            out = _dot(inner.astype(BF16), v_h) + _dot(q_h, state_b[rows, :]) * qdec_ref[h]
            k_dec = (k_t[rows, :] * kdec_ref[h]).astype(BF16)
            state_scr[b, rows, :] = cdec_ref[h] * state_scr[b, rows, :] + _dot(k_dec, v_h)
            mu = jnp.mean(out, axis=-1, keepdims=True)
            cen = out - mu
            var = jnp.mean(cen * cen, axis=-1, keepdims=True)
            y = cen * lax.rsqrt(var + GN_EPS)
            o_ref[b, :, cols] = (y * _silu(g_ref[b, :, cols].astype(F32))).astype(BF16)


def _retention_consts():
    h = np.arange(R_HEADS, dtype=np.float64)
    log_g = np.log(1.0 - np.exp2(-5.0 - h))
    n = np.arange(R_CHUNK, dtype=np.float64)
    diff = n[:, None] - n[None, :]
    scale = R_QK_DIM ** -0.5
    decay = np.where(diff >= 0, np.exp(np.maximum(diff, 0.0) * log_g[:, None, None]), 0.0) * scale
    q_decay = np.exp((n + 1.0) * log_g[:, None])[:, :, None]
    k_decay = np.exp((R_CHUNK - 1.0 - n) * log_g[:, None])[:, None, :] * scale
    chunk_decay = np.exp(R_CHUNK * log_g)
    return (jnp.asarray(decay, F32), jnp.asarray(q_decay, F32), jnp.asarray(k_decay, F32),
            jnp.asarray(chunk_decay, F32))


def _retention(proj3):
    bsz, seq, _ = proj3.shape
    c = R_CHUNK
    decay, qdec, kdec, cdec = _retention_consts()
    grid_spec = pltpu.PrefetchScalarGridSpec(
        num_scalar_prefetch=1,
        grid=(seq // c,),
        in_specs=[pl.BlockSpec((bsz, c, R_QK_WIDTH), lambda i, s: (0, i, COL_QR // R_QK_WIDTH)),
                  pl.BlockSpec((bsz, c, R_QK_WIDTH), lambda i, s: (0, i, COL_KR // R_QK_WIDTH)),
                  pl.BlockSpec((bsz, c, R_V_WIDTH), lambda i, s: (0, i, COL_VR // R_V_WIDTH)),
                  pl.BlockSpec((bsz, c, R_V_WIDTH), lambda i, s: (0, i, COL_GR // R_V_WIDTH)),
                  pl.BlockSpec((R_HEADS, c, c), lambda i, s: (0, 0, 0)),
                  pl.BlockSpec((R_HEADS, c, 1), lambda i, s: (0, 0, 0)),
                  pl.BlockSpec((R_HEADS, 1, c), lambda i, s: (0, 0, 0))],
        out_specs=pl.BlockSpec((bsz, c, R_V_WIDTH), lambda i, s: (0, i, 0)),
        scratch_shapes=[pltpu.VMEM((bsz, R_QK_WIDTH, R_V_DIM), F32)],
    )
    return pl.pallas_call(
        _retention_kernel,
        grid_spec=grid_spec,
        out_shape=jax.ShapeDtypeStruct((bsz, seq, R_V_WIDTH), BF16),
        compiler_params=_cparams(("arbitrary",)),
        name="retention",
    )(cdec, proj3, proj3, proj3, proj3, decay, qdec, kdec)


def _pack_halves(x):
    w = x.shape[1] // 2
    bits = lax.bitcast_convert_type(x.astype(BF16).astype(F32), jnp.uint32)
    return (bits[:, :w] >> 16) | (bits[:, w:] & jnp.uint32(0xFFFF0000))


def _unpack_halves(p):
    lo = lax.bitcast_convert_type(p << 16, F32)
    hi = lax.bitcast_convert_type(p & jnp.uint32(0xFFFF0000), F32)
    return lo, hi


LANES = 128
ROW_SUBLANES = D_MODEL // LANES


def _store_tile_rows(ref, lead, x):
    m = x.shape[0]
    for j in range(ROW_SUBLANES):
        ref[lead + (pl.ds(j, m, stride=ROW_SUBLANES), slice(None))] = x[:, j * LANES:(j + 1) * LANES]


def _load_tile_rows(ref, lead, m):
    return [ref[lead + (pl.ds(j, m, stride=ROW_SUBLANES), slice(None))] for j in range(ROW_SUBLANES)]


def _mix_kernel(ya_ref, yr_ref, ga_ref, gt_ref, x_ref, wpa_ref, wpr_ref, wout_ref,
                gt1_ref, g_ref, sc_ref, sh_ref, x1_ref, h2_ref, h2p_ref):
    a = _dot(ya_ref[...], wpa_ref[...]) * _sigmoid(ga_ref[...].astype(F32))
    r = _dot(yr_ref[...], wpr_ref[...]) * _sigmoid(gt_ref[...].astype(F32))
    mix = _dot((a + r).astype(BF16), wout_ref[...])
    x1 = x_ref[...] + gt1_ref[0] * mix
    x1_ref[...] = x1
    ms = jnp.mean(x1 * x1, axis=-1, keepdims=True)
    y = x1 * lax.rsqrt(ms + NORM_EPS) * g_ref[...]
    h2 = y * (1.0 + sc_ref[0]) + sh_ref[0]
    h2_ref[...] = h2.astype(BF16)
    h2p_ref[...] = _pack_halves(h2)


def _mix(ya, yr, proj, x2d, wpa, wpr, wout, gt1, g, sc, sh, seq):
    t, d = x2d.shape
    tm = min(512, seq)
    per_b = seq // tm
    row = lambda i: (i, 0)
    full = lambda i: (0, 0)
    per_batch = lambda i: (i // per_b, 0, 0)
    return pl.pallas_call(
        _mix_kernel,
        grid=(t // tm,),
        in_specs=[pl.BlockSpec((tm, A_WIDTH), row),
                  pl.BlockSpec((tm, R_V_WIDTH), row),
                  pl.BlockSpec((tm, d), lambda i: (i, COL_GA // D_MODEL)),
                  pl.BlockSpec((tm, d), lambda i: (i, COL_GT // D_MODEL)),
                  pl.BlockSpec((tm, d), row),
                  pl.BlockSpec((A_WIDTH, d), full),
                  pl.BlockSpec((R_V_WIDTH, d), full),
                  pl.BlockSpec((d, d), full),
                  pl.BlockSpec((1, 1, d), per_batch),
                  pl.BlockSpec((1, d), full),
                  pl.BlockSpec((1, 1, d), per_batch),
                  pl.BlockSpec((1, 1, d), per_batch)],
        out_specs=[pl.BlockSpec((tm, d), row), pl.BlockSpec((tm, d), row),
                   pl.BlockSpec((tm, d // 2), row)],
        out_shape=[jax.ShapeDtypeStruct((t, d), F32), jax.ShapeDtypeStruct((t, d), BF16),
                   jax.ShapeDtypeStruct((t, d // 2), jnp.uint32)],
        compiler_params=_cparams(("parallel",)),
        name="merge_outproj_norm",
    )(ya, yr, proj, proj, x2d, wpa, wpr, wout, gt1, g, sc, sh)


def _router_kernel(h_ref, wr_ref, b_ref, e_ref, w_ref, r_ref, c_ref):
    logits = _dot_nt(wr_ref[...], h_ref[...])
    scores = _sigmoid(logits)
    choice = scores + b_ref[...]
    tm = logits.shape[1]
    giota = lax.broadcasted_iota(jnp.int32, (GROUP_SIZE, tm), 0)
    gs_rows = []
    for g in range(N_GROUPS):
        cg = choice[g * GROUP_SIZE:(g + 1) * GROUP_SIZE, :]
        m1 = jnp.max(cg, axis=0, keepdims=True)
        i1 = jnp.min(jnp.where(cg == m1, giota, GROUP_SIZE), axis=0, keepdims=True)
        m2 = jnp.max(jnp.where(giota == i1, KNOCKED_OUT, cg), axis=0, keepdims=True)
        gs_rows.append(m1 + m2)
    gs = jnp.concatenate(gs_rows, axis=0)
    grow = lax.broadcasted_iota(jnp.int32, (N_GROUPS, tm), 0)
    gmask = jnp.zeros((N_GROUPS, tm), jnp.bool_)
    for _ in range(TOPK_GROUPS):
        mx = jnp.max(gs, axis=0, keepdims=True)
        ix = jnp.min(jnp.where(gs == mx, grow, N_GROUPS), axis=0, keepdims=True)
        hit = grow == ix
        gmask = jnp.logical_or(gmask, hit)
        gs = jnp.where(hit, KNOCKED_OUT, gs)
    gmask_f = jnp.where(gmask, 1.0, 0.0)
    masked = jnp.concatenate(
        [jnp.where(gmask_f[g:g + 1, :] > 0.5, choice[g * GROUP_SIZE:(g + 1) * GROUP_SIZE, :], NEG_INF)
         for g in range(N_GROUPS)], axis=0)
    erow = lax.broadcasted_iota(jnp.int32, (N_EXPERTS, tm), 0)
    idx_rows, w_rows = [], []
    chosen = jnp.zeros((N_EXPERTS, tm), F32)
    for _ in range(TOP_K):
        mx = jnp.max(masked, axis=0, keepdims=True)
        ix = jnp.min(jnp.where(masked == mx, erow, N_EXPERTS), axis=0, keepdims=True)
        hit = erow == ix
        w_rows.append(jnp.sum(jnp.where(hit, scores, 0.0), axis=0, keepdims=True))
        idx_rows.append(ix)
        chosen = jnp.where(hit, 1.0, chosen)
        masked = jnp.where(hit, KNOCKED_OUT, masked)
    w = jnp.concatenate(w_rows, axis=0)
    w = w / (jnp.sum(w, axis=0, keepdims=True) + 1e-20) * ROUTED_SCALE
    e_ref[...] = jnp.concatenate(idx_rows, axis=0)
    w_ref[...] = w
    chosen_b = chosen.astype(BF16)
    earlier = (lax.broadcasted_iota(jnp.int32, (tm, tm), 0)
               < lax.broadcasted_iota(jnp.int32, (tm, tm), 1)).astype(BF16)
    before = _dot(chosen_b, earlier)
    ranks = [jnp.sum(jnp.where(erow == ix, before, 0.0), axis=0, keepdims=True) for ix in idx_rows]
    r_ref[...] = jnp.concatenate(ranks, axis=0).astype(jnp.int32)
    c_ref[...] = _dot(chosen_b, jnp.ones((tm, 128), BF16))


ROUTER_ROWS = 512


def _router(h2, wr_t, bias_col):
    t, d = h2.shape
    tm = ROUTER_ROWS
    by_tile = lambda i: (0, i)
    return pl.pallas_call(
        _router_kernel,
        grid=(t // tm,),
        in_specs=[pl.BlockSpec((tm, d), lambda i: (i, 0)),
                  pl.BlockSpec((N_EXPERTS, d), lambda i: (0, 0)),
                  pl.BlockSpec((N_EXPERTS, 1), lambda i: (0, 0))],
        out_specs=[pl.BlockSpec((TOP_K, tm), by_tile), pl.BlockSpec((TOP_K, tm), by_tile),
                   pl.BlockSpec((TOP_K, tm), by_tile), pl.BlockSpec((N_EXPERTS, 128), by_tile)],
        out_shape=[jax.ShapeDtypeStruct((TOP_K, t), jnp.int32),
                   jax.ShapeDtypeStruct((TOP_K, t), F32),
                   jax.ShapeDtypeStruct((TOP_K, t), jnp.int32),
                   jax.ShapeDtypeStruct((N_EXPERTS, (t // tm) * 128), F32)],
        compiler_params=_cparams(("parallel",)),
        name="router_topk",
    )(h2, wr_t, bias_col)


def _pos_kernel(e_ref, r_ref, base_ref, p_ref):
    tm = e_ref.shape[1]
    erow = lax.broadcasted_iota(jnp.int32, (N_EXPERTS, tm), 0)
    base = base_ref[0]
    rows = [jnp.sum(jnp.where(erow == e_ref[k:k + 1, :], base, 0.0), axis=0, keepdims=True)
            for k in range(TOP_K)]
    p_ref[0] = jnp.concatenate(rows, axis=0).astype(jnp.int32) + r_ref[...]


MOVE_ROWS = 256


def _positions(eidx_t, rank_t, tile_base):
    t = eidx_t.shape[1]
    tm = min(MOVE_ROWS, t)
    per_router_tile = ROUTER_ROWS // tm
    return pl.pallas_call(
        _pos_kernel,
        grid=(t // tm,),
        in_specs=[pl.BlockSpec((TOP_K, tm), lambda i: (0, i)),
                  pl.BlockSpec((TOP_K, tm), lambda i: (0, i)),
                  pl.BlockSpec((1, N_EXPERTS, 1), lambda i: (i // per_router_tile, 0, 0))],
        out_specs=pl.BlockSpec((1, TOP_K, tm), lambda i: (i, 0, 0)),
        out_shape=jax.ShapeDtypeStruct((t // tm, TOP_K, tm), jnp.int32),
        compiler_params=_cparams(("parallel",)),
        name="slot_positions",
    )(eidx_t, rank_t, tile_base)


SLOT_ROWS = 512
PAD_CHUNKS = (256, 128, 64, 32, 16, 8)


def _zero_pads_kernel(pad_start_ref, pad_len_ref, xs_in, xs_hbm, zero_buf, pad_sem):
    del xs_in
    zero_buf[...] = jnp.zeros_like(zero_buf)

    def pad_copies(e, wait):
        start = pad_start_ref[e]
        n = pad_len_ref[e]
        head = jnp.minimum((-start) & (ROW_SUBLANES - 1), n)

        def fill(first, size, pred):
            @pl.when(pred)
            def _():
                cp = pltpu.make_async_copy(zero_buf.at[pl.ds(0, size), :], xs_hbm.at[pl.ds(first, size), :], pad_sem)
                if wait:
                    cp.wait()
                else:
                    cp.start()

        for j in range(ROW_SUBLANES - 1):
            fill(start + j, 1, j < head)
        ptr = start + head
        rest = n - head
        for chunk in PAD_CHUNKS:
            fill(pl.multiple_of(ptr, ROW_SUBLANES), chunk, (rest & chunk) != 0)
            ptr = ptr + (rest & chunk)

    def issue(e, carry):
        pad_copies(e, False)
        return carry

    def drain(e, carry):
        pad_copies(e, True)
        return carry

    lax.fori_loop(0, N_EXPERTS, issue, 0)
    lax.fori_loop(0, N_EXPERTS, drain, 0)


def _zero_pads(pad_start, pad_len, xs):
    grid_spec = pltpu.PrefetchScalarGridSpec(
        num_scalar_prefetch=2,
        grid=(1,),
        in_specs=[pl.BlockSpec(memory_space=pl.ANY)],
        out_specs=pl.BlockSpec(memory_space=pl.ANY),
        scratch_shapes=[pltpu.VMEM((PAD_CHUNKS[0], xs.shape[1]), xs.dtype),
                        pltpu.SemaphoreType.DMA],
    )
    return pl.pallas_call(
        _zero_pads_kernel,
        grid_spec=grid_spec,
        out_shape=jax.ShapeDtypeStruct(xs.shape, xs.dtype),
        input_output_aliases={2: 0},
        compiler_params=_cparams(("arbitrary",)),
        name="zero_pad_slots",
    )(pad_start, pad_len, xs)


SC_DISPATCH_TOKENS = 64


def _sc_dispatch(pos_blocks, h_rows, n_rows):
    info = plsc.get_sparse_core_info()
    n_cores = info.num_cores
    n_workers = n_cores * info.num_subcores
    t = h_rows.shape[0]
    chunk = SC_DISPATCH_TOKENS
    steps = t // (n_workers * chunk)
    mesh = plsc.VectorSubcoreMesh(core_axis_name="c", subcore_axis_name="s")

    @functools.partial(
        pl.kernel, mesh=mesh,
        out_type=jax.ShapeDtypeStruct((n_rows,) + h_rows.shape[1:], h_rows.dtype),
        scratch_types=[pltpu.VMEM((TOP_K, chunk), jnp.int32),
                       pltpu.VMEM((chunk,) + h_rows.shape[1:], h_rows.dtype),
                       pltpu.SemaphoreType.DMA],
        name="sc_dispatch_rows",
    )
    def scatter_rows(pos_hbm, h_hbm, out_hbm, idx_v, rows_v, sem):
        wid = lax.axis_index("s") * n_cores + lax.axis_index("c")

        @pl.loop(0, steps)
        def _(step):
            blk = wid * steps + step
            pltpu.sync_copy(pos_hbm.at[blk], idx_v)
            pltpu.sync_copy(h_hbm.at[pl.ds(blk * chunk, chunk)], rows_v)
            scatters = [pltpu.make_async_copy(rows_v, out_hbm.at[idx_v.at[k]], sem) for k in range(TOP_K)]
            for cp in scatters:
                cp.start()
            for cp in scatters:
                cp.wait()

    return scatter_rows(pos_blocks, h_rows)


def _experts_kernel(blk_e_ref, nblk_ref, x_ref, w1_ref, w3_ref, w2_ref, y_ref, w1b, w3b, w2b):
    s = pl.program_id(0)

    @pl.when(s < nblk_ref[0])
    def _():
        @pl.when(jnp.logical_or(s == 0, blk_e_ref[s] != blk_e_ref[jnp.maximum(s - 1, 0)]))
        def _():
            w1b[...] = w1_ref[0].astype(BF16)
            w3b[...] = w3_ref[0].astype(BF16)
            w2b[...] = w2_ref[0].astype(BF16)

        half = x_ref.shape[1]
        lo, hi = _unpack_halves(x_ref[...])
        lo = lo.astype(BF16)
        hi = hi.astype(BF16)
        h1 = _dot(lo, w1b[:half, :]) + _dot(hi, w1b[half:, :])
        h3 = _dot(lo, w3b[:half, :]) + _dot(hi, w3b[half:, :])
        mid = (_silu(h1) * h3).astype(BF16)
        y_ref[...] = _pack_halves(_dot(mid, w2b[...]))


def _sc_gather(pos_blocks, ys, t):
    info = plsc.get_sparse_core_info()
    n_cores = info.num_cores
    n_workers = n_cores * info.num_subcores
    chunk = SC_DISPATCH_TOKENS
    steps = t // (n_workers * chunk)
    mesh = plsc.VectorSubcoreMesh(core_axis_name="c", subcore_axis_name="s")

    @functools.partial(
        pl.kernel, mesh=mesh,
        out_type=jax.ShapeDtypeStruct((TOP_K * t,) + ys.shape[1:], ys.dtype),
        scratch_types=[pltpu.VMEM((TOP_K, chunk), jnp.int32),
                       pltpu.VMEM((chunk,) + ys.shape[1:], ys.dtype),
                       pltpu.VMEM((chunk,) + ys.shape[1:], ys.dtype),
                       pltpu.SemaphoreType.DMA((2,))],
        name="sc_gather_rows",
    )
    def gather_rows(pos_hbm, ys_hbm, out_hbm, idx_v, rows_a, rows_b, sems):
        wid = lax.axis_index("s") * n_cores + lax.axis_index("c")
        bufs = (rows_a, rows_b)

        @pl.loop(0, steps)
        def _(step):
            blk = wid * steps + step
            pltpu.sync_copy(pos_hbm.at[blk], idx_v)
            gathers = [pltpu.make_async_copy(ys_hbm.at[idx_v.at[k]], bufs[k % 2], sems.at[k % 2])
                       for k in range(TOP_K)]
            gathers[0].start()
            for k in range(TOP_K):
                gathers[k].wait()
                if k + 1 < TOP_K:
                    gathers[k + 1].start()
                pltpu.sync_copy(bufs[k % 2], out_hbm.at[pl.ds(k * t + blk * chunk, chunk)])

    return gather_rows(pos_blocks, ys)


def _experts(blk_e, nblk_used, xs, w1, w3, w2):
    n_rows, half = xs.shape
    d = D_MODEL
    blk = lambda s, be, nb: (jnp.minimum(s, nb[0] - 1), 0)
    wblk = lambda s, be, nb: (be[jnp.minimum(s, nb[0] - 1)], 0, 0)
    grid_spec = pltpu.PrefetchScalarGridSpec(
        num_scalar_prefetch=2,
        grid=(n_rows // SLOT_ROWS,),
        in_specs=[pl.BlockSpec((SLOT_ROWS, half), blk),
                  pl.BlockSpec((1, d, EXPERT_FF), wblk),
                  pl.BlockSpec((1, d, EXPERT_FF), wblk),
                  pl.BlockSpec((1, EXPERT_FF, d), wblk)],
        out_specs=pl.BlockSpec((SLOT_ROWS, half), blk),
        scratch_shapes=[pltpu.VMEM((d, EXPERT_FF), BF16),
                        pltpu.VMEM((d, EXPERT_FF), BF16),
                        pltpu.VMEM((EXPERT_FF, d), BF16)],
    )
    return pl.pallas_call(
        _experts_kernel,
        grid_spec=grid_spec,
        out_shape=jax.ShapeDtypeStruct((n_rows, half), jnp.uint32),
        compiler_params=_cparams(("arbitrary",)),
        name="routed_experts",
    )(blk_e, nblk_used, xs, w1, w3, w2)


def _combine_kernel(*refs):
    y_refs = refs[:TOP_K]
    w_ref, h_ref, x1_ref, ws1_ref, ws3_ref, ws2_ref, gt2_ref, g_ref = refs[TOP_K:TOP_K + 8]
    o_ref = refs[-1]
    tm, d = x1_ref.shape
    half = d // 2
    h = h_ref[...]
    mid = (_silu(_dot(h, ws1_ref[...])) * _dot(h, ws3_ref[...])).astype(BF16)
    shared = _dot(mid, ws2_ref[...])
    w = w_ref[...]
    acc_lo = jnp.zeros((tm, half), F32)
    acc_hi = jnp.zeros((tm, half), F32)
    for k in range(TOP_K):
        lo, hi = _unpack_halves(y_refs[k][...])
        acc_lo = acc_lo + lo * w[:, k:k + 1]
        acc_hi = acc_hi + hi * w[:, k:k + 1]
    routed = jnp.concatenate([acc_lo, acc_hi], axis=1)
    x2 = x1_ref[...] + gt2_ref[0] * (routed + shared)
    ms = jnp.mean(x2 * x2, axis=-1, keepdims=True)
    o_ref[...] = x2 * lax.rsqrt(ms + NORM_EPS) * g_ref[...]


COMBINE_ROWS = 256
COMBINE_PARTS = 4


def _combine(y_kt, wts, h2, x1, ws1, ws3, ws2, gt2, g_final, seq, part, n_parts, prev_out):
    t, d = x1.shape
    tm = min(COMBINE_ROWS, seq)
    per_b = seq // tm
    tiles = t // tm // n_parts
    first = part * tiles
    row = lambda i: (first + i, 0)
    full = lambda i: (0, 0)
    y_specs = [pl.BlockSpec((tm, d // 2), functools.partial(lambda i, k: (k * tiles + i, 0), k=k))
               for k in range(TOP_K)]
    in_specs = y_specs + [
        pl.BlockSpec((tm, TOP_K), row),
        pl.BlockSpec((tm, d), row),
        pl.BlockSpec((tm, d), row),
        pl.BlockSpec((d, SHARED_FF), full),
        pl.BlockSpec((d, SHARED_FF), full),
        pl.BlockSpec((SHARED_FF, d), full),
        pl.BlockSpec((1, 1, d), lambda i: ((first + i) // per_b, 0, 0)),
        pl.BlockSpec((1, d), full)]
    args = [y_kt] * TOP_K + [wts, h2, x1, ws1, ws3, ws2, gt2, g_final]
    aliases = {}
    if prev_out is not None:
        in_specs.append(pl.BlockSpec(memory_space=pl.ANY))
        aliases = {len(args): 0}
        args.append(prev_out)
    return pl.pallas_call(
        _combine_kernel,
        grid=(tiles,),
        in_specs=in_specs,
        out_specs=pl.BlockSpec((tm, d), row),
        out_shape=jax.ShapeDtypeStruct((t, d), F32),
        input_output_aliases=aliases,
        compiler_params=_cparams(("parallel",)),
        name="combine_shared_final",
    )(*args)


def _slot_tables(cnt, t):
    ntiles = cnt.shape[1] // 128
    cnt_tile = cnt.reshape(N_EXPERTS, ntiles, 128)[:, :, 0].astype(jnp.int32)
    counts = jnp.sum(cnt_tile, axis=1)
    padded = (counts + SLOT_ROWS - 1) // SLOT_ROWS * SLOT_ROWS
    pstart = jnp.cumsum(padded) - padded
    tile_base = pstart[:, None] + jnp.cumsum(cnt_tile, axis=1) - cnt_tile
    n_blk = -(-(t * TOP_K) // SLOT_ROWS) + N_EXPERTS
    blk_end = jnp.cumsum(padded // SLOT_ROWS)
    blk_e = jnp.sum((blk_end[None, :] <= jnp.arange(n_blk)[:, None]).astype(jnp.int32), axis=1)
    blk_e = jnp.minimum(blk_e, N_EXPERTS - 1)
    return (blk_e, blk_end[-1:].astype(jnp.int32), pstart + counts, padded - counts,
            tile_base.T.astype(F32).reshape(ntiles, N_EXPERTS, 1), n_blk * SLOT_ROWS)


def _permute_in_cols(w_in):
    qa, ka, va, qr, kr, vr, gr, ga, gt = jnp.split(
        w_in, np.cumsum((A_WIDTH, A_WIDTH, A_WIDTH, R_QK_WIDTH, R_QK_WIDTH, R_V_WIDTH, R_V_WIDTH,
                         D_MODEL))[:].tolist(), axis=1)
    return jnp.concatenate([vr, gr, ga, gt, qa, ka, va, qr, kr], axis=1)


def kernel(x, c, w_ada, b_ada, g_mix, w_in, w_pa, w_pr, w_out, g_ffn, w_router, router_bias,
           w1, w3, w2, ws1, ws3, ws2, g_final):
    bsz, seq, d = x.shape
    t = bsz * seq
    depth = w_ada.shape[0]
    assert depth == 1, "the final norm is fused into the single layer's last kernel"
    slopes = jnp.exp2(-8.0 / A_HEADS * jnp.arange(1, A_HEADS + 1, dtype=F32))
    x2d = x.reshape(t, d)
    for l in range(depth):
        mod = _ada(c, w_ada[l], b_ada[l])
        sh1, sc1, gt1, sh2, sc2, gt2 = [m.reshape(bsz, 1, d) for m in jnp.split(mod, 6, axis=-1)]
        w_in_p = _permute_in_cols(w_in[l]).astype(BF16)
        proj = _inproj(x2d, g_mix[l].reshape(1, d), sc1, sh1, w_in_p, seq)
        proj3 = proj.reshape(bsz, seq, IN_COLS)
        k_aug, q_aug_t, v_aug_t = _moba_prep(proj3, slopes)
        ya = _moba_attn(k_aug, q_aug_t, v_aug_t).reshape(t, A_WIDTH)
        yr = _retention(proj3).reshape(t, R_V_WIDTH)
        x1, h2, h2p = _mix(ya, yr, proj, x2d, w_pa[l].astype(BF16), w_pr[l].astype(BF16),
                           w_out[l].astype(BF16), gt1, g_ffn[l].reshape(1, d), sc2, sh2, seq)
        eidx_t, wts_t, rank_t, cnt = _router(h2, w_router[l].T.astype(BF16),
                                             router_bias[l].reshape(N_EXPERTS, 1))
        blk_e, nblk_used, pad_start, pad_len, tile_base, n_rows = _slot_tables(cnt, t)
        pos3 = _positions(eidx_t, rank_t, tile_base)
        pos_blocks = jnp.transpose(
            pos3.reshape(pos3.shape[0], TOP_K, -1, SC_DISPATCH_TOKENS), (0, 2, 1, 3)
        ).reshape(t // SC_DISPATCH_TOKENS, TOP_K, SC_DISPATCH_TOKENS)
        xs = _zero_pads(pad_start, pad_len, _sc_dispatch(pos_blocks, h2p, n_rows))
        ys = _experts(blk_e, nblk_used, xs, w1[l], w3[l], w2[l])
        wts = wts_t.T
        shared_w = (ws1[l].astype(BF16), ws3[l].astype(BF16), ws2[l].astype(BF16))
        blocks_per_part = pos_blocks.shape[0] // COMBINE_PARTS
        x2d = None
        for part in range(COMBINE_PARTS):
            y_kt = _sc_gather(pos_blocks[part * blocks_per_part:(part + 1) * blocks_per_part], ys,
                              t // COMBINE_PARTS)
            x2d = _combine(y_kt, wts, h2, x1, *shared_w, gt2, g_final.reshape(1, d), seq,
                           part, COMBINE_PARTS, x2d)
    return x2d.reshape(bsz, seq, d)
```

```python
import functools

import jax
import jax.numpy as jnp
import numpy as np
from jax import lax
from jax.experimental import pallas as pl
from jax.experimental.pallas import tpu as pltpu
from jax.experimental.pallas import tpu_sc as plsc

F32 = jnp.float32
BF16 = jnp.bfloat16

D_MODEL = 1024
A_HEADS = 8
A_HEAD_DIM = 64
A_WIDTH = A_HEADS * A_HEAD_DIM
MOBA_BLOCK = 256
MOBA_TOPK = 3
R_HEADS = 8
R_QK_DIM = 64
R_V_DIM = 128
R_QK_WIDTH = R_HEADS * R_QK_DIM
R_V_WIDTH = R_HEADS * R_V_DIM
R_CHUNK = 128
N_EXPERTS = 256
TOP_K = 8
N_GROUPS = 8
GROUP_SIZE = N_EXPERTS // N_GROUPS
TOPK_GROUPS = 4
EXPERT_FF = 256
SHARED_FF = 256
ROUTED_SCALE = 2.5
NORM_EPS = 1e-6
GN_EPS = 1e-6
NEG_INF = -1e30
KNOCKED_OUT = -3e38

COL_VR, COL_GR, COL_GA, COL_GT = 0, 1024, 2048, 3072
COL_QA, COL_KA, COL_VA, COL_QR, COL_KR = 4096, 4608, 5120, 5632, 6144
IN_COLS = 6656
AUG = 128
FEAT_BIAS = A_HEAD_DIM
FEAT_POS = A_HEAD_DIM + 32
V_ROWS = A_HEAD_DIM + 16
MOBA_HEADS_PER_STEP = 8

VMEM_LIMIT = 56 * 1024 * 1024


def _cparams(sem, vmem=VMEM_LIMIT):
    return pltpu.CompilerParams(dimension_semantics=sem, vmem_limit_bytes=vmem)


def _dot(a, b):
    return jnp.dot(a, b, preferred_element_type=F32)


def _dot_nt(a, b):
    return lax.dot_general(a, b, (((1,), (1,)), ((), ())), preferred_element_type=F32)


def _sigmoid(x):
    return 1.0 / (1.0 + jnp.exp(-x))


def _silu(x):
    return x * _sigmoid(x)


def _ada_kernel(c_ref, w_ref, b_ref, o_ref):
    c = c_ref[...]
    s = _silu(c)
    s_hi = s.astype(BF16)
    s_lo = (s - s_hi.astype(F32)).astype(BF16)
    w = w_ref[...]
    w_hi = w.astype(BF16)
    w_lo = (w - w_hi.astype(F32)).astype(BF16)
    o_ref[...] = _dot(s_hi, w_hi) + _dot(s_hi, w_lo) + _dot(s_lo, w_hi) + b_ref[...]


def _ada(c, w_ada, b_ada):
    bsz, d = c.shape
    n = w_ada.shape[1]
    tn = 1024
    return pl.pallas_call(
        _ada_kernel,
        grid=(n // tn,),
        in_specs=[pl.BlockSpec((bsz, d), lambda j: (0, 0)),
                  pl.BlockSpec((d, tn), lambda j: (0, j)),
                  pl.BlockSpec((1, tn), lambda j: (0, j))],
        out_specs=pl.BlockSpec((bsz, tn), lambda j: (0, j)),
        out_shape=jax.ShapeDtypeStruct((bsz, n), F32),
        compiler_params=_cparams(("parallel",)),
        name="ada_mod",
    )(c, w_ada, b_ada.reshape(1, n))


INPROJ_COLS = 512


def _inproj_kernel(x_ref, g_ref, sc_ref, sh_ref, w_ref, o_ref):
    x = x_ref[...]
    ms = jnp.mean(x * x, axis=-1, keepdims=True)
    y = x * lax.rsqrt(ms + NORM_EPS) * g_ref[...]
    h = (y * (1.0 + sc_ref[0]) + sh_ref[0]).astype(BF16)
    for j in range(w_ref.shape[1] // INPROJ_COLS):
        cols = slice(j * INPROJ_COLS, (j + 1) * INPROJ_COLS)
        o_ref[:, cols] = _dot(h, w_ref[:, cols]).astype(BF16)


def _inproj(x2d, g, sc, sh, w_bf16, seq):
    t, d = x2d.shape
    n = w_bf16.shape[1]
    tm = min(512, seq)
    per_b = seq // tm
    return pl.pallas_call(
        _inproj_kernel,
        grid=(t // tm,),
        in_specs=[pl.BlockSpec((tm, d), lambda i: (i, 0)),
                  pl.BlockSpec((1, d), lambda i: (0, 0)),
                  pl.BlockSpec((1, 1, d), lambda i: (i // per_b, 0, 0)),
                  pl.BlockSpec((1, 1, d), lambda i: (i // per_b, 0, 0)),
                  pl.BlockSpec((d, n), lambda i: (0, 0))],
        out_specs=pl.BlockSpec((tm, n), lambda i: (i, 0)),
        out_shape=jax.ShapeDtypeStruct((t, n), BF16),
        compiler_params=_cparams(("parallel",)),
        name="norm_inproj",
    )(x2d, g, sc, sh, w_bf16)


def _moba_prep_kernel(slopes_ref, q_ref, k_ref, v_ref, ko_ref, qo_ref, vo_ref, kmean_scr):
    i = pl.program_id(1)
    nblk = kmean_scr.shape[0]
    width = q_ref.shape[1]

    @pl.when(i == 0)
    def _():
        kmean_scr[...] = jnp.zeros_like(kmean_scr)

    q = q_ref[...]
    k = k_ref[...]
    v = v_ref[...]
    kmean_scr[pl.ds(i, 1), :] = jnp.mean(k.astype(F32), axis=0, keepdims=True)

    eye = (lax.broadcasted_iota(jnp.int32, (width, width), 0)
           == lax.broadcasted_iota(jnp.int32, (width, width), 1)).astype(BF16)
    q_t = _dot_nt(eye, q)
    v_t = _dot_nt(eye, v)

    km = kmean_scr[...]
    km_rep = jnp.concatenate([km] * A_HEADS, axis=0)
    r_head = lax.broadcasted_iota(jnp.int32, km_rep.shape, 0) // nblk
    c_head = lax.broadcasted_iota(jnp.int32, km_rep.shape, 1) // A_HEAD_DIM
    km_bd = jnp.where(r_head == c_head, km_rep, 0.0)
    km_hi = km_bd.astype(BF16)
    km_lo = (km_bd - km_hi.astype(F32)).astype(BF16)
    q_t_b = q_t.astype(BF16)
    gate_all = _dot(km_hi, q_t_b) + _dot(km_lo, q_t_b)

    mb = q.shape[0]
    blk = lax.broadcasted_iota(jnp.int32, (nblk, mb), 0)
    lane_pos = lax.broadcasted_iota(jnp.int32, (16, mb), 1).astype(F32)
    row16 = lax.broadcasted_iota(jnp.int32, (16, mb), 0)
    key_pos = lax.broadcasted_iota(jnp.int32, (mb, AUG), 0).astype(F32)
    kcol = lax.broadcasted_iota(jnp.int32, (mb, AUG), 1)
    sel_r = lax.broadcasted_iota(jnp.int32, (width, AUG), 0)
    sel_c = lax.broadcasted_iota(jnp.int32, (width, AUG), 1)

    for h in range(A_HEADS):
        slope = slopes_ref[h]
        g = jnp.where(blk < i, gate_all[h * nblk:(h + 1) * nblk, :], NEG_INF)
        sel = jnp.zeros((nblk, mb), jnp.bool_)
        for r in range(MOBA_TOPK):
            m = jnp.max(g, axis=0, keepdims=True)
            idx = jnp.min(jnp.where(g == m, blk, nblk), axis=0, keepdims=True)
            hit = blk == idx
            sel = jnp.logical_or(sel, jnp.logical_and(hit, r < i))
            g = jnp.where(hit, KNOCKED_OUT, g)
        bias_t = jnp.where(sel, 0.0, NEG_INF)

        scale = A_HEAD_DIM ** -0.5
        qo_ref[0, h, 0:A_HEAD_DIM, :] = (q_t[h * A_HEAD_DIM:(h + 1) * A_HEAD_DIM, :] * scale).astype(BF16)
        qo_ref[0, h, FEAT_BIAS:FEAT_BIAS + nblk, :] = bias_t.astype(BF16)
        if nblk < 32:
            qo_ref[0, h, FEAT_BIAS + nblk:FEAT_POS, :] = jnp.zeros((32 - nblk, mb), BF16)
        blk_off = slope * (i * mb).astype(F32)
        pos_feat = jnp.where(row16 == 0, -slope * lane_pos,
                             jnp.where(row16 == 2, -blk_off,
                                       jnp.where(jnp.logical_or(row16 == 1, row16 == 3), 1.0, 0.0)))
        qo_ref[0, h, FEAT_POS:FEAT_POS + 16, :] = pos_feat.astype(BF16)
        qo_ref[0, h, FEAT_POS + 16:AUG, :] = jnp.zeros((AUG - FEAT_POS - 16, mb), BF16)

        vo_ref[0, h, 0:A_HEAD_DIM, :] = v_t[h * A_HEAD_DIM:(h + 1) * A_HEAD_DIM, :].astype(BF16)
        vo_ref[0, h, A_HEAD_DIM:V_ROWS, :] = jnp.where(row16 == 0, 1.0, 0.0).astype(BF16)

        pick = jnp.where(jnp.logical_and(sel_r == sel_c + h * A_HEAD_DIM, sel_c < A_HEAD_DIM),
                         1.0, 0.0).astype(BF16)
        k_feat = jnp.where(
            jnp.logical_or(kcol == FEAT_BIAS + i, jnp.logical_or(kcol == FEAT_POS, kcol == FEAT_POS + 2)), 1.0,
            jnp.where(kcol == FEAT_POS + 1, slope * key_pos, jnp.where(kcol == FEAT_POS + 3, blk_off, 0.0)))
        ko_ref[0, h, :, :] = (_dot(k, pick) + k_feat).astype(BF16)


def _moba_prep(proj3, slopes):
    bsz, seq, _ = proj3.shape
    nblk = seq // MOBA_BLOCK
    mb = MOBA_BLOCK
    grid_spec = pltpu.PrefetchScalarGridSpec(
        num_scalar_prefetch=1,
        grid=(bsz, nblk),
        in_specs=[pl.BlockSpec((None, mb, A_WIDTH), lambda b, i, s: (b, i, COL_QA // A_WIDTH)),
                  pl.BlockSpec((None, mb, A_WIDTH), lambda b, i, s: (b, i, COL_KA // A_WIDTH)),
                  pl.BlockSpec((None, mb, A_WIDTH), lambda b, i, s: (b, i, COL_VA // A_WIDTH))],
        out_specs=[pl.BlockSpec((1, A_HEADS, mb, AUG), lambda b, i, s: (b, 0, i, 0)),
                   pl.BlockSpec((1, A_HEADS, AUG, mb), lambda b, i, s: (b, 0, 0, i)),
                   pl.BlockSpec((1, A_HEADS, V_ROWS, mb), lambda b, i, s: (b, 0, 0, i))],
        scratch_shapes=[pltpu.VMEM((nblk, A_WIDTH), F32)],
    )
    return pl.pallas_call(
        _moba_prep_kernel,
        grid_spec=grid_spec,
        out_shape=[jax.ShapeDtypeStruct((bsz, A_HEADS, seq, AUG), BF16),
                   jax.ShapeDtypeStruct((bsz, A_HEADS, AUG, seq), BF16),
                   jax.ShapeDtypeStruct((bsz, A_HEADS, V_ROWS, seq), BF16)],
        compiler_params=_cparams(("parallel", "arbitrary")),
        name="moba_prep",
    )(slopes, proj3, proj3, proj3)


def _moba_attn_kernel(q_ref, k_ref, v_ref, o_ref, s_a, s_b, *, group, n_groups):
    i = pl.program_id(2)
    mb = MOBA_BLOCK
    span = group * mb
    own = pl.multiple_of(i * mb, mb)
    key_i = lax.broadcasted_iota(jnp.int32, (mb, mb), 0)
    qry_i = lax.broadcasted_iota(jnp.int32, (mb, mb), 1)
    feat = lax.broadcasted_iota(jnp.int32, (AUG, mb), 0)
    is_bias = jnp.logical_and(feat >= FEAT_BIAS, feat < FEAT_POS)
    q_ts, carry0 = [], []
    for hh in range(MOBA_HEADS_PER_STEP):
        q_t = q_ref[0, hh]
        q_ts.append(q_t)
        q_own = jnp.where(is_bias, jnp.zeros_like(q_t), q_t)
        s = _dot(k_ref[0, hh, pl.ds(own, mb), :], q_own)
        s = jnp.where(key_i <= qry_i, s, NEG_INF)
        m0 = jnp.max(s, axis=0, keepdims=True)
        p = jnp.exp(s - m0)
        carry0 += [m0, _dot(v_ref[0, hh, :, pl.ds(own, mb)], p.astype(BF16))]

    def scores(g, dst):
        start = pl.multiple_of(jnp.minimum(g, n_groups - 1) * span, span)
        for hh in range(MOBA_HEADS_PER_STEP):
            dst[hh] = _dot(k_ref[0, hh, pl.ds(start, span), :], q_ts[hh])

    def absorb(g, src, carry):
        start = pl.multiple_of(g * span, span)
        new = []
        for hh in range(MOBA_HEADS_PER_STEP):
            m, acc = carry[2 * hh], carry[2 * hh + 1]
            sb = src[hh]
            m_new = jnp.maximum(m, jnp.max(sb, axis=0, keepdims=True))
            pb = jnp.exp(sb - m_new)
            alpha = jnp.exp(m - m_new)
            acc = acc * alpha + _dot(v_ref[0, hh, :, pl.ds(start, span)], pb.astype(BF16))
            new += [m_new, acc]
        return tuple(new)

    def body(pair, carry):
        scores(2 * pair + 1, s_b)
        carry = absorb(2 * pair, s_a, carry)
        scores(2 * pair + 2, s_a)
        return absorb(2 * pair + 1, s_b, carry)

    scores(0, s_a)
    live_groups = (i + group - 1) // group
    res = lax.fori_loop(0, (live_groups + 1) // 2, body, tuple(carry0))
    outs = [res[2 * hh + 1][0:A_HEAD_DIM, :] / res[2 * hh + 1][A_HEAD_DIM:A_HEAD_DIM + 1, :] for hh in range(MOBA_HEADS_PER_STEP)]
    o_t = jnp.concatenate(outs, axis=0).astype(BF16)
    eye = (key_i == qry_i).astype(BF16)
    o_ref[0] = _dot_nt(eye, o_t).astype(BF16)


def _moba_attn(k_aug, q_aug_t, v_aug_t):
    bsz, nh, seq, _ = k_aug.shape
    mb = MOBA_BLOCK
    group = min(2, seq // mb)
    n_groups = seq // (group * mb)
    hps = MOBA_HEADS_PER_STEP
    return pl.pallas_call(
        functools.partial(_moba_attn_kernel, group=group, n_groups=n_groups),
        grid=(bsz, nh // hps, seq // mb),
        in_specs=[pl.BlockSpec((1, hps, AUG, mb), lambda b, h, i: (b, h, 0, i)),
                  pl.BlockSpec((1, hps, seq, AUG), lambda b, h, i: (b, h, 0, 0), pipeline_mode=pl.Buffered(1)),
                  pl.BlockSpec((1, hps, V_ROWS, seq), lambda b, h, i: (b, h, 0, 0), pipeline_mode=pl.Buffered(1))],
        out_specs=pl.BlockSpec((1, mb, hps * A_HEAD_DIM), lambda b, h, i: (b, i, h)),
        out_shape=jax.ShapeDtypeStruct((bsz, seq, A_WIDTH), BF16),
        scratch_shapes=[pltpu.VMEM((hps, group * mb, mb), F32), pltpu.VMEM((hps, group * mb, mb), F32)],
        compiler_params=_cparams(("parallel", "parallel", "arbitrary")),
        name="moba_attn",
    )(q_aug_t, k_aug, v_aug_t)


def _retention_kernel(cdec_ref, q_ref, k_ref, v_ref, g_ref, decay_ref, qdec_ref, kdec_ref, o_ref, state_scr):
    @pl.when(pl.program_id(0) == 0)
    def _():
        state_scr[...] = jnp.zeros_like(state_scr)

    width = q_ref.shape[2]
    eye = (lax.broadcasted_iota(jnp.int32, (width, width), 0)
           == lax.broadcasted_iota(jnp.int32, (width, width), 1)).astype(BF16)
    for b in range(q_ref.shape[0]):
        q = q_ref[b]
        k_t = _dot_nt(eye, k_ref[b])
        k_t_b = k_t.astype(BF16)
        state_b = state_scr[b].astype(BF16)
        for h in range(R_HEADS):
            rows = slice(h * R_QK_DIM, (h + 1) * R_QK_DIM)
            cols = slice(h * R_V_DIM, (h + 1) * R_V_DIM)
            q_h = q[:, rows]
            v_h = v_ref[b, :, cols]
            inner = _dot(q_h, k_t_b[rows, :]) * decay_ref[h]
            out = _dot(inner.astype(BF16), v_h) + _dot(q_h, state_b[rows, :]) * qdec_ref[h]
            k_dec = (k_t[rows, :] * kdec_ref[h]).astype(BF16)
            state_scr[b, rows, :] = cdec_ref[h] * state_scr[b, rows, :] + _dot(k_dec, v_h)
            mu = jnp.mean(out, axis=-1, keepdims=True)
            cen = out - mu
            var = jnp.mean(cen * cen, axis=-1, keepdims=True)
            y = cen * lax.rsqrt(var + GN_EPS)
            o_ref[b, :, cols] = (y * _silu(g_ref[b, :, cols].astype(F32))).astype(BF16)


def _retention_consts():
    h = np.arange(R_HEADS, dtype=np.float64)
    log_g = np.log(1.0 - np.exp2(-5.0 - h))
    n = np.arange(R_CHUNK, dtype=np.float64)
    diff = n[:, None] - n[None, :]
    scale = R_QK_DIM ** -0.5
    decay = np.where(diff >= 0, np.exp(np.maximum(diff, 0.0) * log_g[:, None, None]), 0.0) * scale
    q_decay = np.exp((n + 1.0) * log_g[:, None])[:, :, None]
    k_decay = np.exp((R_CHUNK - 1.0 - n) * log_g[:, None])[:, None, :] * scale
    chunk_decay = np.exp(R_CHUNK * log_g)
    return (jnp.asarray(decay, F32), jnp.asarray(q_decay, F32), jnp.asarray(k_decay, F32),
            jnp.asarray(chunk_decay, F32))


def _retention(proj3):
    bsz, seq, _ = proj3.shape
    c = R_CHUNK
    decay, qdec, kdec, cdec = _retention_consts()
    grid_spec = pltpu.PrefetchScalarGridSpec(
        num_scalar_prefetch=1,
        grid=(seq // c,),
        in_specs=[pl.BlockSpec((bsz, c, R_QK_WIDTH), lambda i, s: (0, i, COL_QR // R_QK_WIDTH)),
                  pl.BlockSpec((bsz, c, R_QK_WIDTH), lambda i, s: (0, i, COL_KR // R_QK_WIDTH)),
                  pl.BlockSpec((bsz, c, R_V_WIDTH), lambda i, s: (0, i, COL_VR // R_V_WIDTH)),
                  pl.BlockSpec((bsz, c, R_V_WIDTH), lambda i, s: (0, i, COL_GR // R_V_WIDTH)),
                  pl.BlockSpec((R_HEADS, c, c), lambda i, s: (0, 0, 0)),
                  pl.BlockSpec((R_HEADS, c, 1), lambda i, s: (0, 0, 0)),
                  pl.BlockSpec((R_HEADS, 1, c), lambda i, s: (0, 0, 0))],
        out_specs=pl.BlockSpec((bsz, c, R_V_WIDTH), lambda i, s: (0, i, 0)),
        scratch_shapes=[pltpu.VMEM((bsz, R_QK_WIDTH, R_V_DIM), F32)],
    )
    return pl.pallas_call(
        _retention_kernel,
        grid_spec=grid_spec,
        out_shape=jax.ShapeDtypeStruct((bsz, seq, R_V_WIDTH), BF16),
        compiler_params=_cparams(("arbitrary",)),
        name="retention",
    )(cdec, proj3, proj3, proj3, proj3, decay, qdec, kdec)


def _pack_halves(x):
    w = x.shape[1] // 2
    bits = lax.bitcast_convert_type(x.astype(BF16).astype(F32), jnp.uint32)
    return (bits[:, :w] >> 16) | (bits[:, w:] & jnp.uint32(0xFFFF0000))


def _unpack_halves(p):
    lo = lax.bitcast_convert_type(p << 16, F32)
    hi = lax.bitcast_convert_type(p & jnp.uint32(0xFFFF0000), F32)
    return lo, hi


SUBLANES = 8


def _mix_kernel(ya_ref, yr_ref, ga_ref, gt_ref, x_ref, wpa_ref, wpr_ref, wout_ref,
                gt1_ref, g_ref, sc_ref, sh_ref, x1_ref, h2_ref, h2p_ref):
    a = _dot(ya_ref[...], wpa_ref[...]) * _sigmoid(ga_ref[...].astype(F32))
    r = _dot(yr_ref[...], wpr_ref[...]) * _sigmoid(gt_ref[...].astype(F32))
    mix = _dot((a + r).astype(BF16), wout_ref[...])
    x1 = x_ref[...] + gt1_ref[0] * mix
    x1_ref[...] = x1
    ms = jnp.mean(x1 * x1, axis=-1, keepdims=True)
    y = x1 * lax.rsqrt(ms + NORM_EPS) * g_ref[...]
    h2 = y * (1.0 + sc_ref[0]) + sh_ref[0]
    h2_ref[...] = h2.astype(BF16)
    h2p_ref[...] = _pack_halves(h2)


def _mix(ya, yr, proj, x2d, wpa, wpr, wout, gt1, g, sc, sh, seq):
    t, d = x2d.shape
    tm = min(512, seq)
    per_b = seq // tm
    row = lambda i: (i, 0)
    full = lambda i: (0, 0)
    per_batch = lambda i: (i // per_b, 0, 0)
    return pl.pallas_call(
        _mix_kernel,
        grid=(t // tm,),
        in_specs=[pl.BlockSpec((tm, A_WIDTH), row),
                  pl.BlockSpec((tm, R_V_WIDTH), row),
                  pl.BlockSpec((tm, d), lambda i: (i, COL_GA // D_MODEL)),
                  pl.BlockSpec((tm, d), lambda i: (i, COL_GT // D_MODEL)),
                  pl.BlockSpec((tm, d), row),
                  pl.BlockSpec((A_WIDTH, d), full),
                  pl.BlockSpec((R_V_WIDTH, d), full),
                  pl.BlockSpec((d, d), full),
                  pl.BlockSpec((1, 1, d), per_batch),
                  pl.BlockSpec((1, d), full),
                  pl.BlockSpec((1, 1, d), per_batch),
                  pl.BlockSpec((1, 1, d), per_batch)],
        out_specs=[pl.BlockSpec((tm, d), row), pl.BlockSpec((tm, d), row),
                   pl.BlockSpec((tm, d // 2), row)],
        out_shape=[jax.ShapeDtypeStruct((t, d), F32), jax.ShapeDtypeStruct((t, d), BF16),
                   jax.ShapeDtypeStruct((t, d // 2), jnp.uint32)],
        compiler_params=_cparams(("parallel",)),
        name="merge_outproj_norm",
    )(ya, yr, proj, proj, x2d, wpa, wpr, wout, gt1, g, sc, sh)


def _router_kernel(h_ref, wr_ref, b_ref, e_ref, w_ref, r_ref, c_ref):
    logits = _dot_nt(wr_ref[...], h_ref[...])
    scores = _sigmoid(logits)
    choice = scores + b_ref[...]
    tm = logits.shape[1]
    giota = lax.broadcasted_iota(jnp.int32, (GROUP_SIZE, tm), 0)
    gs_rows = []
    for g in range(N_GROUPS):
        cg = choice[g * GROUP_SIZE:(g + 1) * GROUP_SIZE, :]
        m1 = jnp.max(cg, axis=0, keepdims=True)
        i1 = jnp.min(jnp.where(cg == m1, giota, GROUP_SIZE), axis=0, keepdims=True)
        m2 = jnp.max(jnp.where(giota == i1, KNOCKED_OUT, cg), axis=0, keepdims=True)
        gs_rows.append(m1 + m2)
    gs = jnp.concatenate(gs_rows, axis=0)
    grow = lax.broadcasted_iota(jnp.int32, (N_GROUPS, tm), 0)
    gmask = jnp.zeros((N_GROUPS, tm), jnp.bool_)
    for _ in range(TOPK_GROUPS):
        mx = jnp.max(gs, axis=0, keepdims=True)
        ix = jnp.min(jnp.where(gs == mx, grow, N_GROUPS), axis=0, keepdims=True)
        hit = grow == ix
        gmask = jnp.logical_or(gmask, hit)
        gs = jnp.where(hit, KNOCKED_OUT, gs)
    gmask_f = jnp.where(gmask, 1.0, 0.0)
    masked = jnp.concatenate(
        [jnp.where(gmask_f[g:g + 1, :] > 0.5, choice[g * GROUP_SIZE:(g + 1) * GROUP_SIZE, :], NEG_INF)
         for g in range(N_GROUPS)], axis=0)
    erow = lax.broadcasted_iota(jnp.int32, (N_EXPERTS, tm), 0)
    idx_rows, w_rows = [], []
    chosen = jnp.zeros((N_EXPERTS, tm), F32)
    for _ in range(TOP_K):
        mx = jnp.max(masked, axis=0, keepdims=True)
        ix = jnp.min(jnp.where(masked == mx, erow, N_EXPERTS), axis=0, keepdims=True)
        hit = erow == ix
        w_rows.append(jnp.sum(jnp.where(hit, scores, 0.0), axis=0, keepdims=True))
        idx_rows.append(ix)
        chosen = jnp.where(hit, 1.0, chosen)
        masked = jnp.where(hit, KNOCKED_OUT, masked)
    w = jnp.concatenate(w_rows, axis=0)
    w = w / (jnp.sum(w, axis=0, keepdims=True) + 1e-20) * ROUTED_SCALE
    e_ref[...] = jnp.concatenate(idx_rows, axis=0)
    w_ref[...] = w
    chosen_b = chosen.astype(BF16)
    earlier = (lax.broadcasted_iota(jnp.int32, (tm, tm), 0)
               < lax.broadcasted_iota(jnp.int32, (tm, tm), 1)).astype(BF16)
    before = _dot(chosen_b, earlier)
    ranks = [jnp.sum(jnp.where(erow == ix, before, 0.0), axis=0, keepdims=True) for ix in idx_rows]
    r_ref[...] = jnp.concatenate(ranks, axis=0).astype(jnp.int32)
    c_ref[...] = _dot(chosen_b, jnp.ones((tm, 128), BF16))


ROUTER_ROWS = 512


def _router(h2, wr_t, bias_col):
    t, d = h2.shape
    tm = ROUTER_ROWS
    by_tile = lambda i: (0, i)
    return pl.pallas_call(
        _router_kernel,
        grid=(t // tm,),
        in_specs=[pl.BlockSpec((tm, d), lambda i: (i, 0)),
                  pl.BlockSpec((N_EXPERTS, d), lambda i: (0, 0)),
                  pl.BlockSpec((N_EXPERTS, 1), lambda i: (0, 0))],
        out_specs=[pl.BlockSpec((TOP_K, tm), by_tile), pl.BlockSpec((TOP_K, tm), by_tile),
                   pl.BlockSpec((TOP_K, tm), by_tile), pl.BlockSpec((N_EXPERTS, 128), by_tile)],
        out_shape=[jax.ShapeDtypeStruct((TOP_K, t), jnp.int32),
                   jax.ShapeDtypeStruct((TOP_K, t), F32),
                   jax.ShapeDtypeStruct((TOP_K, t), jnp.int32),
                   jax.ShapeDtypeStruct((N_EXPERTS, (t // tm) * 128), F32)],
        compiler_params=_cparams(("parallel",)),
        name="router_topk",
    )(h2, wr_t, bias_col)


def _pos_kernel(e_ref, r_ref, base_ref, p_ref):
    tm = e_ref.shape[1]
    erow = lax.broadcasted_iota(jnp.int32, (N_EXPERTS, tm), 0)
    base = base_ref[0]
    rows = [jnp.sum(jnp.where(erow == e_ref[k:k + 1, :], base, 0.0), axis=0, keepdims=True)
            for k in range(TOP_K)]
    p_ref[0] = jnp.concatenate(rows, axis=0).astype(jnp.int32) + r_ref[...]


MOVE_ROWS = 512


def _positions(eidx_t, rank_t, tile_base):
    t = eidx_t.shape[1]
    tm = min(MOVE_ROWS, t)
    per_router_tile = ROUTER_ROWS // tm
    return pl.pallas_call(
        _pos_kernel,
        grid=(t // tm,),
        in_specs=[pl.BlockSpec((TOP_K, tm), lambda i: (0, i)),
                  pl.BlockSpec((TOP_K, tm), lambda i: (0, i)),
                  pl.BlockSpec((1, N_EXPERTS, 1), lambda i: (i // per_router_tile, 0, 0))],
        out_specs=pl.BlockSpec((1, TOP_K, tm), lambda i: (i, 0, 0)),
        out_shape=jax.ShapeDtypeStruct((t // tm, TOP_K, tm), jnp.int32),
        compiler_params=_cparams(("parallel",)),
        name="slot_positions",
    )(eidx_t, rank_t, tile_base)


SLOT_ROWS = 512
PAD_CHUNKS = (256, 128, 64, 32, 16, 8)


def _zero_pads_kernel(pad_start_ref, pad_len_ref, xs_in, xs_hbm, zero_buf, pad_sem):
    del xs_in
    zero_buf[...] = jnp.zeros_like(zero_buf)

    def pad_copies(e, wait):
        start = pad_start_ref[e]
        n = pad_len_ref[e]
        head = jnp.minimum((-start) & (SUBLANES - 1), n)

        def fill(first, size, pred):
            @pl.when(pred)
            def _():
                cp = pltpu.make_async_copy(zero_buf.at[pl.ds(0, size), :], xs_hbm.at[pl.ds(first, size), :], pad_sem)
                if wait:
                    cp.wait()
                else:
                    cp.start()

        for j in range(SUBLANES - 1):
            fill(start + j, 1, j < head)
        ptr = start + head
        rest = n - head
        for chunk in PAD_CHUNKS:
            fill(pl.multiple_of(ptr, SUBLANES), chunk, (rest & chunk) != 0)
            ptr = ptr + (rest & chunk)

    def issue(e, carry):
        pad_copies(e, False)
        return carry

    def drain(e, carry):
        pad_copies(e, True)
        return carry

    lax.fori_loop(0, N_EXPERTS, issue, 0)
    lax.fori_loop(0, N_EXPERTS, drain, 0)


def _zero_pads(pad_start, pad_len, xs):
    grid_spec = pltpu.PrefetchScalarGridSpec(
        num_scalar_prefetch=2,
        grid=(1,),
        in_specs=[pl.BlockSpec(memory_space=pl.ANY)],
        out_specs=pl.BlockSpec(memory_space=pl.ANY),
        scratch_shapes=[pltpu.VMEM((PAD_CHUNKS[0], xs.shape[1]), xs.dtype),
                        pltpu.SemaphoreType.DMA],
    )
    return pl.pallas_call(
        _zero_pads_kernel,
        grid_spec=grid_spec,
        out_shape=jax.ShapeDtypeStruct(xs.shape, xs.dtype),
        input_output_aliases={2: 0},
        compiler_params=_cparams(("arbitrary",)),
        name="zero_pad_slots",
    )(pad_start, pad_len, xs)


SC_DISPATCH_TOKENS = 64


def _sc_dispatch(pos_blocks, h_rows, n_rows):
    info = plsc.get_sparse_core_info()
    n_cores = info.num_cores
    n_workers = n_cores * info.num_subcores
    t = h_rows.shape[0]
    chunk = SC_DISPATCH_TOKENS
    steps = t // (n_workers * chunk)
    mesh = plsc.VectorSubcoreMesh(core_axis_name="c", subcore_axis_name="s")

    @functools.partial(
        pl.kernel, mesh=mesh,
        out_type=jax.ShapeDtypeStruct((n_rows,) + h_rows.shape[1:], h_rows.dtype),
        scratch_types=[pltpu.VMEM((TOP_K, chunk), jnp.int32),
                       pltpu.VMEM((chunk,) + h_rows.shape[1:], h_rows.dtype),
                       pltpu.SemaphoreType.DMA],
        name="sc_dispatch_rows",
    )
    def scatter_rows(pos_hbm, h_hbm, out_hbm, idx_v, rows_v, sem):
        wid = lax.axis_index("s") * n_cores + lax.axis_index("c")

        @pl.loop(0, steps)
        def _(step):
            blk = wid * steps + step
            pltpu.sync_copy(pos_hbm.at[blk], idx_v)
            pltpu.sync_copy(h_hbm.at[pl.ds(blk * chunk, chunk)], rows_v)
            scatters = [pltpu.make_async_copy(rows_v, out_hbm.at[idx_v.at[k]], sem) for k in range(TOP_K)]
            for cp in scatters:
                cp.start()
            for cp in scatters:
                cp.wait()

    return scatter_rows(pos_blocks, h_rows)


def _experts_kernel(blk_e_ref, nblk_ref, x_ref, w1_ref, w3_ref, w2_ref, y_ref, w1b, w3b, w2b):
    s = pl.program_id(0)

    @pl.when(s < nblk_ref[0])
    def _():
        @pl.when(jnp.logical_or(s == 0, blk_e_ref[s] != blk_e_ref[jnp.maximum(s - 1, 0)]))
        def _():
            w1b[...] = w1_ref[0].astype(BF16)
            w3b[...] = w3_ref[0].astype(BF16)
            w2b[...] = w2_ref[0].astype(BF16)

        half = x_ref.shape[1]
        lo, hi = _unpack_halves(x_ref[...])
        lo = lo.astype(BF16)
        hi = hi.astype(BF16)
        h1 = _dot(lo, w1b[:half, :]) + _dot(hi, w1b[half:, :])
        h3 = _dot(lo, w3b[:half, :]) + _dot(hi, w3b[half:, :])
        mid = (_silu(h1) * h3).astype(BF16)
        y_ref[...] = _pack_halves(_dot(mid, w2b[...]))


def _sc_gather(pos_blocks, ys, t):
    info = plsc.get_sparse_core_info()
    n_cores = info.num_cores
    n_workers = n_cores * info.num_subcores
    chunk = SC_DISPATCH_TOKENS
    steps = t // (n_workers * chunk)
    mesh = plsc.VectorSubcoreMesh(core_axis_name="c", subcore_axis_name="s")

    @functools.partial(
        pl.kernel, mesh=mesh,
        out_type=jax.ShapeDtypeStruct((TOP_K * t,) + ys.shape[1:], ys.dtype),
        scratch_types=[pltpu.VMEM((TOP_K, chunk), jnp.int32),
                       pltpu.VMEM((chunk,) + ys.shape[1:], ys.dtype),
                       pltpu.VMEM((chunk,) + ys.shape[1:], ys.dtype),
                       pltpu.SemaphoreType.DMA((2,))],
        name="sc_gather_rows",
    )
    def gather_rows(pos_hbm, ys_hbm, out_hbm, idx_v, rows_a, rows_b, sems):
        wid = lax.axis_index("s") * n_cores + lax.axis_index("c")
        bufs = (rows_a, rows_b)

        @pl.loop(0, steps)
        def _(step):
            blk = wid * steps + step
            pltpu.sync_copy(pos_hbm.at[blk], idx_v)
            gathers = [pltpu.make_async_copy(ys_hbm.at[idx_v.at[k]], bufs[k % 2], sems.at[k % 2])
                       for k in range(TOP_K)]
            gathers[0].start()
            for k in range(TOP_K):
                gathers[k].wait()
                if k + 1 < TOP_K:
                    gathers[k + 1].start()
                pltpu.sync_copy(bufs[k % 2], out_hbm.at[pl.ds(k * t + blk * chunk, chunk)])

    return gather_rows(pos_blocks, ys)


def _experts(blk_e, nblk_used, xs, w1, w3, w2):
    n_rows, half = xs.shape
    d = D_MODEL
    blk = lambda s, be, nb: (jnp.minimum(s, nb[0] - 1), 0)
    wblk = lambda s, be, nb: (be[jnp.minimum(s, nb[0] - 1)], 0, 0)
    grid_spec = pltpu.PrefetchScalarGridSpec(
        num_scalar_prefetch=2,
        grid=(n_rows // SLOT_ROWS,),
        in_specs=[pl.BlockSpec((SLOT_ROWS, half), blk),
                  pl.BlockSpec((1, d, EXPERT_FF), wblk),
                  pl.BlockSpec((1, d, EXPERT_FF), wblk),
                  pl.BlockSpec((1, EXPERT_FF, d), wblk)],
        out_specs=pl.BlockSpec((SLOT_ROWS, half), blk),
        scratch_shapes=[pltpu.VMEM((d, EXPERT_FF), BF16),
                        pltpu.VMEM((d, EXPERT_FF), BF16),
                        pltpu.VMEM((EXPERT_FF, d), BF16)],
    )
    return pl.pallas_call(
        _experts_kernel,
        grid_spec=grid_spec,
        out_shape=jax.ShapeDtypeStruct((n_rows, half), jnp.uint32),
        compiler_params=_cparams(("arbitrary",)),
        name="routed_experts",
    )(blk_e, nblk_used, xs, w1, w3, w2)


def _combine_kernel(*refs):
    y_refs = refs[:TOP_K]
    w_ref, h_ref, x1_ref, ws1_ref, ws3_ref, ws2_ref, gt2_ref, g_ref = refs[TOP_K:TOP_K + 8]
    o_ref = refs[-1]
    tm, d = x1_ref.shape
    half = d // 2
    h = h_ref[...]
    mid = (_silu(_dot(h, ws1_ref[...])) * _dot(h, ws3_ref[...])).astype(BF16)
    shared = _dot(mid, ws2_ref[...])
    w = w_ref[...]
    acc_lo = jnp.zeros((tm, half), F32)
    acc_hi = jnp.zeros((tm, half), F32)
    for k in range(TOP_K):
        lo, hi = _unpack_halves(y_refs[k][...])
        acc_lo = acc_lo + lo * w[:, k:k + 1]
        acc_hi = acc_hi + hi * w[:, k:k + 1]
    routed = jnp.concatenate([acc_lo, acc_hi], axis=1)
    x2 = x1_ref[...] + gt2_ref[0] * (routed + shared)
    ms = jnp.mean(x2 * x2, axis=-1, keepdims=True)
    o_ref[...] = x2 * lax.rsqrt(ms + NORM_EPS) * g_ref[...]


COMBINE_ROWS = 256
COMBINE_PARTS = 4


def _combine(y_kt, wts, h2, x1, ws1, ws3, ws2, gt2, g_final, seq, part, n_parts, prev_out):
    t, d = x1.shape
    tm = min(COMBINE_ROWS, seq)
    per_b = seq // tm
    tiles = t // tm // n_parts
    first = part * tiles
    row = lambda i: (first + i, 0)
    full = lambda i: (0, 0)
    y_specs = [pl.BlockSpec((tm, d // 2), functools.partial(lambda i, k: (k * tiles + i, 0), k=k))
               for k in range(TOP_K)]
    in_specs = y_specs + [
        pl.BlockSpec((tm, TOP_K), row),
        pl.BlockSpec((tm, d), row),
        pl.BlockSpec((tm, d), row),
        pl.BlockSpec((d, SHARED_FF), full),
        pl.BlockSpec((d, SHARED_FF), full),
        pl.BlockSpec((SHARED_FF, d), full),
        pl.BlockSpec((1, 1, d), lambda i: ((first + i) // per_b, 0, 0)),
        pl.BlockSpec((1, d), full)]
    args = [y_kt] * TOP_K + [wts, h2, x1, ws1, ws3, ws2, gt2, g_final]
    aliases = {}
    if prev_out is not None:
        in_specs.append(pl.BlockSpec(memory_space=pl.ANY))
        aliases = {len(args): 0}
        args.append(prev_out)
    return pl.pallas_call(
        _combine_kernel,
        grid=(tiles,),
        in_specs=in_specs,
        out_specs=pl.BlockSpec((tm, d), row),
        out_shape=jax.ShapeDtypeStruct((t, d), F32),
        input_output_aliases=aliases,
        compiler_params=_cparams(("parallel",)),
        name="combine_shared_final",
    )(*args)


def _slot_tables(cnt, t):
    ntiles = cnt.shape[1] // 128
    cnt_tile = cnt.reshape(N_EXPERTS, ntiles, 128)[:, :, 0].astype(jnp.int32)
    counts = jnp.sum(cnt_tile, axis=1)
    padded = (counts + SLOT_ROWS - 1) // SLOT_ROWS * SLOT_ROWS
    pstart = jnp.cumsum(padded) - padded
    tile_base = pstart[:, None] + jnp.cumsum(cnt_tile, axis=1) - cnt_tile
    n_blk = -(-(t * TOP_K) // SLOT_ROWS) + N_EXPERTS
    blk_end = jnp.cumsum(padded // SLOT_ROWS)
    blk_e = jnp.sum((blk_end[None, :] <= jnp.arange(n_blk)[:, None]).astype(jnp.int32), axis=1)
    blk_e = jnp.minimum(blk_e, N_EXPERTS - 1)
    return (blk_e, blk_end[-1:].astype(jnp.int32), pstart + counts, padded - counts,
            tile_base.T.astype(F32).reshape(ntiles, N_EXPERTS, 1), n_blk * SLOT_ROWS)


def _permute_in_cols(w_in):
    qa, ka, va, qr, kr, vr, gr, ga, gt = jnp.split(
        w_in, np.cumsum((A_WIDTH, A_WIDTH, A_WIDTH, R_QK_WIDTH, R_QK_WIDTH, R_V_WIDTH, R_V_WIDTH,
                         D_MODEL))[:].tolist(), axis=1)
    return jnp.concatenate([vr, gr, ga, gt, qa, ka, va, qr, kr], axis=1)


def kernel(x, c, w_ada, b_ada, g_mix, w_in, w_pa, w_pr, w_out, g_ffn, w_router, router_bias,
           w1, w3, w2, ws1, ws3, ws2, g_final):
    bsz, seq, d = x.shape
    t = bsz * seq
    depth = w_ada.shape[0]
    assert depth == 1, "the final norm is fused into the single layer's last kernel"
    slopes = jnp.exp2(-8.0 / A_HEADS * jnp.arange(1, A_HEADS + 1, dtype=F32))
    x2d = x.reshape(t, d)
    for l in range(depth):
        mod = _ada(c, w_ada[l], b_ada[l])
        sh1, sc1, gt1, sh2, sc2, gt2 = [m.reshape(bsz, 1, d) for m in jnp.split(mod, 6, axis=-1)]
        w_in_p = _permute_in_cols(w_in[l]).astype(BF16)
        proj = _inproj(x2d, g_mix[l].reshape(1, d), sc1, sh1, w_in_p, seq)
        proj3 = proj.reshape(bsz, seq, IN_COLS)
        k_aug, q_aug_t, v_aug_t = _moba_prep(proj3, slopes)
        ya = _moba_attn(k_aug, q_aug_t, v_aug_t).reshape(t, A_WIDTH)
        yr = _retention(proj3).reshape(t, R_V_WIDTH)
        x1, h2, h2p = _mix(ya, yr, proj, x2d, w_pa[l].astype(BF16), w_pr[l].astype(BF16),
                           w_out[l].astype(BF16), gt1, g_ffn[l].reshape(1, d), sc2, sh2, seq)
        eidx_t, wts_t, rank_t, cnt = _router(h2, w_router[l].T.astype(BF16),
                                             router_bias[l].reshape(N_EXPERTS, 1))
        blk_e, nblk_used, pad_start, pad_len, tile_base, n_rows = _slot_tables(cnt, t)
        pos3 = _positions(eidx_t, rank_t, tile_base)
        pos_blocks = jnp.transpose(
            pos3.reshape(pos3.shape[0], TOP_K, -1, SC_DISPATCH_TOKENS), (0, 2, 1, 3)
        ).reshape(t // SC_DISPATCH_TOKENS, TOP_K, SC_DISPATCH_TOKENS)
        xs = _zero_pads(pad_start, pad_len, _sc_dispatch(pos_blocks, h2p, n_rows))
        ys = _experts(blk_e, nblk_used, xs, w1[l], w3[l], w2[l])
        wts = wts_t.T
        shared_w = (ws1[l].astype(BF16), ws3[l].astype(BF16), ws2[l].astype(BF16))
        blocks_per_part = pos_blocks.shape[0] // COMBINE_PARTS
        x2d = None
        for part in range(COMBINE_PARTS):
            y_kt = _sc_gather(pos_blocks[part * blocks_per_part:(part + 1) * blocks_per_part], ys,
                              t // COMBINE_PARTS)
            x2d = _combine(y_kt, wts, h2, x1, *shared_w, gt2, g_final.reshape(1, d), seq,
                           part, COMBINE_PARTS, x2d)
    return x2d.reshape(bsz, seq, d)
```

```python
import functools

import jax
import jax.numpy as jnp
import numpy as np
from jax import lax
from jax.experimental import pallas as pl
from jax.experimental.pallas import tpu as pltpu
from jax.experimental.pallas import tpu_sc as plsc

F32 = jnp.float32
BF16 = jnp.bfloat16

D_MODEL = 1024
A_HEADS = 8
A_HEAD_DIM = 64
A_WIDTH = A_HEADS * A_HEAD_DIM
MOBA_BLOCK = 256
MOBA_TOPK = 3
R_HEADS = 8
R_QK_DIM = 64
R_V_DIM = 128
R_QK_WIDTH = R_HEADS * R_QK_DIM
R_V_WIDTH = R_HEADS * R_V_DIM
R_CHUNK = 128
N_EXPERTS = 256
TOP_K = 8
N_GROUPS = 8
GROUP_SIZE = N_EXPERTS // N_GROUPS
TOPK_GROUPS = 4
EXPERT_FF = 256
SHARED_FF = 256
ROUTED_SCALE = 2.5
NORM_EPS = 1e-6
GN_EPS = 1e-6
NEG_INF = -1e30
KNOCKED_OUT = -3e38

COL_VR, COL_GR, COL_GA, COL_GT = 0, 1024, 2048, 3072
COL_QA, COL_KA, COL_VA, COL_QR, COL_KR = 4096, 4608, 5120, 5632, 6144
IN_COLS = 6656
AUG = 128
FEAT_BIAS = A_HEAD_DIM
FEAT_POS = A_HEAD_DIM + 32
V_ROWS = A_HEAD_DIM + 16
MOBA_HEADS_PER_STEP = 8

VMEM_LIMIT = 56 * 1024 * 1024


def _cparams(sem, vmem=VMEM_LIMIT):
    return pltpu.CompilerParams(dimension_semantics=sem, vmem_limit_bytes=vmem)


def _dot(a, b):
    return jnp.dot(a, b, preferred_element_type=F32)


def _dot_nt(a, b):
    return lax.dot_general(a, b, (((1,), (1,)), ((), ())), preferred_element_type=F32)


def _sigmoid(x):
    return 1.0 / (1.0 + jnp.exp(-x))


def _silu(x):
    return x * _sigmoid(x)


def _ada_kernel(c_ref, w_ref, b_ref, o_ref):
    c = c_ref[...]
    s = _silu(c)
    s_hi = s.astype(BF16)
    s_lo = (s - s_hi.astype(F32)).astype(BF16)
    w = w_ref[...]
    w_hi = w.astype(BF16)
    w_lo = (w - w_hi.astype(F32)).astype(BF16)
    o_ref[...] = _dot(s_hi, w_hi) + _dot(s_hi, w_lo) + _dot(s_lo, w_hi) + b_ref[...]


def _ada(c, w_ada, b_ada):
    bsz, d = c.shape
    n = w_ada.shape[1]
    tn = 1024
    return pl.pallas_call(
        _ada_kernel,
        grid=(n // tn,),
        in_specs=[pl.BlockSpec((bsz, d), lambda j: (0, 0)),
                  pl.BlockSpec((d, tn), lambda j: (0, j)),
                  pl.BlockSpec((1, tn), lambda j: (0, j))],
        out_specs=pl.BlockSpec((bsz, tn), lambda j: (0, j)),
        out_shape=jax.ShapeDtypeStruct((bsz, n), F32),
        compiler_params=_cparams(("parallel",)),
        name="ada_mod",
    )(c, w_ada, b_ada.reshape(1, n))


INPROJ_COLS = 512


def _inproj_kernel(x_ref, g_ref, sc_ref, sh_ref, w_ref, o_ref):
    x = x_ref[...]
    ms = jnp.mean(x * x, axis=-1, keepdims=True)
    y = x * lax.rsqrt(ms + NORM_EPS) * g_ref[...]
    h = (y * (1.0 + sc_ref[0]) + sh_ref[0]).astype(BF16)
    for j in range(w_ref.shape[1] // INPROJ_COLS):
        cols = slice(j * INPROJ_COLS, (j + 1) * INPROJ_COLS)
        o_ref[:, cols] = _dot(h, w_ref[:, cols]).astype(BF16)


def _inproj(x2d, g, sc, sh, w_bf16, seq):
    t, d = x2d.shape
    n = w_bf16.shape[1]
    tm = min(512, seq)
    per_b = seq // tm
    return pl.pallas_call(
        _inproj_kernel,
        grid=(t // tm,),
        in_specs=[pl.BlockSpec((tm, d), lambda i: (i, 0)),
                  pl.BlockSpec((1, d), lambda i: (0, 0)),
                  pl.BlockSpec((1, 1, d), lambda i: (i // per_b, 0, 0)),
                  pl.BlockSpec((1, 1, d), lambda i: (i // per_b, 0, 0)),
                  pl.BlockSpec((d, n), lambda i: (0, 0))],
        out_specs=pl.BlockSpec((tm, n), lambda i: (i, 0)),
        out_shape=jax.ShapeDtypeStruct((t, n), BF16),
        compiler_params=_cparams(("parallel",)),
        name="norm_inproj",
    )(x2d, g, sc, sh, w_bf16)


def _moba_prep_kernel(slopes_ref, q_ref, k_ref, v_ref, ko_ref, qo_ref, vo_ref, kmean_scr):
    i = pl.program_id(1)
    nblk = kmean_scr.shape[0]
    width = q_ref.shape[1]

    @pl.when(i == 0)
    def _():
        kmean_scr[...] = jnp.zeros_like(kmean_scr)

    q = q_ref[...]
    k = k_ref[...]
    v = v_ref[...]
    kmean_scr[pl.ds(i, 1), :] = jnp.mean(k.astype(F32), axis=0, keepdims=True)

    eye = (lax.broadcasted_iota(jnp.int32, (width, width), 0)
           == lax.broadcasted_iota(jnp.int32, (width, width), 1)).astype(BF16)
    q_t = _dot_nt(eye, q)
    v_t = _dot_nt(eye, v)

    km = kmean_scr[...]
    km_rep = jnp.concatenate([km] * A_HEADS, axis=0)
    r_head = lax.broadcasted_iota(jnp.int32, km_rep.shape, 0) // nblk
    c_head = lax.broadcasted_iota(jnp.int32, km_rep.shape, 1) // A_HEAD_DIM
    km_bd = jnp.where(r_head == c_head, km_rep, 0.0)
    km_hi = km_bd.astype(BF16)
    km_lo = (km_bd - km_hi.astype(F32)).astype(BF16)
    q_t_b = q_t.astype(BF16)
    gate_all = _dot(km_hi, q_t_b) + _dot(km_lo, q_t_b)

    mb = q.shape[0]
    blk = lax.broadcasted_iota(jnp.int32, (nblk, mb), 0)
    lane_pos = lax.broadcasted_iota(jnp.int32, (16, mb), 1).astype(F32)
    row16 = lax.broadcasted_iota(jnp.int32, (16, mb), 0)
    key_pos = lax.broadcasted_iota(jnp.int32, (mb, AUG), 0).astype(F32)
    kcol = lax.broadcasted_iota(jnp.int32, (mb, AUG), 1)
    sel_r = lax.broadcasted_iota(jnp.int32, (width, AUG), 0)
    sel_c = lax.broadcasted_iota(jnp.int32, (width, AUG), 1)

    for h in range(A_HEADS):
        slope = slopes_ref[h]
        g = jnp.where(blk < i, gate_all[h * nblk:(h + 1) * nblk, :], NEG_INF)
        sel = jnp.zeros((nblk, mb), jnp.bool_)
        for r in range(MOBA_TOPK):
            m = jnp.max(g, axis=0, keepdims=True)
            idx = jnp.min(jnp.where(g == m, blk, nblk), axis=0, keepdims=True)
            hit = blk == idx
            sel = jnp.logical_or(sel, jnp.logical_and(hit, r < i))
            g = jnp.where(hit, KNOCKED_OUT, g)
        bias_t = jnp.where(sel, 0.0, NEG_INF)

        scale = A_HEAD_DIM ** -0.5
        qo_ref[0, h, 0:A_HEAD_DIM, :] = (q_t[h * A_HEAD_DIM:(h + 1) * A_HEAD_DIM, :] * scale).astype(BF16)
        qo_ref[0, h, FEAT_BIAS:FEAT_BIAS + nblk, :] = bias_t.astype(BF16)
        if nblk < 32:
            qo_ref[0, h, FEAT_BIAS + nblk:FEAT_POS, :] = jnp.zeros((32 - nblk, mb), BF16)
        blk_off = slope * (i * mb).astype(F32)
        pos_feat = jnp.where(row16 == 0, -slope * lane_pos,
                             jnp.where(row16 == 2, -blk_off,
                                       jnp.where(jnp.logical_or(row16 == 1, row16 == 3), 1.0, 0.0)))
        qo_ref[0, h, FEAT_POS:FEAT_POS + 16, :] = pos_feat.astype(BF16)
        qo_ref[0, h, FEAT_POS + 16:AUG, :] = jnp.zeros((AUG - FEAT_POS - 16, mb), BF16)

        vo_ref[0, h, 0:A_HEAD_DIM, :] = v_t[h * A_HEAD_DIM:(h + 1) * A_HEAD_DIM, :].astype(BF16)
        vo_ref[0, h, A_HEAD_DIM:V_ROWS, :] = jnp.where(row16 == 0, 1.0, 0.0).astype(BF16)

        pick = jnp.where(jnp.logical_and(sel_r == sel_c + h * A_HEAD_DIM, sel_c < A_HEAD_DIM),
                         1.0, 0.0).astype(BF16)
        k_feat = jnp.where(
            jnp.logical_or(kcol == FEAT_BIAS + i, jnp.logical_or(kcol == FEAT_POS, kcol == FEAT_POS + 2)), 1.0,
            jnp.where(kcol == FEAT_POS + 1, slope * key_pos, jnp.where(kcol == FEAT_POS + 3, blk_off, 0.0)))
        ko_ref[0, h, :, :] = (_dot(k, pick) + k_feat).astype(BF16)


def _moba_prep(proj3, slopes):
    bsz, seq, _ = proj3.shape
    nblk = seq // MOBA_BLOCK
    mb = MOBA_BLOCK
    grid_spec = pltpu.PrefetchScalarGridSpec(
        num_scalar_prefetch=1,
        grid=(bsz, nblk),
        in_specs=[pl.BlockSpec((None, mb, A_WIDTH), lambda b, i, s: (b, i, COL_QA // A_WIDTH)),
                  pl.BlockSpec((None, mb, A_WIDTH), lambda b, i, s: (b, i, COL_KA // A_WIDTH)),
                  pl.BlockSpec((None, mb, A_WIDTH), lambda b, i, s: (b, i, COL_VA // A_WIDTH))],
        out_specs=[pl.BlockSpec((1, A_HEADS, mb, AUG), lambda b, i, s: (b, 0, i, 0)),
                   pl.BlockSpec((1, A_HEADS, AUG, mb), lambda b, i, s: (b, 0, 0, i)),
                   pl.BlockSpec((1, A_HEADS, V_ROWS, mb), lambda b, i, s: (b, 0, 0, i))],
        scratch_shapes=[pltpu.VMEM((nblk, A_WIDTH), F32)],
    )
    return pl.pallas_call(
        _moba_prep_kernel,
        grid_spec=grid_spec,
        out_shape=[jax.ShapeDtypeStruct((bsz, A_HEADS, seq, AUG), BF16),
                   jax.ShapeDtypeStruct((bsz, A_HEADS, AUG, seq), BF16),
                   jax.ShapeDtypeStruct((bsz, A_HEADS, V_ROWS, seq), BF16)],
        compiler_params=_cparams(("parallel", "arbitrary")),
        name="moba_prep",
    )(slopes, proj3, proj3, proj3)


def _moba_attn_kernel(q_ref, k_ref, v_ref, o_ref, s_a, s_b, *, group, n_groups):
    i = pl.program_id(2)
    mb = MOBA_BLOCK
    span = group * mb
    own = pl.multiple_of(i * mb, mb)
    key_i = lax.broadcasted_iota(jnp.int32, (mb, mb), 0)
    qry_i = lax.broadcasted_iota(jnp.int32, (mb, mb), 1)
    feat = lax.broadcasted_iota(jnp.int32, (AUG, mb), 0)
    is_bias = jnp.logical_and(feat >= FEAT_BIAS, feat < FEAT_POS)
    q_ts, carry0 = [], []
    for hh in range(MOBA_HEADS_PER_STEP):
        q_t = q_ref[0, hh]
        q_ts.append(q_t)
        q_own = jnp.where(is_bias, jnp.zeros_like(q_t), q_t)
        s = _dot(k_ref[0, hh, pl.ds(own, mb), :], q_own)
        s = jnp.where(key_i <= qry_i, s, NEG_INF)
        m0 = jnp.max(s, axis=0, keepdims=True)
        p = jnp.exp(s - m0)
        carry0 += [m0, _dot(v_ref[0, hh, :, pl.ds(own, mb)], p.astype(BF16))]

    def scores(g, dst):
        start = pl.multiple_of(jnp.minimum(g, n_groups - 1) * span, span)
        for hh in range(MOBA_HEADS_PER_STEP):
            dst[hh] = _dot(k_ref[0, hh, pl.ds(start, span), :], q_ts[hh])

    def absorb(g, src, carry):
        start = pl.multiple_of(g * span, span)
        new = []
        for hh in range(MOBA_HEADS_PER_STEP):
            m, acc = carry[2 * hh], carry[2 * hh + 1]
            sb = src[hh]
            m_new = jnp.maximum(m, jnp.max(sb, axis=0, keepdims=True))
            pb = jnp.exp(sb - m_new)
            alpha = jnp.exp(m - m_new)
            acc = acc * alpha + _dot(v_ref[0, hh, :, pl.ds(start, span)], pb.astype(BF16))
            new += [m_new, acc]
        return tuple(new)

    def body(pair, carry):
        scores(2 * pair + 1, s_b)
        carry = absorb(2 * pair, s_a, carry)
        scores(2 * pair + 2, s_a)
        return absorb(2 * pair + 1, s_b, carry)

    scores(0, s_a)
    live_groups = (i + group - 1) // group
    res = lax.fori_loop(0, (live_groups + 1) // 2, body, tuple(carry0))
    outs = [res[2 * hh + 1][0:A_HEAD_DIM, :] / res[2 * hh + 1][A_HEAD_DIM:A_HEAD_DIM + 1, :] for hh in range(MOBA_HEADS_PER_STEP)]
    o_t = jnp.concatenate(outs, axis=0).astype(BF16)
    eye = (key_i == qry_i).astype(BF16)
    o_ref[0] = _dot_nt(eye, o_t).astype(BF16)


def _moba_attn(k_aug, q_aug_t, v_aug_t):
    bsz, nh, seq, _ = k_aug.shape
    mb = MOBA_BLOCK
    group = min(2, seq // mb)
    n_groups = seq // (group * mb)
    hps = MOBA_HEADS_PER_STEP
    return pl.pallas_call(
        functools.partial(_moba_attn_kernel, group=group, n_groups=n_groups),
        grid=(bsz, nh // hps, seq // mb),
        in_specs=[pl.BlockSpec((1, hps, AUG, mb), lambda b, h, i: (b, h, 0, i)),
                  pl.BlockSpec((1, hps, seq, AUG), lambda b, h, i: (b, h, 0, 0), pipeline_mode=pl.Buffered(1)),
                  pl.BlockSpec((1, hps, V_ROWS, seq), lambda b, h, i: (b, h, 0, 0), pipeline_mode=pl.Buffered(1))],
        out_specs=pl.BlockSpec((1, mb, hps * A_HEAD_DIM), lambda b, h, i: (b, i, h)),
        out_shape=jax.ShapeDtypeStruct((bsz, seq, A_WIDTH), BF16),
        scratch_shapes=[pltpu.VMEM((hps, group * mb, mb), F32), pltpu.VMEM((hps, group * mb, mb), F32)],
        compiler_params=_cparams(("parallel", "parallel", "arbitrary")),
        name="moba_attn",
    )(q_aug_t, k_aug, v_aug_t)


def _retention_kernel(cdec_ref, q_ref, k_ref, v_ref, g_ref, decay_ref, qdec_ref, kdec_ref, o_ref, state_scr):
    @pl.when(pl.program_id(0) == 0)
    def _():
        state_scr[...] = jnp.zeros_like(state_scr)

    width = q_ref.shape[2]
    eye = (lax.broadcasted_iota(jnp.int32, (width, width), 0)
           == lax.broadcasted_iota(jnp.int32, (width, width), 1)).astype(BF16)
    for b in range(q_ref.shape[0]):
        q = q_ref[b]
        k_t = _dot_nt(eye, k_ref[b])
        k_t_b = k_t.astype(BF16)
        state_b = state_scr[b].astype(BF16)
        for h in range(R_HEADS):
            rows = slice(h * R_QK_DIM, (h + 1) * R_QK_DIM)
            cols = slice(h * R_V_DIM, (h + 1) * R_V_DIM)
            q_h = q[:, rows]
            v_h = v_ref[b, :, cols]
            inner = _dot(q_h, k_t_b[rows, :]) * decay_ref[h]
            out = _dot(inner.astype(BF16), v_h) + _dot(q_h, state_b[rows, :]) * qdec_ref[h]
            k_dec = (k_t[rows, :] * kdec_ref[h]).astype(BF16)
            state_scr[b, rows, :] = cdec_ref[h] * state_scr[b, rows, :] + _dot(k_dec, v_h)
            mu = jnp.mean(out, axis=-1, keepdims=True)
            cen = out - mu
            var = jnp.mean(cen * cen, axis=-1, keepdims=True)
            y = cen * lax.rsqrt(var + GN_EPS)
            o_ref[b, :, cols] = (y * _silu(g_ref[b, :, cols].astype(F32))).astype(BF16)


def _retention_consts():
    h = np.arange(R_HEADS, dtype=np.float64)
    log_g = np.log(1.0 - np.exp2(-5.0 - h))
    n = np.arange(R_CHUNK, dtype=np.float64)
    diff = n[:, None] - n[None, :]
    scale = R_QK_DIM ** -0.5
    decay = np.where(diff >= 0, np.exp(np.maximum(diff, 0.0) * log_g[:, None, None]), 0.0) * scale
    q_decay = np.exp((n + 1.0) * log_g[:, None])[:, :, None]
    k_decay = np.exp((R_CHUNK - 1.0 - n) * log_g[:, None])[:, None, :] * scale
    chunk_decay = np.exp(R_CHUNK * log_g)
    return (jnp.asarray(decay, F32), jnp.asarray(q_decay, F32), jnp.asarray(k_decay, F32),
            jnp.asarray(chunk_decay, F32))


def _retention(proj3):
    bsz, seq, _ = proj3.shape
    c = R_CHUNK
    decay, qdec, kdec, cdec = _retention_consts()
    grid_spec = pltpu.PrefetchScalarGridSpec(
        num_scalar_prefetch=1,
        grid=(seq // c,),
        in_specs=[pl.BlockSpec((bsz, c, R_QK_WIDTH), lambda i, s: (0, i, COL_QR // R_QK_WIDTH)),
                  pl.BlockSpec((bsz, c, R_QK_WIDTH), lambda i, s: (0, i, COL_KR // R_QK_WIDTH)),
                  pl.BlockSpec((bsz, c, R_V_WIDTH), lambda i, s: (0, i, COL_VR // R_V_WIDTH)),
                  pl.BlockSpec((bsz, c, R_V_WIDTH), lambda i, s: (0, i, COL_GR // R_V_WIDTH)),
                  pl.BlockSpec((R_HEADS, c, c), lambda i, s: (0, 0, 0)),
                  pl.BlockSpec((R_HEADS, c, 1), lambda i, s: (0, 0, 0)),
                  pl.BlockSpec((R_HEADS, 1, c), lambda i, s: (0, 0, 0))],
        out_specs=pl.BlockSpec((bsz, c, R_V_WIDTH), lambda i, s: (0, i, 0)),
        scratch_shapes=[pltpu.VMEM((bsz, R_QK_WIDTH, R_V_DIM), F32)],
    )
    return pl.pallas_call(
        _retention_kernel,
        grid_spec=grid_spec,
        out_shape=jax.ShapeDtypeStruct((bsz, seq, R_V_WIDTH), BF16),
        compiler_params=_cparams(("arbitrary",)),
        name="retention",
    )(cdec, proj3, proj3, proj3, proj3, decay, qdec, kdec)


def _pack_halves(x):
    w = x.shape[1] // 2
    bits = lax.bitcast_convert_type(x.astype(BF16).astype(F32), jnp.uint32)
    return (bits[:, :w] >> 16) | (bits[:, w:] & jnp.uint32(0xFFFF0000))


def _unpack_halves(p):
    lo = lax.bitcast_convert_type(p << 16, F32)
    hi = lax.bitcast_convert_type(p & jnp.uint32(0xFFFF0000), F32)
    return lo, hi


SUBLANES = 8


def _mix_kernel(ya_ref, yr_ref, ga_ref, gt_ref, x_ref, wpa_ref, wpr_ref, wout_ref,
                gt1_ref, g_ref, sc_ref, sh_ref, x1_ref, h2_ref, h2p_ref):
    a = _dot(ya_ref[...], wpa_ref[...]) * _sigmoid(ga_ref[...].astype(F32))
    r = _dot(yr_ref[...], wpr_ref[...]) * _sigmoid(gt_ref[...].astype(F32))
    mix = _dot((a + r).astype(BF16), wout_ref[...])
    x1 = x_ref[...] + gt1_ref[0] * mix
    x1_ref[...] = x1
    ms = jnp.mean(x1 * x1, axis=-1, keepdims=True)
    y = x1 * lax.rsqrt(ms + NORM_EPS) * g_ref[...]
    h2 = y * (1.0 + sc_ref[0]) + sh_ref[0]
    h2_ref[...] = h2.astype(BF16)
    h2p_ref[...] = _pack_halves(h2)


def _mix(ya, yr, proj, x2d, wpa, wpr, wout, gt1, g, sc, sh, seq):
    t, d = x2d.shape
    tm = min(512, seq)
    per_b = seq // tm
    row = lambda i: (i, 0)
    full = lambda i: (0, 0)
    per_batch = lambda i: (i // per_b, 0, 0)
    return pl.pallas_call(
        _mix_kernel,
        grid=(t // tm,),
        in_specs=[pl.BlockSpec((tm, A_WIDTH), row),
                  pl.BlockSpec((tm, R_V_WIDTH), row),
                  pl.BlockSpec((tm, d), lambda i: (i, COL_GA // D_MODEL)),
                  pl.BlockSpec((tm, d), lambda i: (i, COL_GT // D_MODEL)),
                  pl.BlockSpec((tm, d), row),
                  pl.BlockSpec((A_WIDTH, d), full),
                  pl.BlockSpec((R_V_WIDTH, d), full),
                  pl.BlockSpec((d, d), full),
                  pl.BlockSpec((1, 1, d), per_batch),
                  pl.BlockSpec((1, d), full),
                  pl.BlockSpec((1, 1, d), per_batch),
                  pl.BlockSpec((1, 1, d), per_batch)],
        out_specs=[pl.BlockSpec((tm, d), row), pl.BlockSpec((tm, d), row),
                   pl.BlockSpec((tm, d // 2), row)],
        out_shape=[jax.ShapeDtypeStruct((t, d), F32), jax.ShapeDtypeStruct((t, d), BF16),
                   jax.ShapeDtypeStruct((t, d // 2), jnp.uint32)],
        compiler_params=_cparams(("parallel",)),
        name="merge_outproj_norm",
    )(ya, yr, proj, proj, x2d, wpa, wpr, wout, gt1, g, sc, sh)


def _router_kernel(h_ref, wr_ref, b_ref, e_ref, w_ref, r_ref, c_ref):
    logits = _dot_nt(wr_ref[...], h_ref[...])
    scores = _sigmoid(logits)
    choice = scores + b_ref[...]
    tm = logits.shape[1]
    giota = lax.broadcasted_iota(jnp.int32, (GROUP_SIZE, tm), 0)
    gs_rows = []
    for g in range(N_GROUPS):
        cg = choice[g * GROUP_SIZE:(g + 1) * GROUP_SIZE, :]
        m1 = jnp.max(cg, axis=0, keepdims=True)
        i1 = jnp.min(jnp.where(cg == m1, giota, GROUP_SIZE), axis=0, keepdims=True)
        m2 = jnp.max(jnp.where(giota == i1, KNOCKED_OUT, cg), axis=0, keepdims=True)
        gs_rows.append(m1 + m2)
    gs = jnp.concatenate(gs_rows, axis=0)
    grow = lax.broadcasted_iota(jnp.int32, (N_GROUPS, tm), 0)
    gmask = jnp.zeros((N_GROUPS, tm), jnp.bool_)
    for _ in range(TOPK_GROUPS):
        mx = jnp.max(gs, axis=0, keepdims=True)
        ix = jnp.min(jnp.where(gs == mx, grow, N_GROUPS), axis=0, keepdims=True)
        hit = grow == ix
        gmask = jnp.logical_or(gmask, hit)
        gs = jnp.where(hit, KNOCKED_OUT, gs)
    gmask_f = jnp.where(gmask, 1.0, 0.0)
    masked = jnp.concatenate(
        [jnp.where(gmask_f[g:g + 1, :] > 0.5, choice[g * GROUP_SIZE:(g + 1) * GROUP_SIZE, :], NEG_INF)
         for g in range(N_GROUPS)], axis=0)
    erow = lax.broadcasted_iota(jnp.int32, (N_EXPERTS, tm), 0)
    idx_rows, w_rows = [], []
    chosen = jnp.zeros((N_EXPERTS, tm), F32)
    for _ in range(TOP_K):
        mx = jnp.max(masked, axis=0, keepdims=True)
        ix = jnp.min(jnp.where(masked == mx, erow, N_EXPERTS), axis=0, keepdims=True)
        hit = erow == ix
        w_rows.append(jnp.sum(jnp.where(hit, scores, 0.0), axis=0, keepdims=True))
        idx_rows.append(ix)
        chosen = jnp.where(hit, 1.0, chosen)
        masked = jnp.where(hit, KNOCKED_OUT, masked)
    w = jnp.concatenate(w_rows, axis=0)
    w = w / (jnp.sum(w, axis=0, keepdims=True) + 1e-20) * ROUTED_SCALE
    e_ref[...] = jnp.concatenate(idx_rows, axis=0)
    w_ref[...] = w
    chosen_b = chosen.astype(BF16)
    earlier = (lax.broadcasted_iota(jnp.int32, (tm, tm), 0)
               < lax.broadcasted_iota(jnp.int32, (tm, tm), 1)).astype(BF16)
    before = _dot(chosen_b, earlier)
    ranks = [jnp.sum(jnp.where(erow == ix, before, 0.0), axis=0, keepdims=True) for ix in idx_rows]
    r_ref[...] = jnp.concatenate(ranks, axis=0).astype(jnp.int32)
    c_ref[...] = _dot(chosen_b, jnp.ones((tm, 128), BF16))


ROUTER_ROWS = 512


def _router(h2, wr_t, bias_col):
    t, d = h2.shape
    tm = ROUTER_ROWS
    by_tile = lambda i: (0, i)
    return pl.pallas_call(
        _router_kernel,
        grid=(t // tm,),
        in_specs=[pl.BlockSpec((tm, d), lambda i: (i, 0)),
                  pl.BlockSpec((N_EXPERTS, d), lambda i: (0, 0)),
                  pl.BlockSpec((N_EXPERTS, 1), lambda i: (0, 0))],
        out_specs=[pl.BlockSpec((TOP_K, tm), by_tile), pl.BlockSpec((TOP_K, tm), by_tile),
                   pl.BlockSpec((TOP_K, tm), by_tile), pl.BlockSpec((N_EXPERTS, 128), by_tile)],
        out_shape=[jax.ShapeDtypeStruct((TOP_K, t), jnp.int32),
                   jax.ShapeDtypeStruct((TOP_K, t), F32),
                   jax.ShapeDtypeStruct((TOP_K, t), jnp.int32),
                   jax.ShapeDtypeStruct((N_EXPERTS, (t // tm) * 128), F32)],
        compiler_params=_cparams(("parallel",)),
        name="router_topk",
    )(h2, wr_t, bias_col)


def _pos_kernel(e_ref, r_ref, base_ref, p_ref):
    tm = e_ref.shape[1]
    erow = lax.broadcasted_iota(jnp.int32, (N_EXPERTS, tm), 0)
    base = base_ref[0]
    rows = [jnp.sum(jnp.where(erow == e_ref[k:k + 1, :], base, 0.0), axis=0, keepdims=True)
            for k in range(TOP_K)]
    p_ref[0] = jnp.concatenate(rows, axis=0).astype(jnp.int32) + r_ref[...]


MOVE_ROWS = 512


def _positions(eidx_t, rank_t, tile_base):
    t = eidx_t.shape[1]
    tm = min(MOVE_ROWS, t)
    per_router_tile = ROUTER_ROWS // tm
    return pl.pallas_call(
        _pos_kernel,
        grid=(t // tm,),
        in_specs=[pl.BlockSpec((TOP_K, tm), lambda i: (0, i)),
                  pl.BlockSpec((TOP_K, tm), lambda i: (0, i)),
                  pl.BlockSpec((1, N_EXPERTS, 1), lambda i: (i // per_router_tile, 0, 0))],
        out_specs=pl.BlockSpec((1, TOP_K, tm), lambda i: (i, 0, 0)),
        out_shape=jax.ShapeDtypeStruct((t // tm, TOP_K, tm), jnp.int32),
        compiler_params=_cparams(("parallel",)),
        name="slot_positions",
    )(eidx_t, rank_t, tile_base)


SLOT_ROWS = 512
PAD_CHUNKS = (256, 128, 64, 32, 16, 8)


def _zero_pads_kernel(pad_start_ref, pad_len_ref, xs_in, xs_hbm, zero_buf, pad_sem):
    del xs_in
    zero_buf[...] = jnp.zeros_like(zero_buf)

    def pad_copies(e, wait):
        start = pad_start_ref[e]
        n = pad_len_ref[e]
        head = jnp.minimum((-start) & (SUBLANES - 1), n)

        def fill(first, size, pred):
            @pl.when(pred)
            def _():
                cp = pltpu.make_async_copy(zero_buf.at[pl.ds(0, size), :], xs_hbm.at[pl.ds(first, size), :], pad_sem)
                if wait:
                    cp.wait()
                else:
                    cp.start()

        for j in range(SUBLANES - 1):
            fill(start + j, 1, j < head)
        ptr = start + head
        rest = n - head
        for chunk in PAD_CHUNKS:
            fill(pl.multiple_of(ptr, SUBLANES), chunk, (rest & chunk) != 0)
            ptr = ptr + (rest & chunk)

    def issue(e, carry):
        pad_copies(e, False)
        return carry

    def drain(e, carry):
        pad_copies(e, True)
        return carry

    lax.fori_loop(0, N_EXPERTS, issue, 0)
    lax.fori_loop(0, N_EXPERTS, drain, 0)


def _zero_pads(pad_start, pad_len, xs):
    grid_spec = pltpu.PrefetchScalarGridSpec(
        num_scalar_prefetch=2,
        grid=(1,),
        in_specs=[pl.BlockSpec(memory_space=pl.ANY)],
        out_specs=pl.BlockSpec(memory_space=pl.ANY),
        scratch_shapes=[pltpu.VMEM((PAD_CHUNKS[0], xs.shape[1]), xs.dtype),
                        pltpu.SemaphoreType.DMA],
    )
    return pl.pallas_call(
        _zero_pads_kernel,
        grid_spec=grid_spec,
        out_shape=jax.ShapeDtypeStruct(xs.shape, xs.dtype),
        input_output_aliases={2: 0},
        compiler_params=_cparams(("arbitrary",)),
        name="zero_pad_slots",
    )(pad_start, pad_len, xs)


SC_DISPATCH_TOKENS = 64


def _sc_dispatch(pos_blocks, h_rows, n_rows):
    info = plsc.get_sparse_core_info()
    n_cores = info.num_cores
    n_workers = n_cores * info.num_subcores
    t = h_rows.shape[0]
    chunk = SC_DISPATCH_TOKENS
    steps = t // (n_workers * chunk)
    mesh = plsc.VectorSubcoreMesh(core_axis_name="c", subcore_axis_name="s")

    @functools.partial(
        pl.kernel, mesh=mesh,
        out_type=jax.ShapeDtypeStruct((n_rows,) + h_rows.shape[1:], h_rows.dtype),
        scratch_types=[pltpu.VMEM((TOP_K, chunk), jnp.int32),
                       pltpu.VMEM((chunk,) + h_rows.shape[1:], h_rows.dtype),
                       pltpu.SemaphoreType.DMA],
        name="sc_dispatch_rows",
    )
    def scatter_rows(pos_hbm, h_hbm, out_hbm, idx_v, rows_v, sem):
        wid = lax.axis_index("s") * n_cores + lax.axis_index("c")

        @pl.loop(0, steps)
        def _(step):
            blk = wid * steps + step
            pltpu.sync_copy(pos_hbm.at[blk], idx_v)
            pltpu.sync_copy(h_hbm.at[pl.ds(blk * chunk, chunk)], rows_v)
            scatters = [pltpu.make_async_copy(rows_v, out_hbm.at[idx_v.at[k]], sem) for k in range(TOP_K)]
            for cp in scatters:
                cp.start()
            for cp in scatters:
                cp.wait()

    return scatter_rows(pos_blocks, h_rows)


WEIGHT_SLOTS = 3


def _experts_kernel(blk_e_ref, nblk_ref, ord_ref, eid_ref, nord_ref, x_ref, w1_hbm, w3_hbm, w2_hbm, y_ref,
                    w1f, w3f, w2f, w1b, w3b, w2b, wsem):
    s = pl.program_id(0)

    def fetch(j):
        slot = j % WEIGHT_SLOTS
        e = eid_ref[j]
        return [pltpu.make_async_copy(src.at[e], dst.at[slot], wsem.at[slot])
                for src, dst in ((w1_hbm, w1f), (w3_hbm, w3f), (w2_hbm, w2f))]

    @pl.when(s < nblk_ref[0])
    def _():
        j = ord_ref[s]

        @pl.when(s == 0)
        def _():
            for cp in fetch(0):
                cp.start()

            @pl.when(nord_ref[0] > 1)
            def _():
                for cp in fetch(1):
                    cp.start()

        @pl.when(jnp.logical_or(s == 0, blk_e_ref[s] != blk_e_ref[jnp.maximum(s - 1, 0)]))
        def _():
            for cp in fetch(j):
                cp.wait()

            @pl.when(j + 2 < nord_ref[0])
            def _():
                for cp in fetch(j + 2):
                    cp.start()

            slot = j % WEIGHT_SLOTS
            w1b[...] = w1f[slot].astype(BF16)
            w3b[...] = w3f[slot].astype(BF16)
            w2b[...] = w2f[slot].astype(BF16)

        half = x_ref.shape[1]
        lo, hi = _unpack_halves(x_ref[...])
        lo = lo.astype(BF16)
        hi = hi.astype(BF16)
        h1 = _dot(lo, w1b[:half, :]) + _dot(hi, w1b[half:, :])
        h3 = _dot(lo, w3b[:half, :]) + _dot(hi, w3b[half:, :])
        mid = (_silu(h1) * h3).astype(BF16)
        y_ref[...] = _pack_halves(_dot(mid, w2b[...]))


def _sc_gather(pos_blocks, ys, t):
    info = plsc.get_sparse_core_info()
    n_cores = info.num_cores
    n_workers = n_cores * info.num_subcores
    chunk = SC_DISPATCH_TOKENS
    steps = t // (n_workers * chunk)
    mesh = plsc.VectorSubcoreMesh(core_axis_name="c", subcore_axis_name="s")

    @functools.partial(
        pl.kernel, mesh=mesh,
        out_type=jax.ShapeDtypeStruct((TOP_K * t,) + ys.shape[1:], ys.dtype),
        scratch_types=[pltpu.VMEM((TOP_K, chunk), jnp.int32),
                       pltpu.VMEM((chunk,) + ys.shape[1:], ys.dtype),
                       pltpu.VMEM((chunk,) + ys.shape[1:], ys.dtype),
                       pltpu.SemaphoreType.DMA((2,))],
        name="sc_gather_rows",
    )
    def gather_rows(pos_hbm, ys_hbm, out_hbm, idx_v, rows_a, rows_b, sems):
        wid = lax.axis_index("s") * n_cores + lax.axis_index("c")
        bufs = (rows_a, rows_b)

        @pl.loop(0, steps)
        def _(step):
            blk = wid * steps + step
            pltpu.sync_copy(pos_hbm.at[blk], idx_v)
            gathers = [pltpu.make_async_copy(ys_hbm.at[idx_v.at[k]], bufs[k % 2], sems.at[k % 2])
                       for k in range(TOP_K)]
            gathers[0].start()
            for k in range(TOP_K):
                gathers[k].wait()
                if k + 1 < TOP_K:
                    gathers[k + 1].start()
                pltpu.sync_copy(bufs[k % 2], out_hbm.at[pl.ds(k * t + blk * chunk, chunk)])

    return gather_rows(pos_blocks, ys)


def _experts(blk_e, nblk_used, blk_ord, eid_of_ord, n_ord, xs, w1, w3, w2):
    n_rows, half = xs.shape
    d = D_MODEL
    blk = lambda s, be, nb, bo, eo, no: (jnp.minimum(s, nb[0] - 1), 0)
    grid_spec = pltpu.PrefetchScalarGridSpec(
        num_scalar_prefetch=5,
        grid=(n_rows // SLOT_ROWS,),
        in_specs=[pl.BlockSpec((SLOT_ROWS, half), blk),
                  pl.BlockSpec(memory_space=pl.ANY),
                  pl.BlockSpec(memory_space=pl.ANY),
                  pl.BlockSpec(memory_space=pl.ANY)],
        out_specs=pl.BlockSpec((SLOT_ROWS, half), blk),
        scratch_shapes=[pltpu.VMEM((WEIGHT_SLOTS, d, EXPERT_FF), F32),
                        pltpu.VMEM((WEIGHT_SLOTS, d, EXPERT_FF), F32),
                        pltpu.VMEM((WEIGHT_SLOTS, EXPERT_FF, d), F32),
                        pltpu.VMEM((d, EXPERT_FF), BF16),
                        pltpu.VMEM((d, EXPERT_FF), BF16),
                        pltpu.VMEM((EXPERT_FF, d), BF16),
                        pltpu.SemaphoreType.DMA((WEIGHT_SLOTS,))],
    )
    return pl.pallas_call(
        _experts_kernel,
        grid_spec=grid_spec,
        out_shape=jax.ShapeDtypeStruct((n_rows, half), jnp.uint32),
        compiler_params=_cparams(("arbitrary",)),
        name="routed_experts",
    )(blk_e, nblk_used, blk_ord, eid_of_ord, n_ord, xs, w1, w3, w2)


def _combine_kernel(*refs):
    y_refs = refs[:TOP_K]
    w_ref, h_ref, x1_ref, ws1_ref, ws3_ref, ws2_ref, gt2_ref, g_ref = refs[TOP_K:TOP_K + 8]
    o_ref = refs[-1]
    tm, d = x1_ref.shape
    half = d // 2
    h = h_ref[...]
    mid = (_silu(_dot(h, ws1_ref[...])) * _dot(h, ws3_ref[...])).astype(BF16)
    shared = _dot(mid, ws2_ref[...])
    w = w_ref[...]
    acc_lo = jnp.zeros((tm, half), F32)
    acc_hi = jnp.zeros((tm, half), F32)
    for k in range(TOP_K):
        lo, hi = _unpack_halves(y_refs[k][...])
        acc_lo = acc_lo + lo * w[:, k:k + 1]
        acc_hi = acc_hi + hi * w[:, k:k + 1]
    routed = jnp.concatenate([acc_lo, acc_hi], axis=1)
    x2 = x1_ref[...] + gt2_ref[0] * (routed + shared)
    ms = jnp.mean(x2 * x2, axis=-1, keepdims=True)
    o_ref[...] = x2 * lax.rsqrt(ms + NORM_EPS) * g_ref[...]


COMBINE_ROWS = 256
COMBINE_PARTS = 4


def _combine(y_kt, wts, h2, x1, ws1, ws3, ws2, gt2, g_final, seq, part, n_parts, prev_out):
    t, d = x1.shape
    tm = min(COMBINE_ROWS, seq)
    per_b = seq // tm
    tiles = t // tm // n_parts
    first = part * tiles
    row = lambda i: (first + i, 0)
    full = lambda i: (0, 0)
    y_specs = [pl.BlockSpec((tm, d // 2), functools.partial(lambda i, k: (k * tiles + i, 0), k=k))
               for k in range(TOP_K)]
    in_specs = y_specs + [
        pl.BlockSpec((tm, TOP_K), row),
        pl.BlockSpec((tm, d), row),
        pl.BlockSpec((tm, d), row),
        pl.BlockSpec((d, SHARED_FF), full),
        pl.BlockSpec((d, SHARED_FF), full),
        pl.BlockSpec((SHARED_FF, d), full),
        pl.BlockSpec((1, 1, d), lambda i: ((first + i) // per_b, 0, 0)),
        pl.BlockSpec((1, d), full)]
    args = [y_kt] * TOP_K + [wts, h2, x1, ws1, ws3, ws2, gt2, g_final]
    aliases = {}
    if prev_out is not None:
        in_specs.append(pl.BlockSpec(memory_space=pl.ANY))
        aliases = {len(args): 0}
        args.append(prev_out)
    return pl.pallas_call(
        _combine_kernel,
        grid=(tiles,),
        in_specs=in_specs,
        out_specs=pl.BlockSpec((tm, d), row),
        out_shape=jax.ShapeDtypeStruct((t, d), F32),
        input_output_aliases=aliases,
        compiler_params=_cparams(("parallel",)),
        name="combine_shared_final",
    )(*args)


def _slot_tables(cnt, t):
    ntiles = cnt.shape[1] // 128
    cnt_tile = cnt.reshape(N_EXPERTS, ntiles, 128)[:, :, 0].astype(jnp.int32)
    counts = jnp.sum(cnt_tile, axis=1)
    padded = (counts + SLOT_ROWS - 1) // SLOT_ROWS * SLOT_ROWS
    pstart = jnp.cumsum(padded) - padded
    tile_base = pstart[:, None] + jnp.cumsum(cnt_tile, axis=1) - cnt_tile
    n_blk = -(-(t * TOP_K) // SLOT_ROWS) + N_EXPERTS
    blk_end = jnp.cumsum(padded // SLOT_ROWS)
    blk_e = jnp.sum((blk_end[None, :] <= jnp.arange(n_blk)[:, None]).astype(jnp.int32), axis=1)
    blk_e = jnp.minimum(blk_e, N_EXPERTS - 1)
    owns = (padded > 0).astype(jnp.int32)
    ord_of_e = jnp.cumsum(owns) - owns
    ids = jnp.arange(N_EXPERTS, dtype=jnp.int32)
    eid_of_ord = jnp.sum(jnp.where((ord_of_e[None, :] == ids[:, None]) & (owns[None, :] > 0), ids[None, :], 0), axis=1)
    blk_ord = jnp.sum(jnp.where(blk_e[:, None] == ids[None, :], ord_of_e[None, :], 0), axis=1)
    experts_tables = (blk_e, blk_end[-1:].astype(jnp.int32), blk_ord.astype(jnp.int32),
                      eid_of_ord.astype(jnp.int32), jnp.sum(owns).reshape(1).astype(jnp.int32))
    return (experts_tables, pstart + counts, padded - counts,
            tile_base.T.astype(F32).reshape(ntiles, N_EXPERTS, 1), n_blk * SLOT_ROWS)


def _permute_in_cols(w_in):
    qa, ka, va, qr, kr, vr, gr, ga, gt = jnp.split(
        w_in, np.cumsum((A_WIDTH, A_WIDTH, A_WIDTH, R_QK_WIDTH, R_QK_WIDTH, R_V_WIDTH, R_V_WIDTH,
                         D_MODEL))[:].tolist(), axis=1)
    return jnp.concatenate([vr, gr, ga, gt, qa, ka, va, qr, kr], axis=1)


def kernel(x, c, w_ada, b_ada, g_mix, w_in, w_pa, w_pr, w_out, g_ffn, w_router, router_bias,
           w1, w3, w2, ws1, ws3, ws2, g_final):
    bsz, seq, d = x.shape
    t = bsz * seq
    depth = w_ada.shape[0]
    assert depth == 1, "the final norm is fused into the single layer's last kernel"
    slopes = jnp.exp2(-8.0 / A_HEADS * jnp.arange(1, A_HEADS + 1, dtype=F32))
    x2d = x.reshape(t, d)
    for l in range(depth):
        mod = _ada(c, w_ada[l], b_ada[l])
        sh1, sc1, gt1, sh2, sc2, gt2 = [m.reshape(bsz, 1, d) for m in jnp.split(mod, 6, axis=-1)]
        w_in_p = _permute_in_cols(w_in[l]).astype(BF16)
        proj = _inproj(x2d, g_mix[l].reshape(1, d), sc1, sh1, w_in_p, seq)
        proj3 = proj.reshape(bsz, seq, IN_COLS)
        k_aug, q_aug_t, v_aug_t = _moba_prep(proj3, slopes)
        ya = _moba_attn(k_aug, q_aug_t, v_aug_t).reshape(t, A_WIDTH)
        yr = _retention(proj3).reshape(t, R_V_WIDTH)
        x1, h2, h2p = _mix(ya, yr, proj, x2d, w_pa[l].astype(BF16), w_pr[l].astype(BF16),
                           w_out[l].astype(BF16), gt1, g_ffn[l].reshape(1, d), sc2, sh2, seq)
        eidx_t, wts_t, rank_t, cnt = _router(h2, w_router[l].T.astype(BF16),
                                             router_bias[l].reshape(N_EXPERTS, 1))
        experts_tables, pad_start, pad_len, tile_base, n_rows = _slot_tables(cnt, t)
        pos3 = _positions(eidx_t, rank_t, tile_base)
        pos_blocks = jnp.transpose(
            pos3.reshape(pos3.shape[0], TOP_K, -1, SC_DISPATCH_TOKENS), (0, 2, 1, 3)
        ).reshape(t // SC_DISPATCH_TOKENS, TOP_K, SC_DISPATCH_TOKENS)
        xs = _zero_pads(pad_start, pad_len, _sc_dispatch(pos_blocks, h2p, n_rows))
        ys = _experts(*experts_tables, xs, w1[l], w3[l], w2[l])
        wts = wts_t.T
        shared_w = (ws1[l].astype(BF16), ws3[l].astype(BF16), ws2[l].astype(BF16))
        blocks_per_part = pos_blocks.shape[0] // COMBINE_PARTS
        x2d = None
        for part in range(COMBINE_PARTS):
            y_kt = _sc_gather(pos_blocks[part * blocks_per_part:(part + 1) * blocks_per_part], ys,
                              t // COMBINE_PARTS)
            x2d = _combine(y_kt, wts, h2, x1, *shared_w, gt2, g_final.reshape(1, d), seq,
                           part, COMBINE_PARTS, x2d)
    return x2d.reshape(bsz, seq, d)
```

```python
import functools

import jax
import jax.numpy as jnp
import numpy as np
from jax import lax
from jax.experimental import pallas as pl
from jax.experimental.pallas import tpu as pltpu
from jax.experimental.pallas import tpu_sc as plsc

F32 = jnp.float32
BF16 = jnp.bfloat16

D_MODEL = 1024
A_HEADS = 8
A_HEAD_DIM = 64
A_WIDTH = A_HEADS * A_HEAD_DIM
MOBA_BLOCK = 256
MOBA_TOPK = 3
R_HEADS = 8
R_QK_DIM = 64
R_V_DIM = 128
R_QK_WIDTH = R_HEADS * R_QK_DIM
R_V_WIDTH = R_HEADS * R_V_DIM
R_CHUNK = 128
N_EXPERTS = 256
TOP_K = 8
N_GROUPS = 8
GROUP_SIZE = N_EXPERTS // N_GROUPS
TOPK_GROUPS = 4
EXPERT_FF = 256
SHARED_FF = 256
ROUTED_SCALE = 2.5
NORM_EPS = 1e-6
GN_EPS = 1e-6
NEG_INF = -1e30
KNOCKED_OUT = -3e38

COL_VR, COL_GR, COL_GA, COL_GT = 0, 1024, 2048, 3072
COL_QA, COL_KA, COL_VA, COL_QR, COL_KR = 4096, 4608, 5120, 5632, 6144
IN_COLS = 6656
AUG = 128
FEAT_BIAS = A_HEAD_DIM
FEAT_POS = A_HEAD_DIM + 32
V_ROWS = A_HEAD_DIM + 16
MOBA_HEADS_PER_STEP = 8

VMEM_LIMIT = 56 * 1024 * 1024


def _cparams(sem, vmem=VMEM_LIMIT):
    return pltpu.CompilerParams(dimension_semantics=sem, vmem_limit_bytes=vmem)


def _dot(a, b):
    return jnp.dot(a, b, preferred_element_type=F32)


def _dot_nt(a, b):
    return lax.dot_general(a, b, (((1,), (1,)), ((), ())), preferred_element_type=F32)


def _sigmoid(x):
    return 1.0 / (1.0 + jnp.exp(-x))


def _silu(x):
    return x * _sigmoid(x)


def _ada_kernel(c_ref, w_ref, b_ref, o_ref):
    c = c_ref[...]
    s = _silu(c)
    s_hi = s.astype(BF16)
    s_lo = (s - s_hi.astype(F32)).astype(BF16)
    w = w_ref[...]
    w_hi = w.astype(BF16)
    w_lo = (w - w_hi.astype(F32)).astype(BF16)
    o_ref[...] = _dot(s_hi, w_hi) + _dot(s_hi, w_lo) + _dot(s_lo, w_hi) + b_ref[...]


def _ada(c, w_ada, b_ada):
    bsz, d = c.shape
    n = w_ada.shape[1]
    tn = 1024
    return pl.pallas_call(
        _ada_kernel,
        grid=(n // tn,),
        in_specs=[pl.BlockSpec((bsz, d), lambda j: (0, 0)),
                  pl.BlockSpec((d, tn), lambda j: (0, j)),
                  pl.BlockSpec((1, tn), lambda j: (0, j))],
        out_specs=pl.BlockSpec((bsz, tn), lambda j: (0, j)),
        out_shape=jax.ShapeDtypeStruct((bsz, n), F32),
        compiler_params=_cparams(("parallel",)),
        name="ada_mod",
    )(c, w_ada, b_ada.reshape(1, n))


INPROJ_COLS = 512


def _inproj_kernel(x_ref, g_ref, sc_ref, sh_ref, w_ref, o_ref):
    x = x_ref[...]
    ms = jnp.mean(x * x, axis=-1, keepdims=True)
    y = x * lax.rsqrt(ms + NORM_EPS) * g_ref[...]
    h = (y * (1.0 + sc_ref[0]) + sh_ref[0]).astype(BF16)
    for j in range(w_ref.shape[1] // INPROJ_COLS):
        cols = slice(j * INPROJ_COLS, (j + 1) * INPROJ_COLS)
        o_ref[:, cols] = _dot(h, w_ref[:, cols]).astype(BF16)


def _inproj(x2d, g, sc, sh, w_bf16, seq):
    t, d = x2d.shape
    n = w_bf16.shape[1]
    tm = min(512, seq)
    per_b = seq // tm
    return pl.pallas_call(
        _inproj_kernel,
        grid=(t // tm,),
        in_specs=[pl.BlockSpec((tm, d), lambda i: (i, 0)),
                  pl.BlockSpec((1, d), lambda i: (0, 0)),
                  pl.BlockSpec((1, 1, d), lambda i: (i // per_b, 0, 0)),
                  pl.BlockSpec((1, 1, d), lambda i: (i // per_b, 0, 0)),
                  pl.BlockSpec((d, n), lambda i: (0, 0))],
        out_specs=pl.BlockSpec((tm, n), lambda i: (i, 0)),
        out_shape=jax.ShapeDtypeStruct((t, n), BF16),
        compiler_params=_cparams(("parallel",)),
        name="norm_inproj",
    )(x2d, g, sc, sh, w_bf16)


def _moba_prepare(i, slopes_ref, q_ref, k_ref, v_ref, ko_ref, qo_ref, vo_ref, kmean_scr):
    nblk = kmean_scr.shape[0]
    width = q_ref.shape[1]
    seq_rows = pl.ds(pl.multiple_of(i * MOBA_BLOCK, MOBA_BLOCK), MOBA_BLOCK)
    q = q_ref[...]
    k = k_ref[...]
    v = v_ref[...]
    kmean_scr[pl.ds(i, 1), :] = jnp.mean(k.astype(F32), axis=0, keepdims=True)

    eye = (lax.broadcasted_iota(jnp.int32, (width, width), 0)
           == lax.broadcasted_iota(jnp.int32, (width, width), 1)).astype(BF16)
    q_t = _dot_nt(eye, q)
    v_t = _dot_nt(eye, v)

    km = kmean_scr[...]
    km_rep = jnp.concatenate([km] * A_HEADS, axis=0)
    r_head = lax.broadcasted_iota(jnp.int32, km_rep.shape, 0) // nblk
    c_head = lax.broadcasted_iota(jnp.int32, km_rep.shape, 1) // A_HEAD_DIM
    km_bd = jnp.where(r_head == c_head, km_rep, 0.0)
    km_hi = km_bd.astype(BF16)
    km_lo = (km_bd - km_hi.astype(F32)).astype(BF16)
    q_t_b = q_t.astype(BF16)
    gate_all = _dot(km_hi, q_t_b) + _dot(km_lo, q_t_b)

    mb = q.shape[0]
    blk = lax.broadcasted_iota(jnp.int32, (nblk, mb), 0)
    lane_pos = lax.broadcasted_iota(jnp.int32, (16, mb), 1).astype(F32)
    row16 = lax.broadcasted_iota(jnp.int32, (16, mb), 0)
    key_pos = lax.broadcasted_iota(jnp.int32, (mb, AUG), 0).astype(F32)
    kcol = lax.broadcasted_iota(jnp.int32, (mb, AUG), 1)
    sel_r = lax.broadcasted_iota(jnp.int32, (width, AUG), 0)
    sel_c = lax.broadcasted_iota(jnp.int32, (width, AUG), 1)

    for h in range(A_HEADS):
        slope = slopes_ref[h]
        g = jnp.where(blk < i, gate_all[h * nblk:(h + 1) * nblk, :], NEG_INF)
        sel = jnp.zeros((nblk, mb), jnp.bool_)
        for r in range(MOBA_TOPK):
            m = jnp.max(g, axis=0, keepdims=True)
            idx = jnp.min(jnp.where(g == m, blk, nblk), axis=0, keepdims=True)
            hit = blk == idx
            sel = jnp.logical_or(sel, jnp.logical_and(hit, r < i))
            g = jnp.where(hit, KNOCKED_OUT, g)
        bias_t = jnp.where(sel, 0.0, NEG_INF)

        scale = A_HEAD_DIM ** -0.5
        qo_ref[0, h, 0:A_HEAD_DIM, :] = (q_t[h * A_HEAD_DIM:(h + 1) * A_HEAD_DIM, :] * scale).astype(BF16)
        qo_ref[0, h, FEAT_BIAS:FEAT_BIAS + nblk, :] = bias_t.astype(BF16)
        if nblk < 32:
            qo_ref[0, h, FEAT_BIAS + nblk:FEAT_POS, :] = jnp.zeros((32 - nblk, mb), BF16)
        blk_off = slope * (i * mb).astype(F32)
        pos_feat = jnp.where(row16 == 0, -slope * lane_pos,
                             jnp.where(row16 == 2, -blk_off,
                                       jnp.where(jnp.logical_or(row16 == 1, row16 == 3), 1.0, 0.0)))
        qo_ref[0, h, FEAT_POS:FEAT_POS + 16, :] = pos_feat.astype(BF16)
        qo_ref[0, h, FEAT_POS + 16:AUG, :] = jnp.zeros((AUG - FEAT_POS - 16, mb), BF16)

        vo_ref[0, h, 0:A_HEAD_DIM, seq_rows] = v_t[h * A_HEAD_DIM:(h + 1) * A_HEAD_DIM, :].astype(BF16)
        vo_ref[0, h, A_HEAD_DIM:V_ROWS, seq_rows] = jnp.where(row16 == 0, 1.0, 0.0).astype(BF16)

        pick = jnp.where(jnp.logical_and(sel_r == sel_c + h * A_HEAD_DIM, sel_c < A_HEAD_DIM),
                         1.0, 0.0).astype(BF16)
        k_feat = jnp.where(
            jnp.logical_or(kcol == FEAT_BIAS + i, jnp.logical_or(kcol == FEAT_POS, kcol == FEAT_POS + 2)), 1.0,
            jnp.where(kcol == FEAT_POS + 1, slope * key_pos, jnp.where(kcol == FEAT_POS + 3, blk_off, 0.0)))
        ko_ref[0, h, seq_rows, :] = (_dot(k, pick) + k_feat).astype(BF16)


def _moba_attend(i, q_ref, k_ref, v_ref, o_ref, s_a, s_b, group, n_groups):
    mb = MOBA_BLOCK
    span = group * mb
    own = pl.multiple_of(i * mb, mb)
    key_i = lax.broadcasted_iota(jnp.int32, (mb, mb), 0)
    qry_i = lax.broadcasted_iota(jnp.int32, (mb, mb), 1)
    feat = lax.broadcasted_iota(jnp.int32, (AUG, mb), 0)
    is_bias = jnp.logical_and(feat >= FEAT_BIAS, feat < FEAT_POS)
    q_ts, carry0 = [], []
    for hh in range(MOBA_HEADS_PER_STEP):
        q_t = q_ref[0, hh]
        q_ts.append(q_t)
        q_own = jnp.where(is_bias, jnp.zeros_like(q_t), q_t)
        s = _dot(k_ref[0, hh, pl.ds(own, mb), :], q_own)
        s = jnp.where(key_i <= qry_i, s, NEG_INF)
        m0 = jnp.max(s, axis=0, keepdims=True)
        p = jnp.exp(s - m0)
        carry0 += [m0, _dot(v_ref[0, hh, :, pl.ds(own, mb)], p.astype(BF16))]

    def scores(g, dst):
        start = pl.multiple_of(jnp.minimum(g, n_groups - 1) * span, span)
        for hh in range(MOBA_HEADS_PER_STEP):
            dst[hh] = _dot(k_ref[0, hh, pl.ds(start, span), :], q_ts[hh])

    def absorb(g, src, carry):
        start = pl.multiple_of(g * span, span)
        new = []
        for hh in range(MOBA_HEADS_PER_STEP):
            m, acc = carry[2 * hh], carry[2 * hh + 1]
            sb = src[hh]
            m_new = jnp.maximum(m, jnp.max(sb, axis=0, keepdims=True))
            pb = jnp.exp(sb - m_new)
            alpha = jnp.exp(m - m_new)
            acc = acc * alpha + _dot(v_ref[0, hh, :, pl.ds(start, span)], pb.astype(BF16))
            new += [m_new, acc]
        return tuple(new)

    def body(pair, carry):
        scores(2 * pair + 1, s_b)
        carry = absorb(2 * pair, s_a, carry)
        scores(2 * pair + 2, s_a)
        return absorb(2 * pair + 1, s_b, carry)

    scores(0, s_a)
    live_groups = (i + group - 1) // group
    res = lax.fori_loop(0, (live_groups + 1) // 2, body, tuple(carry0))
    outs = [res[2 * hh + 1][0:A_HEAD_DIM, :] / res[2 * hh + 1][A_HEAD_DIM:A_HEAD_DIM + 1, :] for hh in range(MOBA_HEADS_PER_STEP)]
    o_t = jnp.concatenate(outs, axis=0).astype(BF16)
    eye = (key_i == qry_i).astype(BF16)
    o_ref[0] = _dot_nt(eye, o_t).astype(BF16)


def _moba_kernel(slopes_ref, q_ref, k_ref, v_ref, o_ref, kmean_scr, q_scr, k_scr, v_scr, s_a, s_b,
                 *, group, n_groups):
    i = pl.program_id(1)

    @pl.when(i == 0)
    def _():
        kmean_scr[...] = jnp.zeros_like(kmean_scr)
        k_scr[...] = jnp.zeros_like(k_scr)
        v_scr[...] = jnp.zeros_like(v_scr)

    _moba_prepare(i, slopes_ref, q_ref, k_ref, v_ref, k_scr, q_scr, v_scr, kmean_scr)
    _moba_attend(i, q_scr, k_scr, v_scr, o_ref, s_a, s_b, group, n_groups)


def _moba(proj3, slopes):
    bsz, seq, _ = proj3.shape
    mb = MOBA_BLOCK
    nblk = seq // mb
    group = min(2, nblk)
    n_groups = nblk // group
    assert MOBA_HEADS_PER_STEP == A_HEADS
    grid_spec = pltpu.PrefetchScalarGridSpec(
        num_scalar_prefetch=1,
        grid=(bsz, nblk),
        in_specs=[pl.BlockSpec((None, mb, A_WIDTH), lambda b, i, s: (b, i, COL_QA // A_WIDTH)),
                  pl.BlockSpec((None, mb, A_WIDTH), lambda b, i, s: (b, i, COL_KA // A_WIDTH)),
                  pl.BlockSpec((None, mb, A_WIDTH), lambda b, i, s: (b, i, COL_VA // A_WIDTH))],
        out_specs=pl.BlockSpec((1, mb, A_WIDTH), lambda b, i, s: (b, i, 0)),
        scratch_shapes=[pltpu.VMEM((nblk, A_WIDTH), F32),
                        pltpu.VMEM((1, A_HEADS, AUG, mb), BF16),
                        pltpu.VMEM((1, A_HEADS, seq, AUG), BF16),
                        pltpu.VMEM((1, A_HEADS, V_ROWS, seq), BF16),
                        pltpu.VMEM((A_HEADS, group * mb, mb), F32),
                        pltpu.VMEM((A_HEADS, group * mb, mb), F32)],
    )
    return pl.pallas_call(
        functools.partial(_moba_kernel, group=group, n_groups=n_groups),
        grid_spec=grid_spec,
        out_shape=jax.ShapeDtypeStruct((bsz, seq, A_WIDTH), BF16),
        compiler_params=_cparams(("parallel", "arbitrary")),
        name="moba_attention",
    )(slopes, proj3, proj3, proj3)


def _retention_kernel(cdec_ref, q_ref, k_ref, v_ref, g_ref, decay_ref, qdec_ref, kdec_ref, o_ref, state_scr):
    @pl.when(pl.program_id(0) == 0)
    def _():
        state_scr[...] = jnp.zeros_like(state_scr)

    width = q_ref.shape[2]
    eye = (lax.broadcasted_iota(jnp.int32, (width, width), 0)
           == lax.broadcasted_iota(jnp.int32, (width, width), 1)).astype(BF16)
    for b in range(q_ref.shape[0]):
        q = q_ref[b]
        k_t = _dot_nt(eye, k_ref[b])
        k_t_b = k_t.astype(BF16)
        state_b = state_scr[b].astype(BF16)
        for h in range(R_HEADS):
            rows = slice(h * R_QK_DIM, (h + 1) * R_QK_DIM)
            cols = slice(h * R_V_DIM, (h + 1) * R_V_DIM)
            q_h = q[:, rows]
            v_h = v_ref[b, :, cols]
            inner = _dot(q_h, k_t_b[rows, :]) * decay_ref[h]
            out = _dot(inner.astype(BF16), v_h) + _dot(q_h, state_b[rows, :]) * qdec_ref[h]
            k_dec = (k_t[rows, :] * kdec_ref[h]).astype(BF16)
            state_scr[b, rows, :] = cdec_ref[h] * state_scr[b, rows, :] + _dot(k_dec, v_h)
            mu = jnp.mean(out, axis=-1, keepdims=True)
            cen = out - mu
            var = jnp.mean(cen * cen, axis=-1, keepdims=True)
            y = cen * lax.rsqrt(var + GN_EPS)
            o_ref[b, :, cols] = (y * _silu(g_ref[b, :, cols].astype(F32))).astype(BF16)


def _retention_consts():
    h = np.arange(R_HEADS, dtype=np.float64)
    log_g = np.log(1.0 - np.exp2(-5.0 - h))
    n = np.arange(R_CHUNK, dtype=np.float64)
    diff = n[:, None] - n[None, :]
    scale = R_QK_DIM ** -0.5
    decay = np.where(diff >= 0, np.exp(np.maximum(diff, 0.0) * log_g[:, None, None]), 0.0) * scale
    q_decay = np.exp((n + 1.0) * log_g[:, None])[:, :, None]
    k_decay = np.exp((R_CHUNK - 1.0 - n) * log_g[:, None])[:, None, :] * scale
    chunk_decay = np.exp(R_CHUNK * log_g)
    return (jnp.asarray(decay, F32), jnp.asarray(q_decay, F32), jnp.asarray(k_decay, F32),
            jnp.asarray(chunk_decay, F32))


def _retention(proj3):
    bsz, seq, _ = proj3.shape
    c = R_CHUNK
    decay, qdec, kdec, cdec = _retention_consts()
    grid_spec = pltpu.PrefetchScalarGridSpec(
        num_scalar_prefetch=1,
        grid=(seq // c,),
        in_specs=[pl.BlockSpec((bsz, c, R_QK_WIDTH), lambda i, s: (0, i, COL_QR // R_QK_WIDTH)),
                  pl.BlockSpec((bsz, c, R_QK_WIDTH), lambda i, s: (0, i, COL_KR // R_QK_WIDTH)),
                  pl.BlockSpec((bsz, c, R_V_WIDTH), lambda i, s: (0, i, COL_VR // R_V_WIDTH)),
                  pl.BlockSpec((bsz, c, R_V_WIDTH), lambda i, s: (0, i, COL_GR // R_V_WIDTH)),
                  pl.BlockSpec((R_HEADS, c, c), lambda i, s: (0, 0, 0)),
                  pl.BlockSpec((R_HEADS, c, 1), lambda i, s: (0, 0, 0)),
                  pl.BlockSpec((R_HEADS, 1, c), lambda i, s: (0, 0, 0))],
        out_specs=pl.BlockSpec((bsz, c, R_V_WIDTH), lambda i, s: (0, i, 0)),
        scratch_shapes=[pltpu.VMEM((bsz, R_QK_WIDTH, R_V_DIM), F32)],
    )
    return pl.pallas_call(
        _retention_kernel,
        grid_spec=grid_spec,
        out_shape=jax.ShapeDtypeStruct((bsz, seq, R_V_WIDTH), BF16),
        compiler_params=_cparams(("arbitrary",)),
        name="retention",
    )(cdec, proj3, proj3, proj3, proj3, decay, qdec, kdec)


def _pack_halves(x):
    w = x.shape[1] // 2
    bits = lax.bitcast_convert_type(x.astype(BF16).astype(F32), jnp.uint32)
    return (bits[:, :w] >> 16) | (bits[:, w:] & jnp.uint32(0xFFFF0000))


def _unpack_halves(p):
    lo = lax.bitcast_convert_type(p << 16, F32)
    hi = lax.bitcast_convert_type(p & jnp.uint32(0xFFFF0000), F32)
    return lo, hi


SUBLANES = 8


def _mix_kernel(ya_ref, yr_ref, ga_ref, gt_ref, x_ref, wpa_ref, wpr_ref, wout_ref,
                gt1_ref, g_ref, sc_ref, sh_ref, x1_ref, h2_ref, h2p_ref):
    a = _dot(ya_ref[...], wpa_ref[...]) * _sigmoid(ga_ref[...].astype(F32))
    r = _dot(yr_ref[...], wpr_ref[...]) * _sigmoid(gt_ref[...].astype(F32))
    mix = _dot((a + r).astype(BF16), wout_ref[...])
    x1 = x_ref[...] + gt1_ref[0] * mix
    x1_ref[...] = x1
    ms = jnp.mean(x1 * x1, axis=-1, keepdims=True)
    y = x1 * lax.rsqrt(ms + NORM_EPS) * g_ref[...]
    h2 = y * (1.0 + sc_ref[0]) + sh_ref[0]
    h2_ref[...] = h2.astype(BF16)
    h2p_ref[...] = _pack_halves(h2)


def _mix(ya, yr, proj, x2d, wpa, wpr, wout, gt1, g, sc, sh, seq):
    t, d = x2d.shape
    tm = min(512, seq)
    per_b = seq // tm
    row = lambda i: (i, 0)
    full = lambda i: (0, 0)
    per_batch = lambda i: (i // per_b, 0, 0)
    return pl.pallas_call(
        _mix_kernel,
        grid=(t // tm,),
        in_specs=[pl.BlockSpec((tm, A_WIDTH), row),
                  pl.BlockSpec((tm, R_V_WIDTH), row),
                  pl.BlockSpec((tm, d), lambda i: (i, COL_GA // D_MODEL)),
                  pl.BlockSpec((tm, d), lambda i: (i, COL_GT // D_MODEL)),
                  pl.BlockSpec((tm, d), row),
                  pl.BlockSpec((A_WIDTH, d), full),
                  pl.BlockSpec((R_V_WIDTH, d), full),
                  pl.BlockSpec((d, d), full),
                  pl.BlockSpec((1, 1, d), per_batch),
                  pl.BlockSpec((1, d), full),
                  pl.BlockSpec((1, 1, d), per_batch),
                  pl.BlockSpec((1, 1, d), per_batch)],
        out_specs=[pl.BlockSpec((tm, d), row), pl.BlockSpec((tm, d), row),
                   pl.BlockSpec((tm, d // 2), row)],
        out_shape=[jax.ShapeDtypeStruct((t, d), F32), jax.ShapeDtypeStruct((t, d), BF16),
                   jax.ShapeDtypeStruct((t, d // 2), jnp.uint32)],
        compiler_params=_cparams(("parallel",)),
        name="merge_outproj_norm",
    )(ya, yr, proj, proj, x2d, wpa, wpr, wout, gt1, g, sc, sh)


def _router_kernel(h_ref, wr_ref, b_ref, e_ref, w_ref, r_ref, c_ref):
    logits = _dot_nt(wr_ref[...], h_ref[...])
    scores = _sigmoid(logits)
    choice = scores + b_ref[...]
    tm = logits.shape[1]
    giota = lax.broadcasted_iota(jnp.int32, (GROUP_SIZE, tm), 0)
    gs_rows = []
    for g in range(N_GROUPS):
        cg = choice[g * GROUP_SIZE:(g + 1) * GROUP_SIZE, :]
        m1 = jnp.max(cg, axis=0, keepdims=True)
        i1 = jnp.min(jnp.where(cg == m1, giota, GROUP_SIZE), axis=0, keepdims=True)
        m2 = jnp.max(jnp.where(giota == i1, KNOCKED_OUT, cg), axis=0, keepdims=True)
        gs_rows.append(m1 + m2)
    gs = jnp.concatenate(gs_rows, axis=0)
    grow = lax.broadcasted_iota(jnp.int32, (N_GROUPS, tm), 0)
    gmask = jnp.zeros((N_GROUPS, tm), jnp.bool_)
    for _ in range(TOPK_GROUPS):
        mx = jnp.max(gs, axis=0, keepdims=True)
        ix = jnp.min(jnp.where(gs == mx, grow, N_GROUPS), axis=0, keepdims=True)
        hit = grow == ix
        gmask = jnp.logical_or(gmask, hit)
        gs = jnp.where(hit, KNOCKED_OUT, gs)
    gmask_f = jnp.where(gmask, 1.0, 0.0)
    masked = jnp.concatenate(
        [jnp.where(gmask_f[g:g + 1, :] > 0.5, choice[g * GROUP_SIZE:(g + 1) * GROUP_SIZE, :], NEG_INF)
         for g in range(N_GROUPS)], axis=0)
    erow = lax.broadcasted_iota(jnp.int32, (N_EXPERTS, tm), 0)
    idx_rows, w_rows = [], []
    chosen = jnp.zeros((N_EXPERTS, tm), F32)
    for _ in range(TOP_K):
        mx = jnp.max(masked, axis=0, keepdims=True)
        ix = jnp.min(jnp.where(masked == mx, erow, N_EXPERTS), axis=0, keepdims=True)
        hit = erow == ix
        w_rows.append(jnp.sum(jnp.where(hit, scores, 0.0), axis=0, keepdims=True))
        idx_rows.append(ix)
        chosen = jnp.where(hit, 1.0, chosen)
        masked = jnp.where(hit, KNOCKED_OUT, masked)
    w = jnp.concatenate(w_rows, axis=0)
    w = w / (jnp.sum(w, axis=0, keepdims=True) + 1e-20) * ROUTED_SCALE
    e_ref[...] = jnp.concatenate(idx_rows, axis=0)
    w_ref[...] = w
    chosen_b = chosen.astype(BF16)
    earlier = (lax.broadcasted_iota(jnp.int32, (tm, tm), 0)
               < lax.broadcasted_iota(jnp.int32, (tm, tm), 1)).astype(BF16)
    before = _dot(chosen_b, earlier)
    ranks = [jnp.sum(jnp.where(erow == ix, before, 0.0), axis=0, keepdims=True) for ix in idx_rows]
    r_ref[...] = jnp.concatenate(ranks, axis=0).astype(jnp.int32)
    c_ref[...] = _dot(chosen_b, jnp.ones((tm, 128), BF16))


ROUTER_ROWS = 512


def _router(h2, wr_t, bias_col):
    t, d = h2.shape
    tm = ROUTER_ROWS
    by_tile = lambda i: (0, i)
    return pl.pallas_call(
        _router_kernel,
        grid=(t // tm,),
        in_specs=[pl.BlockSpec((tm, d), lambda i: (i, 0)),
                  pl.BlockSpec((N_EXPERTS, d), lambda i: (0, 0)),
                  pl.BlockSpec((N_EXPERTS, 1), lambda i: (0, 0))],
        out_specs=[pl.BlockSpec((TOP_K, tm), by_tile), pl.BlockSpec((TOP_K, tm), by_tile),
                   pl.BlockSpec((TOP_K, tm), by_tile), pl.BlockSpec((N_EXPERTS, 128), by_tile)],
        out_shape=[jax.ShapeDtypeStruct((TOP_K, t), jnp.int32),
                   jax.ShapeDtypeStruct((TOP_K, t), F32),
                   jax.ShapeDtypeStruct((TOP_K, t), jnp.int32),
                   jax.ShapeDtypeStruct((N_EXPERTS, (t // tm) * 128), F32)],
        compiler_params=_cparams(("parallel",)),
        name="router_topk",
    )(h2, wr_t, bias_col)


def _pos_kernel(e_ref, r_ref, base_ref, p_ref):
    tm = e_ref.shape[1]
    erow = lax.broadcasted_iota(jnp.int32, (N_EXPERTS, tm), 0)
    base = base_ref[0]
    rows = [jnp.sum(jnp.where(erow == e_ref[k:k + 1, :], base, 0.0), axis=0, keepdims=True)
            for k in range(TOP_K)]
    p_ref[0] = jnp.concatenate(rows, axis=0).astype(jnp.int32) + r_ref[...]


MOVE_ROWS = 512


def _positions(eidx_t, rank_t, tile_base):
    t = eidx_t.shape[1]
    tm = min(MOVE_ROWS, t)
    per_router_tile = ROUTER_ROWS // tm
    return pl.pallas_call(
        _pos_kernel,
        grid=(t // tm,),
        in_specs=[pl.BlockSpec((TOP_K, tm), lambda i: (0, i)),
                  pl.BlockSpec((TOP_K, tm), lambda i: (0, i)),
                  pl.BlockSpec((1, N_EXPERTS, 1), lambda i: (i // per_router_tile, 0, 0))],
        out_specs=pl.BlockSpec((1, TOP_K, tm), lambda i: (i, 0, 0)),
        out_shape=jax.ShapeDtypeStruct((t // tm, TOP_K, tm), jnp.int32),
        compiler_params=_cparams(("parallel",)),
        name="slot_positions",
    )(eidx_t, rank_t, tile_base)


SLOT_ROWS = 512
PAD_CHUNKS = (256, 128, 64, 32, 16, 8)


def _zero_pads_kernel(pad_start_ref, pad_len_ref, xs_in, xs_hbm, zero_buf, pad_sem):
    del xs_in
    zero_buf[...] = jnp.zeros_like(zero_buf)

    def pad_copies(e, wait):
        start = pad_start_ref[e]
        n = pad_len_ref[e]
        head = jnp.minimum((-start) & (SUBLANES - 1), n)

        def fill(first, size, pred):
            @pl.when(pred)
            def _():
                cp = pltpu.make_async_copy(zero_buf.at[pl.ds(0, size), :], xs_hbm.at[pl.ds(first, size), :], pad_sem)
                if wait:
                    cp.wait()
                else:
                    cp.start()

        for j in range(SUBLANES - 1):
            fill(start + j, 1, j < head)
        ptr = start + head
        rest = n - head
        for chunk in PAD_CHUNKS:
            fill(pl.multiple_of(ptr, SUBLANES), chunk, (rest & chunk) != 0)
            ptr = ptr + (rest & chunk)

    def issue(e, carry):
        pad_copies(e, False)
        return carry

    def drain(e, carry):
        pad_copies(e, True)
        return carry

    lax.fori_loop(0, N_EXPERTS, issue, 0)
    lax.fori_loop(0, N_EXPERTS, drain, 0)


def _zero_pads(pad_start, pad_len, xs):
    grid_spec = pltpu.PrefetchScalarGridSpec(
        num_scalar_prefetch=2,
        grid=(1,),
        in_specs=[pl.BlockSpec(memory_space=pl.ANY)],
        out_specs=pl.BlockSpec(memory_space=pl.ANY),
        scratch_shapes=[pltpu.VMEM((PAD_CHUNKS[0], xs.shape[1]), xs.dtype),
                        pltpu.SemaphoreType.DMA],
    )
    return pl.pallas_call(
        _zero_pads_kernel,
        grid_spec=grid_spec,
        out_shape=jax.ShapeDtypeStruct(xs.shape, xs.dtype),
        input_output_aliases={2: 0},
        compiler_params=_cparams(("arbitrary",)),
        name="zero_pad_slots",
    )(pad_start, pad_len, xs)


SC_DISPATCH_TOKENS = 64


def _sc_dispatch(pos_blocks, h_rows, n_rows):
    info = plsc.get_sparse_core_info()
    n_cores = info.num_cores
    n_workers = n_cores * info.num_subcores
    t = h_rows.shape[0]
    chunk = SC_DISPATCH_TOKENS
    steps = t // (n_workers * chunk)
    mesh = plsc.VectorSubcoreMesh(core_axis_name="c", subcore_axis_name="s")

    @functools.partial(
        pl.kernel, mesh=mesh,
        out_type=jax.ShapeDtypeStruct((n_rows,) + h_rows.shape[1:], h_rows.dtype),
        scratch_types=[pltpu.VMEM((TOP_K, chunk), jnp.int32),
                       pltpu.VMEM((chunk,) + h_rows.shape[1:], h_rows.dtype),
                       pltpu.SemaphoreType.DMA],
        name="sc_dispatch_rows",
    )
    def scatter_rows(pos_hbm, h_hbm, out_hbm, idx_v, rows_v, sem):
        wid = lax.axis_index("s") * n_cores + lax.axis_index("c")

        @pl.loop(0, steps)
        def _(step):
            blk = wid * steps + step
            pltpu.sync_copy(pos_hbm.at[blk], idx_v)
            pltpu.sync_copy(h_hbm.at[pl.ds(blk * chunk, chunk)], rows_v)
            scatters = [pltpu.make_async_copy(rows_v, out_hbm.at[idx_v.at[k]], sem) for k in range(TOP_K)]
            for cp in scatters:
                cp.start()
            for cp in scatters:
                cp.wait()

    return scatter_rows(pos_blocks, h_rows)


WEIGHT_SLOTS = 3


def _experts_kernel(blk_e_ref, nblk_ref, ord_ref, eid_ref, nord_ref, x_ref, w1_hbm, w3_hbm, w2_hbm, y_ref,
                    w1f, w3f, w2f, w1b, w3b, w2b, wsem):
    s = pl.program_id(0)

    def fetch(j):
        slot = j % WEIGHT_SLOTS
        e = eid_ref[j]
        return [pltpu.make_async_copy(src.at[e], dst.at[slot], wsem.at[slot])
                for src, dst in ((w1_hbm, w1f), (w3_hbm, w3f), (w2_hbm, w2f))]

    @pl.when(s < nblk_ref[0])
    def _():
        j = ord_ref[s]

        @pl.when(s == 0)
        def _():
            for cp in fetch(0):
                cp.start()

            @pl.when(nord_ref[0] > 1)
            def _():
                for cp in fetch(1):
                    cp.start()

        @pl.when(jnp.logical_or(s == 0, blk_e_ref[s] != blk_e_ref[jnp.maximum(s - 1, 0)]))
        def _():
            for cp in fetch(j):
                cp.wait()

            @pl.when(j + 2 < nord_ref[0])
            def _():
                for cp in fetch(j + 2):
                    cp.start()

            slot = j % WEIGHT_SLOTS
            w1b[...] = w1f[slot].astype(BF16)
            w3b[...] = w3f[slot].astype(BF16)
            w2b[...] = w2f[slot].astype(BF16)

        half = x_ref.shape[1]
        lo, hi = _unpack_halves(x_ref[...])
        lo = lo.astype(BF16)
        hi = hi.astype(BF16)
        h1 = _dot(lo, w1b[:half, :]) + _dot(hi, w1b[half:, :])
        h3 = _dot(lo, w3b[:half, :]) + _dot(hi, w3b[half:, :])
        mid = (_silu(h1) * h3).astype(BF16)
        y_ref[...] = _pack_halves(_dot(mid, w2b[...]))


def _sc_gather(pos_blocks, ys, t):
    info = plsc.get_sparse_core_info()
    n_cores = info.num_cores
    n_workers = n_cores * info.num_subcores
    chunk = SC_DISPATCH_TOKENS
    steps = t // (n_workers * chunk)
    mesh = plsc.VectorSubcoreMesh(core_axis_name="c", subcore_axis_name="s")

    @functools.partial(
        pl.kernel, mesh=mesh,
        out_type=jax.ShapeDtypeStruct((TOP_K * t,) + ys.shape[1:], ys.dtype),
        scratch_types=[pltpu.VMEM((TOP_K, chunk), jnp.int32),
                       pltpu.VMEM((chunk,) + ys.shape[1:], ys.dtype),
                       pltpu.VMEM((chunk,) + ys.shape[1:], ys.dtype),
                       pltpu.SemaphoreType.DMA((2,))],
        name="sc_gather_rows",
    )
    def gather_rows(pos_hbm, ys_hbm, out_hbm, idx_v, rows_a, rows_b, sems):
        wid = lax.axis_index("s") * n_cores + lax.axis_index("c")
        bufs = (rows_a, rows_b)

        @pl.loop(0, steps)
        def _(step):
            blk = wid * steps + step
            pltpu.sync_copy(pos_hbm.at[blk], idx_v)
            gathers = [pltpu.make_async_copy(ys_hbm.at[idx_v.at[k]], bufs[k % 2], sems.at[k % 2])
                       for k in range(TOP_K)]
            gathers[0].start()
            for k in range(TOP_K):
                gathers[k].wait()
                if k + 1 < TOP_K:
                    gathers[k + 1].start()
                pltpu.sync_copy(bufs[k % 2], out_hbm.at[pl.ds(k * t + blk * chunk, chunk)])

    return gather_rows(pos_blocks, ys)


def _experts(blk_e, nblk_used, blk_ord, eid_of_ord, n_ord, xs, w1, w3, w2):
    n_rows, half = xs.shape
    d = D_MODEL
    blk = lambda s, be, nb, bo, eo, no: (jnp.minimum(s, nb[0] - 1), 0)
    grid_spec = pltpu.PrefetchScalarGridSpec(
        num_scalar_prefetch=5,
        grid=(n_rows // SLOT_ROWS,),
        in_specs=[pl.BlockSpec((SLOT_ROWS, half), blk),
                  pl.BlockSpec(memory_space=pl.ANY),
                  pl.BlockSpec(memory_space=pl.ANY),
                  pl.BlockSpec(memory_space=pl.ANY)],
        out_specs=pl.BlockSpec((SLOT_ROWS, half), blk),
        scratch_shapes=[pltpu.VMEM((WEIGHT_SLOTS, d, EXPERT_FF), F32),
                        pltpu.VMEM((WEIGHT_SLOTS, d, EXPERT_FF), F32),
                        pltpu.VMEM((WEIGHT_SLOTS, EXPERT_FF, d), F32),
                        pltpu.VMEM((d, EXPERT_FF), BF16),
                        pltpu.VMEM((d, EXPERT_FF), BF16),
                        pltpu.VMEM((EXPERT_FF, d), BF16),
                        pltpu.SemaphoreType.DMA((WEIGHT_SLOTS,))],
    )
    return pl.pallas_call(
        _experts_kernel,
        grid_spec=grid_spec,
        out_shape=jax.ShapeDtypeStruct((n_rows, half), jnp.uint32),
        compiler_params=_cparams(("arbitrary",)),
        name="routed_experts",
    )(blk_e, nblk_used, blk_ord, eid_of_ord, n_ord, xs, w1, w3, w2)


def _combine_kernel(*refs):
    y_refs = refs[:TOP_K]
    w_ref, h_ref, x1_ref, ws1_ref, ws3_ref, ws2_ref, gt2_ref, g_ref = refs[TOP_K:TOP_K + 8]
    o_ref = refs[-1]
    tm, d = x1_ref.shape
    half = d // 2
    h = h_ref[...]
    mid = (_silu(_dot(h, ws1_ref[...])) * _dot(h, ws3_ref[...])).astype(BF16)
    shared = _dot(mid, ws2_ref[...])
    w = w_ref[...]
    acc_lo = jnp.zeros((tm, half), F32)
    acc_hi = jnp.zeros((tm, half), F32)
    for k in range(TOP_K):
        lo, hi = _unpack_halves(y_refs[k][...])
        acc_lo = acc_lo + lo * w[:, k:k + 1]
        acc_hi = acc_hi + hi * w[:, k:k + 1]
    routed = jnp.concatenate([acc_lo, acc_hi], axis=1)
    x2 = x1_ref[...] + gt2_ref[0] * (routed + shared)
    ms = jnp.mean(x2 * x2, axis=-1, keepdims=True)
    o_ref[...] = x2 * lax.rsqrt(ms + NORM_EPS) * g_ref[...]


COMBINE_ROWS = 256
COMBINE_PARTS = 4


def _combine(y_kt, wts, h2, x1, ws1, ws3, ws2, gt2, g_final, seq, part, n_parts, prev_out):
    t, d = x1.shape
    tm = min(COMBINE_ROWS, seq)
    per_b = seq // tm
    tiles = t // tm // n_parts
    first = part * tiles
    row = lambda i: (first + i, 0)
    full = lambda i: (0, 0)
    y_specs = [pl.BlockSpec((tm, d // 2), functools.partial(lambda i, k: (k * tiles + i, 0), k=k))
               for k in range(TOP_K)]
    in_specs = y_specs + [
        pl.BlockSpec((tm, TOP_K), row),
        pl.BlockSpec((tm, d), row),
        pl.BlockSpec((tm, d), row),
        pl.BlockSpec((d, SHARED_FF), full),
        pl.BlockSpec((d, SHARED_FF), full),
        pl.BlockSpec((SHARED_FF, d), full),
        pl.BlockSpec((1, 1, d), lambda i: ((first + i) // per_b, 0, 0)),
        pl.BlockSpec((1, d), full)]
    args = [y_kt] * TOP_K + [wts, h2, x1, ws1, ws3, ws2, gt2, g_final]
    aliases = {}
    if prev_out is not None:
        in_specs.append(pl.BlockSpec(memory_space=pl.ANY))
        aliases = {len(args): 0}
        args.append(prev_out)
    return pl.pallas_call(
        _combine_kernel,
        grid=(tiles,),
        in_specs=in_specs,
        out_specs=pl.BlockSpec((tm, d), row),
        out_shape=jax.ShapeDtypeStruct((t, d), F32),
        input_output_aliases=aliases,
        compiler_params=_cparams(("parallel",)),
        name="combine_shared_final",
    )(*args)


def _slot_tables(cnt, t):
    ntiles = cnt.shape[1] // 128
    cnt_tile = cnt.reshape(N_EXPERTS, ntiles, 128)[:, :, 0].astype(jnp.int32)
    counts = jnp.sum(cnt_tile, axis=1)
    padded = (counts + SLOT_ROWS - 1) // SLOT_ROWS * SLOT_ROWS
    pstart = jnp.cumsum(padded) - padded
    tile_base = pstart[:, None] + jnp.cumsum(cnt_tile, axis=1) - cnt_tile
    n_blk = -(-(t * TOP_K) // SLOT_ROWS) + N_EXPERTS
    blk_end = jnp.cumsum(padded // SLOT_ROWS)
    blk_e = jnp.sum((blk_end[None, :] <= jnp.arange(n_blk)[:, None]).astype(jnp.int32), axis=1)
    blk_e = jnp.minimum(blk_e, N_EXPERTS - 1)
    owns = (padded > 0).astype(jnp.int32)
    ord_of_e = jnp.cumsum(owns) - owns
    ids = jnp.arange(N_EXPERTS, dtype=jnp.int32)
    eid_of_ord = jnp.sum(jnp.where((ord_of_e[None, :] == ids[:, None]) & (owns[None, :] > 0), ids[None, :], 0), axis=1)
    blk_ord = jnp.sum(jnp.where(blk_e[:, None] == ids[None, :], ord_of_e[None, :], 0), axis=1)
    experts_tables = (blk_e, blk_end[-1:].astype(jnp.int32), blk_ord.astype(jnp.int32),
                      eid_of_ord.astype(jnp.int32), jnp.sum(owns).reshape(1).astype(jnp.int32))
    return (experts_tables, pstart + counts, padded - counts,
            tile_base.T.astype(F32).reshape(ntiles, N_EXPERTS, 1), n_blk * SLOT_ROWS)


def _permute_in_cols(w_in):
    qa, ka, va, qr, kr, vr, gr, ga, gt = jnp.split(
        w_in, np.cumsum((A_WIDTH, A_WIDTH, A_WIDTH, R_QK_WIDTH, R_QK_WIDTH, R_V_WIDTH, R_V_WIDTH,
                         D_MODEL))[:].tolist(), axis=1)
    return jnp.concatenate([vr, gr, ga, gt, qa, ka, va, qr, kr], axis=1)


def kernel(x, c, w_ada, b_ada, g_mix, w_in, w_pa, w_pr, w_out, g_ffn, w_router, router_bias,
           w1, w3, w2, ws1, ws3, ws2, g_final):
    bsz, seq, d = x.shape
    t = bsz * seq
    depth = w_ada.shape[0]
    assert depth == 1, "the final norm is fused into the single layer's last kernel"
    slopes = jnp.exp2(-8.0 / A_HEADS * jnp.arange(1, A_HEADS + 1, dtype=F32))
    x2d = x.reshape(t, d)
    for l in range(depth):
        mod = _ada(c, w_ada[l], b_ada[l])
        sh1, sc1, gt1, sh2, sc2, gt2 = [m.reshape(bsz, 1, d) for m in jnp.split(mod, 6, axis=-1)]
        w_in_p = _permute_in_cols(w_in[l]).astype(BF16)
        proj = _inproj(x2d, g_mix[l].reshape(1, d), sc1, sh1, w_in_p, seq)
        proj3 = proj.reshape(bsz, seq, IN_COLS)
        ya = _moba(proj3, slopes).reshape(t, A_WIDTH)
        yr = _retention(proj3).reshape(t, R_V_WIDTH)
        x1, h2, h2p = _mix(ya, yr, proj, x2d, w_pa[l].astype(BF16), w_pr[l].astype(BF16),
                           w_out[l].astype(BF16), gt1, g_ffn[l].reshape(1, d), sc2, sh2, seq)
        eidx_t, wts_t, rank_t, cnt = _router(h2, w_router[l].T.astype(BF16),
                                             router_bias[l].reshape(N_EXPERTS, 1))
        experts_tables, pad_start, pad_len, tile_base, n_rows = _slot_tables(cnt, t)
        pos3 = _positions(eidx_t, rank_t, tile_base)
        pos_blocks = jnp.transpose(
            pos3.reshape(pos3.shape[0], TOP_K, -1, SC_DISPATCH_TOKENS), (0, 2, 1, 3)
        ).reshape(t // SC_DISPATCH_TOKENS, TOP_K, SC_DISPATCH_TOKENS)
        xs = _zero_pads(pad_start, pad_len, _sc_dispatch(pos_blocks, h2p, n_rows))
        ys = _experts(*experts_tables, xs, w1[l], w3[l], w2[l])
        wts = wts_t.T
        shared_w = (ws1[l].astype(BF16), ws3[l].astype(BF16), ws2[l].astype(BF16))
        blocks_per_part = pos_blocks.shape[0] // COMBINE_PARTS
        x2d = None
        for part in range(COMBINE_PARTS):
            y_kt = _sc_gather(pos_blocks[part * blocks_per_part:(part + 1) * blocks_per_part], ys,
                              t // COMBINE_PARTS)
            x2d = _combine(y_kt, wts, h2, x1, *shared_w, gt2, g_final.reshape(1, d), seq,
                           part, COMBINE_PARTS, x2d)
    return x2d.reshape(bsz, seq, d)
```

```python
import functools

import jax
import jax.numpy as jnp
import numpy as np
from jax import lax
from jax.experimental import pallas as pl
from jax.experimental.pallas import tpu as pltpu
from jax.experimental.pallas import tpu_sc as plsc

F32 = jnp.float32
BF16 = jnp.bfloat16

D_MODEL = 1024
A_HEADS = 8
A_HEAD_DIM = 64
A_WIDTH = A_HEADS * A_HEAD_DIM
MOBA_BLOCK = 256
MOBA_TOPK = 3
R_HEADS = 8
R_QK_DIM = 64
R_V_DIM = 128
R_QK_WIDTH = R_HEADS * R_QK_DIM
R_V_WIDTH = R_HEADS * R_V_DIM
R_CHUNK = 128
N_EXPERTS = 256
TOP_K = 8
N_GROUPS = 8
GROUP_SIZE = N_EXPERTS // N_GROUPS
TOPK_GROUPS = 4
EXPERT_FF = 256
SHARED_FF = 256
ROUTED_SCALE = 2.5
NORM_EPS = 1e-6
GN_EPS = 1e-6
NEG_INF = -1e30
KNOCKED_OUT = -3e38

COL_VR, COL_GR, COL_GA, COL_GT = 0, 1024, 2048, 3072
COL_QA, COL_KA, COL_VA, COL_QR, COL_KR = 4096, 4608, 5120, 5632, 6144
IN_COLS = 6656
AUG = 128
FEAT_BIAS = A_HEAD_DIM
FEAT_POS = A_HEAD_DIM + 32
V_ROWS = A_HEAD_DIM + 16
MOBA_HEADS_PER_STEP = 8

VMEM_LIMIT = 56 * 1024 * 1024


def _cparams(sem, vmem=VMEM_LIMIT):
    return pltpu.CompilerParams(dimension_semantics=sem, vmem_limit_bytes=vmem)


def _dot(a, b):
    return jnp.dot(a, b, preferred_element_type=F32)


def _dot_nt(a, b):
    return lax.dot_general(a, b, (((1,), (1,)), ((), ())), preferred_element_type=F32)


def _sigmoid(x):
    return 1.0 / (1.0 + jnp.exp(-x))


def _silu(x):
    return x * _sigmoid(x)


def _ada_kernel(c_ref, w_ref, b_ref, o_ref):
    c = c_ref[...]
    s = _silu(c)
    s_hi = s.astype(BF16)
    s_lo = (s - s_hi.astype(F32)).astype(BF16)
    w = w_ref[...]
    w_hi = w.astype(BF16)
    w_lo = (w - w_hi.astype(F32)).astype(BF16)
    o_ref[...] = _dot(s_hi, w_hi) + _dot(s_hi, w_lo) + _dot(s_lo, w_hi) + b_ref[...]


def _ada(c, w_ada, b_ada):
    bsz, d = c.shape
    n = w_ada.shape[1]
    tn = 1024
    return pl.pallas_call(
        _ada_kernel,
        grid=(n // tn,),
        in_specs=[pl.BlockSpec((bsz, d), lambda j: (0, 0)),
                  pl.BlockSpec((d, tn), lambda j: (0, j)),
                  pl.BlockSpec((1, tn), lambda j: (0, j))],
        out_specs=pl.BlockSpec((bsz, tn), lambda j: (0, j)),
        out_shape=jax.ShapeDtypeStruct((bsz, n), F32),
        compiler_params=_cparams(("parallel",)),
        name="ada_mod",
    )(c, w_ada, b_ada.reshape(1, n))


INPROJ_COLS = 512


def _inproj_kernel(x_ref, g_ref, sc_ref, sh_ref, w_ref, o_ref):
    x = x_ref[...]
    ms = jnp.mean(x * x, axis=-1, keepdims=True)
    y = x * lax.rsqrt(ms + NORM_EPS) * g_ref[...]
    h = (y * (1.0 + sc_ref[0]) + sh_ref[0]).astype(BF16)
    for j in range(w_ref.shape[1] // INPROJ_COLS):
        cols = slice(j * INPROJ_COLS, (j + 1) * INPROJ_COLS)
        o_ref[:, cols] = _dot(h, w_ref[:, cols]).astype(BF16)


def _inproj(x2d, g, sc, sh, w_bf16, seq):
    t, d = x2d.shape
    n = w_bf16.shape[1]
    tm = min(512, seq)
    per_b = seq // tm
    return pl.pallas_call(
        _inproj_kernel,
        grid=(t // tm,),
        in_specs=[pl.BlockSpec((tm, d), lambda i: (i, 0)),
                  pl.BlockSpec((1, d), lambda i: (0, 0)),
                  pl.BlockSpec((1, 1, d), lambda i: (i // per_b, 0, 0)),
                  pl.BlockSpec((1, 1, d), lambda i: (i // per_b, 0, 0)),
                  pl.BlockSpec((d, n), lambda i: (0, 0))],
        out_specs=pl.BlockSpec((tm, n), lambda i: (i, 0)),
        out_shape=jax.ShapeDtypeStruct((t, n), BF16),
        compiler_params=_cparams(("parallel",)),
        name="norm_inproj",
    )(x2d, g, sc, sh, w_bf16)


def _moba_prepare(i, slopes_ref, q_ref, k_ref, v_ref, ko_ref, qo_ref, vo_ref, kmean_scr):
    nblk = kmean_scr.shape[0]
    width = q_ref.shape[1]
    seq_rows = pl.ds(pl.multiple_of(i * MOBA_BLOCK, MOBA_BLOCK), MOBA_BLOCK)
    q = q_ref[...]
    k = k_ref[...]
    v = v_ref[...]
    kmean_scr[pl.ds(i, 1), :] = jnp.mean(k.astype(F32), axis=0, keepdims=True)

    eye = (lax.broadcasted_iota(jnp.int32, (width, width), 0)
           == lax.broadcasted_iota(jnp.int32, (width, width), 1)).astype(BF16)
    q_t = _dot_nt(eye, q)
    v_t = _dot_nt(eye, v)

    km = kmean_scr[...]
    km_rep = jnp.concatenate([km] * A_HEADS, axis=0)
    r_head = lax.broadcasted_iota(jnp.int32, km_rep.shape, 0) // nblk
    c_head = lax.broadcasted_iota(jnp.int32, km_rep.shape, 1) // A_HEAD_DIM
    km_bd = jnp.where(r_head == c_head, km_rep, 0.0)
    km_hi = km_bd.astype(BF16)
    km_lo = (km_bd - km_hi.astype(F32)).astype(BF16)
    q_t_b = q_t.astype(BF16)
    gate_all = _dot(km_hi, q_t_b) + _dot(km_lo, q_t_b)

    mb = q.shape[0]
    blk = lax.broadcasted_iota(jnp.int32, (nblk, mb), 0)
    lane_pos = lax.broadcasted_iota(jnp.int32, (16, mb), 1).astype(F32)
    row16 = lax.broadcasted_iota(jnp.int32, (16, mb), 0)
    key_pos = lax.broadcasted_iota(jnp.int32, (mb, AUG), 0).astype(F32)
    kcol = lax.broadcasted_iota(jnp.int32, (mb, AUG), 1)
    sel_r = lax.broadcasted_iota(jnp.int32, (width, AUG), 0)
    sel_c = lax.broadcasted_iota(jnp.int32, (width, AUG), 1)

    for h in range(A_HEADS):
        slope = slopes_ref[h]
        g = jnp.where(blk < i, gate_all[h * nblk:(h + 1) * nblk, :], NEG_INF)
        sel = jnp.zeros((nblk, mb), jnp.bool_)
        for r in range(MOBA_TOPK):
            m = jnp.max(g, axis=0, keepdims=True)
            idx = jnp.min(jnp.where(g == m, blk, nblk), axis=0, keepdims=True)
            hit = blk == idx
            sel = jnp.logical_or(sel, jnp.logical_and(hit, r < i))
            g = jnp.where(hit, KNOCKED_OUT, g)
        bias_t = jnp.where(sel, 0.0, NEG_INF)

        scale = A_HEAD_DIM ** -0.5
        qo_ref[0, h, 0:A_HEAD_DIM, :] = (q_t[h * A_HEAD_DIM:(h + 1) * A_HEAD_DIM, :] * scale).astype(BF16)
        qo_ref[0, h, FEAT_BIAS:FEAT_BIAS + nblk, :] = bias_t.astype(BF16)
        if nblk < 32:
            qo_ref[0, h, FEAT_BIAS + nblk:FEAT_POS, :] = jnp.zeros((32 - nblk, mb), BF16)
        blk_off = slope * (i * mb).astype(F32)
        pos_feat = jnp.where(row16 == 0, -slope * lane_pos,
                             jnp.where(row16 == 2, -blk_off,
                                       jnp.where(jnp.logical_or(row16 == 1, row16 == 3), 1.0, 0.0)))
        qo_ref[0, h, FEAT_POS:FEAT_POS + 16, :] = pos_feat.astype(BF16)
        qo_ref[0, h, FEAT_POS + 16:AUG, :] = jnp.zeros((AUG - FEAT_POS - 16, mb), BF16)

        vo_ref[0, h, 0:A_HEAD_DIM, seq_rows] = v_t[h * A_HEAD_DIM:(h + 1) * A_HEAD_DIM, :].astype(BF16)
        vo_ref[0, h, A_HEAD_DIM:V_ROWS, seq_rows] = jnp.where(row16 == 0, 1.0, 0.0).astype(BF16)

        pick = jnp.where(jnp.logical_and(sel_r == sel_c + h * A_HEAD_DIM, sel_c < A_HEAD_DIM),
                         1.0, 0.0).astype(BF16)
        k_feat = jnp.where(
            jnp.logical_or(kcol == FEAT_BIAS + i, jnp.logical_or(kcol == FEAT_POS, kcol == FEAT_POS + 2)), 1.0,
            jnp.where(kcol == FEAT_POS + 1, slope * key_pos, jnp.where(kcol == FEAT_POS + 3, blk_off, 0.0)))
        ko_ref[0, h, seq_rows, :] = (_dot(k, pick) + k_feat).astype(BF16)


def _moba_attend(i, q_ref, k_ref, v_ref, o_ref, s_a, s_b, group, n_groups):
    mb = MOBA_BLOCK
    span = group * mb
    own = pl.multiple_of(i * mb, mb)
    key_i = lax.broadcasted_iota(jnp.int32, (mb, mb), 0)
    qry_i = lax.broadcasted_iota(jnp.int32, (mb, mb), 1)
    feat = lax.broadcasted_iota(jnp.int32, (AUG, mb), 0)
    is_bias = jnp.logical_and(feat >= FEAT_BIAS, feat < FEAT_POS)
    q_ts, carry0 = [], []
    for hh in range(MOBA_HEADS_PER_STEP):
        q_t = q_ref[0, hh]
        q_ts.append(q_t)
        q_own = jnp.where(is_bias, jnp.zeros_like(q_t), q_t)
        s = _dot(k_ref[0, hh, pl.ds(own, mb), :], q_own)
        s = jnp.where(key_i <= qry_i, s, NEG_INF)
        m0 = jnp.max(s, axis=0, keepdims=True)
        p = jnp.exp(s - m0)
        carry0 += [m0, _dot(v_ref[0, hh, :, pl.ds(own, mb)], p.astype(BF16))]

    def scores(g, dst):
        start = pl.multiple_of(jnp.minimum(g, n_groups - 1) * span, span)
        for hh in range(MOBA_HEADS_PER_STEP):
            dst[hh] = _dot(k_ref[0, hh, pl.ds(start, span), :], q_ts[hh])

    def absorb(g, src, carry):
        start = pl.multiple_of(g * span, span)
        new = []
        for hh in range(MOBA_HEADS_PER_STEP):
            m, acc = carry[2 * hh], carry[2 * hh + 1]
            sb = src[hh]
            m_new = jnp.maximum(m, jnp.max(sb, axis=0, keepdims=True))
            pb = jnp.exp(sb - m_new)
            alpha = jnp.exp(m - m_new)
            acc = acc * alpha + _dot(v_ref[0, hh, :, pl.ds(start, span)], pb.astype(BF16))
            new += [m_new, acc]
        return tuple(new)

    def body(pair, carry):
        scores(2 * pair + 1, s_b)
        carry = absorb(2 * pair, s_a, carry)
        scores(2 * pair + 2, s_a)
        return absorb(2 * pair + 1, s_b, carry)

    scores(0, s_a)
    live_groups = (i + group - 1) // group
    res = lax.fori_loop(0, live_groups // 2, body, tuple(carry0))
    res = lax.cond(live_groups % 2 == 1, lambda c: absorb(live_groups - 1, s_a, c), lambda c: c, res)
    outs = [res[2 * hh + 1][0:A_HEAD_DIM, :] / res[2 * hh + 1][A_HEAD_DIM:A_HEAD_DIM + 1, :] for hh in range(MOBA_HEADS_PER_STEP)]
    o_t = jnp.concatenate(outs, axis=0).astype(BF16)
    eye = (key_i == qry_i).astype(BF16)
    o_ref[0] = _dot_nt(eye, o_t).astype(BF16)


def _moba_kernel(slopes_ref, q_ref, k_ref, v_ref, o_ref, kmean_scr, q_scr, k_scr, v_scr, s_a, s_b,
                 *, group, n_groups):
    i = pl.program_id(1)

    @pl.when(i == 0)
    def _():
        kmean_scr[...] = jnp.zeros_like(kmean_scr)
        k_scr[...] = jnp.zeros_like(k_scr)
        v_scr[...] = jnp.zeros_like(v_scr)

    _moba_prepare(i, slopes_ref, q_ref, k_ref, v_ref, k_scr, q_scr, v_scr, kmean_scr)
    _moba_attend(i, q_scr, k_scr, v_scr, o_ref, s_a, s_b, group, n_groups)


def _moba(proj3, slopes):
    bsz, seq, _ = proj3.shape
    mb = MOBA_BLOCK
    nblk = seq // mb
    group = min(2, nblk)
    n_groups = nblk // group
    assert MOBA_HEADS_PER_STEP == A_HEADS
    grid_spec = pltpu.PrefetchScalarGridSpec(
        num_scalar_prefetch=1,
        grid=(bsz, nblk),
        in_specs=[pl.BlockSpec((None, mb, A_WIDTH), lambda b, i, s: (b, i, COL_QA // A_WIDTH)),
                  pl.BlockSpec((None, mb, A_WIDTH), lambda b, i, s: (b, i, COL_KA // A_WIDTH)),
                  pl.BlockSpec((None, mb, A_WIDTH), lambda b, i, s: (b, i, COL_VA // A_WIDTH))],
        out_specs=pl.BlockSpec((1, mb, A_WIDTH), lambda b, i, s: (b, i, 0)),
        scratch_shapes=[pltpu.VMEM((nblk, A_WIDTH), F32),
                        pltpu.VMEM((1, A_HEADS, AUG, mb), BF16),
                        pltpu.VMEM((1, A_HEADS, seq, AUG), BF16),
                        pltpu.VMEM((1, A_HEADS, V_ROWS, seq), BF16),
                        pltpu.VMEM((A_HEADS, group * mb, mb), F32),
                        pltpu.VMEM((A_HEADS, group * mb, mb), F32)],
    )
    return pl.pallas_call(
        functools.partial(_moba_kernel, group=group, n_groups=n_groups),
        grid_spec=grid_spec,
        out_shape=jax.ShapeDtypeStruct((bsz, seq, A_WIDTH), BF16),
        compiler_params=_cparams(("parallel", "arbitrary")),
        name="moba_attention",
    )(slopes, proj3, proj3, proj3)


def _retention_kernel(cdec_ref, q_ref, k_ref, v_ref, g_ref, decay_ref, qdec_ref, kdec_ref, o_ref, state_scr):
    @pl.when(pl.program_id(0) == 0)
    def _():
        state_scr[...] = jnp.zeros_like(state_scr)

    width = q_ref.shape[2]
    eye = (lax.broadcasted_iota(jnp.int32, (width, width), 0)
           == lax.broadcasted_iota(jnp.int32, (width, width), 1)).astype(BF16)
    for b in range(q_ref.shape[0]):
        q = q_ref[b]
        k_t = _dot_nt(eye, k_ref[b])
        k_t_b = k_t.astype(BF16)
        state_b = state_scr[b].astype(BF16)
        for h in range(R_HEADS):
            rows = slice(h * R_QK_DIM, (h + 1) * R_QK_DIM)
            cols = slice(h * R_V_DIM, (h + 1) * R_V_DIM)
            q_h = q[:, rows]
            v_h = v_ref[b, :, cols]
            inner = _dot(q_h, k_t_b[rows, :]) * decay_ref[h]
            out = _dot(inner.astype(BF16), v_h) + _dot(q_h, state_b[rows, :]) * qdec_ref[h]
            k_dec = (k_t[rows, :] * kdec_ref[h]).astype(BF16)
            state_scr[b, rows, :] = cdec_ref[h] * state_scr[b, rows, :] + _dot(k_dec, v_h)
            mu = jnp.mean(out, axis=-1, keepdims=True)
            cen = out - mu
            var = jnp.mean(cen * cen, axis=-1, keepdims=True)
            y = cen * lax.rsqrt(var + GN_EPS)
            o_ref[b, :, cols] = (y * _silu(g_ref[b, :, cols].astype(F32))).astype(BF16)


def _retention_consts():
    h = np.arange(R_HEADS, dtype=np.float64)
    log_g = np.log(1.0 - np.exp2(-5.0 - h))
    n = np.arange(R_CHUNK, dtype=np.float64)
    diff = n[:, None] - n[None, :]
    scale = R_QK_DIM ** -0.5
    decay = np.where(diff >= 0, np.exp(np.maximum(diff, 0.0) * log_g[:, None, None]), 0.0) * scale
    q_decay = np.exp((n + 1.0) * log_g[:, None])[:, :, None]
    k_decay = np.exp((R_CHUNK - 1.0 - n) * log_g[:, None])[:, None, :] * scale
    chunk_decay = np.exp(R_CHUNK * log_g)
    return (jnp.asarray(decay, F32), jnp.asarray(q_decay, F32), jnp.asarray(k_decay, F32),
            jnp.asarray(chunk_decay, F32))


def _retention(proj3):
    bsz, seq, _ = proj3.shape
    c = R_CHUNK
    decay, qdec, kdec, cdec = _retention_consts()
    grid_spec = pltpu.PrefetchScalarGridSpec(
        num_scalar_prefetch=1,
        grid=(seq // c,),
        in_specs=[pl.BlockSpec((bsz, c, R_QK_WIDTH), lambda i, s: (0, i, COL_QR // R_QK_WIDTH)),
                  pl.BlockSpec((bsz, c, R_QK_WIDTH), lambda i, s: (0, i, COL_KR // R_QK_WIDTH)),
                  pl.BlockSpec((bsz, c, R_V_WIDTH), lambda i, s: (0, i, COL_VR // R_V_WIDTH)),
                  pl.BlockSpec((bsz, c, R_V_WIDTH), lambda i, s: (0, i, COL_GR // R_V_WIDTH)),
                  pl.BlockSpec((R_HEADS, c, c), lambda i, s: (0, 0, 0)),
                  pl.BlockSpec((R_HEADS, c, 1), lambda i, s: (0, 0, 0)),
                  pl.BlockSpec((R_HEADS, 1, c), lambda i, s: (0, 0, 0))],
        out_specs=pl.BlockSpec((bsz, c, R_V_WIDTH), lambda i, s: (0, i, 0)),
        scratch_shapes=[pltpu.VMEM((bsz, R_QK_WIDTH, R_V_DIM), F32)],
    )
    return pl.pallas_call(
        _retention_kernel,
        grid_spec=grid_spec,
        out_shape=jax.ShapeDtypeStruct((bsz, seq, R_V_WIDTH), BF16),
        compiler_params=_cparams(("arbitrary",)),
        name="retention",
    )(cdec, proj3, proj3, proj3, proj3, decay, qdec, kdec)


def _pack_halves(x):
    w = x.shape[1] // 2
    bits = lax.bitcast_convert_type(x.astype(BF16).astype(F32), jnp.uint32)
    return (bits[:, :w] >> 16) | (bits[:, w:] & jnp.uint32(0xFFFF0000))


def _unpack_halves(p):
    lo = lax.bitcast_convert_type(p << 16, F32)
    hi = lax.bitcast_convert_type(p & jnp.uint32(0xFFFF0000), F32)
    return lo, hi


SUBLANES = 8


def _mix_kernel(ya_ref, yr_ref, ga_ref, gt_ref, x_ref, wpa_ref, wpr_ref, wout_ref,
                gt1_ref, g_ref, sc_ref, sh_ref, x1_ref, h2_ref, h2p_ref):
    a = _dot(ya_ref[...], wpa_ref[...]) * _sigmoid(ga_ref[...].astype(F32))
    r = _dot(yr_ref[...], wpr_ref[...]) * _sigmoid(gt_ref[...].astype(F32))
    mix = _dot((a + r).astype(BF16), wout_ref[...])
    x1 = x_ref[...] + gt1_ref[0] * mix
    x1_ref[...] = x1
    ms = jnp.mean(x1 * x1, axis=-1, keepdims=True)
    y = x1 * lax.rsqrt(ms + NORM_EPS) * g_ref[...]
    h2 = y * (1.0 + sc_ref[0]) + sh_ref[0]
    h2_ref[...] = h2.astype(BF16)
    h2p_ref[...] = _pack_halves(h2)


def _mix(ya, yr, proj, x2d, wpa, wpr, wout, gt1, g, sc, sh, seq):
    t, d = x2d.shape
    tm = min(512, seq)
    per_b = seq // tm
    row = lambda i: (i, 0)
    full = lambda i: (0, 0)
    per_batch = lambda i: (i // per_b, 0, 0)
    return pl.pallas_call(
        _mix_kernel,
        grid=(t // tm,),
        in_specs=[pl.BlockSpec((tm, A_WIDTH), row),
                  pl.BlockSpec((tm, R_V_WIDTH), row),
                  pl.BlockSpec((tm, d), lambda i: (i, COL_GA // D_MODEL)),
                  pl.BlockSpec((tm, d), lambda i: (i, COL_GT // D_MODEL)),
                  pl.BlockSpec((tm, d), row),
                  pl.BlockSpec((A_WIDTH, d), full),
                  pl.BlockSpec((R_V_WIDTH, d), full),
                  pl.BlockSpec((d, d), full),
                  pl.BlockSpec((1, 1, d), per_batch),
                  pl.BlockSpec((1, d), full),
                  pl.BlockSpec((1, 1, d), per_batch),
                  pl.BlockSpec((1, 1, d), per_batch)],
        out_specs=[pl.BlockSpec((tm, d), row), pl.BlockSpec((tm, d), row),
                   pl.BlockSpec((tm, d // 2), row)],
        out_shape=[jax.ShapeDtypeStruct((t, d), F32), jax.ShapeDtypeStruct((t, d), BF16),
                   jax.ShapeDtypeStruct((t, d // 2), jnp.uint32)],
        compiler_params=_cparams(("parallel",)),
        name="merge_outproj_norm",
    )(ya, yr, proj, proj, x2d, wpa, wpr, wout, gt1, g, sc, sh)


def _router_kernel(h_ref, wr_ref, b_ref, e_ref, w_ref, r_ref, c_ref):
    logits = _dot_nt(wr_ref[...], h_ref[...])
    scores = _sigmoid(logits)
    choice = scores + b_ref[...]
    tm = logits.shape[1]
    giota = lax.broadcasted_iota(jnp.int32, (GROUP_SIZE, tm), 0)
    gs_rows = []
    for g in range(N_GROUPS):
        cg = choice[g * GROUP_SIZE:(g + 1) * GROUP_SIZE, :]
        m1 = jnp.max(cg, axis=0, keepdims=True)
        i1 = jnp.min(jnp.where(cg == m1, giota, GROUP_SIZE), axis=0, keepdims=True)
        m2 = jnp.max(jnp.where(giota == i1, KNOCKED_OUT, cg), axis=0, keepdims=True)
        gs_rows.append(m1 + m2)
    gs = jnp.concatenate(gs_rows, axis=0)
    grow = lax.broadcasted_iota(jnp.int32, (N_GROUPS, tm), 0)
    gmask = jnp.zeros((N_GROUPS, tm), jnp.bool_)
    for _ in range(TOPK_GROUPS):
        mx = jnp.max(gs, axis=0, keepdims=True)
        ix = jnp.min(jnp.where(gs == mx, grow, N_GROUPS), axis=0, keepdims=True)
        hit = grow == ix
        gmask = jnp.logical_or(gmask, hit)
        gs = jnp.where(hit, KNOCKED_OUT, gs)
    gmask_f = jnp.where(gmask, 1.0, 0.0)
    masked = jnp.concatenate(
        [jnp.where(gmask_f[g:g + 1, :] > 0.5, choice[g * GROUP_SIZE:(g + 1) * GROUP_SIZE, :], NEG_INF)
         for g in range(N_GROUPS)], axis=0)
    erow = lax.broadcasted_iota(jnp.int32, (N_EXPERTS, tm), 0)
    idx_rows, w_rows = [], []
    chosen = jnp.zeros((N_EXPERTS, tm), F32)
    for _ in range(TOP_K):
        mx = jnp.max(masked, axis=0, keepdims=True)
        ix = jnp.min(jnp.where(masked == mx, erow, N_EXPERTS), axis=0, keepdims=True)
        hit = erow == ix
        w_rows.append(jnp.sum(jnp.where(hit, scores, 0.0), axis=0, keepdims=True))
        idx_rows.append(ix)
        chosen = jnp.where(hit, 1.0, chosen)
        masked = jnp.where(hit, KNOCKED_OUT, masked)
    w = jnp.concatenate(w_rows, axis=0)
    w = w / (jnp.sum(w, axis=0, keepdims=True) + 1e-20) * ROUTED_SCALE
    e_ref[...] = jnp.concatenate(idx_rows, axis=0)
    w_ref[...] = w
    chosen_b = chosen.astype(BF16)
    earlier = (lax.broadcasted_iota(jnp.int32, (tm, tm), 0)
               < lax.broadcasted_iota(jnp.int32, (tm, tm), 1)).astype(BF16)
    before = _dot(chosen_b, earlier)
    ranks = [jnp.sum(jnp.where(erow == ix, before, 0.0), axis=0, keepdims=True) for ix in idx_rows]
    r_ref[...] = jnp.concatenate(ranks, axis=0).astype(jnp.int32)
    c_ref[...] = _dot(chosen_b, jnp.ones((tm, 128), BF16))


ROUTER_ROWS = 512


def _router(h2, wr_t, bias_col):
    t, d = h2.shape
    tm = ROUTER_ROWS
    by_tile = lambda i: (0, i)
    return pl.pallas_call(
        _router_kernel,
        grid=(t // tm,),
        in_specs=[pl.BlockSpec((tm, d), lambda i: (i, 0)),
                  pl.BlockSpec((N_EXPERTS, d), lambda i: (0, 0)),
                  pl.BlockSpec((N_EXPERTS, 1), lambda i: (0, 0))],
        out_specs=[pl.BlockSpec((TOP_K, tm), by_tile), pl.BlockSpec((TOP_K, tm), by_tile),
                   pl.BlockSpec((TOP_K, tm), by_tile), pl.BlockSpec((N_EXPERTS, 128), by_tile)],
        out_shape=[jax.ShapeDtypeStruct((TOP_K, t), jnp.int32),
                   jax.ShapeDtypeStruct((TOP_K, t), F32),
                   jax.ShapeDtypeStruct((TOP_K, t), jnp.int32),
                   jax.ShapeDtypeStruct((N_EXPERTS, (t // tm) * 128), F32)],
        compiler_params=_cparams(("parallel",)),
        name="router_topk",
    )(h2, wr_t, bias_col)


def _pos_kernel(e_ref, r_ref, base_ref, p_ref):
    tm = e_ref.shape[1]
    erow = lax.broadcasted_iota(jnp.int32, (N_EXPERTS, tm), 0)
    base = base_ref[0]
    rows = [jnp.sum(jnp.where(erow == e_ref[k:k + 1, :], base, 0.0), axis=0, keepdims=True)
            for k in range(TOP_K)]
    p_ref[0] = jnp.concatenate(rows, axis=0).astype(jnp.int32) + r_ref[...]


MOVE_ROWS = 512


def _positions(eidx_t, rank_t, tile_base):
    t = eidx_t.shape[1]
    tm = min(MOVE_ROWS, t)
    per_router_tile = ROUTER_ROWS // tm
    return pl.pallas_call(
        _pos_kernel,
        grid=(t // tm,),
        in_specs=[pl.BlockSpec((TOP_K, tm), lambda i: (0, i)),
                  pl.BlockSpec((TOP_K, tm), lambda i: (0, i)),
                  pl.BlockSpec((1, N_EXPERTS, 1), lambda i: (i // per_router_tile, 0, 0))],
        out_specs=pl.BlockSpec((1, TOP_K, tm), lambda i: (i, 0, 0)),
        out_shape=jax.ShapeDtypeStruct((t // tm, TOP_K, tm), jnp.int32),
        compiler_params=_cparams(("parallel",)),
        name="slot_positions",
    )(eidx_t, rank_t, tile_base)


SLOT_ROWS = 256
PAD_CHUNKS = (128, 64, 32, 16, 8)


def _zero_pads_kernel(pad_start_ref, pad_len_ref, xs_in, xs_hbm, zero_buf, pad_sem):
    del xs_in
    zero_buf[...] = jnp.zeros_like(zero_buf)

    def pad_copies(e, wait):
        start = pad_start_ref[e]
        n = pad_len_ref[e]
        head = jnp.minimum((-start) & (SUBLANES - 1), n)

        def fill(first, size, pred):
            @pl.when(pred)
            def _():
                cp = pltpu.make_async_copy(zero_buf.at[pl.ds(0, size), :], xs_hbm.at[pl.ds(first, size), :], pad_sem)
                if wait:
                    cp.wait()
                else:
                    cp.start()

        for j in range(SUBLANES - 1):
            fill(start + j, 1, j < head)
        ptr = start + head
        rest = n - head
        for chunk in PAD_CHUNKS:
            fill(pl.multiple_of(ptr, SUBLANES), chunk, (rest & chunk) != 0)
            ptr = ptr + (rest & chunk)

    def issue(e, carry):
        pad_copies(e, False)
        return carry

    def drain(e, carry):
        pad_copies(e, True)
        return carry

    lax.fori_loop(0, N_EXPERTS, issue, 0)
    lax.fori_loop(0, N_EXPERTS, drain, 0)


def _zero_pads(pad_start, pad_len, xs):
    grid_spec = pltpu.PrefetchScalarGridSpec(
        num_scalar_prefetch=2,
        grid=(1,),
        in_specs=[pl.BlockSpec(memory_space=pl.ANY)],
        out_specs=pl.BlockSpec(memory_space=pl.ANY),
        scratch_shapes=[pltpu.VMEM((PAD_CHUNKS[0], xs.shape[1]), xs.dtype),
                        pltpu.SemaphoreType.DMA],
    )
    return pl.pallas_call(
        _zero_pads_kernel,
        grid_spec=grid_spec,
        out_shape=jax.ShapeDtypeStruct(xs.shape, xs.dtype),
        input_output_aliases={2: 0},
        compiler_params=_cparams(("arbitrary",)),
        name="zero_pad_slots",
    )(pad_start, pad_len, xs)


SC_DISPATCH_TOKENS = 64


def _sc_dispatch(pos_blocks, h_rows, n_rows):
    info = plsc.get_sparse_core_info()
    n_cores = info.num_cores
    n_workers = n_cores * info.num_subcores
    t = h_rows.shape[0]
    chunk = SC_DISPATCH_TOKENS
    steps = t // (n_workers * chunk)
    mesh = plsc.VectorSubcoreMesh(core_axis_name="c", subcore_axis_name="s")

    @functools.partial(
        pl.kernel, mesh=mesh,
        out_type=jax.ShapeDtypeStruct((n_rows,) + h_rows.shape[1:], h_rows.dtype),
        scratch_types=[pltpu.VMEM((TOP_K, chunk), jnp.int32),
                       pltpu.VMEM((chunk,) + h_rows.shape[1:], h_rows.dtype),
                       pltpu.SemaphoreType.DMA],
        name="sc_dispatch_rows",
    )
    def scatter_rows(pos_hbm, h_hbm, out_hbm, idx_v, rows_v, sem):
        wid = lax.axis_index("s") * n_cores + lax.axis_index("c")

        @pl.loop(0, steps)
        def _(step):
            blk = wid * steps + step
            pltpu.sync_copy(pos_hbm.at[blk], idx_v)
            pltpu.sync_copy(h_hbm.at[pl.ds(blk * chunk, chunk)], rows_v)
            scatters = [pltpu.make_async_copy(rows_v, out_hbm.at[idx_v.at[k]], sem) for k in range(TOP_K)]
            for cp in scatters:
                cp.start()
            for cp in scatters:
                cp.wait()

    return scatter_rows(pos_blocks, h_rows)


WEIGHT_SLOTS = 3


def _experts_kernel(blk_e_ref, nblk_ref, ord_ref, eid_ref, nord_ref, x_ref, w1_hbm, w3_hbm, w2_hbm, y_ref,
                    w1f, w3f, w2f, w1b, w3b, w2b, wsem):
    s = pl.program_id(0)

    def fetch(j):
        slot = j % WEIGHT_SLOTS
        e = eid_ref[j]
        return [pltpu.make_async_copy(src.at[e], dst.at[slot], wsem.at[slot])
                for src, dst in ((w1_hbm, w1f), (w3_hbm, w3f), (w2_hbm, w2f))]

    @pl.when(s < nblk_ref[0])
    def _():
        j = ord_ref[s]

        ahead = WEIGHT_SLOTS - 1

        @pl.when(s == 0)
        def _():
            for first in range(ahead):
                @pl.when(first < nord_ref[0])
                def _(first=first):
                    for cp in fetch(first):
                        cp.start()

        @pl.when(jnp.logical_or(s == 0, blk_e_ref[s] != blk_e_ref[jnp.maximum(s - 1, 0)]))
        def _():
            for cp in fetch(j):
                cp.wait()

            @pl.when(j + ahead < nord_ref[0])
            def _():
                for cp in fetch(j + ahead):
                    cp.start()

            slot = j % WEIGHT_SLOTS
            w1b[...] = w1f[slot].astype(BF16)
            w3b[...] = w3f[slot].astype(BF16)
            w2b[...] = w2f[slot].astype(BF16)

        half = x_ref.shape[1]
        lo, hi = _unpack_halves(x_ref[...])
        lo = lo.astype(BF16)
        hi = hi.astype(BF16)
        h1 = _dot(lo, w1b[:half, :]) + _dot(hi, w1b[half:, :])
        h3 = _dot(lo, w3b[:half, :]) + _dot(hi, w3b[half:, :])
        mid = (_silu(h1) * h3).astype(BF16)
        y_ref[...] = _pack_halves(_dot(mid, w2b[...]))


def _sc_gather(pos_blocks, ys, t):
    info = plsc.get_sparse_core_info()
    n_cores = info.num_cores
    n_workers = n_cores * info.num_subcores
    chunk = SC_DISPATCH_TOKENS
    steps = t // (n_workers * chunk)
    mesh = plsc.VectorSubcoreMesh(core_axis_name="c", subcore_axis_name="s")

    @functools.partial(
        pl.kernel, mesh=mesh,
        out_type=jax.ShapeDtypeStruct((TOP_K * t,) + ys.shape[1:], ys.dtype),
        scratch_types=[pltpu.VMEM((TOP_K, chunk), jnp.int32),
                       pltpu.VMEM((chunk,) + ys.shape[1:], ys.dtype),
                       pltpu.VMEM((chunk,) + ys.shape[1:], ys.dtype),
                       pltpu.SemaphoreType.DMA((2,))],
        name="sc_gather_rows",
    )
    def gather_rows(pos_hbm, ys_hbm, out_hbm, idx_v, rows_a, rows_b, sems):
        wid = lax.axis_index("s") * n_cores + lax.axis_index("c")
        bufs = (rows_a, rows_b)

        @pl.loop(0, steps)
        def _(step):
            blk = wid * steps + step
            pltpu.sync_copy(pos_hbm.at[blk], idx_v)
            gathers = [pltpu.make_async_copy(ys_hbm.at[idx_v.at[k]], bufs[k % 2], sems.at[k % 2])
                       for k in range(TOP_K)]
            gathers[0].start()
            for k in range(TOP_K):
                gathers[k].wait()
                if k + 1 < TOP_K:
                    gathers[k + 1].start()
                pltpu.sync_copy(bufs[k % 2], out_hbm.at[pl.ds(k * t + blk * chunk, chunk)])

    return gather_rows(pos_blocks, ys)


def _experts(blk_e, nblk_used, blk_ord, eid_of_ord, n_ord, xs, w1, w3, w2):
    n_rows, half = xs.shape
    d = D_MODEL
    blk = lambda s, be, nb, bo, eo, no: (jnp.minimum(s, nb[0] - 1), 0)
    grid_spec = pltpu.PrefetchScalarGridSpec(
        num_scalar_prefetch=5,
        grid=(n_rows // SLOT_ROWS,),
        in_specs=[pl.BlockSpec((SLOT_ROWS, half), blk),
                  pl.BlockSpec(memory_space=pl.ANY),
                  pl.BlockSpec(memory_space=pl.ANY),
                  pl.BlockSpec(memory_space=pl.ANY)],
        out_specs=pl.BlockSpec((SLOT_ROWS, half), blk),
        scratch_shapes=[pltpu.VMEM((WEIGHT_SLOTS, d, EXPERT_FF), F32),
                        pltpu.VMEM((WEIGHT_SLOTS, d, EXPERT_FF), F32),
                        pltpu.VMEM((WEIGHT_SLOTS, EXPERT_FF, d), F32),
                        pltpu.VMEM((d, EXPERT_FF), BF16),
                        pltpu.VMEM((d, EXPERT_FF), BF16),
                        pltpu.VMEM((EXPERT_FF, d), BF16),
                        pltpu.SemaphoreType.DMA((WEIGHT_SLOTS,))],
    )
    return pl.pallas_call(
        _experts_kernel,
        grid_spec=grid_spec,
        out_shape=jax.ShapeDtypeStruct((n_rows, half), jnp.uint32),
        compiler_params=_cparams(("arbitrary",)),
        name="routed_experts",
    )(blk_e, nblk_used, blk_ord, eid_of_ord, n_ord, xs, w1, w3, w2)


def _combine_kernel(*refs):
    y_refs = refs[:TOP_K]
    w_ref, h_ref, x1_ref, ws1_ref, ws3_ref, ws2_ref, gt2_ref, g_ref = refs[TOP_K:TOP_K + 8]
    o_ref = refs[-1]
    tm, d = x1_ref.shape
    half = d // 2
    h = h_ref[...]
    mid = (_silu(_dot(h, ws1_ref[...])) * _dot(h, ws3_ref[...])).astype(BF16)
    shared = _dot(mid, ws2_ref[...])
    w = w_ref[...]
    acc_lo = jnp.zeros((tm, half), F32)
    acc_hi = jnp.zeros((tm, half), F32)
    for k in range(TOP_K):
        lo, hi = _unpack_halves(y_refs[k][...])
        acc_lo = acc_lo + lo * w[:, k:k + 1]
        acc_hi = acc_hi + hi * w[:, k:k + 1]
    routed = jnp.concatenate([acc_lo, acc_hi], axis=1)
    x2 = x1_ref[...] + gt2_ref[0] * (routed + shared)
    ms = jnp.mean(x2 * x2, axis=-1, keepdims=True)
    o_ref[...] = x2 * lax.rsqrt(ms + NORM_EPS) * g_ref[...]


COMBINE_ROWS = 256
COMBINE_PARTS = 4


def _combine(y_kt, wts, h2, x1, ws1, ws3, ws2, gt2, g_final, seq, part, n_parts, prev_out):
    t, d = x1.shape
    tm = min(COMBINE_ROWS, seq)
    per_b = seq // tm
    tiles = t // tm // n_parts
    first = part * tiles
    row = lambda i: (first + i, 0)
    full = lambda i: (0, 0)
    y_specs = [pl.BlockSpec((tm, d // 2), functools.partial(lambda i, k: (k * tiles + i, 0), k=k))
               for k in range(TOP_K)]
    in_specs = y_specs + [
        pl.BlockSpec((tm, TOP_K), row),
        pl.BlockSpec((tm, d), row),
        pl.BlockSpec((tm, d), row),
        pl.BlockSpec((d, SHARED_FF), full),
        pl.BlockSpec((d, SHARED_FF), full),
        pl.BlockSpec((SHARED_FF, d), full),
        pl.BlockSpec((1, 1, d), lambda i: ((first + i) // per_b, 0, 0)),
        pl.BlockSpec((1, d), full)]
    args = [y_kt] * TOP_K + [wts, h2, x1, ws1, ws3, ws2, gt2, g_final]
    aliases = {}
    if prev_out is not None:
        in_specs.append(pl.BlockSpec(memory_space=pl.ANY))
        aliases = {len(args): 0}
        args.append(prev_out)
    return pl.pallas_call(
        _combine_kernel,
        grid=(tiles,),
        in_specs=in_specs,
        out_specs=pl.BlockSpec((tm, d), row),
        out_shape=jax.ShapeDtypeStruct((t, d), F32),
        input_output_aliases=aliases,
        compiler_params=_cparams(("parallel",)),
        name="combine_shared_final",
    )(*args)


def _slot_tables(cnt, t):
    ntiles = cnt.shape[1] // 128
    cnt_tile = cnt.reshape(N_EXPERTS, ntiles, 128)[:, :, 0].astype(jnp.int32)
    counts = jnp.sum(cnt_tile, axis=1)
    padded = (counts + SLOT_ROWS - 1) // SLOT_ROWS * SLOT_ROWS
    pstart = jnp.cumsum(padded) - padded
    tile_base = pstart[:, None] + jnp.cumsum(cnt_tile, axis=1) - cnt_tile
    n_blk = -(-(t * TOP_K) // SLOT_ROWS) + N_EXPERTS
    blk_end = jnp.cumsum(padded // SLOT_ROWS)
    blk_e = jnp.sum((blk_end[None, :] <= jnp.arange(n_blk)[:, None]).astype(jnp.int32), axis=1)
    blk_e = jnp.minimum(blk_e, N_EXPERTS - 1)
    owns = (padded > 0).astype(jnp.int32)
    ord_of_e = jnp.cumsum(owns) - owns
    ids = jnp.arange(N_EXPERTS, dtype=jnp.int32)
    eid_of_ord = jnp.sum(jnp.where((ord_of_e[None, :] == ids[:, None]) & (owns[None, :] > 0), ids[None, :], 0), axis=1)
    blk_ord = jnp.sum(jnp.where(blk_e[:, None] == ids[None, :], ord_of_e[None, :], 0), axis=1)
    experts_tables = (blk_e, blk_end[-1:].astype(jnp.int32), blk_ord.astype(jnp.int32),
                      eid_of_ord.astype(jnp.int32), jnp.sum(owns).reshape(1).astype(jnp.int32))
    return (experts_tables, pstart + counts, padded - counts,
            tile_base.T.astype(F32).reshape(ntiles, N_EXPERTS, 1), n_blk * SLOT_ROWS)


def _permute_in_cols(w_in):
    qa, ka, va, qr, kr, vr, gr, ga, gt = jnp.split(
        w_in, np.cumsum((A_WIDTH, A_WIDTH, A_WIDTH, R_QK_WIDTH, R_QK_WIDTH, R_V_WIDTH, R_V_WIDTH,
                         D_MODEL))[:].tolist(), axis=1)
    return jnp.concatenate([vr, gr, ga, gt, qa, ka, va, qr, kr], axis=1)


def kernel(x, c, w_ada, b_ada, g_mix, w_in, w_pa, w_pr, w_out, g_ffn, w_router, router_bias,
           w1, w3, w2, ws1, ws3, ws2, g_final):
    bsz, seq, d = x.shape
    t = bsz * seq
    depth = w_ada.shape[0]
    assert depth == 1, "the final norm is fused into the single layer's last kernel"
    slopes = jnp.exp2(-8.0 / A_HEADS * jnp.arange(1, A_HEADS + 1, dtype=F32))
    x2d = x.reshape(t, d)
    for l in range(depth):
        mod = _ada(c, w_ada[l], b_ada[l])
        sh1, sc1, gt1, sh2, sc2, gt2 = [m.reshape(bsz, 1, d) for m in jnp.split(mod, 6, axis=-1)]
        w_in_p = _permute_in_cols(w_in[l]).astype(BF16)
        proj = _inproj(x2d, g_mix[l].reshape(1, d), sc1, sh1, w_in_p, seq)
        proj3 = proj.reshape(bsz, seq, IN_COLS)
        ya = _moba(proj3, slopes).reshape(t, A_WIDTH)
        yr = _retention(proj3).reshape(t, R_V_WIDTH)
        x1, h2, h2p = _mix(ya, yr, proj, x2d, w_pa[l].astype(BF16), w_pr[l].astype(BF16),
                           w_out[l].astype(BF16), gt1, g_ffn[l].reshape(1, d), sc2, sh2, seq)
        eidx_t, wts_t, rank_t, cnt = _router(h2, w_router[l].T.astype(BF16),
                                             router_bias[l].reshape(N_EXPERTS, 1))
        experts_tables, pad_start, pad_len, tile_base, n_rows = _slot_tables(cnt, t)
        pos3 = _positions(eidx_t, rank_t, tile_base)
        pos_blocks = jnp.transpose(
            pos3.reshape(pos3.shape[0], TOP_K, -1, SC_DISPATCH_TOKENS), (0, 2, 1, 3)
        ).reshape(t // SC_DISPATCH_TOKENS, TOP_K, SC_DISPATCH_TOKENS)
        xs = _zero_pads(pad_start, pad_len, _sc_dispatch(pos_blocks, h2p, n_rows))
        ys = _experts(*experts_tables, xs, w1[l], w3[l], w2[l])
        wts = wts_t.T
        shared_w = (ws1[l].astype(BF16), ws3[l].astype(BF16), ws2[l].astype(BF16))
        blocks_per_part = pos_blocks.shape[0] // COMBINE_PARTS
        x2d = None
        for part in range(COMBINE_PARTS):
            y_kt = _sc_gather(pos_blocks[part * blocks_per_part:(part + 1) * blocks_per_part], ys,
                              t // COMBINE_PARTS)
            x2d = _combine(y_kt, wts, h2, x1, *shared_w, gt2, g_final.reshape(1, d), seq,
                           part, COMBINE_PARTS, x2d)
    return x2d.reshape(bsz, seq, d)
```

```python
import functools

import jax
import jax.numpy as jnp
import numpy as np
from jax import lax
from jax.experimental import pallas as pl
from jax.experimental.pallas import tpu as pltpu
from jax.experimental.pallas import tpu_sc as plsc

F32 = jnp.float32
BF16 = jnp.bfloat16

D_MODEL = 1024
A_HEADS = 8
A_HEAD_DIM = 64
A_WIDTH = A_HEADS * A_HEAD_DIM
MOBA_BLOCK = 256
MOBA_TOPK = 3
R_HEADS = 8
R_QK_DIM = 64
R_V_DIM = 128
R_QK_WIDTH = R_HEADS * R_QK_DIM
R_V_WIDTH = R_HEADS * R_V_DIM
R_CHUNK = 128
N_EXPERTS = 256
TOP_K = 8
N_GROUPS = 8
GROUP_SIZE = N_EXPERTS // N_GROUPS
TOPK_GROUPS = 4
EXPERT_FF = 256
SHARED_FF = 256
ROUTED_SCALE = 2.5
NORM_EPS = 1e-6
GN_EPS = 1e-6
NEG_INF = -1e30
KNOCKED_OUT = -3e38

COL_VR, COL_GR, COL_GA, COL_GT = 0, 1024, 2048, 3072
COL_QA, COL_KA, COL_VA, COL_QR, COL_KR = 4096, 4608, 5120, 5632, 6144
IN_COLS = 6656
AUG = 128
FEAT_BIAS = A_HEAD_DIM
FEAT_POS = A_HEAD_DIM + 32
ALIBI_PIECES = 3
LOG2_E = 1.4426950408889634
V_ROWS = A_HEAD_DIM + 16
MOBA_HEADS_PER_STEP = 8

VMEM_LIMIT = 56 * 1024 * 1024


def _cparams(sem, vmem=VMEM_LIMIT):
    return pltpu.CompilerParams(dimension_semantics=sem, vmem_limit_bytes=vmem)


def _dot(a, b):
    return jnp.dot(a, b, preferred_element_type=F32)


def _dot_nt(a, b):
    return lax.dot_general(a, b, (((1,), (1,)), ((), ())), preferred_element_type=F32)


def _sigmoid(x):
    return 1.0 / (1.0 + jnp.exp(-x))


def _silu(x):
    return x * _sigmoid(x)


def _ada_kernel(c_ref, w_ref, b_ref, o_ref):
    c = c_ref[...]
    s = _silu(c)
    s_hi = s.astype(BF16)
    s_lo = (s - s_hi.astype(F32)).astype(BF16)
    w = w_ref[...]
    w_hi = w.astype(BF16)
    w_lo = (w - w_hi.astype(F32)).astype(BF16)
    o_ref[...] = _dot(s_hi, w_hi) + _dot(s_hi, w_lo) + _dot(s_lo, w_hi) + b_ref[...]


def _ada(c, w_ada, b_ada):
    bsz, d = c.shape
    n = w_ada.shape[1]
    tn = 1024
    return pl.pallas_call(
        _ada_kernel,
        grid=(n // tn,),
        in_specs=[pl.BlockSpec((bsz, d), lambda j: (0, 0)),
                  pl.BlockSpec((d, tn), lambda j: (0, j)),
                  pl.BlockSpec((1, tn), lambda j: (0, j))],
        out_specs=pl.BlockSpec((bsz, tn), lambda j: (0, j)),
        out_shape=jax.ShapeDtypeStruct((bsz, n), F32),
        compiler_params=_cparams(("parallel",)),
        name="ada_mod",
    )(c, w_ada, b_ada.reshape(1, n))


INPROJ_COLS = 512


def _inproj_kernel(x_ref, g_ref, sc_ref, sh_ref, w_ref, o_ref):
    x = x_ref[...]
    ms = jnp.mean(x * x, axis=-1, keepdims=True)
    y = x * lax.rsqrt(ms + NORM_EPS) * g_ref[...]
    h = (y * (1.0 + sc_ref[0]) + sh_ref[0]).astype(BF16)
    for j in range(w_ref.shape[1] // INPROJ_COLS):
        cols = slice(j * INPROJ_COLS, (j + 1) * INPROJ_COLS)
        o_ref[:, cols] = _dot(h, w_ref[:, cols]).astype(BF16)


def _inproj(x2d, g, sc, sh, w_bf16, seq):
    t, d = x2d.shape
    n = w_bf16.shape[1]
    tm = min(512, seq)
    per_b = seq // tm
    return pl.pallas_call(
        _inproj_kernel,
        grid=(t // tm,),
        in_specs=[pl.BlockSpec((tm, d), lambda i: (i, 0)),
                  pl.BlockSpec((1, d), lambda i: (0, 0)),
                  pl.BlockSpec((1, 1, d), lambda i: (i // per_b, 0, 0)),
                  pl.BlockSpec((1, 1, d), lambda i: (i // per_b, 0, 0)),
                  pl.BlockSpec((d, n), lambda i: (0, 0))],
        out_specs=pl.BlockSpec((tm, n), lambda i: (i, 0)),
        out_shape=jax.ShapeDtypeStruct((t, n), BF16),
        compiler_params=_cparams(("parallel",)),
        name="norm_inproj",
    )(x2d, g, sc, sh, w_bf16)


def _moba_prepare(i, slopes_ref, q_ref, k_ref, v_ref, ko_ref, qo_ref, vo_ref, kmean_scr):
    nblk = kmean_scr.shape[0]
    width = q_ref.shape[1]
    seq_rows = pl.ds(pl.multiple_of(i * MOBA_BLOCK, MOBA_BLOCK), MOBA_BLOCK)
    q = q_ref[...]
    k = k_ref[...]
    v = v_ref[...]
    kmean_scr[pl.ds(i, 1), :] = jnp.mean(k.astype(F32), axis=0, keepdims=True)

    eye = (lax.broadcasted_iota(jnp.int32, (width, width), 0)
           == lax.broadcasted_iota(jnp.int32, (width, width), 1)).astype(BF16)
    q_t = _dot_nt(eye, q)
    v_t = _dot_nt(eye, v)

    km = kmean_scr[...]
    km_rep = jnp.concatenate([km] * A_HEADS, axis=0)
    r_head = lax.broadcasted_iota(jnp.int32, km_rep.shape, 0) // nblk
    c_head = lax.broadcasted_iota(jnp.int32, km_rep.shape, 1) // A_HEAD_DIM
    km_bd = jnp.where(r_head == c_head, km_rep, 0.0)
    km_hi = km_bd.astype(BF16)
    km_lo = (km_bd - km_hi.astype(F32)).astype(BF16)
    q_t_b = q_t.astype(BF16)
    gate_all = _dot(km_hi, q_t_b) + _dot(km_lo, q_t_b)

    mb = q.shape[0]
    blk = lax.broadcasted_iota(jnp.int32, (nblk, mb), 0)
    lane_pos = lax.broadcasted_iota(jnp.int32, (16, mb), 1).astype(F32)
    row16 = lax.broadcasted_iota(jnp.int32, (16, mb), 0)
    key_pos = lax.broadcasted_iota(jnp.int32, (mb, AUG), 0).astype(F32)
    kcol = lax.broadcasted_iota(jnp.int32, (mb, AUG), 1)
    sel_r = lax.broadcasted_iota(jnp.int32, (width, AUG), 0)
    sel_c = lax.broadcasted_iota(jnp.int32, (width, AUG), 1)

    blk_first = (i * mb).astype(F32)
    for h in range(A_HEADS):
        pieces = [slopes_ref[h * ALIBI_PIECES + c] for c in range(ALIBI_PIECES)]
        g = jnp.where(blk < i, gate_all[h * nblk:(h + 1) * nblk, :], NEG_INF)
        sel = jnp.zeros((nblk, mb), jnp.bool_)
        for r in range(MOBA_TOPK):
            m = jnp.max(g, axis=0, keepdims=True)
            idx = jnp.min(jnp.where(g == m, blk, nblk), axis=0, keepdims=True)
            hit = blk == idx
            sel = jnp.logical_or(sel, jnp.logical_and(hit, r < i))
            g = jnp.where(hit, KNOCKED_OUT, g)
        bias_t = jnp.where(sel, 0.0, NEG_INF)

        scale = A_HEAD_DIM ** -0.5 * LOG2_E
        qo_ref[0, h, 0:A_HEAD_DIM, :] = (q_t[h * A_HEAD_DIM:(h + 1) * A_HEAD_DIM, :] * scale).astype(BF16)
        qo_ref[0, h, FEAT_BIAS:FEAT_BIAS + nblk, :] = bias_t.astype(BF16)
        if nblk < 32:
            qo_ref[0, h, FEAT_BIAS + nblk:FEAT_POS, :] = jnp.zeros((32 - nblk, mb), BF16)
        piece_rows = jnp.where(row16 % ALIBI_PIECES == 0, pieces[0],
                               jnp.where(row16 % ALIBI_PIECES == 1, pieces[1], pieces[2]))
        pos_feat = jnp.where(row16 < 3, -lane_pos,
                             jnp.where(row16 < 6, piece_rows,
                                       jnp.where(row16 < 9, -blk_first, jnp.where(row16 < 12, piece_rows, 0.0))))
        qo_ref[0, h, FEAT_POS:FEAT_POS + 16, :] = pos_feat.astype(BF16)
        qo_ref[0, h, FEAT_POS + 16:AUG, :] = jnp.zeros((AUG - FEAT_POS - 16, mb), BF16)

        vo_ref[0, h, 0:A_HEAD_DIM, seq_rows] = v_t[h * A_HEAD_DIM:(h + 1) * A_HEAD_DIM, :].astype(BF16)
        vo_ref[0, h, A_HEAD_DIM:V_ROWS, seq_rows] = jnp.where(row16 == 0, 1.0, 0.0).astype(BF16)

        pick = jnp.where(jnp.logical_and(sel_r == sel_c + h * A_HEAD_DIM, sel_c < A_HEAD_DIM),
                         1.0, 0.0).astype(BF16)
        pos_col = kcol - FEAT_POS
        piece_cols = jnp.where(pos_col % ALIBI_PIECES == 0, pieces[0],
                               jnp.where(pos_col % ALIBI_PIECES == 1, pieces[1], pieces[2]))
        k_feat = jnp.where(
            kcol == FEAT_BIAS + i, 1.0,
            jnp.where(pos_col < 0, 0.0,
                      jnp.where(pos_col < 3, piece_cols,
                                jnp.where(pos_col < 6, key_pos,
                                          jnp.where(pos_col < 9, piece_cols,
                                                    jnp.where(pos_col < 12, blk_first, 0.0))))))
        ko_ref[0, h, seq_rows, :] = (_dot(k, pick) + k_feat).astype(BF16)


def _moba_attend(i, q_ref, k_ref, v_ref, o_ref, s_a, s_b, group, n_groups):
    mb = MOBA_BLOCK
    span = group * mb
    own = pl.multiple_of(i * mb, mb)
    key_i = lax.broadcasted_iota(jnp.int32, (mb, mb), 0)
    qry_i = lax.broadcasted_iota(jnp.int32, (mb, mb), 1)
    feat = lax.broadcasted_iota(jnp.int32, (AUG, mb), 0)
    is_bias = jnp.logical_and(feat >= FEAT_BIAS, feat < FEAT_POS)
    q_ts, carry0 = [], []
    for hh in range(MOBA_HEADS_PER_STEP):
        q_t = q_ref[0, hh]
        q_ts.append(q_t)
        q_own = jnp.where(is_bias, jnp.zeros_like(q_t), q_t)
        s = _dot(k_ref[0, hh, pl.ds(own, mb), :], q_own)
        s = jnp.where(key_i <= qry_i, s, NEG_INF)
        m0 = jnp.max(s, axis=0, keepdims=True)
        p = jnp.exp2(s - m0)
        carry0 += [m0, _dot(v_ref[0, hh, :, pl.ds(own, mb)], p.astype(BF16))]

    def scores(g, dst):
        start = pl.multiple_of(jnp.minimum(g, n_groups - 1) * span, span)
        for hh in range(MOBA_HEADS_PER_STEP):
            dst[hh] = _dot(k_ref[0, hh, pl.ds(start, span), :], q_ts[hh])

    def absorb(g, src, carry):
        start = pl.multiple_of(g * span, span)
        new = []
        for hh in range(MOBA_HEADS_PER_STEP):
            m, acc = carry[2 * hh], carry[2 * hh + 1]
            sb = src[hh]
            m_new = jnp.maximum(m, jnp.max(sb, axis=0, keepdims=True))
            pb = jnp.exp2(sb - m_new)
            alpha = jnp.exp2(m - m_new)
            acc = acc * alpha + _dot(v_ref[0, hh, :, pl.ds(start, span)], pb.astype(BF16))
            new += [m_new, acc]
        return tuple(new)

    def body(pair, carry):
        scores(2 * pair + 1, s_b)
        carry = absorb(2 * pair, s_a, carry)
        scores(2 * pair + 2, s_a)
        return absorb(2 * pair + 1, s_b, carry)

    scores(0, s_a)
    live_groups = (i + group - 1) // group
    res = lax.fori_loop(0, live_groups // 2, body, tuple(carry0))
    res = lax.cond(live_groups % 2 == 1, lambda c: absorb(live_groups - 1, s_a, c), lambda c: c, res)
    outs = [res[2 * hh + 1][0:A_HEAD_DIM, :] / res[2 * hh + 1][A_HEAD_DIM:A_HEAD_DIM + 1, :] for hh in range(MOBA_HEADS_PER_STEP)]
    o_t = jnp.concatenate(outs, axis=0).astype(BF16)
    eye = (key_i == qry_i).astype(BF16)
    o_ref[0] = _dot_nt(eye, o_t).astype(BF16)


def _moba_kernel(slopes_ref, q_ref, k_ref, v_ref, o_ref, kmean_scr, q_scr, k_scr, v_scr, s_a, s_b,
                 *, group, n_groups):
    i = pl.program_id(1)

    @pl.when(i == 0)
    def _():
        kmean_scr[...] = jnp.zeros_like(kmean_scr)
        k_scr[...] = jnp.zeros_like(k_scr)
        v_scr[...] = jnp.zeros_like(v_scr)

    _moba_prepare(i, slopes_ref, q_ref, k_ref, v_ref, k_scr, q_scr, v_scr, kmean_scr)
    _moba_attend(i, q_scr, k_scr, v_scr, o_ref, s_a, s_b, group, n_groups)


def _moba(proj3, slopes):
    bsz, seq, _ = proj3.shape
    mb = MOBA_BLOCK
    nblk = seq // mb
    group = min(2, nblk)
    n_groups = nblk // group
    assert MOBA_HEADS_PER_STEP == A_HEADS
    grid_spec = pltpu.PrefetchScalarGridSpec(
        num_scalar_prefetch=1,
        grid=(bsz, nblk),
        in_specs=[pl.BlockSpec((None, mb, A_WIDTH), lambda b, i, s: (b, i, COL_QA // A_WIDTH)),
                  pl.BlockSpec((None, mb, A_WIDTH), lambda b, i, s: (b, i, COL_KA // A_WIDTH)),
                  pl.BlockSpec((None, mb, A_WIDTH), lambda b, i, s: (b, i, COL_VA // A_WIDTH))],
        out_specs=pl.BlockSpec((1, mb, A_WIDTH), lambda b, i, s: (b, i, 0)),
        scratch_shapes=[pltpu.VMEM((nblk, A_WIDTH), F32),
                        pltpu.VMEM((1, A_HEADS, AUG, mb), BF16),
                        pltpu.VMEM((1, A_HEADS, seq, AUG), BF16),
                        pltpu.VMEM((1, A_HEADS, V_ROWS, seq), BF16),
                        pltpu.VMEM((A_HEADS, group * mb, mb), F32),
                        pltpu.VMEM((A_HEADS, group * mb, mb), F32)],
    )
    return pl.pallas_call(
        functools.partial(_moba_kernel, group=group, n_groups=n_groups),
        grid_spec=grid_spec,
        out_shape=jax.ShapeDtypeStruct((bsz, seq, A_WIDTH), BF16),
        compiler_params=_cparams(("parallel", "arbitrary")),
        name="moba_attention",
    )(slopes, proj3, proj3, proj3)


def _retention_kernel(cdec_ref, q_ref, k_ref, v_ref, g_ref, decay_ref, qdec_ref, kdec_ref, o_ref, state_scr):
    @pl.when(pl.program_id(0) == 0)
    def _():
        state_scr[...] = jnp.zeros_like(state_scr)

    width = q_ref.shape[2]
    eye = (lax.broadcasted_iota(jnp.int32, (width, width), 0)
           == lax.broadcasted_iota(jnp.int32, (width, width), 1)).astype(BF16)
    for b in range(q_ref.shape[0]):
        q = q_ref[b]
        k_t = _dot_nt(eye, k_ref[b])
        k_t_b = k_t.astype(BF16)
        state_b = state_scr[b].astype(BF16)
        for h in range(R_HEADS):
            rows = slice(h * R_QK_DIM, (h + 1) * R_QK_DIM)
            cols = slice(h * R_V_DIM, (h + 1) * R_V_DIM)
            q_h = q[:, rows]
            v_h = v_ref[b, :, cols]
            inner = _dot(q_h, k_t_b[rows, :]) * decay_ref[h]
            out = _dot(inner.astype(BF16), v_h) + _dot(q_h, state_b[rows, :]) * qdec_ref[h]
            k_dec = (k_t[rows, :] * kdec_ref[h]).astype(BF16)
            state_scr[b, rows, :] = cdec_ref[h] * state_scr[b, rows, :] + _dot(k_dec, v_h)
            mu = jnp.mean(out, axis=-1, keepdims=True)
            cen = out - mu
            var = jnp.mean(cen * cen, axis=-1, keepdims=True)
            y = cen * lax.rsqrt(var + GN_EPS)
            o_ref[b, :, cols] = (y * _silu(g_ref[b, :, cols].astype(F32))).astype(BF16)


def _retention_consts():
    h = np.arange(R_HEADS, dtype=np.float64)
    log_g = np.log(1.0 - np.exp2(-5.0 - h))
    n = np.arange(R_CHUNK, dtype=np.float64)
    diff = n[:, None] - n[None, :]
    scale = R_QK_DIM ** -0.5
    decay = np.where(diff >= 0, np.exp(np.maximum(diff, 0.0) * log_g[:, None, None]), 0.0) * scale
    q_decay = np.exp((n + 1.0) * log_g[:, None])[:, :, None]
    k_decay = np.exp((R_CHUNK - 1.0 - n) * log_g[:, None])[:, None, :] * scale
    chunk_decay = np.exp(R_CHUNK * log_g)
    return (jnp.asarray(decay, F32), jnp.asarray(q_decay, F32), jnp.asarray(k_decay, F32),
            jnp.asarray(chunk_decay, F32))


def _retention(proj3):
    bsz, seq, _ = proj3.shape
    c = R_CHUNK
    decay, qdec, kdec, cdec = _retention_consts()
    grid_spec = pltpu.PrefetchScalarGridSpec(
        num_scalar_prefetch=1,
        grid=(seq // c,),
        in_specs=[pl.BlockSpec((bsz, c, R_QK_WIDTH), lambda i, s: (0, i, COL_QR // R_QK_WIDTH)),
                  pl.BlockSpec((bsz, c, R_QK_WIDTH), lambda i, s: (0, i, COL_KR // R_QK_WIDTH)),
                  pl.BlockSpec((bsz, c, R_V_WIDTH), lambda i, s: (0, i, COL_VR // R_V_WIDTH)),
                  pl.BlockSpec((bsz, c, R_V_WIDTH), lambda i, s: (0, i, COL_GR // R_V_WIDTH)),
                  pl.BlockSpec((R_HEADS, c, c), lambda i, s: (0, 0, 0)),
                  pl.BlockSpec((R_HEADS, c, 1), lambda i, s: (0, 0, 0)),
                  pl.BlockSpec((R_HEADS, 1, c), lambda i, s: (0, 0, 0))],
        out_specs=pl.BlockSpec((bsz, c, R_V_WIDTH), lambda i, s: (0, i, 0)),
        scratch_shapes=[pltpu.VMEM((bsz, R_QK_WIDTH, R_V_DIM), F32)],
    )
    return pl.pallas_call(
        _retention_kernel,
        grid_spec=grid_spec,
        out_shape=jax.ShapeDtypeStruct((bsz, seq, R_V_WIDTH), BF16),
        compiler_params=_cparams(("arbitrary",)),
        name="retention",
    )(cdec, proj3, proj3, proj3, proj3, decay, qdec, kdec)


def _pack_halves(x):
    w = x.shape[1] // 2
    bits = lax.bitcast_convert_type(x.astype(BF16).astype(F32), jnp.uint32)
    return (bits[:, :w] >> 16) | (bits[:, w:] & jnp.uint32(0xFFFF0000))


def _unpack_halves(p):
    lo = lax.bitcast_convert_type(p << 16, F32)
    hi = lax.bitcast_convert_type(p & jnp.uint32(0xFFFF0000), F32)
    return lo, hi


SUBLANES = 8


def _mix_kernel(ya_ref, yr_ref, ga_ref, gt_ref, x_ref, wpa_ref, wpr_ref, wout_ref,
                gt1_ref, g_ref, sc_ref, sh_ref, x1_ref, h2_ref, h2p_ref):
    a = _dot(ya_ref[...], wpa_ref[...]) * _sigmoid(ga_ref[...].astype(F32))
    r = _dot(yr_ref[...], wpr_ref[...]) * _sigmoid(gt_ref[...].astype(F32))
    mix = _dot((a + r).astype(BF16), wout_ref[...])
    x1 = x_ref[...] + gt1_ref[0] * mix
    x1_ref[...] = x1
    ms = jnp.mean(x1 * x1, axis=-1, keepdims=True)
    y = x1 * lax.rsqrt(ms + NORM_EPS) * g_ref[...]
    h2 = y * (1.0 + sc_ref[0]) + sh_ref[0]
    h2_ref[...] = h2.astype(BF16)
    h2p_ref[...] = _pack_halves(h2)


def _mix(ya, yr, proj, x2d, wpa, wpr, wout, gt1, g, sc, sh, seq):
    t, d = x2d.shape
    tm = min(512, seq)
    per_b = seq // tm
    row = lambda i: (i, 0)
    full = lambda i: (0, 0)
    per_batch = lambda i: (i // per_b, 0, 0)
    return pl.pallas_call(
        _mix_kernel,
        grid=(t // tm,),
        in_specs=[pl.BlockSpec((tm, A_WIDTH), row),
                  pl.BlockSpec((tm, R_V_WIDTH), row),
                  pl.BlockSpec((tm, d), lambda i: (i, COL_GA // D_MODEL)),
                  pl.BlockSpec((tm, d), lambda i: (i, COL_GT // D_MODEL)),
                  pl.BlockSpec((tm, d), row),
                  pl.BlockSpec((A_WIDTH, d), full),
                  pl.BlockSpec((R_V_WIDTH, d), full),
                  pl.BlockSpec((d, d), full),
                  pl.BlockSpec((1, 1, d), per_batch),
                  pl.BlockSpec((1, d), full),
                  pl.BlockSpec((1, 1, d), per_batch),
                  pl.BlockSpec((1, 1, d), per_batch)],
        out_specs=[pl.BlockSpec((tm, d), row), pl.BlockSpec((tm, d), row),
                   pl.BlockSpec((tm, d // 2), row)],
        out_shape=[jax.ShapeDtypeStruct((t, d), F32), jax.ShapeDtypeStruct((t, d), BF16),
                   jax.ShapeDtypeStruct((t, d // 2), jnp.uint32)],
        compiler_params=_cparams(("parallel",)),
        name="merge_outproj_norm",
    )(ya, yr, proj, proj, x2d, wpa, wpr, wout, gt1, g, sc, sh)


def _router_kernel(h_ref, wr_ref, b_ref, e_ref, w_ref, r_ref, c_ref):
    logits = _dot_nt(wr_ref[...], h_ref[...])
    scores = _sigmoid(logits)
    choice = scores + b_ref[...]
    tm = logits.shape[1]
    giota = lax.broadcasted_iota(jnp.int32, (GROUP_SIZE, tm), 0)
    gs_rows = []
    for g in range(N_GROUPS):
        cg = choice[g * GROUP_SIZE:(g + 1) * GROUP_SIZE, :]
        m1 = jnp.max(cg, axis=0, keepdims=True)
        i1 = jnp.min(jnp.where(cg == m1, giota, GROUP_SIZE), axis=0, keepdims=True)
        m2 = jnp.max(jnp.where(giota == i1, KNOCKED_OUT, cg), axis=0, keepdims=True)
        gs_rows.append(m1 + m2)
    gs = jnp.concatenate(gs_rows, axis=0)
    grow = lax.broadcasted_iota(jnp.int32, (N_GROUPS, tm), 0)
    gmask = jnp.zeros((N_GROUPS, tm), jnp.bool_)
    for _ in range(TOPK_GROUPS):
        mx = jnp.max(gs, axis=0, keepdims=True)
        ix = jnp.min(jnp.where(gs == mx, grow, N_GROUPS), axis=0, keepdims=True)
        hit = grow == ix
        gmask = jnp.logical_or(gmask, hit)
        gs = jnp.where(hit, KNOCKED_OUT, gs)
    gmask_f = jnp.where(gmask, 1.0, 0.0)
    masked = jnp.concatenate(
        [jnp.where(gmask_f[g:g + 1, :] > 0.5, choice[g * GROUP_SIZE:(g + 1) * GROUP_SIZE, :], NEG_INF)
         for g in range(N_GROUPS)], axis=0)
    erow = lax.broadcasted_iota(jnp.int32, (N_EXPERTS, tm), 0)
    idx_rows, w_rows = [], []
    chosen = jnp.zeros((N_EXPERTS, tm), F32)
    for _ in range(TOP_K):
        mx = jnp.max(masked, axis=0, keepdims=True)
        ix = jnp.min(jnp.where(masked == mx, erow, N_EXPERTS), axis=0, keepdims=True)
        hit = erow == ix
        w_rows.append(jnp.sum(jnp.where(hit, scores, 0.0), axis=0, keepdims=True))
        idx_rows.append(ix)
        chosen = jnp.where(hit, 1.0, chosen)
        masked = jnp.where(hit, KNOCKED_OUT, masked)
    w = jnp.concatenate(w_rows, axis=0)
    w = w / (jnp.sum(w, axis=0, keepdims=True) + 1e-20) * ROUTED_SCALE
    e_ref[...] = jnp.concatenate(idx_rows, axis=0)
    w_ref[...] = w
    chosen_b = chosen.astype(BF16)
    earlier = (lax.broadcasted_iota(jnp.int32, (tm, tm), 0)
               < lax.broadcasted_iota(jnp.int32, (tm, tm), 1)).astype(BF16)
    before = _dot(chosen_b, earlier)
    ranks = [jnp.sum(jnp.where(erow == ix, before, 0.0), axis=0, keepdims=True) for ix in idx_rows]
    r_ref[...] = jnp.concatenate(ranks, axis=0).astype(jnp.int32)
    c_ref[...] = _dot(chosen_b, jnp.ones((tm, 128), BF16))


ROUTER_ROWS = 512


def _router(h2, wr_t, bias_col):
    t, d = h2.shape
    tm = ROUTER_ROWS
    by_tile = lambda i: (0, i)
    return pl.pallas_call(
        _router_kernel,
        grid=(t // tm,),
        in_specs=[pl.BlockSpec((tm, d), lambda i: (i, 0)),
                  pl.BlockSpec((N_EXPERTS, d), lambda i: (0, 0)),
                  pl.BlockSpec((N_EXPERTS, 1), lambda i: (0, 0))],
        out_specs=[pl.BlockSpec((TOP_K, tm), by_tile), pl.BlockSpec((TOP_K, tm), by_tile),
                   pl.BlockSpec((TOP_K, tm), by_tile), pl.BlockSpec((N_EXPERTS, 128), by_tile)],
        out_shape=[jax.ShapeDtypeStruct((TOP_K, t), jnp.int32),
                   jax.ShapeDtypeStruct((TOP_K, t), F32),
                   jax.ShapeDtypeStruct((TOP_K, t), jnp.int32),
                   jax.ShapeDtypeStruct((N_EXPERTS, (t // tm) * 128), F32)],
        compiler_params=_cparams(("parallel",)),
        name="router_topk",
    )(h2, wr_t, bias_col)


def _pos_kernel(e_ref, r_ref, base_ref, p_ref):
    tm = e_ref.shape[1]
    erow = lax.broadcasted_iota(jnp.int32, (N_EXPERTS, tm), 0)
    base = base_ref[0]
    rows = [jnp.sum(jnp.where(erow == e_ref[k:k + 1, :], base, 0.0), axis=0, keepdims=True)
            for k in range(TOP_K)]
    p_ref[0] = jnp.concatenate(rows, axis=0).astype(jnp.int32) + r_ref[...]


MOVE_ROWS = 512


def _positions(eidx_t, rank_t, tile_base):
    t = eidx_t.shape[1]
    tm = min(MOVE_ROWS, t)
    per_router_tile = ROUTER_ROWS // tm
    return pl.pallas_call(
        _pos_kernel,
        grid=(t // tm,),
        in_specs=[pl.BlockSpec((TOP_K, tm), lambda i: (0, i)),
                  pl.BlockSpec((TOP_K, tm), lambda i: (0, i)),
                  pl.BlockSpec((1, N_EXPERTS, 1), lambda i: (i // per_router_tile, 0, 0))],
        out_specs=pl.BlockSpec((1, TOP_K, tm), lambda i: (i, 0, 0)),
        out_shape=jax.ShapeDtypeStruct((t // tm, TOP_K, tm), jnp.int32),
        compiler_params=_cparams(("parallel",)),
        name="slot_positions",
    )(eidx_t, rank_t, tile_base)


SLOT_ROWS = 512
PAD_CHUNKS = (256, 128, 64, 32, 16, 8)


def _zero_pads_kernel(pad_start_ref, pad_len_ref, xs_in, xs_hbm, zero_buf, pad_sem):
    del xs_in
    zero_buf[...] = jnp.zeros_like(zero_buf)

    def pad_copies(e, wait):
        start = pad_start_ref[e]
        n = pad_len_ref[e]
        head = jnp.minimum((-start) & (SUBLANES - 1), n)

        def fill(first, size, pred):
            @pl.when(pred)
            def _():
                cp = pltpu.make_async_copy(zero_buf.at[pl.ds(0, size), :], xs_hbm.at[pl.ds(first, size), :], pad_sem)
                if wait:
                    cp.wait()
                else:
                    cp.start()

        for j in range(SUBLANES - 1):
            fill(start + j, 1, j < head)
        ptr = start + head
        rest = n - head
        for chunk in PAD_CHUNKS:
            fill(pl.multiple_of(ptr, SUBLANES), chunk, (rest & chunk) != 0)
            ptr = ptr + (rest & chunk)

    def issue(e, carry):
        pad_copies(e, False)
        return carry

    def drain(e, carry):
        pad_copies(e, True)
        return carry

    lax.fori_loop(0, N_EXPERTS, issue, 0)
    lax.fori_loop(0, N_EXPERTS, drain, 0)


def _zero_pads(pad_start, pad_len, xs):
    grid_spec = pltpu.PrefetchScalarGridSpec(
        num_scalar_prefetch=2,
        grid=(1,),
        in_specs=[pl.BlockSpec(memory_space=pl.ANY)],
        out_specs=pl.BlockSpec(memory_space=pl.ANY),
        scratch_shapes=[pltpu.VMEM((PAD_CHUNKS[0], xs.shape[1]), xs.dtype),
                        pltpu.SemaphoreType.DMA],
    )
    return pl.pallas_call(
        _zero_pads_kernel,
        grid_spec=grid_spec,
        out_shape=jax.ShapeDtypeStruct(xs.shape, xs.dtype),
        input_output_aliases={2: 0},
        compiler_params=_cparams(("arbitrary",)),
        name="zero_pad_slots",
    )(pad_start, pad_len, xs)


SC_DISPATCH_TOKENS = 64


def _sc_dispatch(pos_blocks, h_rows, n_rows):
    info = plsc.get_sparse_core_info()
    n_cores = info.num_cores
    n_workers = n_cores * info.num_subcores
    t = h_rows.shape[0]
    chunk = SC_DISPATCH_TOKENS
    steps = t // (n_workers * chunk)
    mesh = plsc.VectorSubcoreMesh(core_axis_name="c", subcore_axis_name="s")

    @functools.partial(
        pl.kernel, mesh=mesh,
        out_type=jax.ShapeDtypeStruct((n_rows,) + h_rows.shape[1:], h_rows.dtype),
        scratch_types=[pltpu.VMEM((TOP_K, chunk), jnp.int32),
                       pltpu.VMEM((chunk,) + h_rows.shape[1:], h_rows.dtype),
                       pltpu.SemaphoreType.DMA],
        name="sc_dispatch_rows",
    )
    def scatter_rows(pos_hbm, h_hbm, out_hbm, idx_v, rows_v, sem):
        wid = lax.axis_index("s") * n_cores + lax.axis_index("c")

        @pl.loop(0, steps)
        def _(step):
            blk = wid * steps + step
            pltpu.sync_copy(pos_hbm.at[blk], idx_v)
            pltpu.sync_copy(h_hbm.at[pl.ds(blk * chunk, chunk)], rows_v)
            scatters = [pltpu.make_async_copy(rows_v, out_hbm.at[idx_v.at[k]], sem) for k in range(TOP_K)]
            for cp in scatters:
                cp.start()
            for cp in scatters:
                cp.wait()

    return scatter_rows(pos_blocks, h_rows)


WEIGHT_SLOTS = 4


def _experts_kernel(blk_e_ref, nblk_ref, ord_ref, eid_ref, nord_ref, x_ref, w1_hbm, w3_hbm, w2_hbm, y_ref,
                    w1f, w3f, w2f, w1b, w3b, w2b, wsem):
    s = pl.program_id(0)

    def fetch(j):
        slot = j % WEIGHT_SLOTS
        e = eid_ref[j]
        return [pltpu.make_async_copy(src.at[e], dst.at[slot], wsem.at[slot])
                for src, dst in ((w1_hbm, w1f), (w3_hbm, w3f), (w2_hbm, w2f))]

    @pl.when(s < nblk_ref[0])
    def _():
        j = ord_ref[s]

        ahead = WEIGHT_SLOTS - 1

        @pl.when(s == 0)
        def _():
            for first in range(ahead):
                @pl.when(first < nord_ref[0])
                def _(first=first):
                    for cp in fetch(first):
                        cp.start()

        @pl.when(jnp.logical_or(s == 0, blk_e_ref[s] != blk_e_ref[jnp.maximum(s - 1, 0)]))
        def _():
            for cp in fetch(j):
                cp.wait()

            @pl.when(j + ahead < nord_ref[0])
            def _():
                for cp in fetch(j + ahead):
                    cp.start()

            slot = j % WEIGHT_SLOTS
            w1b[...] = w1f[slot].astype(BF16)
            w3b[...] = w3f[slot].astype(BF16)
            w2b[...] = w2f[slot].astype(BF16)

        half = x_ref.shape[1]
        lo, hi = _unpack_halves(x_ref[...])
        lo = lo.astype(BF16)
        hi = hi.astype(BF16)
        h1 = _dot(lo, w1b[:half, :]) + _dot(hi, w1b[half:, :])
        h3 = _dot(lo, w3b[:half, :]) + _dot(hi, w3b[half:, :])
        mid = (_silu(h1) * h3).astype(BF16)
        y_ref[...] = _pack_halves(_dot(mid, w2b[...]))


def _sc_gather(pos_blocks, ys, t):
    info = plsc.get_sparse_core_info()
    n_cores = info.num_cores
    n_workers = n_cores * info.num_subcores
    chunk = SC_DISPATCH_TOKENS
    steps = t // (n_workers * chunk)
    mesh = plsc.VectorSubcoreMesh(core_axis_name="c", subcore_axis_name="s")

    @functools.partial(
        pl.kernel, mesh=mesh,
        out_type=jax.ShapeDtypeStruct((TOP_K * t,) + ys.shape[1:], ys.dtype),
        scratch_types=[pltpu.VMEM((TOP_K, chunk), jnp.int32),
                       pltpu.VMEM((chunk,) + ys.shape[1:], ys.dtype),
                       pltpu.VMEM((chunk,) + ys.shape[1:], ys.dtype),
                       pltpu.SemaphoreType.DMA((2,))],
        name="sc_gather_rows",
    )
    def gather_rows(pos_hbm, ys_hbm, out_hbm, idx_v, rows_a, rows_b, sems):
        wid = lax.axis_index("s") * n_cores + lax.axis_index("c")
        bufs = (rows_a, rows_b)

        @pl.loop(0, steps)
        def _(step):
            blk = wid * steps + step
            pltpu.sync_copy(pos_hbm.at[blk], idx_v)
            gathers = [pltpu.make_async_copy(ys_hbm.at[idx_v.at[k]], bufs[k % 2], sems.at[k % 2])
                       for k in range(TOP_K)]
            gathers[0].start()
            for k in range(TOP_K):
                gathers[k].wait()
                if k + 1 < TOP_K:
                    gathers[k + 1].start()
                pltpu.sync_copy(bufs[k % 2], out_hbm.at[pl.ds(k * t + blk * chunk, chunk)])

    return gather_rows(pos_blocks, ys)


def _experts(blk_e, nblk_used, blk_ord, eid_of_ord, n_ord, xs, w1, w3, w2):
    n_rows, half = xs.shape
    d = D_MODEL
    blk = lambda s, be, nb, bo, eo, no: (jnp.minimum(s, nb[0] - 1), 0)
    grid_spec = pltpu.PrefetchScalarGridSpec(
        num_scalar_prefetch=5,
        grid=(n_rows // SLOT_ROWS,),
        in_specs=[pl.BlockSpec((SLOT_ROWS, half), blk),
                  pl.BlockSpec(memory_space=pl.ANY),
                  pl.BlockSpec(memory_space=pl.ANY),
                  pl.BlockSpec(memory_space=pl.ANY)],
        out_specs=pl.BlockSpec((SLOT_ROWS, half), blk),
        scratch_shapes=[pltpu.VMEM((WEIGHT_SLOTS, d, EXPERT_FF), F32),
                        pltpu.VMEM((WEIGHT_SLOTS, d, EXPERT_FF), F32),
                        pltpu.VMEM((WEIGHT_SLOTS, EXPERT_FF, d), F32),
                        pltpu.VMEM((d, EXPERT_FF), BF16),
                        pltpu.VMEM((d, EXPERT_FF), BF16),
                        pltpu.VMEM((EXPERT_FF, d), BF16),
                        pltpu.SemaphoreType.DMA((WEIGHT_SLOTS,))],
    )
    return pl.pallas_call(
        _experts_kernel,
        grid_spec=grid_spec,
        out_shape=jax.ShapeDtypeStruct((n_rows, half), jnp.uint32),
        compiler_params=_cparams(("arbitrary",)),
        name="routed_experts",
    )(blk_e, nblk_used, blk_ord, eid_of_ord, n_ord, xs, w1, w3, w2)


def _combine_kernel(*refs):
    y_refs = refs[:TOP_K]
    w_ref, h_ref, x1_ref, ws1_ref, ws3_ref, ws2_ref, gt2_ref, g_ref = refs[TOP_K:TOP_K + 8]
    o_ref = refs[-1]
    tm, d = x1_ref.shape
    half = d // 2
    h = h_ref[...]
    mid = (_silu(_dot(h, ws1_ref[...])) * _dot(h, ws3_ref[...])).astype(BF16)
    shared = _dot(mid, ws2_ref[...])
    w = w_ref[...]
    acc_lo = jnp.zeros((tm, half), F32)
    acc_hi = jnp.zeros((tm, half), F32)
    for k in range(TOP_K):
        lo, hi = _unpack_halves(y_refs[k][...])
        acc_lo = acc_lo + lo * w[:, k:k + 1]
        acc_hi = acc_hi + hi * w[:, k:k + 1]
    routed = jnp.concatenate([acc_lo, acc_hi], axis=1)
    x2 = x1_ref[...] + gt2_ref[0] * (routed + shared)
    ms = jnp.mean(x2 * x2, axis=-1, keepdims=True)
    o_ref[...] = x2 * lax.rsqrt(ms + NORM_EPS) * g_ref[...]


COMBINE_ROWS = 256
COMBINE_PARTS = 4


def _combine(y_kt, wts, h2, x1, ws1, ws3, ws2, gt2, g_final, seq, part, n_parts, prev_out):
    t, d = x1.shape
    tm = min(COMBINE_ROWS, seq)
    per_b = seq // tm
    tiles = t // tm // n_parts
    first = part * tiles
    row = lambda i: (first + i, 0)
    full = lambda i: (0, 0)
    y_specs = [pl.BlockSpec((tm, d // 2), functools.partial(lambda i, k: (k * tiles + i, 0), k=k))
               for k in range(TOP_K)]
    in_specs = y_specs + [
        pl.BlockSpec((tm, TOP_K), row),
        pl.BlockSpec((tm, d), row),
        pl.BlockSpec((tm, d), row),
        pl.BlockSpec((d, SHARED_FF), full),
        pl.BlockSpec((d, SHARED_FF), full),
        pl.BlockSpec((SHARED_FF, d), full),
        pl.BlockSpec((1, 1, d), lambda i: ((first + i) // per_b, 0, 0)),
        pl.BlockSpec((1, d), full)]
    args = [y_kt] * TOP_K + [wts, h2, x1, ws1, ws3, ws2, gt2, g_final]
    aliases = {}
    if prev_out is not None:
        in_specs.append(pl.BlockSpec(memory_space=pl.ANY))
        aliases = {len(args): 0}
        args.append(prev_out)
    return pl.pallas_call(
        _combine_kernel,
        grid=(tiles,),
        in_specs=in_specs,
        out_specs=pl.BlockSpec((tm, d), row),
        out_shape=jax.ShapeDtypeStruct((t, d), F32),
        input_output_aliases=aliases,
        compiler_params=_cparams(("parallel",)),
        name="combine_shared_final",
    )(*args)


def _slot_tables(cnt, t):
    ntiles = cnt.shape[1] // 128
    cnt_tile = cnt.reshape(N_EXPERTS, ntiles, 128)[:, :, 0].astype(jnp.int32)
    counts = jnp.sum(cnt_tile, axis=1)
    padded = (counts + SLOT_ROWS - 1) // SLOT_ROWS * SLOT_ROWS
    pstart = jnp.cumsum(padded) - padded
    tile_base = pstart[:, None] + jnp.cumsum(cnt_tile, axis=1) - cnt_tile
    n_blk = -(-(t * TOP_K) // SLOT_ROWS) + N_EXPERTS
    blk_end = jnp.cumsum(padded // SLOT_ROWS)
    blk_e = jnp.sum((blk_end[None, :] <= jnp.arange(n_blk)[:, None]).astype(jnp.int32), axis=1)
    blk_e = jnp.minimum(blk_e, N_EXPERTS - 1)
    owns = (padded > 0).astype(jnp.int32)
    ord_of_e = jnp.cumsum(owns) - owns
    ids = jnp.arange(N_EXPERTS, dtype=jnp.int32)
    eid_of_ord = jnp.sum(jnp.where((ord_of_e[None, :] == ids[:, None]) & (owns[None, :] > 0), ids[None, :], 0), axis=1)
    blk_ord = jnp.sum(jnp.where(blk_e[:, None] == ids[None, :], ord_of_e[None, :], 0), axis=1)
    experts_tables = (blk_e, blk_end[-1:].astype(jnp.int32), blk_ord.astype(jnp.int32),
                      eid_of_ord.astype(jnp.int32), jnp.sum(owns).reshape(1).astype(jnp.int32))
    return (experts_tables, pstart + counts, padded - counts,
            tile_base.T.astype(F32).reshape(ntiles, N_EXPERTS, 1), n_blk * SLOT_ROWS)


def _permute_in_cols(w_in):
    qa, ka, va, qr, kr, vr, gr, ga, gt = jnp.split(
        w_in, np.cumsum((A_WIDTH, A_WIDTH, A_WIDTH, R_QK_WIDTH, R_QK_WIDTH, R_V_WIDTH, R_V_WIDTH,
                         D_MODEL))[:].tolist(), axis=1)
    return jnp.concatenate([vr, gr, ga, gt, qa, ka, va, qr, kr], axis=1)


def kernel(x, c, w_ada, b_ada, g_mix, w_in, w_pa, w_pr, w_out, g_ffn, w_router, router_bias,
           w1, w3, w2, ws1, ws3, ws2, g_final):
    bsz, seq, d = x.shape
    t = bsz * seq
    depth = w_ada.shape[0]
    assert depth == 1, "the final norm is fused into the single layer's last kernel"
    rest = jnp.exp2(-8.0 / A_HEADS * jnp.arange(1, A_HEADS + 1, dtype=F32)) * LOG2_E
    pieces = []
    for _ in range(ALIBI_PIECES):
        pieces.append(rest.astype(BF16).astype(F32))
        rest = rest - pieces[-1]
    slopes = jnp.stack(pieces, axis=1).reshape(-1)
    x2d = x.reshape(t, d)
    for l in range(depth):
        mod = _ada(c, w_ada[l], b_ada[l])
        sh1, sc1, gt1, sh2, sc2, gt2 = [m.reshape(bsz, 1, d) for m in jnp.split(mod, 6, axis=-1)]
        w_in_p = _permute_in_cols(w_in[l]).astype(BF16)
        proj = _inproj(x2d, g_mix[l].reshape(1, d), sc1, sh1, w_in_p, seq)
        proj3 = proj.reshape(bsz, seq, IN_COLS)
        ya = _moba(proj3, slopes).reshape(t, A_WIDTH)
        yr = _retention(proj3).reshape(t, R_V_WIDTH)
        x1, h2, h2p = _mix(ya, yr, proj, x2d, w_pa[l].astype(BF16), w_pr[l].astype(BF16),
                           w_out[l].astype(BF16), gt1, g_ffn[l].reshape(1, d), sc2, sh2, seq)
        eidx_t, wts_t, rank_t, cnt = _router(h2, w_router[l].T.astype(BF16),
                                             router_bias[l].reshape(N_EXPERTS, 1))
        experts_tables, pad_start, pad_len, tile_base, n_rows = _slot_tables(cnt, t)
        pos3 = _positions(eidx_t, rank_t, tile_base)
        pos_blocks = jnp.transpose(
            pos3.reshape(pos3.shape[0], TOP_K, -1, SC_DISPATCH_TOKENS), (0, 2, 1, 3)
        ).reshape(t // SC_DISPATCH_TOKENS, TOP_K, SC_DISPATCH_TOKENS)
        xs = _zero_pads(pad_start, pad_len, _sc_dispatch(pos_blocks, h2p, n_rows))
        ys = _experts(*experts_tables, xs, w1[l], w3[l], w2[l])
        wts = wts_t.T
        shared_w = (ws1[l].astype(BF16), ws3[l].astype(BF16), ws2[l].astype(BF16))
        blocks_per_part = pos_blocks.shape[0] // COMBINE_PARTS
        x2d = None
        for part in range(COMBINE_PARTS):
            y_kt = _sc_gather(pos_blocks[part * blocks_per_part:(part + 1) * blocks_per_part], ys,
                              t // COMBINE_PARTS)
            x2d = _combine(y_kt, wts, h2, x1, *shared_w, gt2, g_final.reshape(1, d), seq,
                           part, COMBINE_PARTS, x2d)
    return x2d.reshape(bsz, seq, d)
```

```python
import functools

import jax
import jax.numpy as jnp
import numpy as np
from jax import lax
from jax.experimental import pallas as pl
from jax.experimental.pallas import tpu as pltpu
from jax.experimental.pallas import tpu_sc as plsc

F32 = jnp.float32
BF16 = jnp.bfloat16

D_MODEL = 1024
A_HEADS = 8
A_HEAD_DIM = 64
A_WIDTH = A_HEADS * A_HEAD_DIM
MOBA_BLOCK = 256
MOBA_TOPK = 3
R_HEADS = 8
R_QK_DIM = 64
R_V_DIM = 128
R_QK_WIDTH = R_HEADS * R_QK_DIM
R_V_WIDTH = R_HEADS * R_V_DIM
R_CHUNK = 128
N_EXPERTS = 256
TOP_K = 8
N_GROUPS = 8
GROUP_SIZE = N_EXPERTS // N_GROUPS
TOPK_GROUPS = 4
EXPERT_FF = 256
SHARED_FF = 256
ROUTED_SCALE = 2.5
NORM_EPS = 1e-6
GN_EPS = 1e-6
NEG_INF = -1e30
KNOCKED_OUT = -3e38

COL_VR, COL_GR, COL_GA, COL_GT = 0, 1024, 2048, 3072
COL_QA, COL_KA, COL_VA, COL_QR, COL_KR = 4096, 4608, 5120, 5632, 6144
IN_COLS = 6656
AUG = 128
FEAT_BIAS = A_HEAD_DIM
FEAT_POS = A_HEAD_DIM + 32
ALIBI_PIECES = 3
LOG2_E = 1.4426950408889634
LANES = 128
BF16_ROWS = 16
V_ROWS = A_HEAD_DIM + BF16_ROWS
MOBA_HEADS_PER_STEP = 8

VMEM_LIMIT = 56 * 1024 * 1024


def _cparams(sem, vmem=VMEM_LIMIT):
    return pltpu.CompilerParams(dimension_semantics=sem, vmem_limit_bytes=vmem)


def _dot(a, b):
    return jnp.dot(a, b, preferred_element_type=F32)


def _dot_nt(a, b):
    return lax.dot_general(a, b, (((1,), (1,)), ((), ())), preferred_element_type=F32)


def _sigmoid(x):
    return 1.0 / (1.0 + jnp.exp(-x))


def _silu(x):
    return x * _sigmoid(x)


def _ada_kernel(c_ref, w_ref, b_ref, o_ref):
    c = c_ref[...]
    s = _silu(c)
    s_hi = s.astype(BF16)
    s_lo = (s - s_hi.astype(F32)).astype(BF16)
    w = w_ref[...]
    w_hi = w.astype(BF16)
    w_lo = (w - w_hi.astype(F32)).astype(BF16)
    o_ref[...] = _dot(s_hi, w_hi) + _dot(s_hi, w_lo) + _dot(s_lo, w_hi) + b_ref[...]


def _ada(c, w_ada, b_ada):
    bsz, d = c.shape
    n = w_ada.shape[1]
    tn = 1024
    return pl.pallas_call(
        _ada_kernel,
        grid=(n // tn,),
        in_specs=[pl.BlockSpec((bsz, d), lambda j: (0, 0)),
                  pl.BlockSpec((d, tn), lambda j: (0, j)),
                  pl.BlockSpec((1, tn), lambda j: (0, j))],
        out_specs=pl.BlockSpec((bsz, tn), lambda j: (0, j)),
        out_shape=jax.ShapeDtypeStruct((bsz, n), F32),
        compiler_params=_cparams(("parallel",)),
        name="ada_mod",
    )(c, w_ada, b_ada.reshape(1, n))


INPROJ_COLS = 512


def _inproj_kernel(x_ref, g_ref, sc_ref, sh_ref, w_ref, o_ref):
    x = x_ref[...]
    ms = jnp.mean(x * x, axis=-1, keepdims=True)
    y = x * lax.rsqrt(ms + NORM_EPS) * g_ref[...]
    h = (y * (1.0 + sc_ref[0]) + sh_ref[0]).astype(BF16)
    for j in range(w_ref.shape[1] // INPROJ_COLS):
        cols = slice(j * INPROJ_COLS, (j + 1) * INPROJ_COLS)
        o_ref[:, cols] = _dot(h, w_ref[:, cols]).astype(BF16)


def _inproj(x2d, g, sc, sh, w_bf16, seq):
    t, d = x2d.shape
    n = w_bf16.shape[1]
    tm = min(512, seq)
    per_b = seq // tm
    return pl.pallas_call(
        _inproj_kernel,
        grid=(t // tm,),
        in_specs=[pl.BlockSpec((tm, d), lambda i: (i, 0)),
                  pl.BlockSpec((1, d), lambda i: (0, 0)),
                  pl.BlockSpec((1, 1, d), lambda i: (i // per_b, 0, 0)),
                  pl.BlockSpec((1, 1, d), lambda i: (i // per_b, 0, 0)),
                  pl.BlockSpec((d, n), lambda i: (0, 0))],
        out_specs=pl.BlockSpec((tm, n), lambda i: (i, 0)),
        out_shape=jax.ShapeDtypeStruct((t, n), BF16),
        compiler_params=_cparams(("parallel",)),
        name="norm_inproj",
    )(x2d, g, sc, sh, w_bf16)


def _moba_prepare(i, slopes_ref, q_ref, k_ref, v_ref, ko_ref, qo_ref, vo_ref, kmean_scr):
    nblk = kmean_scr.shape[0]
    width = q_ref.shape[1]
    seq_rows = pl.ds(pl.multiple_of(i * MOBA_BLOCK, MOBA_BLOCK), MOBA_BLOCK)
    q = q_ref[...]
    k = k_ref[...]
    v = v_ref[...]
    kmean_scr[pl.ds(i, 1), :] = jnp.mean(k.astype(F32), axis=0, keepdims=True)

    eye = (lax.broadcasted_iota(jnp.int32, (width, width), 0)
           == lax.broadcasted_iota(jnp.int32, (width, width), 1)).astype(BF16)
    q_t = _dot_nt(eye, q)
    v_t = _dot_nt(eye, v)

    km = kmean_scr[...]
    km_rep = jnp.concatenate([km] * A_HEADS, axis=0)
    r_head = lax.broadcasted_iota(jnp.int32, km_rep.shape, 0) // nblk
    c_head = lax.broadcasted_iota(jnp.int32, km_rep.shape, 1) // A_HEAD_DIM
    km_bd = jnp.where(r_head == c_head, km_rep, 0.0)
    km_hi = km_bd.astype(BF16)
    km_lo = (km_bd - km_hi.astype(F32)).astype(BF16)
    q_t_b = q_t.astype(BF16)
    gate_all = _dot(km_hi, q_t_b) + _dot(km_lo, q_t_b)

    mb = q.shape[0]
    blk = lax.broadcasted_iota(jnp.int32, (nblk, mb), 0)
    lane_pos = lax.broadcasted_iota(jnp.int32, (BF16_ROWS, mb), 1).astype(F32)
    row16 = lax.broadcasted_iota(jnp.int32, (BF16_ROWS, mb), 0)
    key_pos = lax.broadcasted_iota(jnp.int32, (mb, AUG), 0).astype(F32)
    kcol = lax.broadcasted_iota(jnp.int32, (mb, AUG), 1)
    sel_r = lax.broadcasted_iota(jnp.int32, (width, AUG), 0)
    sel_c = lax.broadcasted_iota(jnp.int32, (width, AUG), 1)

    blk_first = (i * mb).astype(F32)
    for h in range(A_HEADS):
        pieces = [slopes_ref[h * ALIBI_PIECES + c] for c in range(ALIBI_PIECES)]
        g = jnp.where(blk < i, gate_all[h * nblk:(h + 1) * nblk, :], NEG_INF)
        sel = jnp.zeros((nblk, mb), jnp.bool_)
        for r in range(MOBA_TOPK):
            m = jnp.max(g, axis=0, keepdims=True)
            idx = jnp.min(jnp.where(g == m, blk, nblk), axis=0, keepdims=True)
            hit = blk == idx
            sel = jnp.logical_or(sel, jnp.logical_and(hit, r < i))
            g = jnp.where(hit, KNOCKED_OUT, g)
        bias_t = jnp.where(sel, 0.0, NEG_INF)

        scale = A_HEAD_DIM ** -0.5 * LOG2_E
        qo_ref[0, h, 0:A_HEAD_DIM, :] = (q_t[h * A_HEAD_DIM:(h + 1) * A_HEAD_DIM, :] * scale).astype(BF16)
        qo_ref[0, h, FEAT_BIAS:FEAT_BIAS + nblk, :] = bias_t.astype(BF16)
        if nblk < 32:
            qo_ref[0, h, FEAT_BIAS + nblk:FEAT_POS, :] = jnp.zeros((32 - nblk, mb), BF16)
        piece_rows = jnp.where(row16 % ALIBI_PIECES == 0, pieces[0],
                               jnp.where(row16 % ALIBI_PIECES == 1, pieces[1], pieces[2]))
        pos_feat = jnp.where(row16 < 3, -lane_pos,
                             jnp.where(row16 < 6, piece_rows,
                                       jnp.where(row16 < 9, -blk_first, jnp.where(row16 < 12, piece_rows, 0.0))))
        qo_ref[0, h, FEAT_POS:FEAT_POS + BF16_ROWS, :] = pos_feat.astype(BF16)
        qo_ref[0, h, FEAT_POS + BF16_ROWS:AUG, :] = jnp.zeros((AUG - FEAT_POS - BF16_ROWS, mb), BF16)

        vo_ref[0, h, 0:A_HEAD_DIM, seq_rows] = v_t[h * A_HEAD_DIM:(h + 1) * A_HEAD_DIM, :].astype(BF16)
        vo_ref[0, h, A_HEAD_DIM:V_ROWS, seq_rows] = jnp.where(row16 == 0, 1.0, 0.0).astype(BF16)

        pick = jnp.where(jnp.logical_and(sel_r == sel_c + h * A_HEAD_DIM, sel_c < A_HEAD_DIM),
                         1.0, 0.0).astype(BF16)
        pos_col = kcol - FEAT_POS
        piece_cols = jnp.where(pos_col % ALIBI_PIECES == 0, pieces[0],
                               jnp.where(pos_col % ALIBI_PIECES == 1, pieces[1], pieces[2]))
        k_feat = jnp.where(
            kcol == FEAT_BIAS + i, 1.0,
            jnp.where(pos_col < 0, 0.0,
                      jnp.where(pos_col < 3, piece_cols,
                                jnp.where(pos_col < 6, key_pos,
                                          jnp.where(pos_col < 9, piece_cols,
                                                    jnp.where(pos_col < 12, blk_first, 0.0))))))
        ko_ref[0, h, seq_rows, :] = (_dot(k, pick) + k_feat).astype(BF16)


def _moba_attend(i, q_ref, k_ref, v_ref, o_ref, s_a, s_b, group, n_groups):
    mb = MOBA_BLOCK
    span = group * mb
    own = pl.multiple_of(i * mb, mb)
    key_i = lax.broadcasted_iota(jnp.int32, (mb, mb), 0)
    qry_i = lax.broadcasted_iota(jnp.int32, (mb, mb), 1)
    feat = lax.broadcasted_iota(jnp.int32, (AUG, mb), 0)
    is_bias = jnp.logical_and(feat >= FEAT_BIAS, feat < FEAT_POS)
    q_ts, carry0 = [], []
    for hh in range(MOBA_HEADS_PER_STEP):
        q_t = q_ref[0, hh]
        q_ts.append(q_t)
        q_own = jnp.where(is_bias, jnp.zeros_like(q_t), q_t)
        s = _dot(k_ref[0, hh, pl.ds(own, mb), :], q_own)
        s = jnp.where(key_i <= qry_i, s, NEG_INF)
        m0 = jnp.max(s, axis=0, keepdims=True)
        p = jnp.exp2(s - m0)
        carry0 += [m0, _dot(v_ref[0, hh, :, pl.ds(own, mb)], p.astype(BF16))]

    def scores(g, dst):
        start = pl.multiple_of(jnp.minimum(g, n_groups - 1) * span, span)
        for hh in range(MOBA_HEADS_PER_STEP):
            dst[hh] = _dot(k_ref[0, hh, pl.ds(start, span), :], q_ts[hh])

    def absorb(g, src, carry):
        start = pl.multiple_of(g * span, span)
        new = []
        for hh in range(MOBA_HEADS_PER_STEP):
            m, acc = carry[2 * hh], carry[2 * hh + 1]
            sb = src[hh]
            m_new = jnp.maximum(m, jnp.max(sb, axis=0, keepdims=True))
            pb = jnp.exp2(sb - m_new)
            alpha = jnp.exp2(m - m_new)
            acc = acc * alpha + _dot(v_ref[0, hh, :, pl.ds(start, span)], pb.astype(BF16))
            new += [m_new, acc]
        return tuple(new)

    def body(pair, carry):
        scores(2 * pair + 1, s_b)
        carry = absorb(2 * pair, s_a, carry)
        scores(2 * pair + 2, s_a)
        return absorb(2 * pair + 1, s_b, carry)

    scores(0, s_a)
    live_groups = (i + group - 1) // group
    res = lax.fori_loop(0, live_groups // 2, body, tuple(carry0))
    res = lax.cond(live_groups % 2 == 1, lambda c: absorb(live_groups - 1, s_a, c), lambda c: c, res)
    outs = [res[2 * hh + 1][0:A_HEAD_DIM, :] / res[2 * hh + 1][A_HEAD_DIM:A_HEAD_DIM + 1, :] for hh in range(MOBA_HEADS_PER_STEP)]
    o_t = jnp.concatenate(outs, axis=0).astype(BF16)
    eye = (key_i == qry_i).astype(BF16)
    o_ref[0] = _dot_nt(eye, o_t).astype(BF16)


def _moba_kernel(slopes_ref, q_ref, k_ref, v_ref, o_ref, kmean_scr, q_scr, k_scr, v_scr, s_a, s_b,
                 *, group, n_groups):
    i = pl.program_id(1)

    @pl.when(i == 0)
    def _():
        kmean_scr[...] = jnp.zeros_like(kmean_scr)
        k_scr[...] = jnp.zeros_like(k_scr)
        v_scr[...] = jnp.zeros_like(v_scr)

    _moba_prepare(i, slopes_ref, q_ref, k_ref, v_ref, k_scr, q_scr, v_scr, kmean_scr)
    _moba_attend(i, q_scr, k_scr, v_scr, o_ref, s_a, s_b, group, n_groups)


def _moba(proj3, slopes):
    bsz, seq, _ = proj3.shape
    mb = MOBA_BLOCK
    nblk = seq // mb
    group = min(2, nblk)
    n_groups = nblk // group
    assert MOBA_HEADS_PER_STEP == A_HEADS
    assert seq % mb == 0 and nblk % group == 0 and nblk <= FEAT_POS - FEAT_BIAS, "one selection feature per key block"
    grid_spec = pltpu.PrefetchScalarGridSpec(
        num_scalar_prefetch=1,
        grid=(bsz, nblk),
        in_specs=[pl.BlockSpec((None, mb, A_WIDTH), lambda b, i, s: (b, i, COL_QA // A_WIDTH)),
                  pl.BlockSpec((None, mb, A_WIDTH), lambda b, i, s: (b, i, COL_KA // A_WIDTH)),
                  pl.BlockSpec((None, mb, A_WIDTH), lambda b, i, s: (b, i, COL_VA // A_WIDTH))],
        out_specs=pl.BlockSpec((1, mb, A_WIDTH), lambda b, i, s: (b, i, 0)),
        scratch_shapes=[pltpu.VMEM((nblk, A_WIDTH), F32),
                        pltpu.VMEM((1, A_HEADS, AUG, mb), BF16),
                        pltpu.VMEM((1, A_HEADS, seq, AUG), BF16),
                        pltpu.VMEM((1, A_HEADS, V_ROWS, seq), BF16),
                        pltpu.VMEM((A_HEADS, group * mb, mb), F32),
                        pltpu.VMEM((A_HEADS, group * mb, mb), F32)],
    )
    return pl.pallas_call(
        functools.partial(_moba_kernel, group=group, n_groups=n_groups),
        grid_spec=grid_spec,
        out_shape=jax.ShapeDtypeStruct((bsz, seq, A_WIDTH), BF16),
        compiler_params=_cparams(("parallel", "arbitrary")),
        name="moba_attention",
    )(slopes, proj3, proj3, proj3)


def _retention_kernel(cdec_ref, q_ref, k_ref, v_ref, g_ref, decay_ref, qdec_ref, kdec_ref, o_ref, state_scr):
    @pl.when(pl.program_id(0) == 0)
    def _():
        state_scr[...] = jnp.zeros_like(state_scr)

    width = q_ref.shape[2]
    eye = (lax.broadcasted_iota(jnp.int32, (width, width), 0)
           == lax.broadcasted_iota(jnp.int32, (width, width), 1)).astype(BF16)
    for b in range(q_ref.shape[0]):
        q = q_ref[b]
        k_t = _dot_nt(eye, k_ref[b])
        k_t_b = k_t.astype(BF16)
        state_b = state_scr[b].astype(BF16)
        for h in range(R_HEADS):
            rows = slice(h * R_QK_DIM, (h + 1) * R_QK_DIM)
            cols = slice(h * R_V_DIM, (h + 1) * R_V_DIM)
            q_h = q[:, rows]
            v_h = v_ref[b, :, cols]
            inner = _dot(q_h, k_t_b[rows, :]) * decay_ref[h]
            out = _dot(inner.astype(BF16), v_h) + _dot(q_h, state_b[rows, :]) * qdec_ref[h]
            k_dec = (k_t[rows, :] * kdec_ref[h]).astype(BF16)
            state_scr[b, rows, :] = cdec_ref[h] * state_scr[b, rows, :] + _dot(k_dec, v_h)
            mu = jnp.mean(out, axis=-1, keepdims=True)
            cen = out - mu
            var = jnp.mean(cen * cen, axis=-1, keepdims=True)
            y = cen * lax.rsqrt(var + GN_EPS)
            o_ref[b, :, cols] = (y * _silu(g_ref[b, :, cols].astype(F32))).astype(BF16)


def _retention_consts():
    h = np.arange(R_HEADS, dtype=np.float64)
    log_g = np.log(1.0 - np.exp2(-5.0 - h))
    n = np.arange(R_CHUNK, dtype=np.float64)
    diff = n[:, None] - n[None, :]
    scale = R_QK_DIM ** -0.5
    decay = np.where(diff >= 0, np.exp(np.maximum(diff, 0.0) * log_g[:, None, None]), 0.0) * scale
    q_decay = np.exp((n + 1.0) * log_g[:, None])[:, :, None]
    k_decay = np.exp((R_CHUNK - 1.0 - n) * log_g[:, None])[:, None, :] * scale
    chunk_decay = np.exp(R_CHUNK * log_g)
    return (jnp.asarray(decay, F32), jnp.asarray(q_decay, F32), jnp.asarray(k_decay, F32),
            jnp.asarray(chunk_decay, F32))


def _retention(proj3):
    bsz, seq, _ = proj3.shape
    c = R_CHUNK
    decay, qdec, kdec, cdec = _retention_consts()
    grid_spec = pltpu.PrefetchScalarGridSpec(
        num_scalar_prefetch=1,
        grid=(seq // c,),
        in_specs=[pl.BlockSpec((bsz, c, R_QK_WIDTH), lambda i, s: (0, i, COL_QR // R_QK_WIDTH)),
                  pl.BlockSpec((bsz, c, R_QK_WIDTH), lambda i, s: (0, i, COL_KR // R_QK_WIDTH)),
                  pl.BlockSpec((bsz, c, R_V_WIDTH), lambda i, s: (0, i, COL_VR // R_V_WIDTH)),
                  pl.BlockSpec((bsz, c, R_V_WIDTH), lambda i, s: (0, i, COL_GR // R_V_WIDTH)),
                  pl.BlockSpec((R_HEADS, c, c), lambda i, s: (0, 0, 0)),
                  pl.BlockSpec((R_HEADS, c, 1), lambda i, s: (0, 0, 0)),
                  pl.BlockSpec((R_HEADS, 1, c), lambda i, s: (0, 0, 0))],
        out_specs=pl.BlockSpec((bsz, c, R_V_WIDTH), lambda i, s: (0, i, 0)),
        scratch_shapes=[pltpu.VMEM((bsz, R_QK_WIDTH, R_V_DIM), F32)],
    )
    return pl.pallas_call(
        _retention_kernel,
        grid_spec=grid_spec,
        out_shape=jax.ShapeDtypeStruct((bsz, seq, R_V_WIDTH), BF16),
        compiler_params=_cparams(("arbitrary",)),
        name="retention",
    )(cdec, proj3, proj3, proj3, proj3, decay, qdec, kdec)


def _pack_halves(x):
    w = x.shape[1] // 2
    bits = lax.bitcast_convert_type(x.astype(BF16).astype(F32), jnp.uint32)
    return (bits[:, :w] >> 16) | (bits[:, w:] & jnp.uint32(0xFFFF0000))


def _unpack_halves(p):
    lo = lax.bitcast_convert_type(p << 16, F32)
    hi = lax.bitcast_convert_type(p & jnp.uint32(0xFFFF0000), F32)
    return lo, hi


SUBLANES = 8


ROUTER_ROWS = 512


def _mix_kernel(ya_ref, yr_ref, ga_ref, gt_ref, x_ref, wpa_ref, wpr_ref, wout_ref,
                gt1_ref, g_ref, sc_ref, sh_ref, wr_ref, rb_ref,
                x1_ref, h2_ref, h2p_ref, e_ref, w_ref, r_ref, c_ref):
    a = _dot(ya_ref[...], wpa_ref[...]) * _sigmoid(ga_ref[...].astype(F32))
    r = _dot(yr_ref[...], wpr_ref[...]) * _sigmoid(gt_ref[...].astype(F32))
    mix = _dot((a + r).astype(BF16), wout_ref[...])
    x1 = x_ref[...] + gt1_ref[0] * mix
    x1_ref[...] = x1
    ms = jnp.mean(x1 * x1, axis=-1, keepdims=True)
    y = x1 * lax.rsqrt(ms + NORM_EPS) * g_ref[...]
    h2 = y * (1.0 + sc_ref[0]) + sh_ref[0]
    h2_b = h2.astype(BF16)
    h2_ref[...] = h2_b
    h2p_ref[...] = _pack_halves(h2)
    _route(h2_b, wr_ref, rb_ref, e_ref, w_ref, r_ref, c_ref)


def _mix(ya, yr, proj, x2d, wpa, wpr, wout, gt1, g, sc, sh, wr_t, bias_col, seq):
    t, d = x2d.shape
    tm = min(ROUTER_ROWS, seq)
    per_b = seq // tm
    row = lambda i: (i, 0)
    full = lambda i: (0, 0)
    by_tile = lambda i: (0, i)
    per_batch = lambda i: (i // per_b, 0, 0)
    return pl.pallas_call(
        _mix_kernel,
        grid=(t // tm,),
        in_specs=[pl.BlockSpec((tm, A_WIDTH), row),
                  pl.BlockSpec((tm, R_V_WIDTH), row),
                  pl.BlockSpec((tm, d), lambda i: (i, COL_GA // D_MODEL)),
                  pl.BlockSpec((tm, d), lambda i: (i, COL_GT // D_MODEL)),
                  pl.BlockSpec((tm, d), row),
                  pl.BlockSpec((A_WIDTH, d), full),
                  pl.BlockSpec((R_V_WIDTH, d), full),
                  pl.BlockSpec((d, d), full),
                  pl.BlockSpec((1, 1, d), per_batch),
                  pl.BlockSpec((1, d), full),
                  pl.BlockSpec((1, 1, d), per_batch),
                  pl.BlockSpec((1, 1, d), per_batch),
                  pl.BlockSpec((N_EXPERTS, d), full),
                  pl.BlockSpec((N_EXPERTS, 1), full)],
        out_specs=[pl.BlockSpec((tm, d), row), pl.BlockSpec((tm, d), row),
                   pl.BlockSpec((tm, d // 2), row),
                   pl.BlockSpec((TOP_K, tm), by_tile), pl.BlockSpec((TOP_K, tm), by_tile),
                   pl.BlockSpec((TOP_K, tm), by_tile), pl.BlockSpec((N_EXPERTS, LANES), by_tile)],
        out_shape=[jax.ShapeDtypeStruct((t, d), F32), jax.ShapeDtypeStruct((t, d), BF16),
                   jax.ShapeDtypeStruct((t, d // 2), jnp.uint32),
                   jax.ShapeDtypeStruct((TOP_K, t), jnp.int32),
                   jax.ShapeDtypeStruct((TOP_K, t), F32),
                   jax.ShapeDtypeStruct((TOP_K, t), jnp.int32),
                   jax.ShapeDtypeStruct((N_EXPERTS, (t // tm) * LANES), F32)],
        compiler_params=_cparams(("parallel",)),
        name="merge_norm_route",
    )(ya, yr, proj, proj, x2d, wpa, wpr, wout, gt1, g, sc, sh, wr_t, bias_col)


def _route(h, wr_ref, b_ref, e_ref, w_ref, r_ref, c_ref):
    logits = _dot_nt(wr_ref[...], h)
    scores = _sigmoid(logits)
    choice = scores + b_ref[...]
    tm = logits.shape[1]
    giota = lax.broadcasted_iota(jnp.int32, (GROUP_SIZE, tm), 0)
    gs_rows = []
    for g in range(N_GROUPS):
        cg = choice[g * GROUP_SIZE:(g + 1) * GROUP_SIZE, :]
        m1 = jnp.max(cg, axis=0, keepdims=True)
        i1 = jnp.min(jnp.where(cg == m1, giota, GROUP_SIZE), axis=0, keepdims=True)
        m2 = jnp.max(jnp.where(giota == i1, KNOCKED_OUT, cg), axis=0, keepdims=True)
        gs_rows.append(m1 + m2)
    gs = jnp.concatenate(gs_rows, axis=0)
    grow = lax.broadcasted_iota(jnp.int32, (N_GROUPS, tm), 0)
    gmask = jnp.zeros((N_GROUPS, tm), jnp.bool_)
    for _ in range(TOPK_GROUPS):
        mx = jnp.max(gs, axis=0, keepdims=True)
        ix = jnp.min(jnp.where(gs == mx, grow, N_GROUPS), axis=0, keepdims=True)
        hit = grow == ix
        gmask = jnp.logical_or(gmask, hit)
        gs = jnp.where(hit, KNOCKED_OUT, gs)
    gmask_f = jnp.where(gmask, 1.0, 0.0)
    masked = jnp.concatenate(
        [jnp.where(gmask_f[g:g + 1, :] > 0.5, choice[g * GROUP_SIZE:(g + 1) * GROUP_SIZE, :], NEG_INF)
         for g in range(N_GROUPS)], axis=0)
    erow = lax.broadcasted_iota(jnp.int32, (N_EXPERTS, tm), 0)
    idx_rows, w_rows = [], []
    chosen = jnp.zeros((N_EXPERTS, tm), F32)
    for _ in range(TOP_K):
        mx = jnp.max(masked, axis=0, keepdims=True)
        ix = jnp.min(jnp.where(masked == mx, erow, N_EXPERTS), axis=0, keepdims=True)
        hit = erow == ix
        w_rows.append(jnp.sum(jnp.where(hit, scores, 0.0), axis=0, keepdims=True))
        idx_rows.append(ix)
        chosen = jnp.where(hit, 1.0, chosen)
        masked = jnp.where(hit, KNOCKED_OUT, masked)
    w = jnp.concatenate(w_rows, axis=0)
    w = w / (jnp.sum(w, axis=0, keepdims=True) + 1e-20) * ROUTED_SCALE
    e_ref[...] = jnp.concatenate(idx_rows, axis=0)
    w_ref[...] = w
    chosen_b = chosen.astype(BF16)
    earlier = (lax.broadcasted_iota(jnp.int32, (tm, tm), 0)
               < lax.broadcasted_iota(jnp.int32, (tm, tm), 1)).astype(BF16)
    before = _dot(chosen_b, earlier)
    ranks = [jnp.sum(jnp.where(erow == ix, before, 0.0), axis=0, keepdims=True) for ix in idx_rows]
    r_ref[...] = jnp.concatenate(ranks, axis=0).astype(jnp.int32)
    c_ref[...] = _dot(chosen_b, jnp.ones((tm, LANES), BF16))


def _pos_kernel(e_ref, r_ref, base_ref, p_ref):
    tm = e_ref.shape[1]
    erow = lax.broadcasted_iota(jnp.int32, (N_EXPERTS, tm), 0)
    base = base_ref[0]
    rows = [jnp.sum(jnp.where(erow == e_ref[k:k + 1, :], base, 0.0), axis=0, keepdims=True)
            for k in range(TOP_K)]
    p_ref[0] = jnp.concatenate(rows, axis=0).astype(jnp.int32) + r_ref[...]


MOVE_ROWS = 512


def _positions(eidx_t, rank_t, tile_base):
    t = eidx_t.shape[1]
    tm = min(MOVE_ROWS, t)
    per_router_tile = ROUTER_ROWS // tm
    return pl.pallas_call(
        _pos_kernel,
        grid=(t // tm,),
        in_specs=[pl.BlockSpec((TOP_K, tm), lambda i: (0, i)),
                  pl.BlockSpec((TOP_K, tm), lambda i: (0, i)),
                  pl.BlockSpec((1, N_EXPERTS, 1), lambda i: (i // per_router_tile, 0, 0))],
        out_specs=pl.BlockSpec((1, TOP_K, tm), lambda i: (i, 0, 0)),
        out_shape=jax.ShapeDtypeStruct((t // tm, TOP_K, tm), jnp.int32),
        compiler_params=_cparams(("parallel",)),
        name="slot_positions",
    )(eidx_t, rank_t, tile_base)


SLOT_ROWS = 512
PAD_CHUNKS = (256, 128, 64, 32, 16, 8)


def _zero_pads_kernel(pad_start_ref, pad_len_ref, xs_in, xs_hbm, zero_buf, pad_sem):
    del xs_in
    zero_buf[...] = jnp.zeros_like(zero_buf)

    def pad_copies(e, wait):
        start = pad_start_ref[e]
        n = pad_len_ref[e]
        head = jnp.minimum((-start) & (SUBLANES - 1), n)

        def fill(first, size, pred):
            @pl.when(pred)
            def _():
                cp = pltpu.make_async_copy(zero_buf.at[pl.ds(0, size), :], xs_hbm.at[pl.ds(first, size), :], pad_sem)
                if wait:
                    cp.wait()
                else:
                    cp.start()

        for j in range(SUBLANES - 1):
            fill(start + j, 1, j < head)
        ptr = start + head
        rest = n - head
        for chunk in PAD_CHUNKS:
            fill(pl.multiple_of(ptr, SUBLANES), chunk, (rest & chunk) != 0)
            ptr = ptr + (rest & chunk)

    def issue(e, carry):
        pad_copies(e, False)
        return carry

    def drain(e, carry):
        pad_copies(e, True)
        return carry

    lax.fori_loop(0, N_EXPERTS, issue, 0)
    lax.fori_loop(0, N_EXPERTS, drain, 0)


def _zero_pads(pad_start, pad_len, xs):
    grid_spec = pltpu.PrefetchScalarGridSpec(
        num_scalar_prefetch=2,
        grid=(1,),
        in_specs=[pl.BlockSpec(memory_space=pl.ANY)],
        out_specs=pl.BlockSpec(memory_space=pl.ANY),
        scratch_shapes=[pltpu.VMEM((PAD_CHUNKS[0], xs.shape[1]), xs.dtype),
                        pltpu.SemaphoreType.DMA],
    )
    return pl.pallas_call(
        _zero_pads_kernel,
        grid_spec=grid_spec,
        out_shape=jax.ShapeDtypeStruct(xs.shape, xs.dtype),
        input_output_aliases={2: 0},
        compiler_params=_cparams(("arbitrary",)),
        name="zero_pad_slots",
    )(pad_start, pad_len, xs)


SC_DISPATCH_TOKENS = 64


def _sc_dispatch(pos_blocks, h_rows, n_rows):
    info = plsc.get_sparse_core_info()
    n_cores = info.num_cores
    n_workers = n_cores * info.num_subcores
    t = h_rows.shape[0]
    chunk = SC_DISPATCH_TOKENS
    steps = t // (n_workers * chunk)
    assert steps * n_workers * chunk == t
    mesh = plsc.VectorSubcoreMesh(core_axis_name="c", subcore_axis_name="s")

    @functools.partial(
        pl.kernel, mesh=mesh,
        out_type=jax.ShapeDtypeStruct((n_rows,) + h_rows.shape[1:], h_rows.dtype),
        scratch_types=[pltpu.VMEM((TOP_K, chunk), jnp.int32),
                       pltpu.VMEM((chunk,) + h_rows.shape[1:], h_rows.dtype),
                       pltpu.SemaphoreType.DMA],
        name="sc_dispatch_rows",
    )
    def scatter_rows(pos_hbm, h_hbm, out_hbm, idx_v, rows_v, sem):
        wid = lax.axis_index("s") * n_cores + lax.axis_index("c")

        @pl.loop(0, steps)
        def _(step):
            blk = wid * steps + step
            pltpu.sync_copy(pos_hbm.at[blk], idx_v)
            pltpu.sync_copy(h_hbm.at[pl.ds(blk * chunk, chunk)], rows_v)
            scatters = [pltpu.make_async_copy(rows_v, out_hbm.at[idx_v.at[k]], sem) for k in range(TOP_K)]
            for cp in scatters:
                cp.start()
            for cp in scatters:
                cp.wait()

    return scatter_rows(pos_blocks, h_rows)


WEIGHT_SLOTS = 4


def _experts_kernel(blk_e_ref, nblk_ref, ord_ref, eid_ref, nord_ref, x_ref, w1_hbm, w3_hbm, w2_hbm, y_ref,
                    w1f, w3f, w2f, w1b, w3b, w2b, wsem):
    s = pl.program_id(0)

    def fetch(j):
        slot = j % WEIGHT_SLOTS
        e = eid_ref[j]
        return [pltpu.make_async_copy(src.at[e], dst.at[slot], wsem.at[slot])
                for src, dst in ((w1_hbm, w1f), (w3_hbm, w3f), (w2_hbm, w2f))]

    @pl.when(s < nblk_ref[0])
    def _():
        j = ord_ref[s]

        ahead = WEIGHT_SLOTS - 1

        @pl.when(s == 0)
        def _():
            for first in range(ahead):
                @pl.when(first < nord_ref[0])
                def _(first=first):
                    for cp in fetch(first):
                        cp.start()

        @pl.when(jnp.logical_or(s == 0, blk_e_ref[s] != blk_e_ref[jnp.maximum(s - 1, 0)]))
        def _():
            for cp in fetch(j):
                cp.wait()

            @pl.when(j + ahead < nord_ref[0])
            def _():
                for cp in fetch(j + ahead):
                    cp.start()

            slot = j % WEIGHT_SLOTS
            w1b[...] = w1f[slot].astype(BF16)
            w3b[...] = w3f[slot].astype(BF16)
            w2b[...] = w2f[slot].astype(BF16)

        half = x_ref.shape[1]
        lo, hi = _unpack_halves(x_ref[...])
        lo = lo.astype(BF16)
        hi = hi.astype(BF16)
        h1 = _dot(lo, w1b[:half, :]) + _dot(hi, w1b[half:, :])
        h3 = _dot(lo, w3b[:half, :]) + _dot(hi, w3b[half:, :])
        mid = (_silu(h1) * h3).astype(BF16)
        y_ref[...] = _pack_halves(_dot(mid, w2b[...]))


def _sc_gather(pos_blocks, ys, t):
    info = plsc.get_sparse_core_info()
    n_cores = info.num_cores
    n_workers = n_cores * info.num_subcores
    chunk = SC_DISPATCH_TOKENS
    steps = t // (n_workers * chunk)
    assert steps * n_workers * chunk == t
    mesh = plsc.VectorSubcoreMesh(core_axis_name="c", subcore_axis_name="s")

    @functools.partial(
        pl.kernel, mesh=mesh,
        out_type=jax.ShapeDtypeStruct((TOP_K * t,) + ys.shape[1:], ys.dtype),
        scratch_types=[pltpu.VMEM((TOP_K, chunk), jnp.int32),
                       pltpu.VMEM((chunk,) + ys.shape[1:], ys.dtype),
                       pltpu.VMEM((chunk,) + ys.shape[1:], ys.dtype),
                       pltpu.SemaphoreType.DMA((2,))],
        name="sc_gather_rows",
    )
    def gather_rows(pos_hbm, ys_hbm, out_hbm, idx_v, rows_a, rows_b, sems):
        wid = lax.axis_index("s") * n_cores + lax.axis_index("c")
        bufs = (rows_a, rows_b)

        @pl.loop(0, steps)
        def _(step):
            blk = wid * steps + step
            pltpu.sync_copy(pos_hbm.at[blk], idx_v)
            gathers = [pltpu.make_async_copy(ys_hbm.at[idx_v.at[k]], bufs[k % 2], sems.at[k % 2])
                       for k in range(TOP_K)]
            gathers[0].start()
            for k in range(TOP_K):
                gathers[k].wait()
                if k + 1 < TOP_K:
                    gathers[k + 1].start()
                pltpu.sync_copy(bufs[k % 2], out_hbm.at[pl.ds(k * t + blk * chunk, chunk)])

    return gather_rows(pos_blocks, ys)


def _experts(blk_e, nblk_used, blk_ord, eid_of_ord, n_ord, xs, w1, w3, w2):
    n_rows, half = xs.shape
    d = D_MODEL
    blk = lambda s, be, nb, bo, eo, no: (jnp.minimum(s, nb[0] - 1), 0)
    grid_spec = pltpu.PrefetchScalarGridSpec(
        num_scalar_prefetch=5,
        grid=(n_rows // SLOT_ROWS,),
        in_specs=[pl.BlockSpec((SLOT_ROWS, half), blk),
                  pl.BlockSpec(memory_space=pl.ANY),
                  pl.BlockSpec(memory_space=pl.ANY),
                  pl.BlockSpec(memory_space=pl.ANY)],
        out_specs=pl.BlockSpec((SLOT_ROWS, half), blk),
        scratch_shapes=[pltpu.VMEM((WEIGHT_SLOTS, d, EXPERT_FF), F32),
                        pltpu.VMEM((WEIGHT_SLOTS, d, EXPERT_FF), F32),
                        pltpu.VMEM((WEIGHT_SLOTS, EXPERT_FF, d), F32),
                        pltpu.VMEM((d, EXPERT_FF), BF16),
                        pltpu.VMEM((d, EXPERT_FF), BF16),
                        pltpu.VMEM((EXPERT_FF, d), BF16),
                        pltpu.SemaphoreType.DMA((WEIGHT_SLOTS,))],
    )
    return pl.pallas_call(
        _experts_kernel,
        grid_spec=grid_spec,
        out_shape=jax.ShapeDtypeStruct((n_rows, half), jnp.uint32),
        compiler_params=_cparams(("arbitrary",)),
        name="routed_experts",
    )(blk_e, nblk_used, blk_ord, eid_of_ord, n_ord, xs, w1, w3, w2)


def _combine_kernel(*refs):
    y_refs = refs[:TOP_K]
    w_ref, h_ref, x1_ref, ws1_ref, ws3_ref, ws2_ref, gt2_ref, g_ref = refs[TOP_K:TOP_K + 8]
    o_ref = refs[-1]
    tm, d = x1_ref.shape
    half = d // 2
    h = h_ref[...]
    mid = (_silu(_dot(h, ws1_ref[...])) * _dot(h, ws3_ref[...])).astype(BF16)
    shared = _dot(mid, ws2_ref[...])
    w = w_ref[...]
    acc_lo = jnp.zeros((tm, half), F32)
    acc_hi = jnp.zeros((tm, half), F32)
    for k in range(TOP_K):
        lo, hi = _unpack_halves(y_refs[k][...])
        acc_lo = acc_lo + lo * w[:, k:k + 1]
        acc_hi = acc_hi + hi * w[:, k:k + 1]
    routed = jnp.concatenate([acc_lo, acc_hi], axis=1)
    x2 = x1_ref[...] + gt2_ref[0] * (routed + shared)
    ms = jnp.mean(x2 * x2, axis=-1, keepdims=True)
    o_ref[...] = x2 * lax.rsqrt(ms + NORM_EPS) * g_ref[...]


COMBINE_ROWS = 256
COMBINE_PARTS = 4


def _combine(y_kt, wts, h2, x1, ws1, ws3, ws2, gt2, g_final, seq, part, n_parts, prev_out):
    t, d = x1.shape
    tm = min(COMBINE_ROWS, seq)
    per_b = seq // tm
    tiles = t // tm // n_parts
    first = part * tiles
    row = lambda i: (first + i, 0)
    full = lambda i: (0, 0)
    y_specs = [pl.BlockSpec((tm, d // 2), functools.partial(lambda i, k: (k * tiles + i, 0), k=k))
               for k in range(TOP_K)]
    in_specs = y_specs + [
        pl.BlockSpec((tm, TOP_K), row),
        pl.BlockSpec((tm, d), row),
        pl.BlockSpec((tm, d), row),
        pl.BlockSpec((d, SHARED_FF), full),
        pl.BlockSpec((d, SHARED_FF), full),
        pl.BlockSpec((SHARED_FF, d), full),
        pl.BlockSpec((1, 1, d), lambda i: ((first + i) // per_b, 0, 0)),
        pl.BlockSpec((1, d), full)]
    args = [y_kt] * TOP_K + [wts, h2, x1, ws1, ws3, ws2, gt2, g_final]
    aliases = {}
    if prev_out is not None:
        in_specs.append(pl.BlockSpec(memory_space=pl.ANY))
        aliases = {len(args): 0}
        args.append(prev_out)
    return pl.pallas_call(
        _combine_kernel,
        grid=(tiles,),
        in_specs=in_specs,
        out_specs=pl.BlockSpec((tm, d), row),
        out_shape=jax.ShapeDtypeStruct((t, d), F32),
        input_output_aliases=aliases,
        compiler_params=_cparams(("parallel",)),
        name="combine_shared_final",
    )(*args)


def _slot_tables(cnt, t):
    ntiles = cnt.shape[1] // LANES
    cnt_tile = cnt.reshape(N_EXPERTS, ntiles, LANES)[:, :, 0].astype(jnp.int32)
    counts = jnp.sum(cnt_tile, axis=1)
    padded = (counts + SLOT_ROWS - 1) // SLOT_ROWS * SLOT_ROWS
    pstart = jnp.cumsum(padded) - padded
    tile_base = pstart[:, None] + jnp.cumsum(cnt_tile, axis=1) - cnt_tile
    n_blk = -(-(t * TOP_K) // SLOT_ROWS) + N_EXPERTS
    blk_end = jnp.cumsum(padded // SLOT_ROWS)
    blk_e = jnp.sum((blk_end[None, :] <= jnp.arange(n_blk)[:, None]).astype(jnp.int32), axis=1)
    blk_e = jnp.minimum(blk_e, N_EXPERTS - 1)
    owns = (padded > 0).astype(jnp.int32)
    ord_of_e = jnp.cumsum(owns) - owns
    ids = jnp.arange(N_EXPERTS, dtype=jnp.int32)
    eid_of_ord = jnp.sum(jnp.where((ord_of_e[None, :] == ids[:, None]) & (owns[None, :] > 0), ids[None, :], 0), axis=1)
    blk_ord = jnp.sum(jnp.where(blk_e[:, None] == ids[None, :], ord_of_e[None, :], 0), axis=1)
    experts_tables = (blk_e, blk_end[-1:].astype(jnp.int32), blk_ord.astype(jnp.int32),
                      eid_of_ord.astype(jnp.int32), jnp.sum(owns).reshape(1).astype(jnp.int32))
    return (experts_tables, pstart + counts, padded - counts,
            tile_base.T.astype(F32).reshape(ntiles, N_EXPERTS, 1), n_blk * SLOT_ROWS)


def _permute_in_cols(w_in):
    qa, ka, va, qr, kr, vr, gr, ga, gt = jnp.split(
        w_in, np.cumsum((A_WIDTH, A_WIDTH, A_WIDTH, R_QK_WIDTH, R_QK_WIDTH, R_V_WIDTH, R_V_WIDTH,
                         D_MODEL))[:].tolist(), axis=1)
    return jnp.concatenate([vr, gr, ga, gt, qa, ka, va, qr, kr], axis=1)


def kernel(x, c, w_ada, b_ada, g_mix, w_in, w_pa, w_pr, w_out, g_ffn, w_router, router_bias,
           w1, w3, w2, ws1, ws3, ws2, g_final):
    bsz, seq, d = x.shape
    t = bsz * seq
    depth = w_ada.shape[0]
    assert depth == 1, "the final norm is fused into the single layer's last kernel"
    rest = jnp.exp2(-8.0 / A_HEADS * jnp.arange(1, A_HEADS + 1, dtype=F32)) * LOG2_E
    pieces = []
    for _ in range(ALIBI_PIECES):
        pieces.append(rest.astype(BF16).astype(F32))
        rest = rest - pieces[-1]
    slopes = jnp.stack(pieces, axis=1).reshape(-1)
    x2d = x.reshape(t, d)
    for l in range(depth):
        mod = _ada(c, w_ada[l], b_ada[l])
        sh1, sc1, gt1, sh2, sc2, gt2 = [m.reshape(bsz, 1, d) for m in jnp.split(mod, 6, axis=-1)]
        w_in_p = _permute_in_cols(w_in[l]).astype(BF16)
        proj = _inproj(x2d, g_mix[l].reshape(1, d), sc1, sh1, w_in_p, seq)
        proj3 = proj.reshape(bsz, seq, IN_COLS)
        ya = _moba(proj3, slopes).reshape(t, A_WIDTH)
        yr = _retention(proj3).reshape(t, R_V_WIDTH)
        x1, h2, h2p, eidx_t, wts_t, rank_t, cnt = _mix(
            ya, yr, proj, x2d, w_pa[l].astype(BF16), w_pr[l].astype(BF16), w_out[l].astype(BF16), gt1,
            g_ffn[l].reshape(1, d), sc2, sh2, w_router[l].T.astype(BF16), router_bias[l].reshape(N_EXPERTS, 1), seq)
        experts_tables, pad_start, pad_len, tile_base, n_rows = _slot_tables(cnt, t)
        pos3 = _positions(eidx_t, rank_t, tile_base)
        pos_blocks = jnp.transpose(
            pos3.reshape(pos3.shape[0], TOP_K, -1, SC_DISPATCH_TOKENS), (0, 2, 1, 3)
        ).reshape(t // SC_DISPATCH_TOKENS, TOP_K, SC_DISPATCH_TOKENS)
        xs = _zero_pads(pad_start, pad_len, _sc_dispatch(pos_blocks, h2p, n_rows))
        ys = _experts(*experts_tables, xs, w1[l], w3[l], w2[l])
        wts = wts_t.T
        shared_w = (ws1[l].astype(BF16), ws3[l].astype(BF16), ws2[l].astype(BF16))
        blocks_per_part = pos_blocks.shape[0] // COMBINE_PARTS
        x2d = None
        for part in range(COMBINE_PARTS):
            y_kt = _sc_gather(pos_blocks[part * blocks_per_part:(part + 1) * blocks_per_part], ys,
                              t // COMBINE_PARTS)
            x2d = _combine(y_kt, wts, h2, x1, *shared_w, gt2, g_final.reshape(1, d), seq,
                           part, COMBINE_PARTS, x2d)
    return x2d.reshape(bsz, seq, d)
```

```python
import functools

import jax
import jax.numpy as jnp
import numpy as np
from jax import lax
from jax.experimental import pallas as pl
from jax.experimental.pallas import tpu as pltpu
from jax.experimental.pallas import tpu_sc as plsc

F32 = jnp.float32
BF16 = jnp.bfloat16

D_MODEL = 1024
A_HEADS = 8
A_HEAD_DIM = 64
A_WIDTH = A_HEADS * A_HEAD_DIM
MOBA_BLOCK = 256
MOBA_TOPK = 3
R_HEADS = 8
R_QK_DIM = 64
R_V_DIM = 128
R_QK_WIDTH = R_HEADS * R_QK_DIM
R_V_WIDTH = R_HEADS * R_V_DIM
R_CHUNK = 256
N_EXPERTS = 256
TOP_K = 8
N_GROUPS = 8
GROUP_SIZE = N_EXPERTS // N_GROUPS
TOPK_GROUPS = 4
EXPERT_FF = 256
SHARED_FF = 256
ROUTED_SCALE = 2.5
NORM_EPS = 1e-6
GN_EPS = 1e-6
NEG_INF = -1e30
KNOCKED_OUT = -3e38

COL_VR, COL_GR, COL_GA, COL_GT = 0, 1024, 2048, 3072
COL_QA, COL_KA, COL_VA, COL_QR, COL_KR = 4096, 4608, 5120, 5632, 6144
IN_COLS = 6656
AUG = 128
FEAT_BIAS = A_HEAD_DIM
FEAT_POS = A_HEAD_DIM + 32
ALIBI_PIECES = 3
LOG2_E = 1.4426950408889634
LANES = 128
BF16_ROWS = 16
V_ROWS = A_HEAD_DIM + BF16_ROWS
MOBA_HEADS_PER_STEP = 8

VMEM_LIMIT = 56 * 1024 * 1024


def _cparams(sem, vmem=VMEM_LIMIT):
    return pltpu.CompilerParams(dimension_semantics=sem, vmem_limit_bytes=vmem)


def _dot(a, b):
    return jnp.dot(a, b, preferred_element_type=F32)


def _dot_nt(a, b):
    return lax.dot_general(a, b, (((1,), (1,)), ((), ())), preferred_element_type=F32)


def _sigmoid(x):
    return 1.0 / (1.0 + jnp.exp(-x))


def _silu(x):
    return x * _sigmoid(x)


def _ada_kernel(c_ref, w_ref, b_ref, o_ref):
    c = c_ref[...]
    s = _silu(c)
    s_hi = s.astype(BF16)
    s_lo = (s - s_hi.astype(F32)).astype(BF16)
    w = w_ref[...]
    w_hi = w.astype(BF16)
    w_lo = (w - w_hi.astype(F32)).astype(BF16)
    o_ref[...] = _dot(s_hi, w_hi) + _dot(s_hi, w_lo) + _dot(s_lo, w_hi) + b_ref[...]


def _ada(c, w_ada, b_ada):
    bsz, d = c.shape
    n = w_ada.shape[1]
    tn = 1024
    return pl.pallas_call(
        _ada_kernel,
        grid=(n // tn,),
        in_specs=[pl.BlockSpec((bsz, d), lambda j: (0, 0)),
                  pl.BlockSpec((d, tn), lambda j: (0, j)),
                  pl.BlockSpec((1, tn), lambda j: (0, j))],
        out_specs=pl.BlockSpec((bsz, tn), lambda j: (0, j)),
        out_shape=jax.ShapeDtypeStruct((bsz, n), F32),
        compiler_params=_cparams(("parallel",)),
        name="ada_mod",
    )(c, w_ada, b_ada.reshape(1, n))


INPROJ_COLS = 512


def _inproj_kernel(x_ref, g_ref, sc_ref, sh_ref, w_ref, o_ref):
    x = x_ref[...]
    ms = jnp.mean(x * x, axis=-1, keepdims=True)
    y = x * lax.rsqrt(ms + NORM_EPS) * g_ref[...]
    h = (y * (1.0 + sc_ref[0]) + sh_ref[0]).astype(BF16)
    for j in range(w_ref.shape[1] // INPROJ_COLS):
        cols = slice(j * INPROJ_COLS, (j + 1) * INPROJ_COLS)
        o_ref[:, cols] = _dot(h, w_ref[:, cols]).astype(BF16)


def _inproj(x2d, g, sc, sh, w_bf16, seq):
    t, d = x2d.shape
    n = w_bf16.shape[1]
    tm = min(512, seq)
    per_b = seq // tm
    return pl.pallas_call(
        _inproj_kernel,
        grid=(t // tm,),
        in_specs=[pl.BlockSpec((tm, d), lambda i: (i, 0)),
                  pl.BlockSpec((1, d), lambda i: (0, 0)),
                  pl.BlockSpec((1, 1, d), lambda i: (i // per_b, 0, 0)),
                  pl.BlockSpec((1, 1, d), lambda i: (i // per_b, 0, 0)),
                  pl.BlockSpec((d, n), lambda i: (0, 0))],
        out_specs=pl.BlockSpec((tm, n), lambda i: (i, 0)),
        out_shape=jax.ShapeDtypeStruct((t, n), BF16),
        compiler_params=_cparams(("parallel",)),
        name="norm_inproj",
    )(x2d, g, sc, sh, w_bf16)


def _moba_prepare(i, slopes_ref, q_ref, k_ref, v_ref, ko_ref, qo_ref, vo_ref, kmean_scr):
    nblk = kmean_scr.shape[0]
    width = q_ref.shape[1]
    seq_rows = pl.ds(pl.multiple_of(i * MOBA_BLOCK, MOBA_BLOCK), MOBA_BLOCK)
    q = q_ref[...]
    k = k_ref[...]
    v = v_ref[...]
    kmean_scr[pl.ds(i, 1), :] = jnp.mean(k.astype(F32), axis=0, keepdims=True)

    eye = (lax.broadcasted_iota(jnp.int32, (width, width), 0)
           == lax.broadcasted_iota(jnp.int32, (width, width), 1)).astype(BF16)
    q_t = _dot_nt(eye, q)
    v_t = _dot_nt(eye, v)

    km = kmean_scr[...]
    km_rep = jnp.concatenate([km] * A_HEADS, axis=0)
    r_head = lax.broadcasted_iota(jnp.int32, km_rep.shape, 0) // nblk
    c_head = lax.broadcasted_iota(jnp.int32, km_rep.shape, 1) // A_HEAD_DIM
    km_bd = jnp.where(r_head == c_head, km_rep, 0.0)
    km_hi = km_bd.astype(BF16)
    km_lo = (km_bd - km_hi.astype(F32)).astype(BF16)
    q_t_b = q_t.astype(BF16)
    gate_all = _dot(km_hi, q_t_b) + _dot(km_lo, q_t_b)

    mb = q.shape[0]
    blk = lax.broadcasted_iota(jnp.int32, (nblk, mb), 0)
    lane_pos = lax.broadcasted_iota(jnp.int32, (BF16_ROWS, mb), 1).astype(F32)
    row16 = lax.broadcasted_iota(jnp.int32, (BF16_ROWS, mb), 0)
    key_pos = lax.broadcasted_iota(jnp.int32, (mb, AUG), 0).astype(F32)
    kcol = lax.broadcasted_iota(jnp.int32, (mb, AUG), 1)
    sel_r = lax.broadcasted_iota(jnp.int32, (width, AUG), 0)
    sel_c = lax.broadcasted_iota(jnp.int32, (width, AUG), 1)

    blk_first = (i * mb).astype(F32)
    for h in range(A_HEADS):
        pieces = [slopes_ref[h * ALIBI_PIECES + c] for c in range(ALIBI_PIECES)]
        g = jnp.where(blk < i, gate_all[h * nblk:(h + 1) * nblk, :], NEG_INF)
        sel = jnp.zeros((nblk, mb), jnp.bool_)
        for r in range(MOBA_TOPK):
            m = jnp.max(g, axis=0, keepdims=True)
            idx = jnp.min(jnp.where(g == m, blk, nblk), axis=0, keepdims=True)
            hit = blk == idx
            sel = jnp.logical_or(sel, jnp.logical_and(hit, r < i))
            g = jnp.where(hit, KNOCKED_OUT, g)
        bias_t = jnp.where(sel, 0.0, NEG_INF)

        scale = A_HEAD_DIM ** -0.5 * LOG2_E
        qo_ref[0, h, 0:A_HEAD_DIM, :] = (q_t[h * A_HEAD_DIM:(h + 1) * A_HEAD_DIM, :] * scale).astype(BF16)
        qo_ref[0, h, FEAT_BIAS:FEAT_BIAS + nblk, :] = bias_t.astype(BF16)
        if nblk < 32:
            qo_ref[0, h, FEAT_BIAS + nblk:FEAT_POS, :] = jnp.zeros((32 - nblk, mb), BF16)
        piece_rows = jnp.where(row16 % ALIBI_PIECES == 0, pieces[0],
                               jnp.where(row16 % ALIBI_PIECES == 1, pieces[1], pieces[2]))
        pos_feat = jnp.where(row16 < 3, -lane_pos,
                             jnp.where(row16 < 6, piece_rows,
                                       jnp.where(row16 < 9, -blk_first, jnp.where(row16 < 12, piece_rows, 0.0))))
        qo_ref[0, h, FEAT_POS:FEAT_POS + BF16_ROWS, :] = pos_feat.astype(BF16)
        qo_ref[0, h, FEAT_POS + BF16_ROWS:AUG, :] = jnp.zeros((AUG - FEAT_POS - BF16_ROWS, mb), BF16)

        vo_ref[0, h, 0:A_HEAD_DIM, seq_rows] = v_t[h * A_HEAD_DIM:(h + 1) * A_HEAD_DIM, :].astype(BF16)
        vo_ref[0, h, A_HEAD_DIM:V_ROWS, seq_rows] = jnp.where(row16 == 0, 1.0, 0.0).astype(BF16)

        pick = jnp.where(jnp.logical_and(sel_r == sel_c + h * A_HEAD_DIM, sel_c < A_HEAD_DIM),
                         1.0, 0.0).astype(BF16)
        pos_col = kcol - FEAT_POS
        piece_cols = jnp.where(pos_col % ALIBI_PIECES == 0, pieces[0],
                               jnp.where(pos_col % ALIBI_PIECES == 1, pieces[1], pieces[2]))
        k_feat = jnp.where(
            kcol == FEAT_BIAS + i, 1.0,
            jnp.where(pos_col < 0, 0.0,
                      jnp.where(pos_col < 3, piece_cols,
                                jnp.where(pos_col < 6, key_pos,
                                          jnp.where(pos_col < 9, piece_cols,
                                                    jnp.where(pos_col < 12, blk_first, 0.0))))))
        ko_ref[0, h, seq_rows, :] = (_dot(k, pick) + k_feat).astype(BF16)


def _moba_attend(i, q_ref, k_ref, v_ref, o_ref, s_a, s_b, group, n_groups):
    mb = MOBA_BLOCK
    span = group * mb
    own = pl.multiple_of(i * mb, mb)
    key_i = lax.broadcasted_iota(jnp.int32, (mb, mb), 0)
    qry_i = lax.broadcasted_iota(jnp.int32, (mb, mb), 1)
    feat = lax.broadcasted_iota(jnp.int32, (AUG, mb), 0)
    is_bias = jnp.logical_and(feat >= FEAT_BIAS, feat < FEAT_POS)
    q_ts, carry0 = [], []
    for hh in range(MOBA_HEADS_PER_STEP):
        q_t = q_ref[0, hh]
        q_ts.append(q_t)
        q_own = jnp.where(is_bias, jnp.zeros_like(q_t), q_t)
        s = _dot(k_ref[0, hh, pl.ds(own, mb), :], q_own)
        s = jnp.where(key_i <= qry_i, s, NEG_INF)
        m0 = jnp.max(s, axis=0, keepdims=True)
        p = jnp.exp2(s - m0)
        carry0 += [m0, _dot(v_ref[0, hh, :, pl.ds(own, mb)], p.astype(BF16))]

    def scores(g, dst):
        start = pl.multiple_of(jnp.minimum(g, n_groups - 1) * span, span)
        for hh in range(MOBA_HEADS_PER_STEP):
            dst[hh] = _dot(k_ref[0, hh, pl.ds(start, span), :], q_ts[hh])

    def absorb(g, src, carry):
        start = pl.multiple_of(g * span, span)
        new = []
        for hh in range(MOBA_HEADS_PER_STEP):
            m, acc = carry[2 * hh], carry[2 * hh + 1]
            sb = src[hh]
            m_new = jnp.maximum(m, jnp.max(sb, axis=0, keepdims=True))
            pb = jnp.exp2(sb - m_new)
            alpha = jnp.exp2(m - m_new)
            acc = acc * alpha + _dot(v_ref[0, hh, :, pl.ds(start, span)], pb.astype(BF16))
            new += [m_new, acc]
        return tuple(new)

    def body(pair, carry):
        scores(2 * pair + 1, s_b)
        carry = absorb(2 * pair, s_a, carry)
        scores(2 * pair + 2, s_a)
        return absorb(2 * pair + 1, s_b, carry)

    scores(0, s_a)
    live_groups = (i + group - 1) // group
    res = lax.fori_loop(0, live_groups // 2, body, tuple(carry0))
    res = lax.cond(live_groups % 2 == 1, lambda c: absorb(live_groups - 1, s_a, c), lambda c: c, res)
    outs = [res[2 * hh + 1][0:A_HEAD_DIM, :] / res[2 * hh + 1][A_HEAD_DIM:A_HEAD_DIM + 1, :] for hh in range(MOBA_HEADS_PER_STEP)]
    o_t = jnp.concatenate(outs, axis=0).astype(BF16)
    eye = (key_i == qry_i).astype(BF16)
    o_ref[0] = _dot_nt(eye, o_t).astype(BF16)


def _moba_kernel(slopes_ref, q_ref, k_ref, v_ref, o_ref, kmean_scr, q_scr, k_scr, v_scr, s_a, s_b,
                 *, group, n_groups):
    i = pl.program_id(1)

    @pl.when(i == 0)
    def _():
        kmean_scr[...] = jnp.zeros_like(kmean_scr)
        k_scr[...] = jnp.zeros_like(k_scr)
        v_scr[...] = jnp.zeros_like(v_scr)

    _moba_prepare(i, slopes_ref, q_ref, k_ref, v_ref, k_scr, q_scr, v_scr, kmean_scr)
    _moba_attend(i, q_scr, k_scr, v_scr, o_ref, s_a, s_b, group, n_groups)


def _moba(proj3, slopes):
    bsz, seq, _ = proj3.shape
    mb = MOBA_BLOCK
    nblk = seq // mb
    group = min(2, nblk)
    n_groups = nblk // group
    assert MOBA_HEADS_PER_STEP == A_HEADS
    assert seq % mb == 0 and nblk % group == 0 and nblk <= FEAT_POS - FEAT_BIAS, "one selection feature per key block"
    grid_spec = pltpu.PrefetchScalarGridSpec(
        num_scalar_prefetch=1,
        grid=(bsz, nblk),
        in_specs=[pl.BlockSpec((None, mb, A_WIDTH), lambda b, i, s: (b, i, COL_QA // A_WIDTH)),
                  pl.BlockSpec((None, mb, A_WIDTH), lambda b, i, s: (b, i, COL_KA // A_WIDTH)),
                  pl.BlockSpec((None, mb, A_WIDTH), lambda b, i, s: (b, i, COL_VA // A_WIDTH))],
        out_specs=pl.BlockSpec((1, mb, A_WIDTH), lambda b, i, s: (b, i, 0)),
        scratch_shapes=[pltpu.VMEM((nblk, A_WIDTH), F32),
                        pltpu.VMEM((1, A_HEADS, AUG, mb), BF16),
                        pltpu.VMEM((1, A_HEADS, seq, AUG), BF16),
                        pltpu.VMEM((1, A_HEADS, V_ROWS, seq), BF16),
                        pltpu.VMEM((A_HEADS, group * mb, mb), F32),
                        pltpu.VMEM((A_HEADS, group * mb, mb), F32)],
    )
    return pl.pallas_call(
        functools.partial(_moba_kernel, group=group, n_groups=n_groups),
        grid_spec=grid_spec,
        out_shape=jax.ShapeDtypeStruct((bsz, seq, A_WIDTH), BF16),
        compiler_params=_cparams(("parallel", "arbitrary")),
        name="moba_attention",
    )(slopes, proj3, proj3, proj3)


def _retention_kernel(cdec_ref, q_ref, k_ref, v_ref, g_ref, decay_ref, qdec_ref, kdec_ref, o_ref, state_scr):
    @pl.when(pl.program_id(0) == 0)
    def _():
        state_scr[...] = jnp.zeros_like(state_scr)

    width = q_ref.shape[2]
    eye = (lax.broadcasted_iota(jnp.int32, (width, width), 0)
           == lax.broadcasted_iota(jnp.int32, (width, width), 1)).astype(BF16)
    for b in range(q_ref.shape[0]):
        q = q_ref[b]
        k_t = _dot_nt(eye, k_ref[b])
        k_t_b = k_t.astype(BF16)
        state_b = state_scr[b].astype(BF16)
        for h in range(R_HEADS):
            rows = slice(h * R_QK_DIM, (h + 1) * R_QK_DIM)
            cols = slice(h * R_V_DIM, (h + 1) * R_V_DIM)
            q_h = q[:, rows]
            v_h = v_ref[b, :, cols]
            inner = _dot(q_h, k_t_b[rows, :]) * decay_ref[h]
            out = _dot(inner.astype(BF16), v_h) + _dot(q_h, state_b[rows, :]) * qdec_ref[h]
            k_dec = (k_t[rows, :] * kdec_ref[h]).astype(BF16)
            state_scr[b, rows, :] = cdec_ref[h] * state_scr[b, rows, :] + _dot(k_dec, v_h)
            mu = jnp.mean(out, axis=-1, keepdims=True)
            cen = out - mu
            var = jnp.mean(cen * cen, axis=-1, keepdims=True)
            y = cen * lax.rsqrt(var + GN_EPS)
            o_ref[b, :, cols] = (y * _silu(g_ref[b, :, cols].astype(F32))).astype(BF16)


def _retention_consts():
    h = np.arange(R_HEADS, dtype=np.float64)
    log_g = np.log(1.0 - np.exp2(-5.0 - h))
    n = np.arange(R_CHUNK, dtype=np.float64)
    diff = n[:, None] - n[None, :]
    scale = R_QK_DIM ** -0.5
    decay = np.where(diff >= 0, np.exp(np.maximum(diff, 0.0) * log_g[:, None, None]), 0.0) * scale
    q_decay = np.exp((n + 1.0) * log_g[:, None])[:, :, None]
    k_decay = np.exp((R_CHUNK - 1.0 - n) * log_g[:, None])[:, None, :] * scale
    chunk_decay = np.exp(R_CHUNK * log_g)
    return (jnp.asarray(decay, F32), jnp.asarray(q_decay, F32), jnp.asarray(k_decay, F32),
            jnp.asarray(chunk_decay, F32))


def _retention(proj3):
    bsz, seq, _ = proj3.shape
    c = R_CHUNK
    decay, qdec, kdec, cdec = _retention_consts()
    grid_spec = pltpu.PrefetchScalarGridSpec(
        num_scalar_prefetch=1,
        grid=(seq // c,),
        in_specs=[pl.BlockSpec((bsz, c, R_QK_WIDTH), lambda i, s: (0, i, COL_QR // R_QK_WIDTH)),
                  pl.BlockSpec((bsz, c, R_QK_WIDTH), lambda i, s: (0, i, COL_KR // R_QK_WIDTH)),
                  pl.BlockSpec((bsz, c, R_V_WIDTH), lambda i, s: (0, i, COL_VR // R_V_WIDTH)),
                  pl.BlockSpec((bsz, c, R_V_WIDTH), lambda i, s: (0, i, COL_GR // R_V_WIDTH)),
                  pl.BlockSpec((R_HEADS, c, c), lambda i, s: (0, 0, 0)),
                  pl.BlockSpec((R_HEADS, c, 1), lambda i, s: (0, 0, 0)),
                  pl.BlockSpec((R_HEADS, 1, c), lambda i, s: (0, 0, 0))],
        out_specs=pl.BlockSpec((bsz, c, R_V_WIDTH), lambda i, s: (0, i, 0)),
        scratch_shapes=[pltpu.VMEM((bsz, R_QK_WIDTH, R_V_DIM), F32)],
    )
    return pl.pallas_call(
        _retention_kernel,
        grid_spec=grid_spec,
        out_shape=jax.ShapeDtypeStruct((bsz, seq, R_V_WIDTH), BF16),
        compiler_params=_cparams(("arbitrary",)),
        name="retention",
    )(cdec, proj3, proj3, proj3, proj3, decay, qdec, kdec)


def _pack_halves(x):
    w = x.shape[1] // 2
    bits = lax.bitcast_convert_type(x.astype(BF16).astype(F32), jnp.uint32)
    return (bits[:, :w] >> 16) | (bits[:, w:] & jnp.uint32(0xFFFF0000))


def _unpack_halves(p):
    lo = lax.bitcast_convert_type(p << 16, F32)
    hi = lax.bitcast_convert_type(p & jnp.uint32(0xFFFF0000), F32)
    return lo, hi


SUBLANES = 8


ROUTER_ROWS = 512


def _mix_kernel(ya_ref, yr_ref, ga_ref, gt_ref, x_ref, wpa_ref, wpr_ref, wout_ref,
                gt1_ref, g_ref, sc_ref, sh_ref, wr_ref, rb_ref,
                x1_ref, h2_ref, h2p_ref, e_ref, w_ref, r_ref, c_ref):
    a = _dot(ya_ref[...], wpa_ref[...]) * _sigmoid(ga_ref[...].astype(F32))
    r = _dot(yr_ref[...], wpr_ref[...]) * _sigmoid(gt_ref[...].astype(F32))
    mix = _dot((a + r).astype(BF16), wout_ref[...])
    x1 = x_ref[...] + gt1_ref[0] * mix
    x1_ref[...] = x1
    ms = jnp.mean(x1 * x1, axis=-1, keepdims=True)
    y = x1 * lax.rsqrt(ms + NORM_EPS) * g_ref[...]
    h2 = y * (1.0 + sc_ref[0]) + sh_ref[0]
    h2_b = h2.astype(BF16)
    h2_ref[...] = h2_b
    h2p_ref[...] = _pack_halves(h2)
    _route(h2_b, wr_ref, rb_ref, e_ref, w_ref, r_ref, c_ref)


def _mix(ya, yr, proj, x2d, wpa, wpr, wout, gt1, g, sc, sh, wr_t, bias_col, seq):
    t, d = x2d.shape
    tm = min(ROUTER_ROWS, seq)
    per_b = seq // tm
    row = lambda i: (i, 0)
    full = lambda i: (0, 0)
    by_tile = lambda i: (0, i)
    per_batch = lambda i: (i // per_b, 0, 0)
    return pl.pallas_call(
        _mix_kernel,
        grid=(t // tm,),
        in_specs=[pl.BlockSpec((tm, A_WIDTH), row),
                  pl.BlockSpec((tm, R_V_WIDTH), row),
                  pl.BlockSpec((tm, d), lambda i: (i, COL_GA // D_MODEL)),
                  pl.BlockSpec((tm, d), lambda i: (i, COL_GT // D_MODEL)),
                  pl.BlockSpec((tm, d), row),
                  pl.BlockSpec((A_WIDTH, d), full),
                  pl.BlockSpec((R_V_WIDTH, d), full),
                  pl.BlockSpec((d, d), full),
                  pl.BlockSpec((1, 1, d), per_batch),
                  pl.BlockSpec((1, d), full),
                  pl.BlockSpec((1, 1, d), per_batch),
                  pl.BlockSpec((1, 1, d), per_batch),
                  pl.BlockSpec((N_EXPERTS, d), full),
                  pl.BlockSpec((N_EXPERTS, 1), full)],
        out_specs=[pl.BlockSpec((tm, d), row), pl.BlockSpec((tm, d), row),
                   pl.BlockSpec((tm, d // 2), row),
                   pl.BlockSpec((TOP_K, tm), by_tile), pl.BlockSpec((TOP_K, tm), by_tile),
                   pl.BlockSpec((TOP_K, tm), by_tile), pl.BlockSpec((N_EXPERTS, LANES), by_tile)],
        out_shape=[jax.ShapeDtypeStruct((t, d), F32), jax.ShapeDtypeStruct((t, d), BF16),
                   jax.ShapeDtypeStruct((t, d // 2), jnp.uint32),
                   jax.ShapeDtypeStruct((TOP_K, t), jnp.int32),
                   jax.ShapeDtypeStruct((TOP_K, t), F32),
                   jax.ShapeDtypeStruct((TOP_K, t), jnp.int32),
                   jax.ShapeDtypeStruct((N_EXPERTS, (t // tm) * LANES), F32)],
        compiler_params=_cparams(("parallel",)),
        name="merge_norm_route",
    )(ya, yr, proj, proj, x2d, wpa, wpr, wout, gt1, g, sc, sh, wr_t, bias_col)


def _route(h, wr_ref, b_ref, e_ref, w_ref, r_ref, c_ref):
    logits = _dot_nt(wr_ref[...], h)
    scores = _sigmoid(logits)
    choice = scores + b_ref[...]
    tm = logits.shape[1]
    giota = lax.broadcasted_iota(jnp.int32, (GROUP_SIZE, tm), 0)
    gs_rows = []
    for g in range(N_GROUPS):
        cg = choice[g * GROUP_SIZE:(g + 1) * GROUP_SIZE, :]
        m1 = jnp.max(cg, axis=0, keepdims=True)
        i1 = jnp.min(jnp.where(cg == m1, giota, GROUP_SIZE), axis=0, keepdims=True)
        m2 = jnp.max(jnp.where(giota == i1, KNOCKED_OUT, cg), axis=0, keepdims=True)
        gs_rows.append(m1 + m2)
    gs = jnp.concatenate(gs_rows, axis=0)
    grow = lax.broadcasted_iota(jnp.int32, (N_GROUPS, tm), 0)
    gmask = jnp.zeros((N_GROUPS, tm), jnp.bool_)
    for _ in range(TOPK_GROUPS):
        mx = jnp.max(gs, axis=0, keepdims=True)
        ix = jnp.min(jnp.where(gs == mx, grow, N_GROUPS), axis=0, keepdims=True)
        hit = grow == ix
        gmask = jnp.logical_or(gmask, hit)
        gs = jnp.where(hit, KNOCKED_OUT, gs)
    gmask_f = jnp.where(gmask, 1.0, 0.0)
    masked = jnp.concatenate(
        [jnp.where(gmask_f[g:g + 1, :] > 0.5, choice[g * GROUP_SIZE:(g + 1) * GROUP_SIZE, :], NEG_INF)
         for g in range(N_GROUPS)], axis=0)
    erow = lax.broadcasted_iota(jnp.int32, (N_EXPERTS, tm), 0)
    idx_rows, w_rows = [], []
    chosen = jnp.zeros((N_EXPERTS, tm), F32)
    for _ in range(TOP_K):
        mx = jnp.max(masked, axis=0, keepdims=True)
        ix = jnp.min(jnp.where(masked == mx, erow, N_EXPERTS), axis=0, keepdims=True)
        hit = erow == ix
        w_rows.append(jnp.sum(jnp.where(hit, scores, 0.0), axis=0, keepdims=True))
        idx_rows.append(ix)
        chosen = jnp.where(hit, 1.0, chosen)
        masked = jnp.where(hit, KNOCKED_OUT, masked)
    w = jnp.concatenate(w_rows, axis=0)
    w = w / (jnp.sum(w, axis=0, keepdims=True) + 1e-20) * ROUTED_SCALE
    e_ref[...] = jnp.concatenate(idx_rows, axis=0)
    w_ref[...] = w
    chosen_b = chosen.astype(BF16)
    earlier = (lax.broadcasted_iota(jnp.int32, (tm, tm), 0)
               < lax.broadcasted_iota(jnp.int32, (tm, tm), 1)).astype(BF16)
    before = _dot(chosen_b, earlier)
    ranks = [jnp.sum(jnp.where(erow == ix, before, 0.0), axis=0, keepdims=True) for ix in idx_rows]
    r_ref[...] = jnp.concatenate(ranks, axis=0).astype(jnp.int32)
    c_ref[...] = _dot(chosen_b, jnp.ones((tm, LANES), BF16))


def _pos_kernel(e_ref, r_ref, base_ref, p_ref):
    tm = e_ref.shape[1]
    erow = lax.broadcasted_iota(jnp.int32, (N_EXPERTS, tm), 0)
    base = base_ref[0]
    rows = [jnp.sum(jnp.where(erow == e_ref[k:k + 1, :], base, 0.0), axis=0, keepdims=True)
            for k in range(TOP_K)]
    p_ref[0] = jnp.concatenate(rows, axis=0).astype(jnp.int32) + r_ref[...]


MOVE_ROWS = 512


def _positions(eidx_t, rank_t, tile_base):
    t = eidx_t.shape[1]
    tm = min(MOVE_ROWS, t)
    per_router_tile = ROUTER_ROWS // tm
    return pl.pallas_call(
        _pos_kernel,
        grid=(t // tm,),
        in_specs=[pl.BlockSpec((TOP_K, tm), lambda i: (0, i)),
                  pl.BlockSpec((TOP_K, tm), lambda i: (0, i)),
                  pl.BlockSpec((1, N_EXPERTS, 1), lambda i: (i // per_router_tile, 0, 0))],
        out_specs=pl.BlockSpec((1, TOP_K, tm), lambda i: (i, 0, 0)),
        out_shape=jax.ShapeDtypeStruct((t // tm, TOP_K, tm), jnp.int32),
        compiler_params=_cparams(("parallel",)),
        name="slot_positions",
    )(eidx_t, rank_t, tile_base)


SLOT_ROWS = 512
PAD_CHUNKS = (256, 128, 64, 32, 16, 8)


def _zero_pads_kernel(pad_start_ref, pad_len_ref, xs_in, xs_hbm, zero_buf, pad_sem):
    del xs_in
    zero_buf[...] = jnp.zeros_like(zero_buf)

    def pad_copies(e, wait):
        start = pad_start_ref[e]
        n = pad_len_ref[e]
        head = jnp.minimum((-start) & (SUBLANES - 1), n)

        def fill(first, size, pred):
            @pl.when(pred)
            def _():
                cp = pltpu.make_async_copy(zero_buf.at[pl.ds(0, size), :], xs_hbm.at[pl.ds(first, size), :], pad_sem)
                if wait:
                    cp.wait()
                else:
                    cp.start()

        for j in range(SUBLANES - 1):
            fill(start + j, 1, j < head)
        ptr = start + head
        rest = n - head
        for chunk in PAD_CHUNKS:
            fill(pl.multiple_of(ptr, SUBLANES), chunk, (rest & chunk) != 0)
            ptr = ptr + (rest & chunk)

    def issue(e, carry):
        pad_copies(e, False)
        return carry

    def drain(e, carry):
        pad_copies(e, True)
        return carry

    lax.fori_loop(0, N_EXPERTS, issue, 0)
    lax.fori_loop(0, N_EXPERTS, drain, 0)


def _zero_pads(pad_start, pad_len, xs):
    grid_spec = pltpu.PrefetchScalarGridSpec(
        num_scalar_prefetch=2,
        grid=(1,),
        in_specs=[pl.BlockSpec(memory_space=pl.ANY)],
        out_specs=pl.BlockSpec(memory_space=pl.ANY),
        scratch_shapes=[pltpu.VMEM((PAD_CHUNKS[0], xs.shape[1]), xs.dtype),
                        pltpu.SemaphoreType.DMA],
    )
    return pl.pallas_call(
        _zero_pads_kernel,
        grid_spec=grid_spec,
        out_shape=jax.ShapeDtypeStruct(xs.shape, xs.dtype),
        input_output_aliases={2: 0},
        compiler_params=_cparams(("arbitrary",)),
        name="zero_pad_slots",
    )(pad_start, pad_len, xs)


SC_DISPATCH_TOKENS = 64


def _sc_dispatch(pos_blocks, h_rows, n_rows):
    info = plsc.get_sparse_core_info()
    n_cores = info.num_cores
    n_workers = n_cores * info.num_subcores
    t = h_rows.shape[0]
    chunk = SC_DISPATCH_TOKENS
    steps = t // (n_workers * chunk)
    assert steps * n_workers * chunk == t
    mesh = plsc.VectorSubcoreMesh(core_axis_name="c", subcore_axis_name="s")

    @functools.partial(
        pl.kernel, mesh=mesh,
        out_type=jax.ShapeDtypeStruct((n_rows,) + h_rows.shape[1:], h_rows.dtype),
        scratch_types=[pltpu.VMEM((TOP_K, chunk), jnp.int32),
                       pltpu.VMEM((chunk,) + h_rows.shape[1:], h_rows.dtype),
                       pltpu.SemaphoreType.DMA],
        name="sc_dispatch_rows",
    )
    def scatter_rows(pos_hbm, h_hbm, out_hbm, idx_v, rows_v, sem):
        wid = lax.axis_index("s") * n_cores + lax.axis_index("c")

        @pl.loop(0, steps)
        def _(step):
            blk = wid * steps + step
            pltpu.sync_copy(pos_hbm.at[blk], idx_v)
            pltpu.sync_copy(h_hbm.at[pl.ds(blk * chunk, chunk)], rows_v)
            scatters = [pltpu.make_async_copy(rows_v, out_hbm.at[idx_v.at[k]], sem) for k in range(TOP_K)]
            for cp in scatters:
                cp.start()
            for cp in scatters:
                cp.wait()

    return scatter_rows(pos_blocks, h_rows)


WEIGHT_SLOTS = 4


def _experts_kernel(blk_e_ref, nblk_ref, ord_ref, eid_ref, nord_ref, x_ref, w1_hbm, w3_hbm, w2_hbm, y_ref,
                    w1f, w3f, w2f, w1b, w3b, w2b, wsem):
    s = pl.program_id(0)

    def fetch(j):
        slot = j % WEIGHT_SLOTS
        e = eid_ref[j]
        return [pltpu.make_async_copy(src.at[e], dst.at[slot], wsem.at[slot])
                for src, dst in ((w1_hbm, w1f), (w3_hbm, w3f), (w2_hbm, w2f))]

    @pl.when(s < nblk_ref[0])
    def _():
        j = ord_ref[s]

        ahead = WEIGHT_SLOTS - 1

        @pl.when(s == 0)
        def _():
            for first in range(ahead):
                @pl.when(first < nord_ref[0])
                def _(first=first):
                    for cp in fetch(first):
                        cp.start()

        @pl.when(jnp.logical_or(s == 0, blk_e_ref[s] != blk_e_ref[jnp.maximum(s - 1, 0)]))
        def _():
            for cp in fetch(j):
                cp.wait()

            @pl.when(j + ahead < nord_ref[0])
            def _():
                for cp in fetch(j + ahead):
                    cp.start()

            slot = j % WEIGHT_SLOTS
            w1b[...] = w1f[slot].astype(BF16)
            w3b[...] = w3f[slot].astype(BF16)
            w2b[...] = w2f[slot].astype(BF16)

        half = x_ref.shape[1]
        lo, hi = _unpack_halves(x_ref[...])
        lo = lo.astype(BF16)
        hi = hi.astype(BF16)
        h1 = _dot(lo, w1b[:half, :]) + _dot(hi, w1b[half:, :])
        h3 = _dot(lo, w3b[:half, :]) + _dot(hi, w3b[half:, :])
        mid = (_silu(h1) * h3).astype(BF16)
        y_ref[...] = _pack_halves(_dot(mid, w2b[...]))


def _sc_gather(pos_blocks, ys, t):
    info = plsc.get_sparse_core_info()
    n_cores = info.num_cores
    n_workers = n_cores * info.num_subcores
    chunk = SC_DISPATCH_TOKENS
    steps = t // (n_workers * chunk)
    assert steps * n_workers * chunk == t
    mesh = plsc.VectorSubcoreMesh(core_axis_name="c", subcore_axis_name="s")

    @functools.partial(
        pl.kernel, mesh=mesh,
        out_type=jax.ShapeDtypeStruct((TOP_K * t,) + ys.shape[1:], ys.dtype),
        scratch_types=[pltpu.VMEM((TOP_K, chunk), jnp.int32),
                       pltpu.VMEM((chunk,) + ys.shape[1:], ys.dtype),
                       pltpu.VMEM((chunk,) + ys.shape[1:], ys.dtype),
                       pltpu.SemaphoreType.DMA((2,))],
        name="sc_gather_rows",
    )
    def gather_rows(pos_hbm, ys_hbm, out_hbm, idx_v, rows_a, rows_b, sems):
        wid = lax.axis_index("s") * n_cores + lax.axis_index("c")
        bufs = (rows_a, rows_b)

        @pl.loop(0, steps)
        def _(step):
            blk = wid * steps + step
            pltpu.sync_copy(pos_hbm.at[blk], idx_v)
            gathers = [pltpu.make_async_copy(ys_hbm.at[idx_v.at[k]], bufs[k % 2], sems.at[k % 2])
                       for k in range(TOP_K)]
            gathers[0].start()
            for k in range(TOP_K):
                gathers[k].wait()
                if k + 1 < TOP_K:
                    gathers[k + 1].start()
                pltpu.sync_copy(bufs[k % 2], out_hbm.at[pl.ds(k * t + blk * chunk, chunk)])

    return gather_rows(pos_blocks, ys)


def _experts(blk_e, nblk_used, blk_ord, eid_of_ord, n_ord, xs, w1, w3, w2):
    n_rows, half = xs.shape
    d = D_MODEL
    blk = lambda s, be, nb, bo, eo, no: (jnp.minimum(s, nb[0] - 1), 0)
    grid_spec = pltpu.PrefetchScalarGridSpec(
        num_scalar_prefetch=5,
        grid=(n_rows // SLOT_ROWS,),
        in_specs=[pl.BlockSpec((SLOT_ROWS, half), blk),
                  pl.BlockSpec(memory_space=pl.ANY),
                  pl.BlockSpec(memory_space=pl.ANY),
                  pl.BlockSpec(memory_space=pl.ANY)],
        out_specs=pl.BlockSpec((SLOT_ROWS, half), blk),
        scratch_shapes=[pltpu.VMEM((WEIGHT_SLOTS, d, EXPERT_FF), F32),
                        pltpu.VMEM((WEIGHT_SLOTS, d, EXPERT_FF), F32),
                        pltpu.VMEM((WEIGHT_SLOTS, EXPERT_FF, d), F32),
                        pltpu.VMEM((d, EXPERT_FF), BF16),
                        pltpu.VMEM((d, EXPERT_FF), BF16),
                        pltpu.VMEM((EXPERT_FF, d), BF16),
                        pltpu.SemaphoreType.DMA((WEIGHT_SLOTS,))],
    )
    return pl.pallas_call(
        _experts_kernel,
        grid_spec=grid_spec,
        out_shape=jax.ShapeDtypeStruct((n_rows, half), jnp.uint32),
        compiler_params=_cparams(("arbitrary",)),
        name="routed_experts",
    )(blk_e, nblk_used, blk_ord, eid_of_ord, n_ord, xs, w1, w3, w2)


def _combine_kernel(*refs):
    y_refs = refs[:TOP_K]
    w_ref, h_ref, x1_ref, ws1_ref, ws3_ref, ws2_ref, gt2_ref, g_ref = refs[TOP_K:TOP_K + 8]
    o_ref = refs[-1]
    tm, d = x1_ref.shape
    half = d // 2
    h = h_ref[...]
    mid = (_silu(_dot(h, ws1_ref[...])) * _dot(h, ws3_ref[...])).astype(BF16)
    shared = _dot(mid, ws2_ref[...])
    w = w_ref[...]
    acc_lo = jnp.zeros((tm, half), F32)
    acc_hi = jnp.zeros((tm, half), F32)
    for k in range(TOP_K):
        lo, hi = _unpack_halves(y_refs[k][...])
        acc_lo = acc_lo + lo * w[:, k:k + 1]
        acc_hi = acc_hi + hi * w[:, k:k + 1]
    routed = jnp.concatenate([acc_lo, acc_hi], axis=1)
    x2 = x1_ref[...] + gt2_ref[0] * (routed + shared)
    ms = jnp.mean(x2 * x2, axis=-1, keepdims=True)
    o_ref[...] = x2 * lax.rsqrt(ms + NORM_EPS) * g_ref[...]


COMBINE_ROWS = 256
COMBINE_PARTS = 4


def _combine(y_kt, wts, h2, x1, ws1, ws3, ws2, gt2, g_final, seq, part, n_parts, prev_out):
    t, d = x1.shape
    tm = min(COMBINE_ROWS, seq)
    per_b = seq // tm
    tiles = t // tm // n_parts
    first = part * tiles
    row = lambda i: (first + i, 0)
    full = lambda i: (0, 0)
    y_specs = [pl.BlockSpec((tm, d // 2), functools.partial(lambda i, k: (k * tiles + i, 0), k=k))
               for k in range(TOP_K)]
    in_specs = y_specs + [
        pl.BlockSpec((tm, TOP_K), row),
        pl.BlockSpec((tm, d), row),
        pl.BlockSpec((tm, d), row),
        pl.BlockSpec((d, SHARED_FF), full),
        pl.BlockSpec((d, SHARED_FF), full),
        pl.BlockSpec((SHARED_FF, d), full),
        pl.BlockSpec((1, 1, d), lambda i: ((first + i) // per_b, 0, 0)),
        pl.BlockSpec((1, d), full)]
    args = [y_kt] * TOP_K + [wts, h2, x1, ws1, ws3, ws2, gt2, g_final]
    aliases = {}
    if prev_out is not None:
        in_specs.append(pl.BlockSpec(memory_space=pl.ANY))
        aliases = {len(args): 0}
        args.append(prev_out)
    return pl.pallas_call(
        _combine_kernel,
        grid=(tiles,),
        in_specs=in_specs,
        out_specs=pl.BlockSpec((tm, d), row),
        out_shape=jax.ShapeDtypeStruct((t, d), F32),
        input_output_aliases=aliases,
        compiler_params=_cparams(("parallel",)),
        name="combine_shared_final",
    )(*args)


def _slot_tables(cnt, t):
    ntiles = cnt.shape[1] // LANES
    cnt_tile = cnt.reshape(N_EXPERTS, ntiles, LANES)[:, :, 0].astype(jnp.int32)
    counts = jnp.sum(cnt_tile, axis=1)
    padded = (counts + SLOT_ROWS - 1) // SLOT_ROWS * SLOT_ROWS
    pstart = jnp.cumsum(padded) - padded
    tile_base = pstart[:, None] + jnp.cumsum(cnt_tile, axis=1) - cnt_tile
    n_blk = -(-(t * TOP_K) // SLOT_ROWS) + N_EXPERTS
    blk_end = jnp.cumsum(padded // SLOT_ROWS)
    blk_e = jnp.sum((blk_end[None, :] <= jnp.arange(n_blk)[:, None]).astype(jnp.int32), axis=1)
    blk_e = jnp.minimum(blk_e, N_EXPERTS - 1)
    owns = (padded > 0).astype(jnp.int32)
    ord_of_e = jnp.cumsum(owns) - owns
    ids = jnp.arange(N_EXPERTS, dtype=jnp.int32)
    eid_of_ord = jnp.sum(jnp.where((ord_of_e[None, :] == ids[:, None]) & (owns[None, :] > 0), ids[None, :], 0), axis=1)
    blk_ord = jnp.sum(jnp.where(blk_e[:, None] == ids[None, :], ord_of_e[None, :], 0), axis=1)
    experts_tables = (blk_e, blk_end[-1:].astype(jnp.int32), blk_ord.astype(jnp.int32),
                      eid_of_ord.astype(jnp.int32), jnp.sum(owns).reshape(1).astype(jnp.int32))
    return (experts_tables, pstart + counts, padded - counts,
            tile_base.T.astype(F32).reshape(ntiles, N_EXPERTS, 1), n_blk * SLOT_ROWS)


def _permute_in_cols(w_in):
    qa, ka, va, qr, kr, vr, gr, ga, gt = jnp.split(
        w_in, np.cumsum((A_WIDTH, A_WIDTH, A_WIDTH, R_QK_WIDTH, R_QK_WIDTH, R_V_WIDTH, R_V_WIDTH,
                         D_MODEL))[:].tolist(), axis=1)
    return jnp.concatenate([vr, gr, ga, gt, qa, ka, va, qr, kr], axis=1)


def kernel(x, c, w_ada, b_ada, g_mix, w_in, w_pa, w_pr, w_out, g_ffn, w_router, router_bias,
           w1, w3, w2, ws1, ws3, ws2, g_final):
    bsz, seq, d = x.shape
    t = bsz * seq
    depth = w_ada.shape[0]
    assert depth == 1, "the final norm is fused into the single layer's last kernel"
    rest = jnp.exp2(-8.0 / A_HEADS * jnp.arange(1, A_HEADS + 1, dtype=F32)) * LOG2_E
    pieces = []
    for _ in range(ALIBI_PIECES):
        pieces.append(rest.astype(BF16).astype(F32))
        rest = rest - pieces[-1]
    slopes = jnp.stack(pieces, axis=1).reshape(-1)
    x2d = x.reshape(t, d)
    for l in range(depth):
        mod = _ada(c, w_ada[l], b_ada[l])
        sh1, sc1, gt1, sh2, sc2, gt2 = [m.reshape(bsz, 1, d) for m in jnp.split(mod, 6, axis=-1)]
        w_in_p = _permute_in_cols(w_in[l]).astype(BF16)
        proj = _inproj(x2d, g_mix[l].reshape(1, d), sc1, sh1, w_in_p, seq)
        proj3 = proj.reshape(bsz, seq, IN_COLS)
        ya = _moba(proj3, slopes).reshape(t, A_WIDTH)
        yr = _retention(proj3).reshape(t, R_V_WIDTH)
        x1, h2, h2p, eidx_t, wts_t, rank_t, cnt = _mix(
            ya, yr, proj, x2d, w_pa[l].astype(BF16), w_pr[l].astype(BF16), w_out[l].astype(BF16), gt1,
            g_ffn[l].reshape(1, d), sc2, sh2, w_router[l].T.astype(BF16), router_bias[l].reshape(N_EXPERTS, 1), seq)
        experts_tables, pad_start, pad_len, tile_base, n_rows = _slot_tables(cnt, t)
        pos3 = _positions(eidx_t, rank_t, tile_base)
        pos_blocks = jnp.transpose(
            pos3.reshape(pos3.shape[0], TOP_K, -1, SC_DISPATCH_TOKENS), (0, 2, 1, 3)
        ).reshape(t // SC_DISPATCH_TOKENS, TOP_K, SC_DISPATCH_TOKENS)
        xs = _zero_pads(pad_start, pad_len, _sc_dispatch(pos_blocks, h2p, n_rows))
        ys = _experts(*experts_tables, xs, w1[l], w3[l], w2[l])
        wts = wts_t.T
        shared_w = (ws1[l].astype(BF16), ws3[l].astype(BF16), ws2[l].astype(BF16))
        blocks_per_part = pos_blocks.shape[0] // COMBINE_PARTS
        x2d = None
        for part in range(COMBINE_PARTS):
            y_kt = _sc_gather(pos_blocks[part * blocks_per_part:(part + 1) * blocks_per_part], ys,
                              t // COMBINE_PARTS)
            x2d = _combine(y_kt, wts, h2, x1, *shared_w, gt2, g_final.reshape(1, d), seq,
                           part, COMBINE_PARTS, x2d)
    return x2d.reshape(bsz, seq, d)
```

```python
import functools

import jax
import jax.numpy as jnp
import numpy as np
from jax import lax
from jax.experimental import pallas as pl
from jax.experimental.pallas import tpu as pltpu
from jax.experimental.pallas import tpu_sc as plsc

F32 = jnp.float32
BF16 = jnp.bfloat16

D_MODEL = 1024
A_HEADS = 8
A_HEAD_DIM = 64
A_WIDTH = A_HEADS * A_HEAD_DIM
MOBA_BLOCK = 256
MOBA_TOPK = 3
R_HEADS = 8
R_QK_DIM = 64
R_V_DIM = 128
R_QK_WIDTH = R_HEADS * R_QK_DIM
R_V_WIDTH = R_HEADS * R_V_DIM
R_CHUNK = 256
N_EXPERTS = 256
TOP_K = 8
N_GROUPS = 8
GROUP_SIZE = N_EXPERTS // N_GROUPS
TOPK_GROUPS = 4
EXPERT_FF = 256
SHARED_FF = 256
ROUTED_SCALE = 2.5
NORM_EPS = 1e-6
GN_EPS = 1e-6
NEG_INF = -1e30
KNOCKED_OUT = -3e38

COL_VR, COL_GR, COL_GA, COL_GT = 0, 1024, 2048, 3072
COL_QA, COL_KA, COL_VA, COL_QR, COL_KR = 4096, 4608, 5120, 5632, 6144
IN_COLS = 6656
AUG = 128
FEAT_BIAS = A_HEAD_DIM
FEAT_POS = A_HEAD_DIM + 32
ALIBI_PIECES = 3
LOG2_E = 1.4426950408889634
LANES = 128
BF16_ROWS = 16
V_ROWS = A_HEAD_DIM + BF16_ROWS
MOBA_HEADS_PER_STEP = 8

VMEM_LIMIT = 56 * 1024 * 1024


def _cparams(sem, vmem=VMEM_LIMIT):
    return pltpu.CompilerParams(dimension_semantics=sem, vmem_limit_bytes=vmem)


def _dot(a, b):
    return jnp.dot(a, b, preferred_element_type=F32)


def _dot_nt(a, b):
    return lax.dot_general(a, b, (((1,), (1,)), ((), ())), preferred_element_type=F32)


def _sigmoid(x):
    return 1.0 / (1.0 + jnp.exp(-x))


def _silu(x):
    return x * _sigmoid(x)


def _ada_kernel(c_ref, w_ref, b_ref, o_ref):
    c = c_ref[...]
    s = _silu(c)
    s_hi = s.astype(BF16)
    s_lo = (s - s_hi.astype(F32)).astype(BF16)
    w = w_ref[...]
    w_hi = w.astype(BF16)
    w_lo = (w - w_hi.astype(F32)).astype(BF16)
    o_ref[...] = _dot(s_hi, w_hi) + _dot(s_hi, w_lo) + _dot(s_lo, w_hi) + b_ref[...]


def _ada(c, w_ada, b_ada):
    bsz, d = c.shape
    n = w_ada.shape[1]
    tn = 1024
    return pl.pallas_call(
        _ada_kernel,
        grid=(n // tn,),
        in_specs=[pl.BlockSpec((bsz, d), lambda j: (0, 0)),
                  pl.BlockSpec((d, tn), lambda j: (0, j)),
                  pl.BlockSpec((1, tn), lambda j: (0, j))],
        out_specs=pl.BlockSpec((bsz, tn), lambda j: (0, j)),
        out_shape=jax.ShapeDtypeStruct((bsz, n), F32),
        compiler_params=_cparams(("parallel",)),
        name="ada_mod",
    )(c, w_ada, b_ada.reshape(1, n))


INPROJ_COLS = 512


def _inproj_kernel(x_ref, g_ref, sc_ref, sh_ref, w_ref, o_ref):
    x = x_ref[...]
    ms = jnp.mean(x * x, axis=-1, keepdims=True)
    y = x * lax.rsqrt(ms + NORM_EPS) * g_ref[...]
    h = (y * (1.0 + sc_ref[0]) + sh_ref[0]).astype(BF16)
    for j in range(w_ref.shape[1] // INPROJ_COLS):
        cols = slice(j * INPROJ_COLS, (j + 1) * INPROJ_COLS)
        o_ref[:, cols] = _dot(h, w_ref[:, cols]).astype(BF16)


def _inproj(x2d, g, sc, sh, w_bf16, seq):
    t, d = x2d.shape
    n = w_bf16.shape[1]
    tm = min(512, seq)
    per_b = seq // tm
    return pl.pallas_call(
        _inproj_kernel,
        grid=(t // tm,),
        in_specs=[pl.BlockSpec((tm, d), lambda i: (i, 0)),
                  pl.BlockSpec((1, d), lambda i: (0, 0)),
                  pl.BlockSpec((1, 1, d), lambda i: (i // per_b, 0, 0)),
                  pl.BlockSpec((1, 1, d), lambda i: (i // per_b, 0, 0)),
                  pl.BlockSpec((d, n), lambda i: (0, 0))],
        out_specs=pl.BlockSpec((tm, n), lambda i: (i, 0)),
        out_shape=jax.ShapeDtypeStruct((t, n), BF16),
        compiler_params=_cparams(("parallel",)),
        name="norm_inproj",
    )(x2d, g, sc, sh, w_bf16)


def _moba_prepare(i, slopes_ref, q_ref, k_ref, v_ref, ko_ref, qo_ref, vo_ref, kmean_scr):
    nblk = kmean_scr.shape[0]
    width = q_ref.shape[1]
    seq_rows = pl.ds(pl.multiple_of(i * MOBA_BLOCK, MOBA_BLOCK), MOBA_BLOCK)
    q = q_ref[...]
    k = k_ref[...]
    v = v_ref[...]
    kmean_scr[pl.ds(i, 1), :] = jnp.mean(k.astype(F32), axis=0, keepdims=True)

    eye = (lax.broadcasted_iota(jnp.int32, (width, width), 0)
           == lax.broadcasted_iota(jnp.int32, (width, width), 1)).astype(BF16)
    q_t = _dot_nt(eye, q)
    v_t = _dot_nt(eye, v)

    km = kmean_scr[...]
    km_rep = jnp.concatenate([km] * A_HEADS, axis=0)
    r_head = lax.broadcasted_iota(jnp.int32, km_rep.shape, 0) // nblk
    c_head = lax.broadcasted_iota(jnp.int32, km_rep.shape, 1) // A_HEAD_DIM
    km_bd = jnp.where(r_head == c_head, km_rep, 0.0)
    km_hi = km_bd.astype(BF16)
    km_lo = (km_bd - km_hi.astype(F32)).astype(BF16)
    q_t_b = q_t.astype(BF16)
    gate_all = _dot(km_hi, q_t_b) + _dot(km_lo, q_t_b)

    mb = q.shape[0]
    blk = lax.broadcasted_iota(jnp.int32, (nblk, mb), 0)
    lane_pos = lax.broadcasted_iota(jnp.int32, (BF16_ROWS, mb), 1).astype(F32)
    row16 = lax.broadcasted_iota(jnp.int32, (BF16_ROWS, mb), 0)
    key_pos = lax.broadcasted_iota(jnp.int32, (mb, AUG), 0).astype(F32)
    kcol = lax.broadcasted_iota(jnp.int32, (mb, AUG), 1)
    sel_r = lax.broadcasted_iota(jnp.int32, (width, AUG), 0)
    sel_c = lax.broadcasted_iota(jnp.int32, (width, AUG), 1)

    blk_first = (i * mb).astype(F32)
    for h in range(A_HEADS):
        pieces = [slopes_ref[h * ALIBI_PIECES + c] for c in range(ALIBI_PIECES)]
        g = jnp.where(blk < i, gate_all[h * nblk:(h + 1) * nblk, :], NEG_INF)
        sel = jnp.zeros((nblk, mb), jnp.bool_)
        for r in range(MOBA_TOPK):
            m = jnp.max(g, axis=0, keepdims=True)
            idx = jnp.min(jnp.where(g == m, blk, nblk), axis=0, keepdims=True)
            hit = blk == idx
            sel = jnp.logical_or(sel, jnp.logical_and(hit, r < i))
            g = jnp.where(hit, KNOCKED_OUT, g)
        bias_t = jnp.where(sel, 0.0, NEG_INF)

        scale = A_HEAD_DIM ** -0.5 * LOG2_E
        qo_ref[0, h, 0:A_HEAD_DIM, :] = (q_t[h * A_HEAD_DIM:(h + 1) * A_HEAD_DIM, :] * scale).astype(BF16)
        qo_ref[0, h, FEAT_BIAS:FEAT_BIAS + nblk, :] = bias_t.astype(BF16)
        if nblk < 32:
            qo_ref[0, h, FEAT_BIAS + nblk:FEAT_POS, :] = jnp.zeros((32 - nblk, mb), BF16)
        piece_rows = jnp.where(row16 % ALIBI_PIECES == 0, pieces[0],
                               jnp.where(row16 % ALIBI_PIECES == 1, pieces[1], pieces[2]))
        pos_feat = jnp.where(row16 < 3, -lane_pos,
                             jnp.where(row16 < 6, piece_rows,
                                       jnp.where(row16 < 9, -blk_first, jnp.where(row16 < 12, piece_rows, 0.0))))
        qo_ref[0, h, FEAT_POS:FEAT_POS + BF16_ROWS, :] = pos_feat.astype(BF16)
        qo_ref[0, h, FEAT_POS + BF16_ROWS:AUG, :] = jnp.zeros((AUG - FEAT_POS - BF16_ROWS, mb), BF16)

        vo_ref[0, h, 0:A_HEAD_DIM, seq_rows] = v_t[h * A_HEAD_DIM:(h + 1) * A_HEAD_DIM, :].astype(BF16)
        vo_ref[0, h, A_HEAD_DIM:V_ROWS, seq_rows] = jnp.where(row16 == 0, 1.0, 0.0).astype(BF16)

        pick = jnp.where(jnp.logical_and(sel_r == sel_c + h * A_HEAD_DIM, sel_c < A_HEAD_DIM),
                         1.0, 0.0).astype(BF16)
        pos_col = kcol - FEAT_POS
        piece_cols = jnp.where(pos_col % ALIBI_PIECES == 0, pieces[0],
                               jnp.where(pos_col % ALIBI_PIECES == 1, pieces[1], pieces[2]))
        k_feat = jnp.where(
            kcol == FEAT_BIAS + i, 1.0,
            jnp.where(pos_col < 0, 0.0,
                      jnp.where(pos_col < 3, piece_cols,
                                jnp.where(pos_col < 6, key_pos,
                                          jnp.where(pos_col < 9, piece_cols,
                                                    jnp.where(pos_col < 12, blk_first, 0.0))))))
        ko_ref[0, h, seq_rows, :] = (_dot(k, pick) + k_feat).astype(BF16)


def _moba_attend(i, q_ref, k_ref, v_ref, o_ref, s_a, s_b, group, n_groups):
    mb = MOBA_BLOCK
    span = group * mb
    own = pl.multiple_of(i * mb, mb)
    key_i = lax.broadcasted_iota(jnp.int32, (mb, mb), 0)
    qry_i = lax.broadcasted_iota(jnp.int32, (mb, mb), 1)
    feat = lax.broadcasted_iota(jnp.int32, (AUG, mb), 0)
    is_bias = jnp.logical_and(feat >= FEAT_BIAS, feat < FEAT_POS)
    q_ts, carry0 = [], []
    for hh in range(MOBA_HEADS_PER_STEP):
        q_t = q_ref[0, hh]
        q_ts.append(q_t)
        q_own = jnp.where(is_bias, jnp.zeros_like(q_t), q_t)
        s = _dot(k_ref[0, hh, pl.ds(own, mb), :], q_own)
        s = jnp.where(key_i <= qry_i, s, NEG_INF)
        m0 = jnp.max(s, axis=0, keepdims=True)
        p = jnp.exp2(s - m0)
        carry0 += [m0, _dot(v_ref[0, hh, :, pl.ds(own, mb)], p.astype(BF16))]

    def scores(g, dst):
        start = pl.multiple_of(jnp.minimum(g, n_groups - 1) * span, span)
        for hh in range(MOBA_HEADS_PER_STEP):
            dst[hh] = _dot(k_ref[0, hh, pl.ds(start, span), :], q_ts[hh])

    def absorb(g, src, carry):
        start = pl.multiple_of(g * span, span)
        new = []
        for hh in range(MOBA_HEADS_PER_STEP):
            m, acc = carry[2 * hh], carry[2 * hh + 1]
            sb = src[hh]
            m_new = jnp.maximum(m, jnp.max(sb, axis=0, keepdims=True))
            pb = jnp.exp2(sb - m_new)
            alpha = jnp.exp2(m - m_new)
            acc = acc * alpha + _dot(v_ref[0, hh, :, pl.ds(start, span)], pb.astype(BF16))
            new += [m_new, acc]
        return tuple(new)

    def body(pair, carry):
        scores(2 * pair + 1, s_b)
        carry = absorb(2 * pair, s_a, carry)
        scores(2 * pair + 2, s_a)
        return absorb(2 * pair + 1, s_b, carry)

    scores(0, s_a)
    live_groups = (i + group - 1) // group
    res = lax.fori_loop(0, live_groups // 2, body, tuple(carry0))
    res = lax.cond(live_groups % 2 == 1, lambda c: absorb(live_groups - 1, s_a, c), lambda c: c, res)
    outs = [res[2 * hh + 1][0:A_HEAD_DIM, :] / res[2 * hh + 1][A_HEAD_DIM:A_HEAD_DIM + 1, :] for hh in range(MOBA_HEADS_PER_STEP)]
    o_t = jnp.concatenate(outs, axis=0).astype(BF16)
    eye = (key_i == qry_i).astype(BF16)
    o_ref[0] = _dot_nt(eye, o_t).astype(BF16)


def _moba_kernel(slopes_ref, q_ref, k_ref, v_ref, o_ref, kmean_scr, q_scr, k_scr, v_scr, s_a, s_b,
                 *, group, n_groups):
    i = pl.program_id(1)

    @pl.when(i == 0)
    def _():
        kmean_scr[...] = jnp.zeros_like(kmean_scr)
        k_scr[...] = jnp.zeros_like(k_scr)
        v_scr[...] = jnp.zeros_like(v_scr)

    _moba_prepare(i, slopes_ref, q_ref, k_ref, v_ref, k_scr, q_scr, v_scr, kmean_scr)
    _moba_attend(i, q_scr, k_scr, v_scr, o_ref, s_a, s_b, group, n_groups)


def _moba(proj3, slopes):
    bsz, seq, _ = proj3.shape
    mb = MOBA_BLOCK
    nblk = seq // mb
    group = min(2, nblk)
    n_groups = nblk // group
    assert MOBA_HEADS_PER_STEP == A_HEADS
    assert seq % mb == 0 and nblk % group == 0 and nblk <= FEAT_POS - FEAT_BIAS, "one selection feature per key block"
    grid_spec = pltpu.PrefetchScalarGridSpec(
        num_scalar_prefetch=1,
        grid=(bsz, nblk),
        in_specs=[pl.BlockSpec((None, mb, A_WIDTH), lambda b, i, s: (b, i, COL_QA // A_WIDTH)),
                  pl.BlockSpec((None, mb, A_WIDTH), lambda b, i, s: (b, i, COL_KA // A_WIDTH)),
                  pl.BlockSpec((None, mb, A_WIDTH), lambda b, i, s: (b, i, COL_VA // A_WIDTH))],
        out_specs=pl.BlockSpec((1, mb, A_WIDTH), lambda b, i, s: (b, i, 0)),
        scratch_shapes=[pltpu.VMEM((nblk, A_WIDTH), F32),
                        pltpu.VMEM((1, A_HEADS, AUG, mb), BF16),
                        pltpu.VMEM((1, A_HEADS, seq, AUG), BF16),
                        pltpu.VMEM((1, A_HEADS, V_ROWS, seq), BF16),
                        pltpu.VMEM((A_HEADS, group * mb, mb), F32),
                        pltpu.VMEM((A_HEADS, group * mb, mb), F32)],
    )
    return pl.pallas_call(
        functools.partial(_moba_kernel, group=group, n_groups=n_groups),
        grid_spec=grid_spec,
        out_shape=jax.ShapeDtypeStruct((bsz, seq, A_WIDTH), BF16),
        compiler_params=_cparams(("parallel", "arbitrary")),
        name="moba_attention",
    )(slopes, proj3, proj3, proj3)


def _retention_kernel(cdec_ref, q_ref, k_ref, v_ref, g_ref, decay_ref, qdec_ref, kdec_ref, o_ref, state_scr):
    @pl.when(pl.program_id(0) == 0)
    def _():
        state_scr[...] = jnp.zeros_like(state_scr)

    width = q_ref.shape[2]
    eye = (lax.broadcasted_iota(jnp.int32, (width, width), 0)
           == lax.broadcasted_iota(jnp.int32, (width, width), 1)).astype(BF16)
    for b in range(q_ref.shape[0]):
        q = q_ref[b]
        k_t = _dot_nt(eye, k_ref[b])
        k_t_b = k_t.astype(BF16)
        state_b = state_scr[b].astype(BF16)
        for h in range(R_HEADS):
            rows = slice(h * R_QK_DIM, (h + 1) * R_QK_DIM)
            cols = slice(h * R_V_DIM, (h + 1) * R_V_DIM)
            q_h = q[:, rows]
            v_h = v_ref[b, :, cols]
            inner = _dot(q_h, k_t_b[rows, :]) * decay_ref[h]
            out = _dot(inner.astype(BF16), v_h) + _dot(q_h, state_b[rows, :]) * qdec_ref[h]
            k_dec = (k_t[rows, :] * kdec_ref[h]).astype(BF16)
            state_scr[b, rows, :] = cdec_ref[h] * state_scr[b, rows, :] + _dot(k_dec, v_h)
            mu = jnp.mean(out, axis=-1, keepdims=True)
            cen = out - mu
            var = jnp.mean(cen * cen, axis=-1, keepdims=True)
            y = cen * lax.rsqrt(var + GN_EPS)
            o_ref[b, :, cols] = (y * _silu(g_ref[b, :, cols].astype(F32))).astype(BF16)


def _retention_consts():
    h = np.arange(R_HEADS, dtype=np.float64)
    log_g = np.log(1.0 - np.exp2(-5.0 - h))
    n = np.arange(R_CHUNK, dtype=np.float64)
    diff = n[:, None] - n[None, :]
    scale = R_QK_DIM ** -0.5
    decay = np.where(diff >= 0, np.exp(np.maximum(diff, 0.0) * log_g[:, None, None]), 0.0) * scale
    q_decay = np.exp((n + 1.0) * log_g[:, None])[:, :, None]
    k_decay = np.exp((R_CHUNK - 1.0 - n) * log_g[:, None])[:, None, :] * scale
    chunk_decay = np.exp(R_CHUNK * log_g)
    return (jnp.asarray(decay, F32), jnp.asarray(q_decay, F32), jnp.asarray(k_decay, F32),
            jnp.asarray(chunk_decay, F32))


def _retention(proj3):
    bsz, seq, _ = proj3.shape
    c = R_CHUNK
    decay, qdec, kdec, cdec = _retention_consts()
    grid_spec = pltpu.PrefetchScalarGridSpec(
        num_scalar_prefetch=1,
        grid=(seq // c,),
        in_specs=[pl.BlockSpec((bsz, c, R_QK_WIDTH), lambda i, s: (0, i, COL_QR // R_QK_WIDTH)),
                  pl.BlockSpec((bsz, c, R_QK_WIDTH), lambda i, s: (0, i, COL_KR // R_QK_WIDTH)),
                  pl.BlockSpec((bsz, c, R_V_WIDTH), lambda i, s: (0, i, COL_VR // R_V_WIDTH)),
                  pl.BlockSpec((bsz, c, R_V_WIDTH), lambda i, s: (0, i, COL_GR // R_V_WIDTH)),
                  pl.BlockSpec((R_HEADS, c, c), lambda i, s: (0, 0, 0)),
                  pl.BlockSpec((R_HEADS, c, 1), lambda i, s: (0, 0, 0)),
                  pl.BlockSpec((R_HEADS, 1, c), lambda i, s: (0, 0, 0))],
        out_specs=pl.BlockSpec((bsz, c, R_V_WIDTH), lambda i, s: (0, i, 0)),
        scratch_shapes=[pltpu.VMEM((bsz, R_QK_WIDTH, R_V_DIM), F32)],
    )
    return pl.pallas_call(
        _retention_kernel,
        grid_spec=grid_spec,
        out_shape=jax.ShapeDtypeStruct((bsz, seq, R_V_WIDTH), BF16),
        compiler_params=_cparams(("arbitrary",)),
        name="retention",
    )(cdec, proj3, proj3, proj3, proj3, decay, qdec, kdec)


def _pack_halves(x):
    w = x.shape[1] // 2
    bits = lax.bitcast_convert_type(x.astype(BF16).astype(F32), jnp.uint32)
    return (bits[:, :w] >> 16) | (bits[:, w:] & jnp.uint32(0xFFFF0000))


def _unpack_halves(p):
    lo = lax.bitcast_convert_type(p << 16, F32)
    hi = lax.bitcast_convert_type(p & jnp.uint32(0xFFFF0000), F32)
    return lo, hi


SUBLANES = 8


ROUTER_ROWS = 512


def _mix_kernel(ya_ref, yr_ref, ga_ref, gt_ref, x_ref, wpa_ref, wpr_ref, wout_ref,
                gt1_ref, g_ref, sc_ref, sh_ref, wr_ref, rb_ref,
                x1_ref, h2_ref, h2p_ref, e_ref, w_ref, r_ref, c_ref):
    a = _dot(ya_ref[...], wpa_ref[...]) * _sigmoid(ga_ref[...].astype(F32))
    r = _dot(yr_ref[...], wpr_ref[...]) * _sigmoid(gt_ref[...].astype(F32))
    mix = _dot((a + r).astype(BF16), wout_ref[...])
    x1 = x_ref[...] + gt1_ref[0] * mix
    x1_ref[...] = x1
    ms = jnp.mean(x1 * x1, axis=-1, keepdims=True)
    y = x1 * lax.rsqrt(ms + NORM_EPS) * g_ref[...]
    h2 = y * (1.0 + sc_ref[0]) + sh_ref[0]
    h2_b = h2.astype(BF16)
    h2_ref[...] = h2_b
    h2p_ref[...] = _pack_halves(h2)
    _route(h2_b, wr_ref, rb_ref, e_ref, w_ref, r_ref, c_ref)


def _mix(ya, yr, proj, x2d, wpa, wpr, wout, gt1, g, sc, sh, wr_t, bias_col, seq):
    t, d = x2d.shape
    tm = min(ROUTER_ROWS, seq)
    per_b = seq // tm
    row = lambda i: (i, 0)
    full = lambda i: (0, 0)
    by_tile = lambda i: (0, i)
    per_batch = lambda i: (i // per_b, 0, 0)
    return pl.pallas_call(
        _mix_kernel,
        grid=(t // tm,),
        in_specs=[pl.BlockSpec((tm, A_WIDTH), row),
                  pl.BlockSpec((tm, R_V_WIDTH), row),
                  pl.BlockSpec((tm, d), lambda i: (i, COL_GA // D_MODEL)),
                  pl.BlockSpec((tm, d), lambda i: (i, COL_GT // D_MODEL)),
                  pl.BlockSpec((tm, d), row),
                  pl.BlockSpec((A_WIDTH, d), full),
                  pl.BlockSpec((R_V_WIDTH, d), full),
                  pl.BlockSpec((d, d), full),
                  pl.BlockSpec((1, 1, d), per_batch),
                  pl.BlockSpec((1, d), full),
                  pl.BlockSpec((1, 1, d), per_batch),
                  pl.BlockSpec((1, 1, d), per_batch),
                  pl.BlockSpec((N_EXPERTS, d), full),
                  pl.BlockSpec((N_EXPERTS, 1), full)],
        out_specs=[pl.BlockSpec((tm, d), row), pl.BlockSpec((tm, d), row),
                   pl.BlockSpec((tm, d // 2), row),
                   pl.BlockSpec((TOP_K, tm), by_tile), pl.BlockSpec((TOP_K, tm), by_tile),
                   pl.BlockSpec((TOP_K, tm), by_tile), pl.BlockSpec((N_EXPERTS, LANES), by_tile)],
        out_shape=[jax.ShapeDtypeStruct((t, d), F32), jax.ShapeDtypeStruct((t, d), BF16),
                   jax.ShapeDtypeStruct((t, d // 2), jnp.uint32),
                   jax.ShapeDtypeStruct((TOP_K, t), jnp.int32),
                   jax.ShapeDtypeStruct((TOP_K, t), F32),
                   jax.ShapeDtypeStruct((TOP_K, t), jnp.int32),
                   jax.ShapeDtypeStruct((N_EXPERTS, (t // tm) * LANES), F32)],
        compiler_params=_cparams(("parallel",)),
        name="merge_norm_route",
    )(ya, yr, proj, proj, x2d, wpa, wpr, wout, gt1, g, sc, sh, wr_t, bias_col)


def _route(h, wr_ref, b_ref, e_ref, w_ref, r_ref, c_ref):
    logits = _dot_nt(wr_ref[...], h)
    scores = _sigmoid(logits)
    choice = scores + b_ref[...]
    tm = logits.shape[1]
    giota = lax.broadcasted_iota(jnp.int32, (GROUP_SIZE, tm), 0)
    gs_rows = []
    for g in range(N_GROUPS):
        cg = choice[g * GROUP_SIZE:(g + 1) * GROUP_SIZE, :]
        m1 = jnp.max(cg, axis=0, keepdims=True)
        i1 = jnp.min(jnp.where(cg == m1, giota, GROUP_SIZE), axis=0, keepdims=True)
        m2 = jnp.max(jnp.where(giota == i1, KNOCKED_OUT, cg), axis=0, keepdims=True)
        gs_rows.append(m1 + m2)
    gs = jnp.concatenate(gs_rows, axis=0)
    grow = lax.broadcasted_iota(jnp.int32, (N_GROUPS, tm), 0)
    gmask = jnp.zeros((N_GROUPS, tm), jnp.bool_)
    for _ in range(TOPK_GROUPS):
        mx = jnp.max(gs, axis=0, keepdims=True)
        ix = jnp.min(jnp.where(gs == mx, grow, N_GROUPS), axis=0, keepdims=True)
        hit = grow == ix
        gmask = jnp.logical_or(gmask, hit)
        gs = jnp.where(hit, KNOCKED_OUT, gs)
    gmask_f = jnp.where(gmask, 1.0, 0.0)
    masked = jnp.concatenate(
        [jnp.where(gmask_f[g:g + 1, :] > 0.5, choice[g * GROUP_SIZE:(g + 1) * GROUP_SIZE, :], NEG_INF)
         for g in range(N_GROUPS)], axis=0)
    erow = lax.broadcasted_iota(jnp.int32, (N_EXPERTS, tm), 0)
    idx_rows, w_rows = [], []
    chosen = jnp.zeros((N_EXPERTS, tm), F32)
    for _ in range(TOP_K):
        mx = jnp.max(masked, axis=0, keepdims=True)
        ix = jnp.min(jnp.where(masked == mx, erow, N_EXPERTS), axis=0, keepdims=True)
        hit = erow == ix
        w_rows.append(jnp.sum(jnp.where(hit, scores, 0.0), axis=0, keepdims=True))
        idx_rows.append(ix)
        chosen = jnp.where(hit, 1.0, chosen)
        masked = jnp.where(hit, KNOCKED_OUT, masked)
    w = jnp.concatenate(w_rows, axis=0)
    w = w / (jnp.sum(w, axis=0, keepdims=True) + 1e-20) * ROUTED_SCALE
    e_ref[...] = jnp.concatenate(idx_rows, axis=0)
    w_ref[...] = w
    chosen_b = chosen.astype(BF16)
    earlier = (lax.broadcasted_iota(jnp.int32, (tm, tm), 0)
               < lax.broadcasted_iota(jnp.int32, (tm, tm), 1)).astype(BF16)
    before = _dot(chosen_b, earlier)
    ranks = [jnp.sum(jnp.where(erow == ix, before, 0.0), axis=0, keepdims=True) for ix in idx_rows]
    r_ref[...] = jnp.concatenate(ranks, axis=0).astype(jnp.int32)
    c_ref[...] = _dot(chosen_b, jnp.ones((tm, LANES), BF16))


def _pos_kernel(e_ref, r_ref, base_ref, p_ref):
    tm = e_ref.shape[1]
    erow = lax.broadcasted_iota(jnp.int32, (N_EXPERTS, tm), 0)
    base = base_ref[0]
    rows = [jnp.sum(jnp.where(erow == e_ref[k:k + 1, :], base, 0.0), axis=0, keepdims=True)
            for k in range(TOP_K)]
    p_ref[0] = jnp.concatenate(rows, axis=0).astype(jnp.int32) + r_ref[...]


MOVE_ROWS = 512


def _positions(eidx_t, rank_t, tile_base):
    t = eidx_t.shape[1]
    tm = min(MOVE_ROWS, t)
    per_router_tile = ROUTER_ROWS // tm
    return pl.pallas_call(
        _pos_kernel,
        grid=(t // tm,),
        in_specs=[pl.BlockSpec((TOP_K, tm), lambda i: (0, i)),
                  pl.BlockSpec((TOP_K, tm), lambda i: (0, i)),
                  pl.BlockSpec((1, N_EXPERTS, 1), lambda i: (i // per_router_tile, 0, 0))],
        out_specs=pl.BlockSpec((1, TOP_K, tm), lambda i: (i, 0, 0)),
        out_shape=jax.ShapeDtypeStruct((t // tm, TOP_K, tm), jnp.int32),
        compiler_params=_cparams(("parallel",)),
        name="slot_positions",
    )(eidx_t, rank_t, tile_base)


SLOT_ROWS = 512
PAD_CHUNKS = (256, 128, 64, 32, 16, 8)


def _zero_pads_kernel(pad_start_ref, pad_len_ref, xs_in, xs_hbm, zero_buf, pad_sem):
    del xs_in
    zero_buf[...] = jnp.zeros_like(zero_buf)

    def pad_copies(e, wait):
        start = pad_start_ref[e]
        n = pad_len_ref[e]
        head = jnp.minimum((-start) & (SUBLANES - 1), n)

        def fill(first, size, pred):
            @pl.when(pred)
            def _():
                cp = pltpu.make_async_copy(zero_buf.at[pl.ds(0, size), :], xs_hbm.at[pl.ds(first, size), :], pad_sem)
                if wait:
                    cp.wait()
                else:
                    cp.start()

        for j in range(SUBLANES - 1):
            fill(start + j, 1, j < head)
        ptr = start + head
        rest = n - head
        for chunk in PAD_CHUNKS:
            fill(pl.multiple_of(ptr, SUBLANES), chunk, (rest & chunk) != 0)
            ptr = ptr + (rest & chunk)

    def issue(e, carry):
        pad_copies(e, False)
        return carry

    def drain(e, carry):
        pad_copies(e, True)
        return carry

    lax.fori_loop(0, N_EXPERTS, issue, 0)
    lax.fori_loop(0, N_EXPERTS, drain, 0)


def _zero_pads(pad_start, pad_len, xs):
    grid_spec = pltpu.PrefetchScalarGridSpec(
        num_scalar_prefetch=2,
        grid=(1,),
        in_specs=[pl.BlockSpec(memory_space=pl.ANY)],
        out_specs=pl.BlockSpec(memory_space=pl.ANY),
        scratch_shapes=[pltpu.VMEM((PAD_CHUNKS[0], xs.shape[1]), xs.dtype),
                        pltpu.SemaphoreType.DMA],
    )
    return pl.pallas_call(
        _zero_pads_kernel,
        grid_spec=grid_spec,
        out_shape=jax.ShapeDtypeStruct(xs.shape, xs.dtype),
        input_output_aliases={2: 0},
        compiler_params=_cparams(("arbitrary",)),
        name="zero_pad_slots",
    )(pad_start, pad_len, xs)


SC_DISPATCH_TOKENS = 64


def _sc_dispatch(pos_blocks, h_rows, n_rows):
    info = plsc.get_sparse_core_info()
    n_cores = info.num_cores
    n_workers = n_cores * info.num_subcores
    t = h_rows.shape[0]
    chunk = SC_DISPATCH_TOKENS
    steps = t // (n_workers * chunk)
    assert steps * n_workers * chunk == t
    mesh = plsc.VectorSubcoreMesh(core_axis_name="c", subcore_axis_name="s")

    @functools.partial(
        pl.kernel, mesh=mesh,
        out_type=jax.ShapeDtypeStruct((n_rows,) + h_rows.shape[1:], h_rows.dtype),
        scratch_types=[pltpu.VMEM((TOP_K, chunk), jnp.int32),
                       pltpu.VMEM((chunk,) + h_rows.shape[1:], h_rows.dtype),
                       pltpu.SemaphoreType.DMA],
        name="sc_dispatch_rows",
    )
    def scatter_rows(pos_hbm, h_hbm, out_hbm, idx_v, rows_v, sem):
        wid = lax.axis_index("s") * n_cores + lax.axis_index("c")

        @pl.loop(0, steps)
        def _(step):
            blk = wid * steps + step
            pltpu.sync_copy(pos_hbm.at[blk], idx_v)
            pltpu.sync_copy(h_hbm.at[pl.ds(blk * chunk, chunk)], rows_v)
            scatters = [pltpu.make_async_copy(rows_v, out_hbm.at[idx_v.at[k]], sem) for k in range(TOP_K)]
            for cp in scatters:
                cp.start()
            for cp in scatters:
                cp.wait()

    return scatter_rows(pos_blocks, h_rows)


WEIGHT_SLOTS = 4


def _experts_kernel(blk_e_ref, nblk_ref, ord_ref, eid_ref, nord_ref, x_ref, w1_hbm, w3_hbm, w2_hbm, y_ref,
                    w1f, w3f, w2f, wsem):
    s = pl.program_id(0)

    def fetch(j):
        slot = j % WEIGHT_SLOTS
        e = eid_ref[j]
        return [pltpu.make_async_copy(src.at[e], dst.at[slot], wsem.at[slot])
                for src, dst in ((w1_hbm, w1f), (w3_hbm, w3f), (w2_hbm, w2f))]

    @pl.when(s < nblk_ref[0])
    def _():
        j = ord_ref[s]

        ahead = WEIGHT_SLOTS - 1

        @pl.when(s == 0)
        def _():
            for first in range(ahead):
                @pl.when(first < nord_ref[0])
                def _(first=first):
                    for cp in fetch(first):
                        cp.start()

        @pl.when(jnp.logical_or(s == 0, blk_e_ref[s] != blk_e_ref[jnp.maximum(s - 1, 0)]))
        def _():
            for cp in fetch(j):
                cp.wait()

            @pl.when(j + ahead < nord_ref[0])
            def _():
                for cp in fetch(j + ahead):
                    cp.start()

        slot = j % WEIGHT_SLOTS
        half = x_ref.shape[1]
        lo, hi = _unpack_halves(x_ref[...])
        lo = lo.astype(BF16)
        hi = hi.astype(BF16)
        h1 = (_dot(lo, w1f[slot, :half, :].astype(BF16)) + _dot(hi, w1f[slot, half:, :].astype(BF16)))
        h3 = (_dot(lo, w3f[slot, :half, :].astype(BF16)) + _dot(hi, w3f[slot, half:, :].astype(BF16)))
        mid = (_silu(h1) * h3).astype(BF16)
        y_ref[...] = _pack_halves(_dot(mid, w2f[slot].astype(BF16)))


def _sc_gather(pos_blocks, ys, t):
    info = plsc.get_sparse_core_info()
    n_cores = info.num_cores
    n_workers = n_cores * info.num_subcores
    chunk = SC_DISPATCH_TOKENS
    steps = t // (n_workers * chunk)
    assert steps * n_workers * chunk == t
    mesh = plsc.VectorSubcoreMesh(core_axis_name="c", subcore_axis_name="s")

    @functools.partial(
        pl.kernel, mesh=mesh,
        out_type=jax.ShapeDtypeStruct((TOP_K * t,) + ys.shape[1:], ys.dtype),
        scratch_types=[pltpu.VMEM((TOP_K, chunk), jnp.int32),
                       pltpu.VMEM((chunk,) + ys.shape[1:], ys.dtype),
                       pltpu.VMEM((chunk,) + ys.shape[1:], ys.dtype),
                       pltpu.SemaphoreType.DMA((2,))],
        name="sc_gather_rows",
    )
    def gather_rows(pos_hbm, ys_hbm, out_hbm, idx_v, rows_a, rows_b, sems):
        wid = lax.axis_index("s") * n_cores + lax.axis_index("c")
        bufs = (rows_a, rows_b)

        @pl.loop(0, steps)
        def _(step):
            blk = wid * steps + step
            pltpu.sync_copy(pos_hbm.at[blk], idx_v)
            gathers = [pltpu.make_async_copy(ys_hbm.at[idx_v.at[k]], bufs[k % 2], sems.at[k % 2])
                       for k in range(TOP_K)]
            gathers[0].start()
            for k in range(TOP_K):
                gathers[k].wait()
                if k + 1 < TOP_K:
                    gathers[k + 1].start()
                pltpu.sync_copy(bufs[k % 2], out_hbm.at[pl.ds(k * t + blk * chunk, chunk)])

    return gather_rows(pos_blocks, ys)


def _experts(blk_e, nblk_used, blk_ord, eid_of_ord, n_ord, xs, w1, w3, w2):
    n_rows, half = xs.shape
    d = D_MODEL
    blk = lambda s, be, nb, bo, eo, no: (jnp.minimum(s, nb[0] - 1), 0)
    grid_spec = pltpu.PrefetchScalarGridSpec(
        num_scalar_prefetch=5,
        grid=(n_rows // SLOT_ROWS,),
        in_specs=[pl.BlockSpec((SLOT_ROWS, half), blk),
                  pl.BlockSpec(memory_space=pl.ANY),
                  pl.BlockSpec(memory_space=pl.ANY),
                  pl.BlockSpec(memory_space=pl.ANY)],
        out_specs=pl.BlockSpec((SLOT_ROWS, half), blk),
        scratch_shapes=[pltpu.VMEM((WEIGHT_SLOTS, d, EXPERT_FF), F32),
                        pltpu.VMEM((WEIGHT_SLOTS, d, EXPERT_FF), F32),
                        pltpu.VMEM((WEIGHT_SLOTS, EXPERT_FF, d), F32),
                        pltpu.SemaphoreType.DMA((WEIGHT_SLOTS,))],
    )
    return pl.pallas_call(
        _experts_kernel,
        grid_spec=grid_spec,
        out_shape=jax.ShapeDtypeStruct((n_rows, half), jnp.uint32),
        compiler_params=_cparams(("arbitrary",)),
        name="routed_experts",
    )(blk_e, nblk_used, blk_ord, eid_of_ord, n_ord, xs, w1, w3, w2)


def _combine_kernel(*refs):
    y_refs = refs[:TOP_K]
    w_ref, h_ref, x1_ref, ws1_ref, ws3_ref, ws2_ref, gt2_ref, g_ref = refs[TOP_K:TOP_K + 8]
    o_ref = refs[-1]
    tm, d = x1_ref.shape
    half = d // 2
    h = h_ref[...]
    mid = (_silu(_dot(h, ws1_ref[...])) * _dot(h, ws3_ref[...])).astype(BF16)
    shared = _dot(mid, ws2_ref[...])
    w = w_ref[...]
    acc_lo = jnp.zeros((tm, half), F32)
    acc_hi = jnp.zeros((tm, half), F32)
    for k in range(TOP_K):
        lo, hi = _unpack_halves(y_refs[k][...])
        acc_lo = acc_lo + lo * w[:, k:k + 1]
        acc_hi = acc_hi + hi * w[:, k:k + 1]
    routed = jnp.concatenate([acc_lo, acc_hi], axis=1)
    x2 = x1_ref[...] + gt2_ref[0] * (routed + shared)
    ms = jnp.mean(x2 * x2, axis=-1, keepdims=True)
    o_ref[...] = x2 * lax.rsqrt(ms + NORM_EPS) * g_ref[...]


COMBINE_ROWS = 256
COMBINE_PARTS = 4


def _combine(y_kt, wts, h2, x1, ws1, ws3, ws2, gt2, g_final, seq, part, n_parts, prev_out):
    t, d = x1.shape
    tm = min(COMBINE_ROWS, seq)
    per_b = seq // tm
    tiles = t // tm // n_parts
    first = part * tiles
    row = lambda i: (first + i, 0)
    full = lambda i: (0, 0)
    y_specs = [pl.BlockSpec((tm, d // 2), functools.partial(lambda i, k: (k * tiles + i, 0), k=k))
               for k in range(TOP_K)]
    in_specs = y_specs + [
        pl.BlockSpec((tm, TOP_K), row),
        pl.BlockSpec((tm, d), row),
        pl.BlockSpec((tm, d), row),
        pl.BlockSpec((d, SHARED_FF), full),
        pl.BlockSpec((d, SHARED_FF), full),
        pl.BlockSpec((SHARED_FF, d), full),
        pl.BlockSpec((1, 1, d), lambda i: ((first + i) // per_b, 0, 0)),
        pl.BlockSpec((1, d), full)]
    args = [y_kt] * TOP_K + [wts, h2, x1, ws1, ws3, ws2, gt2, g_final]
    aliases = {}
    if prev_out is not None:
        in_specs.append(pl.BlockSpec(memory_space=pl.ANY))
        aliases = {len(args): 0}
        args.append(prev_out)
    return pl.pallas_call(
        _combine_kernel,
        grid=(tiles,),
        in_specs=in_specs,
        out_specs=pl.BlockSpec((tm, d), row),
        out_shape=jax.ShapeDtypeStruct((t, d), F32),
        input_output_aliases=aliases,
        compiler_params=_cparams(("parallel",)),
        name="combine_shared_final",
    )(*args)


def _slot_tables(cnt, t):
    ntiles = cnt.shape[1] // LANES
    cnt_tile = cnt.reshape(N_EXPERTS, ntiles, LANES)[:, :, 0].astype(jnp.int32)
    counts = jnp.sum(cnt_tile, axis=1)
    padded = (counts + SLOT_ROWS - 1) // SLOT_ROWS * SLOT_ROWS
    pstart = jnp.cumsum(padded) - padded
    tile_base = pstart[:, None] + jnp.cumsum(cnt_tile, axis=1) - cnt_tile
    n_blk = -(-(t * TOP_K) // SLOT_ROWS) + N_EXPERTS
    blk_end = jnp.cumsum(padded // SLOT_ROWS)
    blk_e = jnp.sum((blk_end[None, :] <= jnp.arange(n_blk)[:, None]).astype(jnp.int32), axis=1)
    blk_e = jnp.minimum(blk_e, N_EXPERTS - 1)
    owns = (padded > 0).astype(jnp.int32)
    ord_of_e = jnp.cumsum(owns) - owns
    ids = jnp.arange(N_EXPERTS, dtype=jnp.int32)
    eid_of_ord = jnp.sum(jnp.where((ord_of_e[None, :] == ids[:, None]) & (owns[None, :] > 0), ids[None, :], 0), axis=1)
    blk_ord = jnp.sum(jnp.where(blk_e[:, None] == ids[None, :], ord_of_e[None, :], 0), axis=1)
    experts_tables = (blk_e, blk_end[-1:].astype(jnp.int32), blk_ord.astype(jnp.int32),
                      eid_of_ord.astype(jnp.int32), jnp.sum(owns).reshape(1).astype(jnp.int32))
    return (experts_tables, pstart + counts, padded - counts,
            tile_base.T.astype(F32).reshape(ntiles, N_EXPERTS, 1), n_blk * SLOT_ROWS)


def _permute_in_cols(w_in):
    qa, ka, va, qr, kr, vr, gr, ga, gt = jnp.split(
        w_in, np.cumsum((A_WIDTH, A_WIDTH, A_WIDTH, R_QK_WIDTH, R_QK_WIDTH, R_V_WIDTH, R_V_WIDTH,
                         D_MODEL))[:].tolist(), axis=1)
    return jnp.concatenate([vr, gr, ga, gt, qa, ka, va, qr, kr], axis=1)


def kernel(x, c, w_ada, b_ada, g_mix, w_in, w_pa, w_pr, w_out, g_ffn, w_router, router_bias,
           w1, w3, w2, ws1, ws3, ws2, g_final):
    bsz, seq, d = x.shape
    t = bsz * seq
    depth = w_ada.shape[0]
    assert depth == 1, "the final norm is fused into the single layer's last kernel"
    rest = jnp.exp2(-8.0 / A_HEADS * jnp.arange(1, A_HEADS + 1, dtype=F32)) * LOG2_E
    pieces = []
    for _ in range(ALIBI_PIECES):
        pieces.append(rest.astype(BF16).astype(F32))
        rest = rest - pieces[-1]
    slopes = jnp.stack(pieces, axis=1).reshape(-1)
    x2d = x.reshape(t, d)
    for l in range(depth):
        mod = _ada(c, w_ada[l], b_ada[l])
        sh1, sc1, gt1, sh2, sc2, gt2 = [m.reshape(bsz, 1, d) for m in jnp.split(mod, 6, axis=-1)]
        w_in_p = _permute_in_cols(w_in[l]).astype(BF16)
        proj = _inproj(x2d, g_mix[l].reshape(1, d), sc1, sh1, w_in_p, seq)
        proj3 = proj.reshape(bsz, seq, IN_COLS)
        ya = _moba(proj3, slopes).reshape(t, A_WIDTH)
        yr = _retention(proj3).reshape(t, R_V_WIDTH)
        x1, h2, h2p, eidx_t, wts_t, rank_t, cnt = _mix(
            ya, yr, proj, x2d, w_pa[l].astype(BF16), w_pr[l].astype(BF16), w_out[l].astype(BF16), gt1,
            g_ffn[l].reshape(1, d), sc2, sh2, w_router[l].T.astype(BF16), router_bias[l].reshape(N_EXPERTS, 1), seq)
        experts_tables, pad_start, pad_len, tile_base, n_rows = _slot_tables(cnt, t)
        pos3 = _positions(eidx_t, rank_t, tile_base)
        pos_blocks = jnp.transpose(
            pos3.reshape(pos3.shape[0], TOP_K, -1, SC_DISPATCH_TOKENS), (0, 2, 1, 3)
        ).reshape(t // SC_DISPATCH_TOKENS, TOP_K, SC_DISPATCH_TOKENS)
        xs = _zero_pads(pad_start, pad_len, _sc_dispatch(pos_blocks, h2p, n_rows))
        ys = _experts(*experts_tables, xs, w1[l], w3[l], w2[l])
        wts = wts_t.T
        shared_w = (ws1[l].astype(BF16), ws3[l].astype(BF16), ws2[l].astype(BF16))
        blocks_per_part = pos_blocks.shape[0] // COMBINE_PARTS
        x2d = None
        for part in range(COMBINE_PARTS):
            y_kt = _sc_gather(pos_blocks[part * blocks_per_part:(part + 1) * blocks_per_part], ys,
                              t // COMBINE_PARTS)
            x2d = _combine(y_kt, wts, h2, x1, *shared_w, gt2, g_final.reshape(1, d), seq,
                           part, COMBINE_PARTS, x2d)
    return x2d.reshape(bsz, seq, d)
```

```python
import functools

import jax
import jax.numpy as jnp
import numpy as np
from jax import lax
from jax.experimental import pallas as pl
from jax.experimental.pallas import tpu as pltpu
from jax.experimental.pallas import tpu_sc as plsc

F32 = jnp.float32
BF16 = jnp.bfloat16

D_MODEL = 1024
A_HEADS = 8
A_HEAD_DIM = 64
A_WIDTH = A_HEADS * A_HEAD_DIM
MOBA_BLOCK = 256
MOBA_TOPK = 3
R_HEADS = 8
R_QK_DIM = 64
R_V_DIM = 128
R_QK_WIDTH = R_HEADS * R_QK_DIM
R_V_WIDTH = R_HEADS * R_V_DIM
R_CHUNK = 256
N_EXPERTS = 256
TOP_K = 8
N_GROUPS = 8
GROUP_SIZE = N_EXPERTS // N_GROUPS
TOPK_GROUPS = 4
EXPERT_FF = 256
SHARED_FF = 256
ROUTED_SCALE = 2.5
NORM_EPS = 1e-6
GN_EPS = 1e-6
NEG_INF = -1e30
KNOCKED_OUT = -3e38

COL_VR, COL_GR, COL_GA, COL_GT = 0, 1024, 2048, 3072
COL_QA, COL_KA, COL_VA, COL_QR, COL_KR = 4096, 4608, 5120, 5632, 6144
IN_COLS = 6656
AUG = 128
FEAT_BIAS = A_HEAD_DIM
FEAT_POS = A_HEAD_DIM + 32
ALIBI_PIECES = 3
LOG2_E = 1.4426950408889634
LANES = 128
BF16_ROWS = 16
V_ROWS = A_HEAD_DIM + BF16_ROWS
MOBA_HEADS_PER_STEP = 8

VMEM_LIMIT = 56 * 1024 * 1024


def _cparams(sem, vmem=VMEM_LIMIT):
    return pltpu.CompilerParams(dimension_semantics=sem, vmem_limit_bytes=vmem)


def _dot(a, b):
    return jnp.dot(a, b, preferred_element_type=F32)


def _dot_nt(a, b):
    return lax.dot_general(a, b, (((1,), (1,)), ((), ())), preferred_element_type=F32)


def _sigmoid(x):
    return 1.0 / (1.0 + jnp.exp(-x))


def _silu(x):
    return x * _sigmoid(x)


def _ada_kernel(c_ref, w_ref, b_ref, o_ref):
    c = c_ref[...]
    s = _silu(c)
    s_hi = s.astype(BF16)
    s_lo = (s - s_hi.astype(F32)).astype(BF16)
    w = w_ref[...]
    w_hi = w.astype(BF16)
    w_lo = (w - w_hi.astype(F32)).astype(BF16)
    o_ref[...] = _dot(s_hi, w_hi) + _dot(s_hi, w_lo) + _dot(s_lo, w_hi) + b_ref[...]


def _ada(c, w_ada, b_ada):
    bsz, d = c.shape
    n = w_ada.shape[1]
    tn = 1024
    return pl.pallas_call(
        _ada_kernel,
        grid=(n // tn,),
        in_specs=[pl.BlockSpec((bsz, d), lambda j: (0, 0)),
                  pl.BlockSpec((d, tn), lambda j: (0, j)),
                  pl.BlockSpec((1, tn), lambda j: (0, j))],
        out_specs=pl.BlockSpec((bsz, tn), lambda j: (0, j)),
        out_shape=jax.ShapeDtypeStruct((bsz, n), F32),
        compiler_params=_cparams(("parallel",)),
        name="ada_mod",
    )(c, w_ada, b_ada.reshape(1, n))


INPROJ_COLS = 512


def _inproj_kernel(x_ref, g_ref, sc_ref, sh_ref, w_ref, o_ref):
    x = x_ref[...]
    ms = jnp.mean(x * x, axis=-1, keepdims=True)
    y = x * lax.rsqrt(ms + NORM_EPS) * g_ref[...]
    h = (y * (1.0 + sc_ref[0]) + sh_ref[0]).astype(BF16)
    for j in range(w_ref.shape[1] // INPROJ_COLS):
        cols = slice(j * INPROJ_COLS, (j + 1) * INPROJ_COLS)
        o_ref[:, cols] = _dot(h, w_ref[:, cols]).astype(BF16)


def _inproj(x2d, g, sc, sh, w_bf16, seq):
    t, d = x2d.shape
    n = w_bf16.shape[1]
    tm = min(512, seq)
    per_b = seq // tm
    return pl.pallas_call(
        _inproj_kernel,
        grid=(t // tm,),
        in_specs=[pl.BlockSpec((tm, d), lambda i: (i, 0)),
                  pl.BlockSpec((1, d), lambda i: (0, 0)),
                  pl.BlockSpec((1, 1, d), lambda i: (i // per_b, 0, 0)),
                  pl.BlockSpec((1, 1, d), lambda i: (i // per_b, 0, 0)),
                  pl.BlockSpec((d, n), lambda i: (0, 0))],
        out_specs=pl.BlockSpec((tm, n), lambda i: (i, 0)),
        out_shape=jax.ShapeDtypeStruct((t, n), BF16),
        compiler_params=_cparams(("parallel",)),
        name="norm_inproj",
    )(x2d, g, sc, sh, w_bf16)


def _moba_prepare(i, slopes_ref, q_ref, k_ref, v_ref, ko_ref, qo_ref, vo_ref, kmean_scr):
    nblk = kmean_scr.shape[0]
    width = q_ref.shape[1]
    seq_rows = pl.ds(pl.multiple_of(i * MOBA_BLOCK, MOBA_BLOCK), MOBA_BLOCK)
    q = q_ref[...]
    k = k_ref[...]
    v = v_ref[...]
    kmean_scr[pl.ds(i, 1), :] = jnp.mean(k.astype(F32), axis=0, keepdims=True)

    eye = (lax.broadcasted_iota(jnp.int32, (A_HEAD_DIM, A_HEAD_DIM), 0)
           == lax.broadcasted_iota(jnp.int32, (A_HEAD_DIM, A_HEAD_DIM), 1)).astype(BF16)
    heads = [slice(h * A_HEAD_DIM, (h + 1) * A_HEAD_DIM) for h in range(width // A_HEAD_DIM)]
    q_t = jnp.concatenate([_dot_nt(eye, q[:, hd]) for hd in heads], axis=0)
    v_t = jnp.concatenate([_dot_nt(eye, v[:, hd]) for hd in heads], axis=0)

    km = kmean_scr[...]
    km_rep = jnp.concatenate([km] * A_HEADS, axis=0)
    r_head = lax.broadcasted_iota(jnp.int32, km_rep.shape, 0) // nblk
    c_head = lax.broadcasted_iota(jnp.int32, km_rep.shape, 1) // A_HEAD_DIM
    km_bd = jnp.where(r_head == c_head, km_rep, 0.0)
    km_hi = km_bd.astype(BF16)
    km_lo = (km_bd - km_hi.astype(F32)).astype(BF16)
    q_t_b = q_t.astype(BF16)
    gate_all = _dot(km_hi, q_t_b) + _dot(km_lo, q_t_b)

    mb = q.shape[0]
    blk = lax.broadcasted_iota(jnp.int32, (nblk, mb), 0)
    lane_pos = lax.broadcasted_iota(jnp.int32, (BF16_ROWS, mb), 1).astype(F32)
    row16 = lax.broadcasted_iota(jnp.int32, (BF16_ROWS, mb), 0)
    key_pos = lax.broadcasted_iota(jnp.int32, (mb, AUG), 0).astype(F32)
    kcol = lax.broadcasted_iota(jnp.int32, (mb, AUG), 1)
    widen = (lax.broadcasted_iota(jnp.int32, (A_HEAD_DIM, AUG), 0)
             == lax.broadcasted_iota(jnp.int32, (A_HEAD_DIM, AUG), 1)).astype(BF16)

    blk_first = (i * mb).astype(F32)
    for h in range(A_HEADS):
        pieces = [slopes_ref[h * ALIBI_PIECES + c] for c in range(ALIBI_PIECES)]
        g = jnp.where(blk < i, gate_all[h * nblk:(h + 1) * nblk, :], NEG_INF)
        sel = jnp.zeros((nblk, mb), jnp.bool_)
        for r in range(MOBA_TOPK):
            m = jnp.max(g, axis=0, keepdims=True)
            idx = jnp.min(jnp.where(g == m, blk, nblk), axis=0, keepdims=True)
            hit = blk == idx
            sel = jnp.logical_or(sel, jnp.logical_and(hit, r < i))
            g = jnp.where(hit, KNOCKED_OUT, g)
        bias_t = jnp.where(sel, 0.0, NEG_INF)

        scale = A_HEAD_DIM ** -0.5 * LOG2_E
        qo_ref[0, h, 0:A_HEAD_DIM, :] = (q_t[h * A_HEAD_DIM:(h + 1) * A_HEAD_DIM, :] * scale).astype(BF16)
        qo_ref[0, h, FEAT_BIAS:FEAT_BIAS + nblk, :] = bias_t.astype(BF16)
        if nblk < 32:
            qo_ref[0, h, FEAT_BIAS + nblk:FEAT_POS, :] = jnp.zeros((32 - nblk, mb), BF16)
        piece_rows = jnp.where(row16 % ALIBI_PIECES == 0, pieces[0],
                               jnp.where(row16 % ALIBI_PIECES == 1, pieces[1], pieces[2]))
        pos_feat = jnp.where(row16 < 3, -lane_pos,
                             jnp.where(row16 < 6, piece_rows,
                                       jnp.where(row16 < 9, -blk_first, jnp.where(row16 < 12, piece_rows, 0.0))))
        qo_ref[0, h, FEAT_POS:FEAT_POS + BF16_ROWS, :] = pos_feat.astype(BF16)
        qo_ref[0, h, FEAT_POS + BF16_ROWS:AUG, :] = jnp.zeros((AUG - FEAT_POS - BF16_ROWS, mb), BF16)

        vo_ref[0, h, 0:A_HEAD_DIM, seq_rows] = v_t[h * A_HEAD_DIM:(h + 1) * A_HEAD_DIM, :].astype(BF16)
        vo_ref[0, h, A_HEAD_DIM:V_ROWS, seq_rows] = jnp.where(row16 == 0, 1.0, 0.0).astype(BF16)

        pos_col = kcol - FEAT_POS
        piece_cols = jnp.where(pos_col % ALIBI_PIECES == 0, pieces[0],
                               jnp.where(pos_col % ALIBI_PIECES == 1, pieces[1], pieces[2]))
        k_feat = jnp.where(
            kcol == FEAT_BIAS + i, 1.0,
            jnp.where(pos_col < 0, 0.0,
                      jnp.where(pos_col < 3, piece_cols,
                                jnp.where(pos_col < 6, key_pos,
                                          jnp.where(pos_col < 9, piece_cols,
                                                    jnp.where(pos_col < 12, blk_first, 0.0))))))
        k_wide = _dot(k[:, h * A_HEAD_DIM:(h + 1) * A_HEAD_DIM], widen)
        ko_ref[0, h, seq_rows, :] = (k_wide + k_feat).astype(BF16)


def _moba_attend(i, q_ref, k_ref, v_ref, o_ref, s_a, s_b, group, n_groups):
    mb = MOBA_BLOCK
    span = group * mb
    own = pl.multiple_of(i * mb, mb)
    key_i = lax.broadcasted_iota(jnp.int32, (mb, mb), 0)
    qry_i = lax.broadcasted_iota(jnp.int32, (mb, mb), 1)
    feat = lax.broadcasted_iota(jnp.int32, (AUG, mb), 0)
    is_bias = jnp.logical_and(feat >= FEAT_BIAS, feat < FEAT_POS)
    q_ts, carry0 = [], []
    for hh in range(MOBA_HEADS_PER_STEP):
        q_t = q_ref[0, hh]
        q_ts.append(q_t)
        q_own = jnp.where(is_bias, jnp.zeros_like(q_t), q_t)
        s = _dot(k_ref[0, hh, pl.ds(own, mb), :], q_own)
        s = jnp.where(key_i <= qry_i, s, NEG_INF)
        m0 = jnp.max(s, axis=0, keepdims=True)
        p = jnp.exp2(s - m0)
        carry0 += [m0, _dot(v_ref[0, hh, :, pl.ds(own, mb)], p.astype(BF16))]

    def scores(g, dst):
        start = pl.multiple_of(jnp.minimum(g, n_groups - 1) * span, span)
        for hh in range(MOBA_HEADS_PER_STEP):
            dst[hh] = _dot(k_ref[0, hh, pl.ds(start, span), :], q_ts[hh])

    def absorb(g, src, carry):
        start = pl.multiple_of(g * span, span)
        new = []
        for hh in range(MOBA_HEADS_PER_STEP):
            m, acc = carry[2 * hh], carry[2 * hh + 1]
            sb = src[hh]
            m_new = jnp.maximum(m, jnp.max(sb, axis=0, keepdims=True))
            pb = jnp.exp2(sb - m_new)
            alpha = jnp.exp2(m - m_new)
            acc = acc * alpha + _dot(v_ref[0, hh, :, pl.ds(start, span)], pb.astype(BF16))
            new += [m_new, acc]
        return tuple(new)

    def body(pair, carry):
        scores(2 * pair + 1, s_b)
        carry = absorb(2 * pair, s_a, carry)
        scores(2 * pair + 2, s_a)
        return absorb(2 * pair + 1, s_b, carry)

    scores(0, s_a)
    live_groups = (i + group - 1) // group
    res = lax.fori_loop(0, live_groups // 2, body, tuple(carry0))
    res = lax.cond(live_groups % 2 == 1, lambda c: absorb(live_groups - 1, s_a, c), lambda c: c, res)
    outs = [res[2 * hh + 1][0:A_HEAD_DIM, :] / res[2 * hh + 1][A_HEAD_DIM:A_HEAD_DIM + 1, :] for hh in range(MOBA_HEADS_PER_STEP)]
    o_t = jnp.concatenate(outs, axis=0).astype(BF16)
    eye = (key_i == qry_i).astype(BF16)
    o_ref[0] = _dot_nt(eye, o_t).astype(BF16)


def _moba_kernel(slopes_ref, q_ref, k_ref, v_ref, o_ref, kmean_scr, q_scr, k_scr, v_scr, s_a, s_b,
                 *, group, n_groups):
    i = pl.program_id(1)

    @pl.when(i == 0)
    def _():
        kmean_scr[...] = jnp.zeros_like(kmean_scr)
        k_scr[...] = jnp.zeros_like(k_scr)
        v_scr[...] = jnp.zeros_like(v_scr)

    _moba_prepare(i, slopes_ref, q_ref, k_ref, v_ref, k_scr, q_scr, v_scr, kmean_scr)
    _moba_attend(i, q_scr, k_scr, v_scr, o_ref, s_a, s_b, group, n_groups)


def _moba(proj3, slopes):
    bsz, seq, _ = proj3.shape
    mb = MOBA_BLOCK
    nblk = seq // mb
    group = min(2, nblk)
    n_groups = nblk // group
    assert MOBA_HEADS_PER_STEP == A_HEADS
    assert seq % mb == 0 and nblk % group == 0 and nblk <= FEAT_POS - FEAT_BIAS, "one selection feature per key block"
    grid_spec = pltpu.PrefetchScalarGridSpec(
        num_scalar_prefetch=1,
        grid=(bsz, nblk),
        in_specs=[pl.BlockSpec((None, mb, A_WIDTH), lambda b, i, s: (b, i, COL_QA // A_WIDTH)),
                  pl.BlockSpec((None, mb, A_WIDTH), lambda b, i, s: (b, i, COL_KA // A_WIDTH)),
                  pl.BlockSpec((None, mb, A_WIDTH), lambda b, i, s: (b, i, COL_VA // A_WIDTH))],
        out_specs=pl.BlockSpec((1, mb, A_WIDTH), lambda b, i, s: (b, i, 0)),
        scratch_shapes=[pltpu.VMEM((nblk, A_WIDTH), F32),
                        pltpu.VMEM((1, A_HEADS, AUG, mb), BF16),
                        pltpu.VMEM((1, A_HEADS, seq, AUG), BF16),
                        pltpu.VMEM((1, A_HEADS, V_ROWS, seq), BF16),
                        pltpu.VMEM((A_HEADS, group * mb, mb), F32),
                        pltpu.VMEM((A_HEADS, group * mb, mb), F32)],
    )
    return pl.pallas_call(
        functools.partial(_moba_kernel, group=group, n_groups=n_groups),
        grid_spec=grid_spec,
        out_shape=jax.ShapeDtypeStruct((bsz, seq, A_WIDTH), BF16),
        compiler_params=_cparams(("parallel", "arbitrary")),
        name="moba_attention",
    )(slopes, proj3, proj3, proj3)


def _retention_kernel(cdec_ref, q_ref, k_ref, v_ref, g_ref, decay_ref, qdec_ref, kdec_ref, o_ref, state_scr):
    @pl.when(pl.program_id(0) == 0)
    def _():
        state_scr[...] = jnp.zeros_like(state_scr)

    width = q_ref.shape[2]
    eye = (lax.broadcasted_iota(jnp.int32, (width, width), 0)
           == lax.broadcasted_iota(jnp.int32, (width, width), 1)).astype(BF16)
    for b in range(q_ref.shape[0]):
        q = q_ref[b]
        k_t = _dot_nt(eye, k_ref[b])
        k_t_b = k_t.astype(BF16)
        state_b = state_scr[b].astype(BF16)
        for h in range(R_HEADS):
            rows = slice(h * R_QK_DIM, (h + 1) * R_QK_DIM)
            cols = slice(h * R_V_DIM, (h + 1) * R_V_DIM)
            q_h = q[:, rows]
            v_h = v_ref[b, :, cols]
            inner = _dot(q_h, k_t_b[rows, :]) * decay_ref[h]
            out = _dot(inner.astype(BF16), v_h) + _dot(q_h, state_b[rows, :]) * qdec_ref[h]
            k_dec = (k_t[rows, :] * kdec_ref[h]).astype(BF16)
            state_scr[b, rows, :] = cdec_ref[h] * state_scr[b, rows, :] + _dot(k_dec, v_h)
            mu = jnp.mean(out, axis=-1, keepdims=True)
            cen = out - mu
            var = jnp.mean(cen * cen, axis=-1, keepdims=True)
            y = cen * lax.rsqrt(var + GN_EPS)
            o_ref[b, :, cols] = (y * _silu(g_ref[b, :, cols].astype(F32))).astype(BF16)


def _retention_consts():
    h = np.arange(R_HEADS, dtype=np.float64)
    log_g = np.log(1.0 - np.exp2(-5.0 - h))
    n = np.arange(R_CHUNK, dtype=np.float64)
    diff = n[:, None] - n[None, :]
    scale = R_QK_DIM ** -0.5
    decay = np.where(diff >= 0, np.exp(np.maximum(diff, 0.0) * log_g[:, None, None]), 0.0) * scale
    q_decay = np.exp((n + 1.0) * log_g[:, None])[:, :, None]
    k_decay = np.exp((R_CHUNK - 1.0 - n) * log_g[:, None])[:, None, :] * scale
    chunk_decay = np.exp(R_CHUNK * log_g)
    return (jnp.asarray(decay, F32), jnp.asarray(q_decay, F32), jnp.asarray(k_decay, F32),
            jnp.asarray(chunk_decay, F32))


def _retention(proj3):
    bsz, seq, _ = proj3.shape
    c = R_CHUNK
    decay, qdec, kdec, cdec = _retention_consts()
    grid_spec = pltpu.PrefetchScalarGridSpec(
        num_scalar_prefetch=1,
        grid=(seq // c,),
        in_specs=[pl.BlockSpec((bsz, c, R_QK_WIDTH), lambda i, s: (0, i, COL_QR // R_QK_WIDTH)),
                  pl.BlockSpec((bsz, c, R_QK_WIDTH), lambda i, s: (0, i, COL_KR // R_QK_WIDTH)),
                  pl.BlockSpec((bsz, c, R_V_WIDTH), lambda i, s: (0, i, COL_VR // R_V_WIDTH)),
                  pl.BlockSpec((bsz, c, R_V_WIDTH), lambda i, s: (0, i, COL_GR // R_V_WIDTH)),
                  pl.BlockSpec((R_HEADS, c, c), lambda i, s: (0, 0, 0)),
                  pl.BlockSpec((R_HEADS, c, 1), lambda i, s: (0, 0, 0)),
                  pl.BlockSpec((R_HEADS, 1, c), lambda i, s: (0, 0, 0))],
        out_specs=pl.BlockSpec((bsz, c, R_V_WIDTH), lambda i, s: (0, i, 0)),
        scratch_shapes=[pltpu.VMEM((bsz, R_QK_WIDTH, R_V_DIM), F32)],
    )
    return pl.pallas_call(
        _retention_kernel,
        grid_spec=grid_spec,
        out_shape=jax.ShapeDtypeStruct((bsz, seq, R_V_WIDTH), BF16),
        compiler_params=_cparams(("arbitrary",)),
        name="retention",
    )(cdec, proj3, proj3, proj3, proj3, decay, qdec, kdec)


def _pack_halves(x):
    w = x.shape[1] // 2
    bits = lax.bitcast_convert_type(x.astype(BF16).astype(F32), jnp.uint32)
    return (bits[:, :w] >> 16) | (bits[:, w:] & jnp.uint32(0xFFFF0000))


def _unpack_halves(p):
    lo = lax.bitcast_convert_type(p << 16, F32)
    hi = lax.bitcast_convert_type(p & jnp.uint32(0xFFFF0000), F32)
    return lo, hi


SUBLANES = 8


ROUTER_ROWS = 512


def _mix_kernel(ya_ref, yr_ref, ga_ref, gt_ref, x_ref, wpa_ref, wpr_ref, wout_ref,
                gt1_ref, g_ref, sc_ref, sh_ref, wr_ref, rb_ref,
                x1_ref, h2_ref, h2p_ref, e_ref, w_ref, r_ref, c_ref):
    a = _dot(ya_ref[...], wpa_ref[...]) * _sigmoid(ga_ref[...].astype(F32))
    r = _dot(yr_ref[...], wpr_ref[...]) * _sigmoid(gt_ref[...].astype(F32))
    mix = _dot((a + r).astype(BF16), wout_ref[...])
    x1 = x_ref[...] + gt1_ref[0] * mix
    x1_ref[...] = x1
    ms = jnp.mean(x1 * x1, axis=-1, keepdims=True)
    y = x1 * lax.rsqrt(ms + NORM_EPS) * g_ref[...]
    h2 = y * (1.0 + sc_ref[0]) + sh_ref[0]
    h2_b = h2.astype(BF16)
    h2_ref[...] = h2_b
    h2p_ref[...] = _pack_halves(h2)
    _route(h2_b, wr_ref, rb_ref, e_ref, w_ref, r_ref, c_ref)


def _mix(ya, yr, proj, x2d, wpa, wpr, wout, gt1, g, sc, sh, wr_t, bias_col, seq):
    t, d = x2d.shape
    tm = min(ROUTER_ROWS, seq)
    per_b = seq // tm
    row = lambda i: (i, 0)
    full = lambda i: (0, 0)
    by_tile = lambda i: (0, i)
    per_batch = lambda i: (i // per_b, 0, 0)
    return pl.pallas_call(
        _mix_kernel,
        grid=(t // tm,),
        in_specs=[pl.BlockSpec((tm, A_WIDTH), row),
                  pl.BlockSpec((tm, R_V_WIDTH), row),
                  pl.BlockSpec((tm, d), lambda i: (i, COL_GA // D_MODEL)),
                  pl.BlockSpec((tm, d), lambda i: (i, COL_GT // D_MODEL)),
                  pl.BlockSpec((tm, d), row),
                  pl.BlockSpec((A_WIDTH, d), full),
                  pl.BlockSpec((R_V_WIDTH, d), full),
                  pl.BlockSpec((d, d), full),
                  pl.BlockSpec((1, 1, d), per_batch),
                  pl.BlockSpec((1, d), full),
                  pl.BlockSpec((1, 1, d), per_batch),
                  pl.BlockSpec((1, 1, d), per_batch),
                  pl.BlockSpec((N_EXPERTS, d), full),
                  pl.BlockSpec((N_EXPERTS, 1), full)],
        out_specs=[pl.BlockSpec((tm, d), row), pl.BlockSpec((tm, d), row),
                   pl.BlockSpec((tm, d // 2), row),
                   pl.BlockSpec((TOP_K, tm), by_tile), pl.BlockSpec((TOP_K, tm), by_tile),
                   pl.BlockSpec((TOP_K, tm), by_tile), pl.BlockSpec((N_EXPERTS, LANES), by_tile)],
        out_shape=[jax.ShapeDtypeStruct((t, d), F32), jax.ShapeDtypeStruct((t, d), BF16),
                   jax.ShapeDtypeStruct((t, d // 2), jnp.uint32),
                   jax.ShapeDtypeStruct((TOP_K, t), jnp.int32),
                   jax.ShapeDtypeStruct((TOP_K, t), F32),
                   jax.ShapeDtypeStruct((TOP_K, t), jnp.int32),
                   jax.ShapeDtypeStruct((N_EXPERTS, (t // tm) * LANES), F32)],
        compiler_params=_cparams(("parallel",)),
        name="merge_norm_route",
    )(ya, yr, proj, proj, x2d, wpa, wpr, wout, gt1, g, sc, sh, wr_t, bias_col)


def _route(h, wr_ref, b_ref, e_ref, w_ref, r_ref, c_ref):
    logits = _dot_nt(wr_ref[...], h)
    scores = _sigmoid(logits)
    choice = scores + b_ref[...]
    tm = logits.shape[1]
    giota = lax.broadcasted_iota(jnp.int32, (GROUP_SIZE, tm), 0)
    gs_rows = []
    for g in range(N_GROUPS):
        cg = choice[g * GROUP_SIZE:(g + 1) * GROUP_SIZE, :]
        m1 = jnp.max(cg, axis=0, keepdims=True)
        i1 = jnp.min(jnp.where(cg == m1, giota, GROUP_SIZE), axis=0, keepdims=True)
        m2 = jnp.max(jnp.where(giota == i1, KNOCKED_OUT, cg), axis=0, keepdims=True)
        gs_rows.append(m1 + m2)
    gs = jnp.concatenate(gs_rows, axis=0)
    grow = lax.broadcasted_iota(jnp.int32, (N_GROUPS, tm), 0)
    gmask = jnp.zeros((N_GROUPS, tm), jnp.bool_)
    for _ in range(TOPK_GROUPS):
        mx = jnp.max(gs, axis=0, keepdims=True)
        ix = jnp.min(jnp.where(gs == mx, grow, N_GROUPS), axis=0, keepdims=True)
        hit = grow == ix
        gmask = jnp.logical_or(gmask, hit)
        gs = jnp.where(hit, KNOCKED_OUT, gs)
    gmask_f = jnp.where(gmask, 1.0, 0.0)
    masked = jnp.concatenate(
        [jnp.where(gmask_f[g:g + 1, :] > 0.5, choice[g * GROUP_SIZE:(g + 1) * GROUP_SIZE, :], NEG_INF)
         for g in range(N_GROUPS)], axis=0)
    erow = lax.broadcasted_iota(jnp.int32, (N_EXPERTS, tm), 0)
    idx_rows, w_rows = [], []
    chosen = jnp.zeros((N_EXPERTS, tm), F32)
    for _ in range(TOP_K):
        mx = jnp.max(masked, axis=0, keepdims=True)
        ix = jnp.min(jnp.where(masked == mx, erow, N_EXPERTS), axis=0, keepdims=True)
        hit = erow == ix
        w_rows.append(jnp.sum(jnp.where(hit, scores, 0.0), axis=0, keepdims=True))
        idx_rows.append(ix)
        chosen = jnp.where(hit, 1.0, chosen)
        masked = jnp.where(hit, KNOCKED_OUT, masked)
    w = jnp.concatenate(w_rows, axis=0)
    w = w / (jnp.sum(w, axis=0, keepdims=True) + 1e-20) * ROUTED_SCALE
    e_ref[...] = jnp.concatenate(idx_rows, axis=0)
    w_ref[...] = w
    chosen_b = chosen.astype(BF16)
    earlier = (lax.broadcasted_iota(jnp.int32, (tm, tm), 0)
               < lax.broadcasted_iota(jnp.int32, (tm, tm), 1)).astype(BF16)
    before = _dot(chosen_b, earlier)
    ranks = [jnp.sum(jnp.where(erow == ix, before, 0.0), axis=0, keepdims=True) for ix in idx_rows]
    r_ref[...] = jnp.concatenate(ranks, axis=0).astype(jnp.int32)
    c_ref[...] = _dot(chosen_b, jnp.ones((tm, LANES), BF16))


def _pos_kernel(e_ref, r_ref, base_ref, p_ref):
    tm = e_ref.shape[1]
    erow = lax.broadcasted_iota(jnp.int32, (N_EXPERTS, tm), 0)
    base = base_ref[0]
    rows = [jnp.sum(jnp.where(erow == e_ref[k:k + 1, :], base, 0.0), axis=0, keepdims=True)
            for k in range(TOP_K)]
    p_ref[0] = jnp.concatenate(rows, axis=0).astype(jnp.int32) + r_ref[...]


MOVE_ROWS = 512


def _positions(eidx_t, rank_t, tile_base):
    t = eidx_t.shape[1]
    tm = min(MOVE_ROWS, t)
    per_router_tile = ROUTER_ROWS // tm
    return pl.pallas_call(
        _pos_kernel,
        grid=(t // tm,),
        in_specs=[pl.BlockSpec((TOP_K, tm), lambda i: (0, i)),
                  pl.BlockSpec((TOP_K, tm), lambda i: (0, i)),
                  pl.BlockSpec((1, N_EXPERTS, 1), lambda i: (i // per_router_tile, 0, 0))],
        out_specs=pl.BlockSpec((1, TOP_K, tm), lambda i: (i, 0, 0)),
        out_shape=jax.ShapeDtypeStruct((t // tm, TOP_K, tm), jnp.int32),
        compiler_params=_cparams(("parallel",)),
        name="slot_positions",
    )(eidx_t, rank_t, tile_base)


SLOT_ROWS = 512
PAD_CHUNKS = (256, 128, 64, 32, 16, 8)


def _zero_pads_kernel(pad_start_ref, pad_len_ref, xs_in, xs_hbm, zero_buf, pad_sem):
    del xs_in
    zero_buf[...] = jnp.zeros_like(zero_buf)

    def pad_copies(e, wait):
        start = pad_start_ref[e]
        n = pad_len_ref[e]
        head = jnp.minimum((-start) & (SUBLANES - 1), n)

        def fill(first, size, pred):
            @pl.when(pred)
            def _():
                cp = pltpu.make_async_copy(zero_buf.at[pl.ds(0, size), :], xs_hbm.at[pl.ds(first, size), :], pad_sem)
                if wait:
                    cp.wait()
                else:
                    cp.start()

        for j in range(SUBLANES - 1):
            fill(start + j, 1, j < head)
        ptr = start + head
        rest = n - head
        for chunk in PAD_CHUNKS:
            fill(pl.multiple_of(ptr, SUBLANES), chunk, (rest & chunk) != 0)
            ptr = ptr + (rest & chunk)

    def issue(e, carry):
        pad_copies(e, False)
        return carry

    def drain(e, carry):
        pad_copies(e, True)
        return carry

    lax.fori_loop(0, N_EXPERTS, issue, 0)
    lax.fori_loop(0, N_EXPERTS, drain, 0)


def _zero_pads(pad_start, pad_len, xs):
    grid_spec = pltpu.PrefetchScalarGridSpec(
        num_scalar_prefetch=2,
        grid=(1,),
        in_specs=[pl.BlockSpec(memory_space=pl.ANY)],
        out_specs=pl.BlockSpec(memory_space=pl.ANY),
        scratch_shapes=[pltpu.VMEM((PAD_CHUNKS[0], xs.shape[1]), xs.dtype),
                        pltpu.SemaphoreType.DMA],
    )
    return pl.pallas_call(
        _zero_pads_kernel,
        grid_spec=grid_spec,
        out_shape=jax.ShapeDtypeStruct(xs.shape, xs.dtype),
        input_output_aliases={2: 0},
        compiler_params=_cparams(("arbitrary",)),
        name="zero_pad_slots",
    )(pad_start, pad_len, xs)


SC_DISPATCH_TOKENS = 64


def _sc_dispatch(pos_blocks, h_rows, n_rows):
    info = plsc.get_sparse_core_info()
    n_cores = info.num_cores
    n_workers = n_cores * info.num_subcores
    t = h_rows.shape[0]
    chunk = SC_DISPATCH_TOKENS
    steps = t // (n_workers * chunk)
    assert steps * n_workers * chunk == t
    mesh = plsc.VectorSubcoreMesh(core_axis_name="c", subcore_axis_name="s")

    @functools.partial(
        pl.kernel, mesh=mesh,
        out_type=jax.ShapeDtypeStruct((n_rows,) + h_rows.shape[1:], h_rows.dtype),
        scratch_types=[pltpu.VMEM((TOP_K, chunk), jnp.int32),
                       pltpu.VMEM((chunk,) + h_rows.shape[1:], h_rows.dtype),
                       pltpu.SemaphoreType.DMA],
        name="sc_dispatch_rows",
    )
    def scatter_rows(pos_hbm, h_hbm, out_hbm, idx_v, rows_v, sem):
        wid = lax.axis_index("s") * n_cores + lax.axis_index("c")

        @pl.loop(0, steps)
        def _(step):
            blk = wid * steps + step
            pltpu.sync_copy(pos_hbm.at[blk], idx_v)
            pltpu.sync_copy(h_hbm.at[pl.ds(blk * chunk, chunk)], rows_v)
            scatters = [pltpu.make_async_copy(rows_v, out_hbm.at[idx_v.at[k]], sem) for k in range(TOP_K)]
            for cp in scatters:
                cp.start()
            for cp in scatters:
                cp.wait()

    return scatter_rows(pos_blocks, h_rows)


WEIGHT_SLOTS = 4


def _experts_kernel(blk_e_ref, nblk_ref, ord_ref, eid_ref, nord_ref, x_ref, w1_hbm, w3_hbm, w2_hbm, y_ref,
                    w1f, w3f, w2f, wsem):
    s = pl.program_id(0)

    def fetch(j):
        slot = j % WEIGHT_SLOTS
        e = eid_ref[j]
        return [pltpu.make_async_copy(src.at[e], dst.at[slot], wsem.at[slot])
                for src, dst in ((w1_hbm, w1f), (w3_hbm, w3f), (w2_hbm, w2f))]

    @pl.when(s < nblk_ref[0])
    def _():
        j = ord_ref[s]

        ahead = WEIGHT_SLOTS - 1

        @pl.when(s == 0)
        def _():
            for first in range(ahead):
                @pl.when(first < nord_ref[0])
                def _(first=first):
                    for cp in fetch(first):
                        cp.start()

        @pl.when(jnp.logical_or(s == 0, blk_e_ref[s] != blk_e_ref[jnp.maximum(s - 1, 0)]))
        def _():
            for cp in fetch(j):
                cp.wait()

            @pl.when(j + ahead < nord_ref[0])
            def _():
                for cp in fetch(j + ahead):
                    cp.start()

        slot = j % WEIGHT_SLOTS
        half = x_ref.shape[1]
        lo, hi = _unpack_halves(x_ref[...])
        lo = lo.astype(BF16)
        hi = hi.astype(BF16)
        h1 = (_dot(lo, w1f[slot, :half, :].astype(BF16)) + _dot(hi, w1f[slot, half:, :].astype(BF16)))
        h3 = (_dot(lo, w3f[slot, :half, :].astype(BF16)) + _dot(hi, w3f[slot, half:, :].astype(BF16)))
        mid = (_silu(h1) * h3).astype(BF16)
        y_ref[...] = _pack_halves(_dot(mid, w2f[slot].astype(BF16)))


def _sc_gather(pos_blocks, ys, t):
    info = plsc.get_sparse_core_info()
    n_cores = info.num_cores
    n_workers = n_cores * info.num_subcores
    chunk = SC_DISPATCH_TOKENS
    steps = t // (n_workers * chunk)
    assert steps * n_workers * chunk == t
    mesh = plsc.VectorSubcoreMesh(core_axis_name="c", subcore_axis_name="s")

    @functools.partial(
        pl.kernel, mesh=mesh,
        out_type=jax.ShapeDtypeStruct((TOP_K * t,) + ys.shape[1:], ys.dtype),
        scratch_types=[pltpu.VMEM((TOP_K, chunk), jnp.int32),
                       pltpu.VMEM((chunk,) + ys.shape[1:], ys.dtype),
                       pltpu.VMEM((chunk,) + ys.shape[1:], ys.dtype),
                       pltpu.SemaphoreType.DMA((2,))],
        name="sc_gather_rows",
    )
    def gather_rows(pos_hbm, ys_hbm, out_hbm, idx_v, rows_a, rows_b, sems):
        wid = lax.axis_index("s") * n_cores + lax.axis_index("c")
        bufs = (rows_a, rows_b)

        @pl.loop(0, steps)
        def _(step):
            blk = wid * steps + step
            pltpu.sync_copy(pos_hbm.at[blk], idx_v)
            gathers = [pltpu.make_async_copy(ys_hbm.at[idx_v.at[k]], bufs[k % 2], sems.at[k % 2])
                       for k in range(TOP_K)]
            gathers[0].start()
            for k in range(TOP_K):
                gathers[k].wait()
                if k + 1 < TOP_K:
                    gathers[k + 1].start()
                pltpu.sync_copy(bufs[k % 2], out_hbm.at[pl.ds(k * t + blk * chunk, chunk)])

    return gather_rows(pos_blocks, ys)


def _experts(blk_e, nblk_used, blk_ord, eid_of_ord, n_ord, xs, w1, w3, w2):
    n_rows, half = xs.shape
    d = D_MODEL
    blk = lambda s, be, nb, bo, eo, no: (jnp.minimum(s, nb[0] - 1), 0)
    grid_spec = pltpu.PrefetchScalarGridSpec(
        num_scalar_prefetch=5,
        grid=(n_rows // SLOT_ROWS,),
        in_specs=[pl.BlockSpec((SLOT_ROWS, half), blk),
                  pl.BlockSpec(memory_space=pl.ANY),
                  pl.BlockSpec(memory_space=pl.ANY),
                  pl.BlockSpec(memory_space=pl.ANY)],
        out_specs=pl.BlockSpec((SLOT_ROWS, half), blk),
        scratch_shapes=[pltpu.VMEM((WEIGHT_SLOTS, d, EXPERT_FF), F32),
                        pltpu.VMEM((WEIGHT_SLOTS, d, EXPERT_FF), F32),
                        pltpu.VMEM((WEIGHT_SLOTS, EXPERT_FF, d), F32),
                        pltpu.SemaphoreType.DMA((WEIGHT_SLOTS,))],
    )
    return pl.pallas_call(
        _experts_kernel,
        grid_spec=grid_spec,
        out_shape=jax.ShapeDtypeStruct((n_rows, half), jnp.uint32),
        compiler_params=_cparams(("arbitrary",)),
        name="routed_experts",
    )(blk_e, nblk_used, blk_ord, eid_of_ord, n_ord, xs, w1, w3, w2)


def _combine_kernel(*refs):
    y_refs = refs[:TOP_K]
    w_ref, h_ref, x1_ref, ws1_ref, ws3_ref, ws2_ref, gt2_ref, g_ref = refs[TOP_K:TOP_K + 8]
    o_ref = refs[-1]
    tm, d = x1_ref.shape
    half = d // 2
    h = h_ref[...]
    mid = (_silu(_dot(h, ws1_ref[...])) * _dot(h, ws3_ref[...])).astype(BF16)
    shared = _dot(mid, ws2_ref[...])
    w = w_ref[...]
    acc_lo = jnp.zeros((tm, half), F32)
    acc_hi = jnp.zeros((tm, half), F32)
    for k in range(TOP_K):
        lo, hi = _unpack_halves(y_refs[k][...])
        acc_lo = acc_lo + lo * w[:, k:k + 1]
        acc_hi = acc_hi + hi * w[:, k:k + 1]
    routed = jnp.concatenate([acc_lo, acc_hi], axis=1)
    x2 = x1_ref[...] + gt2_ref[0] * (routed + shared)
    ms = jnp.mean(x2 * x2, axis=-1, keepdims=True)
    o_ref[...] = x2 * lax.rsqrt(ms + NORM_EPS) * g_ref[...]


COMBINE_ROWS = 256
COMBINE_PARTS = 4


def _combine(y_kt, wts, h2, x1, ws1, ws3, ws2, gt2, g_final, seq, part, n_parts, prev_out):
    t, d = x1.shape
    tm = min(COMBINE_ROWS, seq)
    per_b = seq // tm
    tiles = t // tm // n_parts
    first = part * tiles
    row = lambda i: (first + i, 0)
    full = lambda i: (0, 0)
    y_specs = [pl.BlockSpec((tm, d // 2), functools.partial(lambda i, k: (k * tiles + i, 0), k=k))
               for k in range(TOP_K)]
    in_specs = y_specs + [
        pl.BlockSpec((tm, TOP_K), row),
        pl.BlockSpec((tm, d), row),
        pl.BlockSpec((tm, d), row),
        pl.BlockSpec((d, SHARED_FF), full),
        pl.BlockSpec((d, SHARED_FF), full),
        pl.BlockSpec((SHARED_FF, d), full),
        pl.BlockSpec((1, 1, d), lambda i: ((first + i) // per_b, 0, 0)),
        pl.BlockSpec((1, d), full)]
    args = [y_kt] * TOP_K + [wts, h2, x1, ws1, ws3, ws2, gt2, g_final]
    aliases = {}
    if prev_out is not None:
        in_specs.append(pl.BlockSpec(memory_space=pl.ANY))
        aliases = {len(args): 0}
        args.append(prev_out)
    return pl.pallas_call(
        _combine_kernel,
        grid=(tiles,),
        in_specs=in_specs,
        out_specs=pl.BlockSpec((tm, d), row),
        out_shape=jax.ShapeDtypeStruct((t, d), F32),
        input_output_aliases=aliases,
        compiler_params=_cparams(("parallel",)),
        name="combine_shared_final",
    )(*args)


def _slot_tables(cnt, t):
    ntiles = cnt.shape[1] // LANES
    cnt_tile = cnt.reshape(N_EXPERTS, ntiles, LANES)[:, :, 0].astype(jnp.int32)
    counts = jnp.sum(cnt_tile, axis=1)
    padded = (counts + SLOT_ROWS - 1) // SLOT_ROWS * SLOT_ROWS
    pstart = jnp.cumsum(padded) - padded
    tile_base = pstart[:, None] + jnp.cumsum(cnt_tile, axis=1) - cnt_tile
    n_blk = -(-(t * TOP_K) // SLOT_ROWS) + N_EXPERTS
    blk_end = jnp.cumsum(padded // SLOT_ROWS)
    blk_e = jnp.sum((blk_end[None, :] <= jnp.arange(n_blk)[:, None]).astype(jnp.int32), axis=1)
    blk_e = jnp.minimum(blk_e, N_EXPERTS - 1)
    owns = (padded > 0).astype(jnp.int32)
    ord_of_e = jnp.cumsum(owns) - owns
    ids = jnp.arange(N_EXPERTS, dtype=jnp.int32)
    eid_of_ord = jnp.sum(jnp.where((ord_of_e[None, :] == ids[:, None]) & (owns[None, :] > 0), ids[None, :], 0), axis=1)
    blk_ord = jnp.sum(jnp.where(blk_e[:, None] == ids[None, :], ord_of_e[None, :], 0), axis=1)
    experts_tables = (blk_e, blk_end[-1:].astype(jnp.int32), blk_ord.astype(jnp.int32),
                      eid_of_ord.astype(jnp.int32), jnp.sum(owns).reshape(1).astype(jnp.int32))
    return (experts_tables, pstart + counts, padded - counts,
            tile_base.T.astype(F32).reshape(ntiles, N_EXPERTS, 1), n_blk * SLOT_ROWS)


def _permute_in_cols(w_in):
    qa, ka, va, qr, kr, vr, gr, ga, gt = jnp.split(
        w_in, np.cumsum((A_WIDTH, A_WIDTH, A_WIDTH, R_QK_WIDTH, R_QK_WIDTH, R_V_WIDTH, R_V_WIDTH,
                         D_MODEL))[:].tolist(), axis=1)
    return jnp.concatenate([vr, gr, ga, gt, qa, ka, va, qr, kr], axis=1)


def kernel(x, c, w_ada, b_ada, g_mix, w_in, w_pa, w_pr, w_out, g_ffn, w_router, router_bias,
           w1, w3, w2, ws1, ws3, ws2, g_final):
    bsz, seq, d = x.shape
    t = bsz * seq
    depth = w_ada.shape[0]
    assert depth == 1, "the final norm is fused into the single layer's last kernel"
    rest = jnp.exp2(-8.0 / A_HEADS * jnp.arange(1, A_HEADS + 1, dtype=F32)) * LOG2_E
    pieces = []
    for _ in range(ALIBI_PIECES):
        pieces.append(rest.astype(BF16).astype(F32))
        rest = rest - pieces[-1]
    slopes = jnp.stack(pieces, axis=1).reshape(-1)
    x2d = x.reshape(t, d)
    for l in range(depth):
        mod = _ada(c, w_ada[l], b_ada[l])
        sh1, sc1, gt1, sh2, sc2, gt2 = [m.reshape(bsz, 1, d) for m in jnp.split(mod, 6, axis=-1)]
        w_in_p = _permute_in_cols(w_in[l]).astype(BF16)
        proj = _inproj(x2d, g_mix[l].reshape(1, d), sc1, sh1, w_in_p, seq)
        proj3 = proj.reshape(bsz, seq, IN_COLS)
        ya = _moba(proj3, slopes).reshape(t, A_WIDTH)
        yr = _retention(proj3).reshape(t, R_V_WIDTH)
        x1, h2, h2p, eidx_t, wts_t, rank_t, cnt = _mix(
            ya, yr, proj, x2d, w_pa[l].astype(BF16), w_pr[l].astype(BF16), w_out[l].astype(BF16), gt1,
            g_ffn[l].reshape(1, d), sc2, sh2, w_router[l].T.astype(BF16), router_bias[l].reshape(N_EXPERTS, 1), seq)
        experts_tables, pad_start, pad_len, tile_base, n_rows = _slot_tables(cnt, t)
        pos3 = _positions(eidx_t, rank_t, tile_base)
        pos_blocks = jnp.transpose(
            pos3.reshape(pos3.shape[0], TOP_K, -1, SC_DISPATCH_TOKENS), (0, 2, 1, 3)
        ).reshape(t // SC_DISPATCH_TOKENS, TOP_K, SC_DISPATCH_TOKENS)
        xs = _zero_pads(pad_start, pad_len, _sc_dispatch(pos_blocks, h2p, n_rows))
        ys = _experts(*experts_tables, xs, w1[l], w3[l], w2[l])
        wts = wts_t.T
        shared_w = (ws1[l].astype(BF16), ws3[l].astype(BF16), ws2[l].astype(BF16))
        blocks_per_part = pos_blocks.shape[0] // COMBINE_PARTS
        x2d = None
        for part in range(COMBINE_PARTS):
            y_kt = _sc_gather(pos_blocks[part * blocks_per_part:(part + 1) * blocks_per_part], ys,
                              t // COMBINE_PARTS)
            x2d = _combine(y_kt, wts, h2, x1, *shared_w, gt2, g_final.reshape(1, d), seq,
                           part, COMBINE_PARTS, x2d)
    return x2d.reshape(bsz, seq, d)
```

```python
import functools

import jax
import jax.numpy as jnp
import numpy as np
from jax import lax
from jax.experimental import pallas as pl
from jax.experimental.pallas import tpu as pltpu
from jax.experimental.pallas import tpu_sc as plsc

F32 = jnp.float32
BF16 = jnp.bfloat16

D_MODEL = 1024
A_HEADS = 8
A_HEAD_DIM = 64
A_WIDTH = A_HEADS * A_HEAD_DIM
MOBA_BLOCK = 256
MOBA_TOPK = 3
R_HEADS = 8
R_QK_DIM = 64
R_V_DIM = 128
R_QK_WIDTH = R_HEADS * R_QK_DIM
R_V_WIDTH = R_HEADS * R_V_DIM
R_CHUNK = 256
N_EXPERTS = 256
TOP_K = 8
N_GROUPS = 8
GROUP_SIZE = N_EXPERTS // N_GROUPS
TOPK_GROUPS = 4
EXPERT_FF = 256
SHARED_FF = 256
ROUTED_SCALE = 2.5
NORM_EPS = 1e-6
GN_EPS = 1e-6
NEG_INF = -1e30
KNOCKED_OUT = -3e38

COL_VR, COL_GR, COL_GA, COL_GT = 0, 1024, 2048, 3072
COL_QA, COL_KA, COL_VA, COL_QR, COL_KR = 4096, 4608, 5120, 5632, 6144
IN_COLS = 6656
AUG = 128
FEAT_BIAS = A_HEAD_DIM
FEAT_POS = A_HEAD_DIM + 32
ALIBI_PIECES = 3
LOG2_E = 1.4426950408889634
LANES = 128
BF16_ROWS = 16
V_ROWS = A_HEAD_DIM + BF16_ROWS
MOBA_HEADS_PER_STEP = 8

VMEM_LIMIT = 56 * 1024 * 1024


def _cparams(sem, vmem=VMEM_LIMIT):
    return pltpu.CompilerParams(dimension_semantics=sem, vmem_limit_bytes=vmem)


def _dot(a, b):
    return jnp.dot(a, b, preferred_element_type=F32)


def _dot_nt(a, b):
    return lax.dot_general(a, b, (((1,), (1,)), ((), ())), preferred_element_type=F32)


def _sigmoid(x):
    return 1.0 / (1.0 + jnp.exp(-x))


def _silu(x):
    return x * _sigmoid(x)


def _ada_kernel(c_ref, w_ref, b_ref, o_ref):
    c = c_ref[...]
    s = _silu(c)
    s_hi = s.astype(BF16)
    s_lo = (s - s_hi.astype(F32)).astype(BF16)
    w = w_ref[...]
    w_hi = w.astype(BF16)
    w_lo = (w - w_hi.astype(F32)).astype(BF16)
    o_ref[...] = _dot(s_hi, w_hi) + _dot(s_hi, w_lo) + _dot(s_lo, w_hi) + b_ref[...]


def _ada(c, w_ada, b_ada):
    bsz, d = c.shape
    n = w_ada.shape[1]
    tn = 1024
    return pl.pallas_call(
        _ada_kernel,
        grid=(n // tn,),
        in_specs=[pl.BlockSpec((bsz, d), lambda j: (0, 0)),
                  pl.BlockSpec((d, tn), lambda j: (0, j)),
                  pl.BlockSpec((1, tn), lambda j: (0, j))],
        out_specs=pl.BlockSpec((bsz, tn), lambda j: (0, j)),
        out_shape=jax.ShapeDtypeStruct((bsz, n), F32),
        compiler_params=_cparams(("parallel",)),
        name="ada_mod",
    )(c, w_ada, b_ada.reshape(1, n))


INPROJ_COLS = 512


def _inproj_kernel(x_ref, g_ref, sc_ref, sh_ref, w_ref, o_ref):
    x = x_ref[...]
    ms = jnp.mean(x * x, axis=-1, keepdims=True)
    y = x * lax.rsqrt(ms + NORM_EPS) * g_ref[...]
    h = (y * (1.0 + sc_ref[0]) + sh_ref[0]).astype(BF16)
    for j in range(w_ref.shape[1] // INPROJ_COLS):
        cols = slice(j * INPROJ_COLS, (j + 1) * INPROJ_COLS)
        o_ref[:, cols] = _dot(h, w_ref[:, cols]).astype(BF16)


def _inproj(x2d, g, sc, sh, w_bf16, seq):
    t, d = x2d.shape
    n = w_bf16.shape[1]
    tm = min(512, seq)
    per_b = seq // tm
    return pl.pallas_call(
        _inproj_kernel,
        grid=(t // tm,),
        in_specs=[pl.BlockSpec((tm, d), lambda i: (i, 0)),
                  pl.BlockSpec((1, d), lambda i: (0, 0)),
                  pl.BlockSpec((1, 1, d), lambda i: (i // per_b, 0, 0)),
                  pl.BlockSpec((1, 1, d), lambda i: (i // per_b, 0, 0)),
                  pl.BlockSpec((d, n), lambda i: (0, 0))],
        out_specs=pl.BlockSpec((tm, n), lambda i: (i, 0)),
        out_shape=jax.ShapeDtypeStruct((t, n), BF16),
        compiler_params=_cparams(("parallel",)),
        name="norm_inproj",
    )(x2d, g, sc, sh, w_bf16)


def _moba_prepare(i, slopes_ref, q_ref, k_ref, v_ref, ko_ref, qo_ref, vo_ref, kmean_scr):
    nblk = kmean_scr.shape[0]
    width = q_ref.shape[1]
    seq_rows = pl.ds(pl.multiple_of(i * MOBA_BLOCK, MOBA_BLOCK), MOBA_BLOCK)
    q = q_ref[...]
    k = k_ref[...]
    v = v_ref[...]
    kmean_scr[pl.ds(i, 1), :] = jnp.mean(k.astype(F32), axis=0, keepdims=True)

    eye = (lax.broadcasted_iota(jnp.int32, (A_HEAD_DIM, A_HEAD_DIM), 0)
           == lax.broadcasted_iota(jnp.int32, (A_HEAD_DIM, A_HEAD_DIM), 1)).astype(BF16)
    heads = [slice(h * A_HEAD_DIM, (h + 1) * A_HEAD_DIM) for h in range(width // A_HEAD_DIM)]
    q_t = jnp.concatenate([_dot_nt(eye, q[:, hd]) for hd in heads], axis=0)
    v_t = jnp.concatenate([_dot_nt(eye, v[:, hd]) for hd in heads], axis=0)

    km = kmean_scr[...]
    km_rep = jnp.concatenate([km] * A_HEADS, axis=0)
    r_head = lax.broadcasted_iota(jnp.int32, km_rep.shape, 0) // nblk
    c_head = lax.broadcasted_iota(jnp.int32, km_rep.shape, 1) // A_HEAD_DIM
    km_bd = jnp.where(r_head == c_head, km_rep, 0.0)
    km_hi = km_bd.astype(BF16)
    km_lo = (km_bd - km_hi.astype(F32)).astype(BF16)
    q_t_b = q_t.astype(BF16)
    gate_all = _dot(km_hi, q_t_b) + _dot(km_lo, q_t_b)

    mb = q.shape[0]
    blk = lax.broadcasted_iota(jnp.int32, (nblk, mb), 0)
    lane_pos = lax.broadcasted_iota(jnp.int32, (BF16_ROWS, mb), 1).astype(F32)
    row16 = lax.broadcasted_iota(jnp.int32, (BF16_ROWS, mb), 0)
    key_pos = lax.broadcasted_iota(jnp.int32, (mb, AUG), 0).astype(F32)
    kcol = lax.broadcasted_iota(jnp.int32, (mb, AUG), 1)
    widen = (lax.broadcasted_iota(jnp.int32, (A_HEAD_DIM, AUG), 0)
             == lax.broadcasted_iota(jnp.int32, (A_HEAD_DIM, AUG), 1)).astype(BF16)

    blk_first = (i * mb).astype(F32)
    for h in range(A_HEADS):
        pieces = [slopes_ref[h * ALIBI_PIECES + c] for c in range(ALIBI_PIECES)]
        g = jnp.where(blk < i, gate_all[h * nblk:(h + 1) * nblk, :], NEG_INF)
        sel = jnp.zeros((nblk, mb), jnp.bool_)
        for r in range(MOBA_TOPK):
            m = jnp.max(g, axis=0, keepdims=True)
            idx = jnp.min(jnp.where(g == m, blk, nblk), axis=0, keepdims=True)
            hit = blk == idx
            sel = jnp.logical_or(sel, jnp.logical_and(hit, r < i))
            g = jnp.where(hit, KNOCKED_OUT, g)
        bias_t = jnp.where(sel, 0.0, NEG_INF)

        scale = A_HEAD_DIM ** -0.5 * LOG2_E
        qo_ref[0, h, 0:A_HEAD_DIM, :] = (q_t[h * A_HEAD_DIM:(h + 1) * A_HEAD_DIM, :] * scale).astype(BF16)
        qo_ref[0, h, FEAT_BIAS:FEAT_BIAS + nblk, :] = bias_t.astype(BF16)
        if nblk < 32:
            qo_ref[0, h, FEAT_BIAS + nblk:FEAT_POS, :] = jnp.zeros((32 - nblk, mb), BF16)
        piece_rows = jnp.where(row16 % ALIBI_PIECES == 0, pieces[0],
                               jnp.where(row16 % ALIBI_PIECES == 1, pieces[1], pieces[2]))
        pos_feat = jnp.where(row16 < 3, -lane_pos,
                             jnp.where(row16 < 6, piece_rows,
                                       jnp.where(row16 < 9, -blk_first, jnp.where(row16 < 12, piece_rows, 0.0))))
        qo_ref[0, h, FEAT_POS:FEAT_POS + BF16_ROWS, :] = pos_feat.astype(BF16)
        qo_ref[0, h, FEAT_POS + BF16_ROWS:AUG, :] = jnp.zeros((AUG - FEAT_POS - BF16_ROWS, mb), BF16)

        vo_ref[0, h, 0:A_HEAD_DIM, seq_rows] = v_t[h * A_HEAD_DIM:(h + 1) * A_HEAD_DIM, :].astype(BF16)
        vo_ref[0, h, A_HEAD_DIM:V_ROWS, seq_rows] = jnp.where(row16 == 0, 1.0, 0.0).astype(BF16)

        pos_col = kcol - FEAT_POS
        piece_cols = jnp.where(pos_col % ALIBI_PIECES == 0, pieces[0],
                               jnp.where(pos_col % ALIBI_PIECES == 1, pieces[1], pieces[2]))
        k_feat = jnp.where(
            kcol == FEAT_BIAS + i, 1.0,
            jnp.where(pos_col < 0, 0.0,
                      jnp.where(pos_col < 3, piece_cols,
                                jnp.where(pos_col < 6, key_pos,
                                          jnp.where(pos_col < 9, piece_cols,
                                                    jnp.where(pos_col < 12, blk_first, 0.0))))))
        k_wide = _dot(k[:, h * A_HEAD_DIM:(h + 1) * A_HEAD_DIM], widen)
        ko_ref[0, h, seq_rows, :] = (k_wide + k_feat).astype(BF16)


def _moba_attend(i, q_ref, k_ref, v_ref, o_ref, s_a, s_b, s_c, group, n_groups):
    mb = MOBA_BLOCK
    span = group * mb
    own = pl.multiple_of(i * mb, mb)
    key_i = lax.broadcasted_iota(jnp.int32, (mb, mb), 0)
    qry_i = lax.broadcasted_iota(jnp.int32, (mb, mb), 1)
    feat = lax.broadcasted_iota(jnp.int32, (AUG, mb), 0)
    is_bias = jnp.logical_and(feat >= FEAT_BIAS, feat < FEAT_POS)
    q_ts, carry0 = [], []
    for hh in range(MOBA_HEADS_PER_STEP):
        q_t = q_ref[0, hh]
        q_ts.append(q_t)
        q_own = jnp.where(is_bias, jnp.zeros_like(q_t), q_t)
        s = _dot(k_ref[0, hh, pl.ds(own, mb), :], q_own)
        s = jnp.where(key_i <= qry_i, s, NEG_INF)
        m0 = jnp.max(s, axis=0, keepdims=True)
        p = jnp.exp2(s - m0)
        carry0 += [m0, _dot(v_ref[0, hh, :, pl.ds(own, mb)], p.astype(BF16))]

    def scores(g, dst):
        start = pl.multiple_of(jnp.minimum(g, n_groups - 1) * span, span)
        for hh in range(MOBA_HEADS_PER_STEP):
            dst[hh] = _dot(k_ref[0, hh, pl.ds(start, span), :], q_ts[hh])

    def absorb(g, src, carry):
        start = pl.multiple_of(g * span, span)
        new = []
        for hh in range(MOBA_HEADS_PER_STEP):
            m, acc = carry[2 * hh], carry[2 * hh + 1]
            sb = src[hh]
            m_new = jnp.maximum(m, jnp.max(sb, axis=0, keepdims=True))
            pb = jnp.exp2(sb - m_new)
            alpha = jnp.exp2(m - m_new)
            acc = acc * alpha + _dot(v_ref[0, hh, :, pl.ds(start, span)], pb.astype(BF16))
            new += [m_new, acc]
        return tuple(new)

    def body(trip, carry):
        g = 3 * trip
        scores(g + 1, s_b)
        carry = absorb(g, s_a, carry)
        scores(g + 2, s_c)
        carry = absorb(g + 1, s_b, carry)
        scores(g + 3, s_a)
        return absorb(g + 2, s_c, carry)

    scores(0, s_a)
    live_groups = (i + group - 1) // group
    trips = live_groups // 3
    res = lax.fori_loop(0, trips, body, tuple(carry0))
    left = live_groups - 3 * trips

    def one_more(c):
        return absorb(3 * trips, s_a, c)

    def two_more(c):
        scores(3 * trips + 1, s_b)
        return absorb(3 * trips + 1, s_b, absorb(3 * trips, s_a, c))

    res = lax.cond(left == 1, one_more, lambda c: c, res)
    res = lax.cond(left == 2, two_more, lambda c: c, res)
    outs = [res[2 * hh + 1][0:A_HEAD_DIM, :] / res[2 * hh + 1][A_HEAD_DIM:A_HEAD_DIM + 1, :] for hh in range(MOBA_HEADS_PER_STEP)]
    o_t = jnp.concatenate(outs, axis=0).astype(BF16)
    eye = (key_i == qry_i).astype(BF16)
    o_ref[0] = _dot_nt(eye, o_t).astype(BF16)


def _moba_kernel(slopes_ref, q_ref, k_ref, v_ref, o_ref, kmean_scr, q_scr, k_scr, v_scr, s_a, s_b, s_c,
                 *, group, n_groups):
    i = pl.program_id(1)

    @pl.when(i == 0)
    def _():
        kmean_scr[...] = jnp.zeros_like(kmean_scr)
        k_scr[...] = jnp.zeros_like(k_scr)
        v_scr[...] = jnp.zeros_like(v_scr)

    _moba_prepare(i, slopes_ref, q_ref, k_ref, v_ref, k_scr, q_scr, v_scr, kmean_scr)
    _moba_attend(i, q_scr, k_scr, v_scr, o_ref, s_a, s_b, s_c, group, n_groups)


def _moba(proj3, slopes):
    bsz, seq, _ = proj3.shape
    mb = MOBA_BLOCK
    nblk = seq // mb
    group = min(2, nblk)
    n_groups = nblk // group
    assert MOBA_HEADS_PER_STEP == A_HEADS
    assert seq % mb == 0 and nblk % group == 0 and nblk <= FEAT_POS - FEAT_BIAS, "one selection feature per key block"
    grid_spec = pltpu.PrefetchScalarGridSpec(
        num_scalar_prefetch=1,
        grid=(bsz, nblk),
        in_specs=[pl.BlockSpec((None, mb, A_WIDTH), lambda b, i, s: (b, i, COL_QA // A_WIDTH)),
                  pl.BlockSpec((None, mb, A_WIDTH), lambda b, i, s: (b, i, COL_KA // A_WIDTH)),
                  pl.BlockSpec((None, mb, A_WIDTH), lambda b, i, s: (b, i, COL_VA // A_WIDTH))],
        out_specs=pl.BlockSpec((1, mb, A_WIDTH), lambda b, i, s: (b, i, 0)),
        scratch_shapes=[pltpu.VMEM((nblk, A_WIDTH), F32),
                        pltpu.VMEM((1, A_HEADS, AUG, mb), BF16),
                        pltpu.VMEM((1, A_HEADS, seq, AUG), BF16),
                        pltpu.VMEM((1, A_HEADS, V_ROWS, seq), BF16),
                        pltpu.VMEM((A_HEADS, group * mb, mb), F32),
                        pltpu.VMEM((A_HEADS, group * mb, mb), F32),
                        pltpu.VMEM((A_HEADS, group * mb, mb), F32)],
    )
    return pl.pallas_call(
        functools.partial(_moba_kernel, group=group, n_groups=n_groups),
        grid_spec=grid_spec,
        out_shape=jax.ShapeDtypeStruct((bsz, seq, A_WIDTH), BF16),
        compiler_params=_cparams(("parallel", "arbitrary")),
        name="moba_attention",
    )(slopes, proj3, proj3, proj3)


def _retention_kernel(cdec_ref, q_ref, k_ref, v_ref, g_ref, decay_ref, qdec_ref, kdec_ref, o_ref, state_scr):
    @pl.when(pl.program_id(0) == 0)
    def _():
        state_scr[...] = jnp.zeros_like(state_scr)

    width = q_ref.shape[2]
    eye = (lax.broadcasted_iota(jnp.int32, (width, width), 0)
           == lax.broadcasted_iota(jnp.int32, (width, width), 1)).astype(BF16)
    for b in range(q_ref.shape[0]):
        q = q_ref[b]
        k_t = _dot_nt(eye, k_ref[b])
        k_t_b = k_t.astype(BF16)
        state_b = state_scr[b].astype(BF16)
        for h in range(R_HEADS):
            rows = slice(h * R_QK_DIM, (h + 1) * R_QK_DIM)
            cols = slice(h * R_V_DIM, (h + 1) * R_V_DIM)
            q_h = q[:, rows]
            v_h = v_ref[b, :, cols]
            inner = _dot(q_h, k_t_b[rows, :]) * decay_ref[h]
            out = _dot(inner.astype(BF16), v_h) + _dot(q_h, state_b[rows, :]) * qdec_ref[h]
            k_dec = (k_t[rows, :] * kdec_ref[h]).astype(BF16)
            state_scr[b, rows, :] = cdec_ref[h] * state_scr[b, rows, :] + _dot(k_dec, v_h)
            mu = jnp.mean(out, axis=-1, keepdims=True)
            cen = out - mu
            var = jnp.mean(cen * cen, axis=-1, keepdims=True)
            y = cen * lax.rsqrt(var + GN_EPS)
            o_ref[b, :, cols] = (y * _silu(g_ref[b, :, cols].astype(F32))).astype(BF16)


def _retention_consts():
    h = np.arange(R_HEADS, dtype=np.float64)
    log_g = np.log(1.0 - np.exp2(-5.0 - h))
    n = np.arange(R_CHUNK, dtype=np.float64)
    diff = n[:, None] - n[None, :]
    scale = R_QK_DIM ** -0.5
    decay = np.where(diff >= 0, np.exp(np.maximum(diff, 0.0) * log_g[:, None, None]), 0.0) * scale
    q_decay = np.exp((n + 1.0) * log_g[:, None])[:, :, None]
    k_decay = np.exp((R_CHUNK - 1.0 - n) * log_g[:, None])[:, None, :] * scale
    chunk_decay = np.exp(R_CHUNK * log_g)
    return (jnp.asarray(decay, F32), jnp.asarray(q_decay, F32), jnp.asarray(k_decay, F32),
            jnp.asarray(chunk_decay, F32))


def _retention(proj3):
    bsz, seq, _ = proj3.shape
    c = R_CHUNK
    decay, qdec, kdec, cdec = _retention_consts()
    grid_spec = pltpu.PrefetchScalarGridSpec(
        num_scalar_prefetch=1,
        grid=(seq // c,),
        in_specs=[pl.BlockSpec((bsz, c, R_QK_WIDTH), lambda i, s: (0, i, COL_QR // R_QK_WIDTH)),
                  pl.BlockSpec((bsz, c, R_QK_WIDTH), lambda i, s: (0, i, COL_KR // R_QK_WIDTH)),
                  pl.BlockSpec((bsz, c, R_V_WIDTH), lambda i, s: (0, i, COL_VR // R_V_WIDTH)),
                  pl.BlockSpec((bsz, c, R_V_WIDTH), lambda i, s: (0, i, COL_GR // R_V_WIDTH)),
                  pl.BlockSpec((R_HEADS, c, c), lambda i, s: (0, 0, 0)),
                  pl.BlockSpec((R_HEADS, c, 1), lambda i, s: (0, 0, 0)),
                  pl.BlockSpec((R_HEADS, 1, c), lambda i, s: (0, 0, 0))],
        out_specs=pl.BlockSpec((bsz, c, R_V_WIDTH), lambda i, s: (0, i, 0)),
        scratch_shapes=[pltpu.VMEM((bsz, R_QK_WIDTH, R_V_DIM), F32)],
    )
    return pl.pallas_call(
        _retention_kernel,
        grid_spec=grid_spec,
        out_shape=jax.ShapeDtypeStruct((bsz, seq, R_V_WIDTH), BF16),
        compiler_params=_cparams(("arbitrary",)),
        name="retention",
    )(cdec, proj3, proj3, proj3, proj3, decay, qdec, kdec)


def _pack_halves(x):
    w = x.shape[1] // 2
    bits = lax.bitcast_convert_type(x.astype(BF16).astype(F32), jnp.uint32)
    return (bits[:, :w] >> 16) | (bits[:, w:] & jnp.uint32(0xFFFF0000))


def _unpack_halves(p):
    lo = lax.bitcast_convert_type(p << 16, F32)
    hi = lax.bitcast_convert_type(p & jnp.uint32(0xFFFF0000), F32)
    return lo, hi


SUBLANES = 8


ROUTER_ROWS = 512


def _mix_kernel(ya_ref, yr_ref, ga_ref, gt_ref, x_ref, wpa_ref, wpr_ref, wout_ref,
                gt1_ref, g_ref, sc_ref, sh_ref, wr_ref, rb_ref,
                x1_ref, h2_ref, h2p_ref, e_ref, w_ref, r_ref, c_ref):
    a = _dot(ya_ref[...], wpa_ref[...]) * _sigmoid(ga_ref[...].astype(F32))
    r = _dot(yr_ref[...], wpr_ref[...]) * _sigmoid(gt_ref[...].astype(F32))
    mix = _dot((a + r).astype(BF16), wout_ref[...])
    x1 = x_ref[...] + gt1_ref[0] * mix
    x1_ref[...] = x1
    ms = jnp.mean(x1 * x1, axis=-1, keepdims=True)
    y = x1 * lax.rsqrt(ms + NORM_EPS) * g_ref[...]
    h2 = y * (1.0 + sc_ref[0]) + sh_ref[0]
    h2_b = h2.astype(BF16)
    h2_ref[...] = h2_b
    h2p_ref[...] = _pack_halves(h2)
    _route(h2_b, wr_ref, rb_ref, e_ref, w_ref, r_ref, c_ref)


def _mix(ya, yr, proj, x2d, wpa, wpr, wout, gt1, g, sc, sh, wr_t, bias_col, seq):
    t, d = x2d.shape
    tm = min(ROUTER_ROWS, seq)
    per_b = seq // tm
    row = lambda i: (i, 0)
    full = lambda i: (0, 0)
    by_tile = lambda i: (0, i)
    per_batch = lambda i: (i // per_b, 0, 0)
    return pl.pallas_call(
        _mix_kernel,
        grid=(t // tm,),
        in_specs=[pl.BlockSpec((tm, A_WIDTH), row),
                  pl.BlockSpec((tm, R_V_WIDTH), row),
                  pl.BlockSpec((tm, d), lambda i: (i, COL_GA // D_MODEL)),
                  pl.BlockSpec((tm, d), lambda i: (i, COL_GT // D_MODEL)),
                  pl.BlockSpec((tm, d), row),
                  pl.BlockSpec((A_WIDTH, d), full),
                  pl.BlockSpec((R_V_WIDTH, d), full),
                  pl.BlockSpec((d, d), full),
                  pl.BlockSpec((1, 1, d), per_batch),
                  pl.BlockSpec((1, d), full),
                  pl.BlockSpec((1, 1, d), per_batch),
                  pl.BlockSpec((1, 1, d), per_batch),
                  pl.BlockSpec((N_EXPERTS, d), full),
                  pl.BlockSpec((N_EXPERTS, 1), full)],
        out_specs=[pl.BlockSpec((tm, d), row), pl.BlockSpec((tm, d), row),
                   pl.BlockSpec((tm, d // 2), row),
                   pl.BlockSpec((TOP_K, tm), by_tile), pl.BlockSpec((TOP_K, tm), by_tile),
                   pl.BlockSpec((TOP_K, tm), by_tile), pl.BlockSpec((N_EXPERTS, LANES), by_tile)],
        out_shape=[jax.ShapeDtypeStruct((t, d), F32), jax.ShapeDtypeStruct((t, d), BF16),
                   jax.ShapeDtypeStruct((t, d // 2), jnp.uint32),
                   jax.ShapeDtypeStruct((TOP_K, t), jnp.int32),
                   jax.ShapeDtypeStruct((TOP_K, t), F32),
                   jax.ShapeDtypeStruct((TOP_K, t), jnp.int32),
                   jax.ShapeDtypeStruct((N_EXPERTS, (t // tm) * LANES), F32)],
        compiler_params=_cparams(("parallel",)),
        name="merge_norm_route",
    )(ya, yr, proj, proj, x2d, wpa, wpr, wout, gt1, g, sc, sh, wr_t, bias_col)


def _route(h, wr_ref, b_ref, e_ref, w_ref, r_ref, c_ref):
    logits = _dot_nt(wr_ref[...], h)
    scores = _sigmoid(logits)
    choice = scores + b_ref[...]
    tm = logits.shape[1]
    giota = lax.broadcasted_iota(jnp.int32, (GROUP_SIZE, tm), 0)
    gs_rows = []
    for g in range(N_GROUPS):
        cg = choice[g * GROUP_SIZE:(g + 1) * GROUP_SIZE, :]
        m1 = jnp.max(cg, axis=0, keepdims=True)
        i1 = jnp.min(jnp.where(cg == m1, giota, GROUP_SIZE), axis=0, keepdims=True)
        m2 = jnp.max(jnp.where(giota == i1, KNOCKED_OUT, cg), axis=0, keepdims=True)
        gs_rows.append(m1 + m2)
    gs = jnp.concatenate(gs_rows, axis=0)
    grow = lax.broadcasted_iota(jnp.int32, (N_GROUPS, tm), 0)
    gmask = jnp.zeros((N_GROUPS, tm), jnp.bool_)
    for _ in range(TOPK_GROUPS):
        mx = jnp.max(gs, axis=0, keepdims=True)
        ix = jnp.min(jnp.where(gs == mx, grow, N_GROUPS), axis=0, keepdims=True)
        hit = grow == ix
        gmask = jnp.logical_or(gmask, hit)
        gs = jnp.where(hit, KNOCKED_OUT, gs)
    gmask_f = jnp.where(gmask, 1.0, 0.0)
    masked = jnp.concatenate(
        [jnp.where(gmask_f[g:g + 1, :] > 0.5, choice[g * GROUP_SIZE:(g + 1) * GROUP_SIZE, :], NEG_INF)
         for g in range(N_GROUPS)], axis=0)
    erow = lax.broadcasted_iota(jnp.int32, (N_EXPERTS, tm), 0)
    idx_rows, w_rows = [], []
    chosen = jnp.zeros((N_EXPERTS, tm), F32)
    for _ in range(TOP_K):
        mx = jnp.max(masked, axis=0, keepdims=True)
        ix = jnp.min(jnp.where(masked == mx, erow, N_EXPERTS), axis=0, keepdims=True)
        hit = erow == ix
        w_rows.append(jnp.sum(jnp.where(hit, scores, 0.0), axis=0, keepdims=True))
        idx_rows.append(ix)
        chosen = jnp.where(hit, 1.0, chosen)
        masked = jnp.where(hit, KNOCKED_OUT, masked)
    w = jnp.concatenate(w_rows, axis=0)
    w = w / (jnp.sum(w, axis=0, keepdims=True) + 1e-20) * ROUTED_SCALE
    e_ref[...] = jnp.concatenate(idx_rows, axis=0)
    w_ref[...] = w
    chosen_b = chosen.astype(BF16)
    earlier = (lax.broadcasted_iota(jnp.int32, (tm, tm), 0)
               < lax.broadcasted_iota(jnp.int32, (tm, tm), 1)).astype(BF16)
    before = _dot(chosen_b, earlier)
    ranks = [jnp.sum(jnp.where(erow == ix, before, 0.0), axis=0, keepdims=True) for ix in idx_rows]
    r_ref[...] = jnp.concatenate(ranks, axis=0).astype(jnp.int32)
    c_ref[...] = _dot(chosen_b, jnp.ones((tm, LANES), BF16))


def _pos_kernel(e_ref, r_ref, base_ref, p_ref):
    tm = e_ref.shape[1]
    erow = lax.broadcasted_iota(jnp.int32, (N_EXPERTS, tm), 0)
    base = base_ref[0]
    rows = [jnp.sum(jnp.where(erow == e_ref[k:k + 1, :], base, 0.0), axis=0, keepdims=True)
            for k in range(TOP_K)]
    p_ref[0] = jnp.concatenate(rows, axis=0).astype(jnp.int32) + r_ref[...]


MOVE_ROWS = 512


def _positions(eidx_t, rank_t, tile_base):
    t = eidx_t.shape[1]
    tm = min(MOVE_ROWS, t)
    per_router_tile = ROUTER_ROWS // tm
    return pl.pallas_call(
        _pos_kernel,
        grid=(t // tm,),
        in_specs=[pl.BlockSpec((TOP_K, tm), lambda i: (0, i)),
                  pl.BlockSpec((TOP_K, tm), lambda i: (0, i)),
                  pl.BlockSpec((1, N_EXPERTS, 1), lambda i: (i // per_router_tile, 0, 0))],
        out_specs=pl.BlockSpec((1, TOP_K, tm), lambda i: (i, 0, 0)),
        out_shape=jax.ShapeDtypeStruct((t // tm, TOP_K, tm), jnp.int32),
        compiler_params=_cparams(("parallel",)),
        name="slot_positions",
    )(eidx_t, rank_t, tile_base)


SLOT_ROWS = 512
PAD_CHUNKS = (256, 128, 64, 32, 16, 8)


def _zero_pads_kernel(pad_start_ref, pad_len_ref, xs_in, xs_hbm, zero_buf, pad_sem):
    del xs_in
    zero_buf[...] = jnp.zeros_like(zero_buf)

    def pad_copies(e, wait):
        start = pad_start_ref[e]
        n = pad_len_ref[e]
        head = jnp.minimum((-start) & (SUBLANES - 1), n)

        def fill(first, size, pred):
            @pl.when(pred)
            def _():
                cp = pltpu.make_async_copy(zero_buf.at[pl.ds(0, size), :], xs_hbm.at[pl.ds(first, size), :], pad_sem)
                if wait:
                    cp.wait()
                else:
                    cp.start()

        for j in range(SUBLANES - 1):
            fill(start + j, 1, j < head)
        ptr = start + head
        rest = n - head
        for chunk in PAD_CHUNKS:
            fill(pl.multiple_of(ptr, SUBLANES), chunk, (rest & chunk) != 0)
            ptr = ptr + (rest & chunk)

    def issue(e, carry):
        pad_copies(e, False)
        return carry

    def drain(e, carry):
        pad_copies(e, True)
        return carry

    lax.fori_loop(0, N_EXPERTS, issue, 0)
    lax.fori_loop(0, N_EXPERTS, drain, 0)


def _zero_pads(pad_start, pad_len, xs):
    grid_spec = pltpu.PrefetchScalarGridSpec(
        num_scalar_prefetch=2,
        grid=(1,),
        in_specs=[pl.BlockSpec(memory_space=pl.ANY)],
        out_specs=pl.BlockSpec(memory_space=pl.ANY),
        scratch_shapes=[pltpu.VMEM((PAD_CHUNKS[0], xs.shape[1]), xs.dtype),
                        pltpu.SemaphoreType.DMA],
    )
    return pl.pallas_call(
        _zero_pads_kernel,
        grid_spec=grid_spec,
        out_shape=jax.ShapeDtypeStruct(xs.shape, xs.dtype),
        input_output_aliases={2: 0},
        compiler_params=_cparams(("arbitrary",)),
        name="zero_pad_slots",
    )(pad_start, pad_len, xs)


SC_DISPATCH_TOKENS = 64


def _sc_dispatch(pos_blocks, h_rows, n_rows):
    info = plsc.get_sparse_core_info()
    n_cores = info.num_cores
    n_workers = n_cores * info.num_subcores
    t = h_rows.shape[0]
    chunk = SC_DISPATCH_TOKENS
    steps = t // (n_workers * chunk)
    assert steps * n_workers * chunk == t
    mesh = plsc.VectorSubcoreMesh(core_axis_name="c", subcore_axis_name="s")

    @functools.partial(
        pl.kernel, mesh=mesh,
        out_type=jax.ShapeDtypeStruct((n_rows,) + h_rows.shape[1:], h_rows.dtype),
        scratch_types=[pltpu.VMEM((TOP_K, chunk), jnp.int32),
                       pltpu.VMEM((chunk,) + h_rows.shape[1:], h_rows.dtype),
                       pltpu.SemaphoreType.DMA],
        name="sc_dispatch_rows",
    )
    def scatter_rows(pos_hbm, h_hbm, out_hbm, idx_v, rows_v, sem):
        wid = lax.axis_index("s") * n_cores + lax.axis_index("c")

        @pl.loop(0, steps)
        def _(step):
            blk = wid * steps + step
            pltpu.sync_copy(pos_hbm.at[blk], idx_v)
            pltpu.sync_copy(h_hbm.at[pl.ds(blk * chunk, chunk)], rows_v)
            scatters = [pltpu.make_async_copy(rows_v, out_hbm.at[idx_v.at[k]], sem) for k in range(TOP_K)]
            for cp in scatters:
                cp.start()
            for cp in scatters:
                cp.wait()

    return scatter_rows(pos_blocks, h_rows)


WEIGHT_SLOTS = 4


def _experts_kernel(blk_e_ref, nblk_ref, ord_ref, eid_ref, nord_ref, x_ref, w1_hbm, w3_hbm, w2_hbm, y_ref,
                    w1f, w3f, w2f, wsem):
    s = pl.program_id(0)

    def fetch(j):
        slot = j % WEIGHT_SLOTS
        e = eid_ref[j]
        return [pltpu.make_async_copy(src.at[e], dst.at[slot], wsem.at[slot])
                for src, dst in ((w1_hbm, w1f), (w3_hbm, w3f), (w2_hbm, w2f))]

    @pl.when(s < nblk_ref[0])
    def _():
        j = ord_ref[s]

        ahead = WEIGHT_SLOTS - 1

        @pl.when(s == 0)
        def _():
            for first in range(ahead):
                @pl.when(first < nord_ref[0])
                def _(first=first):
                    for cp in fetch(first):
                        cp.start()

        @pl.when(jnp.logical_or(s == 0, blk_e_ref[s] != blk_e_ref[jnp.maximum(s - 1, 0)]))
        def _():
            for cp in fetch(j):
                cp.wait()

            @pl.when(j + ahead < nord_ref[0])
            def _():
                for cp in fetch(j + ahead):
                    cp.start()

        slot = j % WEIGHT_SLOTS
        half = x_ref.shape[1]
        lo, hi = _unpack_halves(x_ref[...])
        lo = lo.astype(BF16)
        hi = hi.astype(BF16)
        h1 = (_dot(lo, w1f[slot, :half, :].astype(BF16)) + _dot(hi, w1f[slot, half:, :].astype(BF16)))
        h3 = (_dot(lo, w3f[slot, :half, :].astype(BF16)) + _dot(hi, w3f[slot, half:, :].astype(BF16)))
        mid = (_silu(h1) * h3).astype(BF16)
        y_ref[...] = _pack_halves(_dot(mid, w2f[slot].astype(BF16)))


def _sc_gather(pos_blocks, ys, t):
    info = plsc.get_sparse_core_info()
    n_cores = info.num_cores
    n_workers = n_cores * info.num_subcores
    chunk = SC_DISPATCH_TOKENS
    steps = t // (n_workers * chunk)
    assert steps * n_workers * chunk == t
    mesh = plsc.VectorSubcoreMesh(core_axis_name="c", subcore_axis_name="s")

    @functools.partial(
        pl.kernel, mesh=mesh,
        out_type=jax.ShapeDtypeStruct((TOP_K * t,) + ys.shape[1:], ys.dtype),
        scratch_types=[pltpu.VMEM((TOP_K, chunk), jnp.int32),
                       pltpu.VMEM((chunk,) + ys.shape[1:], ys.dtype),
                       pltpu.VMEM((chunk,) + ys.shape[1:], ys.dtype),
                       pltpu.SemaphoreType.DMA((2,))],
        name="sc_gather_rows",
    )
    def gather_rows(pos_hbm, ys_hbm, out_hbm, idx_v, rows_a, rows_b, sems):
        wid = lax.axis_index("s") * n_cores + lax.axis_index("c")
        bufs = (rows_a, rows_b)

        @pl.loop(0, steps)
        def _(step):
            blk = wid * steps + step
            pltpu.sync_copy(pos_hbm.at[blk], idx_v)
            gathers = [pltpu.make_async_copy(ys_hbm.at[idx_v.at[k]], bufs[k % 2], sems.at[k % 2])
                       for k in range(TOP_K)]
            gathers[0].start()
            for k in range(TOP_K):
                gathers[k].wait()
                if k + 1 < TOP_K:
                    gathers[k + 1].start()
                pltpu.sync_copy(bufs[k % 2], out_hbm.at[pl.ds(k * t + blk * chunk, chunk)])

    return gather_rows(pos_blocks, ys)


def _experts(blk_e, nblk_used, blk_ord, eid_of_ord, n_ord, xs, w1, w3, w2):
    n_rows, half = xs.shape
    d = D_MODEL
    blk = lambda s, be, nb, bo, eo, no: (jnp.minimum(s, nb[0] - 1), 0)
    grid_spec = pltpu.PrefetchScalarGridSpec(
        num_scalar_prefetch=5,
        grid=(n_rows // SLOT_ROWS,),
        in_specs=[pl.BlockSpec((SLOT_ROWS, half), blk),
                  pl.BlockSpec(memory_space=pl.ANY),
                  pl.BlockSpec(memory_space=pl.ANY),
                  pl.BlockSpec(memory_space=pl.ANY)],
        out_specs=pl.BlockSpec((SLOT_ROWS, half), blk),
        scratch_shapes=[pltpu.VMEM((WEIGHT_SLOTS, d, EXPERT_FF), F32),
                        pltpu.VMEM((WEIGHT_SLOTS, d, EXPERT_FF), F32),
                        pltpu.VMEM((WEIGHT_SLOTS, EXPERT_FF, d), F32),
                        pltpu.SemaphoreType.DMA((WEIGHT_SLOTS,))],
    )
    return pl.pallas_call(
        _experts_kernel,
        grid_spec=grid_spec,
        out_shape=jax.ShapeDtypeStruct((n_rows, half), jnp.uint32),
        compiler_params=_cparams(("arbitrary",)),
        name="routed_experts",
    )(blk_e, nblk_used, blk_ord, eid_of_ord, n_ord, xs, w1, w3, w2)


def _combine_kernel(*refs):
    y_refs = refs[:TOP_K]
    w_ref, h_ref, x1_ref, ws1_ref, ws3_ref, ws2_ref, gt2_ref, g_ref = refs[TOP_K:TOP_K + 8]
    o_ref = refs[-1]
    tm, d = x1_ref.shape
    half = d // 2
    h = h_ref[...]
    mid = (_silu(_dot(h, ws1_ref[...])) * _dot(h, ws3_ref[...])).astype(BF16)
    shared = _dot(mid, ws2_ref[...])
    w = w_ref[...]
    acc_lo = jnp.zeros((tm, half), F32)
    acc_hi = jnp.zeros((tm, half), F32)
    for k in range(TOP_K):
        lo, hi = _unpack_halves(y_refs[k][...])
        acc_lo = acc_lo + lo * w[:, k:k + 1]
        acc_hi = acc_hi + hi * w[:, k:k + 1]
    routed = jnp.concatenate([acc_lo, acc_hi], axis=1)
    x2 = x1_ref[...] + gt2_ref[0] * (routed + shared)
    ms = jnp.mean(x2 * x2, axis=-1, keepdims=True)
    o_ref[...] = x2 * lax.rsqrt(ms + NORM_EPS) * g_ref[...]


COMBINE_ROWS = 256
COMBINE_PARTS = 4


def _combine(y_kt, wts, h2, x1, ws1, ws3, ws2, gt2, g_final, seq, part, n_parts, prev_out):
    t, d = x1.shape
    tm = min(COMBINE_ROWS, seq)
    per_b = seq // tm
    tiles = t // tm // n_parts
    first = part * tiles
    row = lambda i: (first + i, 0)
    full = lambda i: (0, 0)
    y_specs = [pl.BlockSpec((tm, d // 2), functools.partial(lambda i, k: (k * tiles + i, 0), k=k))
               for k in range(TOP_K)]
    in_specs = y_specs + [
        pl.BlockSpec((tm, TOP_K), row),
        pl.BlockSpec((tm, d), row),
        pl.BlockSpec((tm, d), row),
        pl.BlockSpec((d, SHARED_FF), full),
        pl.BlockSpec((d, SHARED_FF), full),
        pl.BlockSpec((SHARED_FF, d), full),
        pl.BlockSpec((1, 1, d), lambda i: ((first + i) // per_b, 0, 0)),
        pl.BlockSpec((1, d), full)]
    args = [y_kt] * TOP_K + [wts, h2, x1, ws1, ws3, ws2, gt2, g_final]
    aliases = {}
    if prev_out is not None:
        in_specs.append(pl.BlockSpec(memory_space=pl.ANY))
        aliases = {len(args): 0}
        args.append(prev_out)
    return pl.pallas_call(
        _combine_kernel,
        grid=(tiles,),
        in_specs=in_specs,
        out_specs=pl.BlockSpec((tm, d), row),
        out_shape=jax.ShapeDtypeStruct((t, d), F32),
        input_output_aliases=aliases,
        compiler_params=_cparams(("parallel",)),
        name="combine_shared_final",
    )(*args)


def _slot_tables(cnt, t):
    ntiles = cnt.shape[1] // LANES
    cnt_tile = cnt.reshape(N_EXPERTS, ntiles, LANES)[:, :, 0].astype(jnp.int32)
    counts = jnp.sum(cnt_tile, axis=1)
    padded = (counts + SLOT_ROWS - 1) // SLOT_ROWS * SLOT_ROWS
    pstart = jnp.cumsum(padded) - padded
    tile_base = pstart[:, None] + jnp.cumsum(cnt_tile, axis=1) - cnt_tile
    n_blk = -(-(t * TOP_K) // SLOT_ROWS) + N_EXPERTS
    blk_end = jnp.cumsum(padded // SLOT_ROWS)
    blk_e = jnp.sum((blk_end[None, :] <= jnp.arange(n_blk)[:, None]).astype(jnp.int32), axis=1)
    blk_e = jnp.minimum(blk_e, N_EXPERTS - 1)
    owns = (padded > 0).astype(jnp.int32)
    ord_of_e = jnp.cumsum(owns) - owns
    ids = jnp.arange(N_EXPERTS, dtype=jnp.int32)
    eid_of_ord = jnp.sum(jnp.where((ord_of_e[None, :] == ids[:, None]) & (owns[None, :] > 0), ids[None, :], 0), axis=1)
    blk_ord = jnp.sum(jnp.where(blk_e[:, None] == ids[None, :], ord_of_e[None, :], 0), axis=1)
    experts_tables = (blk_e, blk_end[-1:].astype(jnp.int32), blk_ord.astype(jnp.int32),
                      eid_of_ord.astype(jnp.int32), jnp.sum(owns).reshape(1).astype(jnp.int32))
    return (experts_tables, pstart + counts, padded - counts,
            tile_base.T.astype(F32).reshape(ntiles, N_EXPERTS, 1), n_blk * SLOT_ROWS)


def _permute_in_cols(w_in):
    qa, ka, va, qr, kr, vr, gr, ga, gt = jnp.split(
        w_in, np.cumsum((A_WIDTH, A_WIDTH, A_WIDTH, R_QK_WIDTH, R_QK_WIDTH, R_V_WIDTH, R_V_WIDTH,
                         D_MODEL))[:].tolist(), axis=1)
    return jnp.concatenate([vr, gr, ga, gt, qa, ka, va, qr, kr], axis=1)


def kernel(x, c, w_ada, b_ada, g_mix, w_in, w_pa, w_pr, w_out, g_ffn, w_router, router_bias,
           w1, w3, w2, ws1, ws3, ws2, g_final):
    bsz, seq, d = x.shape
    t = bsz * seq
    depth = w_ada.shape[0]
    assert depth == 1, "the final norm is fused into the single layer's last kernel"
    rest = jnp.exp2(-8.0 / A_HEADS * jnp.arange(1, A_HEADS + 1, dtype=F32)) * LOG2_E
    pieces = []
    for _ in range(ALIBI_PIECES):
        pieces.append(rest.astype(BF16).astype(F32))
        rest = rest - pieces[-1]
    slopes = jnp.stack(pieces, axis=1).reshape(-1)
    x2d = x.reshape(t, d)
    for l in range(depth):
        mod = _ada(c, w_ada[l], b_ada[l])
        sh1, sc1, gt1, sh2, sc2, gt2 = [m.reshape(bsz, 1, d) for m in jnp.split(mod, 6, axis=-1)]
        w_in_p = _permute_in_cols(w_in[l]).astype(BF16)
        proj = _inproj(x2d, g_mix[l].reshape(1, d), sc1, sh1, w_in_p, seq)
        proj3 = proj.reshape(bsz, seq, IN_COLS)
        ya = _moba(proj3, slopes).reshape(t, A_WIDTH)
        yr = _retention(proj3).reshape(t, R_V_WIDTH)
        x1, h2, h2p, eidx_t, wts_t, rank_t, cnt = _mix(
            ya, yr, proj, x2d, w_pa[l].astype(BF16), w_pr[l].astype(BF16), w_out[l].astype(BF16), gt1,
            g_ffn[l].reshape(1, d), sc2, sh2, w_router[l].T.astype(BF16), router_bias[l].reshape(N_EXPERTS, 1), seq)
        experts_tables, pad_start, pad_len, tile_base, n_rows = _slot_tables(cnt, t)
        pos3 = _positions(eidx_t, rank_t, tile_base)
        pos_blocks = jnp.transpose(
            pos3.reshape(pos3.shape[0], TOP_K, -1, SC_DISPATCH_TOKENS), (0, 2, 1, 3)
        ).reshape(t // SC_DISPATCH_TOKENS, TOP_K, SC_DISPATCH_TOKENS)
        xs = _zero_pads(pad_start, pad_len, _sc_dispatch(pos_blocks, h2p, n_rows))
        ys = _experts(*experts_tables, xs, w1[l], w3[l], w2[l])
        wts = wts_t.T
        shared_w = (ws1[l].astype(BF16), ws3[l].astype(BF16), ws2[l].astype(BF16))
        blocks_per_part = pos_blocks.shape[0] // COMBINE_PARTS
        x2d = None
        for part in range(COMBINE_PARTS):
            y_kt = _sc_gather(pos_blocks[part * blocks_per_part:(part + 1) * blocks_per_part], ys,
                              t // COMBINE_PARTS)
            x2d = _combine(y_kt, wts, h2, x1, *shared_w, gt2, g_final.reshape(1, d), seq,
                           part, COMBINE_PARTS, x2d)
    return x2d.reshape(bsz, seq, d)
```

```python
import functools

import jax
import jax.numpy as jnp
import numpy as np
from jax import lax
from jax.experimental import pallas as pl
from jax.experimental.pallas import tpu as pltpu
from jax.experimental.pallas import tpu_sc as plsc

F32 = jnp.float32
BF16 = jnp.bfloat16

D_MODEL = 1024
A_HEADS = 8
A_HEAD_DIM = 64
A_WIDTH = A_HEADS * A_HEAD_DIM
MOBA_BLOCK = 256
MOBA_TOPK = 3
R_HEADS = 8
R_QK_DIM = 64
R_V_DIM = 128
R_QK_WIDTH = R_HEADS * R_QK_DIM
R_V_WIDTH = R_HEADS * R_V_DIM
R_CHUNK = 256
N_EXPERTS = 256
TOP_K = 8
N_GROUPS = 8
GROUP_SIZE = N_EXPERTS // N_GROUPS
TOPK_GROUPS = 4
EXPERT_FF = 256
SHARED_FF = 256
ROUTED_SCALE = 2.5
NORM_EPS = 1e-6
GN_EPS = 1e-6
NEG_INF = -1e30
KNOCKED_OUT = -3e38

COL_VR, COL_GR, COL_GA, COL_GT = 0, 1024, 2048, 3072
COL_QA, COL_KA, COL_VA, COL_QR, COL_KR = 4096, 4608, 5120, 5632, 6144
IN_COLS = 6656
AUG = 128
FEAT_BIAS = A_HEAD_DIM
FEAT_POS = A_HEAD_DIM + 32
ALIBI_PIECES = 3
LOG2_E = 1.4426950408889634
LANES = 128
BF16_ROWS = 16
V_ROWS = A_HEAD_DIM + BF16_ROWS
MOBA_HEADS_PER_STEP = 8

VMEM_LIMIT = 56 * 1024 * 1024


def _cparams(sem, vmem=VMEM_LIMIT):
    return pltpu.CompilerParams(dimension_semantics=sem, vmem_limit_bytes=vmem)


def _dot(a, b):
    return jnp.dot(a, b, preferred_element_type=F32)


def _dot_nt(a, b):
    return lax.dot_general(a, b, (((1,), (1,)), ((), ())), preferred_element_type=F32)


def _sigmoid(x):
    return 1.0 / (1.0 + jnp.exp(-x))


def _silu(x):
    return x * _sigmoid(x)


def _ada_kernel(c_ref, w_ref, b_ref, o_ref):
    c = c_ref[...]
    s = _silu(c)
    s_hi = s.astype(BF16)
    s_lo = (s - s_hi.astype(F32)).astype(BF16)
    w = w_ref[...]
    w_hi = w.astype(BF16)
    w_lo = (w - w_hi.astype(F32)).astype(BF16)
    o_ref[...] = _dot(s_hi, w_hi) + _dot(s_hi, w_lo) + _dot(s_lo, w_hi) + b_ref[...]


def _ada(c, w_ada, b_ada):
    bsz, d = c.shape
    n = w_ada.shape[1]
    tn = 1024
    return pl.pallas_call(
        _ada_kernel,
        grid=(n // tn,),
        in_specs=[pl.BlockSpec((bsz, d), lambda j: (0, 0)),
                  pl.BlockSpec((d, tn), lambda j: (0, j)),
                  pl.BlockSpec((1, tn), lambda j: (0, j))],
        out_specs=pl.BlockSpec((bsz, tn), lambda j: (0, j)),
        out_shape=jax.ShapeDtypeStruct((bsz, n), F32),
        compiler_params=_cparams(("parallel",)),
        name="ada_mod",
    )(c, w_ada, b_ada.reshape(1, n))


INPROJ_COLS = 512


def _inproj_kernel(x_ref, g_ref, sc_ref, sh_ref, w_ref, o_ref):
    x = x_ref[...]
    ms = jnp.mean(x * x, axis=-1, keepdims=True)
    y = x * lax.rsqrt(ms + NORM_EPS) * g_ref[...]
    h = (y * (1.0 + sc_ref[0]) + sh_ref[0]).astype(BF16)
    for j in range(w_ref.shape[1] // INPROJ_COLS):
        cols = slice(j * INPROJ_COLS, (j + 1) * INPROJ_COLS)
        o_ref[:, cols] = _dot(h, w_ref[:, cols]).astype(BF16)


def _inproj(x2d, g, sc, sh, w_bf16, seq):
    t, d = x2d.shape
    n = w_bf16.shape[1]
    tm = min(512, seq)
    per_b = seq // tm
    return pl.pallas_call(
        _inproj_kernel,
        grid=(t // tm,),
        in_specs=[pl.BlockSpec((tm, d), lambda i: (i, 0)),
                  pl.BlockSpec((1, d), lambda i: (0, 0)),
                  pl.BlockSpec((1, 1, d), lambda i: (i // per_b, 0, 0)),
                  pl.BlockSpec((1, 1, d), lambda i: (i // per_b, 0, 0)),
                  pl.BlockSpec((d, n), lambda i: (0, 0))],
        out_specs=pl.BlockSpec((tm, n), lambda i: (i, 0)),
        out_shape=jax.ShapeDtypeStruct((t, n), BF16),
        compiler_params=_cparams(("parallel",)),
        name="norm_inproj",
    )(x2d, g, sc, sh, w_bf16)


def _moba_prepare(i, slopes_ref, q_ref, k_ref, v_ref, ko_ref, qo_ref, vo_ref, kmean_scr):
    nblk = kmean_scr.shape[0]
    width = q_ref.shape[1]
    seq_rows = pl.ds(pl.multiple_of(i * MOBA_BLOCK, MOBA_BLOCK), MOBA_BLOCK)
    q = q_ref[...]
    k = k_ref[...]
    v = v_ref[...]
    kmean_scr[pl.ds(i, 1), :] = jnp.mean(k.astype(F32), axis=0, keepdims=True)

    eye = (lax.broadcasted_iota(jnp.int32, (A_HEAD_DIM, A_HEAD_DIM), 0)
           == lax.broadcasted_iota(jnp.int32, (A_HEAD_DIM, A_HEAD_DIM), 1)).astype(BF16)
    heads = [slice(h * A_HEAD_DIM, (h + 1) * A_HEAD_DIM) for h in range(width // A_HEAD_DIM)]
    q_t = jnp.concatenate([_dot_nt(eye, q[:, hd]) for hd in heads], axis=0)
    v_t = jnp.concatenate([_dot_nt(eye, v[:, hd]) for hd in heads], axis=0)

    km = kmean_scr[...]
    km_rep = jnp.concatenate([km] * A_HEADS, axis=0)
    r_head = lax.broadcasted_iota(jnp.int32, km_rep.shape, 0) // nblk
    c_head = lax.broadcasted_iota(jnp.int32, km_rep.shape, 1) // A_HEAD_DIM
    km_bd = jnp.where(r_head == c_head, km_rep, 0.0)
    km_hi = km_bd.astype(BF16)
    km_lo = (km_bd - km_hi.astype(F32)).astype(BF16)
    q_t_b = q_t.astype(BF16)
    gate_all = _dot(km_hi, q_t_b) + _dot(km_lo, q_t_b)

    mb = q.shape[0]
    blk = lax.broadcasted_iota(jnp.int32, (nblk, mb), 0)
    lane_pos = lax.broadcasted_iota(jnp.int32, (BF16_ROWS, mb), 1).astype(F32)
    row16 = lax.broadcasted_iota(jnp.int32, (BF16_ROWS, mb), 0)
    key_pos = lax.broadcasted_iota(jnp.int32, (mb, AUG), 0).astype(F32)
    kcol = lax.broadcasted_iota(jnp.int32, (mb, AUG), 1)
    widen = (lax.broadcasted_iota(jnp.int32, (A_HEAD_DIM, AUG), 0)
             == lax.broadcasted_iota(jnp.int32, (A_HEAD_DIM, AUG), 1)).astype(BF16)

    blk_first = (i * mb).astype(F32)
    for h in range(A_HEADS):
        pieces = [slopes_ref[h * ALIBI_PIECES + c] for c in range(ALIBI_PIECES)]
        g = jnp.where(blk < i, gate_all[h * nblk:(h + 1) * nblk, :], NEG_INF)
        sel = jnp.zeros((nblk, mb), jnp.bool_)
        for r in range(MOBA_TOPK):
            m = jnp.max(g, axis=0, keepdims=True)
            idx = jnp.min(jnp.where(g == m, blk, nblk), axis=0, keepdims=True)
            hit = blk == idx
            sel = jnp.logical_or(sel, jnp.logical_and(hit, r < i))
            g = jnp.where(hit, KNOCKED_OUT, g)
        bias_t = jnp.where(sel, 0.0, NEG_INF)

        scale = A_HEAD_DIM ** -0.5 * LOG2_E
        qo_ref[0, h, 0:A_HEAD_DIM, :] = (q_t[h * A_HEAD_DIM:(h + 1) * A_HEAD_DIM, :] * scale).astype(BF16)
        qo_ref[0, h, FEAT_BIAS:FEAT_BIAS + nblk, :] = bias_t.astype(BF16)
        if nblk < 32:
            qo_ref[0, h, FEAT_BIAS + nblk:FEAT_POS, :] = jnp.zeros((32 - nblk, mb), BF16)
        piece_rows = jnp.where(row16 % ALIBI_PIECES == 0, pieces[0],
                               jnp.where(row16 % ALIBI_PIECES == 1, pieces[1], pieces[2]))
        pos_feat = jnp.where(row16 < 3, -lane_pos,
                             jnp.where(row16 < 6, piece_rows,
                                       jnp.where(row16 < 9, -blk_first, jnp.where(row16 < 12, piece_rows, 0.0))))
        qo_ref[0, h, FEAT_POS:FEAT_POS + BF16_ROWS, :] = pos_feat.astype(BF16)
        qo_ref[0, h, FEAT_POS + BF16_ROWS:AUG, :] = jnp.zeros((AUG - FEAT_POS - BF16_ROWS, mb), BF16)

        vo_ref[0, h, 0:A_HEAD_DIM, seq_rows] = v_t[h * A_HEAD_DIM:(h + 1) * A_HEAD_DIM, :].astype(BF16)
        vo_ref[0, h, A_HEAD_DIM:V_ROWS, seq_rows] = jnp.where(row16 == 0, 1.0, 0.0).astype(BF16)

        pos_col = kcol - FEAT_POS
        piece_cols = jnp.where(pos_col % ALIBI_PIECES == 0, pieces[0],
                               jnp.where(pos_col % ALIBI_PIECES == 1, pieces[1], pieces[2]))
        k_feat = jnp.where(
            kcol == FEAT_BIAS + i, 1.0,
            jnp.where(pos_col < 0, 0.0,
                      jnp.where(pos_col < 3, piece_cols,
                                jnp.where(pos_col < 6, key_pos,
                                          jnp.where(pos_col < 9, piece_cols,
                                                    jnp.where(pos_col < 12, blk_first, 0.0))))))
        k_wide = _dot(k[:, h * A_HEAD_DIM:(h + 1) * A_HEAD_DIM], widen)
        ko_ref[0, h, seq_rows, :] = (k_wide + k_feat).astype(BF16)


def _moba_attend(i, q_ref, k_ref, v_ref, o_ref, s_a, s_b, s_c, group, n_groups):
    mb = MOBA_BLOCK
    span = group * mb
    own = pl.multiple_of(i * mb, mb)
    key_i = lax.broadcasted_iota(jnp.int32, (mb, mb), 0)
    qry_i = lax.broadcasted_iota(jnp.int32, (mb, mb), 1)
    feat = lax.broadcasted_iota(jnp.int32, (AUG, mb), 0)
    is_bias = jnp.logical_and(feat >= FEAT_BIAS, feat < FEAT_POS)
    q_ts, carry0 = [], []
    for hh in range(MOBA_HEADS_PER_STEP):
        q_t = q_ref[0, hh]
        q_ts.append(q_t)
        q_own = jnp.where(is_bias, jnp.zeros_like(q_t), q_t)
        s = _dot(k_ref[0, hh, pl.ds(own, mb), :], q_own)
        s = jnp.where(key_i <= qry_i, s, NEG_INF)
        m0 = jnp.max(s, axis=0, keepdims=True)
        p = jnp.exp2(s - m0)
        carry0 += [m0, _dot(v_ref[0, hh, :, pl.ds(own, mb)], p.astype(BF16))]

    def scores(g, dst):
        start = pl.multiple_of(jnp.minimum(g, n_groups - 1) * span, span)
        for hh in range(MOBA_HEADS_PER_STEP):
            dst[hh] = _dot(k_ref[0, hh, pl.ds(start, span), :], q_ts[hh])

    def absorb(g, src, carry):
        start = pl.multiple_of(g * span, span)
        new = []
        for hh in range(MOBA_HEADS_PER_STEP):
            m, acc = carry[2 * hh], carry[2 * hh + 1]
            sb = src[hh]
            m_new = jnp.maximum(m, jnp.max(sb, axis=0, keepdims=True))
            pb = jnp.exp2(sb - m_new)
            alpha = jnp.exp2(m - m_new)
            acc = acc * alpha + _dot(v_ref[0, hh, :, pl.ds(start, span)], pb.astype(BF16))
            new += [m_new, acc]
        return tuple(new)

    ring = (s_a, s_b, s_c)

    def body(trip, carry):
        g = len(ring) * trip
        for r in range(len(ring)):
            scores(g + r + 1, ring[(r + 1) % len(ring)])
            carry = absorb(g + r, ring[r], carry)
        return carry

    scores(0, s_a)
    live_groups = (i + group - 1) // group
    trips = live_groups // len(ring)
    res = lax.fori_loop(0, trips, body, tuple(carry0))
    done = len(ring) * trips
    res = lax.cond(live_groups > done, lambda c: absorb(done, s_a, c), lambda c: c, res)

    def straggler(g, c):
        scores(g, s_b)
        return absorb(g, s_b, c)

    res = lax.fori_loop(done + 1, live_groups, straggler, res)
    outs = [res[2 * hh + 1][0:A_HEAD_DIM, :] / res[2 * hh + 1][A_HEAD_DIM:A_HEAD_DIM + 1, :] for hh in range(MOBA_HEADS_PER_STEP)]
    o_t = jnp.concatenate(outs, axis=0).astype(BF16)
    eye = (key_i == qry_i).astype(BF16)
    o_ref[0] = _dot_nt(eye, o_t).astype(BF16)


def _moba_kernel(slopes_ref, q_ref, k_ref, v_ref, o_ref, kmean_scr, q_scr, k_scr, v_scr, s_a, s_b, s_c,
                 *, group, n_groups):
    i = pl.program_id(1)

    @pl.when(i == 0)
    def _():
        kmean_scr[...] = jnp.zeros_like(kmean_scr)
        k_scr[...] = jnp.zeros_like(k_scr)
        v_scr[...] = jnp.zeros_like(v_scr)

    _moba_prepare(i, slopes_ref, q_ref, k_ref, v_ref, k_scr, q_scr, v_scr, kmean_scr)
    _moba_attend(i, q_scr, k_scr, v_scr, o_ref, s_a, s_b, s_c, group, n_groups)


def _moba(proj3, slopes):
    bsz, seq, _ = proj3.shape
    mb = MOBA_BLOCK
    nblk = seq // mb
    group = min(2, nblk)
    n_groups = nblk // group
    assert MOBA_HEADS_PER_STEP == A_HEADS
    assert seq % mb == 0 and nblk % group == 0 and nblk <= FEAT_POS - FEAT_BIAS, "one selection feature per key block"
    grid_spec = pltpu.PrefetchScalarGridSpec(
        num_scalar_prefetch=1,
        grid=(bsz, nblk),
        in_specs=[pl.BlockSpec((None, mb, A_WIDTH), lambda b, i, s: (b, i, COL_QA // A_WIDTH)),
                  pl.BlockSpec((None, mb, A_WIDTH), lambda b, i, s: (b, i, COL_KA // A_WIDTH)),
                  pl.BlockSpec((None, mb, A_WIDTH), lambda b, i, s: (b, i, COL_VA // A_WIDTH))],
        out_specs=pl.BlockSpec((1, mb, A_WIDTH), lambda b, i, s: (b, i, 0)),
        scratch_shapes=[pltpu.VMEM((nblk, A_WIDTH), F32),
                        pltpu.VMEM((1, A_HEADS, AUG, mb), BF16),
                        pltpu.VMEM((1, A_HEADS, seq, AUG), BF16),
                        pltpu.VMEM((1, A_HEADS, V_ROWS, seq), BF16),
                        pltpu.VMEM((A_HEADS, group * mb, mb), F32),
                        pltpu.VMEM((A_HEADS, group * mb, mb), F32),
                        pltpu.VMEM((A_HEADS, group * mb, mb), F32)],
    )
    return pl.pallas_call(
        functools.partial(_moba_kernel, group=group, n_groups=n_groups),
        grid_spec=grid_spec,
        out_shape=jax.ShapeDtypeStruct((bsz, seq, A_WIDTH), BF16),
        compiler_params=_cparams(("parallel", "arbitrary")),
        name="moba_attention",
    )(slopes, proj3, proj3, proj3)


def _retention_kernel(cdec_ref, q_ref, k_ref, v_ref, g_ref, decay_ref, qdec_ref, kdec_ref, o_ref, state_scr):
    @pl.when(pl.program_id(0) == 0)
    def _():
        state_scr[...] = jnp.zeros_like(state_scr)

    width = q_ref.shape[2]
    eye = (lax.broadcasted_iota(jnp.int32, (width, width), 0)
           == lax.broadcasted_iota(jnp.int32, (width, width), 1)).astype(BF16)
    for b in range(q_ref.shape[0]):
        q = q_ref[b]
        k_t = _dot_nt(eye, k_ref[b])
        k_t_b = k_t.astype(BF16)
        state_b = state_scr[b].astype(BF16)
        for h in range(R_HEADS):
            rows = slice(h * R_QK_DIM, (h + 1) * R_QK_DIM)
            cols = slice(h * R_V_DIM, (h + 1) * R_V_DIM)
            q_h = q[:, rows]
            v_h = v_ref[b, :, cols]
            inner = _dot(q_h, k_t_b[rows, :]) * decay_ref[h]
            out = _dot(inner.astype(BF16), v_h) + _dot(q_h, state_b[rows, :]) * qdec_ref[h]
            k_dec = (k_t[rows, :] * kdec_ref[h]).astype(BF16)
            state_scr[b, rows, :] = cdec_ref[h] * state_scr[b, rows, :] + _dot(k_dec, v_h)
            mu = jnp.mean(out, axis=-1, keepdims=True)
            cen = out - mu
            var = jnp.mean(cen * cen, axis=-1, keepdims=True)
            y = cen * lax.rsqrt(var + GN_EPS)
            o_ref[b, :, cols] = (y * _silu(g_ref[b, :, cols].astype(F32))).astype(BF16)


def _retention_consts():
    h = np.arange(R_HEADS, dtype=np.float64)
    log_g = np.log(1.0 - np.exp2(-5.0 - h))
    n = np.arange(R_CHUNK, dtype=np.float64)
    diff = n[:, None] - n[None, :]
    scale = R_QK_DIM ** -0.5
    decay = np.where(diff >= 0, np.exp(np.maximum(diff, 0.0) * log_g[:, None, None]), 0.0) * scale
    q_decay = np.exp((n + 1.0) * log_g[:, None])[:, :, None]
    k_decay = np.exp((R_CHUNK - 1.0 - n) * log_g[:, None])[:, None, :] * scale
    chunk_decay = np.exp(R_CHUNK * log_g)
    return (jnp.asarray(decay, F32), jnp.asarray(q_decay, F32), jnp.asarray(k_decay, F32),
            jnp.asarray(chunk_decay, F32))


def _retention(proj3):
    bsz, seq, _ = proj3.shape
    c = R_CHUNK
    decay, qdec, kdec, cdec = _retention_consts()
    grid_spec = pltpu.PrefetchScalarGridSpec(
        num_scalar_prefetch=1,
        grid=(seq // c,),
        in_specs=[pl.BlockSpec((bsz, c, R_QK_WIDTH), lambda i, s: (0, i, COL_QR // R_QK_WIDTH)),
                  pl.BlockSpec((bsz, c, R_QK_WIDTH), lambda i, s: (0, i, COL_KR // R_QK_WIDTH)),
                  pl.BlockSpec((bsz, c, R_V_WIDTH), lambda i, s: (0, i, COL_VR // R_V_WIDTH)),
                  pl.BlockSpec((bsz, c, R_V_WIDTH), lambda i, s: (0, i, COL_GR // R_V_WIDTH)),
                  pl.BlockSpec((R_HEADS, c, c), lambda i, s: (0, 0, 0)),
                  pl.BlockSpec((R_HEADS, c, 1), lambda i, s: (0, 0, 0)),
                  pl.BlockSpec((R_HEADS, 1, c), lambda i, s: (0, 0, 0))],
        out_specs=pl.BlockSpec((bsz, c, R_V_WIDTH), lambda i, s: (0, i, 0)),
        scratch_shapes=[pltpu.VMEM((bsz, R_QK_WIDTH, R_V_DIM), F32)],
    )
    return pl.pallas_call(
        _retention_kernel,
        grid_spec=grid_spec,
        out_shape=jax.ShapeDtypeStruct((bsz, seq, R_V_WIDTH), BF16),
        compiler_params=_cparams(("arbitrary",)),
        name="retention",
    )(cdec, proj3, proj3, proj3, proj3, decay, qdec, kdec)


def _pack_halves(x):
    w = x.shape[1] // 2
    bits = lax.bitcast_convert_type(x.astype(BF16).astype(F32), jnp.uint32)
    return (bits[:, :w] >> 16) | (bits[:, w:] & jnp.uint32(0xFFFF0000))


def _unpack_halves(p):
    lo = lax.bitcast_convert_type(p << 16, F32)
    hi = lax.bitcast_convert_type(p & jnp.uint32(0xFFFF0000), F32)
    return lo, hi


SUBLANES = 8


ROUTER_ROWS = 512


def _mix_kernel(ya_ref, yr_ref, ga_ref, gt_ref, x_ref, wpa_ref, wpr_ref, wout_ref,
                gt1_ref, g_ref, sc_ref, sh_ref, wr_ref, rb_ref,
                x1_ref, h2_ref, h2p_ref, e_ref, w_ref, r_ref, c_ref):
    a = _dot(ya_ref[...], wpa_ref[...]) * _sigmoid(ga_ref[...].astype(F32))
    r = _dot(yr_ref[...], wpr_ref[...]) * _sigmoid(gt_ref[...].astype(F32))
    mix = _dot((a + r).astype(BF16), wout_ref[...])
    x1 = x_ref[...] + gt1_ref[0] * mix
    x1_ref[...] = x1
    ms = jnp.mean(x1 * x1, axis=-1, keepdims=True)
    y = x1 * lax.rsqrt(ms + NORM_EPS) * g_ref[...]
    h2 = y * (1.0 + sc_ref[0]) + sh_ref[0]
    h2_b = h2.astype(BF16)
    h2_ref[...] = h2_b
    h2p_ref[...] = _pack_halves(h2)
    _route(h2_b, wr_ref, rb_ref, e_ref, w_ref, r_ref, c_ref)


def _mix(ya, yr, proj, x2d, wpa, wpr, wout, gt1, g, sc, sh, wr_t, bias_col, seq):
    t, d = x2d.shape
    tm = min(ROUTER_ROWS, seq)
    per_b = seq // tm
    row = lambda i: (i, 0)
    full = lambda i: (0, 0)
    by_tile = lambda i: (0, i)
    per_batch = lambda i: (i // per_b, 0, 0)
    return pl.pallas_call(
        _mix_kernel,
        grid=(t // tm,),
        in_specs=[pl.BlockSpec((tm, A_WIDTH), row),
                  pl.BlockSpec((tm, R_V_WIDTH), row),
                  pl.BlockSpec((tm, d), lambda i: (i, COL_GA // D_MODEL)),
                  pl.BlockSpec((tm, d), lambda i: (i, COL_GT // D_MODEL)),
                  pl.BlockSpec((tm, d), row),
                  pl.BlockSpec((A_WIDTH, d), full),
                  pl.BlockSpec((R_V_WIDTH, d), full),
                  pl.BlockSpec((d, d), full),
                  pl.BlockSpec((1, 1, d), per_batch),
                  pl.BlockSpec((1, d), full),
                  pl.BlockSpec((1, 1, d), per_batch),
                  pl.BlockSpec((1, 1, d), per_batch),
                  pl.BlockSpec((N_EXPERTS, d), full),
                  pl.BlockSpec((N_EXPERTS, 1), full)],
        out_specs=[pl.BlockSpec((tm, d), row), pl.BlockSpec((tm, d), row),
                   pl.BlockSpec((tm, d // 2), row),
                   pl.BlockSpec((TOP_K, tm), by_tile), pl.BlockSpec((TOP_K, tm), by_tile),
                   pl.BlockSpec((TOP_K, tm), by_tile), pl.BlockSpec((N_EXPERTS, LANES), by_tile)],
        out_shape=[jax.ShapeDtypeStruct((t, d), F32), jax.ShapeDtypeStruct((t, d), BF16),
                   jax.ShapeDtypeStruct((t, d // 2), jnp.uint32),
                   jax.ShapeDtypeStruct((TOP_K, t), jnp.int32),
                   jax.ShapeDtypeStruct((TOP_K, t), F32),
                   jax.ShapeDtypeStruct((TOP_K, t), jnp.int32),
                   jax.ShapeDtypeStruct((N_EXPERTS, (t // tm) * LANES), F32)],
        compiler_params=_cparams(("parallel",)),
        name="merge_norm_route",
    )(ya, yr, proj, proj, x2d, wpa, wpr, wout, gt1, g, sc, sh, wr_t, bias_col)


def _route(h, wr_ref, b_ref, e_ref, w_ref, r_ref, c_ref):
    logits = _dot_nt(wr_ref[...], h)
    scores = _sigmoid(logits)
    choice = scores + b_ref[...]
    tm = logits.shape[1]
    giota = lax.broadcasted_iota(jnp.int32, (GROUP_SIZE, tm), 0)
    gs_rows = []
    for g in range(N_GROUPS):
        cg = choice[g * GROUP_SIZE:(g + 1) * GROUP_SIZE, :]
        m1 = jnp.max(cg, axis=0, keepdims=True)
        i1 = jnp.min(jnp.where(cg == m1, giota, GROUP_SIZE), axis=0, keepdims=True)
        m2 = jnp.max(jnp.where(giota == i1, KNOCKED_OUT, cg), axis=0, keepdims=True)
        gs_rows.append(m1 + m2)
    gs = jnp.concatenate(gs_rows, axis=0)
    grow = lax.broadcasted_iota(jnp.int32, (N_GROUPS, tm), 0)
    gmask = jnp.zeros((N_GROUPS, tm), jnp.bool_)
    for _ in range(TOPK_GROUPS):
        mx = jnp.max(gs, axis=0, keepdims=True)
        ix = jnp.min(jnp.where(gs == mx, grow, N_GROUPS), axis=0, keepdims=True)
        hit = grow == ix
        gmask = jnp.logical_or(gmask, hit)
        gs = jnp.where(hit, KNOCKED_OUT, gs)
    gmask_f = jnp.where(gmask, 1.0, 0.0)
    masked = jnp.concatenate(
        [jnp.where(gmask_f[g:g + 1, :] > 0.5, choice[g * GROUP_SIZE:(g + 1) * GROUP_SIZE, :], NEG_INF)
         for g in range(N_GROUPS)], axis=0)
    erow = lax.broadcasted_iota(jnp.int32, (N_EXPERTS, tm), 0)
    idx_rows, w_rows = [], []
    chosen = jnp.zeros((N_EXPERTS, tm), F32)
    for _ in range(TOP_K):
        mx = jnp.max(masked, axis=0, keepdims=True)
        ix = jnp.min(jnp.where(masked == mx, erow, N_EXPERTS), axis=0, keepdims=True)
        hit = erow == ix
        w_rows.append(jnp.sum(jnp.where(hit, scores, 0.0), axis=0, keepdims=True))
        idx_rows.append(ix)
        chosen = jnp.where(hit, 1.0, chosen)
        masked = jnp.where(hit, KNOCKED_OUT, masked)
    w = jnp.concatenate(w_rows, axis=0)
    w = w / (jnp.sum(w, axis=0, keepdims=True) + 1e-20) * ROUTED_SCALE
    e_ref[...] = jnp.concatenate(idx_rows, axis=0)
    w_ref[...] = w
    chosen_b = chosen.astype(BF16)
    earlier = (lax.broadcasted_iota(jnp.int32, (tm, tm), 0)
               < lax.broadcasted_iota(jnp.int32, (tm, tm), 1)).astype(BF16)
    before = _dot(chosen_b, earlier)
    ranks = [jnp.sum(jnp.where(erow == ix, before, 0.0), axis=0, keepdims=True) for ix in idx_rows]
    r_ref[...] = jnp.concatenate(ranks, axis=0).astype(jnp.int32)
    c_ref[...] = _dot(chosen_b, jnp.ones((tm, LANES), BF16))


def _pos_kernel(e_ref, r_ref, base_ref, p_ref):
    tm = e_ref.shape[1]
    erow = lax.broadcasted_iota(jnp.int32, (N_EXPERTS, tm), 0)
    base = base_ref[0]
    rows = [jnp.sum(jnp.where(erow == e_ref[k:k + 1, :], base, 0.0), axis=0, keepdims=True)
            for k in range(TOP_K)]
    p_ref[0] = jnp.concatenate(rows, axis=0).astype(jnp.int32) + r_ref[...]


MOVE_ROWS = 512


def _positions(eidx_t, rank_t, tile_base):
    t = eidx_t.shape[1]
    tm = min(MOVE_ROWS, t)
    per_router_tile = ROUTER_ROWS // tm
    return pl.pallas_call(
        _pos_kernel,
        grid=(t // tm,),
        in_specs=[pl.BlockSpec((TOP_K, tm), lambda i: (0, i)),
                  pl.BlockSpec((TOP_K, tm), lambda i: (0, i)),
                  pl.BlockSpec((1, N_EXPERTS, 1), lambda i: (i // per_router_tile, 0, 0))],
        out_specs=pl.BlockSpec((1, TOP_K, tm), lambda i: (i, 0, 0)),
        out_shape=jax.ShapeDtypeStruct((t // tm, TOP_K, tm), jnp.int32),
        compiler_params=_cparams(("parallel",)),
        name="slot_positions",
    )(eidx_t, rank_t, tile_base)


SLOT_ROWS = 512
PAD_CHUNKS = (256, 128, 64, 32, 16, 8)


def _zero_pads_kernel(pad_start_ref, pad_len_ref, xs_in, xs_hbm, zero_buf, pad_sem):
    del xs_in
    zero_buf[...] = jnp.zeros_like(zero_buf)

    def pad_copies(e, wait):
        start = pad_start_ref[e]
        n = pad_len_ref[e]
        head = jnp.minimum((-start) & (SUBLANES - 1), n)

        def fill(first, size, pred):
            @pl.when(pred)
            def _():
                cp = pltpu.make_async_copy(zero_buf.at[pl.ds(0, size), :], xs_hbm.at[pl.ds(first, size), :], pad_sem)
                if wait:
                    cp.wait()
                else:
                    cp.start()

        for j in range(SUBLANES - 1):
            fill(start + j, 1, j < head)
        ptr = start + head
        rest = n - head
        for chunk in PAD_CHUNKS:
            fill(pl.multiple_of(ptr, SUBLANES), chunk, (rest & chunk) != 0)
            ptr = ptr + (rest & chunk)

    def issue(e, carry):
        pad_copies(e, False)
        return carry

    def drain(e, carry):
        pad_copies(e, True)
        return carry

    lax.fori_loop(0, N_EXPERTS, issue, 0)
    lax.fori_loop(0, N_EXPERTS, drain, 0)


def _zero_pads(pad_start, pad_len, xs):
    grid_spec = pltpu.PrefetchScalarGridSpec(
        num_scalar_prefetch=2,
        grid=(1,),
        in_specs=[pl.BlockSpec(memory_space=pl.ANY)],
        out_specs=pl.BlockSpec(memory_space=pl.ANY),
        scratch_shapes=[pltpu.VMEM((PAD_CHUNKS[0], xs.shape[1]), xs.dtype),
                        pltpu.SemaphoreType.DMA],
    )
    return pl.pallas_call(
        _zero_pads_kernel,
        grid_spec=grid_spec,
        out_shape=jax.ShapeDtypeStruct(xs.shape, xs.dtype),
        input_output_aliases={2: 0},
        compiler_params=_cparams(("arbitrary",)),
        name="zero_pad_slots",
    )(pad_start, pad_len, xs)


SC_DISPATCH_TOKENS = 64


def _sc_dispatch(pos_blocks, h_rows, n_rows):
    info = plsc.get_sparse_core_info()
    n_cores = info.num_cores
    n_workers = n_cores * info.num_subcores
    t = h_rows.shape[0]
    chunk = SC_DISPATCH_TOKENS
    steps = t // (n_workers * chunk)
    assert steps * n_workers * chunk == t
    mesh = plsc.VectorSubcoreMesh(core_axis_name="c", subcore_axis_name="s")

    @functools.partial(
        pl.kernel, mesh=mesh,
        out_type=jax.ShapeDtypeStruct((n_rows,) + h_rows.shape[1:], h_rows.dtype),
        scratch_types=[pltpu.VMEM((TOP_K, chunk), jnp.int32),
                       pltpu.VMEM((chunk,) + h_rows.shape[1:], h_rows.dtype),
                       pltpu.SemaphoreType.DMA],
        name="sc_dispatch_rows",
    )
    def scatter_rows(pos_hbm, h_hbm, out_hbm, idx_v, rows_v, sem):
        wid = lax.axis_index("s") * n_cores + lax.axis_index("c")

        @pl.loop(0, steps)
        def _(step):
            blk = wid * steps + step
            pltpu.sync_copy(pos_hbm.at[blk], idx_v)
            pltpu.sync_copy(h_hbm.at[pl.ds(blk * chunk, chunk)], rows_v)
            scatters = [pltpu.make_async_copy(rows_v, out_hbm.at[idx_v.at[k]], sem) for k in range(TOP_K)]
            for cp in scatters:
                cp.start()
            for cp in scatters:
                cp.wait()

    return scatter_rows(pos_blocks, h_rows)


WEIGHT_SLOTS = 4


def _experts_kernel(blk_e_ref, nblk_ref, ord_ref, eid_ref, nord_ref, x_ref, w1_hbm, w3_hbm, w2_hbm, y_ref,
                    w1f, w3f, w2f, wsem):
    s = pl.program_id(0)

    def fetch(j):
        slot = j % WEIGHT_SLOTS
        e = eid_ref[j]
        return [pltpu.make_async_copy(src.at[e], dst.at[slot], wsem.at[slot])
                for src, dst in ((w1_hbm, w1f), (w3_hbm, w3f), (w2_hbm, w2f))]

    @pl.when(s < nblk_ref[0])
    def _():
        j = ord_ref[s]

        ahead = WEIGHT_SLOTS - 1

        @pl.when(s == 0)
        def _():
            for first in range(ahead):
                @pl.when(first < nord_ref[0])
                def _(first=first):
                    for cp in fetch(first):
                        cp.start()

        @pl.when(jnp.logical_or(s == 0, blk_e_ref[s] != blk_e_ref[jnp.maximum(s - 1, 0)]))
        def _():
            for cp in fetch(j):
                cp.wait()

            @pl.when(j + ahead < nord_ref[0])
            def _():
                for cp in fetch(j + ahead):
                    cp.start()

        slot = j % WEIGHT_SLOTS
        half = x_ref.shape[1]
        lo, hi = _unpack_halves(x_ref[...])
        lo = lo.astype(BF16)
        hi = hi.astype(BF16)
        h1 = (_dot(lo, w1f[slot, :half, :].astype(BF16)) + _dot(hi, w1f[slot, half:, :].astype(BF16)))
        h3 = (_dot(lo, w3f[slot, :half, :].astype(BF16)) + _dot(hi, w3f[slot, half:, :].astype(BF16)))
        mid = (_silu(h1) * h3).astype(BF16)
        y_ref[...] = _pack_halves(_dot(mid, w2f[slot].astype(BF16)))


def _sc_gather(pos_blocks, ys, t):
    info = plsc.get_sparse_core_info()
    n_cores = info.num_cores
    n_workers = n_cores * info.num_subcores
    chunk = SC_DISPATCH_TOKENS
    steps = t // (n_workers * chunk)
    assert steps * n_workers * chunk == t
    mesh = plsc.VectorSubcoreMesh(core_axis_name="c", subcore_axis_name="s")

    @functools.partial(
        pl.kernel, mesh=mesh,
        out_type=jax.ShapeDtypeStruct((TOP_K * t,) + ys.shape[1:], ys.dtype),
        scratch_types=[pltpu.VMEM((TOP_K, chunk), jnp.int32),
                       pltpu.VMEM((chunk,) + ys.shape[1:], ys.dtype),
                       pltpu.VMEM((chunk,) + ys.shape[1:], ys.dtype),
                       pltpu.SemaphoreType.DMA((2,))],
        name="sc_gather_rows",
    )
    def gather_rows(pos_hbm, ys_hbm, out_hbm, idx_v, rows_a, rows_b, sems):
        wid = lax.axis_index("s") * n_cores + lax.axis_index("c")
        bufs = (rows_a, rows_b)

        @pl.loop(0, steps)
        def _(step):
            blk = wid * steps + step
            pltpu.sync_copy(pos_hbm.at[blk], idx_v)
            gathers = [pltpu.make_async_copy(ys_hbm.at[idx_v.at[k]], bufs[k % 2], sems.at[k % 2])
                       for k in range(TOP_K)]
            gathers[0].start()
            for k in range(TOP_K):
                gathers[k].wait()
                if k + 1 < TOP_K:
                    gathers[k + 1].start()
                pltpu.sync_copy(bufs[k % 2], out_hbm.at[pl.ds(k * t + blk * chunk, chunk)])

    return gather_rows(pos_blocks, ys)


def _experts(blk_e, nblk_used, blk_ord, eid_of_ord, n_ord, xs, w1, w3, w2):
    n_rows, half = xs.shape
    d = D_MODEL
    blk = lambda s, be, nb, bo, eo, no: (jnp.minimum(s, nb[0] - 1), 0)
    grid_spec = pltpu.PrefetchScalarGridSpec(
        num_scalar_prefetch=5,
        grid=(n_rows // SLOT_ROWS,),
        in_specs=[pl.BlockSpec((SLOT_ROWS, half), blk),
                  pl.BlockSpec(memory_space=pl.ANY),
                  pl.BlockSpec(memory_space=pl.ANY),
                  pl.BlockSpec(memory_space=pl.ANY)],
        out_specs=pl.BlockSpec((SLOT_ROWS, half), blk),
        scratch_shapes=[pltpu.VMEM((WEIGHT_SLOTS, d, EXPERT_FF), F32),
                        pltpu.VMEM((WEIGHT_SLOTS, d, EXPERT_FF), F32),
                        pltpu.VMEM((WEIGHT_SLOTS, EXPERT_FF, d), F32),
                        pltpu.SemaphoreType.DMA((WEIGHT_SLOTS,))],
    )
    return pl.pallas_call(
        _experts_kernel,
        grid_spec=grid_spec,
        out_shape=jax.ShapeDtypeStruct((n_rows, half), jnp.uint32),
        compiler_params=_cparams(("arbitrary",)),
        name="routed_experts",
    )(blk_e, nblk_used, blk_ord, eid_of_ord, n_ord, xs, w1, w3, w2)


def _combine_kernel(*refs):
    y_refs = refs[:TOP_K]
    w_ref, h_ref, x1_ref, ws1_ref, ws3_ref, ws2_ref, gt2_ref, g_ref = refs[TOP_K:TOP_K + 8]
    o_ref = refs[-1]
    tm, d = x1_ref.shape
    half = d // 2
    h = h_ref[...]
    mid = (_silu(_dot(h, ws1_ref[...])) * _dot(h, ws3_ref[...])).astype(BF16)
    shared = _dot(mid, ws2_ref[...])
    w = w_ref[...]
    acc_lo = jnp.zeros((tm, half), F32)
    acc_hi = jnp.zeros((tm, half), F32)
    for k in range(TOP_K):
        lo, hi = _unpack_halves(y_refs[k][...])
        acc_lo = acc_lo + lo * w[:, k:k + 1]
        acc_hi = acc_hi + hi * w[:, k:k + 1]
    routed = jnp.concatenate([acc_lo, acc_hi], axis=1)
    x2 = x1_ref[...] + gt2_ref[0] * (routed + shared)
    ms = jnp.mean(x2 * x2, axis=-1, keepdims=True)
    o_ref[...] = x2 * lax.rsqrt(ms + NORM_EPS) * g_ref[...]


COMBINE_ROWS = 256
COMBINE_PARTS = 4


def _combine(y_kt, wts, h2, x1, ws1, ws3, ws2, gt2, g_final, seq, part, n_parts, prev_out):
    t, d = x1.shape
    tm = min(COMBINE_ROWS, seq)
    per_b = seq // tm
    tiles = t // tm // n_parts
    first = part * tiles
    row = lambda i: (first + i, 0)
    full = lambda i: (0, 0)
    y_specs = [pl.BlockSpec((tm, d // 2), functools.partial(lambda i, k: (k * tiles + i, 0), k=k))
               for k in range(TOP_K)]
    in_specs = y_specs + [
        pl.BlockSpec((tm, TOP_K), row),
        pl.BlockSpec((tm, d), row),
        pl.BlockSpec((tm, d), row),
        pl.BlockSpec((d, SHARED_FF), full),
        pl.BlockSpec((d, SHARED_FF), full),
        pl.BlockSpec((SHARED_FF, d), full),
        pl.BlockSpec((1, 1, d), lambda i: ((first + i) // per_b, 0, 0)),
        pl.BlockSpec((1, d), full)]
    args = [y_kt] * TOP_K + [wts, h2, x1, ws1, ws3, ws2, gt2, g_final]
    aliases = {}
    if prev_out is not None:
        in_specs.append(pl.BlockSpec(memory_space=pl.ANY))
        aliases = {len(args): 0}
        args.append(prev_out)
    return pl.pallas_call(
        _combine_kernel,
        grid=(tiles,),
        in_specs=in_specs,
        out_specs=pl.BlockSpec((tm, d), row),
        out_shape=jax.ShapeDtypeStruct((t, d), F32),
        input_output_aliases=aliases,
        compiler_params=_cparams(("parallel",)),
        name="combine_shared_final",
    )(*args)


def _slot_tables(cnt, t):
    ntiles = cnt.shape[1] // LANES
    cnt_tile = cnt.reshape(N_EXPERTS, ntiles, LANES)[:, :, 0].astype(jnp.int32)
    counts = jnp.sum(cnt_tile, axis=1)
    padded = (counts + SLOT_ROWS - 1) // SLOT_ROWS * SLOT_ROWS
    pstart = jnp.cumsum(padded) - padded
    tile_base = pstart[:, None] + jnp.cumsum(cnt_tile, axis=1) - cnt_tile
    n_blk = -(-(t * TOP_K) // SLOT_ROWS) + N_EXPERTS
    blk_end = jnp.cumsum(padded // SLOT_ROWS)
    blk_e = jnp.sum((blk_end[None, :] <= jnp.arange(n_blk)[:, None]).astype(jnp.int32), axis=1)
    blk_e = jnp.minimum(blk_e, N_EXPERTS - 1)
    owns = (padded > 0).astype(jnp.int32)
    ord_of_e = jnp.cumsum(owns) - owns
    ids = jnp.arange(N_EXPERTS, dtype=jnp.int32)
    eid_of_ord = jnp.sum(jnp.where((ord_of_e[None, :] == ids[:, None]) & (owns[None, :] > 0), ids[None, :], 0), axis=1)
    blk_ord = jnp.sum(jnp.where(blk_e[:, None] == ids[None, :], ord_of_e[None, :], 0), axis=1)
    experts_tables = (blk_e, blk_end[-1:].astype(jnp.int32), blk_ord.astype(jnp.int32),
                      eid_of_ord.astype(jnp.int32), jnp.sum(owns).reshape(1).astype(jnp.int32))
    return (experts_tables, pstart + counts, padded - counts,
            tile_base.T.astype(F32).reshape(ntiles, N_EXPERTS, 1), n_blk * SLOT_ROWS)


def _permute_in_cols(w_in):
    qa, ka, va, qr, kr, vr, gr, ga, gt = jnp.split(
        w_in, np.cumsum((A_WIDTH, A_WIDTH, A_WIDTH, R_QK_WIDTH, R_QK_WIDTH, R_V_WIDTH, R_V_WIDTH,
                         D_MODEL))[:].tolist(), axis=1)
    return jnp.concatenate([vr, gr, ga, gt, qa, ka, va, qr, kr], axis=1)


def kernel(x, c, w_ada, b_ada, g_mix, w_in, w_pa, w_pr, w_out, g_ffn, w_router, router_bias,
           w1, w3, w2, ws1, ws3, ws2, g_final):
    bsz, seq, d = x.shape
    t = bsz * seq
    depth = w_ada.shape[0]
    assert depth == 1, "the final norm is fused into the single layer's last kernel"
    rest = jnp.exp2(-8.0 / A_HEADS * jnp.arange(1, A_HEADS + 1, dtype=F32)) * LOG2_E
    pieces = []
    for _ in range(ALIBI_PIECES):
        pieces.append(rest.astype(BF16).astype(F32))
        rest = rest - pieces[-1]
    slopes = jnp.stack(pieces, axis=1).reshape(-1)
    x2d = x.reshape(t, d)
    for l in range(depth):
        mod = _ada(c, w_ada[l], b_ada[l])
        sh1, sc1, gt1, sh2, sc2, gt2 = [m.reshape(bsz, 1, d) for m in jnp.split(mod, 6, axis=-1)]
        w_in_p = _permute_in_cols(w_in[l]).astype(BF16)
        proj = _inproj(x2d, g_mix[l].reshape(1, d), sc1, sh1, w_in_p, seq)
        proj3 = proj.reshape(bsz, seq, IN_COLS)
        ya = _moba(proj3, slopes).reshape(t, A_WIDTH)
        yr = _retention(proj3).reshape(t, R_V_WIDTH)
        x1, h2, h2p, eidx_t, wts_t, rank_t, cnt = _mix(
            ya, yr, proj, x2d, w_pa[l].astype(BF16), w_pr[l].astype(BF16), w_out[l].astype(BF16), gt1,
            g_ffn[l].reshape(1, d), sc2, sh2, w_router[l].T.astype(BF16), router_bias[l].reshape(N_EXPERTS, 1), seq)
        experts_tables, pad_start, pad_len, tile_base, n_rows = _slot_tables(cnt, t)
        pos3 = _positions(eidx_t, rank_t, tile_base)
        pos_blocks = jnp.transpose(
            pos3.reshape(pos3.shape[0], TOP_K, -1, SC_DISPATCH_TOKENS), (0, 2, 1, 3)
        ).reshape(t // SC_DISPATCH_TOKENS, TOP_K, SC_DISPATCH_TOKENS)
        xs = _zero_pads(pad_start, pad_len, _sc_dispatch(pos_blocks, h2p, n_rows))
        ys = _experts(*experts_tables, xs, w1[l], w3[l], w2[l])
        wts = wts_t.T
        shared_w = (ws1[l].astype(BF16), ws3[l].astype(BF16), ws2[l].astype(BF16))
        blocks_per_part = pos_blocks.shape[0] // COMBINE_PARTS
        x2d = None
        for part in range(COMBINE_PARTS):
            y_kt = _sc_gather(pos_blocks[part * blocks_per_part:(part + 1) * blocks_per_part], ys,
                              t // COMBINE_PARTS)
            x2d = _combine(y_kt, wts, h2, x1, *shared_w, gt2, g_final.reshape(1, d), seq,
                           part, COMBINE_PARTS, x2d)
    return x2d.reshape(bsz, seq, d)
```

```python
import functools

import jax
import jax.numpy as jnp
import numpy as np
from jax import lax
from jax.experimental import pallas as pl
from jax.experimental.pallas import tpu as pltpu
from jax.experimental.pallas import tpu_sc as plsc

F32 = jnp.float32
BF16 = jnp.bfloat16

D_MODEL = 1024
A_HEADS = 8
A_HEAD_DIM = 64
A_WIDTH = A_HEADS * A_HEAD_DIM
MOBA_BLOCK = 256
MOBA_TOPK = 3
R_HEADS = 8
R_QK_DIM = 64
R_V_DIM = 128
R_QK_WIDTH = R_HEADS * R_QK_DIM
R_V_WIDTH = R_HEADS * R_V_DIM
R_CHUNK = 256
N_EXPERTS = 256
TOP_K = 8
N_GROUPS = 8
GROUP_SIZE = N_EXPERTS // N_GROUPS
TOPK_GROUPS = 4
EXPERT_FF = 256
SHARED_FF = 256
ROUTED_SCALE = 2.5
NORM_EPS = 1e-6
GN_EPS = 1e-6
NEG_INF = -1e30
KNOCKED_OUT = -3e38

COL_VR, COL_GR, COL_GA, COL_GT = 0, 1024, 2048, 3072
COL_QA, COL_KA, COL_VA, COL_QR, COL_KR = 4096, 4608, 5120, 5632, 6144
IN_COLS = 6656
AUG = 128
FEAT_BIAS = A_HEAD_DIM
FEAT_POS = A_HEAD_DIM + 32
ALIBI_PIECES = 3
LOG2_E = 1.4426950408889634
LANES = 128
BF16_ROWS = 16
V_ROWS = A_HEAD_DIM + BF16_ROWS
MOBA_HEADS_PER_STEP = 8

VMEM_LIMIT = 56 * 1024 * 1024


def _cparams(sem, vmem=VMEM_LIMIT):
    return pltpu.CompilerParams(dimension_semantics=sem, vmem_limit_bytes=vmem)


def _dot(a, b):
    return jnp.dot(a, b, preferred_element_type=F32)


def _dot_nt(a, b):
    return lax.dot_general(a, b, (((1,), (1,)), ((), ())), preferred_element_type=F32)


def _sigmoid(x):
    return 1.0 / (1.0 + jnp.exp(-x))


def _silu(x):
    return x * _sigmoid(x)


def _ada_kernel(c_ref, w_ref, b_ref, o_ref):
    c = c_ref[...]
    s = _silu(c)
    s_hi = s.astype(BF16)
    s_lo = (s - s_hi.astype(F32)).astype(BF16)
    w = w_ref[...]
    w_hi = w.astype(BF16)
    w_lo = (w - w_hi.astype(F32)).astype(BF16)
    o_ref[...] = _dot(s_hi, w_hi) + _dot(s_hi, w_lo) + _dot(s_lo, w_hi) + b_ref[...]


def _ada(c, w_ada, b_ada):
    bsz, d = c.shape
    n = w_ada.shape[1]
    tn = 1024
    return pl.pallas_call(
        _ada_kernel,
        grid=(n // tn,),
        in_specs=[pl.BlockSpec((bsz, d), lambda j: (0, 0)),
                  pl.BlockSpec((d, tn), lambda j: (0, j)),
                  pl.BlockSpec((1, tn), lambda j: (0, j))],
        out_specs=pl.BlockSpec((bsz, tn), lambda j: (0, j)),
        out_shape=jax.ShapeDtypeStruct((bsz, n), F32),
        compiler_params=_cparams(("parallel",)),
        name="ada_mod",
    )(c, w_ada, b_ada.reshape(1, n))


INPROJ_COLS = 512


def _inproj_kernel(x_ref, g_ref, sc_ref, sh_ref, w_ref, o_ref):
    x = x_ref[...]
    ms = jnp.mean(x * x, axis=-1, keepdims=True)
    y = x * lax.rsqrt(ms + NORM_EPS) * g_ref[...]
    h = (y * (1.0 + sc_ref[0]) + sh_ref[0]).astype(BF16)
    for j in range(w_ref.shape[1] // INPROJ_COLS):
        cols = slice(j * INPROJ_COLS, (j + 1) * INPROJ_COLS)
        o_ref[:, cols] = _dot(h, w_ref[:, cols]).astype(BF16)


def _inproj(x2d, g, sc, sh, w_bf16, seq):
    t, d = x2d.shape
    n = w_bf16.shape[1]
    tm = min(512, seq)
    per_b = seq // tm
    return pl.pallas_call(
        _inproj_kernel,
        grid=(t // tm,),
        in_specs=[pl.BlockSpec((tm, d), lambda i: (i, 0)),
                  pl.BlockSpec((1, d), lambda i: (0, 0)),
                  pl.BlockSpec((1, 1, d), lambda i: (i // per_b, 0, 0)),
                  pl.BlockSpec((1, 1, d), lambda i: (i // per_b, 0, 0)),
                  pl.BlockSpec((d, n), lambda i: (0, 0))],
        out_specs=pl.BlockSpec((tm, n), lambda i: (i, 0)),
        out_shape=jax.ShapeDtypeStruct((t, n), BF16),
        compiler_params=_cparams(("parallel",)),
        name="norm_inproj",
    )(x2d, g, sc, sh, w_bf16)


def _moba_prepare(i, slopes_ref, q_ref, k_ref, v_ref, ko_ref, qo_ref, vo_ref, kmean_scr):
    nblk = kmean_scr.shape[0]
    width = q_ref.shape[1]
    seq_rows = pl.ds(pl.multiple_of(i * MOBA_BLOCK, MOBA_BLOCK), MOBA_BLOCK)
    q = q_ref[...]
    k = k_ref[...]
    v = v_ref[...]
    kmean_scr[pl.ds(i, 1), :] = jnp.mean(k.astype(F32), axis=0, keepdims=True)

    eye = (lax.broadcasted_iota(jnp.int32, (A_HEAD_DIM, A_HEAD_DIM), 0)
           == lax.broadcasted_iota(jnp.int32, (A_HEAD_DIM, A_HEAD_DIM), 1)).astype(BF16)
    heads = [slice(h * A_HEAD_DIM, (h + 1) * A_HEAD_DIM) for h in range(width // A_HEAD_DIM)]
    q_t = jnp.concatenate([_dot_nt(eye, q[:, hd]) for hd in heads], axis=0)
    v_t = jnp.concatenate([_dot_nt(eye, v[:, hd]) for hd in heads], axis=0)

    km = kmean_scr[...]
    km_rep = jnp.concatenate([km] * A_HEADS, axis=0)
    r_head = lax.broadcasted_iota(jnp.int32, km_rep.shape, 0) // nblk
    c_head = lax.broadcasted_iota(jnp.int32, km_rep.shape, 1) // A_HEAD_DIM
    km_bd = jnp.where(r_head == c_head, km_rep, 0.0)
    km_hi = km_bd.astype(BF16)
    km_lo = (km_bd - km_hi.astype(F32)).astype(BF16)
    q_t_b = q_t.astype(BF16)
    gate_all = _dot(km_hi, q_t_b) + _dot(km_lo, q_t_b)

    mb = q.shape[0]
    blk = lax.broadcasted_iota(jnp.int32, (nblk, mb), 0)
    lane_pos = lax.broadcasted_iota(jnp.int32, (BF16_ROWS, mb), 1).astype(F32)
    row16 = lax.broadcasted_iota(jnp.int32, (BF16_ROWS, mb), 0)
    key_pos = lax.broadcasted_iota(jnp.int32, (mb, AUG), 0).astype(F32)
    kcol = lax.broadcasted_iota(jnp.int32, (mb, AUG), 1)
    widen = (lax.broadcasted_iota(jnp.int32, (A_HEAD_DIM, AUG), 0)
             == lax.broadcasted_iota(jnp.int32, (A_HEAD_DIM, AUG), 1)).astype(BF16)

    blk_first = (i * mb).astype(F32)
    for h in range(A_HEADS):
        pieces = [slopes_ref[h * ALIBI_PIECES + c] for c in range(ALIBI_PIECES)]
        g = jnp.where(blk < i, gate_all[h * nblk:(h + 1) * nblk, :], NEG_INF)
        sel = jnp.zeros((nblk, mb), jnp.bool_)
        for r in range(MOBA_TOPK):
            m = jnp.max(g, axis=0, keepdims=True)
            idx = jnp.min(jnp.where(g == m, blk, nblk), axis=0, keepdims=True)
            hit = blk == idx
            sel = jnp.logical_or(sel, jnp.logical_and(hit, r < i))
            g = jnp.where(hit, KNOCKED_OUT, g)
        bias_t = jnp.where(sel, 0.0, NEG_INF)

        scale = A_HEAD_DIM ** -0.5 * LOG2_E
        qo_ref[0, h, 0:A_HEAD_DIM, :] = (q_t[h * A_HEAD_DIM:(h + 1) * A_HEAD_DIM, :] * scale).astype(BF16)
        qo_ref[0, h, FEAT_BIAS:FEAT_BIAS + nblk, :] = bias_t.astype(BF16)
        if nblk < 32:
            qo_ref[0, h, FEAT_BIAS + nblk:FEAT_POS, :] = jnp.zeros((32 - nblk, mb), BF16)
        piece_rows = jnp.where(row16 % ALIBI_PIECES == 0, pieces[0],
                               jnp.where(row16 % ALIBI_PIECES == 1, pieces[1], pieces[2]))
        pos_feat = jnp.where(row16 < 3, -lane_pos,
                             jnp.where(row16 < 6, piece_rows,
                                       jnp.where(row16 < 9, -blk_first, jnp.where(row16 < 12, piece_rows, 0.0))))
        qo_ref[0, h, FEAT_POS:FEAT_POS + BF16_ROWS, :] = pos_feat.astype(BF16)
        qo_ref[0, h, FEAT_POS + BF16_ROWS:AUG, :] = jnp.zeros((AUG - FEAT_POS - BF16_ROWS, mb), BF16)

        vo_ref[0, h, 0:A_HEAD_DIM, seq_rows] = v_t[h * A_HEAD_DIM:(h + 1) * A_HEAD_DIM, :].astype(BF16)
        vo_ref[0, h, A_HEAD_DIM:V_ROWS, seq_rows] = jnp.where(row16 == 0, 1.0, 0.0).astype(BF16)

        pos_col = kcol - FEAT_POS
        piece_cols = jnp.where(pos_col % ALIBI_PIECES == 0, pieces[0],
                               jnp.where(pos_col % ALIBI_PIECES == 1, pieces[1], pieces[2]))
        k_feat = jnp.where(
            kcol == FEAT_BIAS + i, 1.0,
            jnp.where(pos_col < 0, 0.0,
                      jnp.where(pos_col < 3, piece_cols,
                                jnp.where(pos_col < 6, key_pos,
                                          jnp.where(pos_col < 9, piece_cols,
                                                    jnp.where(pos_col < 12, blk_first, 0.0))))))
        k_wide = _dot(k[:, h * A_HEAD_DIM:(h + 1) * A_HEAD_DIM], widen)
        ko_ref[0, h, seq_rows, :] = (k_wide + k_feat).astype(BF16)


def _moba_attend(i, q_ref, k_ref, v_ref, o_ref, s_a, s_b, s_c, group, n_groups):
    mb = MOBA_BLOCK
    span = group * mb
    own = pl.multiple_of(i * mb, mb)
    key_i = lax.broadcasted_iota(jnp.int32, (mb, mb), 0)
    qry_i = lax.broadcasted_iota(jnp.int32, (mb, mb), 1)
    feat = lax.broadcasted_iota(jnp.int32, (AUG, mb), 0)
    is_bias = jnp.logical_and(feat >= FEAT_BIAS, feat < FEAT_POS)
    q_ts, carry0 = [], []
    for hh in range(MOBA_HEADS_PER_STEP):
        q_t = q_ref[0, hh]
        q_ts.append(q_t)
        q_own = jnp.where(is_bias, jnp.zeros_like(q_t), q_t)
        s = _dot(k_ref[0, hh, pl.ds(own, mb), :], q_own)
        s = jnp.where(key_i <= qry_i, s, NEG_INF)
        m0 = jnp.max(s, axis=0, keepdims=True)
        p = jnp.exp2(s - m0)
        carry0 += [m0, _dot(v_ref[0, hh, :, pl.ds(own, mb)], p.astype(BF16))]

    def scores(g, dst):
        start = pl.multiple_of(jnp.minimum(g, n_groups - 1) * span, span)
        for hh in range(MOBA_HEADS_PER_STEP):
            dst[hh] = _dot(k_ref[0, hh, pl.ds(start, span), :], q_ts[hh])

    def absorb(g, src, carry):
        start = pl.multiple_of(g * span, span)
        new = []
        for hh in range(MOBA_HEADS_PER_STEP):
            m, acc = carry[2 * hh], carry[2 * hh + 1]
            sb = src[hh]
            m_new = jnp.maximum(m, jnp.max(sb, axis=0, keepdims=True))
            pb = jnp.exp2(sb - m_new)
            alpha = jnp.exp2(m - m_new)
            acc = acc * alpha + _dot(v_ref[0, hh, :, pl.ds(start, span)], pb.astype(BF16))
            new += [m_new, acc]
        return tuple(new)

    ring = (s_a, s_b, s_c)

    def body(trip, carry):
        g = len(ring) * trip
        for r in range(len(ring)):
            scores(g + r + 1, ring[(r + 1) % len(ring)])
            carry = absorb(g + r, ring[r], carry)
        return carry

    scores(0, s_a)
    live_groups = (i + group - 1) // group
    trips = live_groups // len(ring)
    res = lax.fori_loop(0, trips, body, tuple(carry0))
    done = len(ring) * trips
    res = lax.cond(live_groups > done, lambda c: absorb(done, s_a, c), lambda c: c, res)

    def straggler(g, c):
        scores(g, s_b)
        return absorb(g, s_b, c)

    res = lax.fori_loop(done + 1, live_groups, straggler, res)
    outs = [res[2 * hh + 1][0:A_HEAD_DIM, :] / res[2 * hh + 1][A_HEAD_DIM:A_HEAD_DIM + 1, :] for hh in range(MOBA_HEADS_PER_STEP)]
    o_t = jnp.concatenate(outs, axis=0).astype(BF16)
    eye = (key_i == qry_i).astype(BF16)
    o_ref[0] = _dot_nt(eye, o_t).astype(BF16)


def _moba_kernel(slopes_ref, q_ref, k_ref, v_ref, o_ref, kmean_scr, q_scr, k_scr, v_scr, s_a, s_b, s_c,
                 *, group, n_groups):
    i = pl.program_id(1)

    @pl.when(i == 0)
    def _():
        kmean_scr[...] = jnp.zeros_like(kmean_scr)
        k_scr[...] = jnp.zeros_like(k_scr)
        v_scr[...] = jnp.zeros_like(v_scr)

    _moba_prepare(i, slopes_ref, q_ref, k_ref, v_ref, k_scr, q_scr, v_scr, kmean_scr)
    _moba_attend(i, q_scr, k_scr, v_scr, o_ref, s_a, s_b, s_c, group, n_groups)


def _moba(proj3, slopes):
    bsz, seq, _ = proj3.shape
    mb = MOBA_BLOCK
    nblk = seq // mb
    group = min(2, nblk)
    n_groups = nblk // group
    assert MOBA_HEADS_PER_STEP == A_HEADS
    assert seq % mb == 0 and nblk % group == 0 and nblk <= FEAT_POS - FEAT_BIAS, "one selection feature per key block"
    grid_spec = pltpu.PrefetchScalarGridSpec(
        num_scalar_prefetch=1,
        grid=(bsz, nblk),
        in_specs=[pl.BlockSpec((None, mb, A_WIDTH), lambda b, i, s: (b, i, COL_QA // A_WIDTH)),
                  pl.BlockSpec((None, mb, A_WIDTH), lambda b, i, s: (b, i, COL_KA // A_WIDTH)),
                  pl.BlockSpec((None, mb, A_WIDTH), lambda b, i, s: (b, i, COL_VA // A_WIDTH))],
        out_specs=pl.BlockSpec((1, mb, A_WIDTH), lambda b, i, s: (b, i, 0)),
        scratch_shapes=[pltpu.VMEM((nblk, A_WIDTH), F32),
                        pltpu.VMEM((1, A_HEADS, AUG, mb), BF16),
                        pltpu.VMEM((1, A_HEADS, seq, AUG), BF16),
                        pltpu.VMEM((1, A_HEADS, V_ROWS, seq), BF16),
                        pltpu.VMEM((A_HEADS, group * mb, mb), F32),
                        pltpu.VMEM((A_HEADS, group * mb, mb), F32),
                        pltpu.VMEM((A_HEADS, group * mb, mb), F32)],
    )
    return pl.pallas_call(
        functools.partial(_moba_kernel, group=group, n_groups=n_groups),
        grid_spec=grid_spec,
        out_shape=jax.ShapeDtypeStruct((bsz, seq, A_WIDTH), BF16),
        compiler_params=_cparams(("parallel", "arbitrary")),
        name="moba_attention",
    )(slopes, proj3, proj3, proj3)


def _retention_kernel(cdec_ref, q_ref, k_ref, v_ref, g_ref, decay_ref, qdec_ref, kdec_ref, o_ref, state_scr):
    @pl.when(pl.program_id(0) == 0)
    def _():
        state_scr[...] = jnp.zeros_like(state_scr)

    width = q_ref.shape[2]
    eye = (lax.broadcasted_iota(jnp.int32, (width, width), 0)
           == lax.broadcasted_iota(jnp.int32, (width, width), 1)).astype(BF16)
    for b in range(q_ref.shape[0]):
        q = q_ref[b]
        k_t = _dot_nt(eye, k_ref[b])
        k_t_b = k_t.astype(BF16)
        state_b = state_scr[b].astype(BF16)
        for h in range(R_HEADS):
            rows = slice(h * R_QK_DIM, (h + 1) * R_QK_DIM)
            cols = slice(h * R_V_DIM, (h + 1) * R_V_DIM)
            q_h = q[:, rows]
            v_h = v_ref[b, :, cols]
            inner = _dot(q_h, k_t_b[rows, :]) * decay_ref[h]
            out = _dot(inner.astype(BF16), v_h) + _dot(q_h, state_b[rows, :]) * qdec_ref[h]
            k_dec = (k_t[rows, :] * kdec_ref[h]).astype(BF16)
            state_scr[b, rows, :] = cdec_ref[h] * state_scr[b, rows, :] + _dot(k_dec, v_h)
            mu = jnp.mean(out, axis=-1, keepdims=True)
            cen = out - mu
            var = jnp.mean(cen * cen, axis=-1, keepdims=True)
            y = cen * lax.rsqrt(var + GN_EPS)
            o_ref[b, :, cols] = (y * _silu(g_ref[b, :, cols].astype(F32))).astype(BF16)


def _retention_consts():
    h = np.arange(R_HEADS, dtype=np.float64)
    log_g = np.log(1.0 - np.exp2(-5.0 - h))
    n = np.arange(R_CHUNK, dtype=np.float64)
    diff = n[:, None] - n[None, :]
    scale = R_QK_DIM ** -0.5
    decay = np.where(diff >= 0, np.exp(np.maximum(diff, 0.0) * log_g[:, None, None]), 0.0) * scale
    q_decay = np.exp((n + 1.0) * log_g[:, None])[:, :, None]
    k_decay = np.exp((R_CHUNK - 1.0 - n) * log_g[:, None])[:, None, :] * scale
    chunk_decay = np.exp(R_CHUNK * log_g)
    return (jnp.asarray(decay, F32), jnp.asarray(q_decay, F32), jnp.asarray(k_decay, F32),
            jnp.asarray(chunk_decay, F32))


def _retention(proj3):
    bsz, seq, _ = proj3.shape
    c = R_CHUNK
    decay, qdec, kdec, cdec = _retention_consts()
    grid_spec = pltpu.PrefetchScalarGridSpec(
        num_scalar_prefetch=1,
        grid=(seq // c,),
        in_specs=[pl.BlockSpec((bsz, c, R_QK_WIDTH), lambda i, s: (0, i, COL_QR // R_QK_WIDTH)),
                  pl.BlockSpec((bsz, c, R_QK_WIDTH), lambda i, s: (0, i, COL_KR // R_QK_WIDTH)),
                  pl.BlockSpec((bsz, c, R_V_WIDTH), lambda i, s: (0, i, COL_VR // R_V_WIDTH)),
                  pl.BlockSpec((bsz, c, R_V_WIDTH), lambda i, s: (0, i, COL_GR // R_V_WIDTH)),
                  pl.BlockSpec((R_HEADS, c, c), lambda i, s: (0, 0, 0)),
                  pl.BlockSpec((R_HEADS, c, 1), lambda i, s: (0, 0, 0)),
                  pl.BlockSpec((R_HEADS, 1, c), lambda i, s: (0, 0, 0))],
        out_specs=pl.BlockSpec((bsz, c, R_V_WIDTH), lambda i, s: (0, i, 0)),
        scratch_shapes=[pltpu.VMEM((bsz, R_QK_WIDTH, R_V_DIM), F32)],
    )
    return pl.pallas_call(
        _retention_kernel,
        grid_spec=grid_spec,
        out_shape=jax.ShapeDtypeStruct((bsz, seq, R_V_WIDTH), BF16),
        compiler_params=_cparams(("arbitrary",)),
        name="retention",
    )(cdec, proj3, proj3, proj3, proj3, decay, qdec, kdec)


def _pack_halves(x):
    w = x.shape[1] // 2
    bits = lax.bitcast_convert_type(x.astype(BF16).astype(F32), jnp.uint32)
    return (bits[:, :w] >> 16) | (bits[:, w:] & jnp.uint32(0xFFFF0000))


def _unpack_halves(p):
    lo = lax.bitcast_convert_type(p << 16, F32)
    hi = lax.bitcast_convert_type(p & jnp.uint32(0xFFFF0000), F32)
    return lo, hi


SUBLANES = 8


ROUTER_ROWS = 512


def _mix_kernel(ya_ref, yr_ref, ga_ref, gt_ref, x_ref, wpa_ref, wpr_ref, wout_ref,
                gt1_ref, g_ref, sc_ref, sh_ref, wr_ref, rb_ref,
                x1_ref, h2_ref, h2p_ref, e_ref, w_ref, r_ref, c_ref):
    a = _dot(ya_ref[...], wpa_ref[...]) * _sigmoid(ga_ref[...].astype(F32))
    r = _dot(yr_ref[...], wpr_ref[...]) * _sigmoid(gt_ref[...].astype(F32))
    mix = _dot((a + r).astype(BF16), wout_ref[...])
    x1 = x_ref[...] + gt1_ref[0] * mix
    x1_ref[...] = x1
    ms = jnp.mean(x1 * x1, axis=-1, keepdims=True)
    y = x1 * lax.rsqrt(ms + NORM_EPS) * g_ref[...]
    h2 = y * (1.0 + sc_ref[0]) + sh_ref[0]
    h2_b = h2.astype(BF16)
    h2_ref[...] = h2_b
    h2p_ref[...] = _pack_halves(h2)
    _route(h2_b, wr_ref, rb_ref, e_ref, w_ref, r_ref, c_ref)


def _mix(ya, yr, proj, x2d, wpa, wpr, wout, gt1, g, sc, sh, wr_t, bias_col, seq):
    t, d = x2d.shape
    tm = min(ROUTER_ROWS, seq)
    per_b = seq // tm
    row = lambda i: (i, 0)
    full = lambda i: (0, 0)
    by_tile = lambda i: (0, i)
    per_batch = lambda i: (i // per_b, 0, 0)
    return pl.pallas_call(
        _mix_kernel,
        grid=(t // tm,),
        in_specs=[pl.BlockSpec((tm, A_WIDTH), row),
                  pl.BlockSpec((tm, R_V_WIDTH), row),
                  pl.BlockSpec((tm, d), lambda i: (i, COL_GA // D_MODEL)),
                  pl.BlockSpec((tm, d), lambda i: (i, COL_GT // D_MODEL)),
                  pl.BlockSpec((tm, d), row),
                  pl.BlockSpec((A_WIDTH, d), full),
                  pl.BlockSpec((R_V_WIDTH, d), full),
                  pl.BlockSpec((d, d), full),
                  pl.BlockSpec((1, 1, d), per_batch),
                  pl.BlockSpec((1, d), full),
                  pl.BlockSpec((1, 1, d), per_batch),
                  pl.BlockSpec((1, 1, d), per_batch),
                  pl.BlockSpec((N_EXPERTS, d), full),
                  pl.BlockSpec((N_EXPERTS, 1), full)],
        out_specs=[pl.BlockSpec((tm, d), row), pl.BlockSpec((tm, d), row),
                   pl.BlockSpec((tm, d // 2), row),
                   pl.BlockSpec((TOP_K, tm), by_tile), pl.BlockSpec((TOP_K, tm), by_tile),
                   pl.BlockSpec((TOP_K, tm), by_tile), pl.BlockSpec((N_EXPERTS, LANES), by_tile)],
        out_shape=[jax.ShapeDtypeStruct((t, d), F32), jax.ShapeDtypeStruct((t, d), BF16),
                   jax.ShapeDtypeStruct((t, d // 2), jnp.uint32),
                   jax.ShapeDtypeStruct((TOP_K, t), jnp.int32),
                   jax.ShapeDtypeStruct((TOP_K, t), F32),
                   jax.ShapeDtypeStruct((TOP_K, t), jnp.int32),
                   jax.ShapeDtypeStruct((N_EXPERTS, (t // tm) * LANES), F32)],
        compiler_params=_cparams(("parallel",)),
        name="merge_norm_route",
    )(ya, yr, proj, proj, x2d, wpa, wpr, wout, gt1, g, sc, sh, wr_t, bias_col)


def _route(h, wr_ref, b_ref, e_ref, w_ref, r_ref, c_ref):
    logits = _dot_nt(wr_ref[...], h)
    scores = _sigmoid(logits)
    choice = scores + b_ref[...]
    tm = logits.shape[1]
    giota = lax.broadcasted_iota(jnp.int32, (GROUP_SIZE, tm), 0)
    gs_rows = []
    for g in range(N_GROUPS):
        cg = choice[g * GROUP_SIZE:(g + 1) * GROUP_SIZE, :]
        m1 = jnp.max(cg, axis=0, keepdims=True)
        i1 = jnp.min(jnp.where(cg == m1, giota, GROUP_SIZE), axis=0, keepdims=True)
        m2 = jnp.max(jnp.where(giota == i1, KNOCKED_OUT, cg), axis=0, keepdims=True)
        gs_rows.append(m1 + m2)
    gs = jnp.concatenate(gs_rows, axis=0)
    grow = lax.broadcasted_iota(jnp.int32, (N_GROUPS, tm), 0)
    gmask = jnp.zeros((N_GROUPS, tm), jnp.bool_)
    for _ in range(TOPK_GROUPS):
        mx = jnp.max(gs, axis=0, keepdims=True)
        ix = jnp.min(jnp.where(gs == mx, grow, N_GROUPS), axis=0, keepdims=True)
        hit = grow == ix
        gmask = jnp.logical_or(gmask, hit)
        gs = jnp.where(hit, KNOCKED_OUT, gs)
    gmask_f = jnp.where(gmask, 1.0, 0.0)
    masked = jnp.concatenate(
        [jnp.where(gmask_f[g:g + 1, :] > 0.5, choice[g * GROUP_SIZE:(g + 1) * GROUP_SIZE, :], NEG_INF)
         for g in range(N_GROUPS)], axis=0)
    erow = lax.broadcasted_iota(jnp.int32, (N_EXPERTS, tm), 0)
    idx_rows, w_rows = [], []
    chosen = jnp.zeros((N_EXPERTS, tm), F32)
    for _ in range(TOP_K):
        mx = jnp.max(masked, axis=0, keepdims=True)
        ix = jnp.min(jnp.where(masked == mx, erow, N_EXPERTS), axis=0, keepdims=True)
        hit = erow == ix
        w_rows.append(jnp.sum(jnp.where(hit, scores, 0.0), axis=0, keepdims=True))
        idx_rows.append(ix)
        chosen = jnp.where(hit, 1.0, chosen)
        masked = jnp.where(hit, KNOCKED_OUT, masked)
    w = jnp.concatenate(w_rows, axis=0)
    w = w / (jnp.sum(w, axis=0, keepdims=True) + 1e-20) * ROUTED_SCALE
    e_ref[...] = jnp.concatenate(idx_rows, axis=0)
    w_ref[...] = w
    chosen_b = chosen.astype(BF16)
    earlier = (lax.broadcasted_iota(jnp.int32, (tm, tm), 0)
               < lax.broadcasted_iota(jnp.int32, (tm, tm), 1)).astype(BF16)
    before = _dot(chosen_b, earlier)
    ranks = [jnp.sum(jnp.where(erow == ix, before, 0.0), axis=0, keepdims=True) for ix in idx_rows]
    r_ref[...] = jnp.concatenate(ranks, axis=0).astype(jnp.int32)
    c_ref[...] = _dot(chosen_b, jnp.ones((tm, LANES), BF16))


def _pos_kernel(e_ref, r_ref, base_ref, p_ref):
    tm = e_ref.shape[1]
    erow = lax.broadcasted_iota(jnp.int32, (N_EXPERTS, tm), 0)
    base = base_ref[0]
    rows = [jnp.sum(jnp.where(erow == e_ref[k:k + 1, :], base, 0.0), axis=0, keepdims=True)
            for k in range(TOP_K)]
    p_ref[0] = jnp.concatenate(rows, axis=0).astype(jnp.int32) + r_ref[...]


MOVE_ROWS = 512


def _positions(eidx_t, rank_t, tile_base):
    t = eidx_t.shape[1]
    tm = min(MOVE_ROWS, t)
    per_router_tile = ROUTER_ROWS // tm
    return pl.pallas_call(
        _pos_kernel,
        grid=(t // tm,),
        in_specs=[pl.BlockSpec((TOP_K, tm), lambda i: (0, i)),
                  pl.BlockSpec((TOP_K, tm), lambda i: (0, i)),
                  pl.BlockSpec((1, N_EXPERTS, 1), lambda i: (i // per_router_tile, 0, 0))],
        out_specs=pl.BlockSpec((1, TOP_K, tm), lambda i: (i, 0, 0)),
        out_shape=jax.ShapeDtypeStruct((t // tm, TOP_K, tm), jnp.int32),
        compiler_params=_cparams(("parallel",)),
        name="slot_positions",
    )(eidx_t, rank_t, tile_base)


SLOT_ROWS = 512
PAD_CHUNKS = (256, 128, 64, 32, 16, 8)


def _zero_pads_kernel(pad_start_ref, pad_len_ref, xs_in, xs_hbm, zero_buf, pad_sem):
    del xs_in
    zero_buf[...] = jnp.zeros_like(zero_buf)

    def pad_copies(e, wait):
        start = pad_start_ref[e]
        n = pad_len_ref[e]
        head = jnp.minimum((-start) & (SUBLANES - 1), n)

        def fill(first, size, pred):
            @pl.when(pred)
            def _():
                cp = pltpu.make_async_copy(zero_buf.at[pl.ds(0, size), :], xs_hbm.at[pl.ds(first, size), :], pad_sem)
                if wait:
                    cp.wait()
                else:
                    cp.start()

        for j in range(SUBLANES - 1):
            fill(start + j, 1, j < head)
        ptr = start + head
        rest = n - head
        for chunk in PAD_CHUNKS:
            fill(pl.multiple_of(ptr, SUBLANES), chunk, (rest & chunk) != 0)
            ptr = ptr + (rest & chunk)

    def issue(e, carry):
        pad_copies(e, False)
        return carry

    def drain(e, carry):
        pad_copies(e, True)
        return carry

    lax.fori_loop(0, N_EXPERTS, issue, 0)
    lax.fori_loop(0, N_EXPERTS, drain, 0)


def _zero_pads(pad_start, pad_len, xs):
    grid_spec = pltpu.PrefetchScalarGridSpec(
        num_scalar_prefetch=2,
        grid=(1,),
        in_specs=[pl.BlockSpec(memory_space=pl.ANY)],
        out_specs=pl.BlockSpec(memory_space=pl.ANY),
        scratch_shapes=[pltpu.VMEM((PAD_CHUNKS[0], xs.shape[1]), xs.dtype),
                        pltpu.SemaphoreType.DMA],
    )
    return pl.pallas_call(
        _zero_pads_kernel,
        grid_spec=grid_spec,
        out_shape=jax.ShapeDtypeStruct(xs.shape, xs.dtype),
        input_output_aliases={2: 0},
        compiler_params=_cparams(("arbitrary",)),
        name="zero_pad_slots",
    )(pad_start, pad_len, xs)


SC_DISPATCH_TOKENS = 64


def _sc_dispatch(pos_blocks, h_rows, n_rows):
    info = plsc.get_sparse_core_info()
    n_cores = info.num_cores
    n_workers = n_cores * info.num_subcores
    t = h_rows.shape[0]
    chunk = SC_DISPATCH_TOKENS
    steps = t // (n_workers * chunk)
    assert steps * n_workers * chunk == t
    mesh = plsc.VectorSubcoreMesh(core_axis_name="c", subcore_axis_name="s")

    @functools.partial(
        pl.kernel, mesh=mesh,
        out_type=jax.ShapeDtypeStruct((n_rows,) + h_rows.shape[1:], h_rows.dtype),
        scratch_types=[pltpu.VMEM((TOP_K, chunk), jnp.int32),
                       pltpu.VMEM((chunk,) + h_rows.shape[1:], h_rows.dtype),
                       pltpu.SemaphoreType.DMA],
        name="sc_dispatch_rows",
    )
    def scatter_rows(pos_hbm, h_hbm, out_hbm, idx_v, rows_v, sem):
        wid = lax.axis_index("s") * n_cores + lax.axis_index("c")

        @pl.loop(0, steps)
        def _(step):
            blk = wid * steps + step
            pltpu.sync_copy(pos_hbm.at[blk], idx_v)
            pltpu.sync_copy(h_hbm.at[pl.ds(blk * chunk, chunk)], rows_v)
            scatters = [pltpu.make_async_copy(rows_v, out_hbm.at[idx_v.at[k]], sem) for k in range(TOP_K)]
            for cp in scatters:
                cp.start()
            for cp in scatters:
                cp.wait()

    return scatter_rows(pos_blocks, h_rows)


WEIGHT_SLOTS = 4


def _experts_kernel(blk_e_ref, nblk_ref, ord_ref, eid_ref, nord_ref, x_ref, w1_hbm, w3_hbm, w2_hbm, y_ref,
                    w1f, w3f, w2f, wsem):
    s = pl.program_id(0)

    def fetch(j):
        slot = j % WEIGHT_SLOTS
        e = eid_ref[j]
        return [pltpu.make_async_copy(src.at[e], dst.at[slot], wsem.at[slot])
                for src, dst in ((w1_hbm, w1f), (w3_hbm, w3f), (w2_hbm, w2f))]

    @pl.when(s < nblk_ref[0])
    def _():
        j = ord_ref[s]

        ahead = WEIGHT_SLOTS - 1

        @pl.when(s == 0)
        def _():
            for first in range(ahead):
                @pl.when(first < nord_ref[0])
                def _(first=first):
                    for cp in fetch(first):
                        cp.start()

        @pl.when(jnp.logical_or(s == 0, blk_e_ref[s] != blk_e_ref[jnp.maximum(s - 1, 0)]))
        def _():
            for cp in fetch(j):
                cp.wait()

            @pl.when(j + ahead < nord_ref[0])
            def _():
                for cp in fetch(j + ahead):
                    cp.start()

        slot = j % WEIGHT_SLOTS
        half = x_ref.shape[1]
        lo, hi = _unpack_halves(x_ref[...])
        lo = lo.astype(BF16)
        hi = hi.astype(BF16)
        h1 = (_dot(lo, w1f[slot, :half, :].astype(BF16)) + _dot(hi, w1f[slot, half:, :].astype(BF16)))
        h3 = (_dot(lo, w3f[slot, :half, :].astype(BF16)) + _dot(hi, w3f[slot, half:, :].astype(BF16)))
        mid = (_silu(h1) * h3).astype(BF16)
        y_ref[...] = _pack_halves(_dot(mid, w2f[slot].astype(BF16)))


def _sc_gather(pos_blocks, ys, t):
    info = plsc.get_sparse_core_info()
    n_cores = info.num_cores
    n_workers = n_cores * info.num_subcores
    chunk = SC_DISPATCH_TOKENS
    steps = t // (n_workers * chunk)
    assert steps * n_workers * chunk == t
    mesh = plsc.VectorSubcoreMesh(core_axis_name="c", subcore_axis_name="s")

    @functools.partial(
        pl.kernel, mesh=mesh,
        out_type=jax.ShapeDtypeStruct((TOP_K * t,) + ys.shape[1:], ys.dtype),
        scratch_types=[pltpu.VMEM((TOP_K, chunk), jnp.int32),
                       pltpu.VMEM((chunk,) + ys.shape[1:], ys.dtype),
                       pltpu.VMEM((chunk,) + ys.shape[1:], ys.dtype),
                       pltpu.SemaphoreType.DMA((2,))],
        name="sc_gather_rows",
    )
    def gather_rows(pos_hbm, ys_hbm, out_hbm, idx_v, rows_a, rows_b, sems):
        wid = lax.axis_index("s") * n_cores + lax.axis_index("c")
        bufs = (rows_a, rows_b)

        @pl.loop(0, steps)
        def _(step):
            blk = wid * steps + step
            pltpu.sync_copy(pos_hbm.at[blk], idx_v)
            gathers = [pltpu.make_async_copy(ys_hbm.at[idx_v.at[k]], bufs[k % 2], sems.at[k % 2])
                       for k in range(TOP_K)]
            gathers[0].start()
            for k in range(TOP_K):
                gathers[k].wait()
                if k + 1 < TOP_K:
                    gathers[k + 1].start()
                pltpu.sync_copy(bufs[k % 2], out_hbm.at[pl.ds(k * t + blk * chunk, chunk)])

    return gather_rows(pos_blocks, ys)


def _experts(blk_e, nblk_used, blk_ord, eid_of_ord, n_ord, xs, w1, w3, w2):
    n_rows, half = xs.shape
    d = D_MODEL
    blk = lambda s, be, nb, bo, eo, no: (jnp.minimum(s, nb[0] - 1), 0)
    grid_spec = pltpu.PrefetchScalarGridSpec(
        num_scalar_prefetch=5,
        grid=(n_rows // SLOT_ROWS,),
        in_specs=[pl.BlockSpec((SLOT_ROWS, half), blk),
                  pl.BlockSpec(memory_space=pl.ANY),
                  pl.BlockSpec(memory_space=pl.ANY),
                  pl.BlockSpec(memory_space=pl.ANY)],
        out_specs=pl.BlockSpec((SLOT_ROWS, half), blk),
        scratch_shapes=[pltpu.VMEM((WEIGHT_SLOTS, d, EXPERT_FF), F32),
                        pltpu.VMEM((WEIGHT_SLOTS, d, EXPERT_FF), F32),
                        pltpu.VMEM((WEIGHT_SLOTS, EXPERT_FF, d), F32),
                        pltpu.SemaphoreType.DMA((WEIGHT_SLOTS,))],
    )
    return pl.pallas_call(
        _experts_kernel,
        grid_spec=grid_spec,
        out_shape=jax.ShapeDtypeStruct((n_rows, half), jnp.uint32),
        compiler_params=_cparams(("arbitrary",)),
        name="routed_experts",
    )(blk_e, nblk_used, blk_ord, eid_of_ord, n_ord, xs, w1, w3, w2)


def _combine_kernel(*refs):
    y_refs = refs[:TOP_K]
    w_ref, h_ref, x1_ref, ws1_ref, ws3_ref, ws2_ref, gt2_ref, g_ref = refs[TOP_K:TOP_K + 8]
    o_ref = refs[-1]
    tm, d = x1_ref.shape
    half = d // 2
    h = h_ref[...]
    mid = (_silu(_dot(h, ws1_ref[...])) * _dot(h, ws3_ref[...])).astype(BF16)
    shared = _dot(mid, ws2_ref[...])
    w = w_ref[...]
    acc_lo = jnp.zeros((tm, half), F32)
    acc_hi = jnp.zeros((tm, half), F32)
    for k in range(TOP_K):
        lo, hi = _unpack_halves(y_refs[k][...])
        acc_lo = acc_lo + lo * w[:, k:k + 1]
        acc_hi = acc_hi + hi * w[:, k:k + 1]
    routed = jnp.concatenate([acc_lo, acc_hi], axis=1)
    x2 = x1_ref[...] + gt2_ref[0] * (routed + shared)
    ms = jnp.mean(x2 * x2, axis=-1, keepdims=True)
    o_ref[...] = x2 * lax.rsqrt(ms + NORM_EPS) * g_ref[...]


COMBINE_ROWS = 256
COMBINE_PART_EIGHTHS = (1, 3, 3, 1)


def _combine(y_kt, wts, h2, x1, ws1, ws3, ws2, gt2, g_final, seq, first_token, n_tokens, prev_out):
    t, d = x1.shape
    tm = min(COMBINE_ROWS, seq)
    per_b = seq // tm
    tiles = n_tokens // tm
    first = first_token // tm
    assert tiles * tm == n_tokens and first * tm == first_token
    row = lambda i: (first + i, 0)
    full = lambda i: (0, 0)
    y_specs = [pl.BlockSpec((tm, d // 2), functools.partial(lambda i, k: (k * tiles + i, 0), k=k))
               for k in range(TOP_K)]
    in_specs = y_specs + [
        pl.BlockSpec((tm, TOP_K), row),
        pl.BlockSpec((tm, d), row),
        pl.BlockSpec((tm, d), row),
        pl.BlockSpec((d, SHARED_FF), full),
        pl.BlockSpec((d, SHARED_FF), full),
        pl.BlockSpec((SHARED_FF, d), full),
        pl.BlockSpec((1, 1, d), lambda i: ((first + i) // per_b, 0, 0)),
        pl.BlockSpec((1, d), full)]
    args = [y_kt] * TOP_K + [wts, h2, x1, ws1, ws3, ws2, gt2, g_final]
    aliases = {}
    if prev_out is not None:
        in_specs.append(pl.BlockSpec(memory_space=pl.ANY))
        aliases = {len(args): 0}
        args.append(prev_out)
    return pl.pallas_call(
        _combine_kernel,
        grid=(tiles,),
        in_specs=in_specs,
        out_specs=pl.BlockSpec((tm, d), row),
        out_shape=jax.ShapeDtypeStruct((t, d), F32),
        input_output_aliases=aliases,
        compiler_params=_cparams(("parallel",)),
        name="combine_shared_final",
    )(*args)


def _slot_tables(cnt, t):
    ntiles = cnt.shape[1] // LANES
    cnt_tile = cnt.reshape(N_EXPERTS, ntiles, LANES)[:, :, 0].astype(jnp.int32)
    counts = jnp.sum(cnt_tile, axis=1)
    padded = (counts + SLOT_ROWS - 1) // SLOT_ROWS * SLOT_ROWS
    pstart = jnp.cumsum(padded) - padded
    tile_base = pstart[:, None] + jnp.cumsum(cnt_tile, axis=1) - cnt_tile
    n_blk = -(-(t * TOP_K) // SLOT_ROWS) + N_EXPERTS
    blk_end = jnp.cumsum(padded // SLOT_ROWS)
    blk_e = jnp.sum((blk_end[None, :] <= jnp.arange(n_blk)[:, None]).astype(jnp.int32), axis=1)
    blk_e = jnp.minimum(blk_e, N_EXPERTS - 1)
    owns = (padded > 0).astype(jnp.int32)
    ord_of_e = jnp.cumsum(owns) - owns
    ids = jnp.arange(N_EXPERTS, dtype=jnp.int32)
    eid_of_ord = jnp.sum(jnp.where((ord_of_e[None, :] == ids[:, None]) & (owns[None, :] > 0), ids[None, :], 0), axis=1)
    blk_ord = jnp.sum(jnp.where(blk_e[:, None] == ids[None, :], ord_of_e[None, :], 0), axis=1)
    experts_tables = (blk_e, blk_end[-1:].astype(jnp.int32), blk_ord.astype(jnp.int32),
                      eid_of_ord.astype(jnp.int32), jnp.sum(owns).reshape(1).astype(jnp.int32))
    return (experts_tables, pstart + counts, padded - counts,
            tile_base.T.astype(F32).reshape(ntiles, N_EXPERTS, 1), n_blk * SLOT_ROWS)


def _permute_in_cols(w_in):
    qa, ka, va, qr, kr, vr, gr, ga, gt = jnp.split(
        w_in, np.cumsum((A_WIDTH, A_WIDTH, A_WIDTH, R_QK_WIDTH, R_QK_WIDTH, R_V_WIDTH, R_V_WIDTH,
                         D_MODEL))[:].tolist(), axis=1)
    return jnp.concatenate([vr, gr, ga, gt, qa, ka, va, qr, kr], axis=1)


def kernel(x, c, w_ada, b_ada, g_mix, w_in, w_pa, w_pr, w_out, g_ffn, w_router, router_bias,
           w1, w3, w2, ws1, ws3, ws2, g_final):
    bsz, seq, d = x.shape
    t = bsz * seq
    depth = w_ada.shape[0]
    assert depth == 1, "the final norm is fused into the single layer's last kernel"
    rest = jnp.exp2(-8.0 / A_HEADS * jnp.arange(1, A_HEADS + 1, dtype=F32)) * LOG2_E
    pieces = []
    for _ in range(ALIBI_PIECES):
        pieces.append(rest.astype(BF16).astype(F32))
        rest = rest - pieces[-1]
    slopes = jnp.stack(pieces, axis=1).reshape(-1)
    x2d = x.reshape(t, d)
    for l in range(depth):
        mod = _ada(c, w_ada[l], b_ada[l])
        sh1, sc1, gt1, sh2, sc2, gt2 = [m.reshape(bsz, 1, d) for m in jnp.split(mod, 6, axis=-1)]
        w_in_p = _permute_in_cols(w_in[l]).astype(BF16)
        proj = _inproj(x2d, g_mix[l].reshape(1, d), sc1, sh1, w_in_p, seq)
        proj3 = proj.reshape(bsz, seq, IN_COLS)
        ya = _moba(proj3, slopes).reshape(t, A_WIDTH)
        yr = _retention(proj3).reshape(t, R_V_WIDTH)
        x1, h2, h2p, eidx_t, wts_t, rank_t, cnt = _mix(
            ya, yr, proj, x2d, w_pa[l].astype(BF16), w_pr[l].astype(BF16), w_out[l].astype(BF16), gt1,
            g_ffn[l].reshape(1, d), sc2, sh2, w_router[l].T.astype(BF16), router_bias[l].reshape(N_EXPERTS, 1), seq)
        experts_tables, pad_start, pad_len, tile_base, n_rows = _slot_tables(cnt, t)
        pos3 = _positions(eidx_t, rank_t, tile_base)
        pos_blocks = jnp.transpose(
            pos3.reshape(pos3.shape[0], TOP_K, -1, SC_DISPATCH_TOKENS), (0, 2, 1, 3)
        ).reshape(t // SC_DISPATCH_TOKENS, TOP_K, SC_DISPATCH_TOKENS)
        xs = _zero_pads(pad_start, pad_len, _sc_dispatch(pos_blocks, h2p, n_rows))
        ys = _experts(*experts_tables, xs, w1[l], w3[l], w2[l])
        wts = wts_t.T
        shared_w = (ws1[l].astype(BF16), ws3[l].astype(BF16), ws2[l].astype(BF16))
        x2d = None
        first_token = 0
        for eighths in COMBINE_PART_EIGHTHS:
            n_tokens = t * eighths // sum(COMBINE_PART_EIGHTHS)
            blocks = slice(first_token // SC_DISPATCH_TOKENS, (first_token + n_tokens) // SC_DISPATCH_TOKENS)
            y_kt = _sc_gather(pos_blocks[blocks], ys, n_tokens)
            x2d = _combine(y_kt, wts, h2, x1, *shared_w, gt2, g_final.reshape(1, d), seq,
                           first_token, n_tokens, x2d)
            first_token += n_tokens
    return x2d.reshape(bsz, seq, d)
```

```python
import functools

import jax
import jax.numpy as jnp
import numpy as np
from jax import lax
from jax.experimental import pallas as pl
from jax.experimental.pallas import tpu as pltpu
from jax.experimental.pallas import tpu_sc as plsc

F32 = jnp.float32
BF16 = jnp.bfloat16

D_MODEL = 1024
A_HEADS = 8
A_HEAD_DIM = 64
A_WIDTH = A_HEADS * A_HEAD_DIM
MOBA_BLOCK = 256
MOBA_TOPK = 3
R_HEADS = 8
R_QK_DIM = 64
R_V_DIM = 128
R_QK_WIDTH = R_HEADS * R_QK_DIM
R_V_WIDTH = R_HEADS * R_V_DIM
R_CHUNK = 256
N_EXPERTS = 256
TOP_K = 8
N_GROUPS = 8
GROUP_SIZE = N_EXPERTS // N_GROUPS
TOPK_GROUPS = 4
EXPERT_FF = 256
SHARED_FF = 256
ROUTED_SCALE = 2.5
NORM_EPS = 1e-6
GN_EPS = 1e-6
NEG_INF = -1e30
KNOCKED_OUT = -3e38

COL_VR, COL_GR, COL_GA, COL_GT = 0, 1024, 2048, 3072
COL_QA, COL_KA, COL_VA, COL_QR, COL_KR = 4096, 4608, 5120, 5632, 6144
IN_COLS = 6656
AUG = 128
FEAT_BIAS = A_HEAD_DIM
FEAT_POS = A_HEAD_DIM + 32
ALIBI_PIECES = 3
LOG2_E = 1.4426950408889634
LANES = 128
BF16_ROWS = 16
V_ROWS = A_HEAD_DIM + BF16_ROWS
MOBA_HEADS_PER_STEP = 8

VMEM_LIMIT = 56 * 1024 * 1024


def _cparams(sem, vmem=VMEM_LIMIT):
    return pltpu.CompilerParams(dimension_semantics=sem, vmem_limit_bytes=vmem)


def _dot(a, b):
    return jnp.dot(a, b, preferred_element_type=F32)


def _dot_nt(a, b):
    return lax.dot_general(a, b, (((1,), (1,)), ((), ())), preferred_element_type=F32)


def _sigmoid(x):
    return 1.0 / (1.0 + jnp.exp(-x))


def _silu(x):
    return x * _sigmoid(x)


def _ada_kernel(c_ref, w_ref, b_ref, o_ref):
    c = c_ref[...]
    s = _silu(c)
    s_hi = s.astype(BF16)
    s_lo = (s - s_hi.astype(F32)).astype(BF16)
    w = w_ref[...]
    w_hi = w.astype(BF16)
    w_lo = (w - w_hi.astype(F32)).astype(BF16)
    o_ref[...] = _dot(s_hi, w_hi) + _dot(s_hi, w_lo) + _dot(s_lo, w_hi) + b_ref[...]


def _ada(c, w_ada, b_ada):
    bsz, d = c.shape
    n = w_ada.shape[1]
    tn = 1024
    return pl.pallas_call(
        _ada_kernel,
        grid=(n // tn,),
        in_specs=[pl.BlockSpec((bsz, d), lambda j: (0, 0)),
                  pl.BlockSpec((d, tn), lambda j: (0, j)),
                  pl.BlockSpec((1, tn), lambda j: (0, j))],
        out_specs=pl.BlockSpec((bsz, tn), lambda j: (0, j)),
        out_shape=jax.ShapeDtypeStruct((bsz, n), F32),
        compiler_params=_cparams(("parallel",)),
        name="ada_mod",
    )(c, w_ada, b_ada.reshape(1, n))


INPROJ_COLS = 512


def _inproj_kernel(x_ref, g_ref, sc_ref, sh_ref, w_ref, o_ref):
    x = x_ref[...]
    ms = jnp.mean(x * x, axis=-1, keepdims=True)
    y = x * lax.rsqrt(ms + NORM_EPS) * g_ref[...]
    h = (y * (1.0 + sc_ref[0]) + sh_ref[0]).astype(BF16)
    for j in range(w_ref.shape[1] // INPROJ_COLS):
        cols = slice(j * INPROJ_COLS, (j + 1) * INPROJ_COLS)
        o_ref[:, cols] = _dot(h, w_ref[:, cols]).astype(BF16)


def _inproj(x2d, g, sc, sh, w_bf16, seq):
    t, d = x2d.shape
    n = w_bf16.shape[1]
    tm = min(512, seq)
    per_b = seq // tm
    return pl.pallas_call(
        _inproj_kernel,
        grid=(t // tm,),
        in_specs=[pl.BlockSpec((tm, d), lambda i: (i, 0)),
                  pl.BlockSpec((1, d), lambda i: (0, 0)),
                  pl.BlockSpec((1, 1, d), lambda i: (i // per_b, 0, 0)),
                  pl.BlockSpec((1, 1, d), lambda i: (i // per_b, 0, 0)),
                  pl.BlockSpec((d, n), lambda i: (0, 0))],
        out_specs=pl.BlockSpec((tm, n), lambda i: (i, 0)),
        out_shape=jax.ShapeDtypeStruct((t, n), BF16),
        compiler_params=_cparams(("parallel",)),
        name="norm_inproj",
    )(x2d, g, sc, sh, w_bf16)


def _moba_prepare(i, slopes_ref, q_ref, k_ref, v_ref, ko_ref, qo_ref, vo_ref, kmean_scr):
    nblk = kmean_scr.shape[0]
    width = q_ref.shape[1]
    seq_rows = pl.ds(pl.multiple_of(i * MOBA_BLOCK, MOBA_BLOCK), MOBA_BLOCK)
    q = q_ref[...]
    k = k_ref[...]
    v = v_ref[...]
    kmean_scr[pl.ds(i, 1), :] = jnp.mean(k.astype(F32), axis=0, keepdims=True)

    eye = (lax.broadcasted_iota(jnp.int32, (A_HEAD_DIM, A_HEAD_DIM), 0)
           == lax.broadcasted_iota(jnp.int32, (A_HEAD_DIM, A_HEAD_DIM), 1)).astype(BF16)
    heads = [slice(h * A_HEAD_DIM, (h + 1) * A_HEAD_DIM) for h in range(width // A_HEAD_DIM)]
    q_t = jnp.concatenate([_dot_nt(eye, q[:, hd]) for hd in heads], axis=0)
    v_t = jnp.concatenate([_dot_nt(eye, v[:, hd]) for hd in heads], axis=0)

    km = kmean_scr[...]
    km_rep = jnp.concatenate([km] * A_HEADS, axis=0)
    r_head = lax.broadcasted_iota(jnp.int32, km_rep.shape, 0) // nblk
    c_head = lax.broadcasted_iota(jnp.int32, km_rep.shape, 1) // A_HEAD_DIM
    km_bd = jnp.where(r_head == c_head, km_rep, 0.0)
    km_hi = km_bd.astype(BF16)
    km_lo = (km_bd - km_hi.astype(F32)).astype(BF16)
    q_t_b = q_t.astype(BF16)
    gate_all = _dot(km_hi, q_t_b) + _dot(km_lo, q_t_b)

    mb = q.shape[0]
    blk = lax.broadcasted_iota(jnp.int32, (nblk, mb), 0)
    lane_pos = lax.broadcasted_iota(jnp.int32, (BF16_ROWS, mb), 1).astype(F32)
    row16 = lax.broadcasted_iota(jnp.int32, (BF16_ROWS, mb), 0)
    key_pos = lax.broadcasted_iota(jnp.int32, (mb, AUG), 0).astype(F32)
    kcol = lax.broadcasted_iota(jnp.int32, (mb, AUG), 1)
    widen = (lax.broadcasted_iota(jnp.int32, (A_HEAD_DIM, AUG), 0)
             == lax.broadcasted_iota(jnp.int32, (A_HEAD_DIM, AUG), 1)).astype(BF16)

    blk_first = (i * mb).astype(F32)
    for h in range(A_HEADS):
        pieces = [slopes_ref[h * ALIBI_PIECES + c] for c in range(ALIBI_PIECES)]
        g = jnp.where(blk < i, gate_all[h * nblk:(h + 1) * nblk, :], NEG_INF)
        sel = jnp.zeros((nblk, mb), jnp.bool_)
        for r in range(MOBA_TOPK):
            m = jnp.max(g, axis=0, keepdims=True)
            idx = jnp.min(jnp.where(g == m, blk, nblk), axis=0, keepdims=True)
            hit = blk == idx
            sel = jnp.logical_or(sel, jnp.logical_and(hit, r < i))
            g = jnp.where(hit, KNOCKED_OUT, g)
        bias_t = jnp.where(sel, 0.0, NEG_INF)

        scale = A_HEAD_DIM ** -0.5 * LOG2_E
        qo_ref[0, h, 0:A_HEAD_DIM, :] = (q_t[h * A_HEAD_DIM:(h + 1) * A_HEAD_DIM, :] * scale).astype(BF16)
        qo_ref[0, h, FEAT_BIAS:FEAT_BIAS + nblk, :] = bias_t.astype(BF16)
        if nblk < 32:
            qo_ref[0, h, FEAT_BIAS + nblk:FEAT_POS, :] = jnp.zeros((32 - nblk, mb), BF16)
        piece_rows = jnp.where(row16 % ALIBI_PIECES == 0, pieces[0],
                               jnp.where(row16 % ALIBI_PIECES == 1, pieces[1], pieces[2]))
        pos_feat = jnp.where(row16 < 3, -lane_pos,
                             jnp.where(row16 < 6, piece_rows,
                                       jnp.where(row16 < 9, -blk_first, jnp.where(row16 < 12, piece_rows, 0.0))))
        qo_ref[0, h, FEAT_POS:FEAT_POS + BF16_ROWS, :] = pos_feat.astype(BF16)
        qo_ref[0, h, FEAT_POS + BF16_ROWS:AUG, :] = jnp.zeros((AUG - FEAT_POS - BF16_ROWS, mb), BF16)

        vo_ref[0, h, 0:A_HEAD_DIM, seq_rows] = v_t[h * A_HEAD_DIM:(h + 1) * A_HEAD_DIM, :].astype(BF16)
        vo_ref[0, h, A_HEAD_DIM:V_ROWS, seq_rows] = jnp.where(row16 == 0, 1.0, 0.0).astype(BF16)

        pos_col = kcol - FEAT_POS
        piece_cols = jnp.where(pos_col % ALIBI_PIECES == 0, pieces[0],
                               jnp.where(pos_col % ALIBI_PIECES == 1, pieces[1], pieces[2]))
        k_feat = jnp.where(
            kcol == FEAT_BIAS + i, 1.0,
            jnp.where(pos_col < 0, 0.0,
                      jnp.where(pos_col < 3, piece_cols,
                                jnp.where(pos_col < 6, key_pos,
                                          jnp.where(pos_col < 9, piece_cols,
                                                    jnp.where(pos_col < 12, blk_first, 0.0))))))
        k_wide = _dot(k[:, h * A_HEAD_DIM:(h + 1) * A_HEAD_DIM], widen)
        ko_ref[0, h, seq_rows, :] = (k_wide + k_feat).astype(BF16)


def _moba_attend(i, q_ref, k_ref, v_ref, o_ref, s_a, s_b, s_c, acc_scr, group, n_groups):
    mb = MOBA_BLOCK
    span = group * mb
    own = pl.multiple_of(i * mb, mb)
    key_i = lax.broadcasted_iota(jnp.int32, (mb, mb), 0)
    qry_i = lax.broadcasted_iota(jnp.int32, (mb, mb), 1)
    feat = lax.broadcasted_iota(jnp.int32, (AUG, mb), 0)
    is_bias = jnp.logical_and(feat >= FEAT_BIAS, feat < FEAT_POS)
    q_ts, carry0 = [], []
    for hh in range(MOBA_HEADS_PER_STEP):
        q_t = q_ref[0, hh]
        q_ts.append(q_t)
        q_own = jnp.where(is_bias, jnp.zeros_like(q_t), q_t)
        s = _dot(k_ref[0, hh, pl.ds(own, mb), :], q_own)
        s = jnp.where(key_i <= qry_i, s, NEG_INF)
        m0 = jnp.max(s, axis=0, keepdims=True)
        p = jnp.exp2(s - m0)
        carry0.append(m0)
        acc_scr[hh] = _dot(v_ref[0, hh, :, pl.ds(own, mb)], p.astype(BF16))

    def scores(g, dst):
        start = pl.multiple_of(jnp.minimum(g, n_groups - 1) * span, span)
        for hh in range(MOBA_HEADS_PER_STEP):
            dst[hh] = _dot(k_ref[0, hh, pl.ds(start, span), :], q_ts[hh])

    def absorb(g, src, carry):
        start = pl.multiple_of(g * span, span)
        new = []
        for hh in range(MOBA_HEADS_PER_STEP):
            m = carry[hh]
            sb = src[hh]
            m_new = jnp.maximum(m, jnp.max(sb, axis=0, keepdims=True))
            pb = jnp.exp2(sb - m_new)
            alpha = jnp.exp2(m - m_new)
            acc_scr[hh] = acc_scr[hh] * alpha + _dot(v_ref[0, hh, :, pl.ds(start, span)], pb.astype(BF16))
            new.append(m_new)
        return tuple(new)

    ring = (s_a, s_b, s_c)

    def body(trip, carry):
        g = len(ring) * trip
        for r in range(len(ring)):
            scores(g + r + 1, ring[(r + 1) % len(ring)])
            carry = absorb(g + r, ring[r], carry)
        return carry

    scores(0, s_a)
    live_groups = (i + group - 1) // group
    trips = live_groups // len(ring)
    res = lax.fori_loop(0, trips, body, tuple(carry0))
    done = len(ring) * trips
    res = lax.cond(live_groups > done, lambda c: absorb(done, s_a, c), lambda c: c, res)

    def straggler(g, c):
        scores(g, s_b)
        return absorb(g, s_b, c)

    res = lax.fori_loop(done + 1, live_groups, straggler, res)
    del res
    outs = [acc_scr[hh, 0:A_HEAD_DIM, :] / acc_scr[hh, A_HEAD_DIM:A_HEAD_DIM + 1, :]
            for hh in range(MOBA_HEADS_PER_STEP)]
    o_t = jnp.concatenate(outs, axis=0).astype(BF16)
    eye = (key_i == qry_i).astype(BF16)
    o_ref[0] = _dot_nt(eye, o_t).astype(BF16)


def _moba_kernel(slopes_ref, q_ref, k_ref, v_ref, o_ref, kmean_scr, q_scr, k_scr, v_scr, s_a, s_b, s_c, acc_scr,
                 *, group, n_groups):
    i = pl.program_id(1)

    @pl.when(i == 0)
    def _():
        kmean_scr[...] = jnp.zeros_like(kmean_scr)
        k_scr[...] = jnp.zeros_like(k_scr)
        v_scr[...] = jnp.zeros_like(v_scr)

    _moba_prepare(i, slopes_ref, q_ref, k_ref, v_ref, k_scr, q_scr, v_scr, kmean_scr)
    _moba_attend(i, q_scr, k_scr, v_scr, o_ref, s_a, s_b, s_c, acc_scr, group, n_groups)


def _moba(proj3, slopes):
    bsz, seq, _ = proj3.shape
    mb = MOBA_BLOCK
    nblk = seq // mb
    group = min(2, nblk)
    n_groups = nblk // group
    assert MOBA_HEADS_PER_STEP == A_HEADS
    assert seq % mb == 0 and nblk % group == 0 and nblk <= FEAT_POS - FEAT_BIAS, "one selection feature per key block"
    grid_spec = pltpu.PrefetchScalarGridSpec(
        num_scalar_prefetch=1,
        grid=(bsz, nblk),
        in_specs=[pl.BlockSpec((None, mb, A_WIDTH), lambda b, i, s: (b, i, COL_QA // A_WIDTH)),
                  pl.BlockSpec((None, mb, A_WIDTH), lambda b, i, s: (b, i, COL_KA // A_WIDTH)),
                  pl.BlockSpec((None, mb, A_WIDTH), lambda b, i, s: (b, i, COL_VA // A_WIDTH))],
        out_specs=pl.BlockSpec((1, mb, A_WIDTH), lambda b, i, s: (b, i, 0)),
        scratch_shapes=[pltpu.VMEM((nblk, A_WIDTH), F32),
                        pltpu.VMEM((1, A_HEADS, AUG, mb), BF16),
                        pltpu.VMEM((1, A_HEADS, seq, AUG), BF16),
                        pltpu.VMEM((1, A_HEADS, V_ROWS, seq), BF16),
                        pltpu.VMEM((A_HEADS, group * mb, mb), F32),
                        pltpu.VMEM((A_HEADS, group * mb, mb), F32),
                        pltpu.VMEM((A_HEADS, group * mb, mb), F32),
                        pltpu.VMEM((A_HEADS, V_ROWS, mb), F32)],
    )
    return pl.pallas_call(
        functools.partial(_moba_kernel, group=group, n_groups=n_groups),
        grid_spec=grid_spec,
        out_shape=jax.ShapeDtypeStruct((bsz, seq, A_WIDTH), BF16),
        compiler_params=_cparams(("parallel", "arbitrary")),
        name="moba_attention",
    )(slopes, proj3, proj3, proj3)


def _retention_kernel(cdec_ref, q_ref, k_ref, v_ref, g_ref, decay_ref, qdec_ref, kdec_ref, o_ref, state_scr):
    @pl.when(pl.program_id(0) == 0)
    def _():
        state_scr[...] = jnp.zeros_like(state_scr)

    width = q_ref.shape[2]
    eye = (lax.broadcasted_iota(jnp.int32, (width, width), 0)
           == lax.broadcasted_iota(jnp.int32, (width, width), 1)).astype(BF16)
    for b in range(q_ref.shape[0]):
        q = q_ref[b]
        k_t = _dot_nt(eye, k_ref[b])
        k_t_b = k_t.astype(BF16)
        state_b = state_scr[b].astype(BF16)
        for h in range(R_HEADS):
            rows = slice(h * R_QK_DIM, (h + 1) * R_QK_DIM)
            cols = slice(h * R_V_DIM, (h + 1) * R_V_DIM)
            q_h = q[:, rows]
            v_h = v_ref[b, :, cols]
            inner = _dot(q_h, k_t_b[rows, :]) * decay_ref[h]
            out = _dot(inner.astype(BF16), v_h) + _dot(q_h, state_b[rows, :]) * qdec_ref[h]
            k_dec = (k_t[rows, :] * kdec_ref[h]).astype(BF16)
            state_scr[b, rows, :] = cdec_ref[h] * state_scr[b, rows, :] + _dot(k_dec, v_h)
            mu = jnp.mean(out, axis=-1, keepdims=True)
            cen = out - mu
            var = jnp.mean(cen * cen, axis=-1, keepdims=True)
            y = cen * lax.rsqrt(var + GN_EPS)
            o_ref[b, :, cols] = (y * _silu(g_ref[b, :, cols].astype(F32))).astype(BF16)


def _retention_consts():
    h = np.arange(R_HEADS, dtype=np.float64)
    log_g = np.log(1.0 - np.exp2(-5.0 - h))
    n = np.arange(R_CHUNK, dtype=np.float64)
    diff = n[:, None] - n[None, :]
    scale = R_QK_DIM ** -0.5
    decay = np.where(diff >= 0, np.exp(np.maximum(diff, 0.0) * log_g[:, None, None]), 0.0) * scale
    q_decay = np.exp((n + 1.0) * log_g[:, None])[:, :, None]
    k_decay = np.exp((R_CHUNK - 1.0 - n) * log_g[:, None])[:, None, :] * scale
    chunk_decay = np.exp(R_CHUNK * log_g)
    return (jnp.asarray(decay, F32), jnp.asarray(q_decay, F32), jnp.asarray(k_decay, F32),
            jnp.asarray(chunk_decay, F32))


def _retention(proj3):
    bsz, seq, _ = proj3.shape
    c = R_CHUNK
    decay, qdec, kdec, cdec = _retention_consts()
    grid_spec = pltpu.PrefetchScalarGridSpec(
        num_scalar_prefetch=1,
        grid=(seq // c,),
        in_specs=[pl.BlockSpec((bsz, c, R_QK_WIDTH), lambda i, s: (0, i, COL_QR // R_QK_WIDTH)),
                  pl.BlockSpec((bsz, c, R_QK_WIDTH), lambda i, s: (0, i, COL_KR // R_QK_WIDTH)),
                  pl.BlockSpec((bsz, c, R_V_WIDTH), lambda i, s: (0, i, COL_VR // R_V_WIDTH)),
                  pl.BlockSpec((bsz, c, R_V_WIDTH), lambda i, s: (0, i, COL_GR // R_V_WIDTH)),
                  pl.BlockSpec((R_HEADS, c, c), lambda i, s: (0, 0, 0)),
                  pl.BlockSpec((R_HEADS, c, 1), lambda i, s: (0, 0, 0)),
                  pl.BlockSpec((R_HEADS, 1, c), lambda i, s: (0, 0, 0))],
        out_specs=pl.BlockSpec((bsz, c, R_V_WIDTH), lambda i, s: (0, i, 0)),
        scratch_shapes=[pltpu.VMEM((bsz, R_QK_WIDTH, R_V_DIM), F32)],
    )
    return pl.pallas_call(
        _retention_kernel,
        grid_spec=grid_spec,
        out_shape=jax.ShapeDtypeStruct((bsz, seq, R_V_WIDTH), BF16),
        compiler_params=_cparams(("arbitrary",)),
        name="retention",
    )(cdec, proj3, proj3, proj3, proj3, decay, qdec, kdec)


def _pack_halves(x):
    w = x.shape[1] // 2
    bits = lax.bitcast_convert_type(x.astype(BF16).astype(F32), jnp.uint32)
    return (bits[:, :w] >> 16) | (bits[:, w:] & jnp.uint32(0xFFFF0000))


def _unpack_halves(p):
    lo = lax.bitcast_convert_type(p << 16, F32)
    hi = lax.bitcast_convert_type(p & jnp.uint32(0xFFFF0000), F32)
    return lo, hi


SUBLANES = 8


ROUTER_ROWS = 512


def _mix_kernel(ya_ref, yr_ref, ga_ref, gt_ref, x_ref, wpa_ref, wpr_ref, wout_ref,
                gt1_ref, g_ref, sc_ref, sh_ref, wr_ref, rb_ref,
                x1_ref, h2_ref, h2p_ref, e_ref, w_ref, r_ref, c_ref):
    a = _dot(ya_ref[...], wpa_ref[...]) * _sigmoid(ga_ref[...].astype(F32))
    r = _dot(yr_ref[...], wpr_ref[...]) * _sigmoid(gt_ref[...].astype(F32))
    mix = _dot((a + r).astype(BF16), wout_ref[...])
    x1 = x_ref[...] + gt1_ref[0] * mix
    x1_ref[...] = x1
    ms = jnp.mean(x1 * x1, axis=-1, keepdims=True)
    y = x1 * lax.rsqrt(ms + NORM_EPS) * g_ref[...]
    h2 = y * (1.0 + sc_ref[0]) + sh_ref[0]
    h2_b = h2.astype(BF16)
    h2_ref[...] = h2_b
    h2p_ref[...] = _pack_halves(h2)
    _route(h2_b, wr_ref, rb_ref, e_ref, w_ref, r_ref, c_ref)


def _mix(ya, yr, proj, x2d, wpa, wpr, wout, gt1, g, sc, sh, wr_t, bias_col, seq):
    t, d = x2d.shape
    tm = min(ROUTER_ROWS, seq)
    per_b = seq // tm
    row = lambda i: (i, 0)
    full = lambda i: (0, 0)
    by_tile = lambda i: (0, i)
    per_batch = lambda i: (i // per_b, 0, 0)
    return pl.pallas_call(
        _mix_kernel,
        grid=(t // tm,),
        in_specs=[pl.BlockSpec((tm, A_WIDTH), row),
                  pl.BlockSpec((tm, R_V_WIDTH), row),
                  pl.BlockSpec((tm, d), lambda i: (i, COL_GA // D_MODEL)),
                  pl.BlockSpec((tm, d), lambda i: (i, COL_GT // D_MODEL)),
                  pl.BlockSpec((tm, d), row),
                  pl.BlockSpec((A_WIDTH, d), full),
                  pl.BlockSpec((R_V_WIDTH, d), full),
                  pl.BlockSpec((d, d), full),
                  pl.BlockSpec((1, 1, d), per_batch),
                  pl.BlockSpec((1, d), full),
                  pl.BlockSpec((1, 1, d), per_batch),
                  pl.BlockSpec((1, 1, d), per_batch),
                  pl.BlockSpec((N_EXPERTS, d), full),
                  pl.BlockSpec((N_EXPERTS, 1), full)],
        out_specs=[pl.BlockSpec((tm, d), row), pl.BlockSpec((tm, d), row),
                   pl.BlockSpec((tm, d // 2), row),
                   pl.BlockSpec((TOP_K, tm), by_tile), pl.BlockSpec((TOP_K, tm), by_tile),
                   pl.BlockSpec((TOP_K, tm), by_tile), pl.BlockSpec((N_EXPERTS, LANES), by_tile)],
        out_shape=[jax.ShapeDtypeStruct((t, d), F32), jax.ShapeDtypeStruct((t, d), BF16),
                   jax.ShapeDtypeStruct((t, d // 2), jnp.uint32),
                   jax.ShapeDtypeStruct((TOP_K, t), jnp.int32),
                   jax.ShapeDtypeStruct((TOP_K, t), F32),
                   jax.ShapeDtypeStruct((TOP_K, t), jnp.int32),
                   jax.ShapeDtypeStruct((N_EXPERTS, (t // tm) * LANES), F32)],
        compiler_params=_cparams(("parallel",)),
        name="merge_norm_route",
    )(ya, yr, proj, proj, x2d, wpa, wpr, wout, gt1, g, sc, sh, wr_t, bias_col)


def _route(h, wr_ref, b_ref, e_ref, w_ref, r_ref, c_ref):
    logits = _dot_nt(wr_ref[...], h)
    scores = _sigmoid(logits)
    choice = scores + b_ref[...]
    tm = logits.shape[1]
    giota = lax.broadcasted_iota(jnp.int32, (GROUP_SIZE, tm), 0)
    gs_rows = []
    for g in range(N_GROUPS):
        cg = choice[g * GROUP_SIZE:(g + 1) * GROUP_SIZE, :]
        m1 = jnp.max(cg, axis=0, keepdims=True)
        i1 = jnp.min(jnp.where(cg == m1, giota, GROUP_SIZE), axis=0, keepdims=True)
        m2 = jnp.max(jnp.where(giota == i1, KNOCKED_OUT, cg), axis=0, keepdims=True)
        gs_rows.append(m1 + m2)
    gs = jnp.concatenate(gs_rows, axis=0)
    grow = lax.broadcasted_iota(jnp.int32, (N_GROUPS, tm), 0)
    gmask = jnp.zeros((N_GROUPS, tm), jnp.bool_)
    for _ in range(TOPK_GROUPS):
        mx = jnp.max(gs, axis=0, keepdims=True)
        ix = jnp.min(jnp.where(gs == mx, grow, N_GROUPS), axis=0, keepdims=True)
        hit = grow == ix
        gmask = jnp.logical_or(gmask, hit)
        gs = jnp.where(hit, KNOCKED_OUT, gs)
    gmask_f = jnp.where(gmask, 1.0, 0.0)
    masked = jnp.concatenate(
        [jnp.where(gmask_f[g:g + 1, :] > 0.5, choice[g * GROUP_SIZE:(g + 1) * GROUP_SIZE, :], NEG_INF)
         for g in range(N_GROUPS)], axis=0)
    erow = lax.broadcasted_iota(jnp.int32, (N_EXPERTS, tm), 0)
    idx_rows, w_rows = [], []
    chosen = jnp.zeros((N_EXPERTS, tm), F32)
    for _ in range(TOP_K):
        mx = jnp.max(masked, axis=0, keepdims=True)
        ix = jnp.min(jnp.where(masked == mx, erow, N_EXPERTS), axis=0, keepdims=True)
        hit = erow == ix
        w_rows.append(jnp.sum(jnp.where(hit, scores, 0.0), axis=0, keepdims=True))
        idx_rows.append(ix)
        chosen = jnp.where(hit, 1.0, chosen)
        masked = jnp.where(hit, KNOCKED_OUT, masked)
    w = jnp.concatenate(w_rows, axis=0)
    w = w / (jnp.sum(w, axis=0, keepdims=True) + 1e-20) * ROUTED_SCALE
    e_ref[...] = jnp.concatenate(idx_rows, axis=0)
    w_ref[...] = w
    chosen_b = chosen.astype(BF16)
    earlier = (lax.broadcasted_iota(jnp.int32, (tm, tm), 0)
               < lax.broadcasted_iota(jnp.int32, (tm, tm), 1)).astype(BF16)
    before = _dot(chosen_b, earlier)
    ranks = [jnp.sum(jnp.where(erow == ix, before, 0.0), axis=0, keepdims=True) for ix in idx_rows]
    r_ref[...] = jnp.concatenate(ranks, axis=0).astype(jnp.int32)
    c_ref[...] = _dot(chosen_b, jnp.ones((tm, LANES), BF16))


def _pos_kernel(e_ref, r_ref, base_ref, p_ref):
    tm = e_ref.shape[1]
    erow = lax.broadcasted_iota(jnp.int32, (N_EXPERTS, tm), 0)
    base = base_ref[0]
    rows = [jnp.sum(jnp.where(erow == e_ref[k:k + 1, :], base, 0.0), axis=0, keepdims=True)
            for k in range(TOP_K)]
    p_ref[0] = jnp.concatenate(rows, axis=0).astype(jnp.int32) + r_ref[...]


MOVE_ROWS = 512


def _positions(eidx_t, rank_t, tile_base):
    t = eidx_t.shape[1]
    tm = min(MOVE_ROWS, t)
    per_router_tile = ROUTER_ROWS // tm
    return pl.pallas_call(
        _pos_kernel,
        grid=(t // tm,),
        in_specs=[pl.BlockSpec((TOP_K, tm), lambda i: (0, i)),
                  pl.BlockSpec((TOP_K, tm), lambda i: (0, i)),
                  pl.BlockSpec((1, N_EXPERTS, 1), lambda i: (i // per_router_tile, 0, 0))],
        out_specs=pl.BlockSpec((1, TOP_K, tm), lambda i: (i, 0, 0)),
        out_shape=jax.ShapeDtypeStruct((t // tm, TOP_K, tm), jnp.int32),
        compiler_params=_cparams(("parallel",)),
        name="slot_positions",
    )(eidx_t, rank_t, tile_base)


SLOT_ROWS = 512
PAD_CHUNKS = (256, 128, 64, 32, 16, 8)


def _zero_pads_kernel(pad_start_ref, pad_len_ref, xs_in, xs_hbm, zero_buf, pad_sem):
    del xs_in
    zero_buf[...] = jnp.zeros_like(zero_buf)

    def pad_copies(e, wait):
        start = pad_start_ref[e]
        n = pad_len_ref[e]
        head = jnp.minimum((-start) & (SUBLANES - 1), n)

        def fill(first, size, pred):
            @pl.when(pred)
            def _():
                cp = pltpu.make_async_copy(zero_buf.at[pl.ds(0, size), :], xs_hbm.at[pl.ds(first, size), :], pad_sem)
                if wait:
                    cp.wait()
                else:
                    cp.start()

        for j in range(SUBLANES - 1):
            fill(start + j, 1, j < head)
        ptr = start + head
        rest = n - head
        for chunk in PAD_CHUNKS:
            fill(pl.multiple_of(ptr, SUBLANES), chunk, (rest & chunk) != 0)
            ptr = ptr + (rest & chunk)

    def issue(e, carry):
        pad_copies(e, False)
        return carry

    def drain(e, carry):
        pad_copies(e, True)
        return carry

    lax.fori_loop(0, N_EXPERTS, issue, 0)
    lax.fori_loop(0, N_EXPERTS, drain, 0)


def _zero_pads(pad_start, pad_len, xs):
    grid_spec = pltpu.PrefetchScalarGridSpec(
        num_scalar_prefetch=2,
        grid=(1,),
        in_specs=[pl.BlockSpec(memory_space=pl.ANY)],
        out_specs=pl.BlockSpec(memory_space=pl.ANY),
        scratch_shapes=[pltpu.VMEM((PAD_CHUNKS[0], xs.shape[1]), xs.dtype),
                        pltpu.SemaphoreType.DMA],
    )
    return pl.pallas_call(
        _zero_pads_kernel,
        grid_spec=grid_spec,
        out_shape=jax.ShapeDtypeStruct(xs.shape, xs.dtype),
        input_output_aliases={2: 0},
        compiler_params=_cparams(("arbitrary",)),
        name="zero_pad_slots",
    )(pad_start, pad_len, xs)


SC_DISPATCH_TOKENS = 64


def _sc_dispatch(pos_blocks, h_rows, n_rows):
    info = plsc.get_sparse_core_info()
    n_cores = info.num_cores
    n_workers = n_cores * info.num_subcores
    t = h_rows.shape[0]
    chunk = SC_DISPATCH_TOKENS
    steps = t // (n_workers * chunk)
    assert steps * n_workers * chunk == t
    mesh = plsc.VectorSubcoreMesh(core_axis_name="c", subcore_axis_name="s")

    @functools.partial(
        pl.kernel, mesh=mesh,
        out_type=jax.ShapeDtypeStruct((n_rows,) + h_rows.shape[1:], h_rows.dtype),
        scratch_types=[pltpu.VMEM((TOP_K, chunk), jnp.int32),
                       pltpu.VMEM((chunk,) + h_rows.shape[1:], h_rows.dtype),
                       pltpu.SemaphoreType.DMA],
        name="sc_dispatch_rows",
    )
    def scatter_rows(pos_hbm, h_hbm, out_hbm, idx_v, rows_v, sem):
        wid = lax.axis_index("s") * n_cores + lax.axis_index("c")

        @pl.loop(0, steps)
        def _(step):
            blk = wid * steps + step
            pltpu.sync_copy(pos_hbm.at[blk], idx_v)
            pltpu.sync_copy(h_hbm.at[pl.ds(blk * chunk, chunk)], rows_v)
            scatters = [pltpu.make_async_copy(rows_v, out_hbm.at[idx_v.at[k]], sem) for k in range(TOP_K)]
            for cp in scatters:
                cp.start()
            for cp in scatters:
                cp.wait()

    return scatter_rows(pos_blocks, h_rows)


WEIGHT_SLOTS = 4


def _experts_kernel(blk_e_ref, nblk_ref, ord_ref, eid_ref, nord_ref, x_ref, w1_hbm, w3_hbm, w2_hbm, y_ref,
                    w1f, w3f, w2f, wsem):
    s = pl.program_id(0)

    def fetch(j):
        slot = j % WEIGHT_SLOTS
        e = eid_ref[j]
        return [pltpu.make_async_copy(src.at[e], dst.at[slot], wsem.at[slot])
                for src, dst in ((w1_hbm, w1f), (w3_hbm, w3f), (w2_hbm, w2f))]

    @pl.when(s < nblk_ref[0])
    def _():
        j = ord_ref[s]

        ahead = WEIGHT_SLOTS - 1

        @pl.when(s == 0)
        def _():
            for first in range(ahead):
                @pl.when(first < nord_ref[0])
                def _(first=first):
                    for cp in fetch(first):
                        cp.start()

        @pl.when(jnp.logical_or(s == 0, blk_e_ref[s] != blk_e_ref[jnp.maximum(s - 1, 0)]))
        def _():
            for cp in fetch(j):
                cp.wait()

            @pl.when(j + ahead < nord_ref[0])
            def _():
                for cp in fetch(j + ahead):
                    cp.start()

        slot = j % WEIGHT_SLOTS
        half = x_ref.shape[1]
        lo, hi = _unpack_halves(x_ref[...])
        lo = lo.astype(BF16)
        hi = hi.astype(BF16)
        h1 = (_dot(lo, w1f[slot, :half, :].astype(BF16)) + _dot(hi, w1f[slot, half:, :].astype(BF16)))
        h3 = (_dot(lo, w3f[slot, :half, :].astype(BF16)) + _dot(hi, w3f[slot, half:, :].astype(BF16)))
        mid = (_silu(h1) * h3).astype(BF16)
        y_ref[...] = _pack_halves(_dot(mid, w2f[slot].astype(BF16)))


def _sc_gather(pos_blocks, ys, t):
    info = plsc.get_sparse_core_info()
    n_cores = info.num_cores
    n_workers = n_cores * info.num_subcores
    chunk = SC_DISPATCH_TOKENS
    steps = t // (n_workers * chunk)
    assert steps * n_workers * chunk == t
    mesh = plsc.VectorSubcoreMesh(core_axis_name="c", subcore_axis_name="s")

    @functools.partial(
        pl.kernel, mesh=mesh,
        out_type=jax.ShapeDtypeStruct((TOP_K * t,) + ys.shape[1:], ys.dtype),
        scratch_types=[pltpu.VMEM((TOP_K, chunk), jnp.int32),
                       pltpu.VMEM((chunk,) + ys.shape[1:], ys.dtype),
                       pltpu.VMEM((chunk,) + ys.shape[1:], ys.dtype),
                       pltpu.SemaphoreType.DMA((2,))],
        name="sc_gather_rows",
    )
    def gather_rows(pos_hbm, ys_hbm, out_hbm, idx_v, rows_a, rows_b, sems):
        wid = lax.axis_index("s") * n_cores + lax.axis_index("c")
        bufs = (rows_a, rows_b)

        @pl.loop(0, steps)
        def _(step):
            blk = wid * steps + step
            pltpu.sync_copy(pos_hbm.at[blk], idx_v)
            gathers = [pltpu.make_async_copy(ys_hbm.at[idx_v.at[k]], bufs[k % 2], sems.at[k % 2])
                       for k in range(TOP_K)]
            gathers[0].start()
            for k in range(TOP_K):
                gathers[k].wait()
                if k + 1 < TOP_K:
                    gathers[k + 1].start()
                pltpu.sync_copy(bufs[k % 2], out_hbm.at[pl.ds(k * t + blk * chunk, chunk)])

    return gather_rows(pos_blocks, ys)


def _experts(blk_e, nblk_used, blk_ord, eid_of_ord, n_ord, xs, w1, w3, w2):
    n_rows, half = xs.shape
    d = D_MODEL
    blk = lambda s, be, nb, bo, eo, no: (jnp.minimum(s, nb[0] - 1), 0)
    grid_spec = pltpu.PrefetchScalarGridSpec(
        num_scalar_prefetch=5,
        grid=(n_rows // SLOT_ROWS,),
        in_specs=[pl.BlockSpec((SLOT_ROWS, half), blk),
                  pl.BlockSpec(memory_space=pl.ANY),
                  pl.BlockSpec(memory_space=pl.ANY),
                  pl.BlockSpec(memory_space=pl.ANY)],
        out_specs=pl.BlockSpec((SLOT_ROWS, half), blk),
        scratch_shapes=[pltpu.VMEM((WEIGHT_SLOTS, d, EXPERT_FF), F32),
                        pltpu.VMEM((WEIGHT_SLOTS, d, EXPERT_FF), F32),
                        pltpu.VMEM((WEIGHT_SLOTS, EXPERT_FF, d), F32),
                        pltpu.SemaphoreType.DMA((WEIGHT_SLOTS,))],
    )
    return pl.pallas_call(
        _experts_kernel,
        grid_spec=grid_spec,
        out_shape=jax.ShapeDtypeStruct((n_rows, half), jnp.uint32),
        compiler_params=_cparams(("arbitrary",)),
        name="routed_experts",
    )(blk_e, nblk_used, blk_ord, eid_of_ord, n_ord, xs, w1, w3, w2)


def _combine_kernel(*refs):
    y_refs = refs[:TOP_K]
    w_ref, h_ref, x1_ref, ws1_ref, ws3_ref, ws2_ref, gt2_ref, g_ref = refs[TOP_K:TOP_K + 8]
    o_ref = refs[-1]
    tm, d = x1_ref.shape
    half = d // 2
    h = h_ref[...]
    mid = (_silu(_dot(h, ws1_ref[...])) * _dot(h, ws3_ref[...])).astype(BF16)
    shared = _dot(mid, ws2_ref[...])
    w = w_ref[...]
    acc_lo = jnp.zeros((tm, half), F32)
    acc_hi = jnp.zeros((tm, half), F32)
    for k in range(TOP_K):
        lo, hi = _unpack_halves(y_refs[k][...])
        acc_lo = acc_lo + lo * w[:, k:k + 1]
        acc_hi = acc_hi + hi * w[:, k:k + 1]
    routed = jnp.concatenate([acc_lo, acc_hi], axis=1)
    x2 = x1_ref[...] + gt2_ref[0] * (routed + shared)
    ms = jnp.mean(x2 * x2, axis=-1, keepdims=True)
    o_ref[...] = x2 * lax.rsqrt(ms + NORM_EPS) * g_ref[...]


COMBINE_ROWS = 256
COMBINE_PARTS = 4


def _combine(y_kt, wts, h2, x1, ws1, ws3, ws2, gt2, g_final, seq, part, n_parts, prev_out):
    t, d = x1.shape
    tm = min(COMBINE_ROWS, seq)
    per_b = seq // tm
    tiles = t // tm // n_parts
    first = part * tiles
    row = lambda i: (first + i, 0)
    full = lambda i: (0, 0)
    y_specs = [pl.BlockSpec((tm, d // 2), functools.partial(lambda i, k: (k * tiles + i, 0), k=k))
               for k in range(TOP_K)]
    in_specs = y_specs + [
        pl.BlockSpec((tm, TOP_K), row),
        pl.BlockSpec((tm, d), row),
        pl.BlockSpec((tm, d), row),
        pl.BlockSpec((d, SHARED_FF), full),
        pl.BlockSpec((d, SHARED_FF), full),
        pl.BlockSpec((SHARED_FF, d), full),
        pl.BlockSpec((1, 1, d), lambda i: ((first + i) // per_b, 0, 0)),
        pl.BlockSpec((1, d), full)]
    args = [y_kt] * TOP_K + [wts, h2, x1, ws1, ws3, ws2, gt2, g_final]
    aliases = {}
    if prev_out is not None:
        in_specs.append(pl.BlockSpec(memory_space=pl.ANY))
        aliases = {len(args): 0}
        args.append(prev_out)
    return pl.pallas_call(
        _combine_kernel,
        grid=(tiles,),
        in_specs=in_specs,
        out_specs=pl.BlockSpec((tm, d), row),
        out_shape=jax.ShapeDtypeStruct((t, d), F32),
        input_output_aliases=aliases,
        compiler_params=_cparams(("parallel",)),
        name="combine_shared_final",
    )(*args)


def _slot_tables(cnt, t):
    ntiles = cnt.shape[1] // LANES
    cnt_tile = cnt.reshape(N_EXPERTS, ntiles, LANES)[:, :, 0].astype(jnp.int32)
    counts = jnp.sum(cnt_tile, axis=1)
    padded = (counts + SLOT_ROWS - 1) // SLOT_ROWS * SLOT_ROWS
    pstart = jnp.cumsum(padded) - padded
    tile_base = pstart[:, None] + jnp.cumsum(cnt_tile, axis=1) - cnt_tile
    n_blk = -(-(t * TOP_K) // SLOT_ROWS) + N_EXPERTS
    blk_end = jnp.cumsum(padded // SLOT_ROWS)
    blk_e = jnp.sum((blk_end[None, :] <= jnp.arange(n_blk)[:, None]).astype(jnp.int32), axis=1)
    blk_e = jnp.minimum(blk_e, N_EXPERTS - 1)
    owns = (padded > 0).astype(jnp.int32)
    ord_of_e = jnp.cumsum(owns) - owns
    ids = jnp.arange(N_EXPERTS, dtype=jnp.int32)
    eid_of_ord = jnp.sum(jnp.where((ord_of_e[None, :] == ids[:, None]) & (owns[None, :] > 0), ids[None, :], 0), axis=1)
    blk_ord = jnp.sum(jnp.where(blk_e[:, None] == ids[None, :], ord_of_e[None, :], 0), axis=1)
    experts_tables = (blk_e, blk_end[-1:].astype(jnp.int32), blk_ord.astype(jnp.int32),
                      eid_of_ord.astype(jnp.int32), jnp.sum(owns).reshape(1).astype(jnp.int32))
    return (experts_tables, pstart + counts, padded - counts,
            tile_base.T.astype(F32).reshape(ntiles, N_EXPERTS, 1), n_blk * SLOT_ROWS)


def _permute_in_cols(w_in):
    qa, ka, va, qr, kr, vr, gr, ga, gt = jnp.split(
        w_in, np.cumsum((A_WIDTH, A_WIDTH, A_WIDTH, R_QK_WIDTH, R_QK_WIDTH, R_V_WIDTH, R_V_WIDTH,
                         D_MODEL))[:].tolist(), axis=1)
    return jnp.concatenate([vr, gr, ga, gt, qa, ka, va, qr, kr], axis=1)


def kernel(x, c, w_ada, b_ada, g_mix, w_in, w_pa, w_pr, w_out, g_ffn, w_router, router_bias,
           w1, w3, w2, ws1, ws3, ws2, g_final):
    bsz, seq, d = x.shape
    t = bsz * seq
    depth = w_ada.shape[0]
    assert depth == 1, "the final norm is fused into the single layer's last kernel"
    rest = jnp.exp2(-8.0 / A_HEADS * jnp.arange(1, A_HEADS + 1, dtype=F32)) * LOG2_E
    pieces = []
    for _ in range(ALIBI_PIECES):
        pieces.append(rest.astype(BF16).astype(F32))
        rest = rest - pieces[-1]
    slopes = jnp.stack(pieces, axis=1).reshape(-1)
    x2d = x.reshape(t, d)
    for l in range(depth):
        mod = _ada(c, w_ada[l], b_ada[l])
        sh1, sc1, gt1, sh2, sc2, gt2 = [m.reshape(bsz, 1, d) for m in jnp.split(mod, 6, axis=-1)]
        w_in_p = _permute_in_cols(w_in[l]).astype(BF16)
        proj = _inproj(x2d, g_mix[l].reshape(1, d), sc1, sh1, w_in_p, seq)
        proj3 = proj.reshape(bsz, seq, IN_COLS)
        ya = _moba(proj3, slopes).reshape(t, A_WIDTH)
        yr = _retention(proj3).reshape(t, R_V_WIDTH)
        x1, h2, h2p, eidx_t, wts_t, rank_t, cnt = _mix(
            ya, yr, proj, x2d, w_pa[l].astype(BF16), w_pr[l].astype(BF16), w_out[l].astype(BF16), gt1,
            g_ffn[l].reshape(1, d), sc2, sh2, w_router[l].T.astype(BF16), router_bias[l].reshape(N_EXPERTS, 1), seq)
        experts_tables, pad_start, pad_len, tile_base, n_rows = _slot_tables(cnt, t)
        pos3 = _positions(eidx_t, rank_t, tile_base)
        pos_blocks = jnp.transpose(
            pos3.reshape(pos3.shape[0], TOP_K, -1, SC_DISPATCH_TOKENS), (0, 2, 1, 3)
        ).reshape(t // SC_DISPATCH_TOKENS, TOP_K, SC_DISPATCH_TOKENS)
        xs = _zero_pads(pad_start, pad_len, _sc_dispatch(pos_blocks, h2p, n_rows))
        ys = _experts(*experts_tables, xs, w1[l], w3[l], w2[l])
        wts = wts_t.T
        shared_w = (ws1[l].astype(BF16), ws3[l].astype(BF16), ws2[l].astype(BF16))
        blocks_per_part = pos_blocks.shape[0] // COMBINE_PARTS
        x2d = None
        for part in range(COMBINE_PARTS):
            y_kt = _sc_gather(pos_blocks[part * blocks_per_part:(part + 1) * blocks_per_part], ys,
                              t // COMBINE_PARTS)
            x2d = _combine(y_kt, wts, h2, x1, *shared_w, gt2, g_final.reshape(1, d), seq,
                           part, COMBINE_PARTS, x2d)
    return x2d.reshape(bsz, seq, d)
```

```python
import functools

import jax
import jax.numpy as jnp
import numpy as np
from jax import lax
from jax.experimental import pallas as pl
from jax.experimental.pallas import tpu as pltpu
from jax.experimental.pallas import tpu_sc as plsc

F32 = jnp.float32
BF16 = jnp.bfloat16

D_MODEL = 1024
A_HEADS = 8
A_HEAD_DIM = 64
A_WIDTH = A_HEADS * A_HEAD_DIM
MOBA_BLOCK = 256
MOBA_TOPK = 3
R_HEADS = 8
R_QK_DIM = 64
R_V_DIM = 128
R_QK_WIDTH = R_HEADS * R_QK_DIM
R_V_WIDTH = R_HEADS * R_V_DIM
R_CHUNK = 256
N_EXPERTS = 256
TOP_K = 8
N_GROUPS = 8
GROUP_SIZE = N_EXPERTS // N_GROUPS
TOPK_GROUPS = 4
EXPERT_FF = 256
SHARED_FF = 256
ROUTED_SCALE = 2.5
NORM_EPS = 1e-6
GN_EPS = 1e-6
NEG_INF = -1e30
KNOCKED_OUT = -3e38

COL_VR, COL_GR, COL_GA, COL_GT = 0, 1024, 2048, 3072
COL_QA, COL_KA, COL_VA, COL_QR, COL_KR = 4096, 4608, 5120, 5632, 6144
IN_COLS = 6656
AUG = 128
FEAT_BIAS = A_HEAD_DIM
FEAT_POS = A_HEAD_DIM + 32
ALIBI_PIECES = 3
LOG2_E = 1.4426950408889634
LANES = 128
BF16_ROWS = 16
V_ROWS = A_HEAD_DIM + BF16_ROWS
MOBA_HEADS_PER_STEP = 8

VMEM_LIMIT = 56 * 1024 * 1024


def _cparams(sem, vmem=VMEM_LIMIT):
    return pltpu.CompilerParams(dimension_semantics=sem, vmem_limit_bytes=vmem)


def _dot(a, b):
    return jnp.dot(a, b, preferred_element_type=F32)


def _dot_nt(a, b):
    return lax.dot_general(a, b, (((1,), (1,)), ((), ())), preferred_element_type=F32)


def _sigmoid(x):
    return 1.0 / (1.0 + jnp.exp(-x))


def _silu(x):
    return x * _sigmoid(x)


def _ada_kernel(c_ref, w_ref, b_ref, o_ref):
    c = c_ref[...]
    s = _silu(c)
    s_hi = s.astype(BF16)
    s_lo = (s - s_hi.astype(F32)).astype(BF16)
    w = w_ref[...]
    w_hi = w.astype(BF16)
    w_lo = (w - w_hi.astype(F32)).astype(BF16)
    o_ref[...] = _dot(s_hi, w_hi) + _dot(s_hi, w_lo) + _dot(s_lo, w_hi) + b_ref[...]


def _ada(c, w_ada, b_ada):
    bsz, d = c.shape
    n = w_ada.shape[1]
    tn = 1024
    return pl.pallas_call(
        _ada_kernel,
        grid=(n // tn,),
        in_specs=[pl.BlockSpec((bsz, d), lambda j: (0, 0)),
                  pl.BlockSpec((d, tn), lambda j: (0, j)),
                  pl.BlockSpec((1, tn), lambda j: (0, j))],
        out_specs=pl.BlockSpec((bsz, tn), lambda j: (0, j)),
        out_shape=jax.ShapeDtypeStruct((bsz, n), F32),
        compiler_params=_cparams(("parallel",)),
        name="ada_mod",
    )(c, w_ada, b_ada.reshape(1, n))


INPROJ_COLS = 512


def _inproj_kernel(x_ref, g_ref, sc_ref, sh_ref, w_ref, o_ref):
    x = x_ref[...]
    ms = jnp.mean(x * x, axis=-1, keepdims=True)
    y = x * lax.rsqrt(ms + NORM_EPS) * g_ref[...]
    h = (y * (1.0 + sc_ref[0]) + sh_ref[0]).astype(BF16)
    for j in range(w_ref.shape[1] // INPROJ_COLS):
        cols = slice(j * INPROJ_COLS, (j + 1) * INPROJ_COLS)
        o_ref[:, cols] = _dot(h, w_ref[:, cols]).astype(BF16)


def _inproj(x2d, g, sc, sh, w_bf16, seq):
    t, d = x2d.shape
    n = w_bf16.shape[1]
    tm = min(512, seq)
    per_b = seq // tm
    return pl.pallas_call(
        _inproj_kernel,
        grid=(t // tm,),
        in_specs=[pl.BlockSpec((tm, d), lambda i: (i, 0)),
                  pl.BlockSpec((1, d), lambda i: (0, 0)),
                  pl.BlockSpec((1, 1, d), lambda i: (i // per_b, 0, 0)),
                  pl.BlockSpec((1, 1, d), lambda i: (i // per_b, 0, 0)),
                  pl.BlockSpec((d, n), lambda i: (0, 0))],
        out_specs=pl.BlockSpec((tm, n), lambda i: (i, 0)),
        out_shape=jax.ShapeDtypeStruct((t, n), BF16),
        compiler_params=_cparams(("parallel",)),
        name="norm_inproj",
    )(x2d, g, sc, sh, w_bf16)


def _moba_prepare(i, slopes_ref, q_ref, k_ref, v_ref, ko_ref, qo_ref, vo_ref, kmean_scr):
    nblk = kmean_scr.shape[0]
    width = q_ref.shape[1]
    seq_rows = pl.ds(pl.multiple_of(i * MOBA_BLOCK, MOBA_BLOCK), MOBA_BLOCK)
    q = q_ref[...]
    k = k_ref[...]
    v = v_ref[...]
    kmean_scr[pl.ds(i, 1), :] = jnp.mean(k.astype(F32), axis=0, keepdims=True)

    eye = (lax.broadcasted_iota(jnp.int32, (A_HEAD_DIM, A_HEAD_DIM), 0)
           == lax.broadcasted_iota(jnp.int32, (A_HEAD_DIM, A_HEAD_DIM), 1)).astype(BF16)
    heads = [slice(h * A_HEAD_DIM, (h + 1) * A_HEAD_DIM) for h in range(width // A_HEAD_DIM)]
    q_t = jnp.concatenate([_dot_nt(eye, q[:, hd]) for hd in heads], axis=0)
    v_t = jnp.concatenate([_dot_nt(eye, v[:, hd]) for hd in heads], axis=0)

    km = kmean_scr[...]
    km_rep = jnp.concatenate([km] * A_HEADS, axis=0)
    r_head = lax.broadcasted_iota(jnp.int32, km_rep.shape, 0) // nblk
    c_head = lax.broadcasted_iota(jnp.int32, km_rep.shape, 1) // A_HEAD_DIM
    km_bd = jnp.where(r_head == c_head, km_rep, 0.0)
    km_hi = km_bd.astype(BF16)
    km_lo = (km_bd - km_hi.astype(F32)).astype(BF16)
    q_t_b = q_t.astype(BF16)
    gate_all = _dot(km_hi, q_t_b) + _dot(km_lo, q_t_b)

    mb = q.shape[0]
    blk = lax.broadcasted_iota(jnp.int32, (nblk, mb), 0)
    lane_pos = lax.broadcasted_iota(jnp.int32, (BF16_ROWS, mb), 1).astype(F32)
    row16 = lax.broadcasted_iota(jnp.int32, (BF16_ROWS, mb), 0)
    key_pos = lax.broadcasted_iota(jnp.int32, (mb, AUG), 0).astype(F32)
    kcol = lax.broadcasted_iota(jnp.int32, (mb, AUG), 1)
    widen = (lax.broadcasted_iota(jnp.int32, (A_HEAD_DIM, AUG), 0)
             == lax.broadcasted_iota(jnp.int32, (A_HEAD_DIM, AUG), 1)).astype(BF16)

    blk_first = (i * mb).astype(F32)
    for h in range(A_HEADS):
        pieces = [slopes_ref[h * ALIBI_PIECES + c] for c in range(ALIBI_PIECES)]
        g = jnp.where(blk < i, gate_all[h * nblk:(h + 1) * nblk, :], NEG_INF)
        sel = jnp.zeros((nblk, mb), jnp.bool_)
        for r in range(MOBA_TOPK):
            m = jnp.max(g, axis=0, keepdims=True)
            idx = jnp.min(jnp.where(g == m, blk, nblk), axis=0, keepdims=True)
            hit = blk == idx
            sel = jnp.logical_or(sel, jnp.logical_and(hit, r < i))
            g = jnp.where(hit, KNOCKED_OUT, g)
        bias_t = jnp.where(sel, 0.0, NEG_INF)

        scale = A_HEAD_DIM ** -0.5 * LOG2_E
        qo_ref[0, h, 0:A_HEAD_DIM, :] = (q_t[h * A_HEAD_DIM:(h + 1) * A_HEAD_DIM, :] * scale).astype(BF16)
        qo_ref[0, h, FEAT_BIAS:FEAT_BIAS + nblk, :] = bias_t.astype(BF16)
        if nblk < 32:
            qo_ref[0, h, FEAT_BIAS + nblk:FEAT_POS, :] = jnp.zeros((32 - nblk, mb), BF16)
        piece_rows = jnp.where(row16 % ALIBI_PIECES == 0, pieces[0],
                               jnp.where(row16 % ALIBI_PIECES == 1, pieces[1], pieces[2]))
        pos_feat = jnp.where(row16 < 3, -lane_pos,
                             jnp.where(row16 < 6, piece_rows,
                                       jnp.where(row16 < 9, -blk_first, jnp.where(row16 < 12, piece_rows, 0.0))))
        qo_ref[0, h, FEAT_POS:FEAT_POS + BF16_ROWS, :] = pos_feat.astype(BF16)
        qo_ref[0, h, FEAT_POS + BF16_ROWS:AUG, :] = jnp.zeros((AUG - FEAT_POS - BF16_ROWS, mb), BF16)

        vo_ref[0, h, 0:A_HEAD_DIM, seq_rows] = v_t[h * A_HEAD_DIM:(h + 1) * A_HEAD_DIM, :].astype(BF16)
        vo_ref[0, h, A_HEAD_DIM:V_ROWS, seq_rows] = jnp.where(row16 == 0, 1.0, 0.0).astype(BF16)

        pos_col = kcol - FEAT_POS
        piece_cols = jnp.where(pos_col % ALIBI_PIECES == 0, pieces[0],
                               jnp.where(pos_col % ALIBI_PIECES == 1, pieces[1], pieces[2]))
        k_feat = jnp.where(
            kcol == FEAT_BIAS + i, 1.0,
            jnp.where(pos_col < 0, 0.0,
                      jnp.where(pos_col < 3, piece_cols,
                                jnp.where(pos_col < 6, key_pos,
                                          jnp.where(pos_col < 9, piece_cols,
                                                    jnp.where(pos_col < 12, blk_first, 0.0))))))
        k_wide = _dot(k[:, h * A_HEAD_DIM:(h + 1) * A_HEAD_DIM], widen)
        ko_ref[0, h, seq_rows, :] = (k_wide + k_feat).astype(BF16)


def _moba_attend(i, q_ref, k_ref, v_ref, o_ref, s_a, s_b, s_c, acc_scr, group, n_groups):
    mb = MOBA_BLOCK
    span = group * mb
    own = pl.multiple_of(i * mb, mb)
    key_i = lax.broadcasted_iota(jnp.int32, (mb, mb), 0)
    qry_i = lax.broadcasted_iota(jnp.int32, (mb, mb), 1)
    feat = lax.broadcasted_iota(jnp.int32, (AUG, mb), 0)
    is_bias = jnp.logical_and(feat >= FEAT_BIAS, feat < FEAT_POS)
    q_ts, carry0 = [], []
    for hh in range(MOBA_HEADS_PER_STEP):
        q_t = q_ref[0, hh]
        q_ts.append(q_t)
        q_own = jnp.where(is_bias, jnp.zeros_like(q_t), q_t)
        s = _dot(k_ref[0, hh, pl.ds(own, mb), :], q_own)
        s = jnp.where(key_i <= qry_i, s, NEG_INF)
        m0 = jnp.max(s, axis=0, keepdims=True)
        p = jnp.exp2(s - m0)
        carry0.append(m0)
        acc_scr[hh] = _dot(v_ref[0, hh, :, pl.ds(own, mb)], p.astype(BF16))

    def scores(g, dst):
        start = pl.multiple_of(jnp.minimum(g, n_groups - 1) * span, span)
        for hh in range(MOBA_HEADS_PER_STEP):
            dst[hh] = _dot(k_ref[0, hh, pl.ds(start, span), :], q_ts[hh])

    def absorb(g, src, carry):
        start = pl.multiple_of(g * span, span)
        new = []
        for hh in range(MOBA_HEADS_PER_STEP):
            m = carry[hh]
            sb = src[hh]
            m_new = jnp.maximum(m, jnp.max(sb, axis=0, keepdims=True))
            pb = jnp.exp2(sb - m_new)
            alpha = jnp.exp2(m - m_new)
            acc_scr[hh] = acc_scr[hh] * alpha + _dot(v_ref[0, hh, :, pl.ds(start, span)], pb.astype(BF16))
            new.append(m_new)
        return tuple(new)

    ring = (s_a, s_b, s_c)

    def body(trip, carry):
        g = len(ring) * trip
        for r in range(len(ring)):
            scores(g + r + 1, ring[(r + 1) % len(ring)])
            carry = absorb(g + r, ring[r], carry)
        return carry

    scores(0, s_a)
    live_groups = (i + group - 1) // group
    trips = live_groups // len(ring)
    res = lax.fori_loop(0, trips, body, tuple(carry0))
    done = len(ring) * trips
    res = lax.cond(live_groups > done, lambda c: absorb(done, s_a, c), lambda c: c, res)

    def straggler(g, c):
        scores(g, s_b)
        return absorb(g, s_b, c)

    res = lax.fori_loop(done + 1, live_groups, straggler, res)
    del res
    outs = [acc_scr[hh, 0:A_HEAD_DIM, :] / acc_scr[hh, A_HEAD_DIM:A_HEAD_DIM + 1, :]
            for hh in range(MOBA_HEADS_PER_STEP)]
    o_t = jnp.concatenate(outs, axis=0).astype(BF16)
    eye = (key_i == qry_i).astype(BF16)
    o_ref[0] = _dot_nt(eye, o_t).astype(BF16)


def _moba_kernel(slopes_ref, q_ref, k_ref, v_ref, o_ref, kmean_scr, q_scr, k_scr, v_scr, s_a, s_b, s_c, acc_scr,
                 *, group, n_groups):
    i = pl.program_id(1)

    @pl.when(i == 0)
    def _():
        kmean_scr[...] = jnp.zeros_like(kmean_scr)
        k_scr[...] = jnp.zeros_like(k_scr)
        v_scr[...] = jnp.zeros_like(v_scr)

    _moba_prepare(i, slopes_ref, q_ref, k_ref, v_ref, k_scr, q_scr, v_scr, kmean_scr)
    _moba_attend(i, q_scr, k_scr, v_scr, o_ref, s_a, s_b, s_c, acc_scr, group, n_groups)


def _moba(proj3, slopes):
    bsz, seq, _ = proj3.shape
    mb = MOBA_BLOCK
    nblk = seq // mb
    group = min(2, nblk)
    n_groups = nblk // group
    assert MOBA_HEADS_PER_STEP == A_HEADS
    assert seq % mb == 0 and nblk % group == 0 and nblk <= FEAT_POS - FEAT_BIAS, "one selection feature per key block"
    grid_spec = pltpu.PrefetchScalarGridSpec(
        num_scalar_prefetch=1,
        grid=(bsz, nblk),
        in_specs=[pl.BlockSpec((None, mb, A_WIDTH), lambda b, i, s: (b, i, COL_QA // A_WIDTH)),
                  pl.BlockSpec((None, mb, A_WIDTH), lambda b, i, s: (b, i, COL_KA // A_WIDTH)),
                  pl.BlockSpec((None, mb, A_WIDTH), lambda b, i, s: (b, i, COL_VA // A_WIDTH))],
        out_specs=pl.BlockSpec((1, mb, A_WIDTH), lambda b, i, s: (b, i, 0)),
        scratch_shapes=[pltpu.VMEM((nblk, A_WIDTH), F32),
                        pltpu.VMEM((1, A_HEADS, AUG, mb), BF16),
                        pltpu.VMEM((1, A_HEADS, seq, AUG), BF16),
                        pltpu.VMEM((1, A_HEADS, V_ROWS, seq), BF16),
                        pltpu.VMEM((A_HEADS, group * mb, mb), F32),
                        pltpu.VMEM((A_HEADS, group * mb, mb), F32),
                        pltpu.VMEM((A_HEADS, group * mb, mb), F32),
                        pltpu.VMEM((A_HEADS, V_ROWS, mb), F32)],
    )
    return pl.pallas_call(
        functools.partial(_moba_kernel, group=group, n_groups=n_groups),
        grid_spec=grid_spec,
        out_shape=jax.ShapeDtypeStruct((bsz, seq, A_WIDTH), BF16),
        compiler_params=_cparams(("parallel", "arbitrary")),
        name="moba_attention",
    )(slopes, proj3, proj3, proj3)


def _retention_kernel(cdec_ref, q_ref, k_ref, v_ref, g_ref, decay_ref, qdec_ref, kdec_ref, o_ref, state_scr):
    @pl.when(pl.program_id(0) == 0)
    def _():
        state_scr[...] = jnp.zeros_like(state_scr)

    width = q_ref.shape[2]
    eye = (lax.broadcasted_iota(jnp.int32, (width, width), 0)
           == lax.broadcasted_iota(jnp.int32, (width, width), 1)).astype(BF16)
    for b in range(q_ref.shape[0]):
        q = q_ref[b]
        k_t = _dot_nt(eye, k_ref[b])
        k_t_b = k_t.astype(BF16)
        state_b = state_scr[b].astype(BF16)
        for h in range(R_HEADS):
            rows = slice(h * R_QK_DIM, (h + 1) * R_QK_DIM)
            cols = slice(h * R_V_DIM, (h + 1) * R_V_DIM)
            q_h = q[:, rows]
            v_h = v_ref[b, :, cols]
            inner = _dot(q_h, k_t_b[rows, :]) * decay_ref[h]
            out = _dot(inner.astype(BF16), v_h) + _dot(q_h, state_b[rows, :]) * qdec_ref[h]
            k_dec = (k_t[rows, :] * kdec_ref[h]).astype(BF16)
            state_scr[b, rows, :] = cdec_ref[h] * state_scr[b, rows, :] + _dot(k_dec, v_h)
            mu = jnp.mean(out, axis=-1, keepdims=True)
            cen = out - mu
            var = jnp.mean(cen * cen, axis=-1, keepdims=True)
            y = cen * lax.rsqrt(var + GN_EPS)
            o_ref[b, :, cols] = (y * _silu(g_ref[b, :, cols].astype(F32))).astype(BF16)


def _retention_consts():
    h = np.arange(R_HEADS, dtype=np.float64)
    log_g = np.log(1.0 - np.exp2(-5.0 - h))
    n = np.arange(R_CHUNK, dtype=np.float64)
    diff = n[:, None] - n[None, :]
    scale = R_QK_DIM ** -0.5
    decay = np.where(diff >= 0, np.exp(np.maximum(diff, 0.0) * log_g[:, None, None]), 0.0) * scale
    q_decay = np.exp((n + 1.0) * log_g[:, None])[:, :, None]
    k_decay = np.exp((R_CHUNK - 1.0 - n) * log_g[:, None])[:, None, :] * scale
    chunk_decay = np.exp(R_CHUNK * log_g)
    return (jnp.asarray(decay, F32), jnp.asarray(q_decay, F32), jnp.asarray(k_decay, F32),
            jnp.asarray(chunk_decay, F32))


def _retention(proj3):
    bsz, seq, _ = proj3.shape
    c = R_CHUNK
    decay, qdec, kdec, cdec = _retention_consts()
    grid_spec = pltpu.PrefetchScalarGridSpec(
        num_scalar_prefetch=1,
        grid=(seq // c,),
        in_specs=[pl.BlockSpec((bsz, c, R_QK_WIDTH), lambda i, s: (0, i, COL_QR // R_QK_WIDTH)),
                  pl.BlockSpec((bsz, c, R_QK_WIDTH), lambda i, s: (0, i, COL_KR // R_QK_WIDTH)),
                  pl.BlockSpec((bsz, c, R_V_WIDTH), lambda i, s: (0, i, COL_VR // R_V_WIDTH)),
                  pl.BlockSpec((bsz, c, R_V_WIDTH), lambda i, s: (0, i, COL_GR // R_V_WIDTH)),
                  pl.BlockSpec((R_HEADS, c, c), lambda i, s: (0, 0, 0)),
                  pl.BlockSpec((R_HEADS, c, 1), lambda i, s: (0, 0, 0)),
                  pl.BlockSpec((R_HEADS, 1, c), lambda i, s: (0, 0, 0))],
        out_specs=pl.BlockSpec((bsz, c, R_V_WIDTH), lambda i, s: (0, i, 0)),
        scratch_shapes=[pltpu.VMEM((bsz, R_QK_WIDTH, R_V_DIM), F32)],
    )
    return pl.pallas_call(
        _retention_kernel,
        grid_spec=grid_spec,
        out_shape=jax.ShapeDtypeStruct((bsz, seq, R_V_WIDTH), BF16),
        compiler_params=_cparams(("arbitrary",)),
        name="retention",
    )(cdec, proj3, proj3, proj3, proj3, decay, qdec, kdec)


def _pack_halves(x):
    w = x.shape[1] // 2
    bits = lax.bitcast_convert_type(x.astype(BF16).astype(F32), jnp.uint32)
    return (bits[:, :w] >> 16) | (bits[:, w:] & jnp.uint32(0xFFFF0000))


def _unpack_halves(p):
    lo = lax.bitcast_convert_type(p << 16, F32)
    hi = lax.bitcast_convert_type(p & jnp.uint32(0xFFFF0000), F32)
    return lo, hi


SUBLANES = 8


ROUTER_ROWS = 512


def _mix_kernel(ya_ref, yr_ref, ga_ref, gt_ref, x_ref, wpa_ref, wpr_ref, wout_ref,
                gt1_ref, g_ref, sc_ref, sh_ref, wr_ref, rb_ref,
                x1_ref, h2_ref, h2p_ref, e_ref, w_ref, r_ref, c_ref):
    a = _dot(ya_ref[...], wpa_ref[...]) * _sigmoid(ga_ref[...].astype(F32))
    r = _dot(yr_ref[...], wpr_ref[...]) * _sigmoid(gt_ref[...].astype(F32))
    mix = _dot((a + r).astype(BF16), wout_ref[...])
    x1 = x_ref[...] + gt1_ref[0] * mix
    x1_ref[...] = x1
    ms = jnp.mean(x1 * x1, axis=-1, keepdims=True)
    y = x1 * lax.rsqrt(ms + NORM_EPS) * g_ref[...]
    h2 = y * (1.0 + sc_ref[0]) + sh_ref[0]
    h2_b = h2.astype(BF16)
    h2_ref[...] = h2_b
    h2p_ref[...] = _pack_halves(h2)
    _route(h2_b, wr_ref, rb_ref, e_ref, w_ref, r_ref, c_ref)


def _mix(ya, yr, proj, x2d, wpa, wpr, wout, gt1, g, sc, sh, wr_t, bias_col, seq):
    t, d = x2d.shape
    tm = min(ROUTER_ROWS, seq)
    per_b = seq // tm
    row = lambda i: (i, 0)
    full = lambda i: (0, 0)
    by_tile = lambda i: (0, i)
    per_batch = lambda i: (i // per_b, 0, 0)
    return pl.pallas_call(
        _mix_kernel,
        grid=(t // tm,),
        in_specs=[pl.BlockSpec((tm, A_WIDTH), row),
                  pl.BlockSpec((tm, R_V_WIDTH), row),
                  pl.BlockSpec((tm, d), lambda i: (i, COL_GA // D_MODEL)),
                  pl.BlockSpec((tm, d), lambda i: (i, COL_GT // D_MODEL)),
                  pl.BlockSpec((tm, d), row),
                  pl.BlockSpec((A_WIDTH, d), full),
                  pl.BlockSpec((R_V_WIDTH, d), full),
                  pl.BlockSpec((d, d), full),
                  pl.BlockSpec((1, 1, d), per_batch),
                  pl.BlockSpec((1, d), full),
                  pl.BlockSpec((1, 1, d), per_batch),
                  pl.BlockSpec((1, 1, d), per_batch),
                  pl.BlockSpec((N_EXPERTS, d), full),
                  pl.BlockSpec((N_EXPERTS, 1), full)],
        out_specs=[pl.BlockSpec((tm, d), row), pl.BlockSpec((tm, d), row),
                   pl.BlockSpec((tm, d // 2), row),
                   pl.BlockSpec((TOP_K, tm), by_tile), pl.BlockSpec((TOP_K, tm), by_tile),
                   pl.BlockSpec((TOP_K, tm), by_tile), pl.BlockSpec((N_EXPERTS, LANES), by_tile)],
        out_shape=[jax.ShapeDtypeStruct((t, d), F32), jax.ShapeDtypeStruct((t, d), BF16),
                   jax.ShapeDtypeStruct((t, d // 2), jnp.uint32),
                   jax.ShapeDtypeStruct((TOP_K, t), jnp.int32),
                   jax.ShapeDtypeStruct((TOP_K, t), F32),
                   jax.ShapeDtypeStruct((TOP_K, t), jnp.int32),
                   jax.ShapeDtypeStruct((N_EXPERTS, (t // tm) * LANES), F32)],
        compiler_params=_cparams(("parallel",)),
        name="merge_norm_route",
    )(ya, yr, proj, proj, x2d, wpa, wpr, wout, gt1, g, sc, sh, wr_t, bias_col)


def _route(h, wr_ref, b_ref, e_ref, w_ref, r_ref, c_ref):
    logits = _dot_nt(wr_ref[...], h)
    scores = _sigmoid(logits)
    choice = scores + b_ref[...]
    tm = logits.shape[1]
    giota = lax.broadcasted_iota(jnp.int32, (GROUP_SIZE, tm), 0)
    gs_rows = []
    for g in range(N_GROUPS):
        cg = choice[g * GROUP_SIZE:(g + 1) * GROUP_SIZE, :]
        m1 = jnp.max(cg, axis=0, keepdims=True)
        i1 = jnp.min(jnp.where(cg == m1, giota, GROUP_SIZE), axis=0, keepdims=True)
        m2 = jnp.max(jnp.where(giota == i1, KNOCKED_OUT, cg), axis=0, keepdims=True)
        gs_rows.append(m1 + m2)
    gs = jnp.concatenate(gs_rows, axis=0)
    grow = lax.broadcasted_iota(jnp.int32, (N_GROUPS, tm), 0)
    gmask = jnp.zeros((N_GROUPS, tm), jnp.bool_)
    for _ in range(TOPK_GROUPS):
        mx = jnp.max(gs, axis=0, keepdims=True)
        ix = jnp.min(jnp.where(gs == mx, grow, N_GROUPS), axis=0, keepdims=True)
        hit = grow == ix
        gmask = jnp.logical_or(gmask, hit)
        gs = jnp.where(hit, KNOCKED_OUT, gs)
    gmask_f = jnp.where(gmask, 1.0, 0.0)
    masked = jnp.concatenate(
        [jnp.where(gmask_f[g:g + 1, :] > 0.5, choice[g * GROUP_SIZE:(g + 1) * GROUP_SIZE, :], NEG_INF)
         for g in range(N_GROUPS)], axis=0)
    erow = lax.broadcasted_iota(jnp.int32, (N_EXPERTS, tm), 0)
    idx_rows, w_rows = [], []
    chosen = jnp.zeros((N_EXPERTS, tm), F32)
    for _ in range(TOP_K):
        mx = jnp.max(masked, axis=0, keepdims=True)
        ix = jnp.min(jnp.where(masked == mx, erow, N_EXPERTS), axis=0, keepdims=True)
        hit = erow == ix
        w_rows.append(jnp.sum(jnp.where(hit, scores, 0.0), axis=0, keepdims=True))
        idx_rows.append(ix)
        chosen = jnp.where(hit, 1.0, chosen)
        masked = jnp.where(hit, KNOCKED_OUT, masked)
    w = jnp.concatenate(w_rows, axis=0)
    w = w / (jnp.sum(w, axis=0, keepdims=True) + 1e-20) * ROUTED_SCALE
    e_ref[...] = jnp.concatenate(idx_rows, axis=0)
    w_ref[...] = w
    chosen_b = chosen.astype(BF16)
    earlier = (lax.broadcasted_iota(jnp.int32, (tm, tm), 0)
               < lax.broadcasted_iota(jnp.int32, (tm, tm), 1)).astype(BF16)
    before = _dot(chosen_b, earlier)
    ranks = [jnp.sum(jnp.where(erow == ix, before, 0.0), axis=0, keepdims=True) for ix in idx_rows]
    r_ref[...] = jnp.concatenate(ranks, axis=0).astype(jnp.int32)
    c_ref[...] = _dot(chosen_b, jnp.ones((tm, LANES), BF16))


def _pos_kernel(e_ref, r_ref, base_ref, p_ref):
    tm = e_ref.shape[1]
    erow = lax.broadcasted_iota(jnp.int32, (N_EXPERTS, tm), 0)
    base = base_ref[0]
    rows = [jnp.sum(jnp.where(erow == e_ref[k:k + 1, :], base, 0.0), axis=0, keepdims=True)
            for k in range(TOP_K)]
    p_ref[0] = jnp.concatenate(rows, axis=0).astype(jnp.int32) + r_ref[...]


MOVE_ROWS = 512


def _positions(eidx_t, rank_t, tile_base):
    t = eidx_t.shape[1]
    tm = min(MOVE_ROWS, t)
    per_router_tile = ROUTER_ROWS // tm
    return pl.pallas_call(
        _pos_kernel,
        grid=(t // tm,),
        in_specs=[pl.BlockSpec((TOP_K, tm), lambda i: (0, i)),
                  pl.BlockSpec((TOP_K, tm), lambda i: (0, i)),
                  pl.BlockSpec((1, N_EXPERTS, 1), lambda i: (i // per_router_tile, 0, 0))],
        out_specs=pl.BlockSpec((1, TOP_K, tm), lambda i: (i, 0, 0)),
        out_shape=jax.ShapeDtypeStruct((t // tm, TOP_K, tm), jnp.int32),
        compiler_params=_cparams(("parallel",)),
        name="slot_positions",
    )(eidx_t, rank_t, tile_base)


SLOT_ROWS = 512
PAD_CHUNKS = (256, 128, 64, 32, 16, 8)


def _zero_pads_kernel(pad_start_ref, pad_len_ref, xs_in, xs_hbm, zero_buf, pad_sem):
    del xs_in
    zero_buf[...] = jnp.zeros_like(zero_buf)

    def pad_copies(e, wait):
        start = pad_start_ref[e]
        n = pad_len_ref[e]
        head = jnp.minimum((-start) & (SUBLANES - 1), n)

        def fill(first, size, pred):
            @pl.when(pred)
            def _():
                cp = pltpu.make_async_copy(zero_buf.at[pl.ds(0, size), :], xs_hbm.at[pl.ds(first, size), :], pad_sem)
                if wait:
                    cp.wait()
                else:
                    cp.start()

        for j in range(SUBLANES - 1):
            fill(start + j, 1, j < head)
        ptr = start + head
        rest = n - head
        for chunk in PAD_CHUNKS:
            fill(pl.multiple_of(ptr, SUBLANES), chunk, (rest & chunk) != 0)
            ptr = ptr + (rest & chunk)

    def issue(e, carry):
        pad_copies(e, False)
        return carry

    def drain(e, carry):
        pad_copies(e, True)
        return carry

    lax.fori_loop(0, N_EXPERTS, issue, 0)
    lax.fori_loop(0, N_EXPERTS, drain, 0)


def _zero_pads(pad_start, pad_len, xs):
    grid_spec = pltpu.PrefetchScalarGridSpec(
        num_scalar_prefetch=2,
        grid=(1,),
        in_specs=[pl.BlockSpec(memory_space=pl.ANY)],
        out_specs=pl.BlockSpec(memory_space=pl.ANY),
        scratch_shapes=[pltpu.VMEM((PAD_CHUNKS[0], xs.shape[1]), xs.dtype),
                        pltpu.SemaphoreType.DMA],
    )
    return pl.pallas_call(
        _zero_pads_kernel,
        grid_spec=grid_spec,
        out_shape=jax.ShapeDtypeStruct(xs.shape, xs.dtype),
        input_output_aliases={2: 0},
        compiler_params=_cparams(("arbitrary",)),
        name="zero_pad_slots",
    )(pad_start, pad_len, xs)


SC_DISPATCH_TOKENS = 64


def _sc_dispatch(pos_blocks, h_rows, n_rows):
    info = plsc.get_sparse_core_info()
    n_cores = info.num_cores
    n_workers = n_cores * info.num_subcores
    t = h_rows.shape[0]
    chunk = SC_DISPATCH_TOKENS
    steps = t // (n_workers * chunk)
    assert steps * n_workers * chunk == t
    mesh = plsc.VectorSubcoreMesh(core_axis_name="c", subcore_axis_name="s")

    @functools.partial(
        pl.kernel, mesh=mesh,
        out_type=jax.ShapeDtypeStruct((n_rows,) + h_rows.shape[1:], h_rows.dtype),
        scratch_types=[pltpu.VMEM((TOP_K, chunk), jnp.int32),
                       pltpu.VMEM((chunk,) + h_rows.shape[1:], h_rows.dtype),
                       pltpu.SemaphoreType.DMA],
        name="sc_dispatch_rows",
    )
    def scatter_rows(pos_hbm, h_hbm, out_hbm, idx_v, rows_v, sem):
        wid = lax.axis_index("s") * n_cores + lax.axis_index("c")

        @pl.loop(0, steps)
        def _(step):
            blk = wid * steps + step
            pltpu.sync_copy(pos_hbm.at[blk], idx_v)
            pltpu.sync_copy(h_hbm.at[pl.ds(blk * chunk, chunk)], rows_v)
            scatters = [pltpu.make_async_copy(rows_v, out_hbm.at[idx_v.at[k]], sem) for k in range(TOP_K)]
            for cp in scatters:
                cp.start()
            for cp in scatters:
                cp.wait()

    return scatter_rows(pos_blocks, h_rows)


WEIGHT_SLOTS = 4


def _experts_kernel(blk_e_ref, nblk_ref, ord_ref, eid_ref, nord_ref, x_ref, w1_hbm, w3_hbm, w2_hbm, y_ref,
                    w1f, w3f, w2f, wsem):
    s = pl.program_id(0)

    def fetch(j):
        slot = j % WEIGHT_SLOTS
        e = eid_ref[j]
        return [pltpu.make_async_copy(src.at[e], dst.at[slot], wsem.at[slot])
                for src, dst in ((w1_hbm, w1f), (w3_hbm, w3f), (w2_hbm, w2f))]

    @pl.when(s < nblk_ref[0])
    def _():
        j = ord_ref[s]

        ahead = WEIGHT_SLOTS - 1

        @pl.when(s == 0)
        def _():
            for first in range(ahead):
                @pl.when(first < nord_ref[0])
                def _(first=first):
                    for cp in fetch(first):
                        cp.start()

        @pl.when(jnp.logical_or(s == 0, blk_e_ref[s] != blk_e_ref[jnp.maximum(s - 1, 0)]))
        def _():
            for cp in fetch(j):
                cp.wait()

            @pl.when(j + ahead < nord_ref[0])
            def _():
                for cp in fetch(j + ahead):
                    cp.start()

        slot = j % WEIGHT_SLOTS
        half = x_ref.shape[1]
        lo, hi = _unpack_halves(x_ref[...])
        lo = lo.astype(BF16)
        hi = hi.astype(BF16)
        h1 = (_dot(lo, w1f[slot, :half, :].astype(BF16)) + _dot(hi, w1f[slot, half:, :].astype(BF16)))
        h3 = (_dot(lo, w3f[slot, :half, :].astype(BF16)) + _dot(hi, w3f[slot, half:, :].astype(BF16)))
        mid = (_silu(h1) * h3).astype(BF16)
        y_ref[...] = _pack_halves(_dot(mid, w2f[slot].astype(BF16)))


def _sc_gather(pos_blocks, ys, t):
    info = plsc.get_sparse_core_info()
    n_cores = info.num_cores
    n_workers = n_cores * info.num_subcores
    chunk = SC_DISPATCH_TOKENS
    steps = t // (n_workers * chunk)
    assert steps * n_workers * chunk == t
    mesh = plsc.VectorSubcoreMesh(core_axis_name="c", subcore_axis_name="s")

    @functools.partial(
        pl.kernel, mesh=mesh,
        out_type=jax.ShapeDtypeStruct((TOP_K * t,) + ys.shape[1:], ys.dtype),
        scratch_types=[pltpu.VMEM((TOP_K, chunk), jnp.int32),
                       pltpu.VMEM((chunk,) + ys.shape[1:], ys.dtype),
                       pltpu.VMEM((chunk,) + ys.shape[1:], ys.dtype),
                       pltpu.SemaphoreType.DMA((2,))],
        name="sc_gather_rows",
    )
    def gather_rows(pos_hbm, ys_hbm, out_hbm, idx_v, rows_a, rows_b, sems):
        wid = lax.axis_index("s") * n_cores + lax.axis_index("c")
        bufs = (rows_a, rows_b)

        @pl.loop(0, steps)
        def _(step):
            blk = wid * steps + step
            pltpu.sync_copy(pos_hbm.at[blk], idx_v)
            gathers = [pltpu.make_async_copy(ys_hbm.at[idx_v.at[k]], bufs[k % 2], sems.at[k % 2])
                       for k in range(TOP_K)]
            gathers[0].start()
            for k in range(TOP_K):
                gathers[k].wait()
                if k + 1 < TOP_K:
                    gathers[k + 1].start()
                pltpu.sync_copy(bufs[k % 2], out_hbm.at[pl.ds(k * t + blk * chunk, chunk)])

    return gather_rows(pos_blocks, ys)


def _experts(blk_e, nblk_used, blk_ord, eid_of_ord, n_ord, xs, w1, w3, w2):
    n_rows, half = xs.shape
    d = D_MODEL
    blk = lambda s, be, nb, bo, eo, no: (jnp.minimum(s, nb[0] - 1), 0)
    grid_spec = pltpu.PrefetchScalarGridSpec(
        num_scalar_prefetch=5,
        grid=(n_rows // SLOT_ROWS,),
        in_specs=[pl.BlockSpec((SLOT_ROWS, half), blk),
                  pl.BlockSpec(memory_space=pl.ANY),
                  pl.BlockSpec(memory_space=pl.ANY),
                  pl.BlockSpec(memory_space=pl.ANY)],
        out_specs=pl.BlockSpec((SLOT_ROWS, half), blk),
        scratch_shapes=[pltpu.VMEM((WEIGHT_SLOTS, d, EXPERT_FF), F32),
                        pltpu.VMEM((WEIGHT_SLOTS, d, EXPERT_FF), F32),
                        pltpu.VMEM((WEIGHT_SLOTS, EXPERT_FF, d), F32),
                        pltpu.SemaphoreType.DMA((WEIGHT_SLOTS,))],
    )
    return pl.pallas_call(
        _experts_kernel,
        grid_spec=grid_spec,
        out_shape=jax.ShapeDtypeStruct((n_rows, half), jnp.uint32),
        compiler_params=_cparams(("arbitrary",)),
        name="routed_experts",
    )(blk_e, nblk_used, blk_ord, eid_of_ord, n_ord, xs, w1, w3, w2)


def _shared_kernel(h_ref, ws1_ref, ws3_ref, ws2_ref, o_ref):
    h = h_ref[...]
    mid = (_silu(_dot(h, ws1_ref[...])) * _dot(h, ws3_ref[...])).astype(BF16)
    o_ref[...] = _dot(mid, ws2_ref[...]).astype(BF16)


def _shared_expert(h2, ws1, ws3, ws2):
    t, d = h2.shape
    tm = ROUTER_ROWS
    full = lambda i: (0, 0)
    return pl.pallas_call(
        _shared_kernel,
        grid=(t // tm,),
        in_specs=[pl.BlockSpec((tm, d), lambda i: (i, 0)),
                  pl.BlockSpec((d, SHARED_FF), full),
                  pl.BlockSpec((d, SHARED_FF), full),
                  pl.BlockSpec((SHARED_FF, d), full)],
        out_specs=pl.BlockSpec((tm, d), lambda i: (i, 0)),
        out_shape=jax.ShapeDtypeStruct((t, d), BF16),
        compiler_params=_cparams(("parallel",)),
        name="shared_expert",
    )(h2, ws1, ws3, ws2)


def _combine_kernel(*refs):
    y_refs = refs[:TOP_K]
    w_ref, s_ref, x1_ref, gt2_ref, g_ref = refs[TOP_K:TOP_K + 5]
    o_ref = refs[-1]
    tm, d = x1_ref.shape
    half = d // 2
    shared = s_ref[...].astype(F32)
    w = w_ref[...]
    acc_lo = jnp.zeros((tm, half), F32)
    acc_hi = jnp.zeros((tm, half), F32)
    for k in range(TOP_K):
        lo, hi = _unpack_halves(y_refs[k][...])
        acc_lo = acc_lo + lo * w[:, k:k + 1]
        acc_hi = acc_hi + hi * w[:, k:k + 1]
    routed = jnp.concatenate([acc_lo, acc_hi], axis=1)
    x2 = x1_ref[...] + gt2_ref[0] * (routed + shared)
    ms = jnp.mean(x2 * x2, axis=-1, keepdims=True)
    o_ref[...] = x2 * lax.rsqrt(ms + NORM_EPS) * g_ref[...]


COMBINE_ROWS = 256
COMBINE_PARTS = 4


def _combine(y_kt, wts, shared, x1, gt2, g_final, seq, part, n_parts, prev_out):
    t, d = x1.shape
    tm = min(COMBINE_ROWS, seq)
    per_b = seq // tm
    tiles = t // tm // n_parts
    first = part * tiles
    row = lambda i: (first + i, 0)
    full = lambda i: (0, 0)
    y_specs = [pl.BlockSpec((tm, d // 2), functools.partial(lambda i, k: (k * tiles + i, 0), k=k))
               for k in range(TOP_K)]
    in_specs = y_specs + [
        pl.BlockSpec((tm, TOP_K), row),
        pl.BlockSpec((tm, d), row),
        pl.BlockSpec((tm, d), row),
        pl.BlockSpec((1, 1, d), lambda i: ((first + i) // per_b, 0, 0)),
        pl.BlockSpec((1, d), full)]
    args = [y_kt] * TOP_K + [wts, shared, x1, gt2, g_final]
    aliases = {}
    if prev_out is not None:
        in_specs.append(pl.BlockSpec(memory_space=pl.ANY))
        aliases = {len(args): 0}
        args.append(prev_out)
    return pl.pallas_call(
        _combine_kernel,
        grid=(tiles,),
        in_specs=in_specs,
        out_specs=pl.BlockSpec((tm, d), row),
        out_shape=jax.ShapeDtypeStruct((t, d), F32),
        input_output_aliases=aliases,
        compiler_params=_cparams(("parallel",)),
        name="combine_shared_final",
    )(*args)


def _slot_tables(cnt, t):
    ntiles = cnt.shape[1] // LANES
    cnt_tile = cnt.reshape(N_EXPERTS, ntiles, LANES)[:, :, 0].astype(jnp.int32)
    counts = jnp.sum(cnt_tile, axis=1)
    padded = (counts + SLOT_ROWS - 1) // SLOT_ROWS * SLOT_ROWS
    pstart = jnp.cumsum(padded) - padded
    tile_base = pstart[:, None] + jnp.cumsum(cnt_tile, axis=1) - cnt_tile
    n_blk = -(-(t * TOP_K) // SLOT_ROWS) + N_EXPERTS
    blk_end = jnp.cumsum(padded // SLOT_ROWS)
    blk_e = jnp.sum((blk_end[None, :] <= jnp.arange(n_blk)[:, None]).astype(jnp.int32), axis=1)
    blk_e = jnp.minimum(blk_e, N_EXPERTS - 1)
    owns = (padded > 0).astype(jnp.int32)
    ord_of_e = jnp.cumsum(owns) - owns
    ids = jnp.arange(N_EXPERTS, dtype=jnp.int32)
    eid_of_ord = jnp.sum(jnp.where((ord_of_e[None, :] == ids[:, None]) & (owns[None, :] > 0), ids[None, :], 0), axis=1)
    blk_ord = jnp.sum(jnp.where(blk_e[:, None] == ids[None, :], ord_of_e[None, :], 0), axis=1)
    experts_tables = (blk_e, blk_end[-1:].astype(jnp.int32), blk_ord.astype(jnp.int32),
                      eid_of_ord.astype(jnp.int32), jnp.sum(owns).reshape(1).astype(jnp.int32))
    return (experts_tables, pstart + counts, padded - counts,
            tile_base.T.astype(F32).reshape(ntiles, N_EXPERTS, 1), n_blk * SLOT_ROWS)


def _permute_in_cols(w_in):
    qa, ka, va, qr, kr, vr, gr, ga, gt = jnp.split(
        w_in, np.cumsum((A_WIDTH, A_WIDTH, A_WIDTH, R_QK_WIDTH, R_QK_WIDTH, R_V_WIDTH, R_V_WIDTH,
                         D_MODEL))[:].tolist(), axis=1)
    return jnp.concatenate([vr, gr, ga, gt, qa, ka, va, qr, kr], axis=1)


def kernel(x, c, w_ada, b_ada, g_mix, w_in, w_pa, w_pr, w_out, g_ffn, w_router, router_bias,
           w1, w3, w2, ws1, ws3, ws2, g_final):
    bsz, seq, d = x.shape
    t = bsz * seq
    depth = w_ada.shape[0]
    assert depth == 1, "the final norm is fused into the single layer's last kernel"
    rest = jnp.exp2(-8.0 / A_HEADS * jnp.arange(1, A_HEADS + 1, dtype=F32)) * LOG2_E
    pieces = []
    for _ in range(ALIBI_PIECES):
        pieces.append(rest.astype(BF16).astype(F32))
        rest = rest - pieces[-1]
    slopes = jnp.stack(pieces, axis=1).reshape(-1)
    x2d = x.reshape(t, d)
    for l in range(depth):
        mod = _ada(c, w_ada[l], b_ada[l])
        sh1, sc1, gt1, sh2, sc2, gt2 = [m.reshape(bsz, 1, d) for m in jnp.split(mod, 6, axis=-1)]
        w_in_p = _permute_in_cols(w_in[l]).astype(BF16)
        proj = _inproj(x2d, g_mix[l].reshape(1, d), sc1, sh1, w_in_p, seq)
        proj3 = proj.reshape(bsz, seq, IN_COLS)
        ya = _moba(proj3, slopes).reshape(t, A_WIDTH)
        yr = _retention(proj3).reshape(t, R_V_WIDTH)
        x1, h2, h2p, eidx_t, wts_t, rank_t, cnt = _mix(
            ya, yr, proj, x2d, w_pa[l].astype(BF16), w_pr[l].astype(BF16), w_out[l].astype(BF16), gt1,
            g_ffn[l].reshape(1, d), sc2, sh2, w_router[l].T.astype(BF16), router_bias[l].reshape(N_EXPERTS, 1), seq)
        experts_tables, pad_start, pad_len, tile_base, n_rows = _slot_tables(cnt, t)
        pos3 = _positions(eidx_t, rank_t, tile_base)
        pos_blocks = jnp.transpose(
            pos3.reshape(pos3.shape[0], TOP_K, -1, SC_DISPATCH_TOKENS), (0, 2, 1, 3)
        ).reshape(t // SC_DISPATCH_TOKENS, TOP_K, SC_DISPATCH_TOKENS)
        xs = _sc_dispatch(pos_blocks, h2p, n_rows)
        shared = _shared_expert(h2, ws1[l].astype(BF16), ws3[l].astype(BF16), ws2[l].astype(BF16))
        xs = _zero_pads(pad_start, pad_len, xs)
        ys = _experts(*experts_tables, xs, w1[l], w3[l], w2[l])
        wts = wts_t.T
        blocks_per_part = pos_blocks.shape[0] // COMBINE_PARTS
        x2d = None
        for part in range(COMBINE_PARTS):
            y_kt = _sc_gather(pos_blocks[part * blocks_per_part:(part + 1) * blocks_per_part], ys,
                              t // COMBINE_PARTS)
            x2d = _combine(y_kt, wts, shared, x1, gt2, g_final.reshape(1, d), seq,
                           part, COMBINE_PARTS, x2d)
    return x2d.reshape(bsz, seq, d)
```
